```python
import jax, jax.numpy as jnp
from jax import lax
import numpy as np


D_MODEL = 1024
BATCH = 4
SEQ = 4096
DEPTH = 2

ATT_GROUPS = ((128, 1), (512, 4), (2048, 16))
ATT_HEADS_PER_GROUP = 4
ATT_HEAD_DIM = 64
ATT_HEADS = len(ATT_GROUPS) * ATT_HEADS_PER_GROUP
ATT_WIDTH = ATT_HEADS * ATT_HEAD_DIM
ATT_OUT_WIDTH = ATT_HEADS_PER_GROUP * ATT_HEAD_DIM
ROPE_THETA = 10000.0
SG_CHUNK = 128
SG_GROUPS = 6
SG_WIDTH = 768
SG_GROUP_DIM = SG_WIDTH // SG_GROUPS
ML_HEADS = 4
ML_HEAD_DIM = 192
ML_WIDTH = ML_HEADS * ML_HEAD_DIM
ML_CHUNK = 128
ML_CONV = 4
N_BRANCH = 3
IN_SIZES = (ATT_WIDTH, ATT_WIDTH, ATT_WIDTH, SG_WIDTH, SG_WIDTH, 2 * ML_WIDTH, ML_WIDTH, ML_WIDTH, ML_HEADS, ML_HEADS, N_BRANCH * D_MODEL)
N_IN = 9992
F_GATE_OFFSET = 6148
D_FF = 2816
N_EXPERTS = 8
TOP_K = 2
D_FF_EXPERT = 2816
N_DENSE = (DEPTH + 1) // 2
N_MOE = DEPTH // 2
LN_EPS = 1e-5

kernel_name = 'hybrid_dilated_gmlp_mlstm_moe_deepnorm'


def layer_norm(x, g, b):
    xf = x.astype(jnp.float32)
    mu = xf.mean(-1, keepdims=True)
    var = jnp.square(xf - mu).mean(-1, keepdims=True)
    return ((xf - mu) * lax.rsqrt(var + LN_EPS) * g + b).astype(x.dtype)


def rotary(x, positions):
    dh = x.shape[-1]
    half = dh // 2
    freqs = ROPE_THETA ** (-jnp.arange(half, dtype=jnp.float32) * (2.0 / dh))
    ang = positions.astype(jnp.float32)[..., None] * freqs
    cos = jnp.cos(ang)[:, :, None, :]
    sin = jnp.sin(ang)[:, :, None, :]
    xf = x.astype(jnp.float32)
    x1, x2 = xf[..., :half], xf[..., half:]
    return jnp.concatenate([x1 * cos - x2 * sin, x2 * cos + x1 * sin], axis=-1).astype(x.dtype)


def dilated_window_group(q, k, v, window, dilation):
    B, S, H, Dh = q.shape
    steps = window // dilation
    L = S // dilation
    nb = -(-L // steps)
    Lp = nb * steps

    def to_blocks(t):
        t = t.reshape(B, L, dilation, H, Dh).transpose(0, 2, 3, 1, 4)
        t = jnp.pad(t, ((0, 0), (0, 0), (0, 0), (0, Lp - L), (0, 0)))
        return t.reshape(B, dilation, H, nb, steps, Dh)

    def frame(t):
        prev = jnp.pad(t, ((0, 0), (0, 0), (0, 0), (1, 0), (0, 0), (0, 0)))[:, :, :, :nb]
        return jnp.concatenate([prev, t], axis=4)

    qb = to_blocks(q)
    kw = frame(to_blocks(k))
    vw = frame(to_blocks(v))
    s = jnp.einsum('brhnqd,brhnkd->brhnqk', qb, kw).astype(jnp.float32) * (Dh ** -0.5)
    qi = jnp.arange(steps)[:, None] + steps
    ki = jnp.arange(2 * steps)[None, :]
    dist = qi - ki
    k_abs = jnp.arange(nb)[:, None, None] * steps - steps + ki[None]
    valid = (dist >= 0) & (dist <= steps) & (k_abs >= 0)
    s = jnp.where(valid, s, -jnp.inf)
    m = s.max(-1, keepdims=True)
    p = jnp.exp(s - m)
    den = p.sum(-1, keepdims=True)
    o = jnp.einsum('brhnqk,brhnkd->brhnqd', (p / den).astype(v.dtype), vw)
    lse = (m + jnp.log(den))[..., 0]
    o = o.reshape(B, dilation, H, Lp, Dh)[:, :, :, :L].transpose(0, 3, 1, 2, 4).reshape(B, S, H, Dh)
    lse = lse.reshape(B, dilation, H, Lp)[:, :, :, :L].transpose(0, 3, 1, 2).reshape(B, S, H)
    return o, lse


def spatial_gating(u, v, ln_g, ln_b, w_s, b_s):
    B, S, _ = v.shape
    v = layer_norm(v, ln_g, ln_b)
    vc = v.reshape(B, S // SG_CHUNK, SG_CHUNK, SG_GROUPS, SG_GROUP_DIM)
    w = jnp.tril(w_s)
    mixed = jnp.einsum('gts,bcsgd->bctgd', w, vc) + b_s.T[:, :, None]
    return u * mixed.reshape(B, S, SG_WIDTH)


def causal_depthwise_conv(x, w, b):
    K = w.shape[0]
    S = x.shape[1]
    xp = jnp.pad(x, ((0, 0), (K - 1, 0), (0, 0)))
    out = xp[:, 0:S] * w[0]
    for j in range(1, K):
        out = out + xp[:, j:j + S] * w[j]
    return out + b


def mlstm_chunkwise(q, k, v, i_pre, f_pre):
    B, S, H, D = q.shape
    nc = S // ML_CHUNK
    L = ML_CHUNK

    def chunks(t):
        return t.astype(jnp.float32).reshape(B, nc, L, H, D).transpose(1, 0, 3, 2, 4)

    def gchunks(t):
        return t.astype(jnp.float32).reshape(B, nc, L, H).transpose(1, 0, 3, 2)

    qc = chunks(q)
    kc = chunks(k) * (D ** -0.5)
    vc = chunks(v)
    ic = gchunks(i_pre)
    lfc = jax.nn.log_sigmoid(gchunks(f_pre))
    causal = jnp.tril(jnp.ones((L, L), dtype=bool))

    def step(carry, xs):
        C, n, m_prev = carry
        qx, kx, vx, ix, lfx = xs
        b = jnp.cumsum(lfx, axis=-1)
        log_d = jnp.where(causal, b[..., :, None] - b[..., None, :] + ix[..., None, :], -jnp.inf)
        m_inter = b + m_prev[..., None]
        m = jnp.maximum(m_inter, log_d.max(-1))
        sc = jnp.einsum('bhtd,bhsd->bhts', qx, kx) * jnp.exp(log_d - m[..., None])
        inter = jnp.exp(m_inter - m)
        num = jnp.einsum('bhts,bhsd->bhtd', sc, vx) + inter[..., None] * jnp.einsum('bhvk,bhtk->bhtv', C, qx)
        den = sc.sum(-1) + inter * jnp.einsum('bhk,bhtk->bht', n, qx)
        h = num / jnp.maximum(jnp.abs(den), jnp.exp(-m))[..., None]
        m_new = m[..., -1]
        w = jnp.exp(b[..., -1:] - b + ix - m_new[..., None])
        decay = jnp.exp(b[..., -1] + m_prev - m_new)
        C = decay[..., None, None] * C + jnp.einsum('bhs,bhsv,bhsk->bhvk', w, vx, kx)
        n = decay[..., None] * n + jnp.einsum('bhs,bhsk->bhk', w, kx)
        return (C, n, m_new), h

    init = (jnp.zeros((B, H, D, D), jnp.float32), jnp.zeros((B, H, D), jnp.float32), jnp.zeros((B, H), jnp.float32))
    _, h = lax.scan(step, init, (qc, kc, vc, ic, lfc))
    return h.transpose(1, 0, 3, 2, 4).reshape(B, S, H * D)


def token_mixer(x, positions, w_in, b_in, conv_w, conv_b, sg_ln_g, sg_ln_b, sg_w, sg_b, w_br_a, w_br_b, w_br_c, w_out):
    B, S, _ = x.shape
    splits = [int(c) for c in np.cumsum(IN_SIZES)[:-1]]
    h = x @ w_in + b_in
    a_q, a_k, a_v, b_u, b_v, c_qk, c_v, c_o, c_i, c_f, g_pre = jnp.split(h, splits, axis=-1)

    q = rotary(a_q.reshape(B, S, ATT_HEADS, ATT_HEAD_DIM), positions)
    k = rotary(a_k.reshape(B, S, ATT_HEADS, ATT_HEAD_DIM), positions)
    v = a_v.reshape(B, S, ATT_HEADS, ATT_HEAD_DIM)
    outs, lses = [], []
    for gi, (win, dil) in enumerate(ATT_GROUPS):
        sl = slice(gi * ATT_HEADS_PER_GROUP, (gi + 1) * ATT_HEADS_PER_GROUP)
        o, l = dilated_window_group(q[:, :, sl], k[:, :, sl], v[:, :, sl], win, dil)
        outs.append(o)
        lses.append(l)
    wts = jax.nn.softmax(jnp.stack(lses), axis=0)
    y_a = jnp.einsum('gbsh,gbshd->bshd', wts.astype(x.dtype), jnp.stack(outs)).reshape(B, S, ATT_OUT_WIDTH)

    y_b = spatial_gating(jax.nn.gelu(b_u), jax.nn.gelu(b_v), sg_ln_g, sg_ln_b, sg_w, sg_b)

    qk = jax.nn.silu(causal_depthwise_conv(c_qk, conv_w, conv_b))
    c_q, c_k = jnp.split(qk, 2, axis=-1)
    heads = lambda t: t.reshape(B, S, ML_HEADS, ML_HEAD_DIM)
    h_c = mlstm_chunkwise(heads(c_q), heads(c_k), heads(c_v), c_i, c_f)
    y_c = jax.nn.sigmoid(c_o) * h_c.astype(x.dtype)

    gates = jax.nn.sigmoid(g_pre.reshape(B, S, N_BRANCH, D_MODEL))
    z = gates[:, :, 0] * (y_a @ w_br_a) + gates[:, :, 1] * (y_b @ w_br_b) + gates[:, :, 2] * (y_c @ w_br_c)
    return z @ w_out


def swiglu(x, w1, w3, w2):
    return (jax.nn.silu(x @ w1) * (x @ w3)) @ w2


def moe_swiglu(x, router_w, router_b, w1, w3, w2):
    B, S, D = x.shape
    xt = x.reshape(-1, D)
    logits = (xt @ router_w).astype(jnp.float32) + router_b
    top_v, top_i = lax.top_k(logits, TOP_K)
    top_w = jax.nn.softmax(top_v, axis=-1)
    gates = jnp.einsum('nk,nke->ne', top_w, jax.nn.one_hot(top_i, N_EXPERTS, dtype=jnp.float32)).astype(xt.dtype)
    out = jnp.zeros_like(xt)
    for e in range(N_EXPERTS):
        out = out + gates[:, e:e + 1] * swiglu(xt, w1[e], w3[e], w2[e])
    return out.reshape(B, S, D)


def setup_inputs(seed: int = 0) -> dict:
    key = jax.random.key(seed)
    ks = jax.random.split(key, 26)
    nrm = lambda k, shape, scale: jax.random.normal(k, shape, jnp.float32) * scale
    beta = (8.0 * DEPTH) ** -0.25
    x = nrm(ks[0], (BATCH, SEQ, D_MODEL), 1.0)
    start = jax.random.randint(ks[1], (BATCH, 1), 0, 1024, dtype=jnp.int32)
    positions = start + jnp.arange(SEQ, dtype=jnp.int32)[None, :]
    w_in = nrm(ks[2], (DEPTH, D_MODEL, N_IN), D_MODEL ** -0.5)
    b_in = nrm(ks[3], (DEPTH, N_IN), 0.02)
    b_in = b_in.at[:, F_GATE_OFFSET:F_GATE_OFFSET + ML_HEADS].add(jnp.linspace(3.0, 6.0, ML_HEADS, dtype=jnp.float32))
    conv_w = nrm(ks[4], (DEPTH, ML_CONV, 2 * ML_WIDTH), ML_CONV ** -0.5)
    conv_b = nrm(ks[5], (DEPTH, 2 * ML_WIDTH), 0.02)
    sg_ln_g = 1.0 + nrm(ks[6], (DEPTH, SG_WIDTH), 0.02)
    sg_ln_b = nrm(ks[7], (DEPTH, SG_WIDTH), 0.02)
    sg_w = nrm(ks[8], (DEPTH, SG_GROUPS, SG_CHUNK, SG_CHUNK), SG_CHUNK ** -0.5)
    sg_b = 1.0 + nrm(ks[9], (DEPTH, SG_GROUPS, SG_CHUNK), 0.02)
    w_br_a = nrm(ks[10], (DEPTH, ATT_OUT_WIDTH, D_MODEL), ATT_OUT_WIDTH ** -0.5)
    w_br_b = nrm(ks[11], (DEPTH, SG_WIDTH, D_MODEL), SG_WIDTH ** -0.5)
    w_br_c = nrm(ks[12], (DEPTH, ML_WIDTH, D_MODEL), ML_WIDTH ** -0.5)
    w_out = nrm(ks[13], (DEPTH, D_MODEL, D_MODEL), D_MODEL ** -0.5 * beta)
    ln_g = 1.0 + nrm(ks[14], (DEPTH, 2, D_MODEL), 0.02)
    ln_b = nrm(ks[15], (DEPTH, 2, D_MODEL), 0.02)
    ffn_w1 = nrm(ks[16], (N_DENSE, D_MODEL, D_FF), D_MODEL ** -0.5)
    ffn_w3 = nrm(ks[17], (N_DENSE, D_MODEL, D_FF), D_MODEL ** -0.5)
    ffn_w2 = nrm(ks[18], (N_DENSE, D_FF, D_MODEL), D_FF ** -0.5 * beta)
    router_w = nrm(ks[19], (N_MOE, D_MODEL, N_EXPERTS), D_MODEL ** -0.5)
    router_b = nrm(ks[20], (N_MOE, N_EXPERTS), 0.01)
    moe_w1 = nrm(ks[21], (N_MOE, N_EXPERTS, D_MODEL, D_FF_EXPERT), D_MODEL ** -0.5)
    moe_w3 = nrm(ks[22], (N_MOE, N_EXPERTS, D_MODEL, D_FF_EXPERT), D_MODEL ** -0.5)
    moe_w2 = nrm(ks[23], (N_MOE, N_EXPERTS, D_FF_EXPERT, D_MODEL), D_FF_EXPERT ** -0.5 * beta)
    return {'x': x, 'positions': positions, 'w_in': w_in, 'b_in': b_in, 'conv_w': conv_w, 'conv_b': conv_b,
            'sg_ln_g': sg_ln_g, 'sg_ln_b': sg_ln_b, 'sg_w': sg_w, 'sg_b': sg_b,
            'w_br_a': w_br_a, 'w_br_b': w_br_b, 'w_br_c': w_br_c, 'w_out': w_out,
            'ln_g': ln_g, 'ln_b': ln_b, 'ffn_w1': ffn_w1, 'ffn_w3': ffn_w3, 'ffn_w2': ffn_w2,
            'router_w': router_w, 'router_b': router_b, 'moe_w1': moe_w1, 'moe_w3': moe_w3, 'moe_w2': moe_w2}


def reference(x, positions, w_in, b_in, conv_w, conv_b, sg_ln_g, sg_ln_b, sg_w, sg_b,
              w_br_a, w_br_b, w_br_c, w_out, ln_g, ln_b, ffn_w1, ffn_w3, ffn_w2,
              router_w, router_b, moe_w1, moe_w3, moe_w2):
    alpha = (2.0 * DEPTH) ** 0.25
    for layer in range(DEPTH):
        mix = token_mixer(x, positions, w_in[layer], b_in[layer], conv_w[layer], conv_b[layer],
                          sg_ln_g[layer], sg_ln_b[layer], sg_w[layer], sg_b[layer],
                          w_br_a[layer], w_br_b[layer], w_br_c[layer], w_out[layer])
        x = layer_norm(alpha * x + mix, ln_g[layer, 0], ln_b[layer, 0])
        j = layer // 2
        if layer % 2 == 0:
            f = swiglu(x, ffn_w1[j], ffn_w3[j], ffn_w2[j])
        else:
            f = moe_swiglu(x, router_w[j], router_b[j], moe_w1[j], moe_w3[j], moe_w2[j])
        x = layer_norm(alpha * x + f, ln_g[layer, 1], ln_b[layer, 1])
    return x
```

```python
import functools

import numpy as np
import jax
import jax.numpy as jnp
from jax import lax
from jax.experimental import pallas as pl
from jax.experimental.pallas import tpu as pltpu

F32 = jnp.float32
BF16 = jnp.bfloat16

D_MODEL = 1024
ATT_GROUPS = ((128, 1), (512, 4), (2048, 16))
ATT_HEADS_PER_GROUP = 4
ATT_HEAD_DIM = 64
ATT_WIDTH = 768
ATT_GROUP_WIDTH = ATT_HEADS_PER_GROUP * ATT_HEAD_DIM
ROPE_THETA = 10000.0
SG_CHUNK = 128
SG_GROUPS = 6
SG_WIDTH = 768
ML_HEADS = 4
ML_HEAD_DIM = 192
ML_HEAD_PAD = 256
ML_WIDTH = 768
ML_PAD_WIDTH = ML_HEADS * ML_HEAD_PAD
ML_CHUNK = 128
ML_CONV = 4
N_BRANCH = 3
D_FF = 2816
N_EXPERTS = 8
LN_EPS = 1e-5
BLOCK = 128
LANES = 128
VMEM_LIMIT = 56 * 1024 * 1024


def _cparams(*sem):
    return pltpu.CompilerParams(dimension_semantics=sem, vmem_limit_bytes=VMEM_LIMIT)


def _layer_norm(x, g, b):
    mu = jnp.mean(x, axis=-1, keepdims=True)
    xc = x - mu
    var = jnp.mean(xc * xc, axis=-1, keepdims=True)
    return xc * lax.rsqrt(var + LN_EPS) * g + b


def _gelu_tanh(x):
    return 0.5 * x * (1.0 + jnp.tanh(np.sqrt(2.0 / np.pi) * (x + 0.044715 * (x * x * x))))


def _dot(a, b):
    return jnp.dot(a, b, preferred_element_type=F32)


def _dot_nt(a, b):
    return lax.dot_general(a, b, (((1,), (1,)), ((), ())), preferred_element_type=F32)


def _dot_tn(a, b):
    return lax.dot_general(a, b, (((0,), (0,)), ((), ())), preferred_element_type=F32)


def _split3(x):
    hi = x.astype(BF16)
    r1 = x - hi.astype(F32)
    mid = r1.astype(BF16)
    lo = (r1 - mid.astype(F32)).astype(BF16)
    return hi, mid, lo


def _linear_kernel(x_ref, w_ref, b_ref, o_ref):
    o_ref[...] = (_dot(x_ref[...], w_ref[...]) + b_ref[...]).astype(o_ref.dtype)


def _linear(x, w, b, out_dtype, tm, tn):
    m, k = x.shape
    n = w.shape[1]
    return pl.pallas_call(
        _linear_kernel,
        grid=(m // tm, n // tn),
        in_specs=[pl.BlockSpec((tm, k), lambda i, j: (i, 0)),
                  pl.BlockSpec((k, tn), lambda i, j: (0, j)),
                  pl.BlockSpec((1, tn), lambda i, j: (0, j))],
        out_specs=pl.BlockSpec((tm, tn), lambda i, j: (i, j)),
        out_shape=jax.ShapeDtypeStruct((m, n), out_dtype),
        compiler_params=_cparams("parallel", "parallel"),
        name="linear",
    )(x, w, b)


def _rope_table_kernel(pos_ref, freq_ref, sign_ref, cos_ref, sin_ref):
    ang = pos_ref[...] * freq_ref[...]
    cos_ref[...] = jnp.cos(ang)
    sin_ref[...] = jnp.sin(ang) * sign_ref[...]


def _rope_tables(positions):
    n = positions.size
    half = ATT_HEAD_DIM // 2
    freqs = ROPE_THETA ** (-jnp.arange(half, dtype=F32) * (2.0 / ATT_HEAD_DIM))
    lane = np.arange(LANES)
    freq_row = freqs[lane % half][None, :]
    sign_row = jnp.asarray(np.where(lane % ATT_HEAD_DIM < half, -1.0, 1.0), F32)[None, :]
    pos = positions.astype(F32).reshape(n, 1)
    tm = 2048
    return pl.pallas_call(
        _rope_table_kernel,
        grid=(n // tm,),
        in_specs=[pl.BlockSpec((tm, 1), lambda i: (i, 0)),
                  pl.BlockSpec((1, LANES), lambda i: (0, 0)),
                  pl.BlockSpec((1, LANES), lambda i: (0, 0))],
        out_specs=[pl.BlockSpec((tm, LANES), lambda i: (i, 0))] * 2,
        out_shape=[jax.ShapeDtypeStruct((n, LANES), F32)] * 2,
        compiler_params=_cparams("parallel"),
        name="rope_tables",
    )(pos, freq_row, sign_row)


def _attn_kernel(q_ref, k_ref, v_ref, cos_ref, sin_ref, o_ref, lse_ref, qs, ks, vs, *, seq):
    nb = seq // BLOCK
    lane = lax.broadcasted_iota(jnp.int32, (BLOCK, LANES), 1)
    first_half = (lane % ATT_HEAD_DIM) < (ATT_HEAD_DIM // 2)
    low_head = lane < ATT_HEAD_DIM

    def rope(x, c, s):
        partner = jnp.where(first_half, pltpu.roll(x, LANES - 32, 1), pltpu.roll(x, 32, 1))
        return x * c + partner * s

    ks[0:BLOCK, :] = jnp.zeros((BLOCK, ATT_GROUP_WIDTH), BF16)
    vs[0:BLOCK, :] = jnp.zeros((BLOCK, ATT_GROUP_WIDTH), BF16)

    def prep(n, carry):
        r0 = pl.multiple_of(n * BLOCK, BLOCK)
        c = cos_ref[pl.ds(r0, BLOCK), :]
        s = sin_ref[pl.ds(r0, BLOCK), :]
        for pair in range(2):
            cols = slice(pair * LANES, (pair + 1) * LANES)
            q = rope(q_ref[pl.ds(r0, BLOCK), cols].astype(F32), c, s) * (ATT_HEAD_DIM ** -0.5)
            qs[pl.ds(r0, BLOCK), (2 * pair) * LANES:(2 * pair + 1) * LANES] = jnp.where(low_head, q, 0.0).astype(BF16)
            qs[pl.ds(r0, BLOCK), (2 * pair + 1) * LANES:(2 * pair + 2) * LANES] = jnp.where(low_head, 0.0, q).astype(BF16)
            k = rope(k_ref[pl.ds(r0, BLOCK), cols].astype(F32), c, s)
            ks[pl.ds(r0 + BLOCK, BLOCK), cols] = k.astype(BF16)
        vs[pl.ds(r0 + BLOCK, BLOCK), :] = v_ref[pl.ds(r0, BLOCK), :]
        return carry

    lax.fori_loop(0, nb, prep, 0)

    qi = lax.broadcasted_iota(jnp.int32, (BLOCK, 2 * BLOCK), 0)
    ki = lax.broadcasted_iota(jnp.int32, (BLOCK, 2 * BLOCK), 1)
    dist = qi + BLOCK - ki
    band = (dist >= 0) & (dist <= BLOCK)
    neg_inf = jnp.float32(-jnp.inf)
    bias_rest = jnp.where(band, 0.0, neg_inf)
    bias_first = jnp.where(band & (ki >= BLOCK), 0.0, neg_inf)

    def block(n, bias):
        r0 = pl.multiple_of(n * BLOCK, BLOCK)
        for pair in range(2):
            cols = slice(pair * LANES, (pair + 1) * LANES)
            kw = ks[pl.ds(r0, 2 * BLOCK), cols]
            vw = vs[pl.ds(r0, 2 * BLOCK), cols]
            outs = []
            for hh in range(2):
                hd = 2 * pair + hh
                qh = qs[pl.ds(r0, BLOCK), hd * LANES:(hd + 1) * LANES]
                s = _dot_nt(qh, kw) + bias
                m = jnp.max(s, axis=-1, keepdims=True)
                p = jnp.exp(s - m)
                den = jnp.sum(p, axis=-1, keepdims=True)
                o = _dot(p.astype(BF16), vw) * (1.0 / den)
                outs.append((o, m + jnp.log(den)))
            o_ref[pl.ds(r0, BLOCK), cols] = jnp.where(low_head, outs[0][0], outs[1][0]).astype(o_ref.dtype)
            lse_ref[pl.ds(r0, BLOCK), cols] = jnp.where(low_head, outs[0][1], outs[1][1])

    block(0, bias_first)

    def body(n, carry):
        block(n, bias_rest)
        return carry

    lax.fori_loop(1, nb, body, 0)


def _attention_group(att, cos, sin, group, batch, seq):
    window, dil = ATT_GROUPS[group]
    assert window // dil == BLOCK
    sub = seq // dil
    assert sub % BLOCK == 0
    att3 = att.reshape(batch, sub, dil * 3 * ATT_WIDTH)
    cos3 = cos.reshape(batch, sub, dil * LANES)
    sin3 = sin.reshape(batch, sub, dil * LANES)
    blocks_per_row = 3 * ATT_WIDTH // ATT_GROUP_WIDTH
    heads_per_part = ATT_WIDTH // ATT_GROUP_WIDTH

    def col(part):
        return lambda b, r: (b, 0, r * blocks_per_row + part * heads_per_part + group)

    tab = pl.BlockSpec((None, sub, LANES), lambda b, r: (b, 0, r))
    out = pl.BlockSpec((None, sub, ATT_GROUP_WIDTH), lambda b, r: (b, 0, r))
    o, lse = pl.pallas_call(
        functools.partial(_attn_kernel, seq=sub),
        grid=(batch, dil),
        in_specs=[pl.BlockSpec((None, sub, ATT_GROUP_WIDTH), col(0)),
                  pl.BlockSpec((None, sub, ATT_GROUP_WIDTH), col(1)),
                  pl.BlockSpec((None, sub, ATT_GROUP_WIDTH), col(2)),
                  tab, tab],
        out_specs=[out, out],
        out_shape=[jax.ShapeDtypeStruct((batch, sub, dil * ATT_GROUP_WIDTH), BF16),
                   jax.ShapeDtypeStruct((batch, sub, dil * ATT_GROUP_WIDTH), F32)],
        scratch_shapes=[pltpu.VMEM((sub, ATT_HEADS_PER_GROUP * LANES), BF16),
                        pltpu.VMEM((sub + BLOCK, ATT_GROUP_WIDTH), BF16),
                        pltpu.VMEM((sub + BLOCK, ATT_GROUP_WIDTH), BF16)],
        compiler_params=_cparams("parallel", "parallel"),
        name=f"attention_g{group}",
    )(att3, att3, att3, cos3, sin3)
    n = batch * seq
    return o.reshape(n, ATT_GROUP_WIDTH), lse.reshape(n, ATT_GROUP_WIDTH)


def _mlstm_kernel(q_ref, k_ref, v_ref, og_ref, g_ref, cw_ref, cb_ref, out_ref, xq, xk, ct, m_s):
    halo = 8
    taps = ML_CONV

    @pl.when(pl.program_id(1) == 0)
    def _():
        xq[0:halo, :] = jnp.zeros((halo, ML_PAD_WIDTH), F32)
        xk[0:halo, :] = jnp.zeros((halo, ML_PAD_WIDTH), F32)
        ct[...] = jnp.zeros(ct.shape, F32)
        m_s[...] = jnp.zeros(m_s.shape, F32)

    xq[halo:halo + ML_CHUNK, :] = q_ref[...].astype(F32)
    xk[halo:halo + ML_CHUNK, :] = k_ref[...].astype(F32)

    def conv_silu(xbuf, col0):
        acc = cb_ref[:, col0:col0 + ML_PAD_WIDTH]
        for j in range(taps):
            start = halo - (taps - 1) + j
            acc = acc + xbuf[start:start + ML_CHUNK, :] * cw_ref[j:j + 1, col0:col0 + ML_PAD_WIDTH]
        return acc * jax.nn.sigmoid(acc)

    qc = conv_silu(xq, 0).astype(BF16)
    kc = (conv_silu(xk, ML_PAD_WIDTH) * (ML_HEAD_DIM ** -0.5)).astype(BF16)
    xq[0:halo, :] = xq[ML_CHUNK:ML_CHUNK + halo, :]
    xk[0:halo, :] = xk[ML_CHUNK:ML_CHUNK + halo, :]

    gates = g_ref[...]
    log_f = -(jnp.maximum(-gates, 0.0) + jnp.log1p(jnp.exp(-jnp.abs(gates))))
    row = lax.broadcasted_iota(jnp.int32, (ML_CHUNK, ML_CHUNK), 0)
    colm = lax.broadcasted_iota(jnp.int32, (ML_CHUNK, ML_CHUNK), 1)
    causal = row >= colm
    tri = jnp.where(causal, 1.0, 0.0).astype(BF16)
    hi, mid, lo = _split3(log_f)
    bcum = _dot(tri, hi) + _dot(tri, mid) + _dot(tri, lo)
    bcum_t = bcum.T
    gates_t = gates.T
    neg_inf = jnp.float32(-jnp.inf)

    for h in range(ML_HEADS):
        cols = slice(h * ML_HEAD_PAD, (h + 1) * ML_HEAD_PAD)
        b_c = bcum[:, ML_HEADS + h:ML_HEADS + h + 1]
        b_r = bcum_t[ML_HEADS + h:ML_HEADS + h + 1, :]
        i_c = gates[:, h:h + 1]
        i_r = gates_t[h:h + 1, :]
        m_prev = m_s[h, 0:1, 0:1]
        log_d = jnp.where(causal, b_c - b_r + i_r, neg_inf)
        m_inter = b_c + m_prev
        m = jnp.maximum(m_inter, jnp.max(log_d, axis=-1, keepdims=True))
        qh = qc[:, cols]
        kh = kc[:, cols]
        vh = v_ref[:, cols]
        sc = _dot_nt(qh, kh) * jnp.exp(log_d - m)
        inter = jnp.exp(m_inter - m)
        c_old = ct[h]
        num = _dot(sc.astype(BF16), vh) + inter * _dot(qh, c_old.astype(BF16))
        den = num[:, ML_HEAD_DIM:ML_HEAD_DIM + 1]
        hid = num / jnp.maximum(jnp.abs(den), jnp.exp(-m))
        out_ref[:, cols] = (jax.nn.sigmoid(og_ref[:, cols].astype(F32)) * hid).astype(out_ref.dtype)
        b_last = b_c[ML_CHUNK - 1:ML_CHUNK, :]
        m_new = m[ML_CHUNK - 1:ML_CHUNK, :]
        w = jnp.exp(b_last - b_c + i_c - m_new)
        decay = jnp.exp(b_last + m_prev - m_new)
        wv = (w * vh.astype(F32)).astype(BF16)
        ct[h] = decay * c_old + _dot_tn(kh, wv)
        m_s[h] = jnp.broadcast_to(m_new, m_s.shape[1:])


def _mlstm(ml, gates, conv_w, conv_b, batch, seq):
    n = batch * seq
    nc = seq // ML_CHUNK

    def part(p):
        return pl.BlockSpec((ML_CHUNK, ML_PAD_WIDTH), lambda b, c: (b * nc + c, p))

    return pl.pallas_call(
        _mlstm_kernel,
        grid=(batch, nc),
        in_specs=[part(0), part(1), part(2), part(3),
                  pl.BlockSpec((ML_CHUNK, LANES), lambda b, c: (b * nc + c, 0)),
                  pl.BlockSpec((ML_CONV, 2 * ML_PAD_WIDTH), lambda b, c: (0, 0)),
                  pl.BlockSpec((1, 2 * ML_PAD_WIDTH), lambda b, c: (0, 0))],
        out_specs=pl.BlockSpec((ML_CHUNK, ML_PAD_WIDTH), lambda b, c: (b * nc + c, 0)),
        out_shape=jax.ShapeDtypeStruct((n, ML_PAD_WIDTH), BF16),
        scratch_shapes=[pltpu.VMEM((ML_CHUNK + 8, ML_PAD_WIDTH), F32),
                        pltpu.VMEM((ML_CHUNK + 8, ML_PAD_WIDTH), F32),
                        pltpu.VMEM((ML_HEADS, ML_HEAD_PAD, ML_HEAD_PAD), F32),
                        pltpu.VMEM((ML_HEADS, 8, LANES), F32)],
        compiler_params=_cparams("parallel", "arbitrary"),
        name="mlstm",
    )(ml, ml, ml, ml, gates, conv_w, conv_b)


def _mixer_out_kernel(x_ref, o0_ref, o1_ref, o2_ref, l0_ref, l1_ref, l2_ref, sg_ref, hc_ref, gp_ref,
                      sgg_ref, sgb_ref, sgw_ref, sgbias_ref, wa_ref, wb_ref, wc_ref, wo_ref, lng_ref, lnb_ref,
                      xf_ref, xb_ref, *, alpha):
    tm = x_ref.shape[0]
    l0, l1, l2 = l0_ref[...], l1_ref[...], l2_ref[...]
    lmax = jnp.maximum(jnp.maximum(l0, l1), l2)
    e0, e1, e2 = jnp.exp(l0 - lmax), jnp.exp(l1 - lmax), jnp.exp(l2 - lmax)
    inv = 1.0 / (e0 + e1 + e2)
    y_a = (e0 * inv) * o0_ref[...].astype(F32) + (e1 * inv) * o1_ref[...].astype(F32) + (e2 * inv) * o2_ref[...].astype(F32)

    u = _gelu_tanh(sg_ref[:, 0:SG_WIDTH].astype(F32))
    v = _gelu_tanh(sg_ref[:, SG_WIDTH:2 * SG_WIDTH].astype(F32))
    v = _layer_norm(v, sgg_ref[...], sgb_ref[...]).astype(BF16)
    row = lax.broadcasted_iota(jnp.int32, (SG_CHUNK, SG_CHUNK), 0)
    colm = lax.broadcasted_iota(jnp.int32, (SG_CHUNK, SG_CHUNK), 1)
    causal = row >= colm
    group_dim = SG_WIDTH // SG_GROUPS
    chunks = []
    for c in range(tm // SG_CHUNK):
        rows = slice(c * SG_CHUNK, (c + 1) * SG_CHUNK)
        parts = []
        for g in range(SG_GROUPS):
            w = jnp.where(causal, sgw_ref[g], 0.0).astype(BF16)
            mixed = _dot(w, v[rows, g * group_dim:(g + 1) * group_dim]) + sgbias_ref[:, g:g + 1]
            parts.append(mixed)
        chunks.append(jnp.concatenate(parts, axis=1))
    y_b = u * jnp.concatenate(chunks, axis=0)

    def gate(i):
        return jax.nn.sigmoid(gp_ref[:, i * D_MODEL:(i + 1) * D_MODEL].astype(F32))

    z = (gate(0) * _dot(y_a.astype(BF16), wa_ref[...])
         + gate(1) * _dot(y_b.astype(BF16), wb_ref[...])
         + gate(2) * _dot(hc_ref[...], wc_ref[...]))
    mix = _dot(z.astype(BF16), wo_ref[...])
    out = _layer_norm(alpha * x_ref[...] + mix, lng_ref[...], lnb_ref[...])
    xf_ref[...] = out
    xb_ref[...] = out.astype(BF16)


def _mixer_out(x, o_groups, lse_groups, sg, hc, gpre, p, alpha, tm=256):
    n = x.shape[0]

    def rows(width):
        return pl.BlockSpec((tm, width), lambda i: (i, 0))

    def whole(a):
        return pl.BlockSpec(a.shape, lambda i: (0,) * a.ndim)

    weights = [p["sg_ln_g"], p["sg_ln_b"], p["sg_w"], p["sg_bias_t"], p["w_br_a"], p["w_br_b"], p["w_br_c"],
               p["w_out"], p["ln_g0"], p["ln_b0"]]
    return pl.pallas_call(
        functools.partial(_mixer_out_kernel, alpha=alpha),
        grid=(n // tm,),
        in_specs=[rows(D_MODEL)] + [rows(ATT_GROUP_WIDTH)] * 6
                 + [rows(2 * SG_WIDTH), rows(ML_PAD_WIDTH), rows(N_BRANCH * D_MODEL)]
                 + [whole(w) for w in weights],
        out_specs=[rows(D_MODEL), rows(D_MODEL)],
        out_shape=[jax.ShapeDtypeStruct((n, D_MODEL), F32), jax.ShapeDtypeStruct((n, D_MODEL), BF16)],
        compiler_params=_cparams("parallel"),
        name="mixer_out",
    )(x, *o_groups, *lse_groups, sg, hc, gpre, *weights)


def _ffn_kernel(xf_ref, xb_ref, w1_ref, w3_ref, w2_ref, lng_ref, lnb_ref, of_ref, ob_ref, acc, *, alpha):
    f = pl.program_id(1)
    xb = xb_ref[...]
    h1 = _dot(xb, w1_ref[...])
    h3 = _dot(xb, w3_ref[...])
    part = _dot((h1 * jax.nn.sigmoid(h1) * h3).astype(BF16), w2_ref[...])

    @pl.when(f == 0)
    def _():
        acc[...] = part

    @pl.when(f > 0)
    def _():
        acc[...] += part

    @pl.when(f == pl.num_programs(1) - 1)
    def _():
        out = _layer_norm(alpha * xf_ref[...] + acc[...], lng_ref[...], lnb_ref[...])
        of_ref[...] = out
        ob_ref[...] = out.astype(BF16)


def _ffn(xf, xb, w1, w3, w2, ln_g, ln_b, alpha, tm=512, tf=1408):
    n = xf.shape[0]
    return pl.pallas_call(
        functools.partial(_ffn_kernel, alpha=alpha),
        grid=(n // tm, D_FF // tf),
        in_specs=[pl.BlockSpec((tm, D_MODEL), lambda i, f: (i, 0)),
                  pl.BlockSpec((tm, D_MODEL), lambda i, f: (i, 0)),
                  pl.BlockSpec((D_MODEL, tf), lambda i, f: (0, f)),
                  pl.BlockSpec((D_MODEL, tf), lambda i, f: (0, f)),
                  pl.BlockSpec((tf, D_MODEL), lambda i, f: (f, 0)),
                  pl.BlockSpec((1, D_MODEL), lambda i, f: (0, 0)),
                  pl.BlockSpec((1, D_MODEL), lambda i, f: (0, 0))],
        out_specs=[pl.BlockSpec((tm, D_MODEL), lambda i, f: (i, 0))] * 2,
        out_shape=[jax.ShapeDtypeStruct((n, D_MODEL), F32), jax.ShapeDtypeStruct((n, D_MODEL), BF16)],
        scratch_shapes=[pltpu.VMEM((tm, D_MODEL), F32)],
        compiler_params=_cparams("parallel", "arbitrary"),
        name="ffn",
    )(xf, xb, w1, w3, w2, ln_g, ln_b)


def _router_kernel(x_ref, w_ref, b_ref, g_ref):
    xh, xm, xl = _split3(x_ref[...])
    wh, wm, wl = _split3(w_ref[...])
    logits = (_dot(xh, wh) + (_dot(xh, wm) + _dot(xm, wh))
              + (_dot(xh, wl) + _dot(xm, wm) + _dot(xl, wh))) + b_ref[...]
    lane = lax.broadcasted_iota(jnp.int32, logits.shape, 1)
    neg_inf = jnp.float32(-jnp.inf)
    logits = jnp.where(lane < N_EXPERTS, logits, neg_inf)
    v1 = jnp.max(logits, axis=-1, keepdims=True)
    i1 = jnp.min(jnp.where(logits == v1, lane, LANES), axis=-1, keepdims=True)
    rest = jnp.where(lane == i1, neg_inf, logits)
    v2 = jnp.max(rest, axis=-1, keepdims=True)
    i2 = jnp.min(jnp.where(rest == v2, lane, LANES), axis=-1, keepdims=True)
    e = jnp.exp(v2 - v1)
    inv = 1.0 / (1.0 + e)
    g_ref[...] = jnp.where(lane == i1, inv, 0.0) + jnp.where(lane == i2, e * inv, 0.0)


def _router(xf, router_w, router_b, tm=1024):
    n = xf.shape[0]
    w = jnp.pad(router_w, ((0, 0), (0, LANES - N_EXPERTS)))
    b = jnp.pad(router_b, (0, LANES - N_EXPERTS))[None, :]
    return pl.pallas_call(
        _router_kernel,
        grid=(n // tm,),
        in_specs=[pl.BlockSpec((tm, D_MODEL), lambda i: (i, 0)),
                  pl.BlockSpec((D_MODEL, LANES), lambda i: (0, 0)),
                  pl.BlockSpec((1, LANES), lambda i: (0, 0))],
        out_specs=pl.BlockSpec((tm, LANES), lambda i: (i, 0)),
        out_shape=jax.ShapeDtypeStruct((n, LANES), F32),
        compiler_params=_cparams("parallel"),
        name="router",
    )(xf, w, b)


def _moe_kernel(xf_ref, xb_ref, g_ref, w1_ref, w3_ref, w2_ref, lng_ref, lnb_ref, of_ref, acc, *, alpha):
    e = pl.program_id(1)
    f = pl.program_id(2)
    xb = xb_ref[...]
    h1 = _dot(xb, w1_ref[...])
    h3 = _dot(xb, w3_ref[...])
    gates = g_ref[...]
    lane = lax.broadcasted_iota(jnp.int32, gates.shape, 1)
    gate = jnp.sum(jnp.where(lane == e, gates, 0.0), axis=-1, keepdims=True)
    part = gate * _dot((h1 * jax.nn.sigmoid(h1) * h3).astype(BF16), w2_ref[...])
    first = (e == 0) & (f == 0)

    @pl.when(first)
    def _():
        acc[...] = part

    @pl.when(jnp.logical_not(first))
    def _():
        acc[...] += part

    @pl.when((e == pl.num_programs(1) - 1) & (f == pl.num_programs(2) - 1))
    def _():
        of_ref[...] = _layer_norm(alpha * xf_ref[...] + acc[...], lng_ref[...], lnb_ref[...])


def _moe(xf, xb, gates, w1, w3, w2, ln_g, ln_b, alpha, tm=512, tf=1408):
    n = xf.shape[0]
    return pl.pallas_call(
        functools.partial(_moe_kernel, alpha=alpha),
        grid=(n // tm, N_EXPERTS, D_FF // tf),
        in_specs=[pl.BlockSpec((tm, D_MODEL), lambda i, e, f: (i, 0)),
                  pl.BlockSpec((tm, D_MODEL), lambda i, e, f: (i, 0)),
                  pl.BlockSpec((tm, LANES), lambda i, e, f: (i, 0)),
                  pl.BlockSpec((None, D_MODEL, tf), lambda i, e, f: (e, 0, f)),
                  pl.BlockSpec((None, D_MODEL, tf), lambda i, e, f: (e, 0, f)),
                  pl.BlockSpec((None, tf, D_MODEL), lambda i, e, f: (e, f, 0)),
                  pl.BlockSpec((1, D_MODEL), lambda i, e, f: (0, 0)),
                  pl.BlockSpec((1, D_MODEL), lambda i, e, f: (0, 0))],
        out_specs=pl.BlockSpec((tm, D_MODEL), lambda i, e, f: (i, 0)),
        out_shape=jax.ShapeDtypeStruct((n, D_MODEL), F32),
        scratch_shapes=[pltpu.VMEM((tm, D_MODEL), F32)],
        compiler_params=_cparams("parallel", "arbitrary", "arbitrary"),
        name="moe",
    )(xf, xb, gates, w1, w3, w2, ln_g, ln_b)


def _pad_heads(w):
    lead = w.shape[:-1]
    w = w.reshape(*lead, ML_HEADS, ML_HEAD_DIM)
    w = jnp.pad(w, [(0, 0)] * len(lead) + [(0, 0), (0, ML_HEAD_PAD - ML_HEAD_DIM)])
    return w.reshape(*lead, ML_PAD_WIDTH)


def _layer_params(layer, w_in, b_in, conv_w, conv_b, sg_ln_g, sg_ln_b, sg_w, sg_b,
                  w_br_a, w_br_b, w_br_c, w_out, ln_g, ln_b):
    w, b = w_in[layer], b_in[layer]
    o_att, o_sg, o_qk, o_v, o_o, o_i, o_g = 0, 2304, 3840, 5376, 6144, 6912, 6920
    ones_col = jnp.zeros((ML_HEADS, ML_HEAD_PAD), F32).at[:, ML_HEAD_DIM].set(1.0).reshape(ML_PAD_WIDTH)
    w_ml = jnp.concatenate([_pad_heads(w[:, o_qk:o_qk + ML_WIDTH]), _pad_heads(w[:, o_qk + ML_WIDTH:o_v]),
                            _pad_heads(w[:, o_v:o_o]), _pad_heads(w[:, o_o:o_i])], axis=1)
    b_ml = jnp.concatenate([_pad_heads(b[o_qk:o_qk + ML_WIDTH]), _pad_heads(b[o_qk + ML_WIDTH:o_v]),
                            _pad_heads(b[o_v:o_o]) + ones_col, _pad_heads(b[o_o:o_i])])
    cw, cb = conv_w[layer], conv_b[layer]
    return dict(
        w_att=w[:, o_att:o_sg].astype(BF16), b_att=b[None, o_att:o_sg],
        w_sg=w[:, o_sg:o_qk].astype(BF16), b_sg=b[None, o_sg:o_qk],
        w_ml=w_ml.astype(BF16), b_ml=b_ml[None, :],
        w_if=jnp.pad(w[:, o_i:o_g], ((0, 0), (0, LANES - 2 * ML_HEADS))).astype(BF16),
        b_if=jnp.pad(b[o_i:o_g], (0, LANES - 2 * ML_HEADS))[None, :],
        w_g=w[:, o_g:].astype(BF16), b_g=b[None, o_g:],
        conv_w=jnp.concatenate([_pad_heads(cw[:, :ML_WIDTH]), _pad_heads(cw[:, ML_WIDTH:])], axis=1),
        conv_b=jnp.concatenate([_pad_heads(cb[:ML_WIDTH]), _pad_heads(cb[ML_WIDTH:])])[None, :],
        sg_ln_g=sg_ln_g[layer][None, :], sg_ln_b=sg_ln_b[layer][None, :],
        sg_w=sg_w[layer], sg_bias_t=sg_b[layer].T,
        w_br_a=w_br_a[layer].astype(BF16), w_br_b=w_br_b[layer].astype(BF16),
        w_br_c=_pad_heads(w_br_c[layer].T).T.astype(BF16),
        w_out=w_out[layer].astype(BF16),
        ln_g0=ln_g[layer, 0][None, :], ln_b0=ln_b[layer, 0][None, :],
        ln_g1=ln_g[layer, 1][None, :], ln_b1=ln_b[layer, 1][None, :],
    )


def kernel(x, positions, w_in, b_in, conv_w, conv_b, sg_ln_g, sg_ln_b, sg_w, sg_b, w_br_a, w_br_b, w_br_c, w_out,
           ln_g, ln_b, ffn_w1, ffn_w3, ffn_w2, router_w, router_b, moe_w1, moe_w3, moe_w2):
    batch, seq, _ = x.shape
    n = batch * seq
    depth = w_in.shape[0]
    alpha = (2.0 * depth) ** 0.25
    cos, sin = _rope_tables(positions)
    xf = x.reshape(n, D_MODEL)
    xb = xf.astype(BF16)
    for layer in range(depth):
        p = _layer_params(layer, w_in, b_in, conv_w, conv_b, sg_ln_g, sg_ln_b, sg_w, sg_b,
                          w_br_a, w_br_b, w_br_c, w_out, ln_g, ln_b)
        att = _linear(xb, p["w_att"], p["b_att"], BF16, 1024, 768)
        sg = _linear(xb, p["w_sg"], p["b_sg"], BF16, 1024, 768)
        ml = _linear(xb, p["w_ml"], p["b_ml"], BF16, 1024, 1024)
        gates_if = _linear(xb, p["w_if"], p["b_if"], F32, 1024, LANES)
        gpre = _linear(xb, p["w_g"], p["b_g"], BF16, 1024, 1024)
        o_groups, lse_groups = [], []
        for g in range(len(ATT_GROUPS)):
            o, lse = _attention_group(att, cos, sin, g, batch, seq)
            o_groups.append(o)
            lse_groups.append(lse)
        hc = _mlstm(ml, gates_if, p["conv_w"], p["conv_b"], batch, seq)
        xf, xb = _mixer_out(xf, o_groups, lse_groups, sg, hc, gpre, p, alpha)
        j = layer // 2
        if layer % 2 == 0:
            xf, xb = _ffn(xf, xb, ffn_w1[j].astype(BF16), ffn_w3[j].astype(BF16), ffn_w2[j].astype(BF16),
                          p["ln_g1"], p["ln_b1"], alpha)
        else:
            gates = _router(xf, router_w[j], router_b[j])
            xf = _moe(xf, xb, gates, moe_w1[j].astype(BF16), moe_w3[j].astype(BF16), moe_w2[j].astype(BF16),
                      p["ln_g1"], p["ln_b1"], alpha)
            xb = xf.astype(BF16)
    return xf.reshape(batch, seq, D_MODEL)
```

```python
import functools

import numpy as np
import jax
import jax.numpy as jnp
from jax import lax
from jax.experimental import pallas as pl
from jax.experimental.pallas import tpu as pltpu

F32 = jnp.float32
BF16 = jnp.bfloat16
U32 = jnp.uint32

D_MODEL = 1024
ATT_GROUPS = ((128, 1), (512, 4), (2048, 16))
ATT_HEADS_PER_GROUP = 4
ATT_HEAD_DIM = 64
ATT_WIDTH = 768
ATT_GROUP_WIDTH = ATT_HEADS_PER_GROUP * ATT_HEAD_DIM
ROPE_THETA = 10000.0
SG_CHUNK = 128
SG_GROUPS = 6
SG_WIDTH = 768
ML_HEADS = 4
ML_HEAD_DIM = 192
ML_HEAD_PAD = 256
ML_WIDTH = 768
ML_PAD_WIDTH = ML_HEADS * ML_HEAD_PAD
ML_CHUNK = 128
ML_CONV = 4
N_BRANCH = 3
D_FF = 2816
N_EXPERTS = 8
TOP_K = 2
LN_EPS = 1e-5
BLOCK = 128
LANES = 128
VMEM_LIMIT = 56 * 1024 * 1024
PACKED = D_MODEL // 2
EXPERT_TILE = 512


def _cparams(*sem):
    return pltpu.CompilerParams(dimension_semantics=sem, vmem_limit_bytes=VMEM_LIMIT)


def _layer_norm(x, g, b):
    mu = jnp.mean(x, axis=-1, keepdims=True)
    xc = x - mu
    var = jnp.mean(xc * xc, axis=-1, keepdims=True)
    return xc * lax.rsqrt(var + LN_EPS) * g + b


def _gelu_tanh(x):
    return 0.5 * x * (1.0 + jnp.tanh(np.sqrt(2.0 / np.pi) * (x + 0.044715 * (x * x * x))))


def _dot(a, b):
    return jnp.dot(a, b, preferred_element_type=F32)


def _dot_nt(a, b):
    return lax.dot_general(a, b, (((1,), (1,)), ((), ())), preferred_element_type=F32)


def _dot_tn(a, b):
    return lax.dot_general(a, b, (((0,), (0,)), ((), ())), preferred_element_type=F32)


def _split3(x):
    hi = x.astype(BF16)
    r1 = x - hi.astype(F32)
    mid = r1.astype(BF16)
    lo = (r1 - mid.astype(F32)).astype(BF16)
    return hi, mid, lo


def _pack_rows(x):
    xr = x.astype(BF16).astype(F32)
    lo = lax.shift_right_logical(lax.bitcast_convert_type(xr[:, :PACKED], U32), jnp.uint32(16))
    hi = lax.bitcast_convert_type(xr[:, PACKED:], U32)
    return lo | hi


def _unpack_rows(u):
    lo = lax.bitcast_convert_type(lax.shift_left(u, jnp.uint32(16)), F32)
    hi = lax.bitcast_convert_type(u & jnp.uint32(0xFFFF0000), F32)
    return jnp.concatenate([lo, hi], axis=1)


def _linear_kernel(x_ref, w_ref, b_ref, o_ref):
    o_ref[...] = (_dot(x_ref[...], w_ref[...]) + b_ref[...]).astype(o_ref.dtype)


def _linear(x, w, b, out_dtype, tm, tn):
    m, k = x.shape
    n = w.shape[1]
    return pl.pallas_call(
        _linear_kernel,
        grid=(m // tm, n // tn),
        in_specs=[pl.BlockSpec((tm, k), lambda i, j: (i, 0)),
                  pl.BlockSpec((k, tn), lambda i, j: (0, j)),
                  pl.BlockSpec((1, tn), lambda i, j: (0, j))],
        out_specs=pl.BlockSpec((tm, tn), lambda i, j: (i, j)),
        out_shape=jax.ShapeDtypeStruct((m, n), out_dtype),
        compiler_params=_cparams("parallel", "parallel"),
        name="linear",
    )(x, w, b)


def _linear_dilated_kernel(x_ref, w_ref, b_ref, o_ref, acc, *, dil):
    res = _dot(x_ref[...], w_ref[...]) + b_ref[...]
    slabs, rows, _ = acc.shape
    for c in range(slabs):
        acc[c] = res[:, c * LANES:(c + 1) * LANES]
    sub = rows // dil
    for r in range(dil):
        for c in range(slabs):
            o_ref[r, :, c * LANES:(c + 1) * LANES] = acc[c, pl.ds(r, sub, stride=dil), :].astype(o_ref.dtype)


def _linear_dilated(x, w, b, dil, batch, seq):
    k = x.shape[1]
    n = w.shape[1]
    tn = ATT_GROUP_WIDTH
    sub = seq // dil
    return pl.pallas_call(
        functools.partial(_linear_dilated_kernel, dil=dil),
        grid=(batch, n // tn),
        in_specs=[pl.BlockSpec((seq, k), lambda i, j: (i, 0)),
                  pl.BlockSpec((k, tn), lambda i, j: (0, j)),
                  pl.BlockSpec((1, tn), lambda i, j: (0, j))],
        out_specs=pl.BlockSpec((None, dil, sub, tn), lambda i, j: (i, 0, 0, j)),
        out_shape=jax.ShapeDtypeStruct((batch, dil, sub, n), BF16),
        scratch_shapes=[pltpu.VMEM((tn // LANES, seq, LANES), F32)],
        compiler_params=_cparams("parallel", "parallel"),
        name=f"linear_dil{dil}",
    )(x, w, b)


def _rope_table_kernel(pos_ref, freq_ref, sign_ref, cos_ref, sin_ref):
    ang = pos_ref[...] * freq_ref[...]
    cos_ref[...] = jnp.cos(ang)
    sin_ref[...] = jnp.sin(ang) * sign_ref[...]


def _rope_tables(positions):
    n = positions.size
    half = ATT_HEAD_DIM // 2
    freqs = ROPE_THETA ** (-jnp.arange(half, dtype=F32) * (2.0 / ATT_HEAD_DIM))
    lane = np.arange(LANES)
    freq_row = freqs[lane % half][None, :]
    sign_row = jnp.asarray(np.where(lane % ATT_HEAD_DIM < half, -1.0, 1.0), F32)[None, :]
    pos = positions.astype(F32).reshape(n, 1)
    tm = 2048
    return pl.pallas_call(
        _rope_table_kernel,
        grid=(n // tm,),
        in_specs=[pl.BlockSpec((tm, 1), lambda i: (i, 0)),
                  pl.BlockSpec((1, LANES), lambda i: (0, 0)),
                  pl.BlockSpec((1, LANES), lambda i: (0, 0))],
        out_specs=[pl.BlockSpec((tm, LANES), lambda i: (i, 0))] * 2,
        out_shape=[jax.ShapeDtypeStruct((n, LANES), F32)] * 2,
        compiler_params=_cparams("parallel"),
        name="rope_tables",
    )(pos, freq_row, sign_row)


def _attn_kernel(q_ref, k_ref, v_ref, cos_ref, sin_ref, o_ref, lse_ref, qs, ks, vs, *, sub, dil):
    nb = sub // BLOCK
    res = pl.program_id(1)
    lane = lax.broadcasted_iota(jnp.int32, (BLOCK, LANES), 1)
    first_half = (lane % ATT_HEAD_DIM) < (ATT_HEAD_DIM // 2)
    low_head = lane < ATT_HEAD_DIM

    def rope(x, c, s):
        partner = jnp.where(first_half, pltpu.roll(x, LANES - 32, 1), pltpu.roll(x, 32, 1))
        return x * c + partner * s

    ks[0:BLOCK, :] = jnp.zeros((BLOCK, ATT_GROUP_WIDTH), BF16)
    vs[0:BLOCK, :] = jnp.zeros((BLOCK, ATT_GROUP_WIDTH), BF16)

    def prep(n, carry):
        r0 = pl.multiple_of(n * BLOCK, BLOCK)
        if dil == 1:
            c = cos_ref[pl.ds(r0, BLOCK), :]
            s = sin_ref[pl.ds(r0, BLOCK), :]
        else:
            c = cos_ref[pl.ds(r0 * dil + res, BLOCK, stride=dil), :]
            s = sin_ref[pl.ds(r0 * dil + res, BLOCK, stride=dil), :]
        for pair in range(2):
            cols = slice(pair * LANES, (pair + 1) * LANES)
            q = rope(q_ref[pl.ds(r0, BLOCK), cols].astype(F32), c, s) * (ATT_HEAD_DIM ** -0.5)
            qs[pl.ds(r0, BLOCK), (2 * pair) * LANES:(2 * pair + 1) * LANES] = jnp.where(low_head, q, 0.0).astype(BF16)
            qs[pl.ds(r0, BLOCK), (2 * pair + 1) * LANES:(2 * pair + 2) * LANES] = jnp.where(low_head, 0.0, q).astype(BF16)
            k = rope(k_ref[pl.ds(r0, BLOCK), cols].astype(F32), c, s)
            ks[pl.ds(r0 + BLOCK, BLOCK), cols] = k.astype(BF16)
        vs[pl.ds(r0 + BLOCK, BLOCK), :] = v_ref[pl.ds(r0, BLOCK), :]
        return carry

    lax.fori_loop(0, nb, prep, 0)

    qi = lax.broadcasted_iota(jnp.int32, (BLOCK, 2 * BLOCK), 0)
    ki = lax.broadcasted_iota(jnp.int32, (BLOCK, 2 * BLOCK), 1)
    dist = qi + BLOCK - ki
    band = (dist >= 0) & (dist <= BLOCK)
    neg_inf = jnp.float32(-jnp.inf)
    bias_rest = jnp.where(band, 0.0, neg_inf)
    bias_first = jnp.where(band & (ki >= BLOCK), 0.0, neg_inf)

    def block(n, bias):
        r0 = pl.multiple_of(n * BLOCK, BLOCK)
        for pair in range(2):
            cols = slice(pair * LANES, (pair + 1) * LANES)
            kw = ks[pl.ds(r0, 2 * BLOCK), cols]
            vw = vs[pl.ds(r0, 2 * BLOCK), cols]
            outs = []
            for hh in range(2):
                hd = 2 * pair + hh
                qh = qs[pl.ds(r0, BLOCK), hd * LANES:(hd + 1) * LANES]
                s = _dot_nt(qh, kw) + bias
                m = jnp.max(s, axis=-1, keepdims=True)
                p = jnp.exp(s - m)
                den = jnp.sum(p, axis=-1, keepdims=True)
                o = _dot(p.astype(BF16), vw) * (1.0 / den)
                outs.append((o, m + jnp.log(den)))
            o_ref[pl.ds(r0, BLOCK), cols] = jnp.where(low_head, outs[0][0], outs[1][0]).astype(o_ref.dtype)
            lse_ref[pl.ds(r0, BLOCK), cols] = jnp.where(low_head, outs[0][1], outs[1][1])

    block(0, bias_first)

    def body(n, carry):
        block(n, bias_rest)
        return carry

    lax.fori_loop(1, nb, body, 0)


def _attention_group(qkv, cos, sin, group, batch, seq):
    window, dil = ATT_GROUPS[group]
    assert window // dil == BLOCK
    sub = seq // dil
    assert sub % BLOCK == 0

    def part(p):
        return pl.BlockSpec((None, None, sub, ATT_GROUP_WIDTH), lambda b, r: (b, r, 0, p))

    tab = pl.BlockSpec((seq, LANES), lambda b, r: (b, 0))
    out = pl.BlockSpec((None, None, sub, ATT_GROUP_WIDTH), lambda b, r: (b, r, 0, 0))
    return pl.pallas_call(
        functools.partial(_attn_kernel, sub=sub, dil=dil),
        grid=(batch, dil),
        in_specs=[part(0), part(1), part(2), tab, tab],
        out_specs=[out, out],
        out_shape=[jax.ShapeDtypeStruct((batch, dil, sub, ATT_GROUP_WIDTH), BF16),
                   jax.ShapeDtypeStruct((batch, dil, sub, ATT_GROUP_WIDTH), F32)],
        scratch_shapes=[pltpu.VMEM((sub, ATT_HEADS_PER_GROUP * LANES), BF16),
                        pltpu.VMEM((sub + BLOCK, ATT_GROUP_WIDTH), BF16),
                        pltpu.VMEM((sub + BLOCK, ATT_GROUP_WIDTH), BF16)],
        compiler_params=_cparams("parallel", "parallel"),
        name=f"attention_g{group}",
    )(qkv, qkv, qkv, cos, sin)


def _mlstm_kernel(q_ref, k_ref, v_ref, og_ref, g_ref, cw_ref, cb_ref, out_ref, xq, xk, ct, m_s):
    halo = 8
    taps = ML_CONV

    @pl.when(pl.program_id(1) == 0)
    def _():
        xq[0:halo, :] = jnp.zeros((halo, ML_PAD_WIDTH), F32)
        xk[0:halo, :] = jnp.zeros((halo, ML_PAD_WIDTH), F32)
        ct[...] = jnp.zeros(ct.shape, F32)
        m_s[...] = jnp.zeros(m_s.shape, F32)

    xq[halo:halo + ML_CHUNK, :] = q_ref[...].astype(F32)
    xk[halo:halo + ML_CHUNK, :] = k_ref[...].astype(F32)

    def conv_silu(xbuf, col0):
        acc = cb_ref[:, col0:col0 + ML_PAD_WIDTH]
        for j in range(taps):
            start = halo - (taps - 1) + j
            acc = acc + xbuf[start:start + ML_CHUNK, :] * cw_ref[j:j + 1, col0:col0 + ML_PAD_WIDTH]
        return acc * jax.nn.sigmoid(acc)

    qc = conv_silu(xq, 0).astype(BF16)
    kc = (conv_silu(xk, ML_PAD_WIDTH) * (ML_HEAD_DIM ** -0.5)).astype(BF16)
    xq[0:halo, :] = xq[ML_CHUNK:ML_CHUNK + halo, :]
    xk[0:halo, :] = xk[ML_CHUNK:ML_CHUNK + halo, :]

    gates = g_ref[...]
    log_f = -(jnp.maximum(-gates, 0.0) + jnp.log1p(jnp.exp(-jnp.abs(gates))))
    row = lax.broadcasted_iota(jnp.int32, (ML_CHUNK, ML_CHUNK), 0)
    colm = lax.broadcasted_iota(jnp.int32, (ML_CHUNK, ML_CHUNK), 1)
    causal = row >= colm
    tri = jnp.where(causal, 1.0, 0.0).astype(BF16)
    hi, mid, lo = _split3(log_f)
    bcum = _dot(tri, hi) + _dot(tri, mid) + _dot(tri, lo)
    bcum_t = bcum.T
    gates_t = gates.T
    neg_inf = jnp.float32(-jnp.inf)

    for h in range(ML_HEADS):
        cols = slice(h * ML_HEAD_PAD, (h + 1) * ML_HEAD_PAD)
        b_c = bcum[:, ML_HEADS + h:ML_HEADS + h + 1]
        b_r = bcum_t[ML_HEADS + h:ML_HEADS + h + 1, :]
        i_c = gates[:, h:h + 1]
        i_r = gates_t[h:h + 1, :]
        m_prev = m_s[h, 0:1, 0:1]
        log_d = jnp.where(causal, b_c - b_r + i_r, neg_inf)
        m_inter = b_c + m_prev
        m = jnp.maximum(m_inter, jnp.max(log_d, axis=-1, keepdims=True))
        qh = qc[:, cols]
        kh = kc[:, cols]
        vh = v_ref[:, cols]
        sc = _dot_nt(qh, kh) * jnp.exp(log_d - m)
        inter = jnp.exp(m_inter - m)
        c_old = ct[h]
        num = _dot(sc.astype(BF16), vh) + inter * _dot(qh, c_old.astype(BF16))
        den = num[:, ML_HEAD_DIM:ML_HEAD_DIM + 1]
        hid = num / jnp.maximum(jnp.abs(den), jnp.exp(-m))
        out_ref[:, cols] = (jax.nn.sigmoid(og_ref[:, cols].astype(F32)) * hid).astype(out_ref.dtype)
        b_last = b_c[ML_CHUNK - 1:ML_CHUNK, :]
        m_new = m[ML_CHUNK - 1:ML_CHUNK, :]
        w = jnp.exp(b_last - b_c + i_c - m_new)
        decay = jnp.exp(b_last + m_prev - m_new)
        wv = (w * vh.astype(F32)).astype(BF16)
        ct[h] = decay * c_old + _dot_tn(kh, wv)
        m_s[h] = jnp.broadcast_to(m_new, m_s.shape[1:])


def _mlstm(ml, gates, conv_w, conv_b, batch, seq):
    n = batch * seq
    nc = seq // ML_CHUNK

    def part(p):
        return pl.BlockSpec((ML_CHUNK, ML_PAD_WIDTH), lambda b, c: (b * nc + c, p))

    return pl.pallas_call(
        _mlstm_kernel,
        grid=(batch, nc),
        in_specs=[part(0), part(1), part(2), part(3),
                  pl.BlockSpec((ML_CHUNK, LANES), lambda b, c: (b * nc + c, 0)),
                  pl.BlockSpec((ML_CONV, 2 * ML_PAD_WIDTH), lambda b, c: (0, 0)),
                  pl.BlockSpec((1, 2 * ML_PAD_WIDTH), lambda b, c: (0, 0))],
        out_specs=pl.BlockSpec((ML_CHUNK, ML_PAD_WIDTH), lambda b, c: (b * nc + c, 0)),
        out_shape=jax.ShapeDtypeStruct((n, ML_PAD_WIDTH), BF16),
        scratch_shapes=[pltpu.VMEM((ML_CHUNK + 8, ML_PAD_WIDTH), F32),
                        pltpu.VMEM((ML_CHUNK + 8, ML_PAD_WIDTH), F32),
                        pltpu.VMEM((ML_HEADS, ML_HEAD_PAD, ML_HEAD_PAD), F32),
                        pltpu.VMEM((ML_HEADS, 8, LANES), F32)],
        compiler_params=_cparams("parallel", "arbitrary"),
        name="mlstm",
    )(ml, ml, ml, ml, gates, conv_w, conv_b)


def _mixer_out_kernel(x_ref, o0_ref, l0_ref, o1_ref, l1_ref, o2_ref, l2_ref, sg_ref, hc_ref, gp_ref,
                      sgg_ref, sgb_ref, sgw_ref, sgbias_ref, wa_ref, wb_ref, wc_ref, wo_ref, lng_ref, lnb_ref,
                      xf_ref, xb_ref, unperm, *, alpha):
    tm = x_ref.shape[0]

    def token_order(ref, slot):
        dil = ref.shape[0]
        if dil == 1:
            return ref[0].astype(F32)
        slabs = ATT_GROUP_WIDTH // LANES
        for r in range(dil):
            for c in range(slabs):
                unperm[slabs * slot + c, pl.ds(r, tm // dil, stride=dil), :] = (
                    ref[r, :, c * LANES:(c + 1) * LANES].astype(F32))
        return jnp.concatenate([unperm[slabs * slot + c] for c in range(slabs)], axis=1)

    l0, l1, l2 = token_order(l0_ref, 0), token_order(l1_ref, 1), token_order(l2_ref, 2)
    lmax = jnp.maximum(jnp.maximum(l0, l1), l2)
    e0, e1, e2 = jnp.exp(l0 - lmax), jnp.exp(l1 - lmax), jnp.exp(l2 - lmax)
    inv = 1.0 / (e0 + e1 + e2)
    y_a = ((e0 * inv) * token_order(o0_ref, 3) + (e1 * inv) * token_order(o1_ref, 4)
           + (e2 * inv) * token_order(o2_ref, 5))

    u = _gelu_tanh(sg_ref[:, 0:SG_WIDTH].astype(F32))
    v = _gelu_tanh(sg_ref[:, SG_WIDTH:2 * SG_WIDTH].astype(F32))
    v = _layer_norm(v, sgg_ref[...], sgb_ref[...]).astype(BF16)
    row = lax.broadcasted_iota(jnp.int32, (SG_CHUNK, SG_CHUNK), 0)
    colm = lax.broadcasted_iota(jnp.int32, (SG_CHUNK, SG_CHUNK), 1)
    causal = row >= colm
    group_dim = SG_WIDTH // SG_GROUPS
    chunks = []
    for c in range(tm // SG_CHUNK):
        rows = slice(c * SG_CHUNK, (c + 1) * SG_CHUNK)
        parts = []
        for g in range(SG_GROUPS):
            w = jnp.where(causal, sgw_ref[g], 0.0).astype(BF16)
            mixed = _dot(w, v[rows, g * group_dim:(g + 1) * group_dim]) + sgbias_ref[:, g:g + 1]
            parts.append(mixed)
        chunks.append(jnp.concatenate(parts, axis=1))
    y_b = u * jnp.concatenate(chunks, axis=0)

    def gate(i):
        return jax.nn.sigmoid(gp_ref[:, i * D_MODEL:(i + 1) * D_MODEL].astype(F32))

    z = (gate(0) * _dot(y_a.astype(BF16), wa_ref[...])
         + gate(1) * _dot(y_b.astype(BF16), wb_ref[...])
         + gate(2) * _dot(hc_ref[...], wc_ref[...]))
    mix = _dot(z.astype(BF16), wo_ref[...])
    out = _layer_norm(alpha * x_ref[...] + mix, lng_ref[...], lnb_ref[...])
    xf_ref[...] = out
    xb_ref[...] = out.astype(BF16)


def _mixer_out(x, att_outs, sg, hc, gpre, p, alpha, seq, tm=256):
    n = x.shape[0]
    tiles_per_seq = seq // tm

    def rows(width):
        return pl.BlockSpec((tm, width), lambda i: (i, 0))

    def whole(a):
        return pl.BlockSpec(a.shape, lambda i: (0,) * a.ndim)

    def residue_major(dil):
        return pl.BlockSpec((None, dil, tm // dil, ATT_GROUP_WIDTH),
                            lambda i: (i // tiles_per_seq, 0, i % tiles_per_seq, 0))

    att_specs, att_args = [], []
    for (o, lse), (_, dil) in zip(att_outs, ATT_GROUPS):
        att_specs += [residue_major(dil)] * 2
        att_args += [o, lse]
    weights = [p["sg_ln_g"], p["sg_ln_b"], p["sg_w"], p["sg_bias_t"], p["w_br_a"], p["w_br_b"], p["w_br_c"],
               p["w_out"], p["ln_g0"], p["ln_b0"]]
    return pl.pallas_call(
        functools.partial(_mixer_out_kernel, alpha=alpha),
        grid=(n // tm,),
        in_specs=[rows(D_MODEL)] + att_specs
                 + [rows(2 * SG_WIDTH), rows(ML_PAD_WIDTH), rows(N_BRANCH * D_MODEL)]
                 + [whole(w) for w in weights],
        out_specs=[rows(D_MODEL), rows(D_MODEL)],
        out_shape=[jax.ShapeDtypeStruct((n, D_MODEL), F32), jax.ShapeDtypeStruct((n, D_MODEL), BF16)],
        scratch_shapes=[pltpu.VMEM((2 * len(ATT_GROUPS) * ATT_GROUP_WIDTH // LANES, tm, LANES), F32)],
        compiler_params=_cparams("parallel"),
        name="mixer_out",
    )(x, *att_args, sg, hc, gpre, *weights)


def _ffn_kernel(xf_ref, xb_ref, w1_ref, w3_ref, w2_ref, lng_ref, lnb_ref, of_ref, ob_ref, acc, *, alpha):
    f = pl.program_id(1)
    xb = xb_ref[...]
    h1 = _dot(xb, w1_ref[...])
    h3 = _dot(xb, w3_ref[...])
    part = _dot((h1 * jax.nn.sigmoid(h1) * h3).astype(BF16), w2_ref[...])

    @pl.when(f == 0)
    def _():
        acc[...] = part

    @pl.when(f > 0)
    def _():
        acc[...] += part

    @pl.when(f == pl.num_programs(1) - 1)
    def _():
        out = _layer_norm(alpha * xf_ref[...] + acc[...], lng_ref[...], lnb_ref[...])
        of_ref[...] = out
        ob_ref[...] = out.astype(BF16)


def _ffn(xf, xb, w1, w3, w2, ln_g, ln_b, alpha, tm=512, tf=1408):
    n = xf.shape[0]
    return pl.pallas_call(
        functools.partial(_ffn_kernel, alpha=alpha),
        grid=(n // tm, D_FF // tf),
        in_specs=[pl.BlockSpec((tm, D_MODEL), lambda i, f: (i, 0)),
                  pl.BlockSpec((tm, D_MODEL), lambda i, f: (i, 0)),
                  pl.BlockSpec((D_MODEL, tf), lambda i, f: (0, f)),
                  pl.BlockSpec((D_MODEL, tf), lambda i, f: (0, f)),
                  pl.BlockSpec((tf, D_MODEL), lambda i, f: (f, 0)),
                  pl.BlockSpec((1, D_MODEL), lambda i, f: (0, 0)),
                  pl.BlockSpec((1, D_MODEL), lambda i, f: (0, 0))],
        out_specs=[pl.BlockSpec((tm, D_MODEL), lambda i, f: (i, 0))] * 2,
        out_shape=[jax.ShapeDtypeStruct((n, D_MODEL), F32), jax.ShapeDtypeStruct((n, D_MODEL), BF16)],
        scratch_shapes=[pltpu.VMEM((tm, D_MODEL), F32)],
        compiler_params=_cparams("parallel", "arbitrary"),
        name="ffn",
    )(xf, xb, w1, w3, w2, ln_g, ln_b)


def _router_kernel(x_ref, w_ref, b_ref, info_ref):
    xh, xm, xl = _split3(x_ref[...])
    wh, wm, wl = _split3(w_ref[...])
    logits = (_dot(xh, wh) + (_dot(xh, wm) + _dot(xm, wh))
              + (_dot(xh, wl) + _dot(xm, wm) + _dot(xl, wh))) + b_ref[...]
    lane = lax.broadcasted_iota(jnp.int32, logits.shape, 1)
    neg_inf = jnp.float32(-jnp.inf)
    logits = jnp.where(lane < N_EXPERTS, logits, neg_inf)
    v1 = jnp.max(logits, axis=-1, keepdims=True)
    i1 = jnp.min(jnp.where(logits == v1, lane, LANES), axis=-1, keepdims=True)
    rest = jnp.where(lane == i1, neg_inf, logits)
    v2 = jnp.max(rest, axis=-1, keepdims=True)
    i2 = jnp.min(jnp.where(rest == v2, lane, LANES), axis=-1, keepdims=True)
    e = jnp.exp(v2 - v1)
    inv = 1.0 / (1.0 + e)
    info_ref[...] = jnp.where(lane == 0, i1.astype(F32),
                              jnp.where(lane == 1, i2.astype(F32),
                                        jnp.where(lane == 2, inv, jnp.where(lane == 3, e * inv, 0.0))))


def _router(xf, router_w, router_b, tm=1024):
    n = xf.shape[0]
    w = jnp.pad(router_w, ((0, 0), (0, LANES - N_EXPERTS)))
    b = jnp.pad(router_b, (0, LANES - N_EXPERTS))[None, :]
    return pl.pallas_call(
        _router_kernel,
        grid=(n // tm,),
        in_specs=[pl.BlockSpec((tm, D_MODEL), lambda i: (i, 0)),
                  pl.BlockSpec((D_MODEL, LANES), lambda i: (0, 0)),
                  pl.BlockSpec((1, LANES), lambda i: (0, 0))],
        out_specs=pl.BlockSpec((tm, LANES), lambda i: (i, 0)),
        out_shape=jax.ShapeDtypeStruct((n, LANES), F32),
        compiler_params=_cparams("parallel"),
        name="router",
    )(xf, w, b)


def _routing_plan(info, n):
    tile = EXPERT_TILE
    n_tiles = TOP_K * n // tile + N_EXPERTS
    expert = info[:, 0:TOP_K].astype(jnp.int32).reshape(-1)
    onehot = (expert[:, None] == jnp.arange(N_EXPERTS, dtype=jnp.int32)[None, :]).astype(jnp.int32)
    running = jnp.cumsum(onehot, axis=0)
    rank = jnp.sum(onehot * running, axis=1) - 1
    counts = running[-1]
    padded = ((counts + tile - 1) // tile) * tile
    ends = jnp.cumsum(padded)
    starts = ends - padded
    dest = jnp.sum(onehot * starts[None, :], axis=1) + rank
    used = ends[-1] // tile
    tile_start = jnp.arange(n_tiles, dtype=jnp.int32) * tile
    tile_expert = jnp.sum((tile_start[:, None] >= ends[None, :]).astype(jnp.int32), axis=1)
    last_expert = jnp.sum((tile_start[used - 1] >= ends).astype(jnp.int32))
    tile_expert = jnp.minimum(tile_expert, last_expert)
    return dest.astype(jnp.int32), tile_expert.astype(jnp.int32), used.astype(jnp.int32).reshape(1), n_tiles


def _row_copy(src, src_row, dst, dst_row, sem):
    return pltpu.make_async_copy(src.at[pl.ds(src_row, 1), :], dst.at[pl.ds(dst_row, 1), :], sem)


def _dispatch_kernel(dest_ref, x_ref, zero_ref, xs_ref, packed, sem):
    del zero_ref
    tm = x_ref.shape[0]
    packed[...] = _pack_rows(x_ref[...].astype(F32))

    def start(t, carry):
        for c in range(TOP_K):
            _row_copy(packed, t, xs_ref, dest_ref[0, 0, TOP_K * t + c], sem).start()
        return carry

    lax.fori_loop(0, tm, start, 0)

    def wait(t, carry):
        for c in range(TOP_K):
            _row_copy(packed, 0, xs_ref, 0, sem).wait()
        return carry

    lax.fori_loop(0, tm, wait, 0)


def _dispatch(xb, dest, n_slots, tm=512):
    n = xb.shape[0]
    dest3 = dest.reshape(n // tm, 1, TOP_K * tm)
    zeros = jnp.zeros((n_slots, PACKED), U32)
    return pl.pallas_call(
        _dispatch_kernel,
        grid=(n // tm,),
        in_specs=[pl.BlockSpec((1, 1, TOP_K * tm), lambda i: (i, 0, 0), memory_space=pltpu.SMEM),
                  pl.BlockSpec((tm, D_MODEL), lambda i: (i, 0)),
                  pl.BlockSpec(memory_space=pl.ANY)],
        out_specs=pl.BlockSpec(memory_space=pl.ANY),
        out_shape=jax.ShapeDtypeStruct((n_slots, PACKED), U32),
        scratch_shapes=[pltpu.VMEM((tm, PACKED), U32), pltpu.SemaphoreType.DMA],
        input_output_aliases={2: 0},
        compiler_params=_cparams("arbitrary"),
        name="moe_dispatch",
    )(dest3, xb, zeros)


def _experts_kernel(te_ref, used_ref, xs_ref, w1_ref, w3_ref, w2_ref, ys_ref, prod, *, tf):
    del te_ref
    i = pl.program_id(0)

    @pl.when(i < used_ref[0])
    def _():
        x = _unpack_rows(xs_ref[...]).astype(BF16)
        for c in range(D_FF // tf):
            cols = slice(c * tf, (c + 1) * tf)
            h1 = _dot(x, w1_ref[:, cols])
            h3 = _dot(x, w3_ref[:, cols])
            prod[:, cols] = (h1 * jax.nn.sigmoid(h1) * h3).astype(BF16)
        ys_ref[...] = _pack_rows(_dot(prod[...], w2_ref[...]))

    @pl.when(i >= used_ref[0])
    def _():
        ys_ref[...] = jnp.zeros(ys_ref.shape, U32)


def _experts(xs, tile_expert, used, w1, w3, w2, n_tiles, tf=1408):
    tile = EXPERT_TILE
    grid_spec = pltpu.PrefetchScalarGridSpec(
        num_scalar_prefetch=2,
        grid=(n_tiles,),
        in_specs=[pl.BlockSpec((tile, PACKED), lambda i, te, nu: (i, 0)),
                  pl.BlockSpec((None, D_MODEL, D_FF), lambda i, te, nu: (te[i], 0, 0)),
                  pl.BlockSpec((None, D_MODEL, D_FF), lambda i, te, nu: (te[i], 0, 0)),
                  pl.BlockSpec((None, D_FF, D_MODEL), lambda i, te, nu: (te[i], 0, 0))],
        out_specs=pl.BlockSpec((tile, PACKED), lambda i, te, nu: (i, 0)),
        scratch_shapes=[pltpu.VMEM((tile, D_FF), BF16)],
    )
    return pl.pallas_call(
        functools.partial(_experts_kernel, tf=tf),
        grid_spec=grid_spec,
        out_shape=jax.ShapeDtypeStruct((n_tiles * tile, PACKED), U32),
        compiler_params=_cparams("arbitrary"),
        name="moe_experts",
    )(tile_expert, used, xs, w1, w3, w2)


def _combine_kernel(dest_ref, xf_ref, info_ref, ys_ref, lng_ref, lnb_ref, of_ref, got, sem, *, alpha):
    tm = xf_ref.shape[0]

    def start(t, carry):
        for c in range(TOP_K):
            _row_copy(ys_ref, dest_ref[0, 0, TOP_K * t + c], got.at[c], t, sem).start()
        return carry

    lax.fori_loop(0, tm, start, 0)

    def wait(t, carry):
        for c in range(TOP_K):
            _row_copy(ys_ref, 0, got.at[c], 0, sem).wait()
        return carry

    lax.fori_loop(0, tm, wait, 0)
    info = info_ref[...]
    mix = info[:, 2:3] * _unpack_rows(got[0]) + info[:, 3:4] * _unpack_rows(got[1])
    of_ref[...] = _layer_norm(alpha * xf_ref[...] + mix, lng_ref[...], lnb_ref[...])


def _combine(xf, info, ys, dest, ln_g, ln_b, alpha, tm=512):
    n = xf.shape[0]
    dest3 = dest.reshape(n // tm, 1, TOP_K * tm)
    return pl.pallas_call(
        functools.partial(_combine_kernel, alpha=alpha),
        grid=(n // tm,),
        in_specs=[pl.BlockSpec((1, 1, TOP_K * tm), lambda i: (i, 0, 0), memory_space=pltpu.SMEM),
                  pl.BlockSpec((tm, D_MODEL), lambda i: (i, 0)),
                  pl.BlockSpec((tm, LANES), lambda i: (i, 0)),
                  pl.BlockSpec(memory_space=pl.ANY),
                  pl.BlockSpec((1, D_MODEL), lambda i: (0, 0)),
                  pl.BlockSpec((1, D_MODEL), lambda i: (0, 0))],
        out_specs=pl.BlockSpec((tm, D_MODEL), lambda i: (i, 0)),
        out_shape=jax.ShapeDtypeStruct((n, D_MODEL), F32),
        scratch_shapes=[pltpu.VMEM((TOP_K, tm, PACKED), U32), pltpu.SemaphoreType.DMA],
        compiler_params=_cparams("arbitrary"),
        name="moe_combine",
    )(dest3, xf, info, ys, ln_g, ln_b)


def _moe(xf, xb, router_w, router_b, w1, w3, w2, ln_g, ln_b, alpha):
    n = xf.shape[0]
    info = _router(xf, router_w, router_b)
    dest, tile_expert, used, n_tiles = _routing_plan(info, n)
    xs = _dispatch(xb, dest, n_tiles * EXPERT_TILE)
    ys = _experts(xs, tile_expert, used, w1, w3, w2, n_tiles)
    return _combine(xf, info, ys, dest, ln_g, ln_b, alpha)


def _pad_heads(w):
    lead = w.shape[:-1]
    w = w.reshape(*lead, ML_HEADS, ML_HEAD_DIM)
    w = jnp.pad(w, [(0, 0)] * len(lead) + [(0, 0), (0, ML_HEAD_PAD - ML_HEAD_DIM)])
    return w.reshape(*lead, ML_PAD_WIDTH)


def _group_qkv(w, group):
    gw = ATT_GROUP_WIDTH
    return jnp.concatenate([w[..., part * ATT_WIDTH + group * gw:part * ATT_WIDTH + (group + 1) * gw]
                            for part in range(3)], axis=-1)


def _layer_params(layer, w_in, b_in, conv_w, conv_b, sg_ln_g, sg_ln_b, sg_w, sg_b,
                  w_br_a, w_br_b, w_br_c, w_out, ln_g, ln_b):
    w, b = w_in[layer], b_in[layer]
    o_att, o_sg, o_qk, o_v, o_o, o_i, o_g = 0, 2304, 3840, 5376, 6144, 6912, 6920
    ones_col = jnp.zeros((ML_HEADS, ML_HEAD_PAD), F32).at[:, ML_HEAD_DIM].set(1.0).reshape(ML_PAD_WIDTH)
    w_ml = jnp.concatenate([_pad_heads(w[:, o_qk:o_qk + ML_WIDTH]), _pad_heads(w[:, o_qk + ML_WIDTH:o_v]),
                            _pad_heads(w[:, o_v:o_o]), _pad_heads(w[:, o_o:o_i])], axis=1)
    b_ml = jnp.concatenate([_pad_heads(b[o_qk:o_qk + ML_WIDTH]), _pad_heads(b[o_qk + ML_WIDTH:o_v]),
                            _pad_heads(b[o_v:o_o]) + ones_col, _pad_heads(b[o_o:o_i])])
    cw, cb = conv_w[layer], conv_b[layer]
    w_att, b_att = w[:, o_att:o_sg], b[o_att:o_sg]
    return dict(
        w_att=[_group_qkv(w_att, g).astype(BF16) for g in range(len(ATT_GROUPS))],
        b_att=[_group_qkv(b_att, g)[None, :] for g in range(len(ATT_GROUPS))],
        w_sg=w[:, o_sg:o_qk].astype(BF16), b_sg=b[None, o_sg:o_qk],
        w_ml=w_ml.astype(BF16), b_ml=b_ml[None, :],
        w_if=jnp.pad(w[:, o_i:o_g], ((0, 0), (0, LANES - 2 * ML_HEADS))).astype(BF16),
        b_if=jnp.pad(b[o_i:o_g], (0, LANES - 2 * ML_HEADS))[None, :],
        w_g=w[:, o_g:].astype(BF16), b_g=b[None, o_g:],
        conv_w=jnp.concatenate([_pad_heads(cw[:, :ML_WIDTH]), _pad_heads(cw[:, ML_WIDTH:])], axis=1),
        conv_b=jnp.concatenate([_pad_heads(cb[:ML_WIDTH]), _pad_heads(cb[ML_WIDTH:])])[None, :],
        sg_ln_g=sg_ln_g[layer][None, :], sg_ln_b=sg_ln_b[layer][None, :],
        sg_w=sg_w[layer], sg_bias_t=sg_b[layer].T,
        w_br_a=w_br_a[layer].astype(BF16), w_br_b=w_br_b[layer].astype(BF16),
        w_br_c=_pad_heads(w_br_c[layer].T).T.astype(BF16),
        w_out=w_out[layer].astype(BF16),
        ln_g0=ln_g[layer, 0][None, :], ln_b0=ln_b[layer, 0][None, :],
        ln_g1=ln_g[layer, 1][None, :], ln_b1=ln_b[layer, 1][None, :],
    )


def kernel(x, positions, w_in, b_in, conv_w, conv_b, sg_ln_g, sg_ln_b, sg_w, sg_b, w_br_a, w_br_b, w_br_c, w_out,
           ln_g, ln_b, ffn_w1, ffn_w3, ffn_w2, router_w, router_b, moe_w1, moe_w3, moe_w2):
    batch, seq, _ = x.shape
    n = batch * seq
    depth = w_in.shape[0]
    alpha = (2.0 * depth) ** 0.25
    cos, sin = _rope_tables(positions)
    xf = x.reshape(n, D_MODEL)
    xb = xf.astype(BF16)
    for layer in range(depth):
        p = _layer_params(layer, w_in, b_in, conv_w, conv_b, sg_ln_g, sg_ln_b, sg_w, sg_b,
                          w_br_a, w_br_b, w_br_c, w_out, ln_g, ln_b)
        sg = _linear(xb, p["w_sg"], p["b_sg"], BF16, 1024, 768)
        ml = _linear(xb, p["w_ml"], p["b_ml"], BF16, 1024, 1024)
        gates_if = _linear(xb, p["w_if"], p["b_if"], F32, 1024, LANES)
        gpre = _linear(xb, p["w_g"], p["b_g"], BF16, 1024, 1024)
        att_outs = []
        for g, (_, dil) in enumerate(ATT_GROUPS):
            qkv = _linear_dilated(xb, p["w_att"][g], p["b_att"][g], dil, batch, seq)
            att_outs.append(_attention_group(qkv, cos, sin, g, batch, seq))
        hc = _mlstm(ml, gates_if, p["conv_w"], p["conv_b"], batch, seq)
        xf, xb = _mixer_out(xf, att_outs, sg, hc, gpre, p, alpha, seq)
        j = layer // 2
        if layer % 2 == 0:
            xf, xb = _ffn(xf, xb, ffn_w1[j].astype(BF16), ffn_w3[j].astype(BF16), ffn_w2[j].astype(BF16),
                          p["ln_g1"], p["ln_b1"], alpha)
        else:
            xf = _moe(xf, xb, router_w[j], router_b[j], moe_w1[j].astype(BF16), moe_w3[j].astype(BF16),
                      moe_w2[j].astype(BF16), p["ln_g1"], p["ln_b1"], alpha)
            xb = xf.astype(BF16)
    return xf.reshape(batch, seq, D_MODEL)
```

```python
import functools

import numpy as np
import jax
import jax.numpy as jnp
from jax import lax
from jax.experimental import pallas as pl
from jax.experimental.pallas import tpu as pltpu

F32 = jnp.float32
BF16 = jnp.bfloat16

D_MODEL = 1024
ATT_GROUPS = ((128, 1), (512, 4), (2048, 16))
ATT_HEADS_PER_GROUP = 4
ATT_HEAD_DIM = 64
ATT_WIDTH = 768
ATT_GROUP_WIDTH = ATT_HEADS_PER_GROUP * ATT_HEAD_DIM
ROPE_THETA = 10000.0
SG_CHUNK = 128
SG_GROUPS = 6
SG_WIDTH = 768
ML_HEADS = 4
ML_HEAD_DIM = 192
ML_HEAD_PAD = 256
ML_WIDTH = 768
ML_PAD_WIDTH = ML_HEADS * ML_HEAD_PAD
ML_CHUNK = 128
ML_CONV = 4
N_BRANCH = 3
D_FF = 2816
N_EXPERTS = 8
TOP_K = 2
LN_EPS = 1e-5
BLOCK = 128
LANES = 128
VMEM_LIMIT = 56 * 1024 * 1024
EXPERT_TILE = 512
DMA_ISSUE_UNROLL = 8


def _cparams(*sem):
    return pltpu.CompilerParams(dimension_semantics=sem, vmem_limit_bytes=VMEM_LIMIT)


def _layer_norm(x, g, b):
    mu = jnp.mean(x, axis=-1, keepdims=True)
    xc = x - mu
    var = jnp.mean(xc * xc, axis=-1, keepdims=True)
    return xc * lax.rsqrt(var + LN_EPS) * g + b


def _gelu_tanh(x):
    return 0.5 * x * (1.0 + jnp.tanh(np.sqrt(2.0 / np.pi) * (x + 0.044715 * (x * x * x))))


def _sigmoid(x):
    return 0.5 * (1.0 + jnp.tanh(0.5 * x))


def _dot(a, b):
    return jnp.dot(a, b, preferred_element_type=F32)


def _dot_nt(a, b):
    return lax.dot_general(a, b, (((1,), (1,)), ((), ())), preferred_element_type=F32)


def _dot_tn(a, b):
    return lax.dot_general(a, b, (((0,), (0,)), ((), ())), preferred_element_type=F32)


def _split3(x):
    hi = x.astype(BF16)
    r1 = x - hi.astype(F32)
    mid = r1.astype(BF16)
    lo = (r1 - mid.astype(F32)).astype(BF16)
    return hi, mid, lo


def _linear_kernel(x_ref, w_ref, b_ref, o_ref):
    o_ref[...] = (_dot(x_ref[...], w_ref[...]) + b_ref[...]).astype(o_ref.dtype)


def _linear(x, w, b, out_dtype, tm, tn):
    m, k = x.shape
    n = w.shape[1]
    return pl.pallas_call(
        _linear_kernel,
        grid=(m // tm, n // tn),
        in_specs=[pl.BlockSpec((tm, k), lambda i, j: (i, 0)),
                  pl.BlockSpec((k, tn), lambda i, j: (0, j)),
                  pl.BlockSpec((1, tn), lambda i, j: (0, j))],
        out_specs=pl.BlockSpec((tm, tn), lambda i, j: (i, j)),
        out_shape=jax.ShapeDtypeStruct((m, n), out_dtype),
        compiler_params=_cparams("parallel", "parallel"),
        name="linear",
    )(x, w, b)


def _linear_dilated_kernel(x_ref, w_ref, b_ref, o_ref, acc, *, dil):
    res = _dot(x_ref[...], w_ref[...]) + b_ref[...]
    slabs, rows, _ = acc.shape
    for c in range(slabs):
        acc[c] = res[:, c * LANES:(c + 1) * LANES]
    sub = rows // dil
    for r in range(dil):
        for c in range(slabs):
            o_ref[r, :, c * LANES:(c + 1) * LANES] = acc[c, pl.ds(r, sub, stride=dil), :].astype(o_ref.dtype)


def _linear_dilated(x, w, b, dil, batch, seq):
    k = x.shape[1]
    n = w.shape[1]
    tn = ATT_GROUP_WIDTH
    sub = seq // dil
    return pl.pallas_call(
        functools.partial(_linear_dilated_kernel, dil=dil),
        grid=(batch, n // tn),
        in_specs=[pl.BlockSpec((seq, k), lambda i, j: (i, 0)),
                  pl.BlockSpec((k, tn), lambda i, j: (0, j)),
                  pl.BlockSpec((1, tn), lambda i, j: (0, j))],
        out_specs=pl.BlockSpec((None, dil, sub, tn), lambda i, j: (i, 0, 0, j)),
        out_shape=jax.ShapeDtypeStruct((batch, dil, sub, n), BF16),
        scratch_shapes=[pltpu.VMEM((tn // LANES, seq, LANES), F32)],
        compiler_params=_cparams("parallel", "parallel"),
        name=f"linear_dil{dil}",
    )(x, w, b)


def _rope_table_kernel(pos_ref, freq_ref, sign_ref, cos_ref, sin_ref):
    ang = pos_ref[...] * freq_ref[...]
    cos_ref[...] = jnp.cos(ang)
    sin_ref[...] = jnp.sin(ang) * sign_ref[...]


def _rope_tables(positions):
    n = positions.size
    half = ATT_HEAD_DIM // 2
    freqs = ROPE_THETA ** (-jnp.arange(half, dtype=F32) * (2.0 / ATT_HEAD_DIM))
    lane = np.arange(LANES)
    freq_row = freqs[lane % half][None, :]
    sign_row = jnp.asarray(np.where(lane % ATT_HEAD_DIM < half, -1.0, 1.0), F32)[None, :]
    pos = positions.astype(F32).reshape(n, 1)
    tm = 2048
    return pl.pallas_call(
        _rope_table_kernel,
        grid=(n // tm,),
        in_specs=[pl.BlockSpec((tm, 1), lambda i: (i, 0)),
                  pl.BlockSpec((1, LANES), lambda i: (0, 0)),
                  pl.BlockSpec((1, LANES), lambda i: (0, 0))],
        out_specs=[pl.BlockSpec((tm, LANES), lambda i: (i, 0))] * 2,
        out_shape=[jax.ShapeDtypeStruct((n, LANES), F32)] * 2,
        compiler_params=_cparams("parallel"),
        name="rope_tables",
    )(pos, freq_row, sign_row)


def _attn_kernel(q_ref, k_ref, v_ref, cos_ref, sin_ref, o_ref, lse_ref, qs, ks, vs, *, sub, dil):
    nb = sub // BLOCK
    res = pl.program_id(1)
    lane = lax.broadcasted_iota(jnp.int32, (BLOCK, LANES), 1)
    first_half = (lane % ATT_HEAD_DIM) < (ATT_HEAD_DIM // 2)
    low_head = lane < ATT_HEAD_DIM

    def rope(x, c, s):
        partner = jnp.where(first_half, pltpu.roll(x, LANES - 32, 1), pltpu.roll(x, 32, 1))
        return x * c + partner * s

    ks[0:BLOCK, :] = jnp.zeros((BLOCK, ATT_GROUP_WIDTH), BF16)
    vs[0:BLOCK, :] = jnp.zeros((BLOCK, ATT_GROUP_WIDTH), BF16)

    def prep(n, carry):
        r0 = pl.multiple_of(n * BLOCK, BLOCK)
        if dil == 1:
            c = cos_ref[pl.ds(r0, BLOCK), :]
            s = sin_ref[pl.ds(r0, BLOCK), :]
        else:
            c = cos_ref[pl.ds(r0 * dil + res, BLOCK, stride=dil), :]
            s = sin_ref[pl.ds(r0 * dil + res, BLOCK, stride=dil), :]
        for pair in range(2):
            cols = slice(pair * LANES, (pair + 1) * LANES)
            q = rope(q_ref[pl.ds(r0, BLOCK), cols].astype(F32), c, s) * (ATT_HEAD_DIM ** -0.5)
            qs[pl.ds(r0, BLOCK), (2 * pair) * LANES:(2 * pair + 1) * LANES] = jnp.where(low_head, q, 0.0).astype(BF16)
            qs[pl.ds(r0, BLOCK), (2 * pair + 1) * LANES:(2 * pair + 2) * LANES] = jnp.where(low_head, 0.0, q).astype(BF16)
            k = rope(k_ref[pl.ds(r0, BLOCK), cols].astype(F32), c, s)
            ks[pl.ds(r0 + BLOCK, BLOCK), cols] = k.astype(BF16)
        vs[pl.ds(r0 + BLOCK, BLOCK), :] = v_ref[pl.ds(r0, BLOCK), :]
        return carry

    lax.fori_loop(0, nb, prep, 0)

    qi = lax.broadcasted_iota(jnp.int32, (BLOCK, 2 * BLOCK), 0)
    ki = lax.broadcasted_iota(jnp.int32, (BLOCK, 2 * BLOCK), 1)
    dist = qi + BLOCK - ki
    band = (dist >= 0) & (dist <= BLOCK)
    neg_inf = jnp.float32(-jnp.inf)
    bias_rest = jnp.where(band, 0.0, neg_inf)
    bias_first = jnp.where(band & (ki >= BLOCK), 0.0, neg_inf)

    def block(n, bias):
        r0 = pl.multiple_of(n * BLOCK, BLOCK)
        for pair in range(2):
            cols = slice(pair * LANES, (pair + 1) * LANES)
            kw = ks[pl.ds(r0, 2 * BLOCK), cols]
            vw = vs[pl.ds(r0, 2 * BLOCK), cols]
            outs = []
            for hh in range(2):
                hd = 2 * pair + hh
                qh = qs[pl.ds(r0, BLOCK), hd * LANES:(hd + 1) * LANES]
                s = _dot_nt(qh, kw) + bias
                m = jnp.max(s, axis=-1, keepdims=True)
                p = jnp.exp(s - m)
                den = jnp.sum(p, axis=-1, keepdims=True)
                o = _dot(p.astype(BF16), vw) * (1.0 / den)
                outs.append((o, m + jnp.log(den)))
            o_ref[pl.ds(r0, BLOCK), cols] = jnp.where(low_head, outs[0][0], outs[1][0]).astype(o_ref.dtype)
            lse_ref[pl.ds(r0, BLOCK), cols] = jnp.where(low_head, outs[0][1], outs[1][1])

    block(0, bias_first)

    def body(n, carry):
        block(n, bias_rest)
        return carry

    lax.fori_loop(1, nb, body, 0, unroll=2)


def _attention_group(qkv, cos, sin, group, batch, seq):
    window, dil = ATT_GROUPS[group]
    assert window // dil == BLOCK
    sub = seq // dil
    assert sub % BLOCK == 0

    def part(p):
        return pl.BlockSpec((None, None, sub, ATT_GROUP_WIDTH), lambda b, r: (b, r, 0, p))

    tab = pl.BlockSpec((seq, LANES), lambda b, r: (b, 0))
    out = pl.BlockSpec((None, None, sub, ATT_GROUP_WIDTH), lambda b, r: (b, r, 0, 0))
    return pl.pallas_call(
        functools.partial(_attn_kernel, sub=sub, dil=dil),
        grid=(batch, dil),
        in_specs=[part(0), part(1), part(2), tab, tab],
        out_specs=[out, out],
        out_shape=[jax.ShapeDtypeStruct((batch, dil, sub, ATT_GROUP_WIDTH), BF16),
                   jax.ShapeDtypeStruct((batch, dil, sub, ATT_GROUP_WIDTH), F32)],
        scratch_shapes=[pltpu.VMEM((sub, ATT_HEADS_PER_GROUP * LANES), BF16),
                        pltpu.VMEM((sub + BLOCK, ATT_GROUP_WIDTH), BF16),
                        pltpu.VMEM((sub + BLOCK, ATT_GROUP_WIDTH), BF16)],
        compiler_params=_cparams("parallel", "parallel"),
        name=f"attention_g{group}",
    )(qkv, qkv, qkv, cos, sin)


def _mlstm_kernel(q_ref, k_ref, v_ref, og_ref, g_ref, cw_ref, cb_ref, out_ref, xq, xk, ct, m_s):
    halo = 8
    taps = ML_CONV

    @pl.when(pl.program_id(1) == 0)
    def _():
        xq[0:halo, :] = jnp.zeros((halo, ML_PAD_WIDTH), F32)
        xk[0:halo, :] = jnp.zeros((halo, ML_PAD_WIDTH), F32)
        ct[...] = jnp.zeros(ct.shape, F32)
        m_s[...] = jnp.zeros(m_s.shape, F32)

    xq[halo:halo + ML_CHUNK, :] = q_ref[...].astype(F32)
    xk[halo:halo + ML_CHUNK, :] = k_ref[...].astype(F32)

    def conv_silu(xbuf, col0):
        acc = cb_ref[:, col0:col0 + ML_PAD_WIDTH]
        for j in range(taps):
            start = halo - (taps - 1) + j
            acc = acc + xbuf[start:start + ML_CHUNK, :] * cw_ref[j:j + 1, col0:col0 + ML_PAD_WIDTH]
        return acc * _sigmoid(acc)

    qc = conv_silu(xq, 0).astype(BF16)
    kc = (conv_silu(xk, ML_PAD_WIDTH) * (ML_HEAD_DIM ** -0.5)).astype(BF16)
    xq[0:halo, :] = xq[ML_CHUNK:ML_CHUNK + halo, :]
    xk[0:halo, :] = xk[ML_CHUNK:ML_CHUNK + halo, :]

    gates = g_ref[...]
    log_f = -(jnp.maximum(-gates, 0.0) + jnp.log1p(jnp.exp(-jnp.abs(gates))))
    row = lax.broadcasted_iota(jnp.int32, (ML_CHUNK, ML_CHUNK), 0)
    colm = lax.broadcasted_iota(jnp.int32, (ML_CHUNK, ML_CHUNK), 1)
    causal = row >= colm
    tri = jnp.where(causal, 1.0, 0.0).astype(BF16)
    hi, mid, lo = _split3(log_f)
    bcum = _dot(tri, hi) + _dot(tri, mid) + _dot(tri, lo)
    bcum_t = bcum.T
    gates_t = gates.T
    neg_inf = jnp.float32(-jnp.inf)

    for h in range(ML_HEADS):
        cols = slice(h * ML_HEAD_PAD, (h + 1) * ML_HEAD_PAD)
        b_c = bcum[:, ML_HEADS + h:ML_HEADS + h + 1]
        b_r = bcum_t[ML_HEADS + h:ML_HEADS + h + 1, :]
        i_c = gates[:, h:h + 1]
        i_r = gates_t[h:h + 1, :]
        m_prev = m_s[h, 0:1, 0:1]
        log_d = jnp.where(causal, b_c - b_r + i_r, neg_inf)
        m_inter = b_c + m_prev
        m = jnp.maximum(m_inter, jnp.max(log_d, axis=-1, keepdims=True))
        qh = qc[:, cols]
        kh = kc[:, cols]
        vh = v_ref[:, cols]
        sc = _dot_nt(qh, kh) * jnp.exp(log_d - m)
        inter = jnp.exp(m_inter - m)
        c_old = ct[h]
        num = _dot(sc.astype(BF16), vh) + inter * _dot(qh, c_old.astype(BF16))
        den = num[:, ML_HEAD_DIM:ML_HEAD_DIM + 1]
        hid = num / jnp.maximum(jnp.abs(den), jnp.exp(-m))
        out_ref[:, cols] = (_sigmoid(og_ref[:, cols].astype(F32)) * hid).astype(out_ref.dtype)
        b_last = b_c[ML_CHUNK - 1:ML_CHUNK, :]
        m_new = m[ML_CHUNK - 1:ML_CHUNK, :]
        w = jnp.exp(b_last - b_c + i_c - m_new)
        decay = jnp.exp(b_last + m_prev - m_new)
        wv = (w * vh.astype(F32)).astype(BF16)
        ct[h] = decay * c_old + _dot_tn(kh, wv)
        m_s[h] = jnp.broadcast_to(m_new, m_s.shape[1:])


def _mlstm(ml, gates, conv_w, conv_b, batch, seq):
    n = batch * seq
    nc = seq // ML_CHUNK

    def part(p):
        return pl.BlockSpec((ML_CHUNK, ML_PAD_WIDTH), lambda b, c: (b * nc + c, p))

    return pl.pallas_call(
        _mlstm_kernel,
        grid=(batch, nc),
        in_specs=[part(0), part(1), part(2), part(3),
                  pl.BlockSpec((ML_CHUNK, LANES), lambda b, c: (b * nc + c, 0)),
                  pl.BlockSpec((ML_CONV, 2 * ML_PAD_WIDTH), lambda b, c: (0, 0)),
                  pl.BlockSpec((1, 2 * ML_PAD_WIDTH), lambda b, c: (0, 0))],
        out_specs=pl.BlockSpec((ML_CHUNK, ML_PAD_WIDTH), lambda b, c: (b * nc + c, 0)),
        out_shape=jax.ShapeDtypeStruct((n, ML_PAD_WIDTH), BF16),
        scratch_shapes=[pltpu.VMEM((ML_CHUNK + 8, ML_PAD_WIDTH), F32),
                        pltpu.VMEM((ML_CHUNK + 8, ML_PAD_WIDTH), F32),
                        pltpu.VMEM((ML_HEADS, ML_HEAD_PAD, ML_HEAD_PAD), F32),
                        pltpu.VMEM((ML_HEADS, 8, LANES), F32)],
        compiler_params=_cparams("parallel", "arbitrary"),
        name="mlstm",
    )(ml, ml, ml, ml, gates, conv_w, conv_b)


def _mixer_out_kernel(x_ref, o0_ref, l0_ref, o1_ref, l1_ref, o2_ref, l2_ref, sg_ref, hc_ref, gp_ref,
                      sgg_ref, sgb_ref, sgw_ref, sgbias_ref, wa_ref, wb_ref, wc_ref, wo_ref, lng_ref, lnb_ref,
                      xf_ref, xb_ref, unperm, *, alpha):
    tm = x_ref.shape[0]

    def token_order(ref, slot):
        dil = ref.shape[0]
        if dil == 1:
            return ref[0].astype(F32)
        slabs = ATT_GROUP_WIDTH // LANES
        for r in range(dil):
            for c in range(slabs):
                unperm[slabs * slot + c, pl.ds(r, tm // dil, stride=dil), :] = (
                    ref[r, :, c * LANES:(c + 1) * LANES].astype(F32))
        return jnp.concatenate([unperm[slabs * slot + c] for c in range(slabs)], axis=1)

    l0, l1, l2 = token_order(l0_ref, 0), token_order(l1_ref, 1), token_order(l2_ref, 2)
    lmax = jnp.maximum(jnp.maximum(l0, l1), l2)
    e0, e1, e2 = jnp.exp(l0 - lmax), jnp.exp(l1 - lmax), jnp.exp(l2 - lmax)
    inv = 1.0 / (e0 + e1 + e2)
    y_a = ((e0 * inv) * token_order(o0_ref, 3) + (e1 * inv) * token_order(o1_ref, 4)
           + (e2 * inv) * token_order(o2_ref, 5))

    u = _gelu_tanh(sg_ref[:, 0:SG_WIDTH].astype(F32))
    v = _gelu_tanh(sg_ref[:, SG_WIDTH:2 * SG_WIDTH].astype(F32))
    v = _layer_norm(v, sgg_ref[...], sgb_ref[...]).astype(BF16)
    row = lax.broadcasted_iota(jnp.int32, (SG_CHUNK, SG_CHUNK), 0)
    colm = lax.broadcasted_iota(jnp.int32, (SG_CHUNK, SG_CHUNK), 1)
    causal = row >= colm
    group_dim = SG_WIDTH // SG_GROUPS
    chunks = []
    for c in range(tm // SG_CHUNK):
        rows = slice(c * SG_CHUNK, (c + 1) * SG_CHUNK)
        parts = []
        for g in range(SG_GROUPS):
            w = jnp.where(causal, sgw_ref[g], 0.0).astype(BF16)
            mixed = _dot(w, v[rows, g * group_dim:(g + 1) * group_dim]) + sgbias_ref[:, g:g + 1]
            parts.append(mixed)
        chunks.append(jnp.concatenate(parts, axis=1))
    y_b = u * jnp.concatenate(chunks, axis=0)

    def gate(i):
        return _sigmoid(gp_ref[:, i * D_MODEL:(i + 1) * D_MODEL].astype(F32))

    z = (gate(0) * _dot(y_a.astype(BF16), wa_ref[...])
         + gate(1) * _dot(y_b.astype(BF16), wb_ref[...])
         + gate(2) * _dot(hc_ref[...], wc_ref[...]))
    mix = _dot(z.astype(BF16), wo_ref[...])
    out = _layer_norm(alpha * x_ref[...] + mix, lng_ref[...], lnb_ref[...])
    xf_ref[...] = out
    xb_ref[...] = out.astype(BF16)


def _mixer_out(x, att_outs, sg, hc, gpre, p, alpha, seq, tm=256):
    n = x.shape[0]
    tiles_per_seq = seq // tm

    def rows(width):
        return pl.BlockSpec((tm, width), lambda i: (i, 0))

    def whole(a):
        return pl.BlockSpec(a.shape, lambda i: (0,) * a.ndim)

    def residue_major(dil):
        return pl.BlockSpec((None, dil, tm // dil, ATT_GROUP_WIDTH),
                            lambda i: (i // tiles_per_seq, 0, i % tiles_per_seq, 0))

    att_specs, att_args = [], []
    for (o, lse), (_, dil) in zip(att_outs, ATT_GROUPS):
        att_specs += [residue_major(dil)] * 2
        att_args += [o, lse]
    weights = [p["sg_ln_g"], p["sg_ln_b"], p["sg_w"], p["sg_bias_t"], p["w_br_a"], p["w_br_b"], p["w_br_c"],
               p["w_out"], p["ln_g0"], p["ln_b0"]]
    return pl.pallas_call(
        functools.partial(_mixer_out_kernel, alpha=alpha),
        grid=(n // tm,),
        in_specs=[rows(D_MODEL)] + att_specs
                 + [rows(2 * SG_WIDTH), rows(ML_PAD_WIDTH), rows(N_BRANCH * D_MODEL)]
                 + [whole(w) for w in weights],
        out_specs=[rows(D_MODEL), rows(D_MODEL)],
        out_shape=[jax.ShapeDtypeStruct((n, D_MODEL), F32), jax.ShapeDtypeStruct((n, D_MODEL), BF16)],
        scratch_shapes=[pltpu.VMEM((2 * len(ATT_GROUPS) * ATT_GROUP_WIDTH // LANES, tm, LANES), F32)],
        compiler_params=_cparams("parallel"),
        name="mixer_out",
    )(x, *att_args, sg, hc, gpre, *weights)


def _swiglu_tile(x, w1_ref, w3_ref, w2_ref, prod, tf):
    for c in range(D_FF // tf):
        cols = slice(c * tf, (c + 1) * tf)
        h1 = _dot(x, w1_ref[:, cols])
        h3 = _dot(x, w3_ref[:, cols])
        prod[:, cols] = (h1 * _sigmoid(h1) * h3).astype(BF16)
    return _dot(prod[...], w2_ref[...])


def _ffn_kernel(xf_ref, xb_ref, w1_ref, w3_ref, w2_ref, lng_ref, lnb_ref, of_ref, ob_ref, prod, *, alpha, tf):
    f = _swiglu_tile(xb_ref[...], w1_ref, w3_ref, w2_ref, prod, tf)
    out = _layer_norm(alpha * xf_ref[...] + f, lng_ref[...], lnb_ref[...])
    of_ref[...] = out
    ob_ref[...] = out.astype(BF16)


def _ffn(xf, xb, w1, w3, w2, ln_g, ln_b, alpha, tm=512, tf=256):
    n = xf.shape[0]

    def resident(shape):
        return pl.BlockSpec(shape, lambda i: (0, 0), pipeline_mode=pl.Buffered(1))

    return pl.pallas_call(
        functools.partial(_ffn_kernel, alpha=alpha, tf=tf),
        grid=(n // tm,),
        in_specs=[pl.BlockSpec((tm, D_MODEL), lambda i: (i, 0)),
                  pl.BlockSpec((tm, D_MODEL), lambda i: (i, 0)),
                  resident((D_MODEL, D_FF)), resident((D_MODEL, D_FF)), resident((D_FF, D_MODEL)),
                  pl.BlockSpec((1, D_MODEL), lambda i: (0, 0)),
                  pl.BlockSpec((1, D_MODEL), lambda i: (0, 0))],
        out_specs=[pl.BlockSpec((tm, D_MODEL), lambda i: (i, 0))] * 2,
        out_shape=[jax.ShapeDtypeStruct((n, D_MODEL), F32), jax.ShapeDtypeStruct((n, D_MODEL), BF16)],
        scratch_shapes=[pltpu.VMEM((tm, D_FF), BF16)],
        compiler_params=_cparams("parallel"),
        name="ffn",
    )(xf, xb, w1, w3, w2, ln_g, ln_b)


def _router_kernel(x_ref, w_ref, b_ref, info_ref):
    xh, xm, xl = _split3(x_ref[...])
    wh, wm, wl = _split3(w_ref[...])
    logits = (_dot(xh, wh) + (_dot(xh, wm) + _dot(xm, wh))
              + (_dot(xh, wl) + _dot(xm, wm) + _dot(xl, wh))) + b_ref[...]
    lane = lax.broadcasted_iota(jnp.int32, logits.shape, 1)
    neg_inf = jnp.float32(-jnp.inf)
    logits = jnp.where(lane < N_EXPERTS, logits, neg_inf)
    v1 = jnp.max(logits, axis=-1, keepdims=True)
    i1 = jnp.min(jnp.where(logits == v1, lane, LANES), axis=-1, keepdims=True)
    rest = jnp.where(lane == i1, neg_inf, logits)
    v2 = jnp.max(rest, axis=-1, keepdims=True)
    i2 = jnp.min(jnp.where(rest == v2, lane, LANES), axis=-1, keepdims=True)
    e = jnp.exp(v2 - v1)
    inv = 1.0 / (1.0 + e)
    info_ref[...] = jnp.where(lane == 0, i1.astype(F32),
                              jnp.where(lane == 1, i2.astype(F32),
                                        jnp.where(lane == 2, inv, jnp.where(lane == 3, e * inv, 0.0))))


def _router(xf, router_w, router_b, tm=1024):
    n = xf.shape[0]
    w = jnp.pad(router_w, ((0, 0), (0, LANES - N_EXPERTS)))
    b = jnp.pad(router_b, (0, LANES - N_EXPERTS))[None, :]
    return pl.pallas_call(
        _router_kernel,
        grid=(n // tm,),
        in_specs=[pl.BlockSpec((tm, D_MODEL), lambda i: (i, 0)),
                  pl.BlockSpec((D_MODEL, LANES), lambda i: (0, 0)),
                  pl.BlockSpec((1, LANES), lambda i: (0, 0))],
        out_specs=pl.BlockSpec((tm, LANES), lambda i: (i, 0)),
        out_shape=jax.ShapeDtypeStruct((n, LANES), F32),
        compiler_params=_cparams("parallel"),
        name="router",
    )(xf, w, b)


def _routing_plan(info, n):
    tile = EXPERT_TILE
    n_tiles = TOP_K * n // tile + N_EXPERTS
    expert = info[:, 0:TOP_K].astype(jnp.int32).reshape(-1)
    onehot = (expert[:, None] == jnp.arange(N_EXPERTS, dtype=jnp.int32)[None, :]).astype(jnp.int32)
    running = jnp.cumsum(onehot, axis=0)
    rank = jnp.sum(onehot * running, axis=1) - 1
    counts = running[-1]
    padded = ((counts + tile - 1) // tile) * tile
    ends = jnp.cumsum(padded)
    starts = ends - padded
    dest = jnp.sum(onehot * starts[None, :], axis=1) + rank
    used = ends[-1] // tile
    tile_start = jnp.arange(n_tiles, dtype=jnp.int32) * tile
    tile_expert = jnp.sum((tile_start[:, None] >= ends[None, :]).astype(jnp.int32), axis=1)
    last_expert = jnp.sum((tile_start[used - 1] >= ends).astype(jnp.int32))
    tile_expert = jnp.minimum(tile_expert, last_expert)
    return dest.astype(jnp.int32), tile_expert.astype(jnp.int32), used.astype(jnp.int32).reshape(1), n_tiles


def _row_copy(src, src_row, dst, dst_row, sem):
    return pltpu.make_async_copy(src.at[pl.ds(src_row, 1), :], dst.at[pl.ds(dst_row, 1), :], sem)


def _rows_wait(src, dst, rows, sem):
    pltpu.make_async_copy(src.at[pl.ds(0, rows), :], dst.at[pl.ds(0, rows), :], sem).wait()


def _dispatch_kernel(dest_ref, x_ref, zero_ref, xs_ref, sem, *, tm):
    del zero_ref
    i = pl.program_id(0)
    base = i * tm

    def start(t, carry):
        for c in range(TOP_K):
            _row_copy(x_ref, base + t, xs_ref, dest_ref[0, 0, TOP_K * t + c], sem).start()
        return carry

    lax.fori_loop(0, tm, start, 0, unroll=DMA_ISSUE_UNROLL)

    @pl.when(i > 0)
    def _():
        _rows_wait(x_ref, xs_ref, TOP_K * tm, sem)

    @pl.when(i == pl.num_programs(0) - 1)
    def _():
        _rows_wait(x_ref, xs_ref, TOP_K * tm, sem)


def _dispatch(xf, dest, n_slots, tm=512):
    n = xf.shape[0]
    dest3 = dest.reshape(n // tm, 1, TOP_K * tm)
    zeros = jnp.zeros((n_slots, D_MODEL), F32)
    return pl.pallas_call(
        functools.partial(_dispatch_kernel, tm=tm),
        grid=(n // tm,),
        in_specs=[pl.BlockSpec((1, 1, TOP_K * tm), lambda i: (i, 0, 0), memory_space=pltpu.SMEM),
                  pl.BlockSpec(memory_space=pl.ANY),
                  pl.BlockSpec(memory_space=pl.ANY)],
        out_specs=pl.BlockSpec(memory_space=pl.ANY),
        out_shape=jax.ShapeDtypeStruct((n_slots, D_MODEL), F32),
        scratch_shapes=[pltpu.SemaphoreType.DMA],
        input_output_aliases={2: 0},
        compiler_params=_cparams("arbitrary"),
        name="moe_dispatch",
    )(dest3, xf, zeros)


def _experts_kernel(te_ref, used_ref, xs_ref, w1_ref, w3_ref, w2_ref, ys_ref, prod, *, tf):
    del te_ref
    i = pl.program_id(0)

    @pl.when(i < used_ref[0])
    def _():
        ys_ref[...] = _swiglu_tile(xs_ref[...].astype(BF16), w1_ref, w3_ref, w2_ref, prod, tf)

    @pl.when(i >= used_ref[0])
    def _():
        ys_ref[...] = jnp.zeros(ys_ref.shape, ys_ref.dtype)


def _experts(xs, tile_expert, used, w1, w3, w2, n_tiles, tf=256):
    tile = EXPERT_TILE
    grid_spec = pltpu.PrefetchScalarGridSpec(
        num_scalar_prefetch=2,
        grid=(n_tiles,),
        in_specs=[pl.BlockSpec((tile, D_MODEL), lambda i, te, nu: (i, 0)),
                  pl.BlockSpec((None, D_MODEL, D_FF), lambda i, te, nu: (te[i], 0, 0)),
                  pl.BlockSpec((None, D_MODEL, D_FF), lambda i, te, nu: (te[i], 0, 0)),
                  pl.BlockSpec((None, D_FF, D_MODEL), lambda i, te, nu: (te[i], 0, 0))],
        out_specs=pl.BlockSpec((tile, D_MODEL), lambda i, te, nu: (i, 0)),
        scratch_shapes=[pltpu.VMEM((tile, D_FF), BF16)],
    )
    return pl.pallas_call(
        functools.partial(_experts_kernel, tf=tf),
        grid_spec=grid_spec,
        out_shape=jax.ShapeDtypeStruct((n_tiles * tile, D_MODEL), F32),
        compiler_params=_cparams("arbitrary"),
        name="moe_experts",
    )(tile_expert, used, xs, w1, w3, w2)


def _combine_kernel(dest_ref, next_dest_ref, xf_ref, info_ref, ys_ref, lng_ref, lnb_ref, of_ref, got, sem, *, alpha):
    tm = xf_ref.shape[0]
    i = pl.program_id(0)
    slot = i % 2

    def gather(dests, into):
        def start(t, carry):
            for c in range(TOP_K):
                _row_copy(ys_ref, dests[0, 0, TOP_K * t + c], got.at[into, c], t, sem.at[into]).start()
            return carry
        lax.fori_loop(0, tm, start, 0, unroll=DMA_ISSUE_UNROLL)

    @pl.when(i == 0)
    def _():
        gather(dest_ref, 0)

    @pl.when(i + 1 < pl.num_programs(0))
    def _():
        gather(next_dest_ref, 1 - slot)

    for c in range(TOP_K):
        _rows_wait(ys_ref, got.at[slot, c], tm, sem.at[slot])
    info = info_ref[...]
    mix = info[:, 2:3] * got[slot, 0] + info[:, 3:4] * got[slot, 1]
    of_ref[...] = _layer_norm(alpha * xf_ref[...] + mix, lng_ref[...], lnb_ref[...])


def _combine(xf, info, ys, dest, ln_g, ln_b, alpha, tm=256):
    n = xf.shape[0]
    steps = n // tm
    dest3 = dest.reshape(steps, 1, TOP_K * tm)
    return pl.pallas_call(
        functools.partial(_combine_kernel, alpha=alpha),
        grid=(steps,),
        in_specs=[pl.BlockSpec((1, 1, TOP_K * tm), lambda i: (i, 0, 0), memory_space=pltpu.SMEM),
                  pl.BlockSpec((1, 1, TOP_K * tm), lambda i: (jnp.minimum(i + 1, steps - 1), 0, 0),
                               memory_space=pltpu.SMEM),
                  pl.BlockSpec((tm, D_MODEL), lambda i: (i, 0)),
                  pl.BlockSpec((tm, LANES), lambda i: (i, 0)),
                  pl.BlockSpec(memory_space=pl.ANY),
                  pl.BlockSpec((1, D_MODEL), lambda i: (0, 0)),
                  pl.BlockSpec((1, D_MODEL), lambda i: (0, 0))],
        out_specs=pl.BlockSpec((tm, D_MODEL), lambda i: (i, 0)),
        out_shape=jax.ShapeDtypeStruct((n, D_MODEL), F32),
        scratch_shapes=[pltpu.VMEM((2, TOP_K, tm, D_MODEL), F32), pltpu.SemaphoreType.DMA((2,))],
        compiler_params=_cparams("arbitrary"),
        name="moe_combine",
    )(dest3, dest3, xf, info, ys, ln_g, ln_b)


def _moe(xf, router_w, router_b, w1, w3, w2, ln_g, ln_b, alpha):
    n = xf.shape[0]
    info = _router(xf, router_w, router_b)
    dest, tile_expert, used, n_tiles = _routing_plan(info, n)
    xs = _dispatch(xf, dest, n_tiles * EXPERT_TILE)
    ys = _experts(xs, tile_expert, used, w1, w3, w2, n_tiles)
    return _combine(xf, info, ys, dest, ln_g, ln_b, alpha)


def _pad_heads(w):
    lead = w.shape[:-1]
    w = w.reshape(*lead, ML_HEADS, ML_HEAD_DIM)
    w = jnp.pad(w, [(0, 0)] * len(lead) + [(0, 0), (0, ML_HEAD_PAD - ML_HEAD_DIM)])
    return w.reshape(*lead, ML_PAD_WIDTH)


def _group_qkv(w, group):
    gw = ATT_GROUP_WIDTH
    return jnp.concatenate([w[..., part * ATT_WIDTH + group * gw:part * ATT_WIDTH + (group + 1) * gw]
                            for part in range(3)], axis=-1)


def _layer_params(layer, w_in, b_in, conv_w, conv_b, sg_ln_g, sg_ln_b, sg_w, sg_b,
                  w_br_a, w_br_b, w_br_c, w_out, ln_g, ln_b):
    w, b = w_in[layer], b_in[layer]
    o_att, o_sg, o_qk, o_v, o_o, o_i, o_g = 0, 2304, 3840, 5376, 6144, 6912, 6920
    ones_col = jnp.zeros((ML_HEADS, ML_HEAD_PAD), F32).at[:, ML_HEAD_DIM].set(1.0).reshape(ML_PAD_WIDTH)
    w_ml = jnp.concatenate([_pad_heads(w[:, o_qk:o_qk + ML_WIDTH]), _pad_heads(w[:, o_qk + ML_WIDTH:o_v]),
                            _pad_heads(w[:, o_v:o_o]), _pad_heads(w[:, o_o:o_i])], axis=1)
    b_ml = jnp.concatenate([_pad_heads(b[o_qk:o_qk + ML_WIDTH]), _pad_heads(b[o_qk + ML_WIDTH:o_v]),
                            _pad_heads(b[o_v:o_o]) + ones_col, _pad_heads(b[o_o:o_i])])
    cw, cb = conv_w[layer], conv_b[layer]
    w_att, b_att = w[:, o_att:o_sg], b[o_att:o_sg]
    return dict(
        w_att=[_group_qkv(w_att, g).astype(BF16) for g in range(len(ATT_GROUPS))],
        b_att=[_group_qkv(b_att, g)[None, :] for g in range(len(ATT_GROUPS))],
        w_sg=w[:, o_sg:o_qk].astype(BF16), b_sg=b[None, o_sg:o_qk],
        w_ml=w_ml.astype(BF16), b_ml=b_ml[None, :],
        w_if=jnp.pad(w[:, o_i:o_g], ((0, 0), (0, LANES - 2 * ML_HEADS))).astype(BF16),
        b_if=jnp.pad(b[o_i:o_g], (0, LANES - 2 * ML_HEADS))[None, :],
        w_g=w[:, o_g:].astype(BF16), b_g=b[None, o_g:],
        conv_w=jnp.concatenate([_pad_heads(cw[:, :ML_WIDTH]), _pad_heads(cw[:, ML_WIDTH:])], axis=1),
        conv_b=jnp.concatenate([_pad_heads(cb[:ML_WIDTH]), _pad_heads(cb[ML_WIDTH:])])[None, :],
        sg_ln_g=sg_ln_g[layer][None, :], sg_ln_b=sg_ln_b[layer][None, :],
        sg_w=sg_w[layer], sg_bias_t=sg_b[layer].T,
        w_br_a=w_br_a[layer].astype(BF16), w_br_b=w_br_b[layer].astype(BF16),
        w_br_c=_pad_heads(w_br_c[layer].T).T.astype(BF16),
        w_out=w_out[layer].astype(BF16),
        ln_g0=ln_g[layer, 0][None, :], ln_b0=ln_b[layer, 0][None, :],
        ln_g1=ln_g[layer, 1][None, :], ln_b1=ln_b[layer, 1][None, :],
    )


def kernel(x, positions, w_in, b_in, conv_w, conv_b, sg_ln_g, sg_ln_b, sg_w, sg_b, w_br_a, w_br_b, w_br_c, w_out,
           ln_g, ln_b, ffn_w1, ffn_w3, ffn_w2, router_w, router_b, moe_w1, moe_w3, moe_w2):
    batch, seq, _ = x.shape
    n = batch * seq
    depth = w_in.shape[0]
    alpha = (2.0 * depth) ** 0.25
    cos, sin = _rope_tables(positions)
    xf = x.reshape(n, D_MODEL)
    xb = xf.astype(BF16)
    for layer in range(depth):
        p = _layer_params(layer, w_in, b_in, conv_w, conv_b, sg_ln_g, sg_ln_b, sg_w, sg_b,
                          w_br_a, w_br_b, w_br_c, w_out, ln_g, ln_b)
        sg = _linear(xb, p["w_sg"], p["b_sg"], BF16, 2048, 768)
        ml = _linear(xb, p["w_ml"], p["b_ml"], BF16, 2048, 1024)
        gates_if = _linear(xb, p["w_if"], p["b_if"], F32, 2048, LANES)
        gpre = _linear(xb, p["w_g"], p["b_g"], BF16, 2048, 1024)
        att_outs = []
        for g, (_, dil) in enumerate(ATT_GROUPS):
            qkv = _linear_dilated(xb, p["w_att"][g], p["b_att"][g], dil, batch, seq)
            att_outs.append(_attention_group(qkv, cos, sin, g, batch, seq))
        hc = _mlstm(ml, gates_if, p["conv_w"], p["conv_b"], batch, seq)
        xf, xb = _mixer_out(xf, att_outs, sg, hc, gpre, p, alpha, seq)
        j = layer // 2
        if layer % 2 == 0:
            xf, xb = _ffn(xf, xb, ffn_w1[j].astype(BF16), ffn_w3[j].astype(BF16), ffn_w2[j].astype(BF16),
                          p["ln_g1"], p["ln_b1"], alpha)
        else:
            xf = _moe(xf, router_w[j], router_b[j], moe_w1[j].astype(BF16), moe_w3[j].astype(BF16),
                      moe_w2[j].astype(BF16), p["ln_g1"], p["ln_b1"], alpha)
            xb = xf.astype(BF16)
    return xf.reshape(batch, seq, D_MODEL)
```

```python
import functools

import numpy as np
import jax
import jax.numpy as jnp
from jax import lax
from jax.experimental import pallas as pl
from jax.experimental.pallas import tpu as pltpu

F32 = jnp.float32
BF16 = jnp.bfloat16

D_MODEL = 1024
ATT_GROUPS = ((128, 1), (512, 4), (2048, 16))
ATT_HEADS_PER_GROUP = 4
ATT_HEAD_DIM = 64
ATT_WIDTH = 768
ATT_GROUP_WIDTH = ATT_HEADS_PER_GROUP * ATT_HEAD_DIM
ROPE_THETA = 10000.0
SG_CHUNK = 128
SG_GROUPS = 6
SG_WIDTH = 768
ML_HEADS = 4
ML_HEAD_DIM = 192
ML_HEAD_PAD = 256
ML_WIDTH = 768
ML_PAD_WIDTH = ML_HEADS * ML_HEAD_PAD
ML_CHUNK = 128
ML_CONV = 4
N_BRANCH = 3
D_FF = 2816
N_EXPERTS = 8
TOP_K = 2
LN_EPS = 1e-5
BLOCK = 128
LANES = 128
VMEM_LIMIT = 56 * 1024 * 1024
EXPERT_TILE = 512
DMA_ISSUE_UNROLL = 8
MIXER_SUB_ROWS = 256


def _cparams(*sem):
    return pltpu.CompilerParams(dimension_semantics=sem, vmem_limit_bytes=VMEM_LIMIT)


def _layer_norm(x, g, b):
    mu = jnp.mean(x, axis=-1, keepdims=True)
    xc = x - mu
    var = jnp.mean(xc * xc, axis=-1, keepdims=True)
    return xc * lax.rsqrt(var + LN_EPS) * g + b


def _gelu_tanh(x):
    return 0.5 * x * (1.0 + jnp.tanh(np.sqrt(2.0 / np.pi) * (x + 0.044715 * (x * x * x))))


def _sigmoid(x):
    return 0.5 * (1.0 + jnp.tanh(0.5 * x))


def _dot(a, b):
    return jnp.dot(a, b, preferred_element_type=F32)


def _dot_nt(a, b):
    return lax.dot_general(a, b, (((1,), (1,)), ((), ())), preferred_element_type=F32)


def _dot_tn(a, b):
    return lax.dot_general(a, b, (((0,), (0,)), ((), ())), preferred_element_type=F32)


def _split3(x):
    hi = x.astype(BF16)
    r1 = x - hi.astype(F32)
    mid = r1.astype(BF16)
    lo = (r1 - mid.astype(F32)).astype(BF16)
    return hi, mid, lo


def _linear_kernel(x_ref, w_ref, b_ref, o_ref, *, act):
    res = _dot(x_ref[...], w_ref[...]) + b_ref[...]
    o_ref[...] = (res if act is None else act(res)).astype(o_ref.dtype)


def _linear(x, w, b, out_dtype, tm, tn, act=None):
    m, k = x.shape
    n = w.shape[1]
    return pl.pallas_call(
        functools.partial(_linear_kernel, act=act),
        grid=(m // tm, n // tn),
        in_specs=[pl.BlockSpec((tm, k), lambda i, j: (i, 0)),
                  pl.BlockSpec((k, tn), lambda i, j: (0, j)),
                  pl.BlockSpec((1, tn), lambda i, j: (0, j))],
        out_specs=pl.BlockSpec((tm, tn), lambda i, j: (i, j)),
        out_shape=jax.ShapeDtypeStruct((m, n), out_dtype),
        compiler_params=_cparams("parallel", "parallel"),
        name="linear",
    )(x, w, b)


def _linear_dilated_kernel(x_ref, w_ref, b_ref, o_ref, acc, *, dil):
    res = _dot(x_ref[...], w_ref[...]) + b_ref[...]
    slabs, rows, _ = acc.shape
    for c in range(slabs):
        acc[c] = res[:, c * LANES:(c + 1) * LANES]
    sub = rows // dil
    for r in range(dil):
        for c in range(slabs):
            o_ref[r, :, c * LANES:(c + 1) * LANES] = acc[c, pl.ds(r, sub, stride=dil), :].astype(o_ref.dtype)


def _linear_dilated(x, w, b, dil, batch, seq):
    k = x.shape[1]
    n = w.shape[1]
    tn = ATT_GROUP_WIDTH
    sub = seq // dil
    return pl.pallas_call(
        functools.partial(_linear_dilated_kernel, dil=dil),
        grid=(batch, n // tn),
        in_specs=[pl.BlockSpec((seq, k), lambda i, j: (i, 0)),
                  pl.BlockSpec((k, tn), lambda i, j: (0, j)),
                  pl.BlockSpec((1, tn), lambda i, j: (0, j))],
        out_specs=pl.BlockSpec((None, dil, sub, tn), lambda i, j: (i, 0, 0, j)),
        out_shape=jax.ShapeDtypeStruct((batch, dil, sub, n), BF16),
        scratch_shapes=[pltpu.VMEM((tn // LANES, seq, LANES), F32)],
        compiler_params=_cparams("parallel", "parallel"),
        name=f"linear_dil{dil}",
    )(x, w, b)


def _rope_table_kernel(pos_ref, freq_ref, sign_ref, cos_ref, sin_ref):
    ang = pos_ref[...] * freq_ref[...]
    cos_ref[...] = jnp.cos(ang)
    sin_ref[...] = jnp.sin(ang) * sign_ref[...]


def _rope_tables(positions):
    n = positions.size
    half = ATT_HEAD_DIM // 2
    freqs = ROPE_THETA ** (-jnp.arange(half, dtype=F32) * (2.0 / ATT_HEAD_DIM))
    lane = np.arange(LANES)
    freq_row = freqs[lane % half][None, :]
    sign_row = jnp.asarray(np.where(lane % ATT_HEAD_DIM < half, -1.0, 1.0), F32)[None, :]
    pos = positions.astype(F32).reshape(n, 1)
    tm = 2048
    return pl.pallas_call(
        _rope_table_kernel,
        grid=(n // tm,),
        in_specs=[pl.BlockSpec((tm, 1), lambda i: (i, 0)),
                  pl.BlockSpec((1, LANES), lambda i: (0, 0)),
                  pl.BlockSpec((1, LANES), lambda i: (0, 0))],
        out_specs=[pl.BlockSpec((tm, LANES), lambda i: (i, 0))] * 2,
        out_shape=[jax.ShapeDtypeStruct((n, LANES), F32)] * 2,
        compiler_params=_cparams("parallel"),
        name="rope_tables",
    )(pos, freq_row, sign_row)


def _attn_kernel(q_ref, k_ref, v_ref, cos_ref, sin_ref, o_ref, lse_ref, qs, ks, vs, *, sub, dil):
    nb = sub // BLOCK
    res = pl.program_id(1)
    lane = lax.broadcasted_iota(jnp.int32, (BLOCK, LANES), 1)
    first_half = (lane % ATT_HEAD_DIM) < (ATT_HEAD_DIM // 2)
    low_head = lane < ATT_HEAD_DIM

    def rope(x, c, s):
        partner = jnp.where(first_half, pltpu.roll(x, LANES - 32, 1), pltpu.roll(x, 32, 1))
        return x * c + partner * s

    ks[0:BLOCK, :] = jnp.zeros((BLOCK, ATT_GROUP_WIDTH), BF16)
    vs[0:BLOCK, :] = jnp.zeros((BLOCK, ATT_GROUP_WIDTH), BF16)

    def prep(n, carry):
        r0 = pl.multiple_of(n * BLOCK, BLOCK)
        if dil == 1:
            c = cos_ref[pl.ds(r0, BLOCK), :]
            s = sin_ref[pl.ds(r0, BLOCK), :]
        else:
            c = cos_ref[pl.ds(r0 * dil + res, BLOCK, stride=dil), :]
            s = sin_ref[pl.ds(r0 * dil + res, BLOCK, stride=dil), :]
        for pair in range(2):
            cols = slice(pair * LANES, (pair + 1) * LANES)
            q = rope(q_ref[pl.ds(r0, BLOCK), cols].astype(F32), c, s) * (ATT_HEAD_DIM ** -0.5)
            qs[pl.ds(r0, BLOCK), (2 * pair) * LANES:(2 * pair + 1) * LANES] = jnp.where(low_head, q, 0.0).astype(BF16)
            qs[pl.ds(r0, BLOCK), (2 * pair + 1) * LANES:(2 * pair + 2) * LANES] = jnp.where(low_head, 0.0, q).astype(BF16)
            k = rope(k_ref[pl.ds(r0, BLOCK), cols].astype(F32), c, s)
            ks[pl.ds(r0 + BLOCK, BLOCK), cols] = k.astype(BF16)
        vs[pl.ds(r0 + BLOCK, BLOCK), :] = v_ref[pl.ds(r0, BLOCK), :]
        return carry

    lax.fori_loop(0, nb, prep, 0)

    qi = lax.broadcasted_iota(jnp.int32, (BLOCK, 2 * BLOCK), 0)
    ki = lax.broadcasted_iota(jnp.int32, (BLOCK, 2 * BLOCK), 1)
    dist = qi + BLOCK - ki
    band = (dist >= 0) & (dist <= BLOCK)
    neg_inf = jnp.float32(-jnp.inf)
    bias_rest = jnp.where(band, 0.0, neg_inf)
    bias_first = jnp.where(band & (ki >= BLOCK), 0.0, neg_inf)

    def block(n, bias):
        r0 = pl.multiple_of(n * BLOCK, BLOCK)
        for pair in range(2):
            cols = slice(pair * LANES, (pair + 1) * LANES)
            kw = ks[pl.ds(r0, 2 * BLOCK), cols]
            vw = vs[pl.ds(r0, 2 * BLOCK), cols]
            outs = []
            for hh in range(2):
                hd = 2 * pair + hh
                qh = qs[pl.ds(r0, BLOCK), hd * LANES:(hd + 1) * LANES]
                s = _dot_nt(qh, kw) + bias
                m = jnp.max(s, axis=-1, keepdims=True)
                p = jnp.exp(s - m)
                den = jnp.sum(p, axis=-1, keepdims=True)
                o = _dot(p.astype(BF16), vw) * (1.0 / den)
                outs.append((o, m + jnp.log(den)))
            o_ref[pl.ds(r0, BLOCK), cols] = jnp.where(low_head, outs[0][0], outs[1][0]).astype(o_ref.dtype)
            lse_ref[pl.ds(r0, BLOCK), cols] = jnp.where(low_head, outs[0][1], outs[1][1])

    block(0, bias_first)
    if nb <= 2:
        for n in range(1, nb):
            block(n, bias_rest)
        return

    def body(n, carry):
        block(n, bias_rest)
        return carry

    lax.fori_loop(1, nb, body, 0, unroll=2)


def _attention_group(qkv, cos, sin, group, batch, seq):
    window, dil = ATT_GROUPS[group]
    assert window // dil == BLOCK
    sub = seq // dil
    assert sub % BLOCK == 0

    def part(p):
        return pl.BlockSpec((None, None, sub, ATT_GROUP_WIDTH), lambda b, r: (b, r, 0, p))

    tab = pl.BlockSpec((seq, LANES), lambda b, r: (b, 0))
    out = pl.BlockSpec((None, None, sub, ATT_GROUP_WIDTH), lambda b, r: (b, r, 0, 0))
    return pl.pallas_call(
        functools.partial(_attn_kernel, sub=sub, dil=dil),
        grid=(batch, dil),
        in_specs=[part(0), part(1), part(2), tab, tab],
        out_specs=[out, out],
        out_shape=[jax.ShapeDtypeStruct((batch, dil, sub, ATT_GROUP_WIDTH), BF16),
                   jax.ShapeDtypeStruct((batch, dil, sub, ATT_GROUP_WIDTH), F32)],
        scratch_shapes=[pltpu.VMEM((sub, ATT_HEADS_PER_GROUP * LANES), BF16),
                        pltpu.VMEM((sub + BLOCK, ATT_GROUP_WIDTH), BF16),
                        pltpu.VMEM((sub + BLOCK, ATT_GROUP_WIDTH), BF16)],
        compiler_params=_cparams("parallel", "parallel"),
        name=f"attention_g{group}",
    )(qkv, qkv, qkv, cos, sin)


def _mlstm_kernel(q_ref, k_ref, v_ref, og_ref, g_ref, cw_ref, cb_ref, out_ref, xq, xk, ct, m_s):
    halo = 8
    taps = ML_CONV

    @pl.when(pl.program_id(1) == 0)
    def _():
        xq[0:halo, :] = jnp.zeros((halo, ML_PAD_WIDTH), F32)
        xk[0:halo, :] = jnp.zeros((halo, ML_PAD_WIDTH), F32)
        ct[...] = jnp.zeros(ct.shape, F32)
        m_s[...] = jnp.zeros(m_s.shape, F32)

    xq[halo:halo + ML_CHUNK, :] = q_ref[...].astype(F32)
    xk[halo:halo + ML_CHUNK, :] = k_ref[...].astype(F32)

    def conv_silu(xbuf, col0):
        acc = cb_ref[:, col0:col0 + ML_PAD_WIDTH]
        for j in range(taps):
            start = halo - (taps - 1) + j
            acc = acc + xbuf[start:start + ML_CHUNK, :] * cw_ref[j:j + 1, col0:col0 + ML_PAD_WIDTH]
        return acc * _sigmoid(acc)

    qc = conv_silu(xq, 0).astype(BF16)
    kc = (conv_silu(xk, ML_PAD_WIDTH) * (ML_HEAD_DIM ** -0.5)).astype(BF16)
    xq[0:halo, :] = xq[ML_CHUNK:ML_CHUNK + halo, :]
    xk[0:halo, :] = xk[ML_CHUNK:ML_CHUNK + halo, :]

    gates = g_ref[...]
    log_f = -(jnp.maximum(-gates, 0.0) + jnp.log1p(jnp.exp(-jnp.abs(gates))))
    row = lax.broadcasted_iota(jnp.int32, (ML_CHUNK, ML_CHUNK), 0)
    colm = lax.broadcasted_iota(jnp.int32, (ML_CHUNK, ML_CHUNK), 1)
    causal = row >= colm
    tri = jnp.where(causal, 1.0, 0.0).astype(BF16)
    hi, mid, lo = _split3(log_f)
    bcum = _dot(tri, hi) + _dot(tri, mid) + _dot(tri, lo)
    bcum_t = bcum.T
    gates_t = gates.T
    neg_inf = jnp.float32(-jnp.inf)

    for h in range(ML_HEADS):
        cols = slice(h * ML_HEAD_PAD, (h + 1) * ML_HEAD_PAD)
        b_c = bcum[:, ML_HEADS + h:ML_HEADS + h + 1]
        b_r = bcum_t[ML_HEADS + h:ML_HEADS + h + 1, :]
        i_c = gates[:, h:h + 1]
        i_r = gates_t[h:h + 1, :]
        m_prev = m_s[h, 0:1, 0:1]
        log_d = jnp.where(causal, b_c - b_r + i_r, neg_inf)
        m_inter = b_c + m_prev
        m = jnp.maximum(m_inter, jnp.max(log_d, axis=-1, keepdims=True))
        qh = qc[:, cols]
        kh = kc[:, cols]
        vh = v_ref[:, cols]
        sc = _dot_nt(qh, kh) * jnp.exp(log_d - m)
        inter = jnp.exp(m_inter - m)
        c_old = ct[h]
        num = _dot(sc.astype(BF16), vh) + inter * _dot(qh, c_old.astype(BF16))
        den = num[:, ML_HEAD_DIM:ML_HEAD_DIM + 1]
        hid = num / jnp.maximum(jnp.abs(den), jnp.exp(-m))
        out_ref[:, cols] = (og_ref[:, cols].astype(F32) * hid).astype(out_ref.dtype)
        b_last = b_c[ML_CHUNK - 1:ML_CHUNK, :]
        m_new = m[ML_CHUNK - 1:ML_CHUNK, :]
        w = jnp.exp(b_last - b_c + i_c - m_new)
        decay = jnp.exp(b_last + m_prev - m_new)
        wv = (w * vh.astype(F32)).astype(BF16)
        ct[h] = decay * c_old + _dot_tn(kh, wv)
        m_s[h] = jnp.broadcast_to(m_new, m_s.shape[1:])


def _mlstm(ml, og, gates, conv_w, conv_b, batch, seq):
    n = batch * seq
    nc = seq // ML_CHUNK

    def part(p):
        return pl.BlockSpec((ML_CHUNK, ML_PAD_WIDTH), lambda b, c: (b * nc + c, p))

    return pl.pallas_call(
        _mlstm_kernel,
        grid=(batch, nc),
        in_specs=[part(0), part(1), part(2), part(0),
                  pl.BlockSpec((ML_CHUNK, LANES), lambda b, c: (b * nc + c, 0)),
                  pl.BlockSpec((ML_CONV, 2 * ML_PAD_WIDTH), lambda b, c: (0, 0)),
                  pl.BlockSpec((1, 2 * ML_PAD_WIDTH), lambda b, c: (0, 0))],
        out_specs=pl.BlockSpec((ML_CHUNK, ML_PAD_WIDTH), lambda b, c: (b * nc + c, 0)),
        out_shape=jax.ShapeDtypeStruct((n, ML_PAD_WIDTH), BF16),
        scratch_shapes=[pltpu.VMEM((ML_CHUNK + 8, ML_PAD_WIDTH), F32),
                        pltpu.VMEM((ML_CHUNK + 8, ML_PAD_WIDTH), F32),
                        pltpu.VMEM((ML_HEADS, ML_HEAD_PAD, ML_HEAD_PAD), F32),
                        pltpu.VMEM((ML_HEADS, 8, LANES), F32)],
        compiler_params=_cparams("parallel", "arbitrary"),
        name="mlstm",
    )(ml, ml, ml, og, gates, conv_w, conv_b)


def _mixer_out_kernel(x_ref, o0_ref, l0_ref, o1_ref, l1_ref, o2_ref, l2_ref, sg_ref, hc_ref, gp_ref,
                      sgg_ref, sgb_ref, sgw_ref, sgbias_ref, wa_ref, wb_ref, wc_ref, wo_ref, lng_ref, lnb_ref,
                      xf_ref, xb_ref, unperm, *, alpha):
    tm = x_ref.shape[0]
    slabs = ATT_GROUP_WIDTH // LANES
    att_refs = (l0_ref, l1_ref, l2_ref, o0_ref, o1_ref, o2_ref)

    for slot, ref in enumerate(att_refs):
        dil = ref.shape[0]
        if dil > 1:
            for r in range(dil):
                for c in range(slabs):
                    unperm[slabs * slot + c, pl.ds(r, tm // dil, stride=dil), :] = (
                        ref[r, :, c * LANES:(c + 1) * LANES].astype(F32))

    def token_order(slot, rows):
        ref = att_refs[slot]
        if ref.shape[0] == 1:
            return ref[0, rows, :].astype(F32)
        return jnp.concatenate([unperm[slabs * slot + c, rows, :] for c in range(slabs)], axis=1)

    row = lax.broadcasted_iota(jnp.int32, (SG_CHUNK, SG_CHUNK), 0)
    colm = lax.broadcasted_iota(jnp.int32, (SG_CHUNK, SG_CHUNK), 1)
    causal = row >= colm
    group_dim = SG_WIDTH // SG_GROUPS
    sg_w = [jnp.where(causal, sgw_ref[g], 0.0).astype(BF16) for g in range(SG_GROUPS)]

    for h in range(tm // MIXER_SUB_ROWS):
        rows = slice(h * MIXER_SUB_ROWS, (h + 1) * MIXER_SUB_ROWS)

        l0, l1, l2 = token_order(0, rows), token_order(1, rows), token_order(2, rows)
        lmax = jnp.maximum(jnp.maximum(l0, l1), l2)
        e0, e1, e2 = jnp.exp(l0 - lmax), jnp.exp(l1 - lmax), jnp.exp(l2 - lmax)
        inv = 1.0 / (e0 + e1 + e2)
        y_a = ((e0 * inv) * token_order(3, rows) + (e1 * inv) * token_order(4, rows)
               + (e2 * inv) * token_order(5, rows))

        u = sg_ref[rows, 0:SG_WIDTH].astype(F32)
        v = sg_ref[rows, SG_WIDTH:2 * SG_WIDTH].astype(F32)
        v = _layer_norm(v, sgg_ref[...], sgb_ref[...]).astype(BF16)
        chunks = []
        for c in range(MIXER_SUB_ROWS // SG_CHUNK):
            crows = slice(c * SG_CHUNK, (c + 1) * SG_CHUNK)
            parts = [_dot(sg_w[g], v[crows, g * group_dim:(g + 1) * group_dim]) + sgbias_ref[:, g:g + 1]
                     for g in range(SG_GROUPS)]
            chunks.append(jnp.concatenate(parts, axis=1))
        y_b = u * jnp.concatenate(chunks, axis=0)

        def gate(i):
            return gp_ref[rows, i * D_MODEL:(i + 1) * D_MODEL].astype(F32)

        z = (gate(0) * _dot(y_a.astype(BF16), wa_ref[...])
             + gate(1) * _dot(y_b.astype(BF16), wb_ref[...])
             + gate(2) * _dot(hc_ref[rows, :], wc_ref[...]))
        mix = _dot(z.astype(BF16), wo_ref[...])
        out = _layer_norm(alpha * x_ref[rows, :] + mix, lng_ref[...], lnb_ref[...])
        xf_ref[rows, :] = out
        xb_ref[rows, :] = out.astype(BF16)


def _mixer_out(x, att_outs, sg, hc, gpre, p, alpha, seq, tm=512):
    n = x.shape[0]
    tiles_per_seq = seq // tm

    def rows(width):
        return pl.BlockSpec((tm, width), lambda i: (i, 0))

    def whole(a):
        return pl.BlockSpec(a.shape, lambda i: (0,) * a.ndim)

    def residue_major(dil):
        return pl.BlockSpec((None, dil, tm // dil, ATT_GROUP_WIDTH),
                            lambda i: (i // tiles_per_seq, 0, i % tiles_per_seq, 0))

    att_specs, att_args = [], []
    for (o, lse), (_, dil) in zip(att_outs, ATT_GROUPS):
        att_specs += [residue_major(dil)] * 2
        att_args += [o, lse]
    weights = [p["sg_ln_g"], p["sg_ln_b"], p["sg_w"], p["sg_bias_t"], p["w_br_a"], p["w_br_b"], p["w_br_c"],
               p["w_out"], p["ln_g0"], p["ln_b0"]]
    return pl.pallas_call(
        functools.partial(_mixer_out_kernel, alpha=alpha),
        grid=(n // tm,),
        in_specs=[rows(D_MODEL)] + att_specs
                 + [rows(2 * SG_WIDTH), rows(ML_PAD_WIDTH), rows(N_BRANCH * D_MODEL)]
                 + [whole(w) for w in weights],
        out_specs=[rows(D_MODEL), rows(D_MODEL)],
        out_shape=[jax.ShapeDtypeStruct((n, D_MODEL), F32), jax.ShapeDtypeStruct((n, D_MODEL), BF16)],
        scratch_shapes=[pltpu.VMEM((2 * len(ATT_GROUPS) * ATT_GROUP_WIDTH // LANES, tm, LANES), F32)],
        compiler_params=_cparams("parallel"),
        name="mixer_out",
    )(x, *att_args, sg, hc, gpre, *weights)


def _swiglu_tile(x, w1_ref, w3_ref, w2_ref, prod, tf):
    for c in range(D_FF // tf):
        cols = slice(c * tf, (c + 1) * tf)
        h1 = _dot(x, w1_ref[:, cols])
        h3 = _dot(x, w3_ref[:, cols])
        prod[:, cols] = (h1 * _sigmoid(h1) * h3).astype(BF16)
    return _dot(prod[...], w2_ref[...])


def _ffn_kernel(xf_ref, xb_ref, w1_ref, w3_ref, w2_ref, lng_ref, lnb_ref, of_ref, ob_ref, prod, *, alpha, tf):
    f = _swiglu_tile(xb_ref[...], w1_ref, w3_ref, w2_ref, prod, tf)
    out = _layer_norm(alpha * xf_ref[...] + f, lng_ref[...], lnb_ref[...])
    of_ref[...] = out
    ob_ref[...] = out.astype(BF16)


def _ffn(xf, xb, w1, w3, w2, ln_g, ln_b, alpha, tm=512, tf=256):
    n = xf.shape[0]

    def resident(shape):
        return pl.BlockSpec(shape, lambda i: (0, 0), pipeline_mode=pl.Buffered(1))

    return pl.pallas_call(
        functools.partial(_ffn_kernel, alpha=alpha, tf=tf),
        grid=(n // tm,),
        in_specs=[pl.BlockSpec((tm, D_MODEL), lambda i: (i, 0)),
                  pl.BlockSpec((tm, D_MODEL), lambda i: (i, 0)),
                  resident((D_MODEL, D_FF)), resident((D_MODEL, D_FF)), resident((D_FF, D_MODEL)),
                  pl.BlockSpec((1, D_MODEL), lambda i: (0, 0)),
                  pl.BlockSpec((1, D_MODEL), lambda i: (0, 0))],
        out_specs=[pl.BlockSpec((tm, D_MODEL), lambda i: (i, 0))] * 2,
        out_shape=[jax.ShapeDtypeStruct((n, D_MODEL), F32), jax.ShapeDtypeStruct((n, D_MODEL), BF16)],
        scratch_shapes=[pltpu.VMEM((tm, D_FF), BF16)],
        compiler_params=_cparams("parallel"),
        name="ffn",
    )(xf, xb, w1, w3, w2, ln_g, ln_b)


def _router_kernel(x_ref, w_ref, b_ref, info_ref):
    xh, xm, _ = _split3(x_ref[...])
    wh, wm, _ = _split3(w_ref[...])
    logits = (_dot(xh, wh) + (_dot(xh, wm) + _dot(xm, wh))) + b_ref[...]
    lane = lax.broadcasted_iota(jnp.int32, logits.shape, 1)
    neg_inf = jnp.float32(-jnp.inf)
    logits = jnp.where(lane < N_EXPERTS, logits, neg_inf)
    v1 = jnp.max(logits, axis=-1, keepdims=True)
    i1 = jnp.min(jnp.where(logits == v1, lane, LANES), axis=-1, keepdims=True)
    rest = jnp.where(lane == i1, neg_inf, logits)
    v2 = jnp.max(rest, axis=-1, keepdims=True)
    i2 = jnp.min(jnp.where(rest == v2, lane, LANES), axis=-1, keepdims=True)
    e = jnp.exp(v2 - v1)
    inv = 1.0 / (1.0 + e)
    info_ref[...] = jnp.where(lane == 0, i1.astype(F32),
                              jnp.where(lane == 1, i2.astype(F32),
                                        jnp.where(lane == 2, inv, jnp.where(lane == 3, e * inv, 0.0))))


def _router(xf, router_w, router_b, tm=1024):
    n = xf.shape[0]
    w = jnp.pad(router_w, ((0, 0), (0, LANES - N_EXPERTS)))
    b = jnp.pad(router_b, (0, LANES - N_EXPERTS))[None, :]
    return pl.pallas_call(
        _router_kernel,
        grid=(n // tm,),
        in_specs=[pl.BlockSpec((tm, D_MODEL), lambda i: (i, 0)),
                  pl.BlockSpec((D_MODEL, LANES), lambda i: (0, 0)),
                  pl.BlockSpec((1, LANES), lambda i: (0, 0))],
        out_specs=pl.BlockSpec((tm, LANES), lambda i: (i, 0)),
        out_shape=jax.ShapeDtypeStruct((n, LANES), F32),
        compiler_params=_cparams("parallel"),
        name="router",
    )(xf, w, b)


def _routing_plan(info, n):
    tile = EXPERT_TILE
    n_tiles = TOP_K * n // tile + N_EXPERTS
    expert = info[:, 0:TOP_K].astype(jnp.int32).reshape(-1)
    onehot = (expert[:, None] == jnp.arange(N_EXPERTS, dtype=jnp.int32)[None, :]).astype(jnp.int32)
    running = jnp.cumsum(onehot, axis=0)
    rank = jnp.sum(onehot * running, axis=1) - 1
    counts = running[-1]
    padded = ((counts + tile - 1) // tile) * tile
    ends = jnp.cumsum(padded)
    starts = ends - padded
    dest = jnp.sum(onehot * starts[None, :], axis=1) + rank
    used = ends[-1] // tile
    tile_start = jnp.arange(n_tiles, dtype=jnp.int32) * tile
    tile_expert = jnp.sum((tile_start[:, None] >= ends[None, :]).astype(jnp.int32), axis=1)
    last_expert = jnp.sum((tile_start[used - 1] >= ends).astype(jnp.int32))
    tile_expert = jnp.minimum(tile_expert, last_expert)
    return dest.astype(jnp.int32), tile_expert.astype(jnp.int32), used.astype(jnp.int32).reshape(1), n_tiles


def _row_copy(src, src_row, dst, dst_row, sem):
    return pltpu.make_async_copy(src.at[pl.ds(src_row, 1), :], dst.at[pl.ds(dst_row, 1), :], sem)


def _rows_wait(src, dst, rows, sem):
    pltpu.make_async_copy(src.at[pl.ds(0, rows), :], dst.at[pl.ds(0, rows), :], sem).wait()


DISPATCH_SLOTS = 3


def _dispatch_kernel(dest_ref, x_ref, zero_ref, xs_ref, buf, load_sem, row_sem, *, tm):
    del zero_ref
    i = pl.program_id(0)
    steps = pl.num_programs(0)
    slot = i % DISPATCH_SLOTS
    prev_slot = (i + DISPATCH_SLOTS - 1) % DISPATCH_SLOTS

    def load(tile, into):
        return pltpu.make_async_copy(x_ref.at[pl.ds(tile * tm, tm), :], buf.at[into], load_sem.at[into])

    def wait_rows(of_slot):
        for _ in range(TOP_K):
            pltpu.make_async_copy(buf.at[of_slot], xs_ref.at[pl.ds(0, tm), :], row_sem.at[of_slot]).wait()

    @pl.when(i == 0)
    def _():
        load(0, 0).start()

        @pl.when(steps > 1)
        def _():
            load(1, 1).start()

    load(i, slot).wait()

    def start(t, carry):
        for c in range(TOP_K):
            _row_copy(buf.at[slot], t, xs_ref, dest_ref[0, 0, TOP_K * t + c], row_sem.at[slot]).start()
        return carry

    lax.fori_loop(0, tm, start, 0, unroll=DMA_ISSUE_UNROLL)

    @pl.when(i > 0)
    def _():
        wait_rows(prev_slot)

    @pl.when(i + 2 < steps)
    def _():
        load(i + 2, prev_slot).start()

    @pl.when(i == steps - 1)
    def _():
        wait_rows(slot)


def _dispatch(xf, dest, n_slots, tm=512):
    n = xf.shape[0]
    dest3 = dest.reshape(n // tm, 1, TOP_K * tm)
    zeros = jnp.zeros((n_slots, D_MODEL), F32)
    return pl.pallas_call(
        functools.partial(_dispatch_kernel, tm=tm),
        grid=(n // tm,),
        in_specs=[pl.BlockSpec((1, 1, TOP_K * tm), lambda i: (i, 0, 0), memory_space=pltpu.SMEM),
                  pl.BlockSpec(memory_space=pl.ANY),
                  pl.BlockSpec(memory_space=pl.ANY)],
        out_specs=pl.BlockSpec(memory_space=pl.ANY),
        out_shape=jax.ShapeDtypeStruct((n_slots, D_MODEL), F32),
        scratch_shapes=[pltpu.VMEM((DISPATCH_SLOTS, tm, D_MODEL), F32),
                        pltpu.SemaphoreType.DMA((DISPATCH_SLOTS,)),
                        pltpu.SemaphoreType.DMA((DISPATCH_SLOTS,))],
        input_output_aliases={2: 0},
        compiler_params=_cparams("arbitrary"),
        name="moe_dispatch",
    )(dest3, xf, zeros)


def _experts_kernel(te_ref, used_ref, xs_ref, w1_ref, w3_ref, w2_ref, ys_ref, prod, *, tf):
    del te_ref
    i = pl.program_id(0)

    @pl.when(i < used_ref[0])
    def _():
        ys_ref[...] = _swiglu_tile(xs_ref[...].astype(BF16), w1_ref, w3_ref, w2_ref, prod, tf)

    @pl.when(i >= used_ref[0])
    def _():
        ys_ref[...] = jnp.zeros(ys_ref.shape, ys_ref.dtype)


def _experts(xs, tile_expert, used, w1, w3, w2, n_tiles, tf=256):
    tile = EXPERT_TILE
    grid_spec = pltpu.PrefetchScalarGridSpec(
        num_scalar_prefetch=2,
        grid=(n_tiles,),
        in_specs=[pl.BlockSpec((tile, D_MODEL), lambda i, te, nu: (i, 0)),
                  pl.BlockSpec((None, D_MODEL, D_FF), lambda i, te, nu: (te[i], 0, 0)),
                  pl.BlockSpec((None, D_MODEL, D_FF), lambda i, te, nu: (te[i], 0, 0)),
                  pl.BlockSpec((None, D_FF, D_MODEL), lambda i, te, nu: (te[i], 0, 0))],
        out_specs=pl.BlockSpec((tile, D_MODEL), lambda i, te, nu: (i, 0)),
        scratch_shapes=[pltpu.VMEM((tile, D_FF), BF16)],
    )
    return pl.pallas_call(
        functools.partial(_experts_kernel, tf=tf),
        grid_spec=grid_spec,
        out_shape=jax.ShapeDtypeStruct((n_tiles * tile, D_MODEL), F32),
        compiler_params=_cparams("arbitrary"),
        name="moe_experts",
    )(tile_expert, used, xs, w1, w3, w2)


def _combine_kernel(dest_ref, next_dest_ref, xf_ref, info_ref, ys_ref, lng_ref, lnb_ref, of_ref, got, sem, *, alpha):
    tm = xf_ref.shape[0]
    i = pl.program_id(0)
    slot = i % 2

    def gather(dests, into):
        def start(t, carry):
            for c in range(TOP_K):
                _row_copy(ys_ref, dests[0, 0, TOP_K * t + c], got.at[into, c], t, sem.at[into]).start()
            return carry
        lax.fori_loop(0, tm, start, 0, unroll=DMA_ISSUE_UNROLL)

    @pl.when(i == 0)
    def _():
        gather(dest_ref, 0)

    @pl.when(i + 1 < pl.num_programs(0))
    def _():
        gather(next_dest_ref, 1 - slot)

    for c in range(TOP_K):
        _rows_wait(ys_ref, got.at[slot, c], tm, sem.at[slot])
    info = info_ref[...]
    mix = info[:, 2:3] * got[slot, 0] + info[:, 3:4] * got[slot, 1]
    of_ref[...] = _layer_norm(alpha * xf_ref[...] + mix, lng_ref[...], lnb_ref[...])


def _combine(xf, info, ys, dest, ln_g, ln_b, alpha, tm=256):
    n = xf.shape[0]
    steps = n // tm
    dest3 = dest.reshape(steps, 1, TOP_K * tm)
    return pl.pallas_call(
        functools.partial(_combine_kernel, alpha=alpha),
        grid=(steps,),
        in_specs=[pl.BlockSpec((1, 1, TOP_K * tm), lambda i: (i, 0, 0), memory_space=pltpu.SMEM),
                  pl.BlockSpec((1, 1, TOP_K * tm), lambda i: (jnp.minimum(i + 1, steps - 1), 0, 0),
                               memory_space=pltpu.SMEM),
                  pl.BlockSpec((tm, D_MODEL), lambda i: (i, 0)),
                  pl.BlockSpec((tm, LANES), lambda i: (i, 0)),
                  pl.BlockSpec(memory_space=pl.ANY),
                  pl.BlockSpec((1, D_MODEL), lambda i: (0, 0)),
                  pl.BlockSpec((1, D_MODEL), lambda i: (0, 0))],
        out_specs=pl.BlockSpec((tm, D_MODEL), lambda i: (i, 0)),
        out_shape=jax.ShapeDtypeStruct((n, D_MODEL), F32),
        scratch_shapes=[pltpu.VMEM((2, TOP_K, tm, D_MODEL), F32), pltpu.SemaphoreType.DMA((2,))],
        compiler_params=_cparams("arbitrary"),
        name="moe_combine",
    )(dest3, dest3, xf, info, ys, ln_g, ln_b)


def _moe(xf, router_w, router_b, w1, w3, w2, ln_g, ln_b, alpha):
    n = xf.shape[0]
    info = _router(xf, router_w, router_b)
    dest, tile_expert, used, n_tiles = _routing_plan(info, n)
    xs = _dispatch(xf, dest, n_tiles * EXPERT_TILE)
    ys = _experts(xs, tile_expert, used, w1, w3, w2, n_tiles)
    return _combine(xf, info, ys, dest, ln_g, ln_b, alpha)


def _pad_heads(w):
    lead = w.shape[:-1]
    w = w.reshape(*lead, ML_HEADS, ML_HEAD_DIM)
    w = jnp.pad(w, [(0, 0)] * len(lead) + [(0, 0), (0, ML_HEAD_PAD - ML_HEAD_DIM)])
    return w.reshape(*lead, ML_PAD_WIDTH)


def _group_qkv(w, group):
    gw = ATT_GROUP_WIDTH
    return jnp.concatenate([w[..., part * ATT_WIDTH + group * gw:part * ATT_WIDTH + (group + 1) * gw]
                            for part in range(3)], axis=-1)


def _layer_params(layer, w_in, b_in, conv_w, conv_b, sg_ln_g, sg_ln_b, sg_w, sg_b,
                  w_br_a, w_br_b, w_br_c, w_out, ln_g, ln_b):
    w, b = w_in[layer], b_in[layer]
    o_att, o_sg, o_qk, o_v, o_o, o_i, o_g = 0, 2304, 3840, 5376, 6144, 6912, 6920
    ones_col = jnp.zeros((ML_HEADS, ML_HEAD_PAD), F32).at[:, ML_HEAD_DIM].set(1.0).reshape(ML_PAD_WIDTH)
    w_ml = jnp.concatenate([_pad_heads(w[:, o_qk:o_qk + ML_WIDTH]), _pad_heads(w[:, o_qk + ML_WIDTH:o_v]),
                            _pad_heads(w[:, o_v:o_o])], axis=1)
    b_ml = jnp.concatenate([_pad_heads(b[o_qk:o_qk + ML_WIDTH]), _pad_heads(b[o_qk + ML_WIDTH:o_v]),
                            _pad_heads(b[o_v:o_o]) + ones_col])
    cw, cb = conv_w[layer], conv_b[layer]
    w_att, b_att = w[:, o_att:o_sg], b[o_att:o_sg]
    return dict(
        w_att=[_group_qkv(w_att, g).astype(BF16) for g in range(len(ATT_GROUPS))],
        b_att=[_group_qkv(b_att, g)[None, :] for g in range(len(ATT_GROUPS))],
        w_sg=w[:, o_sg:o_qk].astype(BF16), b_sg=b[None, o_sg:o_qk],
        w_ml=w_ml.astype(BF16), b_ml=b_ml[None, :],
        w_mlo=_pad_heads(w[:, o_o:o_i]).astype(BF16), b_mlo=_pad_heads(b[o_o:o_i])[None, :],
        w_if=jnp.pad(w[:, o_i:o_g], ((0, 0), (0, LANES - 2 * ML_HEADS))).astype(BF16),
        b_if=jnp.pad(b[o_i:o_g], (0, LANES - 2 * ML_HEADS))[None, :],
        w_g=w[:, o_g:].astype(BF16), b_g=b[None, o_g:],
        conv_w=jnp.concatenate([_pad_heads(cw[:, :ML_WIDTH]), _pad_heads(cw[:, ML_WIDTH:])], axis=1),
        conv_b=jnp.concatenate([_pad_heads(cb[:ML_WIDTH]), _pad_heads(cb[ML_WIDTH:])])[None, :],
        sg_ln_g=sg_ln_g[layer][None, :], sg_ln_b=sg_ln_b[layer][None, :],
        sg_w=sg_w[layer], sg_bias_t=sg_b[layer].T,
        w_br_a=w_br_a[layer].astype(BF16), w_br_b=w_br_b[layer].astype(BF16),
        w_br_c=_pad_heads(w_br_c[layer].T).T.astype(BF16),
        w_out=w_out[layer].astype(BF16),
        ln_g0=ln_g[layer, 0][None, :], ln_b0=ln_b[layer, 0][None, :],
        ln_g1=ln_g[layer, 1][None, :], ln_b1=ln_b[layer, 1][None, :],
    )


def kernel(x, positions, w_in, b_in, conv_w, conv_b, sg_ln_g, sg_ln_b, sg_w, sg_b, w_br_a, w_br_b, w_br_c, w_out,
           ln_g, ln_b, ffn_w1, ffn_w3, ffn_w2, router_w, router_b, moe_w1, moe_w3, moe_w2):
    batch, seq, _ = x.shape
    n = batch * seq
    depth = w_in.shape[0]
    alpha = (2.0 * depth) ** 0.25
    cos, sin = _rope_tables(positions)
    xf = x.reshape(n, D_MODEL)
    xb = xf.astype(BF16)
    for layer in range(depth):
        p = _layer_params(layer, w_in, b_in, conv_w, conv_b, sg_ln_g, sg_ln_b, sg_w, sg_b,
                          w_br_a, w_br_b, w_br_c, w_out, ln_g, ln_b)
        sg = _linear(xb, p["w_sg"], p["b_sg"], BF16, 2048, 768, act=_gelu_tanh)
        ml = _linear(xb, p["w_ml"], p["b_ml"], BF16, 2048, ML_PAD_WIDTH)
        og = _linear(xb, p["w_mlo"], p["b_mlo"], BF16, 2048, ML_PAD_WIDTH, act=_sigmoid)
        gates_if = _linear(xb, p["w_if"], p["b_if"], F32, 2048, LANES)
        gbr = _linear(xb, p["w_g"], p["b_g"], BF16, 2048, 1024, act=_sigmoid)
        att_outs = []
        for g, (_, dil) in enumerate(ATT_GROUPS):
            qkv = _linear_dilated(xb, p["w_att"][g], p["b_att"][g], dil, batch, seq)
            att_outs.append(_attention_group(qkv, cos, sin, g, batch, seq))
        hc = _mlstm(ml, og, gates_if, p["conv_w"], p["conv_b"], batch, seq)
        xf, xb = _mixer_out(xf, att_outs, sg, hc, gbr, p, alpha, seq)
        j = layer // 2
        if layer % 2 == 0:
            xf, xb = _ffn(xf, xb, ffn_w1[j].astype(BF16), ffn_w3[j].astype(BF16), ffn_w2[j].astype(BF16),
                          p["ln_g1"], p["ln_b1"], alpha)
        else:
            xf = _moe(xf, router_w[j], router_b[j], moe_w1[j].astype(BF16), moe_w3[j].astype(BF16),
                      moe_w2[j].astype(BF16), p["ln_g1"], p["ln_b1"], alpha)
            xb = xf.astype(BF16)
    return xf.reshape(batch, seq, D_MODEL)
```

```python
import functools

import numpy as np
import jax
import jax.numpy as jnp
from jax import lax
from jax.experimental import pallas as pl
from jax.experimental.pallas import tpu as pltpu

F32 = jnp.float32
BF16 = jnp.bfloat16

D_MODEL = 1024
ATT_GROUPS = ((128, 1), (512, 4), (2048, 16))
ATT_HEADS_PER_GROUP = 4
ATT_HEAD_DIM = 64
ATT_WIDTH = 768
ATT_GROUP_WIDTH = ATT_HEADS_PER_GROUP * ATT_HEAD_DIM
ROPE_THETA = 10000.0
SG_CHUNK = 128
SG_GROUPS = 6
SG_WIDTH = 768
ML_HEADS = 4
ML_HEAD_DIM = 192
ML_HEAD_PAD = 256
ML_WIDTH = 768
ML_PAD_WIDTH = ML_HEADS * ML_HEAD_PAD
ML_CHUNK = 128
ML_CONV = 4
N_BRANCH = 3
D_FF = 2816
N_EXPERTS = 8
TOP_K = 2
LN_EPS = 1e-5
BLOCK = 128
LANES = 128
VMEM_LIMIT = 56 * 1024 * 1024
EXPERT_TILE = 512
DMA_ISSUE_UNROLL = 8
MIXER_SUB_ROWS = 256


def _cparams(*sem):
    return pltpu.CompilerParams(dimension_semantics=sem, vmem_limit_bytes=VMEM_LIMIT)


def _layer_norm(x, g, b):
    mu = jnp.mean(x, axis=-1, keepdims=True)
    xc = x - mu
    var = jnp.mean(xc * xc, axis=-1, keepdims=True)
    return xc * lax.rsqrt(var + LN_EPS) * g + b


def _gelu_tanh(x):
    return 0.5 * x * (1.0 + jnp.tanh(np.sqrt(2.0 / np.pi) * (x + 0.044715 * (x * x * x))))


def _sigmoid(x):
    return 0.5 * (1.0 + jnp.tanh(0.5 * x))


def _dot(a, b):
    return jnp.dot(a, b, preferred_element_type=F32)


def _dot_nt(a, b):
    return lax.dot_general(a, b, (((1,), (1,)), ((), ())), preferred_element_type=F32)


def _dot_tn(a, b):
    return lax.dot_general(a, b, (((0,), (0,)), ((), ())), preferred_element_type=F32)


def _split3(x):
    hi = x.astype(BF16)
    r1 = x - hi.astype(F32)
    mid = r1.astype(BF16)
    lo = (r1 - mid.astype(F32)).astype(BF16)
    return hi, mid, lo


def _linear_kernel(x_ref, w_ref, b_ref, o_ref, *, act):
    res = _dot(x_ref[...], w_ref[...]) + b_ref[...]
    o_ref[...] = (res if act is None else act(res)).astype(o_ref.dtype)


def _linear(x, w, b, out_dtype, tm, tn, act=None):
    m, k = x.shape
    n = w.shape[1]
    return pl.pallas_call(
        functools.partial(_linear_kernel, act=act),
        grid=(m // tm, n // tn),
        in_specs=[pl.BlockSpec((tm, k), lambda i, j: (i, 0)),
                  pl.BlockSpec((k, tn), lambda i, j: (0, j)),
                  pl.BlockSpec((1, tn), lambda i, j: (0, j))],
        out_specs=pl.BlockSpec((tm, tn), lambda i, j: (i, j)),
        out_shape=jax.ShapeDtypeStruct((m, n), out_dtype),
        compiler_params=_cparams("parallel", "parallel"),
        name="linear",
    )(x, w, b)


def _linear_dilated_kernel(x_ref, w_ref, b_ref, o_ref, acc, *, dil):
    res = _dot(x_ref[...], w_ref[...]) + b_ref[...]
    slabs, rows, _ = acc.shape
    for c in range(slabs):
        acc[c] = res[:, c * LANES:(c + 1) * LANES]
    sub = rows // dil
    for r in range(dil):
        for c in range(slabs):
            o_ref[r, :, c * LANES:(c + 1) * LANES] = acc[c, pl.ds(r, sub, stride=dil), :].astype(o_ref.dtype)


def _linear_dilated(x, w, b, dil, batch, seq):
    k = x.shape[1]
    n = w.shape[1]
    tn = ATT_GROUP_WIDTH
    sub = seq // dil
    return pl.pallas_call(
        functools.partial(_linear_dilated_kernel, dil=dil),
        grid=(batch, n // tn),
        in_specs=[pl.BlockSpec((seq, k), lambda i, j: (i, 0)),
                  pl.BlockSpec((k, tn), lambda i, j: (0, j)),
                  pl.BlockSpec((1, tn), lambda i, j: (0, j))],
        out_specs=pl.BlockSpec((None, dil, sub, tn), lambda i, j: (i, 0, 0, j)),
        out_shape=jax.ShapeDtypeStruct((batch, dil, sub, n), BF16),
        scratch_shapes=[pltpu.VMEM((tn // LANES, seq, LANES), F32)],
        compiler_params=_cparams("parallel", "parallel"),
        name=f"linear_dil{dil}",
    )(x, w, b)


def _rope_table_kernel(pos_ref, freq_ref, sign_ref, cos_ref, sin_ref):
    ang = pos_ref[...] * freq_ref[...]
    cos_ref[...] = jnp.cos(ang)
    sin_ref[...] = jnp.sin(ang) * sign_ref[...]


def _rope_tables(positions):
    n = positions.size
    half = ATT_HEAD_DIM // 2
    freqs = ROPE_THETA ** (-jnp.arange(half, dtype=F32) * (2.0 / ATT_HEAD_DIM))
    lane = np.arange(LANES)
    freq_row = freqs[lane % half][None, :]
    sign_row = jnp.asarray(np.where(lane % ATT_HEAD_DIM < half, -1.0, 1.0), F32)[None, :]
    pos = positions.astype(F32).reshape(n, 1)
    tm = 2048
    return pl.pallas_call(
        _rope_table_kernel,
        grid=(n // tm,),
        in_specs=[pl.BlockSpec((tm, 1), lambda i: (i, 0)),
                  pl.BlockSpec((1, LANES), lambda i: (0, 0)),
                  pl.BlockSpec((1, LANES), lambda i: (0, 0))],
        out_specs=[pl.BlockSpec((tm, LANES), lambda i: (i, 0))] * 2,
        out_shape=[jax.ShapeDtypeStruct((n, LANES), F32)] * 2,
        compiler_params=_cparams("parallel"),
        name="rope_tables",
    )(pos, freq_row, sign_row)


def _attn_kernel(q_ref, k_ref, v_ref, cos_ref, sin_ref, o_ref, lse_ref, qs, ks, vs, *, sub, dil):
    nb = sub // BLOCK
    res = pl.program_id(1)
    lane = lax.broadcasted_iota(jnp.int32, (BLOCK, LANES), 1)
    first_half = (lane % ATT_HEAD_DIM) < (ATT_HEAD_DIM // 2)
    low_head = lane < ATT_HEAD_DIM

    def rope(x, c, s):
        partner = jnp.where(first_half, pltpu.roll(x, LANES - 32, 1), pltpu.roll(x, 32, 1))
        return x * c + partner * s

    ks[0:BLOCK, :] = jnp.zeros((BLOCK, ATT_GROUP_WIDTH), BF16)
    vs[0:BLOCK, :] = jnp.zeros((BLOCK, ATT_GROUP_WIDTH), BF16)

    def prep(n, carry):
        r0 = pl.multiple_of(n * BLOCK, BLOCK)
        if dil == 1:
            c = cos_ref[pl.ds(r0, BLOCK), :]
            s = sin_ref[pl.ds(r0, BLOCK), :]
        else:
            c = cos_ref[pl.ds(r0 * dil + res, BLOCK, stride=dil), :]
            s = sin_ref[pl.ds(r0 * dil + res, BLOCK, stride=dil), :]
        for pair in range(2):
            cols = slice(pair * LANES, (pair + 1) * LANES)
            q = rope(q_ref[pl.ds(r0, BLOCK), cols].astype(F32), c, s) * (ATT_HEAD_DIM ** -0.5)
            qs[pl.ds(r0, BLOCK), (2 * pair) * LANES:(2 * pair + 1) * LANES] = jnp.where(low_head, q, 0.0).astype(BF16)
            qs[pl.ds(r0, BLOCK), (2 * pair + 1) * LANES:(2 * pair + 2) * LANES] = jnp.where(low_head, 0.0, q).astype(BF16)
            k = rope(k_ref[pl.ds(r0, BLOCK), cols].astype(F32), c, s)
            ks[pl.ds(r0 + BLOCK, BLOCK), cols] = k.astype(BF16)
        vs[pl.ds(r0 + BLOCK, BLOCK), :] = v_ref[pl.ds(r0, BLOCK), :]
        return carry

    lax.fori_loop(0, nb, prep, 0)

    qi = lax.broadcasted_iota(jnp.int32, (BLOCK, 2 * BLOCK), 0)
    ki = lax.broadcasted_iota(jnp.int32, (BLOCK, 2 * BLOCK), 1)
    dist = qi + BLOCK - ki
    band = (dist >= 0) & (dist <= BLOCK)
    neg_inf = jnp.float32(-jnp.inf)
    bias_rest = jnp.where(band, 0.0, neg_inf)
    bias_first = jnp.where(band & (ki >= BLOCK), 0.0, neg_inf)

    def block(n, bias):
        r0 = pl.multiple_of(n * BLOCK, BLOCK)
        heads = range(ATT_HEADS_PER_GROUP)
        pair_cols = [slice(pair * LANES, (pair + 1) * LANES) for pair in range(2)]
        kw = [ks[pl.ds(r0, 2 * BLOCK), c] for c in pair_cols]
        vw = [vs[pl.ds(r0, 2 * BLOCK), c] for c in pair_cols]
        s = [_dot_nt(qs[pl.ds(r0, BLOCK), hd * LANES:(hd + 1) * LANES], kw[hd // 2]) + bias for hd in heads]
        m = [jnp.max(s[hd], axis=-1, keepdims=True) for hd in heads]
        p = [jnp.exp(s[hd] - m[hd]) for hd in heads]
        den = [jnp.sum(p[hd], axis=-1, keepdims=True) for hd in heads]
        o = [_dot(p[hd].astype(BF16), vw[hd // 2]) * (1.0 / den[hd]) for hd in heads]
        lse = [m[hd] + jnp.log(den[hd]) for hd in heads]
        for pair in range(2):
            lo, hi = 2 * pair, 2 * pair + 1
            o_ref[pl.ds(r0, BLOCK), pair_cols[pair]] = jnp.where(low_head, o[lo], o[hi]).astype(o_ref.dtype)
            lse_ref[pl.ds(r0, BLOCK), pair_cols[pair]] = jnp.where(low_head, lse[lo], lse[hi])

    block(0, bias_first)
    if nb <= 2:
        for n in range(1, nb):
            block(n, bias_rest)
        return

    def body(n, carry):
        block(n, bias_rest)
        return carry

    lax.fori_loop(1, nb, body, 0, unroll=2)


def _attention_group(qkv, cos, sin, group, batch, seq):
    window, dil = ATT_GROUPS[group]
    assert window // dil == BLOCK
    sub = seq // dil
    assert sub % BLOCK == 0

    def part(p):
        return pl.BlockSpec((None, None, sub, ATT_GROUP_WIDTH), lambda b, r: (b, r, 0, p))

    tab = pl.BlockSpec((seq, LANES), lambda b, r: (b, 0))
    out = pl.BlockSpec((None, None, sub, ATT_GROUP_WIDTH), lambda b, r: (b, r, 0, 0))
    return pl.pallas_call(
        functools.partial(_attn_kernel, sub=sub, dil=dil),
        grid=(batch, dil),
        in_specs=[part(0), part(1), part(2), tab, tab],
        out_specs=[out, out],
        out_shape=[jax.ShapeDtypeStruct((batch, dil, sub, ATT_GROUP_WIDTH), BF16),
                   jax.ShapeDtypeStruct((batch, dil, sub, ATT_GROUP_WIDTH), F32)],
        scratch_shapes=[pltpu.VMEM((sub, ATT_HEADS_PER_GROUP * LANES), BF16),
                        pltpu.VMEM((sub + BLOCK, ATT_GROUP_WIDTH), BF16),
                        pltpu.VMEM((sub + BLOCK, ATT_GROUP_WIDTH), BF16)],
        compiler_params=_cparams("parallel", "parallel"),
        name=f"attention_g{group}",
    )(qkv, qkv, qkv, cos, sin)


ML_SEQS_PER_STEP = 2


def _mlstm_kernel(qk_ref, v_ref, og_ref, g_ref, cw_ref, cb_ref, sc_ref, out_ref, frame, ct, m_s):
    taps = ML_CONV

    @pl.when(pl.program_id(1) == 0)
    def _():
        frame[:, 0:ML_CHUNK, :] = jnp.zeros((frame.shape[0], ML_CHUNK, 2 * ML_PAD_WIDTH), BF16)
        ct[...] = jnp.zeros(ct.shape, F32)
        m_s[...] = jnp.zeros(m_s.shape, F32)

    out_row = lax.broadcasted_iota(jnp.int32, (ML_CHUNK, 2 * ML_CHUNK), 0)
    src_row = lax.broadcasted_iota(jnp.int32, (ML_CHUNK, 2 * ML_CHUNK), 1)
    picks = [jnp.where(src_row == out_row + (ML_CHUNK - (taps - 1 - j)), 1.0, 0.0).astype(BF16)
             for j in range(taps - 1)]
    row = lax.broadcasted_iota(jnp.int32, (ML_CHUNK, ML_CHUNK), 0)
    colm = lax.broadcasted_iota(jnp.int32, (ML_CHUNK, ML_CHUNK), 1)
    causal = row >= colm
    tri = jnp.where(causal, 1.0, 0.0).astype(BF16)
    neg_inf = jnp.float32(-jnp.inf)

    seqs = range(qk_ref.shape[0])
    pairs = [(s, h) for s in seqs for h in range(ML_HEADS)]

    def head_cols(h):
        return slice(h * ML_HEAD_PAD, (h + 1) * ML_HEAD_PAD)

    act, gates, bcum, bcum_t, gates_t = {}, {}, {}, {}, {}
    for s in seqs:
        cur = qk_ref[s]
        frame[s, ML_CHUNK:2 * ML_CHUNK, :] = cur
        both = frame[s]
        acc = cb_ref[...] + cur.astype(F32) * cw_ref[taps - 1:taps, :]
        for j in range(taps - 1):
            acc = acc + _dot(picks[j], both) * cw_ref[j:j + 1, :]
        frame[s, 0:ML_CHUNK, :] = cur
        act[s] = (acc * _sigmoid(acc) * sc_ref[...]).astype(BF16)
        gates[s] = g_ref[s]
        log_f = -(jnp.maximum(-gates[s], 0.0) + jnp.log1p(jnp.exp(-jnp.abs(gates[s]))))
        hi, mid, lo = _split3(log_f)
        bcum[s] = _dot(tri, hi) + _dot(tri, mid) + _dot(tri, lo)
        bcum_t[s] = bcum[s].T
        gates_t[s] = gates[s].T

    qh, kh, vh, c_old, m_prev, b_c, i_c, log_d, m_inter, m = ({} for _ in range(10))
    for p in pairs:
        s, h = p
        qh[p] = act[s][:, head_cols(h)]
        kh[p] = act[s][:, ML_PAD_WIDTH + h * ML_HEAD_PAD:ML_PAD_WIDTH + (h + 1) * ML_HEAD_PAD]
        vh[p] = v_ref[s, :, head_cols(h)]
        c_old[p] = ct[s, h]
        m_prev[p] = m_s[s, h, 0:1, 0:1]
        b_c[p] = bcum[s][:, ML_HEADS + h:ML_HEADS + h + 1]
        b_r = bcum_t[s][ML_HEADS + h:ML_HEADS + h + 1, :]
        i_c[p] = gates[s][:, h:h + 1]
        i_r = gates_t[s][h:h + 1, :]
        log_d[p] = jnp.where(causal, b_c[p] - b_r + i_r, neg_inf)
        m_inter[p] = b_c[p] + m_prev[p]
        m[p] = jnp.maximum(m_inter[p], jnp.max(log_d[p], axis=-1, keepdims=True))

    scores = {p: _dot_nt(qh[p], kh[p]) for p in pairs}
    carried = {p: _dot(qh[p], c_old[p].astype(BF16)) for p in pairs}
    sc = {p: (scores[p] * jnp.exp(log_d[p] - m[p])).astype(BF16) for p in pairs}
    num = {p: _dot(sc[p], vh[p]) + jnp.exp(m_inter[p] - m[p]) * carried[p] for p in pairs}
    for p in pairs:
        s, h = p
        den = num[p][:, ML_HEAD_DIM:ML_HEAD_DIM + 1]
        hid = num[p] / jnp.maximum(jnp.abs(den), jnp.exp(-m[p]))
        out_ref[s, :, head_cols(h)] = (og_ref[s, :, head_cols(h)].astype(F32) * hid).astype(out_ref.dtype)
    for p in pairs:
        s, h = p
        b_last = b_c[p][ML_CHUNK - 1:ML_CHUNK, :]
        m_new = m[p][ML_CHUNK - 1:ML_CHUNK, :]
        w = jnp.exp(b_last - b_c[p] + i_c[p] - m_new)
        decay = jnp.exp(b_last + m_prev[p] - m_new)
        wv = (w * vh[p].astype(F32)).astype(BF16)
        ct[s, h] = decay * c_old[p] + _dot_tn(kh[p], wv)
        m_s[s, h] = jnp.broadcast_to(m_new, m_s.shape[2:])


def _mlstm(ml, og, gates, conv_w, conv_b, qk_scale, batch, seq):
    nc = seq // ML_CHUNK
    group = ML_SEQS_PER_STEP if batch % ML_SEQS_PER_STEP == 0 else 1

    def whole(a):
        return pl.BlockSpec(a.shape, lambda b, c: (0, 0))

    def chunk(width, col):
        return pl.BlockSpec((group, ML_CHUNK, width), lambda b, c: (b, c, col))

    out = pl.pallas_call(
        _mlstm_kernel,
        grid=(batch // group, nc),
        in_specs=[chunk(2 * ML_PAD_WIDTH, 0), chunk(ML_PAD_WIDTH, 2), chunk(ML_PAD_WIDTH, 0), chunk(LANES, 0),
                  whole(conv_w), whole(conv_b), whole(qk_scale)],
        out_specs=chunk(ML_PAD_WIDTH, 0),
        out_shape=jax.ShapeDtypeStruct((batch, seq, ML_PAD_WIDTH), BF16),
        scratch_shapes=[pltpu.VMEM((group, 2 * ML_CHUNK, 2 * ML_PAD_WIDTH), BF16),
                        pltpu.VMEM((group, ML_HEADS, ML_HEAD_PAD, ML_HEAD_PAD), F32),
                        pltpu.VMEM((group, ML_HEADS, 8, LANES), F32)],
        compiler_params=_cparams("parallel", "arbitrary"),
        name="mlstm",
    )(ml.reshape(batch, seq, -1), ml.reshape(batch, seq, -1), og.reshape(batch, seq, -1),
      gates.reshape(batch, seq, -1), conv_w, conv_b, qk_scale)
    return out.reshape(batch * seq, ML_PAD_WIDTH)


def _mixer_out_kernel(x_ref, o0_ref, l0_ref, o1_ref, l1_ref, o2_ref, l2_ref, sg_ref, hc_ref, gp_ref,
                      sgg_ref, sgb_ref, sgw_ref, sgbias_ref, wa_ref, wb_ref, wc_ref, wo_ref, lng_ref, lnb_ref,
                      xf_ref, xb_ref, unperm, *, alpha):
    tm = x_ref.shape[0]
    slabs = ATT_GROUP_WIDTH // LANES
    att_refs = (l0_ref, l1_ref, l2_ref, o0_ref, o1_ref, o2_ref)

    for slot, ref in enumerate(att_refs):
        dil = ref.shape[0]
        if dil > 1:
            for r in range(dil):
                for c in range(slabs):
                    unperm[slabs * slot + c, pl.ds(r, tm // dil, stride=dil), :] = (
                        ref[r, :, c * LANES:(c + 1) * LANES].astype(F32))

    def token_order(slot, rows):
        ref = att_refs[slot]
        if ref.shape[0] == 1:
            return ref[0, rows, :].astype(F32)
        return jnp.concatenate([unperm[slabs * slot + c, rows, :] for c in range(slabs)], axis=1)

    row = lax.broadcasted_iota(jnp.int32, (SG_CHUNK, SG_CHUNK), 0)
    colm = lax.broadcasted_iota(jnp.int32, (SG_CHUNK, SG_CHUNK), 1)
    causal = row >= colm
    group_dim = SG_WIDTH // SG_GROUPS
    sg_w = [jnp.where(causal, sgw_ref[g], 0.0).astype(BF16) for g in range(SG_GROUPS)]

    for h in range(tm // MIXER_SUB_ROWS):
        rows = slice(h * MIXER_SUB_ROWS, (h + 1) * MIXER_SUB_ROWS)

        l0, l1, l2 = token_order(0, rows), token_order(1, rows), token_order(2, rows)
        lmax = jnp.maximum(jnp.maximum(l0, l1), l2)
        e0, e1, e2 = jnp.exp(l0 - lmax), jnp.exp(l1 - lmax), jnp.exp(l2 - lmax)
        inv = 1.0 / (e0 + e1 + e2)
        y_a = ((e0 * inv) * token_order(3, rows) + (e1 * inv) * token_order(4, rows)
               + (e2 * inv) * token_order(5, rows))

        u = sg_ref[rows, 0:SG_WIDTH].astype(F32)
        v = sg_ref[rows, SG_WIDTH:2 * SG_WIDTH].astype(F32)
        v = _layer_norm(v, sgg_ref[...], sgb_ref[...]).astype(BF16)
        chunks = []
        for c in range(MIXER_SUB_ROWS // SG_CHUNK):
            crows = slice(c * SG_CHUNK, (c + 1) * SG_CHUNK)
            parts = [_dot(sg_w[g], v[crows, g * group_dim:(g + 1) * group_dim]) + sgbias_ref[:, g:g + 1]
                     for g in range(SG_GROUPS)]
            chunks.append(jnp.concatenate(parts, axis=1))
        y_b = u * jnp.concatenate(chunks, axis=0)

        def gate(i):
            return gp_ref[rows, i * D_MODEL:(i + 1) * D_MODEL].astype(F32)

        z = (gate(0) * _dot(y_a.astype(BF16), wa_ref[...])
             + gate(1) * _dot(y_b.astype(BF16), wb_ref[...])
             + gate(2) * _dot(hc_ref[rows, :], wc_ref[...]))
        mix = _dot(z.astype(BF16), wo_ref[...])
        out = _layer_norm(alpha * x_ref[rows, :] + mix, lng_ref[...], lnb_ref[...])
        xf_ref[rows, :] = out
        xb_ref[rows, :] = out.astype(BF16)


def _mixer_out(x, att_outs, sg, hc, gpre, p, alpha, seq, tm=512):
    n = x.shape[0]
    tiles_per_seq = seq // tm

    def rows(width):
        return pl.BlockSpec((tm, width), lambda i: (i, 0))

    def whole(a):
        return pl.BlockSpec(a.shape, lambda i: (0,) * a.ndim)

    def residue_major(dil):
        return pl.BlockSpec((None, dil, tm // dil, ATT_GROUP_WIDTH),
                            lambda i: (i // tiles_per_seq, 0, i % tiles_per_seq, 0))

    att_specs, att_args = [], []
    for (o, lse), (_, dil) in zip(att_outs, ATT_GROUPS):
        att_specs += [residue_major(dil)] * 2
        att_args += [o, lse]
    weights = [p["sg_ln_g"], p["sg_ln_b"], p["sg_w"], p["sg_bias_t"], p["w_br_a"], p["w_br_b"], p["w_br_c"],
               p["w_out"], p["ln_g0"], p["ln_b0"]]
    return pl.pallas_call(
        functools.partial(_mixer_out_kernel, alpha=alpha),
        grid=(n // tm,),
        in_specs=[rows(D_MODEL)] + att_specs
                 + [rows(2 * SG_WIDTH), rows(ML_PAD_WIDTH), rows(N_BRANCH * D_MODEL)]
                 + [whole(w) for w in weights],
        out_specs=[rows(D_MODEL), rows(D_MODEL)],
        out_shape=[jax.ShapeDtypeStruct((n, D_MODEL), F32), jax.ShapeDtypeStruct((n, D_MODEL), BF16)],
        scratch_shapes=[pltpu.VMEM((2 * len(ATT_GROUPS) * ATT_GROUP_WIDTH // LANES, tm, LANES), F32)],
        compiler_params=_cparams("parallel"),
        name="mixer_out",
    )(x, *att_args, sg, hc, gpre, *weights)


def _swiglu_tile(x, w1_ref, w3_ref, w2_ref, prod, tf):
    for c in range(D_FF // tf):
        cols = slice(c * tf, (c + 1) * tf)
        h1 = _dot(x, w1_ref[:, cols])
        h3 = _dot(x, w3_ref[:, cols])
        prod[:, cols] = (h1 * _sigmoid(h1) * h3).astype(BF16)
    return _dot(prod[...], w2_ref[...])


def _ffn_kernel(xf_ref, xb_ref, w1_ref, w3_ref, w2_ref, lng_ref, lnb_ref, of_ref, ob_ref, prod, *, alpha, tf):
    f = _swiglu_tile(xb_ref[...], w1_ref, w3_ref, w2_ref, prod, tf)
    out = _layer_norm(alpha * xf_ref[...] + f, lng_ref[...], lnb_ref[...])
    of_ref[...] = out
    ob_ref[...] = out.astype(BF16)


def _ffn(xf, xb, w1, w3, w2, ln_g, ln_b, alpha, tm=512, tf=256):
    n = xf.shape[0]

    def resident(shape):
        return pl.BlockSpec(shape, lambda i: (0, 0), pipeline_mode=pl.Buffered(1))

    return pl.pallas_call(
        functools.partial(_ffn_kernel, alpha=alpha, tf=tf),
        grid=(n // tm,),
        in_specs=[pl.BlockSpec((tm, D_MODEL), lambda i: (i, 0)),
                  pl.BlockSpec((tm, D_MODEL), lambda i: (i, 0)),
                  resident((D_MODEL, D_FF)), resident((D_MODEL, D_FF)), resident((D_FF, D_MODEL)),
                  pl.BlockSpec((1, D_MODEL), lambda i: (0, 0)),
                  pl.BlockSpec((1, D_MODEL), lambda i: (0, 0))],
        out_specs=[pl.BlockSpec((tm, D_MODEL), lambda i: (i, 0))] * 2,
        out_shape=[jax.ShapeDtypeStruct((n, D_MODEL), F32), jax.ShapeDtypeStruct((n, D_MODEL), BF16)],
        scratch_shapes=[pltpu.VMEM((tm, D_FF), BF16)],
        compiler_params=_cparams("parallel"),
        name="ffn",
    )(xf, xb, w1, w3, w2, ln_g, ln_b)


def _router_kernel(x_ref, w_ref, b_ref, info_ref):
    xh, xm, _ = _split3(x_ref[...])
    wh, wm, _ = _split3(w_ref[...])
    logits = (_dot(xh, wh) + (_dot(xh, wm) + _dot(xm, wh))) + b_ref[...]
    lane = lax.broadcasted_iota(jnp.int32, logits.shape, 1)
    neg_inf = jnp.float32(-jnp.inf)
    logits = jnp.where(lane < N_EXPERTS, logits, neg_inf)
    v1 = jnp.max(logits, axis=-1, keepdims=True)
    i1 = jnp.min(jnp.where(logits == v1, lane, LANES), axis=-1, keepdims=True)
    rest = jnp.where(lane == i1, neg_inf, logits)
    v2 = jnp.max(rest, axis=-1, keepdims=True)
    i2 = jnp.min(jnp.where(rest == v2, lane, LANES), axis=-1, keepdims=True)
    e = jnp.exp(v2 - v1)
    inv = 1.0 / (1.0 + e)
    info_ref[...] = jnp.where(lane == 0, i1.astype(F32),
                              jnp.where(lane == 1, i2.astype(F32),
                                        jnp.where(lane == 2, inv, jnp.where(lane == 3, e * inv, 0.0))))


def _router(xf, router_w, router_b, tm=1024):
    n = xf.shape[0]
    w = jnp.pad(router_w, ((0, 0), (0, LANES - N_EXPERTS)))
    b = jnp.pad(router_b, (0, LANES - N_EXPERTS))[None, :]
    return pl.pallas_call(
        _router_kernel,
        grid=(n // tm,),
        in_specs=[pl.BlockSpec((tm, D_MODEL), lambda i: (i, 0)),
                  pl.BlockSpec((D_MODEL, LANES), lambda i: (0, 0)),
                  pl.BlockSpec((1, LANES), lambda i: (0, 0))],
        out_specs=pl.BlockSpec((tm, LANES), lambda i: (i, 0)),
        out_shape=jax.ShapeDtypeStruct((n, LANES), F32),
        compiler_params=_cparams("parallel"),
        name="router",
    )(xf, w, b)


def _routing_plan(info, n):
    tile = EXPERT_TILE
    n_tiles = TOP_K * n // tile + N_EXPERTS
    expert = info[:, 0:TOP_K].astype(jnp.int32).reshape(-1)
    onehot = (expert[:, None] == jnp.arange(N_EXPERTS, dtype=jnp.int32)[None, :]).astype(jnp.int32)
    running = jnp.cumsum(onehot, axis=0)
    rank = jnp.sum(onehot * running, axis=1) - 1
    counts = running[-1]
    padded = ((counts + tile - 1) // tile) * tile
    ends = jnp.cumsum(padded)
    starts = ends - padded
    dest = jnp.sum(onehot * starts[None, :], axis=1) + rank
    used = ends[-1] // tile
    tile_start = jnp.arange(n_tiles, dtype=jnp.int32) * tile
    tile_expert = jnp.sum((tile_start[:, None] >= ends[None, :]).astype(jnp.int32), axis=1)
    last_expert = jnp.sum((tile_start[used - 1] >= ends).astype(jnp.int32))
    tile_expert = jnp.minimum(tile_expert, last_expert)
    return dest.astype(jnp.int32), tile_expert.astype(jnp.int32), used.astype(jnp.int32).reshape(1), n_tiles


def _row_copy(src, src_row, dst, dst_row, sem):
    return pltpu.make_async_copy(src.at[pl.ds(src_row, 1), :], dst.at[pl.ds(dst_row, 1), :], sem)


def _rows_wait(src, dst, rows, sem):
    pltpu.make_async_copy(src.at[pl.ds(0, rows), :], dst.at[pl.ds(0, rows), :], sem).wait()


DISPATCH_SLOTS = 3


def _dispatch_kernel(dest_ref, x_ref, zero_ref, xs_ref, buf, load_sem, row_sem, *, tm):
    del zero_ref
    i = pl.program_id(0)
    steps = pl.num_programs(0)
    slot = i % DISPATCH_SLOTS
    prev_slot = (i + DISPATCH_SLOTS - 1) % DISPATCH_SLOTS

    def load(tile, into):
        return pltpu.make_async_copy(x_ref.at[pl.ds(tile * tm, tm), :], buf.at[into], load_sem.at[into])

    def wait_rows(of_slot):
        for _ in range(TOP_K):
            pltpu.make_async_copy(buf.at[of_slot], xs_ref.at[pl.ds(0, tm), :], row_sem.at[of_slot]).wait()

    @pl.when(i == 0)
    def _():
        load(0, 0).start()

        @pl.when(steps > 1)
        def _():
            load(1, 1).start()

    load(i, slot).wait()

    def start(t, carry):
        for c in range(TOP_K):
            _row_copy(buf.at[slot], t, xs_ref, dest_ref[0, 0, TOP_K * t + c], row_sem.at[slot]).start()
        return carry

    lax.fori_loop(0, tm, start, 0, unroll=DMA_ISSUE_UNROLL)

    @pl.when(i > 0)
    def _():
        wait_rows(prev_slot)

    @pl.when(i + 2 < steps)
    def _():
        load(i + 2, prev_slot).start()

    @pl.when(i == steps - 1)
    def _():
        wait_rows(slot)


def _dispatch(xf, dest, n_slots, tm=512):
    n = xf.shape[0]
    dest3 = dest.reshape(n // tm, 1, TOP_K * tm)
    zeros = jnp.zeros((n_slots, D_MODEL), F32)
    return pl.pallas_call(
        functools.partial(_dispatch_kernel, tm=tm),
        grid=(n // tm,),
        in_specs=[pl.BlockSpec((1, 1, TOP_K * tm), lambda i: (i, 0, 0), memory_space=pltpu.SMEM),
                  pl.BlockSpec(memory_space=pl.ANY),
                  pl.BlockSpec(memory_space=pl.ANY)],
        out_specs=pl.BlockSpec(memory_space=pl.ANY),
        out_shape=jax.ShapeDtypeStruct((n_slots, D_MODEL), F32),
        scratch_shapes=[pltpu.VMEM((DISPATCH_SLOTS, tm, D_MODEL), F32),
                        pltpu.SemaphoreType.DMA((DISPATCH_SLOTS,)),
                        pltpu.SemaphoreType.DMA((DISPATCH_SLOTS,))],
        input_output_aliases={2: 0},
        compiler_params=_cparams("arbitrary"),
        name="moe_dispatch",
    )(dest3, xf, zeros)


def _experts_kernel(te_ref, used_ref, xs_ref, w1_ref, w3_ref, w2_ref, ys_ref, prod, *, tf):
    del te_ref
    i = pl.program_id(0)

    @pl.when(i < used_ref[0])
    def _():
        ys_ref[...] = _swiglu_tile(xs_ref[...].astype(BF16), w1_ref, w3_ref, w2_ref, prod, tf)

    @pl.when(i >= used_ref[0])
    def _():
        ys_ref[...] = jnp.zeros(ys_ref.shape, ys_ref.dtype)


def _experts(xs, tile_expert, used, w1, w3, w2, n_tiles, tf=256):
    tile = EXPERT_TILE
    grid_spec = pltpu.PrefetchScalarGridSpec(
        num_scalar_prefetch=2,
        grid=(n_tiles,),
        in_specs=[pl.BlockSpec((tile, D_MODEL), lambda i, te, nu: (i, 0)),
                  pl.BlockSpec((None, D_MODEL, D_FF), lambda i, te, nu: (te[i], 0, 0)),
                  pl.BlockSpec((None, D_MODEL, D_FF), lambda i, te, nu: (te[i], 0, 0)),
                  pl.BlockSpec((None, D_FF, D_MODEL), lambda i, te, nu: (te[i], 0, 0))],
        out_specs=pl.BlockSpec((tile, D_MODEL), lambda i, te, nu: (i, 0)),
        scratch_shapes=[pltpu.VMEM((tile, D_FF), BF16)],
    )
    return pl.pallas_call(
        functools.partial(_experts_kernel, tf=tf),
        grid_spec=grid_spec,
        out_shape=jax.ShapeDtypeStruct((n_tiles * tile, D_MODEL), F32),
        compiler_params=_cparams("arbitrary"),
        name="moe_experts",
    )(tile_expert, used, xs, w1, w3, w2)


def _combine_kernel(dest_ref, next_dest_ref, xf_ref, info_ref, ys_ref, lng_ref, lnb_ref, of_ref, got, sem, *, alpha):
    tm = xf_ref.shape[0]
    i = pl.program_id(0)
    slot = i % 2

    def gather(dests, into):
        def start(t, carry):
            for c in range(TOP_K):
                _row_copy(ys_ref, dests[0, 0, TOP_K * t + c], got.at[into, c], t, sem.at[into]).start()
            return carry
        lax.fori_loop(0, tm, start, 0, unroll=DMA_ISSUE_UNROLL)

    @pl.when(i == 0)
    def _():
        gather(dest_ref, 0)

    @pl.when(i + 1 < pl.num_programs(0))
    def _():
        gather(next_dest_ref, 1 - slot)

    for c in range(TOP_K):
        _rows_wait(ys_ref, got.at[slot, c], tm, sem.at[slot])
    info = info_ref[...]
    mix = info[:, 2:3] * got[slot, 0] + info[:, 3:4] * got[slot, 1]
    of_ref[...] = _layer_norm(alpha * xf_ref[...] + mix, lng_ref[...], lnb_ref[...])


def _combine(xf, info, ys, dest, ln_g, ln_b, alpha, tm=256):
    n = xf.shape[0]
    steps = n // tm
    dest3 = dest.reshape(steps, 1, TOP_K * tm)
    return pl.pallas_call(
        functools.partial(_combine_kernel, alpha=alpha),
        grid=(steps,),
        in_specs=[pl.BlockSpec((1, 1, TOP_K * tm), lambda i: (i, 0, 0), memory_space=pltpu.SMEM),
                  pl.BlockSpec((1, 1, TOP_K * tm), lambda i: (jnp.minimum(i + 1, steps - 1), 0, 0),
                               memory_space=pltpu.SMEM),
                  pl.BlockSpec((tm, D_MODEL), lambda i: (i, 0)),
                  pl.BlockSpec((tm, LANES), lambda i: (i, 0)),
                  pl.BlockSpec(memory_space=pl.ANY),
                  pl.BlockSpec((1, D_MODEL), lambda i: (0, 0)),
                  pl.BlockSpec((1, D_MODEL), lambda i: (0, 0))],
        out_specs=pl.BlockSpec((tm, D_MODEL), lambda i: (i, 0)),
        out_shape=jax.ShapeDtypeStruct((n, D_MODEL), F32),
        scratch_shapes=[pltpu.VMEM((2, TOP_K, tm, D_MODEL), F32), pltpu.SemaphoreType.DMA((2,))],
        compiler_params=_cparams("arbitrary"),
        name="moe_combine",
    )(dest3, dest3, xf, info, ys, ln_g, ln_b)


def _moe(xf, router_w, router_b, w1, w3, w2, ln_g, ln_b, alpha):
    n = xf.shape[0]
    info = _router(xf, router_w, router_b)
    dest, tile_expert, used, n_tiles = _routing_plan(info, n)
    xs = _dispatch(xf, dest, n_tiles * EXPERT_TILE)
    ys = _experts(xs, tile_expert, used, w1, w3, w2, n_tiles)
    return _combine(xf, info, ys, dest, ln_g, ln_b, alpha)


def _pad_heads(w):
    lead = w.shape[:-1]
    w = w.reshape(*lead, ML_HEADS, ML_HEAD_DIM)
    w = jnp.pad(w, [(0, 0)] * len(lead) + [(0, 0), (0, ML_HEAD_PAD - ML_HEAD_DIM)])
    return w.reshape(*lead, ML_PAD_WIDTH)


def _group_qkv(w, group):
    gw = ATT_GROUP_WIDTH
    return jnp.concatenate([w[..., part * ATT_WIDTH + group * gw:part * ATT_WIDTH + (group + 1) * gw]
                            for part in range(3)], axis=-1)


def _layer_params(layer, w_in, b_in, conv_w, conv_b, sg_ln_g, sg_ln_b, sg_w, sg_b,
                  w_br_a, w_br_b, w_br_c, w_out, ln_g, ln_b):
    w, b = w_in[layer], b_in[layer]
    o_att, o_sg, o_qk, o_v, o_o, o_i, o_g = 0, 2304, 3840, 5376, 6144, 6912, 6920
    ones_col = jnp.zeros((ML_HEADS, ML_HEAD_PAD), F32).at[:, ML_HEAD_DIM].set(1.0).reshape(ML_PAD_WIDTH)
    w_qk = jnp.concatenate([_pad_heads(w[:, o_qk:o_qk + ML_WIDTH]), _pad_heads(w[:, o_qk + ML_WIDTH:o_v])], axis=1)
    b_qk = jnp.concatenate([_pad_heads(b[o_qk:o_qk + ML_WIDTH]), _pad_heads(b[o_qk + ML_WIDTH:o_v])])
    qk_scale = jnp.concatenate([jnp.ones((ML_PAD_WIDTH,), F32),
                                jnp.full((ML_PAD_WIDTH,), ML_HEAD_DIM ** -0.5, F32)])
    cw, cb = conv_w[layer], conv_b[layer]
    w_att, b_att = w[:, o_att:o_sg], b[o_att:o_sg]
    return dict(
        w_att=[_group_qkv(w_att, g).astype(BF16) for g in range(len(ATT_GROUPS))],
        b_att=[_group_qkv(b_att, g)[None, :] for g in range(len(ATT_GROUPS))],
        w_sg=w[:, o_sg:o_qk].astype(BF16), b_sg=b[None, o_sg:o_qk],
        w_ml=jnp.concatenate([w_qk, _pad_heads(w[:, o_v:o_o])], axis=1).astype(BF16),
        b_ml=jnp.concatenate([b_qk, _pad_heads(b[o_v:o_o]) + ones_col])[None, :],
        qk_scale=qk_scale[None, :],
        w_mlo=_pad_heads(w[:, o_o:o_i]).astype(BF16), b_mlo=_pad_heads(b[o_o:o_i])[None, :],
        w_if=jnp.pad(w[:, o_i:o_g], ((0, 0), (0, LANES - 2 * ML_HEADS))).astype(BF16),
        b_if=jnp.pad(b[o_i:o_g], (0, LANES - 2 * ML_HEADS))[None, :],
        w_g=w[:, o_g:].astype(BF16), b_g=b[None, o_g:],
        conv_w=jnp.concatenate([_pad_heads(cw[:, :ML_WIDTH]), _pad_heads(cw[:, ML_WIDTH:])], axis=1),
        conv_b=jnp.concatenate([_pad_heads(cb[:ML_WIDTH]), _pad_heads(cb[ML_WIDTH:])])[None, :],
        sg_ln_g=sg_ln_g[layer][None, :], sg_ln_b=sg_ln_b[layer][None, :],
        sg_w=sg_w[layer], sg_bias_t=sg_b[layer].T,
        w_br_a=w_br_a[layer].astype(BF16), w_br_b=w_br_b[layer].astype(BF16),
        w_br_c=_pad_heads(w_br_c[layer].T).T.astype(BF16),
        w_out=w_out[layer].astype(BF16),
        ln_g0=ln_g[layer, 0][None, :], ln_b0=ln_b[layer, 0][None, :],
        ln_g1=ln_g[layer, 1][None, :], ln_b1=ln_b[layer, 1][None, :],
    )


def kernel(x, positions, w_in, b_in, conv_w, conv_b, sg_ln_g, sg_ln_b, sg_w, sg_b, w_br_a, w_br_b, w_br_c, w_out,
           ln_g, ln_b, ffn_w1, ffn_w3, ffn_w2, router_w, router_b, moe_w1, moe_w3, moe_w2):
    batch, seq, _ = x.shape
    n = batch * seq
    depth = w_in.shape[0]
    alpha = (2.0 * depth) ** 0.25
    cos, sin = _rope_tables(positions)
    xf = x.reshape(n, D_MODEL)
    xb = xf.astype(BF16)
    for layer in range(depth):
        p = _layer_params(layer, w_in, b_in, conv_w, conv_b, sg_ln_g, sg_ln_b, sg_w, sg_b,
                          w_br_a, w_br_b, w_br_c, w_out, ln_g, ln_b)
        sg = _linear(xb, p["w_sg"], p["b_sg"], BF16, 2048, 768, act=_gelu_tanh)
        ml = _linear(xb, p["w_ml"], p["b_ml"], BF16, 2048, ML_PAD_WIDTH)
        og = _linear(xb, p["w_mlo"], p["b_mlo"], BF16, 2048, ML_PAD_WIDTH, act=_sigmoid)
        gates_if = _linear(xb, p["w_if"], p["b_if"], F32, 2048, LANES)
        gbr = _linear(xb, p["w_g"], p["b_g"], BF16, 2048, 1024, act=_sigmoid)
        att_outs = []
        for g, (_, dil) in enumerate(ATT_GROUPS):
            qkv = _linear_dilated(xb, p["w_att"][g], p["b_att"][g], dil, batch, seq)
            att_outs.append(_attention_group(qkv, cos, sin, g, batch, seq))
        hc = _mlstm(ml, og, gates_if, p["conv_w"], p["conv_b"], p["qk_scale"], batch, seq)
        xf, xb = _mixer_out(xf, att_outs, sg, hc, gbr, p, alpha, seq)
        j = layer // 2
        if layer % 2 == 0:
            xf, xb = _ffn(xf, xb, ffn_w1[j].astype(BF16), ffn_w3[j].astype(BF16), ffn_w2[j].astype(BF16),
                          p["ln_g1"], p["ln_b1"], alpha)
        else:
            xf = _moe(xf, router_w[j], router_b[j], moe_w1[j].astype(BF16), moe_w3[j].astype(BF16),
                      moe_w2[j].astype(BF16), p["ln_g1"], p["ln_b1"], alpha)
            xb = xf.astype(BF16)
    return xf.reshape(batch, seq, D_MODEL)
```

```python
import functools

import numpy as np
import jax
import jax.numpy as jnp
from jax import lax
from jax.experimental import pallas as pl
from jax.experimental.pallas import tpu as pltpu

F32 = jnp.float32
BF16 = jnp.bfloat16

D_MODEL = 1024
ATT_GROUPS = ((128, 1), (512, 4), (2048, 16))
ATT_HEADS_PER_GROUP = 4
ATT_HEAD_DIM = 64
ATT_WIDTH = 768
ATT_GROUP_WIDTH = ATT_HEADS_PER_GROUP * ATT_HEAD_DIM
ROPE_THETA = 10000.0
SG_CHUNK = 128
SG_GROUPS = 6
SG_WIDTH = 768
ML_HEADS = 4
ML_HEAD_DIM = 192
ML_HEAD_PAD = 256
ML_WIDTH = 768
ML_PAD_WIDTH = ML_HEADS * ML_HEAD_PAD
ML_CHUNK = 128
ML_CONV = 4
N_BRANCH = 3
D_FF = 2816
N_EXPERTS = 8
TOP_K = 2
LN_EPS = 1e-5
BLOCK = 128
LANES = 128
VMEM_LIMIT = 56 * 1024 * 1024
EXPERT_TILE = 512
DMA_ISSUE_UNROLL = 8
MIXER_SUB_ROWS = 256


def _cparams(*sem):
    return pltpu.CompilerParams(dimension_semantics=sem, vmem_limit_bytes=VMEM_LIMIT)


def _layer_norm(x, g, b):
    mu = jnp.mean(x, axis=-1, keepdims=True)
    xc = x - mu
    var = jnp.mean(xc * xc, axis=-1, keepdims=True)
    return xc * lax.rsqrt(var + LN_EPS) * g + b


def _gelu_tanh(x):
    return 0.5 * x * (1.0 + jnp.tanh(np.sqrt(2.0 / np.pi) * (x + 0.044715 * (x * x * x))))


def _sigmoid(x):
    return 0.5 * (1.0 + jnp.tanh(0.5 * x))


def _dot(a, b):
    return jnp.dot(a, b, preferred_element_type=F32)


def _dot_nt(a, b):
    return lax.dot_general(a, b, (((1,), (1,)), ((), ())), preferred_element_type=F32)


def _dot_tn(a, b):
    return lax.dot_general(a, b, (((0,), (0,)), ((), ())), preferred_element_type=F32)


def _split3(x):
    hi = x.astype(BF16)
    r1 = x - hi.astype(F32)
    mid = r1.astype(BF16)
    lo = (r1 - mid.astype(F32)).astype(BF16)
    return hi, mid, lo


def _linear_kernel(x_ref, w_ref, b_ref, o_ref, *, act):
    res = _dot(x_ref[...], w_ref[...]) + b_ref[...]
    o_ref[...] = (res if act is None else act(res)).astype(o_ref.dtype)


def _linear(x, w, b, out_dtype, tm, tn, act=None):
    m, k = x.shape
    n = w.shape[1]
    mode = dict(pipeline_mode=pl.Buffered(1)) if tn == n else {}
    return pl.pallas_call(
        functools.partial(_linear_kernel, act=act),
        grid=(m // tm, n // tn),
        in_specs=[pl.BlockSpec((tm, k), lambda i, j: (i, 0)),
                  pl.BlockSpec((k, tn), lambda i, j: (0, j), **mode),
                  pl.BlockSpec((1, tn), lambda i, j: (0, j), **mode)],
        out_specs=pl.BlockSpec((tm, tn), lambda i, j: (i, j)),
        out_shape=jax.ShapeDtypeStruct((m, n), out_dtype),
        compiler_params=_cparams("parallel", "parallel"),
        name="linear",
    )(x, w, b)


def _linear_dilated_kernel(x_ref, w_ref, b_ref, o_ref, acc, *, dil):
    res = _dot(x_ref[...], w_ref[...]) + b_ref[...]
    slabs, rows, _ = acc.shape
    for c in range(slabs):
        acc[c] = res[:, c * LANES:(c + 1) * LANES]
    sub = rows // dil
    for r in range(dil):
        for c in range(slabs):
            o_ref[r, :, c * LANES:(c + 1) * LANES] = acc[c, pl.ds(r, sub, stride=dil), :].astype(o_ref.dtype)


def _linear_dilated(x, w, b, dil, batch, seq):
    k = x.shape[1]
    n = w.shape[1]
    tn = ATT_GROUP_WIDTH
    sub = seq // dil
    return pl.pallas_call(
        functools.partial(_linear_dilated_kernel, dil=dil),
        grid=(batch, n // tn),
        in_specs=[pl.BlockSpec((seq, k), lambda i, j: (i, 0)),
                  pl.BlockSpec((k, tn), lambda i, j: (0, j)),
                  pl.BlockSpec((1, tn), lambda i, j: (0, j))],
        out_specs=pl.BlockSpec((None, dil, sub, tn), lambda i, j: (i, 0, 0, j)),
        out_shape=jax.ShapeDtypeStruct((batch, dil, sub, n), BF16),
        scratch_shapes=[pltpu.VMEM((tn // LANES, seq, LANES), F32)],
        compiler_params=_cparams("parallel", "parallel"),
        name=f"linear_dil{dil}",
    )(x, w, b)


def _rope_table_kernel(pos_ref, freq_ref, sign_ref, cos_ref, sin_ref):
    ang = pos_ref[...] * freq_ref[...]
    cos_ref[...] = jnp.cos(ang)
    sin_ref[...] = jnp.sin(ang) * sign_ref[...]


def _rope_tables(positions):
    n = positions.size
    half = ATT_HEAD_DIM // 2
    freqs = ROPE_THETA ** (-jnp.arange(half, dtype=F32) * (2.0 / ATT_HEAD_DIM))
    lane = np.arange(LANES)
    freq_row = freqs[lane % half][None, :]
    sign_row = jnp.asarray(np.where(lane % ATT_HEAD_DIM < half, -1.0, 1.0), F32)[None, :]
    pos = positions.astype(F32).reshape(n, 1)
    tm = 2048
    return pl.pallas_call(
        _rope_table_kernel,
        grid=(n // tm,),
        in_specs=[pl.BlockSpec((tm, 1), lambda i: (i, 0)),
                  pl.BlockSpec((1, LANES), lambda i: (0, 0)),
                  pl.BlockSpec((1, LANES), lambda i: (0, 0))],
        out_specs=[pl.BlockSpec((tm, LANES), lambda i: (i, 0))] * 2,
        out_shape=[jax.ShapeDtypeStruct((n, LANES), F32)] * 2,
        compiler_params=_cparams("parallel"),
        name="rope_tables",
    )(pos, freq_row, sign_row)


def _attn_kernel(q_ref, k_ref, v_ref, cos_ref, sin_ref, o_ref, lse_ref, qs, ks, vs, *, sub, dil):
    nb = sub // BLOCK
    res = pl.program_id(1)
    lane = lax.broadcasted_iota(jnp.int32, (BLOCK, LANES), 1)
    first_half = (lane % ATT_HEAD_DIM) < (ATT_HEAD_DIM // 2)
    low_head = lane < ATT_HEAD_DIM

    def rope(x, c, s):
        partner = jnp.where(first_half, pltpu.roll(x, LANES - 32, 1), pltpu.roll(x, 32, 1))
        return x * c + partner * s

    ks[0:BLOCK, :] = jnp.zeros((BLOCK, ATT_GROUP_WIDTH), BF16)
    vs[0:BLOCK, :] = jnp.zeros((BLOCK, ATT_GROUP_WIDTH), BF16)

    def prep(n, carry):
        r0 = pl.multiple_of(n * BLOCK, BLOCK)
        if dil == 1:
            c = cos_ref[pl.ds(r0, BLOCK), :]
            s = sin_ref[pl.ds(r0, BLOCK), :]
        else:
            c = cos_ref[pl.ds(r0 * dil + res, BLOCK, stride=dil), :]
            s = sin_ref[pl.ds(r0 * dil + res, BLOCK, stride=dil), :]
        for pair in range(2):
            cols = slice(pair * LANES, (pair + 1) * LANES)
            q = rope(q_ref[pl.ds(r0, BLOCK), cols].astype(F32), c, s) * (ATT_HEAD_DIM ** -0.5)
            qs[pl.ds(r0, BLOCK), (2 * pair) * LANES:(2 * pair + 1) * LANES] = jnp.where(low_head, q, 0.0).astype(BF16)
            qs[pl.ds(r0, BLOCK), (2 * pair + 1) * LANES:(2 * pair + 2) * LANES] = jnp.where(low_head, 0.0, q).astype(BF16)
            k = rope(k_ref[pl.ds(r0, BLOCK), cols].astype(F32), c, s)
            ks[pl.ds(r0 + BLOCK, BLOCK), cols] = k.astype(BF16)
        vs[pl.ds(r0 + BLOCK, BLOCK), :] = v_ref[pl.ds(r0, BLOCK), :]
        return carry

    lax.fori_loop(0, nb, prep, 0)

    qi = lax.broadcasted_iota(jnp.int32, (BLOCK, 2 * BLOCK), 0)
    ki = lax.broadcasted_iota(jnp.int32, (BLOCK, 2 * BLOCK), 1)
    dist = qi + BLOCK - ki
    band = (dist >= 0) & (dist <= BLOCK)
    neg_inf = jnp.float32(-jnp.inf)
    bias_rest = jnp.where(band, 0.0, neg_inf)
    bias_first = jnp.where(band & (ki >= BLOCK), 0.0, neg_inf)

    def block(n, bias):
        r0 = pl.multiple_of(n * BLOCK, BLOCK)
        heads = range(ATT_HEADS_PER_GROUP)
        pair_cols = [slice(pair * LANES, (pair + 1) * LANES) for pair in range(2)]
        kw = [ks[pl.ds(r0, 2 * BLOCK), c] for c in pair_cols]
        vw = [vs[pl.ds(r0, 2 * BLOCK), c] for c in pair_cols]
        s = [_dot_nt(qs[pl.ds(r0, BLOCK), hd * LANES:(hd + 1) * LANES], kw[hd // 2]) + bias for hd in heads]
        m = [jnp.max(s[hd], axis=-1, keepdims=True) for hd in heads]
        p = [jnp.exp(s[hd] - m[hd]) for hd in heads]
        den = [jnp.sum(p[hd], axis=-1, keepdims=True) for hd in heads]
        o = [_dot(p[hd].astype(BF16), vw[hd // 2]) * (1.0 / den[hd]) for hd in heads]
        lse = [m[hd] + jnp.log(den[hd]) for hd in heads]
        for pair in range(2):
            lo, hi = 2 * pair, 2 * pair + 1
            o_ref[pl.ds(r0, BLOCK), pair_cols[pair]] = jnp.where(low_head, o[lo], o[hi]).astype(o_ref.dtype)
            lse_ref[pl.ds(r0, BLOCK), pair_cols[pair]] = jnp.where(low_head, lse[lo], lse[hi])

    block(0, bias_first)
    if nb <= 2:
        for n in range(1, nb):
            block(n, bias_rest)
        return

    def body(n, carry):
        block(n, bias_rest)
        return carry

    lax.fori_loop(1, nb, body, 0, unroll=2)


def _attention_group(qkv, cos, sin, group, batch, seq):
    window, dil = ATT_GROUPS[group]
    assert window // dil == BLOCK
    sub = seq // dil
    assert sub % BLOCK == 0

    def part(p):
        return pl.BlockSpec((None, None, sub, ATT_GROUP_WIDTH), lambda b, r: (b, r, 0, p))

    tab = pl.BlockSpec((seq, LANES), lambda b, r: (b, 0))
    out = pl.BlockSpec((None, None, sub, ATT_GROUP_WIDTH), lambda b, r: (b, r, 0, 0))
    return pl.pallas_call(
        functools.partial(_attn_kernel, sub=sub, dil=dil),
        grid=(batch, dil),
        in_specs=[part(0), part(1), part(2), tab, tab],
        out_specs=[out, out],
        out_shape=[jax.ShapeDtypeStruct((batch, dil, sub, ATT_GROUP_WIDTH), BF16),
                   jax.ShapeDtypeStruct((batch, dil, sub, ATT_GROUP_WIDTH), F32)],
        scratch_shapes=[pltpu.VMEM((sub, ATT_HEADS_PER_GROUP * LANES), BF16),
                        pltpu.VMEM((sub + BLOCK, ATT_GROUP_WIDTH), BF16),
                        pltpu.VMEM((sub + BLOCK, ATT_GROUP_WIDTH), BF16)],
        compiler_params=_cparams("parallel", "parallel"),
        name=f"attention_g{group}",
    )(qkv, qkv, qkv, cos, sin)


ML_SEQS_PER_STEP = 2


def _mlstm_kernel(qk_ref, v_ref, og_ref, g_ref, cw_ref, cb_ref, sc_ref, out_ref, frame, ct, m_s):
    taps = ML_CONV

    @pl.when(pl.program_id(1) == 0)
    def _():
        frame[:, 0:ML_CHUNK, :] = jnp.zeros((frame.shape[0], ML_CHUNK, 2 * ML_PAD_WIDTH), BF16)
        ct[...] = jnp.zeros(ct.shape, F32)
        m_s[...] = jnp.zeros(m_s.shape, F32)

    out_row = lax.broadcasted_iota(jnp.int32, (ML_CHUNK, 2 * ML_CHUNK), 0)
    src_row = lax.broadcasted_iota(jnp.int32, (ML_CHUNK, 2 * ML_CHUNK), 1)
    picks = [jnp.where(src_row == out_row + (ML_CHUNK - (taps - 1 - j)), 1.0, 0.0).astype(BF16)
             for j in range(taps - 1)]
    row = lax.broadcasted_iota(jnp.int32, (ML_CHUNK, ML_CHUNK), 0)
    colm = lax.broadcasted_iota(jnp.int32, (ML_CHUNK, ML_CHUNK), 1)
    causal = row >= colm
    tri = jnp.where(causal, 1.0, 0.0).astype(BF16)
    neg_inf = jnp.float32(-jnp.inf)

    seqs = range(qk_ref.shape[0])
    pairs = [(s, h) for s in seqs for h in range(ML_HEADS)]

    def head_cols(h):
        return slice(h * ML_HEAD_PAD, (h + 1) * ML_HEAD_PAD)

    act, gates, bcum, bcum_t, gates_t = {}, {}, {}, {}, {}
    for s in seqs:
        cur = qk_ref[s]
        frame[s, ML_CHUNK:2 * ML_CHUNK, :] = cur
        both = frame[s]
        acc = cb_ref[...] + cur.astype(F32) * cw_ref[taps - 1:taps, :]
        for j in range(taps - 1):
            acc = acc + _dot(picks[j], both) * cw_ref[j:j + 1, :]
        frame[s, 0:ML_CHUNK, :] = cur
        act[s] = (acc * _sigmoid(acc) * sc_ref[...]).astype(BF16)
        gates[s] = g_ref[s]
        log_f = -(jnp.maximum(-gates[s], 0.0) + jnp.log1p(jnp.exp(-jnp.abs(gates[s]))))
        hi, mid, lo = _split3(log_f)
        bcum[s] = _dot(tri, hi) + _dot(tri, mid) + _dot(tri, lo)
        bcum_t[s] = bcum[s].T
        gates_t[s] = gates[s].T

    qh, kh, vh, c_old, m_prev, b_c, i_c, log_d, m_inter, m = ({} for _ in range(10))
    for p in pairs:
        s, h = p
        qh[p] = act[s][:, head_cols(h)]
        kh[p] = act[s][:, ML_PAD_WIDTH + h * ML_HEAD_PAD:ML_PAD_WIDTH + (h + 1) * ML_HEAD_PAD]
        vh[p] = v_ref[s, :, head_cols(h)]
        c_old[p] = ct[s, h]
        m_prev[p] = m_s[s, h, 0:1, 0:1]
        b_c[p] = bcum[s][:, ML_HEADS + h:ML_HEADS + h + 1]
        b_r = bcum_t[s][ML_HEADS + h:ML_HEADS + h + 1, :]
        i_c[p] = gates[s][:, h:h + 1]
        i_r = gates_t[s][h:h + 1, :]
        log_d[p] = jnp.where(causal, b_c[p] - b_r + i_r, neg_inf)
        m_inter[p] = b_c[p] + m_prev[p]
        m[p] = jnp.maximum(m_inter[p], jnp.max(log_d[p], axis=-1, keepdims=True))

    scores = {p: _dot_nt(qh[p], kh[p]) for p in pairs}
    carried = {p: _dot(qh[p], c_old[p].astype(BF16)) for p in pairs}
    sc = {p: (scores[p] * jnp.exp(log_d[p] - m[p])).astype(BF16) for p in pairs}
    num = {p: _dot(sc[p], vh[p]) + jnp.exp(m_inter[p] - m[p]) * carried[p] for p in pairs}
    for p in pairs:
        s, h = p
        den = num[p][:, ML_HEAD_DIM:ML_HEAD_DIM + 1]
        hid = num[p] / jnp.maximum(jnp.abs(den), jnp.exp(-m[p]))
        out_ref[s, :, head_cols(h)] = (og_ref[s, :, head_cols(h)].astype(F32) * hid).astype(out_ref.dtype)
    for p in pairs:
        s, h = p
        b_last = b_c[p][ML_CHUNK - 1:ML_CHUNK, :]
        m_new = m[p][ML_CHUNK - 1:ML_CHUNK, :]
        w = jnp.exp(b_last - b_c[p] + i_c[p] - m_new)
        decay = jnp.exp(b_last + m_prev[p] - m_new)
        wv = (w * vh[p].astype(F32)).astype(BF16)
        ct[s, h] = decay * c_old[p] + _dot_tn(kh[p], wv)
        m_s[s, h] = jnp.broadcast_to(m_new, m_s.shape[2:])


def _mlstm(ml, og, gates, conv_w, conv_b, qk_scale, batch, seq):
    nc = seq // ML_CHUNK
    group = ML_SEQS_PER_STEP if batch % ML_SEQS_PER_STEP == 0 else 1

    def whole(a):
        return pl.BlockSpec(a.shape, lambda b, c: (0, 0))

    def chunk(width, col):
        return pl.BlockSpec((group, ML_CHUNK, width), lambda b, c: (b, c, col))

    out = pl.pallas_call(
        _mlstm_kernel,
        grid=(batch // group, nc),
        in_specs=[chunk(2 * ML_PAD_WIDTH, 0), chunk(ML_PAD_WIDTH, 2), chunk(ML_PAD_WIDTH, 0), chunk(LANES, 0),
                  whole(conv_w), whole(conv_b), whole(qk_scale)],
        out_specs=chunk(ML_PAD_WIDTH, 0),
        out_shape=jax.ShapeDtypeStruct((batch, seq, ML_PAD_WIDTH), BF16),
        scratch_shapes=[pltpu.VMEM((group, 2 * ML_CHUNK, 2 * ML_PAD_WIDTH), BF16),
                        pltpu.VMEM((group, ML_HEADS, ML_HEAD_PAD, ML_HEAD_PAD), F32),
                        pltpu.VMEM((group, ML_HEADS, 8, LANES), F32)],
        compiler_params=_cparams("parallel", "arbitrary"),
        name="mlstm",
    )(ml.reshape(batch, seq, -1), ml.reshape(batch, seq, -1), og.reshape(batch, seq, -1),
      gates.reshape(batch, seq, -1), conv_w, conv_b, qk_scale)
    return out.reshape(batch * seq, ML_PAD_WIDTH)


def _mixer_out_kernel(x_ref, o0_ref, l0_ref, o1_ref, l1_ref, o2_ref, l2_ref, sg_ref, hc_ref, gp_ref,
                      sgg_ref, sgb_ref, sgw_ref, sgbias_ref, wa_ref, wb_ref, wc_ref, wo_ref, lng_ref, lnb_ref,
                      xf_ref, xb_ref, unperm, *, alpha):
    tm = x_ref.shape[0]
    slabs = ATT_GROUP_WIDTH // LANES
    att_refs = (l0_ref, l1_ref, l2_ref, o0_ref, o1_ref, o2_ref)

    for slot, ref in enumerate(att_refs):
        dil = ref.shape[0]
        if dil > 1:
            for r in range(dil):
                for c in range(slabs):
                    unperm[slabs * slot + c, pl.ds(r, tm // dil, stride=dil), :] = (
                        ref[r, :, c * LANES:(c + 1) * LANES].astype(F32))

    def token_order(slot, rows):
        ref = att_refs[slot]
        if ref.shape[0] == 1:
            return ref[0, rows, :].astype(F32)
        return jnp.concatenate([unperm[slabs * slot + c, rows, :] for c in range(slabs)], axis=1)

    row = lax.broadcasted_iota(jnp.int32, (SG_CHUNK, SG_CHUNK), 0)
    colm = lax.broadcasted_iota(jnp.int32, (SG_CHUNK, SG_CHUNK), 1)
    causal = row >= colm
    group_dim = SG_WIDTH // SG_GROUPS
    sg_w = [jnp.where(causal, sgw_ref[g], 0.0).astype(BF16) for g in range(SG_GROUPS)]

    for h in range(tm // MIXER_SUB_ROWS):
        rows = slice(h * MIXER_SUB_ROWS, (h + 1) * MIXER_SUB_ROWS)

        l0, l1, l2 = token_order(0, rows), token_order(1, rows), token_order(2, rows)
        lmax = jnp.maximum(jnp.maximum(l0, l1), l2)
        e0, e1, e2 = jnp.exp(l0 - lmax), jnp.exp(l1 - lmax), jnp.exp(l2 - lmax)
        inv = 1.0 / (e0 + e1 + e2)
        y_a = ((e0 * inv) * token_order(3, rows) + (e1 * inv) * token_order(4, rows)
               + (e2 * inv) * token_order(5, rows))

        u = sg_ref[rows, 0:SG_WIDTH].astype(F32)
        v = sg_ref[rows, SG_WIDTH:2 * SG_WIDTH].astype(F32)
        v = _layer_norm(v, sgg_ref[...], sgb_ref[...]).astype(BF16)
        chunks = []
        for c in range(MIXER_SUB_ROWS // SG_CHUNK):
            crows = slice(c * SG_CHUNK, (c + 1) * SG_CHUNK)
            parts = [_dot(sg_w[g], v[crows, g * group_dim:(g + 1) * group_dim]) + sgbias_ref[:, g:g + 1]
                     for g in range(SG_GROUPS)]
            chunks.append(jnp.concatenate(parts, axis=1))
        y_b = u * jnp.concatenate(chunks, axis=0)

        def gate(i):
            return gp_ref[rows, i * D_MODEL:(i + 1) * D_MODEL].astype(F32)

        z = (gate(0) * _dot(y_a.astype(BF16), wa_ref[...])
             + gate(1) * _dot(y_b.astype(BF16), wb_ref[...])
             + gate(2) * _dot(hc_ref[rows, :], wc_ref[...]))
        mix = _dot(z.astype(BF16), wo_ref[...])
        out = _layer_norm(alpha * x_ref[rows, :] + mix, lng_ref[...], lnb_ref[...])
        xf_ref[rows, :] = out
        xb_ref[rows, :] = out.astype(BF16)


def _mixer_out(x, att_outs, sg, hc, gpre, p, alpha, seq, tm=512):
    n = x.shape[0]
    tiles_per_seq = seq // tm

    def rows(width):
        return pl.BlockSpec((tm, width), lambda i: (i, 0))

    def whole(a):
        return pl.BlockSpec(a.shape, lambda i: (0,) * a.ndim)

    def residue_major(dil):
        return pl.BlockSpec((None, dil, tm // dil, ATT_GROUP_WIDTH),
                            lambda i: (i // tiles_per_seq, 0, i % tiles_per_seq, 0))

    att_specs, att_args = [], []
    for (o, lse), (_, dil) in zip(att_outs, ATT_GROUPS):
        att_specs += [residue_major(dil)] * 2
        att_args += [o, lse]
    weights = [p["sg_ln_g"], p["sg_ln_b"], p["sg_w"], p["sg_bias_t"], p["w_br_a"], p["w_br_b"], p["w_br_c"],
               p["w_out"], p["ln_g0"], p["ln_b0"]]
    return pl.pallas_call(
        functools.partial(_mixer_out_kernel, alpha=alpha),
        grid=(n // tm,),
        in_specs=[rows(D_MODEL)] + att_specs
                 + [rows(2 * SG_WIDTH), rows(ML_PAD_WIDTH), rows(N_BRANCH * D_MODEL)]
                 + [whole(w) for w in weights],
        out_specs=[rows(D_MODEL), rows(D_MODEL)],
        out_shape=[jax.ShapeDtypeStruct((n, D_MODEL), F32), jax.ShapeDtypeStruct((n, D_MODEL), BF16)],
        scratch_shapes=[pltpu.VMEM((2 * len(ATT_GROUPS) * ATT_GROUP_WIDTH // LANES, tm, LANES), F32)],
        compiler_params=_cparams("parallel"),
        name="mixer_out",
    )(x, *att_args, sg, hc, gpre, *weights)


def _swiglu_tile(x, w1_ref, w3_ref, w2_ref, prod, tf):
    for c in range(D_FF // tf):
        cols = slice(c * tf, (c + 1) * tf)
        h1 = _dot(x, w1_ref[:, cols])
        h3 = _dot(x, w3_ref[:, cols])
        prod[:, cols] = (h1 * _sigmoid(h1) * h3).astype(BF16)
    return _dot(prod[...], w2_ref[...])


def _ffn_kernel(xf_ref, xb_ref, w1_ref, w3_ref, w2_ref, lng_ref, lnb_ref, of_ref, ob_ref, prod, *, alpha, tf):
    f = _swiglu_tile(xb_ref[...], w1_ref, w3_ref, w2_ref, prod, tf)
    out = _layer_norm(alpha * xf_ref[...] + f, lng_ref[...], lnb_ref[...])
    of_ref[...] = out
    ob_ref[...] = out.astype(BF16)


def _ffn(xf, xb, w1, w3, w2, ln_g, ln_b, alpha, tm=512, tf=256):
    n = xf.shape[0]

    def resident(shape):
        return pl.BlockSpec(shape, lambda i: (0, 0), pipeline_mode=pl.Buffered(1))

    return pl.pallas_call(
        functools.partial(_ffn_kernel, alpha=alpha, tf=tf),
        grid=(n // tm,),
        in_specs=[pl.BlockSpec((tm, D_MODEL), lambda i: (i, 0)),
                  pl.BlockSpec((tm, D_MODEL), lambda i: (i, 0)),
                  resident((D_MODEL, D_FF)), resident((D_MODEL, D_FF)), resident((D_FF, D_MODEL)),
                  pl.BlockSpec((1, D_MODEL), lambda i: (0, 0)),
                  pl.BlockSpec((1, D_MODEL), lambda i: (0, 0))],
        out_specs=[pl.BlockSpec((tm, D_MODEL), lambda i: (i, 0))] * 2,
        out_shape=[jax.ShapeDtypeStruct((n, D_MODEL), F32), jax.ShapeDtypeStruct((n, D_MODEL), BF16)],
        scratch_shapes=[pltpu.VMEM((tm, D_FF), BF16)],
        compiler_params=_cparams("parallel"),
        name="ffn",
    )(xf, xb, w1, w3, w2, ln_g, ln_b)


def _router_kernel(x_ref, w_ref, b_ref, info_ref):
    xh, xm, _ = _split3(x_ref[...])
    wh, wm, _ = _split3(w_ref[...])
    logits = (_dot(xh, wh) + (_dot(xh, wm) + _dot(xm, wh))) + b_ref[...]
    lane = lax.broadcasted_iota(jnp.int32, logits.shape, 1)
    neg_inf = jnp.float32(-jnp.inf)
    logits = jnp.where(lane < N_EXPERTS, logits, neg_inf)
    v1 = jnp.max(logits, axis=-1, keepdims=True)
    i1 = jnp.min(jnp.where(logits == v1, lane, LANES), axis=-1, keepdims=True)
    rest = jnp.where(lane == i1, neg_inf, logits)
    v2 = jnp.max(rest, axis=-1, keepdims=True)
    i2 = jnp.min(jnp.where(rest == v2, lane, LANES), axis=-1, keepdims=True)
    e = jnp.exp(v2 - v1)
    inv = 1.0 / (1.0 + e)
    info_ref[...] = jnp.where(lane == 0, i1.astype(F32),
                              jnp.where(lane == 1, i2.astype(F32),
                                        jnp.where(lane == 2, inv, jnp.where(lane == 3, e * inv, 0.0))))


def _router(xf, router_w, router_b, tm=1024):
    n = xf.shape[0]
    w = jnp.pad(router_w, ((0, 0), (0, LANES - N_EXPERTS)))
    b = jnp.pad(router_b, (0, LANES - N_EXPERTS))[None, :]
    return pl.pallas_call(
        _router_kernel,
        grid=(n // tm,),
        in_specs=[pl.BlockSpec((tm, D_MODEL), lambda i: (i, 0)),
                  pl.BlockSpec((D_MODEL, LANES), lambda i: (0, 0)),
                  pl.BlockSpec((1, LANES), lambda i: (0, 0))],
        out_specs=pl.BlockSpec((tm, LANES), lambda i: (i, 0)),
        out_shape=jax.ShapeDtypeStruct((n, LANES), F32),
        compiler_params=_cparams("parallel"),
        name="router",
    )(xf, w, b)


def _routing_plan(info, n):
    tile = EXPERT_TILE
    n_tiles = TOP_K * n // tile + N_EXPERTS
    expert = info[:, 0:TOP_K].astype(jnp.int32).reshape(-1)
    onehot = (expert[:, None] == jnp.arange(N_EXPERTS, dtype=jnp.int32)[None, :]).astype(jnp.int32)
    running = jnp.cumsum(onehot, axis=0)
    rank = jnp.sum(onehot * running, axis=1) - 1
    counts = running[-1]
    padded = ((counts + tile - 1) // tile) * tile
    ends = jnp.cumsum(padded)
    starts = ends - padded
    dest = jnp.sum(onehot * starts[None, :], axis=1) + rank
    used = ends[-1] // tile
    tile_start = jnp.arange(n_tiles, dtype=jnp.int32) * tile
    tile_expert = jnp.sum((tile_start[:, None] >= ends[None, :]).astype(jnp.int32), axis=1)
    last_expert = jnp.sum((tile_start[used - 1] >= ends).astype(jnp.int32))
    tile_expert = jnp.minimum(tile_expert, last_expert)
    plan = dict(dest=dest, tile_expert=tile_expert, used=used.reshape(1),
                pad_start=starts + counts, pad_count=padded - counts)
    return {k: v.astype(jnp.int32) for k, v in plan.items()}, n_tiles


def _rows_wait(src, dst, rows, sem):
    pltpu.make_async_copy(src.at[pl.ds(0, rows)], dst.at[pl.ds(0, rows)], sem).wait()


ROW_SUBLANES = D_MODEL // LANES


def _tile_copy(src, src_row, dst, dst_row, sem):
    s0 = pl.multiple_of(src_row * ROW_SUBLANES, ROW_SUBLANES)
    d0 = pl.multiple_of(dst_row * ROW_SUBLANES, ROW_SUBLANES)
    return pltpu.make_async_copy(src.at[pl.ds(s0, ROW_SUBLANES)], dst.at[pl.ds(d0, ROW_SUBLANES)], sem)


def _to_row_tiles(ref, x):
    m = x.shape[0]
    for c in range(ROW_SUBLANES):
        ref[pl.ds(c, m, stride=ROW_SUBLANES), :] = x[:, c * LANES:(c + 1) * LANES]


def _from_row_tiles(ref):
    m = ref.shape[0] // ROW_SUBLANES
    return jnp.concatenate([ref[pl.ds(c, m, stride=ROW_SUBLANES), :] for c in range(ROW_SUBLANES)], axis=1)


DISPATCH_SLOTS = 3


def _dispatch_kernel(pad_start_ref, pad_count_ref, used_ref, dest_ref, x_ref, xs_ref, buf, tiles, zero_row,
                     load_sem, row_sem, pad_sem, *, tm):
    i = pl.program_id(0)

    @pl.when(i == 0)
    def _():
        zero_row[...] = jnp.zeros(zero_row.shape, F32)
        for e in range(N_EXPERTS):
            def fill(j, carry, e=e):
                _tile_copy(zero_row, 0, xs_ref, pad_start_ref[e] + j, pad_sem).start()
                return carry
            lax.fori_loop(0, pad_count_ref[e], fill, 0)
        for e in range(N_EXPERTS):
            def drain(j, carry):
                _tile_copy(zero_row, 0, xs_ref, 0, pad_sem).wait()
                return carry
            lax.fori_loop(0, pad_count_ref[e], drain, 0)

        tile_rows = tm * ROW_SUBLANES
        spare = xs_ref.shape[0] // tile_rows - used_ref[0]
        tiles[1] = jnp.zeros(tiles.shape[1:], F32)

        def zero_tile(j):
            row0 = pl.multiple_of((used_ref[0] + j) * tile_rows, tile_rows)
            return pltpu.make_async_copy(tiles.at[1], xs_ref.at[pl.ds(row0, tile_rows)], pad_sem)

        def fill_tile(j, carry):
            zero_tile(j).start()
            return carry

        def drain_tile(j, carry):
            zero_tile(j).wait()
            return carry

        lax.fori_loop(0, spare, fill_tile, 0)
        lax.fori_loop(0, spare, drain_tile, 0)

    steps = pl.num_programs(0)
    slot = i % DISPATCH_SLOTS
    prev_slot = (i + DISPATCH_SLOTS - 1) % DISPATCH_SLOTS
    stage = i % 2

    def load(tile, into):
        return pltpu.make_async_copy(x_ref.at[pl.ds(tile * tm, tm), :], buf.at[into], load_sem.at[into])

    def wait_rows(of_stage):
        for _ in range(TOP_K):
            _rows_wait(tiles.at[of_stage], xs_ref, tm * ROW_SUBLANES, row_sem.at[of_stage])

    @pl.when(i == 0)
    def _():
        load(0, 0).start()

        @pl.when(steps > 1)
        def _():
            load(1, 1).start()

    load(i, slot).wait()
    _to_row_tiles(tiles.at[stage], buf[slot])

    def start(t, carry):
        for c in range(TOP_K):
            _tile_copy(tiles.at[stage], t, xs_ref, dest_ref[0, 0, TOP_K * t + c], row_sem.at[stage]).start()
        return carry

    lax.fori_loop(0, tm, start, 0, unroll=DMA_ISSUE_UNROLL)

    @pl.when(i > 0)
    def _():
        wait_rows(1 - stage)

    @pl.when(i + 2 < steps)
    def _():
        load(i + 2, prev_slot).start()

    @pl.when(i == steps - 1)
    def _():
        wait_rows(stage)


def _dispatch(xf, dest, pad_start, pad_count, used, n_slots, tm=EXPERT_TILE):
    n = xf.shape[0]
    dest3 = dest.reshape(n // tm, 1, TOP_K * tm)
    grid_spec = pltpu.PrefetchScalarGridSpec(
        num_scalar_prefetch=3,
        grid=(n // tm,),
        in_specs=[pl.BlockSpec((1, 1, TOP_K * tm), lambda i, ps, pc, nu: (i, 0, 0), memory_space=pltpu.SMEM),
                  pl.BlockSpec(memory_space=pl.ANY)],
        out_specs=pl.BlockSpec(memory_space=pl.ANY),
        scratch_shapes=[pltpu.VMEM((DISPATCH_SLOTS, tm, D_MODEL), F32),
                        pltpu.VMEM((2, tm * ROW_SUBLANES, LANES), F32),
                        pltpu.VMEM((ROW_SUBLANES, LANES), F32),
                        pltpu.SemaphoreType.DMA((DISPATCH_SLOTS,)),
                        pltpu.SemaphoreType.DMA((2,)),
                        pltpu.SemaphoreType.DMA],
    )
    return pl.pallas_call(
        functools.partial(_dispatch_kernel, tm=tm),
        grid_spec=grid_spec,
        out_shape=jax.ShapeDtypeStruct((n_slots * ROW_SUBLANES, LANES), F32),
        compiler_params=_cparams("arbitrary"),
        name="moe_dispatch",
    )(pad_start, pad_count, used, dest3, xf)


def _experts_kernel(te_ref, used_ref, xs_ref, w1_ref, w3_ref, w2_ref, ys_ref, prod, *, tf):
    del te_ref
    i = pl.program_id(0)

    @pl.when(i < used_ref[0])
    def _():
        x = _from_row_tiles(xs_ref).astype(BF16)
        _to_row_tiles(ys_ref, _swiglu_tile(x, w1_ref, w3_ref, w2_ref, prod, tf))

    @pl.when(i >= used_ref[0])
    def _():
        ys_ref[...] = jnp.zeros(ys_ref.shape, ys_ref.dtype)


def _experts(xs, tile_expert, used, w1, w3, w2, n_tiles, tf=256):
    tile = EXPERT_TILE
    grid_spec = pltpu.PrefetchScalarGridSpec(
        num_scalar_prefetch=2,
        grid=(n_tiles,),
        in_specs=[pl.BlockSpec((tile * ROW_SUBLANES, LANES), lambda i, te, nu: (jnp.minimum(i, nu[0] - 1), 0)),
                  pl.BlockSpec((None, D_MODEL, D_FF), lambda i, te, nu: (te[i], 0, 0)),
                  pl.BlockSpec((None, D_MODEL, D_FF), lambda i, te, nu: (te[i], 0, 0)),
                  pl.BlockSpec((None, D_FF, D_MODEL), lambda i, te, nu: (te[i], 0, 0))],
        out_specs=pl.BlockSpec((tile * ROW_SUBLANES, LANES), lambda i, te, nu: (i, 0)),
        scratch_shapes=[pltpu.VMEM((tile, D_FF), BF16)],
    )
    return pl.pallas_call(
        functools.partial(_experts_kernel, tf=tf),
        grid_spec=grid_spec,
        out_shape=jax.ShapeDtypeStruct((n_tiles * tile * ROW_SUBLANES, LANES), F32),
        compiler_params=_cparams("arbitrary"),
        name="moe_experts",
    )(tile_expert, used, xs, w1, w3, w2)


def _combine_kernel(dest_ref, next_dest_ref, xf_ref, info_ref, ys_ref, lng_ref, lnb_ref, of_ref, got, sem, *, alpha):
    tm = xf_ref.shape[0]
    i = pl.program_id(0)
    slot = i % 2

    def gather(dests, into):
        def start(t, carry):
            for c in range(TOP_K):
                _tile_copy(ys_ref, dests[0, 0, TOP_K * t + c], got.at[into, c], t, sem.at[into]).start()
            return carry
        lax.fori_loop(0, tm, start, 0, unroll=DMA_ISSUE_UNROLL)

    @pl.when(i == 0)
    def _():
        gather(dest_ref, 0)

    @pl.when(i + 1 < pl.num_programs(0))
    def _():
        gather(next_dest_ref, 1 - slot)

    for c in range(TOP_K):
        _rows_wait(ys_ref, got.at[slot, c], tm * ROW_SUBLANES, sem.at[slot])
    info = info_ref[...]
    mix = (info[:, 2:3] * _from_row_tiles(got.at[slot, 0]) + info[:, 3:4] * _from_row_tiles(got.at[slot, 1]))
    of_ref[...] = _layer_norm(alpha * xf_ref[...] + mix, lng_ref[...], lnb_ref[...])


def _combine(xf, info, ys, dest, ln_g, ln_b, alpha, tm=256):
    n = xf.shape[0]
    steps = n // tm
    dest3 = dest.reshape(steps, 1, TOP_K * tm)
    return pl.pallas_call(
        functools.partial(_combine_kernel, alpha=alpha),
        grid=(steps,),
        in_specs=[pl.BlockSpec((1, 1, TOP_K * tm), lambda i: (i, 0, 0), memory_space=pltpu.SMEM),
                  pl.BlockSpec((1, 1, TOP_K * tm), lambda i: (jnp.minimum(i + 1, steps - 1), 0, 0),
                               memory_space=pltpu.SMEM),
                  pl.BlockSpec((tm, D_MODEL), lambda i: (i, 0)),
                  pl.BlockSpec((tm, LANES), lambda i: (i, 0)),
                  pl.BlockSpec(memory_space=pl.ANY),
                  pl.BlockSpec((1, D_MODEL), lambda i: (0, 0)),
                  pl.BlockSpec((1, D_MODEL), lambda i: (0, 0))],
        out_specs=pl.BlockSpec((tm, D_MODEL), lambda i: (i, 0)),
        out_shape=jax.ShapeDtypeStruct((n, D_MODEL), F32),
        scratch_shapes=[pltpu.VMEM((2, TOP_K, tm * ROW_SUBLANES, LANES), F32), pltpu.SemaphoreType.DMA((2,))],
        compiler_params=_cparams("arbitrary"),
        name="moe_combine",
    )(dest3, dest3, xf, info, ys, ln_g, ln_b)


def _moe(xf, router_w, router_b, w1, w3, w2, ln_g, ln_b, alpha):
    n = xf.shape[0]
    info = _router(xf, router_w, router_b)
    plan, n_tiles = _routing_plan(info, n)
    xs = _dispatch(xf, plan["dest"], plan["pad_start"], plan["pad_count"], plan["used"], n_tiles * EXPERT_TILE)
    ys = _experts(xs, plan["tile_expert"], plan["used"], w1, w3, w2, n_tiles)
    return _combine(xf, info, ys, plan["dest"], ln_g, ln_b, alpha)


def _pad_heads(w):
    lead = w.shape[:-1]
    w = w.reshape(*lead, ML_HEADS, ML_HEAD_DIM)
    w = jnp.pad(w, [(0, 0)] * len(lead) + [(0, 0), (0, ML_HEAD_PAD - ML_HEAD_DIM)])
    return w.reshape(*lead, ML_PAD_WIDTH)


def _group_qkv(w, group):
    gw = ATT_GROUP_WIDTH
    return jnp.concatenate([w[..., part * ATT_WIDTH + group * gw:part * ATT_WIDTH + (group + 1) * gw]
                            for part in range(3)], axis=-1)


def _layer_params(layer, w_in, b_in, conv_w, conv_b, sg_ln_g, sg_ln_b, sg_w, sg_b,
                  w_br_a, w_br_b, w_br_c, w_out, ln_g, ln_b):
    w, b = w_in[layer], b_in[layer]
    o_att, o_sg, o_qk, o_v, o_o, o_i, o_g = 0, 2304, 3840, 5376, 6144, 6912, 6920
    ones_col = jnp.zeros((ML_HEADS, ML_HEAD_PAD), F32).at[:, ML_HEAD_DIM].set(1.0).reshape(ML_PAD_WIDTH)
    w_qk = jnp.concatenate([_pad_heads(w[:, o_qk:o_qk + ML_WIDTH]), _pad_heads(w[:, o_qk + ML_WIDTH:o_v])], axis=1)
    b_qk = jnp.concatenate([_pad_heads(b[o_qk:o_qk + ML_WIDTH]), _pad_heads(b[o_qk + ML_WIDTH:o_v])])
    qk_scale = jnp.concatenate([jnp.ones((ML_PAD_WIDTH,), F32),
                                jnp.full((ML_PAD_WIDTH,), ML_HEAD_DIM ** -0.5, F32)])
    cw, cb = conv_w[layer], conv_b[layer]
    w_att, b_att = w[:, o_att:o_sg], b[o_att:o_sg]
    return dict(
        w_att=[_group_qkv(w_att, g).astype(BF16) for g in range(len(ATT_GROUPS))],
        b_att=[_group_qkv(b_att, g)[None, :] for g in range(len(ATT_GROUPS))],
        w_sg=w[:, o_sg:o_qk].astype(BF16), b_sg=b[None, o_sg:o_qk],
        w_ml=jnp.concatenate([w_qk, _pad_heads(w[:, o_v:o_o])], axis=1).astype(BF16),
        b_ml=jnp.concatenate([b_qk, _pad_heads(b[o_v:o_o]) + ones_col])[None, :],
        qk_scale=qk_scale[None, :],
        w_mlo=_pad_heads(w[:, o_o:o_i]).astype(BF16), b_mlo=_pad_heads(b[o_o:o_i])[None, :],
        w_if=jnp.pad(w[:, o_i:o_g], ((0, 0), (0, LANES - 2 * ML_HEADS))).astype(BF16),
        b_if=jnp.pad(b[o_i:o_g], (0, LANES - 2 * ML_HEADS))[None, :],
        w_g=w[:, o_g:].astype(BF16), b_g=b[None, o_g:],
        conv_w=jnp.concatenate([_pad_heads(cw[:, :ML_WIDTH]), _pad_heads(cw[:, ML_WIDTH:])], axis=1),
        conv_b=jnp.concatenate([_pad_heads(cb[:ML_WIDTH]), _pad_heads(cb[ML_WIDTH:])])[None, :],
        sg_ln_g=sg_ln_g[layer][None, :], sg_ln_b=sg_ln_b[layer][None, :],
        sg_w=sg_w[layer], sg_bias_t=sg_b[layer].T,
        w_br_a=w_br_a[layer].astype(BF16), w_br_b=w_br_b[layer].astype(BF16),
        w_br_c=_pad_heads(w_br_c[layer].T).T.astype(BF16),
        w_out=w_out[layer].astype(BF16),
        ln_g0=ln_g[layer, 0][None, :], ln_b0=ln_b[layer, 0][None, :],
        ln_g1=ln_g[layer, 1][None, :], ln_b1=ln_b[layer, 1][None, :],
    )


def kernel(x, positions, w_in, b_in, conv_w, conv_b, sg_ln_g, sg_ln_b, sg_w, sg_b, w_br_a, w_br_b, w_br_c, w_out,
           ln_g, ln_b, ffn_w1, ffn_w3, ffn_w2, router_w, router_b, moe_w1, moe_w3, moe_w2):
    batch, seq, _ = x.shape
    n = batch * seq
    depth = w_in.shape[0]
    alpha = (2.0 * depth) ** 0.25
    cos, sin = _rope_tables(positions)
    xf = x.reshape(n, D_MODEL)
    xb = xf.astype(BF16)
    for layer in range(depth):
        p = _layer_params(layer, w_in, b_in, conv_w, conv_b, sg_ln_g, sg_ln_b, sg_w, sg_b,
                          w_br_a, w_br_b, w_br_c, w_out, ln_g, ln_b)
        sg = _linear(xb, p["w_sg"], p["b_sg"], BF16, 1024, 2 * SG_WIDTH, act=_gelu_tanh)
        ml = _linear(xb, p["w_ml"], p["b_ml"], BF16, 1024, 3 * ML_PAD_WIDTH)
        og = _linear(xb, p["w_mlo"], p["b_mlo"], BF16, 2048, ML_PAD_WIDTH, act=_sigmoid)
        gates_if = _linear(xb, p["w_if"], p["b_if"], F32, 2048, LANES)
        gbr = _linear(xb, p["w_g"], p["b_g"], BF16, 1024, N_BRANCH * D_MODEL, act=_sigmoid)
        att_outs = []
        for g, (_, dil) in enumerate(ATT_GROUPS):
            qkv = _linear_dilated(xb, p["w_att"][g], p["b_att"][g], dil, batch, seq)
            att_outs.append(_attention_group(qkv, cos, sin, g, batch, seq))
        hc = _mlstm(ml, og, gates_if, p["conv_w"], p["conv_b"], p["qk_scale"], batch, seq)
        xf, xb = _mixer_out(xf, att_outs, sg, hc, gbr, p, alpha, seq)
        j = layer // 2
        if layer % 2 == 0:
            xf, xb = _ffn(xf, xb, ffn_w1[j].astype(BF16), ffn_w3[j].astype(BF16), ffn_w2[j].astype(BF16),
                          p["ln_g1"], p["ln_b1"], alpha)
        else:
            xf = _moe(xf, router_w[j], router_b[j], moe_w1[j].astype(BF16), moe_w3[j].astype(BF16),
                      moe_w2[j].astype(BF16), p["ln_g1"], p["ln_b1"], alpha)
            xb = xf.astype(BF16)
    return xf.reshape(batch, seq, D_MODEL)
```

```python
import functools

import numpy as np
import jax
import jax.numpy as jnp
from jax import lax
from jax.experimental import pallas as pl
from jax.experimental.pallas import tpu as pltpu

F32 = jnp.float32
BF16 = jnp.bfloat16

D_MODEL = 1024
ATT_GROUPS = ((128, 1), (512, 4), (2048, 16))
ATT_HEADS_PER_GROUP = 4
ATT_HEAD_DIM = 64
ATT_WIDTH = 768
ATT_GROUP_WIDTH = ATT_HEADS_PER_GROUP * ATT_HEAD_DIM
ROPE_THETA = 10000.0
SG_CHUNK = 128
SG_GROUPS = 6
SG_WIDTH = 768
ML_HEADS = 4
ML_HEAD_DIM = 192
ML_HEAD_PAD = 256
ML_WIDTH = 768
ML_PAD_WIDTH = ML_HEADS * ML_HEAD_PAD
ML_CHUNK = 128
ML_CONV = 4
N_BRANCH = 3
D_FF = 2816
N_EXPERTS = 8
TOP_K = 2
LN_EPS = 1e-5
BLOCK = 128
LANES = 128
VMEM_LIMIT = 56 * 1024 * 1024
EXPERT_TILE = 512
DMA_ISSUE_UNROLL = 8
MIXER_SUB_ROWS = 256


def _cparams(*sem):
    return pltpu.CompilerParams(dimension_semantics=sem, vmem_limit_bytes=VMEM_LIMIT)


def _layer_norm(x, g, b):
    mu = jnp.mean(x, axis=-1, keepdims=True)
    xc = x - mu
    var = jnp.mean(xc * xc, axis=-1, keepdims=True)
    return xc * lax.rsqrt(var + LN_EPS) * g + b


def _gelu_tanh(x):
    return 0.5 * x * (1.0 + jnp.tanh(np.sqrt(2.0 / np.pi) * (x + 0.044715 * (x * x * x))))


def _sigmoid(x):
    return 0.5 * (1.0 + jnp.tanh(0.5 * x))


def _dot(a, b):
    return jnp.dot(a, b, preferred_element_type=F32)


def _dot_nt(a, b):
    return lax.dot_general(a, b, (((1,), (1,)), ((), ())), preferred_element_type=F32)


def _dot_tn(a, b):
    return lax.dot_general(a, b, (((0,), (0,)), ((), ())), preferred_element_type=F32)


def _split3(x):
    hi = x.astype(BF16)
    r1 = x - hi.astype(F32)
    mid = r1.astype(BF16)
    lo = (r1 - mid.astype(F32)).astype(BF16)
    return hi, mid, lo


def _linear_kernel(x_ref, w_ref, b_ref, o_ref, *, act):
    res = _dot(x_ref[...], w_ref[...]) + b_ref[...]
    o_ref[...] = (res if act is None else act(res)).astype(o_ref.dtype)


def _linear(x, w, b, out_dtype, tm, tn, act=None):
    m, k = x.shape
    n = w.shape[1]
    mode = dict(pipeline_mode=pl.Buffered(1)) if tn == n else {}
    return pl.pallas_call(
        functools.partial(_linear_kernel, act=act),
        grid=(m // tm, n // tn),
        in_specs=[pl.BlockSpec((tm, k), lambda i, j: (i, 0)),
                  pl.BlockSpec((k, tn), lambda i, j: (0, j), **mode),
                  pl.BlockSpec((1, tn), lambda i, j: (0, j), **mode)],
        out_specs=pl.BlockSpec((tm, tn), lambda i, j: (i, j)),
        out_shape=jax.ShapeDtypeStruct((m, n), out_dtype),
        compiler_params=_cparams("parallel", "parallel"),
        name="linear",
    )(x, w, b)


def _linear_dilated_kernel(x_ref, w_ref, b_ref, o_ref, acc, *, dil):
    res = _dot(x_ref[...], w_ref[...]) + b_ref[...]
    slabs, rows, _ = acc.shape
    for c in range(slabs):
        acc[c] = res[:, c * LANES:(c + 1) * LANES]
    sub = rows // dil
    for r in range(dil):
        for c in range(slabs):
            o_ref[r, :, c * LANES:(c + 1) * LANES] = acc[c, pl.ds(r, sub, stride=dil), :].astype(o_ref.dtype)


def _linear_dilated(x, w, b, dil, batch, seq):
    k = x.shape[1]
    n = w.shape[1]
    tn = ATT_GROUP_WIDTH
    sub = seq // dil
    return pl.pallas_call(
        functools.partial(_linear_dilated_kernel, dil=dil),
        grid=(batch, n // tn),
        in_specs=[pl.BlockSpec((seq, k), lambda i, j: (i, 0)),
                  pl.BlockSpec((k, tn), lambda i, j: (0, j)),
                  pl.BlockSpec((1, tn), lambda i, j: (0, j))],
        out_specs=pl.BlockSpec((None, dil, sub, tn), lambda i, j: (i, 0, 0, j)),
        out_shape=jax.ShapeDtypeStruct((batch, dil, sub, n), BF16),
        scratch_shapes=[pltpu.VMEM((tn // LANES, seq, LANES), F32)],
        compiler_params=_cparams("parallel", "parallel"),
        name=f"linear_dil{dil}",
    )(x, w, b)


def _rope_table_kernel(pos_ref, freq_ref, sign_ref, cos_ref, sin_ref):
    ang = pos_ref[...] * freq_ref[...]
    cos_ref[...] = jnp.cos(ang)
    sin_ref[...] = jnp.sin(ang) * sign_ref[...]


def _rope_tables(positions):
    n = positions.size
    half = ATT_HEAD_DIM // 2
    freqs = ROPE_THETA ** (-jnp.arange(half, dtype=F32) * (2.0 / ATT_HEAD_DIM))
    lane = np.arange(LANES)
    freq_row = freqs[lane % half][None, :]
    sign_row = jnp.asarray(np.where(lane % ATT_HEAD_DIM < half, -1.0, 1.0), F32)[None, :]
    pos = positions.astype(F32).reshape(n, 1)
    tm = 2048
    return pl.pallas_call(
        _rope_table_kernel,
        grid=(n // tm,),
        in_specs=[pl.BlockSpec((tm, 1), lambda i: (i, 0)),
                  pl.BlockSpec((1, LANES), lambda i: (0, 0)),
                  pl.BlockSpec((1, LANES), lambda i: (0, 0))],
        out_specs=[pl.BlockSpec((tm, LANES), lambda i: (i, 0))] * 2,
        out_shape=[jax.ShapeDtypeStruct((n, LANES), F32)] * 2,
        compiler_params=_cparams("parallel"),
        name="rope_tables",
    )(pos, freq_row, sign_row)


def _attn_kernel(q_ref, k_ref, v_ref, cos_ref, sin_ref, o_ref, lse_ref, qs, ks, vs, *, sub, dil):
    nb = sub // BLOCK
    res = pl.program_id(1)
    lane = lax.broadcasted_iota(jnp.int32, (BLOCK, LANES), 1)
    first_half = (lane % ATT_HEAD_DIM) < (ATT_HEAD_DIM // 2)
    low_head = lane < ATT_HEAD_DIM

    def rope(x, c, s):
        partner = jnp.where(first_half, pltpu.roll(x, LANES - 32, 1), pltpu.roll(x, 32, 1))
        return x * c + partner * s

    ks[0:BLOCK, :] = jnp.zeros((BLOCK, ATT_GROUP_WIDTH), BF16)
    vs[0:BLOCK, :] = jnp.zeros((BLOCK, ATT_GROUP_WIDTH), BF16)

    def prep(n, carry):
        r0 = pl.multiple_of(n * BLOCK, BLOCK)
        if dil == 1:
            c = cos_ref[pl.ds(r0, BLOCK), :]
            s = sin_ref[pl.ds(r0, BLOCK), :]
        else:
            c = cos_ref[pl.ds(r0 * dil + res, BLOCK, stride=dil), :]
            s = sin_ref[pl.ds(r0 * dil + res, BLOCK, stride=dil), :]
        for pair in range(2):
            cols = slice(pair * LANES, (pair + 1) * LANES)
            q = rope(q_ref[pl.ds(r0, BLOCK), cols].astype(F32), c, s) * (ATT_HEAD_DIM ** -0.5)
            qs[pl.ds(r0, BLOCK), (2 * pair) * LANES:(2 * pair + 1) * LANES] = jnp.where(low_head, q, 0.0).astype(BF16)
            qs[pl.ds(r0, BLOCK), (2 * pair + 1) * LANES:(2 * pair + 2) * LANES] = jnp.where(low_head, 0.0, q).astype(BF16)
            k = rope(k_ref[pl.ds(r0, BLOCK), cols].astype(F32), c, s)
            ks[pl.ds(r0 + BLOCK, BLOCK), cols] = k.astype(BF16)
        vs[pl.ds(r0 + BLOCK, BLOCK), :] = v_ref[pl.ds(r0, BLOCK), :]
        return carry

    lax.fori_loop(0, nb, prep, 0)

    qi = lax.broadcasted_iota(jnp.int32, (BLOCK, 2 * BLOCK), 0)
    ki = lax.broadcasted_iota(jnp.int32, (BLOCK, 2 * BLOCK), 1)
    dist = qi + BLOCK - ki
    band = (dist >= 0) & (dist <= BLOCK)
    neg_inf = jnp.float32(-jnp.inf)
    bias_rest = jnp.where(band, 0.0, neg_inf)
    bias_first = jnp.where(band & (ki >= BLOCK), 0.0, neg_inf)

    def block(n, bias):
        r0 = pl.multiple_of(n * BLOCK, BLOCK)
        heads = range(ATT_HEADS_PER_GROUP)
        pair_cols = [slice(pair * LANES, (pair + 1) * LANES) for pair in range(2)]
        kw = [ks[pl.ds(r0, 2 * BLOCK), c] for c in pair_cols]
        vw = [vs[pl.ds(r0, 2 * BLOCK), c] for c in pair_cols]
        s = [_dot_nt(qs[pl.ds(r0, BLOCK), hd * LANES:(hd + 1) * LANES], kw[hd // 2]) + bias for hd in heads]
        m = [jnp.max(s[hd], axis=-1, keepdims=True) for hd in heads]
        p = [jnp.exp(s[hd] - m[hd]) for hd in heads]
        den = [jnp.sum(p[hd], axis=-1, keepdims=True) for hd in heads]
        o = [_dot(p[hd].astype(BF16), vw[hd // 2]) * (1.0 / den[hd]) for hd in heads]
        lse = [m[hd] + jnp.log(den[hd]) for hd in heads]
        for pair in range(2):
            lo, hi = 2 * pair, 2 * pair + 1
            o_ref[pl.ds(r0, BLOCK), pair_cols[pair]] = jnp.where(low_head, o[lo], o[hi]).astype(o_ref.dtype)
            lse_ref[pl.ds(r0, BLOCK), pair_cols[pair]] = jnp.where(low_head, lse[lo], lse[hi])

    block(0, bias_first)
    if nb <= 2:
        for n in range(1, nb):
            block(n, bias_rest)
        return

    def body(n, carry):
        block(n, bias_rest)
        return carry

    lax.fori_loop(1, nb, body, 0, unroll=2)


def _attention_group(qkv, cos, sin, group, batch, seq):
    window, dil = ATT_GROUPS[group]
    assert window // dil == BLOCK
    sub = seq // dil
    assert sub % BLOCK == 0

    def part(p):
        return pl.BlockSpec((None, None, sub, ATT_GROUP_WIDTH), lambda b, r: (b, r, 0, p))

    tab = pl.BlockSpec((seq, LANES), lambda b, r: (b, 0))
    out = pl.BlockSpec((None, None, sub, ATT_GROUP_WIDTH), lambda b, r: (b, r, 0, 0))
    return pl.pallas_call(
        functools.partial(_attn_kernel, sub=sub, dil=dil),
        grid=(batch, dil),
        in_specs=[part(0), part(1), part(2), tab, tab],
        out_specs=[out, out],
        out_shape=[jax.ShapeDtypeStruct((batch, dil, sub, ATT_GROUP_WIDTH), BF16),
                   jax.ShapeDtypeStruct((batch, dil, sub, ATT_GROUP_WIDTH), F32)],
        scratch_shapes=[pltpu.VMEM((sub, ATT_HEADS_PER_GROUP * LANES), BF16),
                        pltpu.VMEM((sub + BLOCK, ATT_GROUP_WIDTH), BF16),
                        pltpu.VMEM((sub + BLOCK, ATT_GROUP_WIDTH), BF16)],
        compiler_params=_cparams("parallel", "parallel"),
        name=f"attention_g{group}",
    )(qkv, qkv, qkv, cos, sin)


ML_SEQS_PER_STEP = 2


def _mlstm_kernel(qk_ref, v_ref, og_ref, g_ref, cw_ref, cb_ref, sc_ref, out_ref, frame, ct, m_s):
    taps = ML_CONV

    @pl.when(pl.program_id(1) == 0)
    def _():
        frame[:, 0:ML_CHUNK, :] = jnp.zeros((frame.shape[0], ML_CHUNK, 2 * ML_PAD_WIDTH), BF16)
        ct[...] = jnp.zeros(ct.shape, F32)
        m_s[...] = jnp.zeros(m_s.shape, F32)

    out_row = lax.broadcasted_iota(jnp.int32, (ML_CHUNK, 2 * ML_CHUNK), 0)
    src_row = lax.broadcasted_iota(jnp.int32, (ML_CHUNK, 2 * ML_CHUNK), 1)
    picks = [jnp.where(src_row == out_row + (ML_CHUNK - (taps - 1 - j)), 1.0, 0.0).astype(BF16)
             for j in range(taps - 1)]
    row = lax.broadcasted_iota(jnp.int32, (ML_CHUNK, ML_CHUNK), 0)
    colm = lax.broadcasted_iota(jnp.int32, (ML_CHUNK, ML_CHUNK), 1)
    causal = row >= colm
    tri = jnp.where(causal, 1.0, 0.0).astype(BF16)
    neg_inf = jnp.float32(-jnp.inf)

    seqs = range(qk_ref.shape[0])
    pairs = [(s, h) for s in seqs for h in range(ML_HEADS)]

    def head_cols(h):
        return slice(h * ML_HEAD_PAD, (h + 1) * ML_HEAD_PAD)

    act, gates, bcum, bcum_t, gates_t = {}, {}, {}, {}, {}
    for s in seqs:
        cur = qk_ref[s]
        frame[s, ML_CHUNK:2 * ML_CHUNK, :] = cur
        both = frame[s]
        acc = cb_ref[...] + cur.astype(F32) * cw_ref[taps - 1:taps, :]
        for j in range(taps - 1):
            acc = acc + _dot(picks[j], both) * cw_ref[j:j + 1, :]
        frame[s, 0:ML_CHUNK, :] = cur
        act[s] = (acc * _sigmoid(acc) * sc_ref[...]).astype(BF16)
        gates[s] = g_ref[s]
        log_f = -(jnp.maximum(-gates[s], 0.0) + jnp.log1p(jnp.exp(-jnp.abs(gates[s]))))
        hi, mid, lo = _split3(log_f)
        bcum[s] = _dot(tri, hi) + _dot(tri, mid) + _dot(tri, lo)
        bcum_t[s] = bcum[s].T
        gates_t[s] = gates[s].T

    qh, kh, vh, c_old, m_prev, b_c, i_c, log_d, m_inter, m = ({} for _ in range(10))
    for p in pairs:
        s, h = p
        qh[p] = act[s][:, head_cols(h)]
        kh[p] = act[s][:, ML_PAD_WIDTH + h * ML_HEAD_PAD:ML_PAD_WIDTH + (h + 1) * ML_HEAD_PAD]
        vh[p] = v_ref[s, :, head_cols(h)]
        c_old[p] = ct[s, h]
        m_prev[p] = m_s[s, h, 0:1, 0:1]
        b_c[p] = bcum[s][:, ML_HEADS + h:ML_HEADS + h + 1]
        b_r = bcum_t[s][ML_HEADS + h:ML_HEADS + h + 1, :]
        i_c[p] = gates[s][:, h:h + 1]
        i_r = gates_t[s][h:h + 1, :]
        log_d[p] = jnp.where(causal, b_c[p] - b_r + i_r, neg_inf)
        m_inter[p] = b_c[p] + m_prev[p]
        m[p] = jnp.maximum(m_inter[p], jnp.max(log_d[p], axis=-1, keepdims=True))

    scores = {p: _dot_nt(qh[p], kh[p]) for p in pairs}
    carried = {p: _dot(qh[p], c_old[p].astype(BF16)) for p in pairs}
    sc = {p: (scores[p] * jnp.exp(log_d[p] - m[p])).astype(BF16) for p in pairs}
    num = {p: _dot(sc[p], vh[p]) + jnp.exp(m_inter[p] - m[p]) * carried[p] for p in pairs}
    for p in pairs:
        s, h = p
        den = num[p][:, ML_HEAD_DIM:ML_HEAD_DIM + 1]
        hid = num[p] / jnp.maximum(jnp.abs(den), jnp.exp(-m[p]))
        out_ref[s, :, head_cols(h)] = (og_ref[s, :, head_cols(h)].astype(F32) * hid).astype(out_ref.dtype)
    for p in pairs:
        s, h = p
        b_last = b_c[p][ML_CHUNK - 1:ML_CHUNK, :]
        m_new = m[p][ML_CHUNK - 1:ML_CHUNK, :]
        w = jnp.exp(b_last - b_c[p] + i_c[p] - m_new)
        decay = jnp.exp(b_last + m_prev[p] - m_new)
        wv = (w * vh[p].astype(F32)).astype(BF16)
        ct[s, h] = decay * c_old[p] + _dot_tn(kh[p], wv)
        m_s[s, h] = jnp.broadcast_to(m_new, m_s.shape[2:])


def _mlstm(ml, og, gates, conv_w, conv_b, qk_scale, batch, seq):
    nc = seq // ML_CHUNK
    group = ML_SEQS_PER_STEP if batch % ML_SEQS_PER_STEP == 0 else 1

    def whole(a):
        return pl.BlockSpec(a.shape, lambda b, c: (0, 0))

    def chunk(width, col):
        return pl.BlockSpec((group, ML_CHUNK, width), lambda b, c: (b, c, col))

    out = pl.pallas_call(
        _mlstm_kernel,
        grid=(batch // group, nc),
        in_specs=[chunk(2 * ML_PAD_WIDTH, 0), chunk(ML_PAD_WIDTH, 2), chunk(ML_PAD_WIDTH, 0), chunk(LANES, 0),
                  whole(conv_w), whole(conv_b), whole(qk_scale)],
        out_specs=chunk(ML_PAD_WIDTH, 0),
        out_shape=jax.ShapeDtypeStruct((batch, seq, ML_PAD_WIDTH), BF16),
        scratch_shapes=[pltpu.VMEM((group, 2 * ML_CHUNK, 2 * ML_PAD_WIDTH), BF16),
                        pltpu.VMEM((group, ML_HEADS, ML_HEAD_PAD, ML_HEAD_PAD), F32),
                        pltpu.VMEM((group, ML_HEADS, 8, LANES), F32)],
        compiler_params=_cparams("parallel", "arbitrary"),
        name="mlstm",
    )(ml.reshape(batch, seq, -1), ml.reshape(batch, seq, -1), og.reshape(batch, seq, -1),
      gates.reshape(batch, seq, -1), conv_w, conv_b, qk_scale)
    return out.reshape(batch * seq, ML_PAD_WIDTH)


def _mixer_out_kernel(x_ref, o0_ref, l0_ref, o1_ref, l1_ref, o2_ref, l2_ref, sg_ref, hc_ref, gp_ref,
                      sgg_ref, sgb_ref, sgw_ref, sgbias_ref, wa_ref, wb_ref, wc_ref, wo_ref, lng_ref, lnb_ref,
                      xf_ref, xb_ref, unperm, *, alpha):
    tm = x_ref.shape[0]
    slabs = ATT_GROUP_WIDTH // LANES
    att_refs = (l0_ref, l1_ref, l2_ref, o0_ref, o1_ref, o2_ref)

    for slot, ref in enumerate(att_refs):
        dil = ref.shape[0]
        if dil > 1:
            for r in range(dil):
                for c in range(slabs):
                    unperm[slabs * slot + c, pl.ds(r, tm // dil, stride=dil), :] = (
                        ref[r, :, c * LANES:(c + 1) * LANES].astype(F32))

    def token_order(slot, rows):
        ref = att_refs[slot]
        if ref.shape[0] == 1:
            return ref[0, rows, :].astype(F32)
        return jnp.concatenate([unperm[slabs * slot + c, rows, :] for c in range(slabs)], axis=1)

    row = lax.broadcasted_iota(jnp.int32, (SG_CHUNK, SG_CHUNK), 0)
    colm = lax.broadcasted_iota(jnp.int32, (SG_CHUNK, SG_CHUNK), 1)
    causal = row >= colm
    group_dim = SG_WIDTH // SG_GROUPS
    sg_w = [jnp.where(causal, sgw_ref[g], 0.0).astype(BF16) for g in range(SG_GROUPS)]

    for h in range(tm // MIXER_SUB_ROWS):
        rows = slice(h * MIXER_SUB_ROWS, (h + 1) * MIXER_SUB_ROWS)

        l0, l1, l2 = token_order(0, rows), token_order(1, rows), token_order(2, rows)
        lmax = jnp.maximum(jnp.maximum(l0, l1), l2)
        e0, e1, e2 = jnp.exp(l0 - lmax), jnp.exp(l1 - lmax), jnp.exp(l2 - lmax)
        inv = 1.0 / (e0 + e1 + e2)
        y_a = ((e0 * inv) * token_order(3, rows) + (e1 * inv) * token_order(4, rows)
               + (e2 * inv) * token_order(5, rows))

        u = sg_ref[rows, 0:SG_WIDTH].astype(F32)
        v = sg_ref[rows, SG_WIDTH:2 * SG_WIDTH].astype(F32)
        v = _layer_norm(v, sgg_ref[...], sgb_ref[...]).astype(BF16)
        chunks = []
        for c in range(MIXER_SUB_ROWS // SG_CHUNK):
            crows = slice(c * SG_CHUNK, (c + 1) * SG_CHUNK)
            parts = [_dot(sg_w[g], v[crows, g * group_dim:(g + 1) * group_dim]) + sgbias_ref[:, g:g + 1]
                     for g in range(SG_GROUPS)]
            chunks.append(jnp.concatenate(parts, axis=1))
        y_b = u * jnp.concatenate(chunks, axis=0)

        def gate(i):
            return gp_ref[rows, i * D_MODEL:(i + 1) * D_MODEL].astype(F32)

        z = (gate(0) * _dot(y_a.astype(BF16), wa_ref[...])
             + gate(1) * _dot(y_b.astype(BF16), wb_ref[...])
             + gate(2) * _dot(hc_ref[rows, :], wc_ref[...]))
        mix = _dot(z.astype(BF16), wo_ref[...])
        out = _layer_norm(alpha * x_ref[rows, :] + mix, lng_ref[...], lnb_ref[...])
        xf_ref[rows, :] = out
        xb_ref[rows, :] = out.astype(BF16)


def _mixer_out(x, att_outs, sg, hc, gpre, p, alpha, seq, tm=512):
    n = x.shape[0]
    tiles_per_seq = seq // tm

    def rows(width):
        return pl.BlockSpec((tm, width), lambda i: (i, 0))

    def whole(a):
        return pl.BlockSpec(a.shape, lambda i: (0,) * a.ndim)

    def residue_major(dil):
        return pl.BlockSpec((None, dil, tm // dil, ATT_GROUP_WIDTH),
                            lambda i: (i // tiles_per_seq, 0, i % tiles_per_seq, 0))

    att_specs, att_args = [], []
    for (o, lse), (_, dil) in zip(att_outs, ATT_GROUPS):
        att_specs += [residue_major(dil)] * 2
        att_args += [o, lse]
    weights = [p["sg_ln_g"], p["sg_ln_b"], p["sg_w"], p["sg_bias_t"], p["w_br_a"], p["w_br_b"], p["w_br_c"],
               p["w_out"], p["ln_g0"], p["ln_b0"]]
    return pl.pallas_call(
        functools.partial(_mixer_out_kernel, alpha=alpha),
        grid=(n // tm,),
        in_specs=[rows(D_MODEL)] + att_specs
                 + [rows(2 * SG_WIDTH), rows(ML_PAD_WIDTH), rows(N_BRANCH * D_MODEL)]
                 + [whole(w) for w in weights],
        out_specs=[rows(D_MODEL), rows(D_MODEL)],
        out_shape=[jax.ShapeDtypeStruct((n, D_MODEL), F32), jax.ShapeDtypeStruct((n, D_MODEL), BF16)],
        scratch_shapes=[pltpu.VMEM((2 * len(ATT_GROUPS) * ATT_GROUP_WIDTH // LANES, tm, LANES), F32)],
        compiler_params=_cparams("parallel"),
        name="mixer_out",
    )(x, *att_args, sg, hc, gpre, *weights)


def _swiglu_tile(x, w1_ref, w3_ref, w2_ref, prod, tf):
    for c in range(D_FF // tf):
        cols = slice(c * tf, (c + 1) * tf)
        h1 = _dot(x, w1_ref[:, cols])
        h3 = _dot(x, w3_ref[:, cols])
        prod[:, cols] = (h1 * _sigmoid(h1) * h3).astype(BF16)
    return _dot(prod[...], w2_ref[...])


def _ffn_kernel(xf_ref, xb_ref, w1_ref, w3_ref, w2_ref, lng_ref, lnb_ref, of_ref, ob_ref, prod, *, alpha, tf):
    f = _swiglu_tile(xb_ref[...], w1_ref, w3_ref, w2_ref, prod, tf)
    out = _layer_norm(alpha * xf_ref[...] + f, lng_ref[...], lnb_ref[...])
    of_ref[...] = out
    ob_ref[...] = out.astype(BF16)


def _ffn(xf, xb, w1, w3, w2, ln_g, ln_b, alpha, tm=512, tf=256):
    n = xf.shape[0]

    def resident(shape):
        return pl.BlockSpec(shape, lambda i: (0, 0), pipeline_mode=pl.Buffered(1))

    return pl.pallas_call(
        functools.partial(_ffn_kernel, alpha=alpha, tf=tf),
        grid=(n // tm,),
        in_specs=[pl.BlockSpec((tm, D_MODEL), lambda i: (i, 0)),
                  pl.BlockSpec((tm, D_MODEL), lambda i: (i, 0)),
                  resident((D_MODEL, D_FF)), resident((D_MODEL, D_FF)), resident((D_FF, D_MODEL)),
                  pl.BlockSpec((1, D_MODEL), lambda i: (0, 0)),
                  pl.BlockSpec((1, D_MODEL), lambda i: (0, 0))],
        out_specs=[pl.BlockSpec((tm, D_MODEL), lambda i: (i, 0))] * 2,
        out_shape=[jax.ShapeDtypeStruct((n, D_MODEL), F32), jax.ShapeDtypeStruct((n, D_MODEL), BF16)],
        scratch_shapes=[pltpu.VMEM((tm, D_FF), BF16)],
        compiler_params=_cparams("parallel"),
        name="ffn",
    )(xf, xb, w1, w3, w2, ln_g, ln_b)


def _router_kernel(x_ref, w_ref, b_ref, info_ref):
    xh, xm, _ = _split3(x_ref[...])
    wh, wm, _ = _split3(w_ref[...])
    logits = (_dot(xh, wh) + (_dot(xh, wm) + _dot(xm, wh))) + b_ref[...]
    lane = lax.broadcasted_iota(jnp.int32, logits.shape, 1)
    neg_inf = jnp.float32(-jnp.inf)
    logits = jnp.where(lane < N_EXPERTS, logits, neg_inf)
    v1 = jnp.max(logits, axis=-1, keepdims=True)
    i1 = jnp.min(jnp.where(logits == v1, lane, LANES), axis=-1, keepdims=True)
    rest = jnp.where(lane == i1, neg_inf, logits)
    v2 = jnp.max(rest, axis=-1, keepdims=True)
    i2 = jnp.min(jnp.where(rest == v2, lane, LANES), axis=-1, keepdims=True)
    e = jnp.exp(v2 - v1)
    inv = 1.0 / (1.0 + e)
    info_ref[...] = jnp.where(lane == 0, i1.astype(F32),
                              jnp.where(lane == 1, i2.astype(F32),
                                        jnp.where(lane == 2, inv, jnp.where(lane == 3, e * inv, 0.0))))


def _router(xf, router_w, router_b, tm=1024):
    n = xf.shape[0]
    w = jnp.pad(router_w, ((0, 0), (0, LANES - N_EXPERTS)))
    b = jnp.pad(router_b, (0, LANES - N_EXPERTS))[None, :]
    return pl.pallas_call(
        _router_kernel,
        grid=(n // tm,),
        in_specs=[pl.BlockSpec((tm, D_MODEL), lambda i: (i, 0)),
                  pl.BlockSpec((D_MODEL, LANES), lambda i: (0, 0)),
                  pl.BlockSpec((1, LANES), lambda i: (0, 0))],
        out_specs=pl.BlockSpec((tm, LANES), lambda i: (i, 0)),
        out_shape=jax.ShapeDtypeStruct((n, LANES), F32),
        compiler_params=_cparams("parallel"),
        name="router",
    )(xf, w, b)


def _routing_plan(info, n):
    tile = EXPERT_TILE
    n_tiles = TOP_K * n // tile + N_EXPERTS
    expert = info[:, 0:TOP_K].astype(jnp.int32).reshape(-1)
    onehot = (expert[:, None] == jnp.arange(N_EXPERTS, dtype=jnp.int32)[None, :]).astype(jnp.int32)
    running = jnp.cumsum(onehot, axis=0)
    rank = jnp.sum(onehot * running, axis=1) - 1
    counts = running[-1]
    padded = ((counts + tile - 1) // tile) * tile
    ends = jnp.cumsum(padded)
    starts = ends - padded
    dest = jnp.sum(onehot * starts[None, :], axis=1) + rank
    used = ends[-1] // tile
    tile_start = jnp.arange(n_tiles, dtype=jnp.int32) * tile
    tile_expert = jnp.sum((tile_start[:, None] >= ends[None, :]).astype(jnp.int32), axis=1)
    last_expert = jnp.sum((tile_start[used - 1] >= ends).astype(jnp.int32))
    tile_expert = jnp.minimum(tile_expert, last_expert)
    plan = dict(dest=dest, tile_expert=tile_expert, used=used.reshape(1),
                pad_start=starts + counts, pad_count=padded - counts)
    return {k: v.astype(jnp.int32) for k, v in plan.items()}, n_tiles


def _rows_wait(src, dst, rows, sem):
    pltpu.make_async_copy(src.at[pl.ds(0, rows)], dst.at[pl.ds(0, rows)], sem).wait()


ROW_SUBLANES = D_MODEL // LANES


def _tile_copy(src, src_row, dst, dst_row, sem):
    s0 = pl.multiple_of(src_row * ROW_SUBLANES, ROW_SUBLANES)
    d0 = pl.multiple_of(dst_row * ROW_SUBLANES, ROW_SUBLANES)
    return pltpu.make_async_copy(src.at[pl.ds(s0, ROW_SUBLANES)], dst.at[pl.ds(d0, ROW_SUBLANES)], sem)


def _to_row_tiles(ref, x):
    m = x.shape[0]
    for c in range(ROW_SUBLANES):
        ref[pl.ds(c, m, stride=ROW_SUBLANES), :] = x[:, c * LANES:(c + 1) * LANES]


def _from_row_tiles(ref):
    m = ref.shape[0] // ROW_SUBLANES
    return jnp.concatenate([ref[pl.ds(c, m, stride=ROW_SUBLANES), :] for c in range(ROW_SUBLANES)], axis=1)


DISPATCH_SLOTS = 3


def _dispatch_kernel(pad_start_ref, pad_count_ref, used_ref, dest_ref, x_ref, xs_ref, buf, tiles, zero_row,
                     load_sem, row_sem, pad_sem, *, tm):
    i = pl.program_id(0)

    @pl.when(i == 0)
    def _():
        zero_row[...] = jnp.zeros(zero_row.shape, F32)
        for e in range(N_EXPERTS):
            def fill(j, carry, e=e):
                _tile_copy(zero_row, 0, xs_ref, pad_start_ref[e] + j, pad_sem).start()
                return carry
            lax.fori_loop(0, pad_count_ref[e], fill, 0)
        for e in range(N_EXPERTS):
            def drain(j, carry):
                _tile_copy(zero_row, 0, xs_ref, 0, pad_sem).wait()
                return carry
            lax.fori_loop(0, pad_count_ref[e], drain, 0)

        tile_rows = tm * ROW_SUBLANES
        spare = xs_ref.shape[0] // tile_rows - used_ref[0]
        tiles[1] = jnp.zeros(tiles.shape[1:], F32)

        def zero_tile(j):
            row0 = pl.multiple_of((used_ref[0] + j) * tile_rows, tile_rows)
            return pltpu.make_async_copy(tiles.at[1], xs_ref.at[pl.ds(row0, tile_rows)], pad_sem)

        def fill_tile(j, carry):
            zero_tile(j).start()
            return carry

        def drain_tile(j, carry):
            zero_tile(j).wait()
            return carry

        lax.fori_loop(0, spare, fill_tile, 0)
        lax.fori_loop(0, spare, drain_tile, 0)

    steps = pl.num_programs(0)
    slot = i % DISPATCH_SLOTS
    prev_slot = (i + DISPATCH_SLOTS - 1) % DISPATCH_SLOTS
    stage = i % 2

    def load(tile, into):
        return pltpu.make_async_copy(x_ref.at[pl.ds(tile * tm, tm), :], buf.at[into], load_sem.at[into])

    def wait_rows(of_stage):
        for _ in range(TOP_K):
            _rows_wait(tiles.at[of_stage], xs_ref, tm * ROW_SUBLANES, row_sem.at[of_stage])

    @pl.when(i == 0)
    def _():
        load(0, 0).start()

        @pl.when(steps > 1)
        def _():
            load(1, 1).start()

    load(i, slot).wait()
    _to_row_tiles(tiles.at[stage], buf[slot])

    def start(t, carry):
        for c in range(TOP_K):
            _tile_copy(tiles.at[stage], t, xs_ref, dest_ref[0, 0, TOP_K * t + c], row_sem.at[stage]).start(
                priority=c % 2)
        return carry

    lax.fori_loop(0, tm, start, 0, unroll=DMA_ISSUE_UNROLL)

    @pl.when(i > 0)
    def _():
        wait_rows(1 - stage)

    @pl.when(i + 2 < steps)
    def _():
        load(i + 2, prev_slot).start()

    @pl.when(i == steps - 1)
    def _():
        wait_rows(stage)


def _dispatch(xf, dest, pad_start, pad_count, used, n_slots, tm=EXPERT_TILE):
    n = xf.shape[0]
    dest3 = dest.reshape(n // tm, 1, TOP_K * tm)
    grid_spec = pltpu.PrefetchScalarGridSpec(
        num_scalar_prefetch=3,
        grid=(n // tm,),
        in_specs=[pl.BlockSpec((1, 1, TOP_K * tm), lambda i, ps, pc, nu: (i, 0, 0), memory_space=pltpu.SMEM),
                  pl.BlockSpec(memory_space=pl.ANY)],
        out_specs=pl.BlockSpec(memory_space=pl.ANY),
        scratch_shapes=[pltpu.VMEM((DISPATCH_SLOTS, tm, D_MODEL), F32),
                        pltpu.VMEM((2, tm * ROW_SUBLANES, LANES), F32),
                        pltpu.VMEM((ROW_SUBLANES, LANES), F32),
                        pltpu.SemaphoreType.DMA((DISPATCH_SLOTS,)),
                        pltpu.SemaphoreType.DMA((2,)),
                        pltpu.SemaphoreType.DMA],
    )
    return pl.pallas_call(
        functools.partial(_dispatch_kernel, tm=tm),
        grid_spec=grid_spec,
        out_shape=jax.ShapeDtypeStruct((n_slots * ROW_SUBLANES, LANES), F32),
        compiler_params=_cparams("arbitrary"),
        name="moe_dispatch",
    )(pad_start, pad_count, used, dest3, xf)


def _experts_kernel(te_ref, used_ref, xs_ref, w1_ref, w3_ref, w2_ref, ys_ref, prod, *, tf):
    del te_ref
    i = pl.program_id(0)

    @pl.when(i < used_ref[0])
    def _():
        x = _from_row_tiles(xs_ref).astype(BF16)
        _to_row_tiles(ys_ref, _swiglu_tile(x, w1_ref, w3_ref, w2_ref, prod, tf))

    @pl.when(i >= used_ref[0])
    def _():
        ys_ref[...] = jnp.zeros(ys_ref.shape, ys_ref.dtype)


def _experts(xs, tile_expert, used, w1, w3, w2, n_tiles, tf=256):
    tile = EXPERT_TILE
    grid_spec = pltpu.PrefetchScalarGridSpec(
        num_scalar_prefetch=2,
        grid=(n_tiles,),
        in_specs=[pl.BlockSpec((tile * ROW_SUBLANES, LANES), lambda i, te, nu: (jnp.minimum(i, nu[0] - 1), 0)),
                  pl.BlockSpec((None, D_MODEL, D_FF), lambda i, te, nu: (te[i], 0, 0)),
                  pl.BlockSpec((None, D_MODEL, D_FF), lambda i, te, nu: (te[i], 0, 0)),
                  pl.BlockSpec((None, D_FF, D_MODEL), lambda i, te, nu: (te[i], 0, 0))],
        out_specs=pl.BlockSpec((tile * ROW_SUBLANES, LANES), lambda i, te, nu: (i, 0)),
        scratch_shapes=[pltpu.VMEM((tile, D_FF), BF16)],
    )
    return pl.pallas_call(
        functools.partial(_experts_kernel, tf=tf),
        grid_spec=grid_spec,
        out_shape=jax.ShapeDtypeStruct((n_tiles * tile * ROW_SUBLANES, LANES), F32),
        compiler_params=_cparams("arbitrary"),
        name="moe_experts",
    )(tile_expert, used, xs, w1, w3, w2)


def _combine_kernel(dest_ref, next_dest_ref, xf_ref, info_ref, ys_ref, lng_ref, lnb_ref, of_ref, got, sem, *, alpha):
    tm = xf_ref.shape[0]
    i = pl.program_id(0)
    slot = i % 2

    def gather(dests, into):
        def start(t, carry):
            for c in range(TOP_K):
                _tile_copy(ys_ref, dests[0, 0, TOP_K * t + c], got.at[into, c], t, sem.at[into]).start(
                    priority=c % 2)
            return carry
        lax.fori_loop(0, tm, start, 0, unroll=DMA_ISSUE_UNROLL)

    @pl.when(i == 0)
    def _():
        gather(dest_ref, 0)

    @pl.when(i + 1 < pl.num_programs(0))
    def _():
        gather(next_dest_ref, 1 - slot)

    for c in range(TOP_K):
        _rows_wait(ys_ref, got.at[slot, c], tm * ROW_SUBLANES, sem.at[slot])
    info = info_ref[...]
    mix = (info[:, 2:3] * _from_row_tiles(got.at[slot, 0]) + info[:, 3:4] * _from_row_tiles(got.at[slot, 1]))
    of_ref[...] = _layer_norm(alpha * xf_ref[...] + mix, lng_ref[...], lnb_ref[...])


def _combine(xf, info, ys, dest, ln_g, ln_b, alpha, tm=256):
    n = xf.shape[0]
    steps = n // tm
    dest3 = dest.reshape(steps, 1, TOP_K * tm)
    return pl.pallas_call(
        functools.partial(_combine_kernel, alpha=alpha),
        grid=(steps,),
        in_specs=[pl.BlockSpec((1, 1, TOP_K * tm), lambda i: (i, 0, 0), memory_space=pltpu.SMEM),
                  pl.BlockSpec((1, 1, TOP_K * tm), lambda i: (jnp.minimum(i + 1, steps - 1), 0, 0),
                               memory_space=pltpu.SMEM),
                  pl.BlockSpec((tm, D_MODEL), lambda i: (i, 0)),
                  pl.BlockSpec((tm, LANES), lambda i: (i, 0)),
                  pl.BlockSpec(memory_space=pl.ANY),
                  pl.BlockSpec((1, D_MODEL), lambda i: (0, 0)),
                  pl.BlockSpec((1, D_MODEL), lambda i: (0, 0))],
        out_specs=pl.BlockSpec((tm, D_MODEL), lambda i: (i, 0)),
        out_shape=jax.ShapeDtypeStruct((n, D_MODEL), F32),
        scratch_shapes=[pltpu.VMEM((2, TOP_K, tm * ROW_SUBLANES, LANES), F32), pltpu.SemaphoreType.DMA((2,))],
        compiler_params=_cparams("arbitrary"),
        name="moe_combine",
    )(dest3, dest3, xf, info, ys, ln_g, ln_b)


def _moe(xf, router_w, router_b, w1, w3, w2, ln_g, ln_b, alpha):
    n = xf.shape[0]
    info = _router(xf, router_w, router_b)
    plan, n_tiles = _routing_plan(info, n)
    xs = _dispatch(xf, plan["dest"], plan["pad_start"], plan["pad_count"], plan["used"], n_tiles * EXPERT_TILE)
    ys = _experts(xs, plan["tile_expert"], plan["used"], w1, w3, w2, n_tiles)
    return _combine(xf, info, ys, plan["dest"], ln_g, ln_b, alpha)


def _pad_heads(w):
    lead = w.shape[:-1]
    w = w.reshape(*lead, ML_HEADS, ML_HEAD_DIM)
    w = jnp.pad(w, [(0, 0)] * len(lead) + [(0, 0), (0, ML_HEAD_PAD - ML_HEAD_DIM)])
    return w.reshape(*lead, ML_PAD_WIDTH)


def _group_qkv(w, group):
    gw = ATT_GROUP_WIDTH
    return jnp.concatenate([w[..., part * ATT_WIDTH + group * gw:part * ATT_WIDTH + (group + 1) * gw]
                            for part in range(3)], axis=-1)


def _layer_params(layer, w_in, b_in, conv_w, conv_b, sg_ln_g, sg_ln_b, sg_w, sg_b,
                  w_br_a, w_br_b, w_br_c, w_out, ln_g, ln_b):
    w, b = w_in[layer].astype(BF16), b_in[layer]
    o_att, o_sg, o_qk, o_v, o_o, o_i, o_g = 0, 2304, 3840, 5376, 6144, 6912, 6920
    ones_col = jnp.zeros((ML_HEADS, ML_HEAD_PAD), F32).at[:, ML_HEAD_DIM].set(1.0).reshape(ML_PAD_WIDTH)
    w_qk = jnp.concatenate([_pad_heads(w[:, o_qk:o_qk + ML_WIDTH]), _pad_heads(w[:, o_qk + ML_WIDTH:o_v])], axis=1)
    b_qk = jnp.concatenate([_pad_heads(b[o_qk:o_qk + ML_WIDTH]), _pad_heads(b[o_qk + ML_WIDTH:o_v])])
    qk_scale = jnp.concatenate([jnp.ones((ML_PAD_WIDTH,), F32),
                                jnp.full((ML_PAD_WIDTH,), ML_HEAD_DIM ** -0.5, F32)])
    cw, cb = conv_w[layer], conv_b[layer]
    w_att, b_att = w[:, o_att:o_sg], b[o_att:o_sg]
    return dict(
        w_att=[_group_qkv(w_att, g) for g in range(len(ATT_GROUPS))],
        b_att=[_group_qkv(b_att, g)[None, :] for g in range(len(ATT_GROUPS))],
        w_sg=w[:, o_sg:o_qk], b_sg=b[None, o_sg:o_qk],
        w_ml=jnp.concatenate([w_qk, _pad_heads(w[:, o_v:o_o])], axis=1),
        b_ml=jnp.concatenate([b_qk, _pad_heads(b[o_v:o_o]) + ones_col])[None, :],
        qk_scale=qk_scale[None, :],
        w_mlo=_pad_heads(w[:, o_o:o_i]), b_mlo=_pad_heads(b[o_o:o_i])[None, :],
        w_if=jnp.pad(w[:, o_i:o_g], ((0, 0), (0, LANES - 2 * ML_HEADS))),
        b_if=jnp.pad(b[o_i:o_g], (0, LANES - 2 * ML_HEADS))[None, :],
        w_g=w[:, o_g:], b_g=b[None, o_g:],
        conv_w=jnp.concatenate([_pad_heads(cw[:, :ML_WIDTH]), _pad_heads(cw[:, ML_WIDTH:])], axis=1),
        conv_b=jnp.concatenate([_pad_heads(cb[:ML_WIDTH]), _pad_heads(cb[ML_WIDTH:])])[None, :],
        sg_ln_g=sg_ln_g[layer][None, :], sg_ln_b=sg_ln_b[layer][None, :],
        sg_w=sg_w[layer], sg_bias_t=sg_b[layer].T,
        w_br_a=w_br_a[layer].astype(BF16), w_br_b=w_br_b[layer].astype(BF16),
        w_br_c=_pad_heads(w_br_c[layer].T).T.astype(BF16),
        w_out=w_out[layer].astype(BF16),
        ln_g0=ln_g[layer, 0][None, :], ln_b0=ln_b[layer, 0][None, :],
        ln_g1=ln_g[layer, 1][None, :], ln_b1=ln_b[layer, 1][None, :],
    )


def kernel(x, positions, w_in, b_in, conv_w, conv_b, sg_ln_g, sg_ln_b, sg_w, sg_b, w_br_a, w_br_b, w_br_c, w_out,
           ln_g, ln_b, ffn_w1, ffn_w3, ffn_w2, router_w, router_b, moe_w1, moe_w3, moe_w2):
    batch, seq, _ = x.shape
    n = batch * seq
    depth = w_in.shape[0]
    alpha = (2.0 * depth) ** 0.25
    cos, sin = _rope_tables(positions)
    xf = x.reshape(n, D_MODEL)
    xb = xf.astype(BF16)
    for layer in range(depth):
        p = _layer_params(layer, w_in, b_in, conv_w, conv_b, sg_ln_g, sg_ln_b, sg_w, sg_b,
                          w_br_a, w_br_b, w_br_c, w_out, ln_g, ln_b)
        sg = _linear(xb, p["w_sg"], p["b_sg"], BF16, 1024, 2 * SG_WIDTH, act=_gelu_tanh)
        ml = _linear(xb, p["w_ml"], p["b_ml"], BF16, 1024, 3 * ML_PAD_WIDTH)
        og = _linear(xb, p["w_mlo"], p["b_mlo"], BF16, 2048, ML_PAD_WIDTH, act=_sigmoid)
        gates_if = _linear(xb, p["w_if"], p["b_if"], F32, 2048, LANES)
        gbr = _linear(xb, p["w_g"], p["b_g"], BF16, 1024, N_BRANCH * D_MODEL, act=_sigmoid)
        att_outs = []
        for g, (_, dil) in enumerate(ATT_GROUPS):
            qkv = _linear_dilated(xb, p["w_att"][g], p["b_att"][g], dil, batch, seq)
            att_outs.append(_attention_group(qkv, cos, sin, g, batch, seq))
        hc = _mlstm(ml, og, gates_if, p["conv_w"], p["conv_b"], p["qk_scale"], batch, seq)
        xf, xb = _mixer_out(xf, att_outs, sg, hc, gbr, p, alpha, seq)
        j = layer // 2
        if layer % 2 == 0:
            xf, xb = _ffn(xf, xb, ffn_w1[j].astype(BF16), ffn_w3[j].astype(BF16), ffn_w2[j].astype(BF16),
                          p["ln_g1"], p["ln_b1"], alpha)
        else:
            xf = _moe(xf, router_w[j], router_b[j], moe_w1[j].astype(BF16), moe_w3[j].astype(BF16),
                      moe_w2[j].astype(BF16), p["ln_g1"], p["ln_b1"], alpha)
            xb = xf.astype(BF16)
    return xf.reshape(batch, seq, D_MODEL)
```

```python
import functools

import numpy as np
import jax
import jax.numpy as jnp
from jax import lax
from jax.experimental import pallas as pl
from jax.experimental.pallas import tpu as pltpu

F32 = jnp.float32
BF16 = jnp.bfloat16

D_MODEL = 1024
ATT_GROUPS = ((128, 1), (512, 4), (2048, 16))
ATT_HEADS_PER_GROUP = 4
ATT_HEAD_DIM = 64
ATT_WIDTH = 768
ATT_GROUP_WIDTH = ATT_HEADS_PER_GROUP * ATT_HEAD_DIM
ATT_PARTS_WIDTH = 3 * ATT_GROUP_WIDTH
ROPE_THETA = 10000.0
SG_CHUNK = 128
SG_GROUPS = 6
SG_WIDTH = 768
ML_HEADS = 4
ML_HEAD_DIM = 192
ML_HEAD_PAD = 256
ML_WIDTH = 768
ML_PAD_WIDTH = ML_HEADS * ML_HEAD_PAD
ML_CHUNK = 128
ML_CONV = 4
N_BRANCH = 3
D_FF = 2816
N_EXPERTS = 8
TOP_K = 2
LN_EPS = 1e-5
BLOCK = 128
LANES = 128
VMEM_LIMIT = 56 * 1024 * 1024
EXPERT_TILE = 512
DMA_ISSUE_UNROLL = 8
MIXER_SUB_ROWS = 256


def _cparams(*sem):
    return pltpu.CompilerParams(dimension_semantics=sem, vmem_limit_bytes=VMEM_LIMIT)


def _layer_norm(x, g, b):
    mu = jnp.mean(x, axis=-1, keepdims=True)
    xc = x - mu
    var = jnp.mean(xc * xc, axis=-1, keepdims=True)
    return xc * lax.rsqrt(var + LN_EPS) * g + b


def _gelu_tanh(x):
    return 0.5 * x * (1.0 + jnp.tanh(np.sqrt(2.0 / np.pi) * (x + 0.044715 * (x * x * x))))


def _sigmoid(x):
    return 0.5 * (1.0 + jnp.tanh(0.5 * x))


def _dot(a, b):
    return jnp.dot(a, b, preferred_element_type=F32)


def _dot_nt(a, b):
    return lax.dot_general(a, b, (((1,), (1,)), ((), ())), preferred_element_type=F32)


def _dot_tn(a, b):
    return lax.dot_general(a, b, (((0,), (0,)), ((), ())), preferred_element_type=F32)


def _split3(x):
    hi = x.astype(BF16)
    r1 = x - hi.astype(F32)
    mid = r1.astype(BF16)
    lo = (r1 - mid.astype(F32)).astype(BF16)
    return hi, mid, lo


def _projections_kernel(x_ref, *refs, acts):
    n = len(acts)
    x = x_ref[...]
    for w_ref, b_ref, o_ref, act in zip(refs[:n], refs[n:2 * n], refs[2 * n:], acts):
        res = _dot(x, w_ref[...]) + b_ref[...]
        o_ref[...] = (res if act is None else act(res)).astype(o_ref.dtype)


def _projections(x, heads, tm=512):
    m, k = x.shape

    def resident(a):
        return pl.BlockSpec(a.shape, lambda i: (0, 0), pipeline_mode=pl.Buffered(1))

    ws = [h[0] for h in heads]
    bs = [h[1] for h in heads]
    return pl.pallas_call(
        functools.partial(_projections_kernel, acts=tuple(h[3] for h in heads)),
        grid=(m // tm,),
        in_specs=[pl.BlockSpec((tm, k), lambda i: (i, 0))] + [resident(w) for w in ws] + [resident(b) for b in bs],
        out_specs=[pl.BlockSpec((tm, w.shape[1]), lambda i: (i, 0)) for w in ws],
        out_shape=[jax.ShapeDtypeStruct((m, h[0].shape[1]), h[2]) for h in heads],
        compiler_params=_cparams("parallel"),
        name="projections",
    )(x, *ws, *bs)


def _linear_dilated_kernel(x_ref, w_ref, b_ref, o_ref, acc, *, dil):
    res = _dot(x_ref[...], w_ref[...]) + b_ref[...]
    slabs, rows, _ = acc.shape
    for c in range(slabs):
        acc[c] = res[:, c * LANES:(c + 1) * LANES]
    sub = rows // dil
    for r in range(dil):
        for c in range(slabs):
            o_ref[r, :, c * LANES:(c + 1) * LANES] = acc[c, pl.ds(r, sub, stride=dil), :].astype(o_ref.dtype)


def _linear_dilated(x, w, b, dil, batch, seq):
    k = x.shape[1]
    n = w.shape[1]
    tn = ATT_GROUP_WIDTH
    sub = seq // dil
    return pl.pallas_call(
        functools.partial(_linear_dilated_kernel, dil=dil),
        grid=(batch, n // tn),
        in_specs=[pl.BlockSpec((seq, k), lambda i, j: (i, 0)),
                  pl.BlockSpec((k, tn), lambda i, j: (0, j)),
                  pl.BlockSpec((1, tn), lambda i, j: (0, j))],
        out_specs=pl.BlockSpec((None, dil, sub, tn), lambda i, j: (i, 0, 0, j)),
        out_shape=jax.ShapeDtypeStruct((batch, dil, sub, n), BF16),
        scratch_shapes=[pltpu.VMEM((tn // LANES, seq, LANES), F32)],
        compiler_params=_cparams("parallel", "parallel"),
        name=f"linear_dil{dil}",
    )(x, w, b)


def _rope_table_kernel(pos_ref, freq_ref, sign_ref, cos_ref, sin_ref):
    ang = pos_ref[...] * freq_ref[...]
    cos_ref[...] = jnp.cos(ang)
    sin_ref[...] = jnp.sin(ang) * sign_ref[...]


def _rope_tables(positions):
    n = positions.size
    half = ATT_HEAD_DIM // 2
    freqs = ROPE_THETA ** (-jnp.arange(half, dtype=F32) * (2.0 / ATT_HEAD_DIM))
    lane = np.arange(LANES)
    freq_row = freqs[lane % half][None, :]
    sign_row = jnp.asarray(np.where(lane % ATT_HEAD_DIM < half, -1.0, 1.0), F32)[None, :]
    pos = positions.astype(F32).reshape(n, 1)
    tm = 2048
    return pl.pallas_call(
        _rope_table_kernel,
        grid=(n // tm,),
        in_specs=[pl.BlockSpec((tm, 1), lambda i: (i, 0)),
                  pl.BlockSpec((1, LANES), lambda i: (0, 0)),
                  pl.BlockSpec((1, LANES), lambda i: (0, 0))],
        out_specs=[pl.BlockSpec((tm, LANES), lambda i: (i, 0))] * 2,
        out_shape=[jax.ShapeDtypeStruct((n, LANES), F32)] * 2,
        compiler_params=_cparams("parallel"),
        name="rope_tables",
    )(pos, freq_row, sign_row)


def _attn_kernel(q_ref, k_ref, v_ref, cos_ref, sin_ref, o_ref, lse_ref, qs, ks, vs, *, sub, dil):
    nb = sub // BLOCK
    res = pl.program_id(1)
    lane = lax.broadcasted_iota(jnp.int32, (BLOCK, LANES), 1)
    first_half = (lane % ATT_HEAD_DIM) < (ATT_HEAD_DIM // 2)
    low_head = lane < ATT_HEAD_DIM

    def rope(x, c, s):
        partner = jnp.where(first_half, pltpu.roll(x, LANES - 32, 1), pltpu.roll(x, 32, 1))
        return x * c + partner * s

    ks[0:BLOCK, :] = jnp.zeros((BLOCK, ATT_GROUP_WIDTH), BF16)
    vs[0:BLOCK, :] = jnp.zeros((BLOCK, ATT_GROUP_WIDTH), BF16)

    def prep(n, carry):
        r0 = pl.multiple_of(n * BLOCK, BLOCK)
        if dil == 1:
            c = cos_ref[pl.ds(r0, BLOCK), :]
            s = sin_ref[pl.ds(r0, BLOCK), :]
        else:
            c = cos_ref[pl.ds(r0 * dil + res, BLOCK, stride=dil), :]
            s = sin_ref[pl.ds(r0 * dil + res, BLOCK, stride=dil), :]
        for pair in range(2):
            cols = slice(pair * LANES, (pair + 1) * LANES)
            q = rope(q_ref[pl.ds(r0, BLOCK), cols].astype(F32), c, s) * (ATT_HEAD_DIM ** -0.5)
            qs[pl.ds(r0, BLOCK), (2 * pair) * LANES:(2 * pair + 1) * LANES] = jnp.where(low_head, q, 0.0).astype(BF16)
            qs[pl.ds(r0, BLOCK), (2 * pair + 1) * LANES:(2 * pair + 2) * LANES] = jnp.where(low_head, 0.0, q).astype(BF16)
            k = rope(k_ref[pl.ds(r0, BLOCK), cols].astype(F32), c, s)
            ks[pl.ds(r0 + BLOCK, BLOCK), cols] = k.astype(BF16)
        vs[pl.ds(r0 + BLOCK, BLOCK), :] = v_ref[pl.ds(r0, BLOCK), :]
        return carry

    lax.fori_loop(0, nb, prep, 0)

    qi = lax.broadcasted_iota(jnp.int32, (BLOCK, 2 * BLOCK), 0)
    ki = lax.broadcasted_iota(jnp.int32, (BLOCK, 2 * BLOCK), 1)
    dist = qi + BLOCK - ki
    band = (dist >= 0) & (dist <= BLOCK)
    neg_inf = jnp.float32(-jnp.inf)
    bias_rest = jnp.where(band, 0.0, neg_inf)
    bias_first = jnp.where(band & (ki >= BLOCK), 0.0, neg_inf)

    def block(n, bias):
        r0 = pl.multiple_of(n * BLOCK, BLOCK)
        heads = range(ATT_HEADS_PER_GROUP)
        pair_cols = [slice(pair * LANES, (pair + 1) * LANES) for pair in range(2)]
        kw = [ks[pl.ds(r0, 2 * BLOCK), c] for c in pair_cols]
        vw = [vs[pl.ds(r0, 2 * BLOCK), c] for c in pair_cols]
        s = [_dot_nt(qs[pl.ds(r0, BLOCK), hd * LANES:(hd + 1) * LANES], kw[hd // 2]) + bias for hd in heads]
        m = [jnp.max(s[hd], axis=-1, keepdims=True) for hd in heads]
        p = [jnp.exp(s[hd] - m[hd]) for hd in heads]
        den = [jnp.sum(p[hd], axis=-1, keepdims=True) for hd in heads]
        o = [_dot(p[hd].astype(BF16), vw[hd // 2]) * (1.0 / den[hd]) for hd in heads]
        lse = [m[hd] + jnp.log(den[hd]) for hd in heads]
        for pair in range(2):
            lo, hi = 2 * pair, 2 * pair + 1
            o_ref[pl.ds(r0, BLOCK), pair_cols[pair]] = jnp.where(low_head, o[lo], o[hi]).astype(o_ref.dtype)
            lse_ref[pl.ds(r0, BLOCK), pair_cols[pair]] = jnp.where(low_head, lse[lo], lse[hi])

    block(0, bias_first)
    if nb <= 2:
        for n in range(1, nb):
            block(n, bias_rest)
        return

    def body(n, carry):
        block(n, bias_rest)
        return carry

    lax.fori_loop(1, nb, body, 0, unroll=2)


def _attention_group(qkv, cos, sin, group, batch, seq):
    window, dil = ATT_GROUPS[group]
    assert window // dil == BLOCK
    sub = seq // dil
    assert sub % BLOCK == 0

    def part(p):
        return pl.BlockSpec((None, None, sub, ATT_GROUP_WIDTH), lambda b, r: (b, r, 0, p))

    tab = pl.BlockSpec((seq, LANES), lambda b, r: (b, 0))
    out = pl.BlockSpec((None, None, sub, ATT_GROUP_WIDTH), lambda b, r: (b, r, 0, 0))
    return pl.pallas_call(
        functools.partial(_attn_kernel, sub=sub, dil=dil),
        grid=(batch, dil),
        in_specs=[part(0), part(1), part(2), tab, tab],
        out_specs=[out, out],
        out_shape=[jax.ShapeDtypeStruct((batch, dil, sub, ATT_GROUP_WIDTH), BF16),
                   jax.ShapeDtypeStruct((batch, dil, sub, ATT_GROUP_WIDTH), F32)],
        scratch_shapes=[pltpu.VMEM((sub, ATT_HEADS_PER_GROUP * LANES), BF16),
                        pltpu.VMEM((sub + BLOCK, ATT_GROUP_WIDTH), BF16),
                        pltpu.VMEM((sub + BLOCK, ATT_GROUP_WIDTH), BF16)],
        compiler_params=_cparams("parallel", "parallel"),
        name=f"attention_g{group}",
    )(qkv, qkv, qkv, cos, sin)


ML_SEQS_PER_STEP = 2


def _mlstm_kernel(qk_ref, v_ref, og_ref, g_ref, cw_ref, cb_ref, sc_ref, out_ref, frame, ct, m_s):
    taps = ML_CONV

    @pl.when(pl.program_id(1) == 0)
    def _():
        frame[:, 0:ML_CHUNK, :] = jnp.zeros((frame.shape[0], ML_CHUNK, 2 * ML_PAD_WIDTH), BF16)
        ct[...] = jnp.zeros(ct.shape, F32)
        m_s[...] = jnp.zeros(m_s.shape, F32)

    out_row = lax.broadcasted_iota(jnp.int32, (ML_CHUNK, 2 * ML_CHUNK), 0)
    src_row = lax.broadcasted_iota(jnp.int32, (ML_CHUNK, 2 * ML_CHUNK), 1)
    picks = [jnp.where(src_row == out_row + (ML_CHUNK - (taps - 1 - j)), 1.0, 0.0).astype(BF16)
             for j in range(taps - 1)]
    row = lax.broadcasted_iota(jnp.int32, (ML_CHUNK, ML_CHUNK), 0)
    colm = lax.broadcasted_iota(jnp.int32, (ML_CHUNK, ML_CHUNK), 1)
    causal = row >= colm
    tri = jnp.where(causal, 1.0, 0.0).astype(BF16)
    neg_inf = jnp.float32(-jnp.inf)

    seqs = range(qk_ref.shape[0])
    pairs = [(s, h) for s in seqs for h in range(ML_HEADS)]

    def head_cols(h):
        return slice(h * ML_HEAD_PAD, (h + 1) * ML_HEAD_PAD)

    act, gates, bcum, bcum_t, gates_t = {}, {}, {}, {}, {}
    for s in seqs:
        cur = qk_ref[s]
        frame[s, ML_CHUNK:2 * ML_CHUNK, :] = cur
        both = frame[s]
        acc = cb_ref[...] + cur.astype(F32) * cw_ref[taps - 1:taps, :]
        for j in range(taps - 1):
            acc = acc + _dot(picks[j], both) * cw_ref[j:j + 1, :]
        frame[s, 0:ML_CHUNK, :] = cur
        act[s] = (acc * _sigmoid(acc) * sc_ref[...]).astype(BF16)
        gates[s] = g_ref[s]
        log_f = -(jnp.maximum(-gates[s], 0.0) + jnp.log1p(jnp.exp(-jnp.abs(gates[s]))))
        hi, mid, lo = _split3(log_f)
        bcum[s] = _dot(tri, hi) + _dot(tri, mid) + _dot(tri, lo)
        bcum_t[s] = bcum[s].T
        gates_t[s] = gates[s].T

    qh, kh, vh, c_old, m_prev, b_c, i_c, log_d, m_inter, m = ({} for _ in range(10))
    for p in pairs:
        s, h = p
        qh[p] = act[s][:, head_cols(h)]
        kh[p] = act[s][:, ML_PAD_WIDTH + h * ML_HEAD_PAD:ML_PAD_WIDTH + (h + 1) * ML_HEAD_PAD]
        vh[p] = v_ref[s, :, head_cols(h)]
        c_old[p] = ct[s, h]
        m_prev[p] = m_s[s, h, 0:1, 0:1]
        b_c[p] = bcum[s][:, ML_HEADS + h:ML_HEADS + h + 1]
        b_r = bcum_t[s][ML_HEADS + h:ML_HEADS + h + 1, :]
        i_c[p] = gates[s][:, h:h + 1]
        i_r = gates_t[s][h:h + 1, :]
        log_d[p] = jnp.where(causal, b_c[p] - b_r + i_r, neg_inf)
        m_inter[p] = b_c[p] + m_prev[p]
        m[p] = jnp.maximum(m_inter[p], jnp.max(log_d[p], axis=-1, keepdims=True))

    scores = {p: _dot_nt(qh[p], kh[p]) for p in pairs}
    carried = {p: _dot(qh[p], c_old[p].astype(BF16)) for p in pairs}
    sc = {p: (scores[p] * jnp.exp(log_d[p] - m[p])).astype(BF16) for p in pairs}
    num = {p: _dot(sc[p], vh[p]) + jnp.exp(m_inter[p] - m[p]) * carried[p] for p in pairs}
    for p in pairs:
        s, h = p
        den = num[p][:, ML_HEAD_DIM:ML_HEAD_DIM + 1]
        hid = num[p] / jnp.maximum(jnp.abs(den), jnp.exp(-m[p]))
        out_ref[s, :, head_cols(h)] = (og_ref[s, :, head_cols(h)].astype(F32) * hid).astype(out_ref.dtype)
    for p in pairs:
        s, h = p
        b_last = b_c[p][ML_CHUNK - 1:ML_CHUNK, :]
        m_new = m[p][ML_CHUNK - 1:ML_CHUNK, :]
        w = jnp.exp(b_last - b_c[p] + i_c[p] - m_new)
        decay = jnp.exp(b_last + m_prev[p] - m_new)
        wv = (w * vh[p].astype(F32)).astype(BF16)
        ct[s, h] = decay * c_old[p] + _dot_tn(kh[p], wv)
        m_s[s, h] = jnp.broadcast_to(m_new, m_s.shape[2:])


def _mlstm(ml, og, gates, conv_w, conv_b, qk_scale, batch, seq):
    nc = seq // ML_CHUNK
    group = ML_SEQS_PER_STEP if batch % ML_SEQS_PER_STEP == 0 else 1

    def whole(a):
        return pl.BlockSpec(a.shape, lambda b, c: (0, 0))

    def chunk(width, col):
        return pl.BlockSpec((group, ML_CHUNK, width), lambda b, c: (b, c, col))

    out = pl.pallas_call(
        _mlstm_kernel,
        grid=(batch // group, nc),
        in_specs=[chunk(2 * ML_PAD_WIDTH, 0), chunk(ML_PAD_WIDTH, 2), chunk(ML_PAD_WIDTH, 0), chunk(LANES, 0),
                  whole(conv_w), whole(conv_b), whole(qk_scale)],
        out_specs=chunk(ML_PAD_WIDTH, 0),
        out_shape=jax.ShapeDtypeStruct((batch, seq, ML_PAD_WIDTH), BF16),
        scratch_shapes=[pltpu.VMEM((group, 2 * ML_CHUNK, 2 * ML_PAD_WIDTH), BF16),
                        pltpu.VMEM((group, ML_HEADS, ML_HEAD_PAD, ML_HEAD_PAD), F32),
                        pltpu.VMEM((group, ML_HEADS, 8, LANES), F32)],
        compiler_params=_cparams("parallel", "arbitrary"),
        name="mlstm",
    )(ml.reshape(batch, seq, -1), ml.reshape(batch, seq, -1), og.reshape(batch, seq, -1),
      gates.reshape(batch, seq, -1), conv_w, conv_b, qk_scale)
    return out.reshape(batch * seq, ML_PAD_WIDTH)


def _mixer_out_kernel(x_ref, o0_ref, l0_ref, o1_ref, l1_ref, o2_ref, l2_ref, sg_ref, hc_ref, gp_ref,
                      sgg_ref, sgb_ref, sgw_ref, sgbias_ref, wa_ref, wb_ref, wc_ref, wo_ref, lng_ref, lnb_ref,
                      xf_ref, xb_ref, unperm, *, alpha):
    tm = x_ref.shape[0]
    slabs = ATT_GROUP_WIDTH // LANES
    att_refs = (l0_ref, l1_ref, l2_ref, o0_ref, o1_ref, o2_ref)

    for slot, ref in enumerate(att_refs):
        dil = ref.shape[0]
        if dil > 1:
            for r in range(dil):
                for c in range(slabs):
                    unperm[slabs * slot + c, pl.ds(r, tm // dil, stride=dil), :] = (
                        ref[r, :, c * LANES:(c + 1) * LANES].astype(F32))

    def token_order(slot, rows):
        ref = att_refs[slot]
        if ref.shape[0] == 1:
            return ref[0, rows, :].astype(F32)
        return jnp.concatenate([unperm[slabs * slot + c, rows, :] for c in range(slabs)], axis=1)

    row = lax.broadcasted_iota(jnp.int32, (SG_CHUNK, SG_CHUNK), 0)
    colm = lax.broadcasted_iota(jnp.int32, (SG_CHUNK, SG_CHUNK), 1)
    causal = row >= colm
    group_dim = SG_WIDTH // SG_GROUPS
    sg_w = [jnp.where(causal, sgw_ref[g], 0.0).astype(BF16) for g in range(SG_GROUPS)]

    for h in range(tm // MIXER_SUB_ROWS):
        rows = slice(h * MIXER_SUB_ROWS, (h + 1) * MIXER_SUB_ROWS)

        l0, l1, l2 = token_order(0, rows), token_order(1, rows), token_order(2, rows)
        lmax = jnp.maximum(jnp.maximum(l0, l1), l2)
        e0, e1, e2 = jnp.exp(l0 - lmax), jnp.exp(l1 - lmax), jnp.exp(l2 - lmax)
        inv = 1.0 / (e0 + e1 + e2)
        y_a = ((e0 * inv) * token_order(3, rows) + (e1 * inv) * token_order(4, rows)
               + (e2 * inv) * token_order(5, rows))

        u = sg_ref[rows, 0:SG_WIDTH].astype(F32)
        v = sg_ref[rows, SG_WIDTH:2 * SG_WIDTH].astype(F32)
        v = _layer_norm(v, sgg_ref[...], sgb_ref[...]).astype(BF16)
        chunks = []
        for c in range(MIXER_SUB_ROWS // SG_CHUNK):
            crows = slice(c * SG_CHUNK, (c + 1) * SG_CHUNK)
            parts = [_dot(sg_w[g], v[crows, g * group_dim:(g + 1) * group_dim]) + sgbias_ref[:, g:g + 1]
                     for g in range(SG_GROUPS)]
            chunks.append(jnp.concatenate(parts, axis=1))
        y_b = u * jnp.concatenate(chunks, axis=0)

        def gate(i):
            return gp_ref[rows, i * D_MODEL:(i + 1) * D_MODEL].astype(F32)

        z = (gate(0) * _dot(y_a.astype(BF16), wa_ref[...])
             + gate(1) * _dot(y_b.astype(BF16), wb_ref[...])
             + gate(2) * _dot(hc_ref[rows, :], wc_ref[...]))
        mix = _dot(z.astype(BF16), wo_ref[...])
        out = _layer_norm(alpha * x_ref[rows, :] + mix, lng_ref[...], lnb_ref[...])
        xf_ref[rows, :] = out
        xb_ref[rows, :] = out.astype(BF16)


def _mixer_out(x, att_outs, sg, hc, gpre, p, alpha, seq, tm=512):
    n = x.shape[0]
    tiles_per_seq = seq // tm

    def rows(width):
        return pl.BlockSpec((tm, width), lambda i: (i, 0))

    def whole(a):
        return pl.BlockSpec(a.shape, lambda i: (0,) * a.ndim)

    def residue_major(dil):
        return pl.BlockSpec((None, dil, tm // dil, ATT_GROUP_WIDTH),
                            lambda i: (i // tiles_per_seq, 0, i % tiles_per_seq, 0))

    att_specs, att_args = [], []
    for (o, lse), (_, dil) in zip(att_outs, ATT_GROUPS):
        att_specs += [residue_major(dil)] * 2
        att_args += [o, lse]
    weights = [p["sg_ln_g"], p["sg_ln_b"], p["sg_w"], p["sg_bias_t"], p["w_br_a"], p["w_br_b"], p["w_br_c"],
               p["w_out"], p["ln_g0"], p["ln_b0"]]
    return pl.pallas_call(
        functools.partial(_mixer_out_kernel, alpha=alpha),
        grid=(n // tm,),
        in_specs=[rows(D_MODEL)] + att_specs
                 + [rows(2 * SG_WIDTH), rows(ML_PAD_WIDTH), rows(N_BRANCH * D_MODEL)]
                 + [whole(w) for w in weights],
        out_specs=[rows(D_MODEL), rows(D_MODEL)],
        out_shape=[jax.ShapeDtypeStruct((n, D_MODEL), F32), jax.ShapeDtypeStruct((n, D_MODEL), BF16)],
        scratch_shapes=[pltpu.VMEM((2 * len(ATT_GROUPS) * ATT_GROUP_WIDTH // LANES, tm, LANES), F32)],
        compiler_params=_cparams("parallel"),
        name="mixer_out",
    )(x, *att_args, sg, hc, gpre, *weights)


def _swiglu_tile(x, w1_ref, w3_ref, w2_ref, prod, tf):
    for c in range(D_FF // tf):
        cols = slice(c * tf, (c + 1) * tf)
        h1 = _dot(x, w1_ref[:, cols])
        h3 = _dot(x, w3_ref[:, cols])
        prod[:, cols] = (h1 * _sigmoid(h1) * h3).astype(BF16)
    return _dot(prod[...], w2_ref[...])


def _ffn_kernel(xf_ref, xb_ref, w1_ref, w3_ref, w2_ref, lng_ref, lnb_ref, of_ref, ob_ref, prod, *, alpha, tf):
    f = _swiglu_tile(xb_ref[...], w1_ref, w3_ref, w2_ref, prod, tf)
    out = _layer_norm(alpha * xf_ref[...] + f, lng_ref[...], lnb_ref[...])
    of_ref[...] = out
    ob_ref[...] = out.astype(BF16)


def _ffn(xf, xb, w1, w3, w2, ln_g, ln_b, alpha, tm=512, tf=256):
    n = xf.shape[0]

    def resident(shape):
        return pl.BlockSpec(shape, lambda i: (0, 0), pipeline_mode=pl.Buffered(1))

    return pl.pallas_call(
        functools.partial(_ffn_kernel, alpha=alpha, tf=tf),
        grid=(n // tm,),
        in_specs=[pl.BlockSpec((tm, D_MODEL), lambda i: (i, 0)),
                  pl.BlockSpec((tm, D_MODEL), lambda i: (i, 0)),
                  resident((D_MODEL, D_FF)), resident((D_MODEL, D_FF)), resident((D_FF, D_MODEL)),
                  pl.BlockSpec((1, D_MODEL), lambda i: (0, 0)),
                  pl.BlockSpec((1, D_MODEL), lambda i: (0, 0))],
        out_specs=[pl.BlockSpec((tm, D_MODEL), lambda i: (i, 0))] * 2,
        out_shape=[jax.ShapeDtypeStruct((n, D_MODEL), F32), jax.ShapeDtypeStruct((n, D_MODEL), BF16)],
        scratch_shapes=[pltpu.VMEM((tm, D_FF), BF16)],
        compiler_params=_cparams("parallel"),
        name="ffn",
    )(xf, xb, w1, w3, w2, ln_g, ln_b)


def _router_kernel(x_ref, w_ref, b_ref, info_ref):
    xh, xm, _ = _split3(x_ref[...])
    wh, wm, _ = _split3(w_ref[...])
    logits = (_dot(xh, wh) + (_dot(xh, wm) + _dot(xm, wh))) + b_ref[...]
    lane = lax.broadcasted_iota(jnp.int32, logits.shape, 1)
    neg_inf = jnp.float32(-jnp.inf)
    logits = jnp.where(lane < N_EXPERTS, logits, neg_inf)
    v1 = jnp.max(logits, axis=-1, keepdims=True)
    i1 = jnp.min(jnp.where(logits == v1, lane, LANES), axis=-1, keepdims=True)
    rest = jnp.where(lane == i1, neg_inf, logits)
    v2 = jnp.max(rest, axis=-1, keepdims=True)
    i2 = jnp.min(jnp.where(rest == v2, lane, LANES), axis=-1, keepdims=True)
    e = jnp.exp(v2 - v1)
    inv = 1.0 / (1.0 + e)
    info_ref[...] = jnp.where(lane == 0, i1.astype(F32),
                              jnp.where(lane == 1, i2.astype(F32),
                                        jnp.where(lane == 2, inv, jnp.where(lane == 3, e * inv, 0.0))))


def _router(xf, router_w, router_b, tm=1024):
    n = xf.shape[0]
    w = jnp.pad(router_w, ((0, 0), (0, LANES - N_EXPERTS)))
    b = jnp.pad(router_b, (0, LANES - N_EXPERTS))[None, :]
    return pl.pallas_call(
        _router_kernel,
        grid=(n // tm,),
        in_specs=[pl.BlockSpec((tm, D_MODEL), lambda i: (i, 0)),
                  pl.BlockSpec((D_MODEL, LANES), lambda i: (0, 0)),
                  pl.BlockSpec((1, LANES), lambda i: (0, 0))],
        out_specs=pl.BlockSpec((tm, LANES), lambda i: (i, 0)),
        out_shape=jax.ShapeDtypeStruct((n, LANES), F32),
        compiler_params=_cparams("parallel"),
        name="router",
    )(xf, w, b)


def _routing_plan(info, n):
    tile = EXPERT_TILE
    n_tiles = TOP_K * n // tile + N_EXPERTS
    expert = info[:, 0:TOP_K].astype(jnp.int32).reshape(-1)
    onehot = (expert[:, None] == jnp.arange(N_EXPERTS, dtype=jnp.int32)[None, :]).astype(jnp.int32)
    running = jnp.cumsum(onehot, axis=0)
    rank = jnp.sum(onehot * running, axis=1) - 1
    counts = running[-1]
    padded = ((counts + tile - 1) // tile) * tile
    ends = jnp.cumsum(padded)
    starts = ends - padded
    dest = jnp.sum(onehot * starts[None, :], axis=1) + rank
    used = ends[-1] // tile
    tile_start = jnp.arange(n_tiles, dtype=jnp.int32) * tile
    tile_expert = jnp.sum((tile_start[:, None] >= ends[None, :]).astype(jnp.int32), axis=1)
    last_expert = jnp.sum((tile_start[used - 1] >= ends).astype(jnp.int32))
    tile_expert = jnp.minimum(tile_expert, last_expert)
    plan = dict(dest=dest, tile_expert=tile_expert, used=used.reshape(1),
                pad_start=starts + counts, pad_count=padded - counts)
    return {k: v.astype(jnp.int32) for k, v in plan.items()}, n_tiles


def _rows_wait(src, dst, rows, sem):
    pltpu.make_async_copy(src.at[pl.ds(0, rows)], dst.at[pl.ds(0, rows)], sem).wait()


ROW_SUBLANES = D_MODEL // LANES


def _tile_copy(src, src_row, dst, dst_row, sem):
    s0 = pl.multiple_of(src_row * ROW_SUBLANES, ROW_SUBLANES)
    d0 = pl.multiple_of(dst_row * ROW_SUBLANES, ROW_SUBLANES)
    return pltpu.make_async_copy(src.at[pl.ds(s0, ROW_SUBLANES)], dst.at[pl.ds(d0, ROW_SUBLANES)], sem)


def _to_row_tiles(ref, x):
    m = x.shape[0]
    for c in range(ROW_SUBLANES):
        ref[pl.ds(c, m, stride=ROW_SUBLANES), :] = x[:, c * LANES:(c + 1) * LANES]


def _from_row_tiles(ref):
    m = ref.shape[0] // ROW_SUBLANES
    return jnp.concatenate([ref[pl.ds(c, m, stride=ROW_SUBLANES), :] for c in range(ROW_SUBLANES)], axis=1)


DISPATCH_SLOTS = 3


def _dispatch_kernel(pad_start_ref, pad_count_ref, used_ref, dest_ref, x_ref, xs_ref, buf, tiles, zero_row,
                     load_sem, row_sem, pad_sem, *, tm):
    i = pl.program_id(0)

    @pl.when(i == 0)
    def _():
        zero_row[...] = jnp.zeros(zero_row.shape, F32)
        for e in range(N_EXPERTS):
            def fill(j, carry, e=e):
                _tile_copy(zero_row, 0, xs_ref, pad_start_ref[e] + j, pad_sem).start()
                return carry
            lax.fori_loop(0, pad_count_ref[e], fill, 0)
        for e in range(N_EXPERTS):
            def drain(j, carry):
                _tile_copy(zero_row, 0, xs_ref, 0, pad_sem).wait()
                return carry
            lax.fori_loop(0, pad_count_ref[e], drain, 0)

        tile_rows = tm * ROW_SUBLANES
        spare = xs_ref.shape[0] // tile_rows - used_ref[0]
        tiles[1] = jnp.zeros(tiles.shape[1:], F32)

        def zero_tile(j):
            row0 = pl.multiple_of((used_ref[0] + j) * tile_rows, tile_rows)
            return pltpu.make_async_copy(tiles.at[1], xs_ref.at[pl.ds(row0, tile_rows)], pad_sem)

        def fill_tile(j, carry):
            zero_tile(j).start()
            return carry

        def drain_tile(j, carry):
            zero_tile(j).wait()
            return carry

        lax.fori_loop(0, spare, fill_tile, 0)
        lax.fori_loop(0, spare, drain_tile, 0)

    steps = pl.num_programs(0)
    slot = i % DISPATCH_SLOTS
    prev_slot = (i + DISPATCH_SLOTS - 1) % DISPATCH_SLOTS
    stage = i % 2

    def load(tile, into):
        return pltpu.make_async_copy(x_ref.at[pl.ds(tile * tm, tm), :], buf.at[into], load_sem.at[into])

    def wait_rows(of_stage):
        for _ in range(TOP_K):
            _rows_wait(tiles.at[of_stage], xs_ref, tm * ROW_SUBLANES, row_sem.at[of_stage])

    @pl.when(i == 0)
    def _():
        load(0, 0).start()

        @pl.when(steps > 1)
        def _():
            load(1, 1).start()

    load(i, slot).wait()
    _to_row_tiles(tiles.at[stage], buf[slot])

    def start(t, carry):
        for c in range(TOP_K):
            _tile_copy(tiles.at[stage], t, xs_ref, dest_ref[0, 0, TOP_K * t + c], row_sem.at[stage]).start(
                priority=c % 2)
        return carry

    lax.fori_loop(0, tm, start, 0, unroll=DMA_ISSUE_UNROLL)

    @pl.when(i > 0)
    def _():
        wait_rows(1 - stage)

    @pl.when(i + 2 < steps)
    def _():
        load(i + 2, prev_slot).start()

    @pl.when(i == steps - 1)
    def _():
        wait_rows(stage)


def _dispatch(xf, dest, pad_start, pad_count, used, n_slots, tm=EXPERT_TILE):
    n = xf.shape[0]
    dest3 = dest.reshape(n // tm, 1, TOP_K * tm)
    grid_spec = pltpu.PrefetchScalarGridSpec(
        num_scalar_prefetch=3,
        grid=(n // tm,),
        in_specs=[pl.BlockSpec((1, 1, TOP_K * tm), lambda i, ps, pc, nu: (i, 0, 0), memory_space=pltpu.SMEM),
                  pl.BlockSpec(memory_space=pl.ANY)],
        out_specs=pl.BlockSpec(memory_space=pl.ANY),
        scratch_shapes=[pltpu.VMEM((DISPATCH_SLOTS, tm, D_MODEL), F32),
                        pltpu.VMEM((2, tm * ROW_SUBLANES, LANES), F32),
                        pltpu.VMEM((ROW_SUBLANES, LANES), F32),
                        pltpu.SemaphoreType.DMA((DISPATCH_SLOTS,)),
                        pltpu.SemaphoreType.DMA((2,)),
                        pltpu.SemaphoreType.DMA],
    )
    return pl.pallas_call(
        functools.partial(_dispatch_kernel, tm=tm),
        grid_spec=grid_spec,
        out_shape=jax.ShapeDtypeStruct((n_slots * ROW_SUBLANES, LANES), F32),
        compiler_params=_cparams("arbitrary"),
        name="moe_dispatch",
    )(pad_start, pad_count, used, dest3, xf)


def _experts_kernel(te_ref, used_ref, xs_ref, w1_ref, w3_ref, w2_ref, ys_ref, prod, *, tf):
    del te_ref
    i = pl.program_id(0)

    @pl.when(i < used_ref[0])
    def _():
        x = _from_row_tiles(xs_ref).astype(BF16)
        _to_row_tiles(ys_ref, _swiglu_tile(x, w1_ref, w3_ref, w2_ref, prod, tf))

    @pl.when(i >= used_ref[0])
    def _():
        ys_ref[...] = jnp.zeros(ys_ref.shape, ys_ref.dtype)


def _experts(xs, tile_expert, used, w1, w3, w2, n_tiles, tf=256):
    tile = EXPERT_TILE
    grid_spec = pltpu.PrefetchScalarGridSpec(
        num_scalar_prefetch=2,
        grid=(n_tiles,),
        in_specs=[pl.BlockSpec((tile * ROW_SUBLANES, LANES), lambda i, te, nu: (jnp.minimum(i, nu[0] - 1), 0)),
                  pl.BlockSpec((None, D_MODEL, D_FF), lambda i, te, nu: (te[i], 0, 0)),
                  pl.BlockSpec((None, D_MODEL, D_FF), lambda i, te, nu: (te[i], 0, 0)),
                  pl.BlockSpec((None, D_FF, D_MODEL), lambda i, te, nu: (te[i], 0, 0))],
        out_specs=pl.BlockSpec((tile * ROW_SUBLANES, LANES), lambda i, te, nu: (i, 0)),
        scratch_shapes=[pltpu.VMEM((tile, D_FF), BF16)],
    )
    return pl.pallas_call(
        functools.partial(_experts_kernel, tf=tf),
        grid_spec=grid_spec,
        out_shape=jax.ShapeDtypeStruct((n_tiles * tile * ROW_SUBLANES, LANES), F32),
        compiler_params=_cparams("arbitrary"),
        name="moe_experts",
    )(tile_expert, used, xs, w1, w3, w2)


def _combine_kernel(dest_ref, next_dest_ref, xf_ref, info_ref, ys_ref, lng_ref, lnb_ref, of_ref, got, sem, *, alpha):
    tm = xf_ref.shape[0]
    i = pl.program_id(0)
    slot = i % 2

    def gather(dests, into):
        def start(t, carry):
            for c in range(TOP_K):
                _tile_copy(ys_ref, dests[0, 0, TOP_K * t + c], got.at[into, c], t, sem.at[into]).start(
                    priority=c % 2)
            return carry
        lax.fori_loop(0, tm, start, 0, unroll=DMA_ISSUE_UNROLL)

    @pl.when(i == 0)
    def _():
        gather(dest_ref, 0)

    @pl.when(i + 1 < pl.num_programs(0))
    def _():
        gather(next_dest_ref, 1 - slot)

    for c in range(TOP_K):
        _rows_wait(ys_ref, got.at[slot, c], tm * ROW_SUBLANES, sem.at[slot])
    info = info_ref[...]
    mix = (info[:, 2:3] * _from_row_tiles(got.at[slot, 0]) + info[:, 3:4] * _from_row_tiles(got.at[slot, 1]))
    of_ref[...] = _layer_norm(alpha * xf_ref[...] + mix, lng_ref[...], lnb_ref[...])


def _combine(xf, info, ys, dest, ln_g, ln_b, alpha, tm=256):
    n = xf.shape[0]
    steps = n // tm
    dest3 = dest.reshape(steps, 1, TOP_K * tm)
    return pl.pallas_call(
        functools.partial(_combine_kernel, alpha=alpha),
        grid=(steps,),
        in_specs=[pl.BlockSpec((1, 1, TOP_K * tm), lambda i: (i, 0, 0), memory_space=pltpu.SMEM),
                  pl.BlockSpec((1, 1, TOP_K * tm), lambda i: (jnp.minimum(i + 1, steps - 1), 0, 0),
                               memory_space=pltpu.SMEM),
                  pl.BlockSpec((tm, D_MODEL), lambda i: (i, 0)),
                  pl.BlockSpec((tm, LANES), lambda i: (i, 0)),
                  pl.BlockSpec(memory_space=pl.ANY),
                  pl.BlockSpec((1, D_MODEL), lambda i: (0, 0)),
                  pl.BlockSpec((1, D_MODEL), lambda i: (0, 0))],
        out_specs=pl.BlockSpec((tm, D_MODEL), lambda i: (i, 0)),
        out_shape=jax.ShapeDtypeStruct((n, D_MODEL), F32),
        scratch_shapes=[pltpu.VMEM((2, TOP_K, tm * ROW_SUBLANES, LANES), F32), pltpu.SemaphoreType.DMA((2,))],
        compiler_params=_cparams("arbitrary"),
        name="moe_combine",
    )(dest3, dest3, xf, info, ys, ln_g, ln_b)


def _moe(xf, router_w, router_b, w1, w3, w2, ln_g, ln_b, alpha):
    n = xf.shape[0]
    info = _router(xf, router_w, router_b)
    plan, n_tiles = _routing_plan(info, n)
    xs = _dispatch(xf, plan["dest"], plan["pad_start"], plan["pad_count"], plan["used"], n_tiles * EXPERT_TILE)
    ys = _experts(xs, plan["tile_expert"], plan["used"], w1, w3, w2, n_tiles)
    return _combine(xf, info, ys, plan["dest"], ln_g, ln_b, alpha)


def _pad_heads(w):
    lead = w.shape[:-1]
    w = w.reshape(*lead, ML_HEADS, ML_HEAD_DIM)
    w = jnp.pad(w, [(0, 0)] * len(lead) + [(0, 0), (0, ML_HEAD_PAD - ML_HEAD_DIM)])
    return w.reshape(*lead, ML_PAD_WIDTH)


def _group_qkv(w, group):
    gw = ATT_GROUP_WIDTH
    return jnp.concatenate([w[..., part * ATT_WIDTH + group * gw:part * ATT_WIDTH + (group + 1) * gw]
                            for part in range(3)], axis=-1)


def _layer_params(layer, w_in, b_in, conv_w, conv_b, sg_ln_g, sg_ln_b, sg_w, sg_b,
                  w_br_a, w_br_b, w_br_c, w_out, ln_g, ln_b):
    w, b = w_in[layer].astype(BF16), b_in[layer]
    o_att, o_sg, o_qk, o_v, o_o, o_i, o_g = 0, 2304, 3840, 5376, 6144, 6912, 6920
    ones_col = jnp.zeros((ML_HEADS, ML_HEAD_PAD), F32).at[:, ML_HEAD_DIM].set(1.0).reshape(ML_PAD_WIDTH)
    w_qk = jnp.concatenate([_pad_heads(w[:, o_qk:o_qk + ML_WIDTH]), _pad_heads(w[:, o_qk + ML_WIDTH:o_v])], axis=1)
    b_qk = jnp.concatenate([_pad_heads(b[o_qk:o_qk + ML_WIDTH]), _pad_heads(b[o_qk + ML_WIDTH:o_v])])
    qk_scale = jnp.concatenate([jnp.ones((ML_PAD_WIDTH,), F32),
                                jnp.full((ML_PAD_WIDTH,), ML_HEAD_DIM ** -0.5, F32)])
    cw, cb = conv_w[layer], conv_b[layer]
    w_att, b_att = w[:, o_att:o_sg], b[o_att:o_sg]
    return dict(
        w_att=[_group_qkv(w_att, g) for g in range(len(ATT_GROUPS))],
        b_att=[_group_qkv(b_att, g)[None, :] for g in range(len(ATT_GROUPS))],
        w_sg=w[:, o_sg:o_qk], b_sg=b[None, o_sg:o_qk],
        w_ml=jnp.concatenate([w_qk, _pad_heads(w[:, o_v:o_o])], axis=1),
        b_ml=jnp.concatenate([b_qk, _pad_heads(b[o_v:o_o]) + ones_col])[None, :],
        qk_scale=qk_scale[None, :],
        w_mlo=_pad_heads(w[:, o_o:o_i]), b_mlo=_pad_heads(b[o_o:o_i])[None, :],
        w_if=jnp.pad(w[:, o_i:o_g], ((0, 0), (0, LANES - 2 * ML_HEADS))),
        b_if=jnp.pad(b[o_i:o_g], (0, LANES - 2 * ML_HEADS))[None, :],
        w_g=w[:, o_g:], b_g=b[None, o_g:],
        conv_w=jnp.concatenate([_pad_heads(cw[:, :ML_WIDTH]), _pad_heads(cw[:, ML_WIDTH:])], axis=1),
        conv_b=jnp.concatenate([_pad_heads(cb[:ML_WIDTH]), _pad_heads(cb[ML_WIDTH:])])[None, :],
        sg_ln_g=sg_ln_g[layer][None, :], sg_ln_b=sg_ln_b[layer][None, :],
        sg_w=sg_w[layer], sg_bias_t=sg_b[layer].T,
        w_br_a=w_br_a[layer].astype(BF16), w_br_b=w_br_b[layer].astype(BF16),
        w_br_c=_pad_heads(w_br_c[layer].T).T.astype(BF16),
        w_out=w_out[layer].astype(BF16),
        ln_g0=ln_g[layer, 0][None, :], ln_b0=ln_b[layer, 0][None, :],
        ln_g1=ln_g[layer, 1][None, :], ln_b1=ln_b[layer, 1][None, :],
    )


def kernel(x, positions, w_in, b_in, conv_w, conv_b, sg_ln_g, sg_ln_b, sg_w, sg_b, w_br_a, w_br_b, w_br_c, w_out,
           ln_g, ln_b, ffn_w1, ffn_w3, ffn_w2, router_w, router_b, moe_w1, moe_w3, moe_w2):
    batch, seq, _ = x.shape
    n = batch * seq
    depth = w_in.shape[0]
    alpha = (2.0 * depth) ** 0.25
    cos, sin = _rope_tables(positions)
    xf = x.reshape(n, D_MODEL)
    xb = xf.astype(BF16)
    for layer in range(depth):
        p = _layer_params(layer, w_in, b_in, conv_w, conv_b, sg_ln_g, sg_ln_b, sg_w, sg_b,
                          w_br_a, w_br_b, w_br_c, w_out, ln_g, ln_b)
        undilated = [g for g, (_, dil) in enumerate(ATT_GROUPS) if dil == 1]
        sg, ml, og, gates_if, gbr, *qkv_plain = _projections(xb, [
            (p["w_sg"], p["b_sg"], BF16, _gelu_tanh),
            (p["w_ml"], p["b_ml"], BF16, None),
            (p["w_mlo"], p["b_mlo"], BF16, _sigmoid),
            (p["w_if"], p["b_if"], F32, None),
            (p["w_g"], p["b_g"], BF16, _sigmoid),
        ] + [(p["w_att"][g], p["b_att"][g], BF16, None) for g in undilated])
        att_outs = []
        for g, (_, dil) in enumerate(ATT_GROUPS):
            if dil == 1:
                qkv = qkv_plain[undilated.index(g)].reshape(batch, 1, seq, ATT_PARTS_WIDTH)
            else:
                qkv = _linear_dilated(xb, p["w_att"][g], p["b_att"][g], dil, batch, seq)
            att_outs.append(_attention_group(qkv, cos, sin, g, batch, seq))
        hc = _mlstm(ml, og, gates_if, p["conv_w"], p["conv_b"], p["qk_scale"], batch, seq)
        xf, xb = _mixer_out(xf, att_outs, sg, hc, gbr, p, alpha, seq)
        j = layer // 2
        if layer % 2 == 0:
            xf, xb = _ffn(xf, xb, ffn_w1[j].astype(BF16), ffn_w3[j].astype(BF16), ffn_w2[j].astype(BF16),
                          p["ln_g1"], p["ln_b1"], alpha)
        else:
            xf = _moe(xf, router_w[j], router_b[j], moe_w1[j].astype(BF16), moe_w3[j].astype(BF16),
                      moe_w2[j].astype(BF16), p["ln_g1"], p["ln_b1"], alpha)
            xb = xf.astype(BF16)
    return xf.reshape(batch, seq, D_MODEL)
```

```python
import functools

import numpy as np
import jax
import jax.numpy as jnp
from jax import lax
from jax.experimental import pallas as pl
from jax.experimental.pallas import tpu as pltpu

F32 = jnp.float32
BF16 = jnp.bfloat16

D_MODEL = 1024
ATT_GROUPS = ((128, 1), (512, 4), (2048, 16))
ATT_HEADS_PER_GROUP = 4
ATT_HEAD_DIM = 64
ATT_WIDTH = 768
ATT_GROUP_WIDTH = ATT_HEADS_PER_GROUP * ATT_HEAD_DIM
ATT_PARTS_WIDTH = 3 * ATT_GROUP_WIDTH
ROPE_THETA = 10000.0
SG_CHUNK = 128
SG_GROUPS = 6
SG_WIDTH = 768
ML_HEADS = 4
ML_HEAD_DIM = 192
ML_HEAD_PAD = 256
ML_WIDTH = 768
ML_PAD_WIDTH = ML_HEADS * ML_HEAD_PAD
ML_CHUNK = 128
ML_CONV = 4
N_BRANCH = 3
D_FF = 2816
N_EXPERTS = 8
TOP_K = 2
LN_EPS = 1e-5
BLOCK = 128
LANES = 128
VMEM_LIMIT = 56 * 1024 * 1024
EXPERT_TILE = 512
DMA_ISSUE_UNROLL = 8
MIXER_SUB_ROWS = 256


def _cparams(*sem):
    return pltpu.CompilerParams(dimension_semantics=sem, vmem_limit_bytes=VMEM_LIMIT)


def _layer_norm(x, g, b):
    mu = jnp.mean(x, axis=-1, keepdims=True)
    xc = x - mu
    var = jnp.mean(xc * xc, axis=-1, keepdims=True)
    return xc * lax.rsqrt(var + LN_EPS) * g + b


def _gelu_tanh(x):
    return 0.5 * x * (1.0 + jnp.tanh(np.sqrt(2.0 / np.pi) * (x + 0.044715 * (x * x * x))))


def _sigmoid(x):
    return 0.5 * (1.0 + jnp.tanh(0.5 * x))


def _dot(a, b):
    return jnp.dot(a, b, preferred_element_type=F32)


def _dot_nt(a, b):
    return lax.dot_general(a, b, (((1,), (1,)), ((), ())), preferred_element_type=F32)


def _dot_tn(a, b):
    return lax.dot_general(a, b, (((0,), (0,)), ((), ())), preferred_element_type=F32)


def _split3(x):
    hi = x.astype(BF16)
    r1 = x - hi.astype(F32)
    mid = r1.astype(BF16)
    lo = (r1 - mid.astype(F32)).astype(BF16)
    return hi, mid, lo


def _rider_specs(arrays, steps):
    specs, shapes = [], []
    for a in arrays:
        span = 1
        while (a.shape[0] * span) % steps or (a.shape[0] * span // steps) % 16:
            span *= 2
        rows = a.shape[0] * span // steps
        specs.append(pl.BlockSpec((rows, a.shape[1]), lambda i, span=span: (i // span, 0)))
        shapes.append(jax.ShapeDtypeStruct(a.shape, BF16))
    return specs, shapes


def _cast_riders(in_refs, out_refs):
    for i_ref, o_ref in zip(in_refs, out_refs):
        o_ref[...] = i_ref[...].astype(o_ref.dtype)


def _projections_kernel(x_ref, *refs, acts, riders):
    n = len(acts)
    ws, bs, rin = refs[:n], refs[n:2 * n], refs[2 * n:2 * n + riders]
    outs, rout = refs[2 * n + riders:3 * n + riders], refs[3 * n + riders:]
    x = x_ref[...]
    for w_ref, b_ref, o_ref, act in zip(ws, bs, outs, acts):
        res = _dot(x, w_ref[...]) + b_ref[...]
        o_ref[...] = (res if act is None else act(res)).astype(o_ref.dtype)
    _cast_riders(rin, rout)


def _projections(x, heads, riders=(), tm=512):
    m, k = x.shape

    def resident(a):
        return pl.BlockSpec(a.shape, lambda i: (0, 0), pipeline_mode=pl.Buffered(1))

    ws = [h[0] for h in heads]
    bs = [h[1] for h in heads]
    rider_specs, rider_shapes = _rider_specs(riders, m // tm)
    return pl.pallas_call(
        functools.partial(_projections_kernel, acts=tuple(h[3] for h in heads), riders=len(riders)),
        grid=(m // tm,),
        in_specs=([pl.BlockSpec((tm, k), lambda i: (i, 0))] + [resident(w) for w in ws] + [resident(b) for b in bs]
                  + rider_specs),
        out_specs=[pl.BlockSpec((tm, w.shape[1]), lambda i: (i, 0)) for w in ws] + rider_specs,
        out_shape=[jax.ShapeDtypeStruct((m, h[0].shape[1]), h[2]) for h in heads] + rider_shapes,
        compiler_params=_cparams("arbitrary"),
        name="projections",
    )(x, *ws, *bs, *riders)


def _linear_dilated_kernel(x_ref, w_ref, b_ref, o_ref, acc, *, dil):
    res = _dot(x_ref[...], w_ref[...]) + b_ref[...]
    slabs, rows, _ = acc.shape
    for c in range(slabs):
        acc[c] = res[:, c * LANES:(c + 1) * LANES]
    sub = rows // dil
    for r in range(dil):
        for c in range(slabs):
            o_ref[r, :, c * LANES:(c + 1) * LANES] = acc[c, pl.ds(r, sub, stride=dil), :].astype(o_ref.dtype)


def _linear_dilated(x, w, b, dil, batch, seq):
    k = x.shape[1]
    n = w.shape[1]
    tn = ATT_GROUP_WIDTH
    sub = seq // dil
    return pl.pallas_call(
        functools.partial(_linear_dilated_kernel, dil=dil),
        grid=(batch, n // tn),
        in_specs=[pl.BlockSpec((seq, k), lambda i, j: (i, 0)),
                  pl.BlockSpec((k, tn), lambda i, j: (0, j)),
                  pl.BlockSpec((1, tn), lambda i, j: (0, j))],
        out_specs=pl.BlockSpec((None, dil, sub, tn), lambda i, j: (i, 0, 0, j)),
        out_shape=jax.ShapeDtypeStruct((batch, dil, sub, n), BF16),
        scratch_shapes=[pltpu.VMEM((tn // LANES, seq, LANES), F32)],
        compiler_params=_cparams("parallel", "parallel"),
        name=f"linear_dil{dil}",
    )(x, w, b)


def _rope_table_kernel(pos_ref, freq_ref, sign_ref, cos_ref, sin_ref):
    ang = pos_ref[...] * freq_ref[...]
    cos_ref[...] = jnp.cos(ang)
    sin_ref[...] = jnp.sin(ang) * sign_ref[...]


def _rope_tables(positions):
    n = positions.size
    half = ATT_HEAD_DIM // 2
    freqs = ROPE_THETA ** (-jnp.arange(half, dtype=F32) * (2.0 / ATT_HEAD_DIM))
    lane = np.arange(LANES)
    freq_row = freqs[lane % half][None, :]
    sign_row = jnp.asarray(np.where(lane % ATT_HEAD_DIM < half, -1.0, 1.0), F32)[None, :]
    pos = positions.astype(F32).reshape(n, 1)
    tm = 2048
    return pl.pallas_call(
        _rope_table_kernel,
        grid=(n // tm,),
        in_specs=[pl.BlockSpec((tm, 1), lambda i: (i, 0)),
                  pl.BlockSpec((1, LANES), lambda i: (0, 0)),
                  pl.BlockSpec((1, LANES), lambda i: (0, 0))],
        out_specs=[pl.BlockSpec((tm, LANES), lambda i: (i, 0))] * 2,
        out_shape=[jax.ShapeDtypeStruct((n, LANES), F32)] * 2,
        compiler_params=_cparams("parallel"),
        name="rope_tables",
    )(pos, freq_row, sign_row)


def _attn_kernel(q_ref, k_ref, v_ref, cos_ref, sin_ref, o_ref, lse_ref, qs, ks, vs, *, sub, dil):
    nb = sub // BLOCK
    res = pl.program_id(1)
    lane = lax.broadcasted_iota(jnp.int32, (BLOCK, LANES), 1)
    first_half = (lane % ATT_HEAD_DIM) < (ATT_HEAD_DIM // 2)
    low_head = lane < ATT_HEAD_DIM

    def rope(x, c, s):
        partner = jnp.where(first_half, pltpu.roll(x, LANES - 32, 1), pltpu.roll(x, 32, 1))
        return x * c + partner * s

    ks[0:BLOCK, :] = jnp.zeros((BLOCK, ATT_GROUP_WIDTH), BF16)
    vs[0:BLOCK, :] = jnp.zeros((BLOCK, ATT_GROUP_WIDTH), BF16)

    def prep(n, carry):
        r0 = pl.multiple_of(n * BLOCK, BLOCK)
        if dil == 1:
            c = cos_ref[pl.ds(r0, BLOCK), :]
            s = sin_ref[pl.ds(r0, BLOCK), :]
        else:
            c = cos_ref[pl.ds(r0 * dil + res, BLOCK, stride=dil), :]
            s = sin_ref[pl.ds(r0 * dil + res, BLOCK, stride=dil), :]
        for pair in range(2):
            cols = slice(pair * LANES, (pair + 1) * LANES)
            q = rope(q_ref[pl.ds(r0, BLOCK), cols].astype(F32), c, s) * (ATT_HEAD_DIM ** -0.5)
            qs[pl.ds(r0, BLOCK), (2 * pair) * LANES:(2 * pair + 1) * LANES] = jnp.where(low_head, q, 0.0).astype(BF16)
            qs[pl.ds(r0, BLOCK), (2 * pair + 1) * LANES:(2 * pair + 2) * LANES] = jnp.where(low_head, 0.0, q).astype(BF16)
            k = rope(k_ref[pl.ds(r0, BLOCK), cols].astype(F32), c, s)
            ks[pl.ds(r0 + BLOCK, BLOCK), cols] = k.astype(BF16)
        vs[pl.ds(r0 + BLOCK, BLOCK), :] = v_ref[pl.ds(r0, BLOCK), :]
        return carry

    lax.fori_loop(0, nb, prep, 0)

    qi = lax.broadcasted_iota(jnp.int32, (BLOCK, 2 * BLOCK), 0)
    ki = lax.broadcasted_iota(jnp.int32, (BLOCK, 2 * BLOCK), 1)
    dist = qi + BLOCK - ki
    band = (dist >= 0) & (dist <= BLOCK)
    neg_inf = jnp.float32(-jnp.inf)
    bias_rest = jnp.where(band, 0.0, neg_inf)
    bias_first = jnp.where(band & (ki >= BLOCK), 0.0, neg_inf)

    def block(n, bias):
        r0 = pl.multiple_of(n * BLOCK, BLOCK)
        heads = range(ATT_HEADS_PER_GROUP)
        pair_cols = [slice(pair * LANES, (pair + 1) * LANES) for pair in range(2)]
        kw = [ks[pl.ds(r0, 2 * BLOCK), c] for c in pair_cols]
        vw = [vs[pl.ds(r0, 2 * BLOCK), c] for c in pair_cols]
        s = [_dot_nt(qs[pl.ds(r0, BLOCK), hd * LANES:(hd + 1) * LANES], kw[hd // 2]) + bias for hd in heads]
        m = [jnp.max(s[hd], axis=-1, keepdims=True) for hd in heads]
        p = [jnp.exp(s[hd] - m[hd]) for hd in heads]
        den = [jnp.sum(p[hd], axis=-1, keepdims=True) for hd in heads]
        o = [_dot(p[hd].astype(BF16), vw[hd // 2]) * (1.0 / den[hd]) for hd in heads]
        lse = [m[hd] + jnp.log(den[hd]) for hd in heads]
        for pair in range(2):
            lo, hi = 2 * pair, 2 * pair + 1
            o_ref[pl.ds(r0, BLOCK), pair_cols[pair]] = jnp.where(low_head, o[lo], o[hi]).astype(o_ref.dtype)
            lse_ref[pl.ds(r0, BLOCK), pair_cols[pair]] = jnp.where(low_head, lse[lo], lse[hi])

    block(0, bias_first)
    if nb <= 2:
        for n in range(1, nb):
            block(n, bias_rest)
        return

    def body(n, carry):
        block(n, bias_rest)
        return carry

    lax.fori_loop(1, nb, body, 0, unroll=2)


def _attention_group(qkv, cos, sin, group, batch, seq):
    window, dil = ATT_GROUPS[group]
    assert window // dil == BLOCK
    sub = seq // dil
    assert sub % BLOCK == 0

    def part(p):
        return pl.BlockSpec((None, None, sub, ATT_GROUP_WIDTH), lambda b, r: (b, r, 0, p))

    tab = pl.BlockSpec((seq, LANES), lambda b, r: (b, 0))
    out = pl.BlockSpec((None, None, sub, ATT_GROUP_WIDTH), lambda b, r: (b, r, 0, 0))
    return pl.pallas_call(
        functools.partial(_attn_kernel, sub=sub, dil=dil),
        grid=(batch, dil),
        in_specs=[part(0), part(1), part(2), tab, tab],
        out_specs=[out, out],
        out_shape=[jax.ShapeDtypeStruct((batch, dil, sub, ATT_GROUP_WIDTH), BF16),
                   jax.ShapeDtypeStruct((batch, dil, sub, ATT_GROUP_WIDTH), F32)],
        scratch_shapes=[pltpu.VMEM((sub, ATT_HEADS_PER_GROUP * LANES), BF16),
                        pltpu.VMEM((sub + BLOCK, ATT_GROUP_WIDTH), BF16),
                        pltpu.VMEM((sub + BLOCK, ATT_GROUP_WIDTH), BF16)],
        compiler_params=_cparams("parallel", "parallel"),
        name=f"attention_g{group}",
    )(qkv, qkv, qkv, cos, sin)


ML_SEQS_PER_STEP = 2


def _mlstm_kernel(qk_ref, v_ref, og_ref, g_ref, cw_ref, cb_ref, sc_ref, out_ref, frame, ct, m_s):
    taps = ML_CONV

    @pl.when(pl.program_id(1) == 0)
    def _():
        frame[:, 0:ML_CHUNK, :] = jnp.zeros((frame.shape[0], ML_CHUNK, 2 * ML_PAD_WIDTH), BF16)
        ct[...] = jnp.zeros(ct.shape, F32)
        m_s[...] = jnp.zeros(m_s.shape, F32)

    out_row = lax.broadcasted_iota(jnp.int32, (ML_CHUNK, 2 * ML_CHUNK), 0)
    src_row = lax.broadcasted_iota(jnp.int32, (ML_CHUNK, 2 * ML_CHUNK), 1)
    picks = [jnp.where(src_row == out_row + (ML_CHUNK - (taps - 1 - j)), 1.0, 0.0).astype(BF16)
             for j in range(taps - 1)]
    row = lax.broadcasted_iota(jnp.int32, (ML_CHUNK, ML_CHUNK), 0)
    colm = lax.broadcasted_iota(jnp.int32, (ML_CHUNK, ML_CHUNK), 1)
    causal = row >= colm
    tri = jnp.where(causal, 1.0, 0.0).astype(BF16)
    neg_inf = jnp.float32(-jnp.inf)

    seqs = range(qk_ref.shape[0])
    pairs = [(s, h) for s in seqs for h in range(ML_HEADS)]

    def head_cols(h):
        return slice(h * ML_HEAD_PAD, (h + 1) * ML_HEAD_PAD)

    act, gates, bcum, bcum_t, gates_t = {}, {}, {}, {}, {}
    for s in seqs:
        cur = qk_ref[s]
        frame[s, ML_CHUNK:2 * ML_CHUNK, :] = cur
        both = frame[s]
        acc = cb_ref[...] + cur.astype(F32) * cw_ref[taps - 1:taps, :]
        for j in range(taps - 1):
            acc = acc + _dot(picks[j], both) * cw_ref[j:j + 1, :]
        frame[s, 0:ML_CHUNK, :] = cur
        act[s] = (acc * _sigmoid(acc) * sc_ref[...]).astype(BF16)
        gates[s] = g_ref[s]
        log_f = -(jnp.maximum(-gates[s], 0.0) + jnp.log1p(jnp.exp(-jnp.abs(gates[s]))))
        hi, mid, lo = _split3(log_f)
        bcum[s] = _dot(tri, hi) + _dot(tri, mid) + _dot(tri, lo)
        bcum_t[s] = bcum[s].T
        gates_t[s] = gates[s].T

    qh, kh, vh, c_old, m_prev, b_c, i_c, log_d, m_inter, m = ({} for _ in range(10))
    for p in pairs:
        s, h = p
        qh[p] = act[s][:, head_cols(h)]
        kh[p] = act[s][:, ML_PAD_WIDTH + h * ML_HEAD_PAD:ML_PAD_WIDTH + (h + 1) * ML_HEAD_PAD]
        vh[p] = v_ref[s, :, head_cols(h)]
        c_old[p] = ct[s, h]
        m_prev[p] = m_s[s, h, 0:1, 0:1]
        b_c[p] = bcum[s][:, ML_HEADS + h:ML_HEADS + h + 1]
        b_r = bcum_t[s][ML_HEADS + h:ML_HEADS + h + 1, :]
        i_c[p] = gates[s][:, h:h + 1]
        i_r = gates_t[s][h:h + 1, :]
        log_d[p] = jnp.where(causal, b_c[p] - b_r + i_r, neg_inf)
        m_inter[p] = b_c[p] + m_prev[p]
        m[p] = jnp.maximum(m_inter[p], jnp.max(log_d[p], axis=-1, keepdims=True))

    scores = {p: _dot_nt(qh[p], kh[p]) for p in pairs}
    carried = {p: _dot(qh[p], c_old[p].astype(BF16)) for p in pairs}
    sc = {p: (scores[p] * jnp.exp(log_d[p] - m[p])).astype(BF16) for p in pairs}
    num = {p: _dot(sc[p], vh[p]) + jnp.exp(m_inter[p] - m[p]) * carried[p] for p in pairs}
    for p in pairs:
        s, h = p
        den = num[p][:, ML_HEAD_DIM:ML_HEAD_DIM + 1]
        hid = num[p] / jnp.maximum(jnp.abs(den), jnp.exp(-m[p]))
        out_ref[s, :, head_cols(h)] = (og_ref[s, :, head_cols(h)].astype(F32) * hid).astype(out_ref.dtype)
    for p in pairs:
        s, h = p
        b_last = b_c[p][ML_CHUNK - 1:ML_CHUNK, :]
        m_new = m[p][ML_CHUNK - 1:ML_CHUNK, :]
        w = jnp.exp(b_last - b_c[p] + i_c[p] - m_new)
        decay = jnp.exp(b_last + m_prev[p] - m_new)
        wv = (w * vh[p].astype(F32)).astype(BF16)
        ct[s, h] = decay * c_old[p] + _dot_tn(kh[p], wv)
        m_s[s, h] = jnp.broadcast_to(m_new, m_s.shape[2:])


def _mlstm(ml, og, gates, conv_w, conv_b, qk_scale, batch, seq):
    nc = seq // ML_CHUNK
    group = ML_SEQS_PER_STEP if batch % ML_SEQS_PER_STEP == 0 else 1

    def whole(a):
        return pl.BlockSpec(a.shape, lambda b, c: (0, 0))

    def chunk(width, col):
        return pl.BlockSpec((group, ML_CHUNK, width), lambda b, c: (b, c, col))

    out = pl.pallas_call(
        _mlstm_kernel,
        grid=(batch // group, nc),
        in_specs=[chunk(2 * ML_PAD_WIDTH, 0), chunk(ML_PAD_WIDTH, 2), chunk(ML_PAD_WIDTH, 0), chunk(LANES, 0),
                  whole(conv_w), whole(conv_b), whole(qk_scale)],
        out_specs=chunk(ML_PAD_WIDTH, 0),
        out_shape=jax.ShapeDtypeStruct((batch, seq, ML_PAD_WIDTH), BF16),
        scratch_shapes=[pltpu.VMEM((group, 2 * ML_CHUNK, 2 * ML_PAD_WIDTH), BF16),
                        pltpu.VMEM((group, ML_HEADS, ML_HEAD_PAD, ML_HEAD_PAD), F32),
                        pltpu.VMEM((group, ML_HEADS, 8, LANES), F32)],
        compiler_params=_cparams("parallel", "arbitrary"),
        name="mlstm",
    )(ml.reshape(batch, seq, -1), ml.reshape(batch, seq, -1), og.reshape(batch, seq, -1),
      gates.reshape(batch, seq, -1), conv_w, conv_b, qk_scale)
    return out.reshape(batch * seq, ML_PAD_WIDTH)


def _mixer_out_kernel(x_ref, o0_ref, l0_ref, o1_ref, l1_ref, o2_ref, l2_ref, sg_ref, hc_ref, gp_ref,
                      sgg_ref, sgb_ref, sgw_ref, sgbias_ref, wa_ref, wb_ref, wc_ref, wo_ref, lng_ref, lnb_ref,
                      xf_ref, xb_ref, unperm, *, alpha):
    tm = x_ref.shape[0]
    slabs = ATT_GROUP_WIDTH // LANES
    att_refs = (l0_ref, l1_ref, l2_ref, o0_ref, o1_ref, o2_ref)

    for slot, ref in enumerate(att_refs):
        dil = ref.shape[0]
        if dil > 1:
            for r in range(dil):
                for c in range(slabs):
                    unperm[slabs * slot + c, pl.ds(r, tm // dil, stride=dil), :] = (
                        ref[r, :, c * LANES:(c + 1) * LANES].astype(F32))

    def token_order(slot, rows):
        ref = att_refs[slot]
        if ref.shape[0] == 1:
            return ref[0, rows, :].astype(F32)
        return jnp.concatenate([unperm[slabs * slot + c, rows, :] for c in range(slabs)], axis=1)

    row = lax.broadcasted_iota(jnp.int32, (SG_CHUNK, SG_CHUNK), 0)
    colm = lax.broadcasted_iota(jnp.int32, (SG_CHUNK, SG_CHUNK), 1)
    causal = row >= colm
    group_dim = SG_WIDTH // SG_GROUPS
    sg_w = [jnp.where(causal, sgw_ref[g], 0.0).astype(BF16) for g in range(SG_GROUPS)]

    for h in range(tm // MIXER_SUB_ROWS):
        rows = slice(h * MIXER_SUB_ROWS, (h + 1) * MIXER_SUB_ROWS)

        l0, l1, l2 = token_order(0, rows), token_order(1, rows), token_order(2, rows)
        lmax = jnp.maximum(jnp.maximum(l0, l1), l2)
        e0, e1, e2 = jnp.exp(l0 - lmax), jnp.exp(l1 - lmax), jnp.exp(l2 - lmax)
        inv = 1.0 / (e0 + e1 + e2)
        y_a = ((e0 * inv) * token_order(3, rows) + (e1 * inv) * token_order(4, rows)
               + (e2 * inv) * token_order(5, rows))

        u = sg_ref[rows, 0:SG_WIDTH].astype(F32)
        v = sg_ref[rows, SG_WIDTH:2 * SG_WIDTH].astype(F32)
        v = _layer_norm(v, sgg_ref[...], sgb_ref[...]).astype(BF16)
        chunks = []
        for c in range(MIXER_SUB_ROWS // SG_CHUNK):
            crows = slice(c * SG_CHUNK, (c + 1) * SG_CHUNK)
            parts = [_dot(sg_w[g], v[crows, g * group_dim:(g + 1) * group_dim]) + sgbias_ref[:, g:g + 1]
                     for g in range(SG_GROUPS)]
            chunks.append(jnp.concatenate(parts, axis=1))
        y_b = u * jnp.concatenate(chunks, axis=0)

        def gate(i):
            return gp_ref[rows, i * D_MODEL:(i + 1) * D_MODEL].astype(F32)

        z = (gate(0) * _dot(y_a.astype(BF16), wa_ref[...])
             + gate(1) * _dot(y_b.astype(BF16), wb_ref[...])
             + gate(2) * _dot(hc_ref[rows, :], wc_ref[...]))
        mix = _dot(z.astype(BF16), wo_ref[...])
        out = _layer_norm(alpha * x_ref[rows, :] + mix, lng_ref[...], lnb_ref[...])
        xf_ref[rows, :] = out
        xb_ref[rows, :] = out.astype(BF16)


def _mixer_out(x, att_outs, sg, hc, gpre, p, alpha, seq, tm=512):
    n = x.shape[0]
    tiles_per_seq = seq // tm

    def rows(width):
        return pl.BlockSpec((tm, width), lambda i: (i, 0))

    def whole(a):
        return pl.BlockSpec(a.shape, lambda i: (0,) * a.ndim)

    def residue_major(dil):
        return pl.BlockSpec((None, dil, tm // dil, ATT_GROUP_WIDTH),
                            lambda i: (i // tiles_per_seq, 0, i % tiles_per_seq, 0))

    att_specs, att_args = [], []
    for (o, lse), (_, dil) in zip(att_outs, ATT_GROUPS):
        att_specs += [residue_major(dil)] * 2
        att_args += [o, lse]
    weights = [p["sg_ln_g"], p["sg_ln_b"], p["sg_w"], p["sg_bias_t"], p["w_br_a"], p["w_br_b"], p["w_br_c"],
               p["w_out"], p["ln_g0"], p["ln_b0"]]
    return pl.pallas_call(
        functools.partial(_mixer_out_kernel, alpha=alpha),
        grid=(n // tm,),
        in_specs=[rows(D_MODEL)] + att_specs
                 + [rows(2 * SG_WIDTH), rows(ML_PAD_WIDTH), rows(N_BRANCH * D_MODEL)]
                 + [whole(w) for w in weights],
        out_specs=[rows(D_MODEL), rows(D_MODEL)],
        out_shape=[jax.ShapeDtypeStruct((n, D_MODEL), F32), jax.ShapeDtypeStruct((n, D_MODEL), BF16)],
        scratch_shapes=[pltpu.VMEM((2 * len(ATT_GROUPS) * ATT_GROUP_WIDTH // LANES, tm, LANES), F32)],
        compiler_params=_cparams("parallel"),
        name="mixer_out",
    )(x, *att_args, sg, hc, gpre, *weights)


def _swiglu_tile(x, w1_ref, w3_ref, w2_ref, prod, tf):
    for c in range(D_FF // tf):
        cols = slice(c * tf, (c + 1) * tf)
        h1 = _dot(x, w1_ref[:, cols])
        h3 = _dot(x, w3_ref[:, cols])
        prod[:, cols] = (h1 * _sigmoid(h1) * h3).astype(BF16)
    return _dot(prod[...], w2_ref[...])


def _ffn_kernel(xf_ref, xb_ref, w1_ref, w3_ref, w2_ref, lng_ref, lnb_ref, *refs, alpha, tf, riders):
    rin, (of_ref, ob_ref), rout, prod = refs[:riders], refs[riders:riders + 2], refs[riders + 2:-1], refs[-1]
    f = _swiglu_tile(xb_ref[...], w1_ref, w3_ref, w2_ref, prod, tf)
    out = _layer_norm(alpha * xf_ref[...] + f, lng_ref[...], lnb_ref[...])
    of_ref[...] = out
    ob_ref[...] = out.astype(BF16)
    _cast_riders(rin, rout)


def _ffn(xf, xb, w1, w3, w2, ln_g, ln_b, alpha, riders=(), tm=512, tf=256):
    n = xf.shape[0]

    def resident(shape):
        return pl.BlockSpec(shape, lambda i: (0, 0), pipeline_mode=pl.Buffered(1))

    rider_specs, rider_shapes = _rider_specs(riders, n // tm)
    return pl.pallas_call(
        functools.partial(_ffn_kernel, alpha=alpha, tf=tf, riders=len(riders)),
        grid=(n // tm,),
        in_specs=[pl.BlockSpec((tm, D_MODEL), lambda i: (i, 0)),
                  pl.BlockSpec((tm, D_MODEL), lambda i: (i, 0)),
                  resident((D_MODEL, D_FF)), resident((D_MODEL, D_FF)), resident((D_FF, D_MODEL)),
                  pl.BlockSpec((1, D_MODEL), lambda i: (0, 0)),
                  pl.BlockSpec((1, D_MODEL), lambda i: (0, 0))] + rider_specs,
        out_specs=[pl.BlockSpec((tm, D_MODEL), lambda i: (i, 0))] * 2 + rider_specs,
        out_shape=[jax.ShapeDtypeStruct((n, D_MODEL), F32), jax.ShapeDtypeStruct((n, D_MODEL), BF16)]
                  + rider_shapes,
        scratch_shapes=[pltpu.VMEM((tm, D_FF), BF16)],
        compiler_params=_cparams("arbitrary"),
        name="ffn",
    )(xf, xb, w1, w3, w2, ln_g, ln_b, *riders)


def _router_kernel(x_ref, w_ref, b_ref, info_ref):
    xh, xm, _ = _split3(x_ref[...])
    wh, wm, _ = _split3(w_ref[...])
    logits = (_dot(xh, wh) + (_dot(xh, wm) + _dot(xm, wh))) + b_ref[...]
    lane = lax.broadcasted_iota(jnp.int32, logits.shape, 1)
    neg_inf = jnp.float32(-jnp.inf)
    logits = jnp.where(lane < N_EXPERTS, logits, neg_inf)
    v1 = jnp.max(logits, axis=-1, keepdims=True)
    i1 = jnp.min(jnp.where(logits == v1, lane, LANES), axis=-1, keepdims=True)
    rest = jnp.where(lane == i1, neg_inf, logits)
    v2 = jnp.max(rest, axis=-1, keepdims=True)
    i2 = jnp.min(jnp.where(rest == v2, lane, LANES), axis=-1, keepdims=True)
    e = jnp.exp(v2 - v1)
    inv = 1.0 / (1.0 + e)
    info_ref[...] = jnp.where(lane == 0, i1.astype(F32),
                              jnp.where(lane == 1, i2.astype(F32),
                                        jnp.where(lane == 2, inv, jnp.where(lane == 3, e * inv, 0.0))))


def _router(xf, router_w, router_b, tm=1024):
    n = xf.shape[0]
    w = jnp.pad(router_w, ((0, 0), (0, LANES - N_EXPERTS)))
    b = jnp.pad(router_b, (0, LANES - N_EXPERTS))[None, :]
    return pl.pallas_call(
        _router_kernel,
        grid=(n // tm,),
        in_specs=[pl.BlockSpec((tm, D_MODEL), lambda i: (i, 0)),
                  pl.BlockSpec((D_MODEL, LANES), lambda i: (0, 0)),
                  pl.BlockSpec((1, LANES), lambda i: (0, 0))],
        out_specs=pl.BlockSpec((tm, LANES), lambda i: (i, 0)),
        out_shape=jax.ShapeDtypeStruct((n, LANES), F32),
        compiler_params=_cparams("parallel"),
        name="router",
    )(xf, w, b)


def _routing_plan(info, n):
    tile = EXPERT_TILE
    n_tiles = TOP_K * n // tile + N_EXPERTS
    expert = info[:, 0:TOP_K].astype(jnp.int32).reshape(-1)
    onehot = (expert[:, None] == jnp.arange(N_EXPERTS, dtype=jnp.int32)[None, :]).astype(jnp.int32)
    running = jnp.cumsum(onehot, axis=0)
    rank = jnp.sum(onehot * running, axis=1) - 1
    counts = running[-1]
    padded = ((counts + tile - 1) // tile) * tile
    ends = jnp.cumsum(padded)
    starts = ends - padded
    dest = jnp.sum(onehot * starts[None, :], axis=1) + rank
    used = ends[-1] // tile
    tile_start = jnp.arange(n_tiles, dtype=jnp.int32) * tile
    tile_expert = jnp.sum((tile_start[:, None] >= ends[None, :]).astype(jnp.int32), axis=1)
    last_expert = jnp.sum((tile_start[used - 1] >= ends).astype(jnp.int32))
    tile_expert = jnp.minimum(tile_expert, last_expert)
    plan = dict(dest=dest, tile_expert=tile_expert, used=used.reshape(1),
                pad_start=starts + counts, pad_count=padded - counts)
    return {k: v.astype(jnp.int32) for k, v in plan.items()}, n_tiles


def _rows_wait(src, dst, rows, sem):
    pltpu.make_async_copy(src.at[pl.ds(0, rows)], dst.at[pl.ds(0, rows)], sem).wait()


ROW_SUBLANES = D_MODEL // LANES


def _tile_copy(src, src_row, dst, dst_row, sem):
    s0 = pl.multiple_of(src_row * ROW_SUBLANES, ROW_SUBLANES)
    d0 = pl.multiple_of(dst_row * ROW_SUBLANES, ROW_SUBLANES)
    return pltpu.make_async_copy(src.at[pl.ds(s0, ROW_SUBLANES)], dst.at[pl.ds(d0, ROW_SUBLANES)], sem)


def _to_row_tiles(ref, x):
    m = x.shape[0]
    for c in range(ROW_SUBLANES):
        ref[pl.ds(c, m, stride=ROW_SUBLANES), :] = x[:, c * LANES:(c + 1) * LANES]


def _from_row_tiles(ref):
    m = ref.shape[0] // ROW_SUBLANES
    return jnp.concatenate([ref[pl.ds(c, m, stride=ROW_SUBLANES), :] for c in range(ROW_SUBLANES)], axis=1)


DISPATCH_SLOTS = 3


def _dispatch_kernel(pad_start_ref, pad_count_ref, used_ref, dest_ref, x_ref, xs_ref, buf, tiles, zero_row,
                     load_sem, row_sem, pad_sem, *, tm):
    i = pl.program_id(0)

    @pl.when(i == 0)
    def _():
        zero_row[...] = jnp.zeros(zero_row.shape, F32)
        for e in range(N_EXPERTS):
            def fill(j, carry, e=e):
                _tile_copy(zero_row, 0, xs_ref, pad_start_ref[e] + j, pad_sem).start()
                return carry
            lax.fori_loop(0, pad_count_ref[e], fill, 0)
        for e in range(N_EXPERTS):
            def drain(j, carry):
                _tile_copy(zero_row, 0, xs_ref, 0, pad_sem).wait()
                return carry
            lax.fori_loop(0, pad_count_ref[e], drain, 0)

        tile_rows = tm * ROW_SUBLANES
        spare = xs_ref.shape[0] // tile_rows - used_ref[0]
        tiles[1] = jnp.zeros(tiles.shape[1:], F32)

        def zero_tile(j):
            row0 = pl.multiple_of((used_ref[0] + j) * tile_rows, tile_rows)
            return pltpu.make_async_copy(tiles.at[1], xs_ref.at[pl.ds(row0, tile_rows)], pad_sem)

        def fill_tile(j, carry):
            zero_tile(j).start()
            return carry

        def drain_tile(j, carry):
            zero_tile(j).wait()
            return carry

        lax.fori_loop(0, spare, fill_tile, 0)
        lax.fori_loop(0, spare, drain_tile, 0)

    steps = pl.num_programs(0)
    slot = i % DISPATCH_SLOTS
    prev_slot = (i + DISPATCH_SLOTS - 1) % DISPATCH_SLOTS
    stage = i % 2

    def load(tile, into):
        return pltpu.make_async_copy(x_ref.at[pl.ds(tile * tm, tm), :], buf.at[into], load_sem.at[into])

    def wait_rows(of_stage):
        for _ in range(TOP_K):
            _rows_wait(tiles.at[of_stage], xs_ref, tm * ROW_SUBLANES, row_sem.at[of_stage])

    @pl.when(i == 0)
    def _():
        load(0, 0).start()

        @pl.when(steps > 1)
        def _():
            load(1, 1).start()

    load(i, slot).wait()
    _to_row_tiles(tiles.at[stage], buf[slot])

    def start(t, carry):
        for c in range(TOP_K):
            _tile_copy(tiles.at[stage], t, xs_ref, dest_ref[0, 0, TOP_K * t + c], row_sem.at[stage]).start(
                priority=c % 2)
        return carry

    lax.fori_loop(0, tm, start, 0, unroll=DMA_ISSUE_UNROLL)

    @pl.when(i > 0)
    def _():
        wait_rows(1 - stage)

    @pl.when(i + 2 < steps)
    def _():
        load(i + 2, prev_slot).start()

    @pl.when(i == steps - 1)
    def _():
        wait_rows(stage)


def _dispatch(xf, dest, pad_start, pad_count, used, n_slots, tm=EXPERT_TILE):
    n = xf.shape[0]
    dest3 = dest.reshape(n // tm, 1, TOP_K * tm)
    grid_spec = pltpu.PrefetchScalarGridSpec(
        num_scalar_prefetch=3,
        grid=(n // tm,),
        in_specs=[pl.BlockSpec((1, 1, TOP_K * tm), lambda i, ps, pc, nu: (i, 0, 0), memory_space=pltpu.SMEM),
                  pl.BlockSpec(memory_space=pl.ANY)],
        out_specs=pl.BlockSpec(memory_space=pl.ANY),
        scratch_shapes=[pltpu.VMEM((DISPATCH_SLOTS, tm, D_MODEL), F32),
                        pltpu.VMEM((2, tm * ROW_SUBLANES, LANES), F32),
                        pltpu.VMEM((ROW_SUBLANES, LANES), F32),
                        pltpu.SemaphoreType.DMA((DISPATCH_SLOTS,)),
                        pltpu.SemaphoreType.DMA((2,)),
                        pltpu.SemaphoreType.DMA],
    )
    return pl.pallas_call(
        functools.partial(_dispatch_kernel, tm=tm),
        grid_spec=grid_spec,
        out_shape=jax.ShapeDtypeStruct((n_slots * ROW_SUBLANES, LANES), F32),
        compiler_params=_cparams("arbitrary"),
        name="moe_dispatch",
    )(pad_start, pad_count, used, dest3, xf)


def _experts_kernel(te_ref, used_ref, xs_ref, w1_ref, w3_ref, w2_ref, ys_ref, prod, *, tf):
    del te_ref
    i = pl.program_id(0)

    @pl.when(i < used_ref[0])
    def _():
        x = _from_row_tiles(xs_ref).astype(BF16)
        _to_row_tiles(ys_ref, _swiglu_tile(x, w1_ref, w3_ref, w2_ref, prod, tf))

    @pl.when(i >= used_ref[0])
    def _():
        ys_ref[...] = jnp.zeros(ys_ref.shape, ys_ref.dtype)


def _experts(xs, tile_expert, used, w1, w3, w2, n_tiles, tf=256):
    tile = EXPERT_TILE
    grid_spec = pltpu.PrefetchScalarGridSpec(
        num_scalar_prefetch=2,
        grid=(n_tiles,),
        in_specs=[pl.BlockSpec((tile * ROW_SUBLANES, LANES), lambda i, te, nu: (jnp.minimum(i, nu[0] - 1), 0)),
                  pl.BlockSpec((None, D_MODEL, D_FF), lambda i, te, nu: (te[i], 0, 0)),
                  pl.BlockSpec((None, D_MODEL, D_FF), lambda i, te, nu: (te[i], 0, 0)),
                  pl.BlockSpec((None, D_FF, D_MODEL), lambda i, te, nu: (te[i], 0, 0))],
        out_specs=pl.BlockSpec((tile * ROW_SUBLANES, LANES), lambda i, te, nu: (i, 0)),
        scratch_shapes=[pltpu.VMEM((tile, D_FF), BF16)],
    )
    return pl.pallas_call(
        functools.partial(_experts_kernel, tf=tf),
        grid_spec=grid_spec,
        out_shape=jax.ShapeDtypeStruct((n_tiles * tile * ROW_SUBLANES, LANES), F32),
        compiler_params=_cparams("arbitrary"),
        name="moe_experts",
    )(tile_expert, used, xs, w1, w3, w2)


def _combine_kernel(dest_ref, next_dest_ref, xf_ref, info_ref, ys_ref, lng_ref, lnb_ref, of_ref, got, sem, *, alpha):
    tm = xf_ref.shape[0]
    i = pl.program_id(0)
    slot = i % 2

    def gather(dests, into):
        def start(t, carry):
            for c in range(TOP_K):
                _tile_copy(ys_ref, dests[0, 0, TOP_K * t + c], got.at[into, c], t, sem.at[into]).start(
                    priority=c % 2)
            return carry
        lax.fori_loop(0, tm, start, 0, unroll=DMA_ISSUE_UNROLL)

    @pl.when(i == 0)
    def _():
        gather(dest_ref, 0)

    @pl.when(i + 1 < pl.num_programs(0))
    def _():
        gather(next_dest_ref, 1 - slot)

    for c in range(TOP_K):
        _rows_wait(ys_ref, got.at[slot, c], tm * ROW_SUBLANES, sem.at[slot])
    info = info_ref[...]
    mix = (info[:, 2:3] * _from_row_tiles(got.at[slot, 0]) + info[:, 3:4] * _from_row_tiles(got.at[slot, 1]))
    of_ref[...] = _layer_norm(alpha * xf_ref[...] + mix, lng_ref[...], lnb_ref[...])


def _combine(xf, info, ys, dest, ln_g, ln_b, alpha, tm=256):
    n = xf.shape[0]
    steps = n // tm
    dest3 = dest.reshape(steps, 1, TOP_K * tm)
    return pl.pallas_call(
        functools.partial(_combine_kernel, alpha=alpha),
        grid=(steps,),
        in_specs=[pl.BlockSpec((1, 1, TOP_K * tm), lambda i: (i, 0, 0), memory_space=pltpu.SMEM),
                  pl.BlockSpec((1, 1, TOP_K * tm), lambda i: (jnp.minimum(i + 1, steps - 1), 0, 0),
                               memory_space=pltpu.SMEM),
                  pl.BlockSpec((tm, D_MODEL), lambda i: (i, 0)),
                  pl.BlockSpec((tm, LANES), lambda i: (i, 0)),
                  pl.BlockSpec(memory_space=pl.ANY),
                  pl.BlockSpec((1, D_MODEL), lambda i: (0, 0)),
                  pl.BlockSpec((1, D_MODEL), lambda i: (0, 0))],
        out_specs=pl.BlockSpec((tm, D_MODEL), lambda i: (i, 0)),
        out_shape=jax.ShapeDtypeStruct((n, D_MODEL), F32),
        scratch_shapes=[pltpu.VMEM((2, TOP_K, tm * ROW_SUBLANES, LANES), F32), pltpu.SemaphoreType.DMA((2,))],
        compiler_params=_cparams("arbitrary"),
        name="moe_combine",
    )(dest3, dest3, xf, info, ys, ln_g, ln_b)


def _moe(xf, router_w, router_b, w1, w3, w2, ln_g, ln_b, alpha):
    n = xf.shape[0]
    info = _router(xf, router_w, router_b)
    plan, n_tiles = _routing_plan(info, n)
    xs = _dispatch(xf, plan["dest"], plan["pad_start"], plan["pad_count"], plan["used"], n_tiles * EXPERT_TILE)
    ys = _experts(xs, plan["tile_expert"], plan["used"], w1, w3, w2, n_tiles)
    return _combine(xf, info, ys, plan["dest"], ln_g, ln_b, alpha)


def _pad_heads(w):
    lead = w.shape[:-1]
    w = w.reshape(*lead, ML_HEADS, ML_HEAD_DIM)
    w = jnp.pad(w, [(0, 0)] * len(lead) + [(0, 0), (0, ML_HEAD_PAD - ML_HEAD_DIM)])
    return w.reshape(*lead, ML_PAD_WIDTH)


def _group_qkv(w, group):
    gw = ATT_GROUP_WIDTH
    return jnp.concatenate([w[..., part * ATT_WIDTH + group * gw:part * ATT_WIDTH + (group + 1) * gw]
                            for part in range(3)], axis=-1)


def _layer_params(layer, w_in, b_in, conv_w, conv_b, sg_ln_g, sg_ln_b, sg_w, sg_b,
                  w_br_a, w_br_b, w_br_c, w_out, ln_g, ln_b):
    w, b = w_in[layer].astype(BF16), b_in[layer]
    o_att, o_sg, o_qk, o_v, o_o, o_i, o_g = 0, 2304, 3840, 5376, 6144, 6912, 6920
    ones_col = jnp.zeros((ML_HEADS, ML_HEAD_PAD), F32).at[:, ML_HEAD_DIM].set(1.0).reshape(ML_PAD_WIDTH)
    w_qk = jnp.concatenate([_pad_heads(w[:, o_qk:o_qk + ML_WIDTH]), _pad_heads(w[:, o_qk + ML_WIDTH:o_v])], axis=1)
    b_qk = jnp.concatenate([_pad_heads(b[o_qk:o_qk + ML_WIDTH]), _pad_heads(b[o_qk + ML_WIDTH:o_v])])
    qk_scale = jnp.concatenate([jnp.ones((ML_PAD_WIDTH,), F32),
                                jnp.full((ML_PAD_WIDTH,), ML_HEAD_DIM ** -0.5, F32)])
    cw, cb = conv_w[layer], conv_b[layer]
    w_att, b_att = w[:, o_att:o_sg], b[o_att:o_sg]
    return dict(
        w_att=[_group_qkv(w_att, g) for g in range(len(ATT_GROUPS))],
        b_att=[_group_qkv(b_att, g)[None, :] for g in range(len(ATT_GROUPS))],
        w_sg=w[:, o_sg:o_qk], b_sg=b[None, o_sg:o_qk],
        w_ml=jnp.concatenate([w_qk, _pad_heads(w[:, o_v:o_o])], axis=1),
        b_ml=jnp.concatenate([b_qk, _pad_heads(b[o_v:o_o]) + ones_col])[None, :],
        qk_scale=qk_scale[None, :],
        w_mlo=_pad_heads(w[:, o_o:o_i]), b_mlo=_pad_heads(b[o_o:o_i])[None, :],
        w_if=jnp.pad(w[:, o_i:o_g], ((0, 0), (0, LANES - 2 * ML_HEADS))),
        b_if=jnp.pad(b[o_i:o_g], (0, LANES - 2 * ML_HEADS))[None, :],
        w_g=w[:, o_g:], b_g=b[None, o_g:],
        conv_w=jnp.concatenate([_pad_heads(cw[:, :ML_WIDTH]), _pad_heads(cw[:, ML_WIDTH:])], axis=1),
        conv_b=jnp.concatenate([_pad_heads(cb[:ML_WIDTH]), _pad_heads(cb[ML_WIDTH:])])[None, :],
        sg_ln_g=sg_ln_g[layer][None, :], sg_ln_b=sg_ln_b[layer][None, :],
        sg_w=sg_w[layer], sg_bias_t=sg_b[layer].T,
        w_br_a=w_br_a[layer].astype(BF16), w_br_b=w_br_b[layer].astype(BF16),
        w_br_c=_pad_heads(w_br_c[layer].T).T.astype(BF16),
        w_out=w_out[layer].astype(BF16),
        ln_g0=ln_g[layer, 0][None, :], ln_b0=ln_b[layer, 0][None, :],
        ln_g1=ln_g[layer, 1][None, :], ln_b1=ln_b[layer, 1][None, :],
    )


def kernel(x, positions, w_in, b_in, conv_w, conv_b, sg_ln_g, sg_ln_b, sg_w, sg_b, w_br_a, w_br_b, w_br_c, w_out,
           ln_g, ln_b, ffn_w1, ffn_w3, ffn_w2, router_w, router_b, moe_w1, moe_w3, moe_w2):
    batch, seq, _ = x.shape
    n = batch * seq
    depth = w_in.shape[0]
    alpha = (2.0 * depth) ** 0.25
    cos, sin = _rope_tables(positions)
    xf = x.reshape(n, D_MODEL)
    xb = xf.astype(BF16)
    for layer in range(depth):
        p = _layer_params(layer, w_in, b_in, conv_w, conv_b, sg_ln_g, sg_ln_b, sg_w, sg_b,
                          w_br_a, w_br_b, w_br_c, w_out, ln_g, ln_b)
        undilated = [g for g, (_, dil) in enumerate(ATT_GROUPS) if dil == 1]
        j = layer // 2
        if layer % 2 == 0:
            riders = [ffn_w1[j], ffn_w3[j], ffn_w2[j]]
        else:
            riders = [moe_w2[j].reshape(N_EXPERTS * D_FF, D_MODEL)]
        sg, ml, og, gates_if, gbr, *rest = _projections(xb, [
            (p["w_sg"], p["b_sg"], BF16, _gelu_tanh),
            (p["w_ml"], p["b_ml"], BF16, None),
            (p["w_mlo"], p["b_mlo"], BF16, _sigmoid),
            (p["w_if"], p["b_if"], F32, None),
            (p["w_g"], p["b_g"], BF16, _sigmoid),
        ] + [(p["w_att"][g], p["b_att"][g], BF16, None) for g in undilated],
            riders=riders)
        qkv_plain, cast = rest[:len(undilated)], rest[len(undilated):]
        att_outs = []
        for g, (_, dil) in enumerate(ATT_GROUPS):
            if dil == 1:
                qkv = qkv_plain[undilated.index(g)].reshape(batch, 1, seq, ATT_PARTS_WIDTH)
            else:
                qkv = _linear_dilated(xb, p["w_att"][g], p["b_att"][g], dil, batch, seq)
            att_outs.append(_attention_group(qkv, cos, sin, g, batch, seq))
        hc = _mlstm(ml, og, gates_if, p["conv_w"], p["conv_b"], p["qk_scale"], batch, seq)
        xf, xb = _mixer_out(xf, att_outs, sg, hc, gbr, p, alpha, seq)
        if layer % 2 == 0:
            next_moe = (layer + 1) // 2 if layer + 1 < depth else None
            riders = [] if next_moe is None else [moe_w1[next_moe].reshape(N_EXPERTS * D_MODEL, D_FF),
                                                  moe_w3[next_moe].reshape(N_EXPERTS * D_MODEL, D_FF)]
            xf, xb, *moe_cast = _ffn(xf, xb, *cast, p["ln_g1"], p["ln_b1"], alpha, riders=riders)
        else:
            if layer == 0:
                moe_cast = [moe_w1[j].astype(BF16), moe_w3[j].astype(BF16)]
            w1, w3 = (m.reshape(N_EXPERTS, D_MODEL, D_FF) for m in moe_cast)
            w2 = cast[0].reshape(N_EXPERTS, D_FF, D_MODEL)
            xf = _moe(xf, router_w[j], router_b[j], w1, w3, w2, p["ln_g1"], p["ln_b1"], alpha)
            xb = xf.astype(BF16)
    return xf.reshape(batch, seq, D_MODEL)
```

```python
import functools

import numpy as np
import jax
import jax.numpy as jnp
from jax import lax
from jax.experimental import pallas as pl
from jax.experimental.pallas import tpu as pltpu

F32 = jnp.float32
BF16 = jnp.bfloat16

D_MODEL = 1024
ATT_GROUPS = ((128, 1), (512, 4), (2048, 16))
ATT_HEADS_PER_GROUP = 4
ATT_HEAD_DIM = 64
ATT_WIDTH = 768
ATT_GROUP_WIDTH = ATT_HEADS_PER_GROUP * ATT_HEAD_DIM
ATT_PARTS_WIDTH = 3 * ATT_GROUP_WIDTH
ROPE_THETA = 10000.0
SG_CHUNK = 128
SG_GROUPS = 6
SG_WIDTH = 768
ML_HEADS = 4
ML_HEAD_DIM = 192
ML_HEAD_PAD = 256
ML_WIDTH = 768
ML_PAD_WIDTH = ML_HEADS * ML_HEAD_PAD
ML_CHUNK = 128
ML_CONV = 4
N_BRANCH = 3
D_FF = 2816
N_EXPERTS = 8
TOP_K = 2
LN_EPS = 1e-5
BLOCK = 128
LANES = 128
VMEM_LIMIT = 56 * 1024 * 1024
EXPERT_TILE = 512
DMA_ISSUE_UNROLL = 8
MIXER_SUB_ROWS = 256


def _cparams(*sem):
    return pltpu.CompilerParams(dimension_semantics=sem, vmem_limit_bytes=VMEM_LIMIT)


def _layer_norm(x, g, b):
    mu = jnp.mean(x, axis=-1, keepdims=True)
    xc = x - mu
    var = jnp.mean(xc * xc, axis=-1, keepdims=True)
    return xc * lax.rsqrt(var + LN_EPS) * g + b


def _gelu_tanh(x):
    return 0.5 * x * (1.0 + jnp.tanh(np.sqrt(2.0 / np.pi) * (x + 0.044715 * (x * x * x))))


def _sigmoid(x):
    return 0.5 * (1.0 + jnp.tanh(0.5 * x))


def _dot(a, b):
    return jnp.dot(a, b, preferred_element_type=F32)


def _dot_nt(a, b):
    return lax.dot_general(a, b, (((1,), (1,)), ((), ())), preferred_element_type=F32)


def _dot_tn(a, b):
    return lax.dot_general(a, b, (((0,), (0,)), ((), ())), preferred_element_type=F32)


def _split3(x):
    hi = x.astype(BF16)
    r1 = x - hi.astype(F32)
    mid = r1.astype(BF16)
    lo = (r1 - mid.astype(F32)).astype(BF16)
    return hi, mid, lo


def _rider_specs(arrays, steps):
    specs, shapes = [], []
    for a in arrays:
        span = 1
        while (a.shape[0] * span) % steps or (a.shape[0] * span // steps) % 16:
            span *= 2
        rows = a.shape[0] * span // steps
        specs.append(pl.BlockSpec((rows, a.shape[1]), lambda i, span=span: (i // span, 0)))
        shapes.append(jax.ShapeDtypeStruct(a.shape, BF16))
    return specs, shapes


def _cast_riders(in_refs, out_refs):
    for i_ref, o_ref in zip(in_refs, out_refs):
        o_ref[...] = i_ref[...].astype(o_ref.dtype)


def _projections_kernel(x_ref, *refs, acts, riders):
    n = len(acts)
    ws, bs, rin = refs[:n], refs[n:2 * n], refs[2 * n:2 * n + riders]
    outs, rout = refs[2 * n + riders:3 * n + riders], refs[3 * n + riders:]
    x = x_ref[...]
    for w_ref, b_ref, o_ref, act in zip(ws, bs, outs, acts):
        res = _dot(x, w_ref[...]) + b_ref[...]
        o_ref[...] = (res if act is None else act(res)).astype(o_ref.dtype)
    _cast_riders(rin, rout)


def _projections(x, heads, riders=(), tm=512):
    m, k = x.shape

    def resident(a):
        return pl.BlockSpec(a.shape, lambda i: (0, 0), pipeline_mode=pl.Buffered(1))

    ws = [h[0] for h in heads]
    bs = [h[1] for h in heads]
    rider_specs, rider_shapes = _rider_specs(riders, m // tm)
    return pl.pallas_call(
        functools.partial(_projections_kernel, acts=tuple(h[3] for h in heads), riders=len(riders)),
        grid=(m // tm,),
        in_specs=([pl.BlockSpec((tm, k), lambda i: (i, 0))] + [resident(w) for w in ws] + [resident(b) for b in bs]
                  + rider_specs),
        out_specs=[pl.BlockSpec((tm, w.shape[1]), lambda i: (i, 0)) for w in ws] + rider_specs,
        out_shape=[jax.ShapeDtypeStruct((m, h[0].shape[1]), h[2]) for h in heads] + rider_shapes,
        compiler_params=_cparams("arbitrary"),
        name="projections",
    )(x, *ws, *bs, *riders)


def _linear_dilated_kernel(x_ref, w_ref, b_ref, o_ref, acc, *, dil):
    res = _dot(x_ref[...], w_ref[...]) + b_ref[...]
    slabs, rows, _ = acc.shape
    for c in range(slabs):
        acc[c] = res[:, c * LANES:(c + 1) * LANES]
    sub = rows // dil
    for r in range(dil):
        for c in range(slabs):
            o_ref[r, :, c * LANES:(c + 1) * LANES] = acc[c, pl.ds(r, sub, stride=dil), :].astype(o_ref.dtype)


def _linear_dilated(x, w, b, dil, batch, seq):
    k = x.shape[1]
    n = w.shape[1]
    tn = ATT_GROUP_WIDTH
    sub = seq // dil
    return pl.pallas_call(
        functools.partial(_linear_dilated_kernel, dil=dil),
        grid=(batch, n // tn),
        in_specs=[pl.BlockSpec((seq, k), lambda i, j: (i, 0)),
                  pl.BlockSpec((k, tn), lambda i, j: (0, j)),
                  pl.BlockSpec((1, tn), lambda i, j: (0, j))],
        out_specs=pl.BlockSpec((None, dil, sub, tn), lambda i, j: (i, 0, 0, j)),
        out_shape=jax.ShapeDtypeStruct((batch, dil, sub, n), BF16),
        scratch_shapes=[pltpu.VMEM((tn // LANES, seq, LANES), F32)],
        compiler_params=_cparams("parallel", "parallel"),
        name=f"linear_dil{dil}",
    )(x, w, b)


def _rope_table_kernel(pos_ref, freq_ref, sign_ref, cos_ref, sin_ref):
    ang = pos_ref[...] * freq_ref[...]
    cos_ref[...] = jnp.cos(ang)
    sin_ref[...] = jnp.sin(ang) * sign_ref[...]


def _rope_tables(positions):
    n = positions.size
    half = ATT_HEAD_DIM // 2
    freqs = ROPE_THETA ** (-jnp.arange(half, dtype=F32) * (2.0 / ATT_HEAD_DIM))
    lane = np.arange(LANES)
    freq_row = freqs[lane % half][None, :]
    sign_row = jnp.asarray(np.where(lane % ATT_HEAD_DIM < half, -1.0, 1.0), F32)[None, :]
    pos = positions.astype(F32).reshape(n, 1)
    tm = 2048
    return pl.pallas_call(
        _rope_table_kernel,
        grid=(n // tm,),
        in_specs=[pl.BlockSpec((tm, 1), lambda i: (i, 0)),
                  pl.BlockSpec((1, LANES), lambda i: (0, 0)),
                  pl.BlockSpec((1, LANES), lambda i: (0, 0))],
        out_specs=[pl.BlockSpec((tm, LANES), lambda i: (i, 0))] * 2,
        out_shape=[jax.ShapeDtypeStruct((n, LANES), F32)] * 2,
        compiler_params=_cparams("parallel"),
        name="rope_tables",
    )(pos, freq_row, sign_row)


def _attn_kernel(q_ref, k_ref, v_ref, cos_ref, sin_ref, o_ref, lse_ref, qs, ks, vs, *, sub, dil):
    nb = sub // BLOCK
    res = pl.program_id(1)
    lane = lax.broadcasted_iota(jnp.int32, (BLOCK, LANES), 1)
    first_half = (lane % ATT_HEAD_DIM) < (ATT_HEAD_DIM // 2)
    low_head = lane < ATT_HEAD_DIM

    def rope(x, c, s):
        partner = jnp.where(first_half, pltpu.roll(x, LANES - 32, 1), pltpu.roll(x, 32, 1))
        return x * c + partner * s

    ks[0:BLOCK, :] = jnp.zeros((BLOCK, ATT_GROUP_WIDTH), BF16)
    vs[0:BLOCK, :] = jnp.zeros((BLOCK, ATT_GROUP_WIDTH), BF16)

    def prep(n, carry):
        r0 = pl.multiple_of(n * BLOCK, BLOCK)
        if dil == 1:
            c = cos_ref[pl.ds(r0, BLOCK), :]
            s = sin_ref[pl.ds(r0, BLOCK), :]
        else:
            c = cos_ref[pl.ds(r0 * dil + res, BLOCK, stride=dil), :]
            s = sin_ref[pl.ds(r0 * dil + res, BLOCK, stride=dil), :]
        for pair in range(2):
            cols = slice(pair * LANES, (pair + 1) * LANES)
            q = rope(q_ref[pl.ds(r0, BLOCK), cols].astype(F32), c, s) * (ATT_HEAD_DIM ** -0.5)
            qs[pl.ds(r0, BLOCK), (2 * pair) * LANES:(2 * pair + 1) * LANES] = jnp.where(low_head, q, 0.0).astype(BF16)
            qs[pl.ds(r0, BLOCK), (2 * pair + 1) * LANES:(2 * pair + 2) * LANES] = jnp.where(low_head, 0.0, q).astype(BF16)
            k = rope(k_ref[pl.ds(r0, BLOCK), cols].astype(F32), c, s)
            ks[pl.ds(r0 + BLOCK, BLOCK), cols] = k.astype(BF16)
        vs[pl.ds(r0 + BLOCK, BLOCK), :] = v_ref[pl.ds(r0, BLOCK), :]
        return carry

    lax.fori_loop(0, nb, prep, 0)

    qi = lax.broadcasted_iota(jnp.int32, (BLOCK, 2 * BLOCK), 0)
    ki = lax.broadcasted_iota(jnp.int32, (BLOCK, 2 * BLOCK), 1)
    dist = qi + BLOCK - ki
    band = (dist >= 0) & (dist <= BLOCK)
    neg_inf = jnp.float32(-jnp.inf)
    bias_rest = jnp.where(band, 0.0, neg_inf)
    bias_first = jnp.where(band & (ki >= BLOCK), 0.0, neg_inf)

    def block(n, bias):
        r0 = pl.multiple_of(n * BLOCK, BLOCK)
        heads = range(ATT_HEADS_PER_GROUP)
        pair_cols = [slice(pair * LANES, (pair + 1) * LANES) for pair in range(2)]
        kw = [ks[pl.ds(r0, 2 * BLOCK), c] for c in pair_cols]
        vw = [vs[pl.ds(r0, 2 * BLOCK), c] for c in pair_cols]
        s = [_dot_nt(qs[pl.ds(r0, BLOCK), hd * LANES:(hd + 1) * LANES], kw[hd // 2]) + bias for hd in heads]
        m = [jnp.max(s[hd], axis=-1, keepdims=True) for hd in heads]
        p = [jnp.exp(s[hd] - m[hd]) for hd in heads]
        den = [jnp.sum(p[hd], axis=-1, keepdims=True) for hd in heads]
        o = [_dot(p[hd].astype(BF16), vw[hd // 2]) * (1.0 / den[hd]) for hd in heads]
        lse = [m[hd] + jnp.log(den[hd]) for hd in heads]
        for pair in range(2):
            lo, hi = 2 * pair, 2 * pair + 1
            o_ref[pl.ds(r0, BLOCK), pair_cols[pair]] = jnp.where(low_head, o[lo], o[hi]).astype(o_ref.dtype)
            lse_ref[pl.ds(r0, BLOCK), pair_cols[pair]] = jnp.where(low_head, lse[lo], lse[hi])

    block(0, bias_first)
    if nb <= 2:
        for n in range(1, nb):
            block(n, bias_rest)
        return

    def body(n, carry):
        block(n, bias_rest)
        return carry

    lax.fori_loop(1, nb, body, 0, unroll=4)


def _attention_group(qkv, cos, sin, group, batch, seq):
    window, dil = ATT_GROUPS[group]
    assert window // dil == BLOCK
    sub = seq // dil
    assert sub % BLOCK == 0

    def part(p):
        return pl.BlockSpec((None, None, sub, ATT_GROUP_WIDTH), lambda b, r: (b, r, 0, p))

    tab = pl.BlockSpec((seq, LANES), lambda b, r: (b, 0))
    out = pl.BlockSpec((None, None, sub, ATT_GROUP_WIDTH), lambda b, r: (b, r, 0, 0))
    return pl.pallas_call(
        functools.partial(_attn_kernel, sub=sub, dil=dil),
        grid=(batch, dil),
        in_specs=[part(0), part(1), part(2), tab, tab],
        out_specs=[out, out],
        out_shape=[jax.ShapeDtypeStruct((batch, dil, sub, ATT_GROUP_WIDTH), BF16),
                   jax.ShapeDtypeStruct((batch, dil, sub, ATT_GROUP_WIDTH), F32)],
        scratch_shapes=[pltpu.VMEM((sub, ATT_HEADS_PER_GROUP * LANES), BF16),
                        pltpu.VMEM((sub + BLOCK, ATT_GROUP_WIDTH), BF16),
                        pltpu.VMEM((sub + BLOCK, ATT_GROUP_WIDTH), BF16)],
        compiler_params=_cparams("parallel", "parallel"),
        name=f"attention_g{group}",
    )(qkv, qkv, qkv, cos, sin)


ML_SEQS_PER_STEP = 4


def _mlstm_kernel(qk_ref, v_ref, og_ref, g_ref, cw_ref, cb_ref, sc_ref, out_ref, frame, ct, m_s):
    taps = ML_CONV

    @pl.when(pl.program_id(1) == 0)
    def _():
        frame[:, 0:ML_CHUNK, :] = jnp.zeros((frame.shape[0], ML_CHUNK, 2 * ML_PAD_WIDTH), BF16)
        ct[...] = jnp.zeros(ct.shape, F32)
        m_s[...] = jnp.zeros(m_s.shape, F32)

    out_row = lax.broadcasted_iota(jnp.int32, (ML_CHUNK, 2 * ML_CHUNK), 0)
    src_row = lax.broadcasted_iota(jnp.int32, (ML_CHUNK, 2 * ML_CHUNK), 1)
    picks = [jnp.where(src_row == out_row + (ML_CHUNK - (taps - 1 - j)), 1.0, 0.0).astype(BF16)
             for j in range(taps - 1)]
    row = lax.broadcasted_iota(jnp.int32, (ML_CHUNK, ML_CHUNK), 0)
    colm = lax.broadcasted_iota(jnp.int32, (ML_CHUNK, ML_CHUNK), 1)
    causal = row >= colm
    tri = jnp.where(causal, 1.0, 0.0).astype(BF16)
    neg_inf = jnp.float32(-jnp.inf)

    seqs = range(qk_ref.shape[0])
    pairs = [(s, h) for s in seqs for h in range(ML_HEADS)]

    def head_cols(h):
        return slice(h * ML_HEAD_PAD, (h + 1) * ML_HEAD_PAD)

    act, gates, bcum, bcum_t, gates_t = {}, {}, {}, {}, {}
    for s in seqs:
        cur = qk_ref[s]
        frame[s, ML_CHUNK:2 * ML_CHUNK, :] = cur
        both = frame[s]
        acc = cb_ref[...] + cur.astype(F32) * cw_ref[taps - 1:taps, :]
        for j in range(taps - 1):
            acc = acc + _dot(picks[j], both) * cw_ref[j:j + 1, :]
        frame[s, 0:ML_CHUNK, :] = cur
        act[s] = (acc * _sigmoid(acc) * sc_ref[...]).astype(BF16)
        gates[s] = g_ref[s]
        log_f = -(jnp.maximum(-gates[s], 0.0) + jnp.log1p(jnp.exp(-jnp.abs(gates[s]))))
        hi, mid, lo = _split3(log_f)
        bcum[s] = _dot(tri, hi) + _dot(tri, mid) + _dot(tri, lo)
        bcum_t[s] = bcum[s].T
        gates_t[s] = gates[s].T

    qh, kh, vh, c_old, m_prev, b_c, i_c, log_d, m_inter, m = ({} for _ in range(10))
    for p in pairs:
        s, h = p
        qh[p] = act[s][:, head_cols(h)]
        kh[p] = act[s][:, ML_PAD_WIDTH + h * ML_HEAD_PAD:ML_PAD_WIDTH + (h + 1) * ML_HEAD_PAD]
        vh[p] = v_ref[s, :, head_cols(h)]
        c_old[p] = ct[s, h]
        m_prev[p] = m_s[s, h, 0:1, 0:1]
        b_c[p] = bcum[s][:, ML_HEADS + h:ML_HEADS + h + 1]
        b_r = bcum_t[s][ML_HEADS + h:ML_HEADS + h + 1, :]
        i_c[p] = gates[s][:, h:h + 1]
        i_r = gates_t[s][h:h + 1, :]
        log_d[p] = jnp.where(causal, b_c[p] - b_r + i_r, neg_inf)
        m_inter[p] = b_c[p] + m_prev[p]
        m[p] = jnp.maximum(m_inter[p], jnp.max(log_d[p], axis=-1, keepdims=True))

    scores = {p: _dot_nt(qh[p], kh[p]) for p in pairs}
    carried = {p: _dot(qh[p], c_old[p].astype(BF16)) for p in pairs}
    sc = {p: (scores[p] * jnp.exp(log_d[p] - m[p])).astype(BF16) for p in pairs}
    num = {p: _dot(sc[p], vh[p]) + jnp.exp(m_inter[p] - m[p]) * carried[p] for p in pairs}
    for p in pairs:
        s, h = p
        den = num[p][:, ML_HEAD_DIM:ML_HEAD_DIM + 1]
        hid = num[p] / jnp.maximum(jnp.abs(den), jnp.exp(-m[p]))
        out_ref[s, :, head_cols(h)] = (og_ref[s, :, head_cols(h)].astype(F32) * hid).astype(out_ref.dtype)
    for p in pairs:
        s, h = p
        b_last = b_c[p][ML_CHUNK - 1:ML_CHUNK, :]
        m_new = m[p][ML_CHUNK - 1:ML_CHUNK, :]
        w = jnp.exp(b_last - b_c[p] + i_c[p] - m_new)
        decay = jnp.exp(b_last + m_prev[p] - m_new)
        wv = (w * vh[p].astype(F32)).astype(BF16)
        ct[s, h] = decay * c_old[p] + _dot_tn(kh[p], wv)
        m_s[s, h] = jnp.broadcast_to(m_new, m_s.shape[2:])


def _mlstm(ml, og, gates, conv_w, conv_b, qk_scale, batch, seq):
    nc = seq // ML_CHUNK
    group = ML_SEQS_PER_STEP if batch % ML_SEQS_PER_STEP == 0 else 1

    def whole(a):
        return pl.BlockSpec(a.shape, lambda b, c: (0, 0))

    def chunk(width, col):
        return pl.BlockSpec((group, ML_CHUNK, width), lambda b, c: (b, c, col))

    out = pl.pallas_call(
        _mlstm_kernel,
        grid=(batch // group, nc),
        in_specs=[chunk(2 * ML_PAD_WIDTH, 0), chunk(ML_PAD_WIDTH, 2), chunk(ML_PAD_WIDTH, 0), chunk(LANES, 0),
                  whole(conv_w), whole(conv_b), whole(qk_scale)],
        out_specs=chunk(ML_PAD_WIDTH, 0),
        out_shape=jax.ShapeDtypeStruct((batch, seq, ML_PAD_WIDTH), BF16),
        scratch_shapes=[pltpu.VMEM((group, 2 * ML_CHUNK, 2 * ML_PAD_WIDTH), BF16),
                        pltpu.VMEM((group, ML_HEADS, ML_HEAD_PAD, ML_HEAD_PAD), F32),
                        pltpu.VMEM((group, ML_HEADS, 8, LANES), F32)],
        compiler_params=_cparams("parallel", "arbitrary"),
        name="mlstm",
    )(ml.reshape(batch, seq, -1), ml.reshape(batch, seq, -1), og.reshape(batch, seq, -1),
      gates.reshape(batch, seq, -1), conv_w, conv_b, qk_scale)
    return out.reshape(batch * seq, ML_PAD_WIDTH)


def _mixer_out_kernel(x_ref, o0_ref, l0_ref, o1_ref, l1_ref, o2_ref, l2_ref, sg_ref, hc_ref, gp_ref,
                      sgg_ref, sgb_ref, sgw_ref, sgbias_ref, wa_ref, wb_ref, wc_ref, wo_ref, lng_ref, lnb_ref,
                      xf_ref, xb_ref, unperm, *, alpha):
    tm = x_ref.shape[0]
    slabs = ATT_GROUP_WIDTH // LANES
    att_refs = (l0_ref, l1_ref, l2_ref, o0_ref, o1_ref, o2_ref)

    for slot, ref in enumerate(att_refs):
        dil = ref.shape[0]
        if dil > 1:
            for r in range(dil):
                for c in range(slabs):
                    unperm[slabs * slot + c, pl.ds(r, tm // dil, stride=dil), :] = (
                        ref[r, :, c * LANES:(c + 1) * LANES].astype(F32))

    def token_order(slot, rows):
        ref = att_refs[slot]
        if ref.shape[0] == 1:
            return ref[0, rows, :].astype(F32)
        return jnp.concatenate([unperm[slabs * slot + c, rows, :] for c in range(slabs)], axis=1)

    row = lax.broadcasted_iota(jnp.int32, (SG_CHUNK, SG_CHUNK), 0)
    colm = lax.broadcasted_iota(jnp.int32, (SG_CHUNK, SG_CHUNK), 1)
    causal = row >= colm
    group_dim = SG_WIDTH // SG_GROUPS
    sg_w = [jnp.where(causal, sgw_ref[g], 0.0).astype(BF16) for g in range(SG_GROUPS)]

    for h in range(tm // MIXER_SUB_ROWS):
        rows = slice(h * MIXER_SUB_ROWS, (h + 1) * MIXER_SUB_ROWS)

        l0, l1, l2 = token_order(0, rows), token_order(1, rows), token_order(2, rows)
        lmax = jnp.maximum(jnp.maximum(l0, l1), l2)
        e0, e1, e2 = jnp.exp(l0 - lmax), jnp.exp(l1 - lmax), jnp.exp(l2 - lmax)
        inv = 1.0 / (e0 + e1 + e2)
        y_a = ((e0 * inv) * token_order(3, rows) + (e1 * inv) * token_order(4, rows)
               + (e2 * inv) * token_order(5, rows))

        u = sg_ref[rows, 0:SG_WIDTH].astype(F32)
        v = sg_ref[rows, SG_WIDTH:2 * SG_WIDTH].astype(F32)
        v = _layer_norm(v, sgg_ref[...], sgb_ref[...]).astype(BF16)
        chunks = []
        for c in range(MIXER_SUB_ROWS // SG_CHUNK):
            crows = slice(c * SG_CHUNK, (c + 1) * SG_CHUNK)
            parts = [_dot(sg_w[g], v[crows, g * group_dim:(g + 1) * group_dim]) + sgbias_ref[:, g:g + 1]
                     for g in range(SG_GROUPS)]
            chunks.append(jnp.concatenate(parts, axis=1))
        y_b = u * jnp.concatenate(chunks, axis=0)

        def gate(i):
            return gp_ref[rows, i * D_MODEL:(i + 1) * D_MODEL].astype(F32)

        z = (gate(0) * _dot(y_a.astype(BF16), wa_ref[...])
             + gate(1) * _dot(y_b.astype(BF16), wb_ref[...])
             + gate(2) * _dot(hc_ref[rows, :], wc_ref[...]))
        mix = _dot(z.astype(BF16), wo_ref[...])
        out = _layer_norm(alpha * x_ref[rows, :] + mix, lng_ref[...], lnb_ref[...])
        xf_ref[rows, :] = out
        xb_ref[rows, :] = out.astype(BF16)


def _mixer_out(x, att_outs, sg, hc, gpre, p, alpha, seq, tm=512):
    n = x.shape[0]
    tiles_per_seq = seq // tm

    def rows(width):
        return pl.BlockSpec((tm, width), lambda i: (i, 0))

    def whole(a):
        return pl.BlockSpec(a.shape, lambda i: (0,) * a.ndim)

    def residue_major(dil):
        return pl.BlockSpec((None, dil, tm // dil, ATT_GROUP_WIDTH),
                            lambda i: (i // tiles_per_seq, 0, i % tiles_per_seq, 0))

    att_specs, att_args = [], []
    for (o, lse), (_, dil) in zip(att_outs, ATT_GROUPS):
        att_specs += [residue_major(dil)] * 2
        att_args += [o, lse]
    weights = [p["sg_ln_g"], p["sg_ln_b"], p["sg_w"], p["sg_bias_t"], p["w_br_a"], p["w_br_b"], p["w_br_c"],
               p["w_out"], p["ln_g0"], p["ln_b0"]]
    return pl.pallas_call(
        functools.partial(_mixer_out_kernel, alpha=alpha),
        grid=(n // tm,),
        in_specs=[rows(D_MODEL)] + att_specs
                 + [rows(2 * SG_WIDTH), rows(ML_PAD_WIDTH), rows(N_BRANCH * D_MODEL)]
                 + [whole(w) for w in weights],
        out_specs=[rows(D_MODEL), rows(D_MODEL)],
        out_shape=[jax.ShapeDtypeStruct((n, D_MODEL), F32), jax.ShapeDtypeStruct((n, D_MODEL), BF16)],
        scratch_shapes=[pltpu.VMEM((2 * len(ATT_GROUPS) * ATT_GROUP_WIDTH // LANES, tm, LANES), F32)],
        compiler_params=_cparams("parallel"),
        name="mixer_out",
    )(x, *att_args, sg, hc, gpre, *weights)


def _swiglu_tile(x, w1_ref, w3_ref, w2_ref, prod, tf):
    for c in range(D_FF // tf):
        cols = slice(c * tf, (c + 1) * tf)
        h1 = _dot(x, w1_ref[:, cols])
        h3 = _dot(x, w3_ref[:, cols])
        prod[:, cols] = (h1 * _sigmoid(h1) * h3).astype(BF16)
    return _dot(prod[...], w2_ref[...])


def _ffn_kernel(xf_ref, xb_ref, w1_ref, w3_ref, w2_ref, lng_ref, lnb_ref, *refs, alpha, tf, riders):
    rin, (of_ref, ob_ref), rout, prod = refs[:riders], refs[riders:riders + 2], refs[riders + 2:-1], refs[-1]
    f = _swiglu_tile(xb_ref[...], w1_ref, w3_ref, w2_ref, prod, tf)
    out = _layer_norm(alpha * xf_ref[...] + f, lng_ref[...], lnb_ref[...])
    of_ref[...] = out
    ob_ref[...] = out.astype(BF16)
    _cast_riders(rin, rout)


def _ffn(xf, xb, w1, w3, w2, ln_g, ln_b, alpha, riders=(), tm=512, tf=256):
    n = xf.shape[0]

    def resident(shape):
        return pl.BlockSpec(shape, lambda i: (0, 0), pipeline_mode=pl.Buffered(1))

    rider_specs, rider_shapes = _rider_specs(riders, n // tm)
    return pl.pallas_call(
        functools.partial(_ffn_kernel, alpha=alpha, tf=tf, riders=len(riders)),
        grid=(n // tm,),
        in_specs=[pl.BlockSpec((tm, D_MODEL), lambda i: (i, 0)),
                  pl.BlockSpec((tm, D_MODEL), lambda i: (i, 0)),
                  resident((D_MODEL, D_FF)), resident((D_MODEL, D_FF)), resident((D_FF, D_MODEL)),
                  pl.BlockSpec((1, D_MODEL), lambda i: (0, 0)),
                  pl.BlockSpec((1, D_MODEL), lambda i: (0, 0))] + rider_specs,
        out_specs=[pl.BlockSpec((tm, D_MODEL), lambda i: (i, 0))] * 2 + rider_specs,
        out_shape=[jax.ShapeDtypeStruct((n, D_MODEL), F32), jax.ShapeDtypeStruct((n, D_MODEL), BF16)]
                  + rider_shapes,
        scratch_shapes=[pltpu.VMEM((tm, D_FF), BF16)],
        compiler_params=_cparams("arbitrary"),
        name="ffn",
    )(xf, xb, w1, w3, w2, ln_g, ln_b, *riders)


def _router_kernel(x_ref, w_ref, b_ref, info_ref):
    xh, xm, _ = _split3(x_ref[...])
    wh, wm, _ = _split3(w_ref[...])
    logits = (_dot(xh, wh) + (_dot(xh, wm) + _dot(xm, wh))) + b_ref[...]
    lane = lax.broadcasted_iota(jnp.int32, logits.shape, 1)
    neg_inf = jnp.float32(-jnp.inf)
    logits = jnp.where(lane < N_EXPERTS, logits, neg_inf)
    v1 = jnp.max(logits, axis=-1, keepdims=True)
    i1 = jnp.min(jnp.where(logits == v1, lane, LANES), axis=-1, keepdims=True)
    rest = jnp.where(lane == i1, neg_inf, logits)
    v2 = jnp.max(rest, axis=-1, keepdims=True)
    i2 = jnp.min(jnp.where(rest == v2, lane, LANES), axis=-1, keepdims=True)
    e = jnp.exp(v2 - v1)
    inv = 1.0 / (1.0 + e)
    info_ref[...] = jnp.where(lane == 0, i1.astype(F32),
                              jnp.where(lane == 1, i2.astype(F32),
                                        jnp.where(lane == 2, inv, jnp.where(lane == 3, e * inv, 0.0))))


def _router(xf, router_w, router_b, tm=1024):
    n = xf.shape[0]
    w = jnp.pad(router_w, ((0, 0), (0, LANES - N_EXPERTS)))
    b = jnp.pad(router_b, (0, LANES - N_EXPERTS))[None, :]
    return pl.pallas_call(
        _router_kernel,
        grid=(n // tm,),
        in_specs=[pl.BlockSpec((tm, D_MODEL), lambda i: (i, 0)),
                  pl.BlockSpec((D_MODEL, LANES), lambda i: (0, 0)),
                  pl.BlockSpec((1, LANES), lambda i: (0, 0))],
        out_specs=pl.BlockSpec((tm, LANES), lambda i: (i, 0)),
        out_shape=jax.ShapeDtypeStruct((n, LANES), F32),
        compiler_params=_cparams("parallel"),
        name="router",
    )(xf, w, b)


def _routing_plan(info, n):
    tile = EXPERT_TILE
    n_tiles = TOP_K * n // tile + N_EXPERTS
    expert = info[:, 0:TOP_K].astype(jnp.int32).reshape(-1)
    onehot = (expert[:, None] == jnp.arange(N_EXPERTS, dtype=jnp.int32)[None, :]).astype(jnp.int32)
    running = jnp.cumsum(onehot, axis=0)
    rank = jnp.sum(onehot * running, axis=1) - 1
    counts = running[-1]
    padded = ((counts + tile - 1) // tile) * tile
    ends = jnp.cumsum(padded)
    starts = ends - padded
    dest = jnp.sum(onehot * starts[None, :], axis=1) + rank
    used = ends[-1] // tile
    tile_start = jnp.arange(n_tiles, dtype=jnp.int32) * tile
    tile_expert = jnp.sum((tile_start[:, None] >= ends[None, :]).astype(jnp.int32), axis=1)
    last_expert = jnp.sum((tile_start[used - 1] >= ends).astype(jnp.int32))
    tile_expert = jnp.minimum(tile_expert, last_expert)
    plan = dict(dest=dest, tile_expert=tile_expert, used=used.reshape(1),
                pad_start=starts + counts, pad_count=padded - counts)
    return {k: v.astype(jnp.int32) for k, v in plan.items()}, n_tiles


def _rows_wait(src, dst, rows, sem):
    pltpu.make_async_copy(src.at[pl.ds(0, rows)], dst.at[pl.ds(0, rows)], sem).wait()


ROW_SUBLANES = D_MODEL // LANES


def _tile_copy(src, src_row, dst, dst_row, sem):
    s0 = pl.multiple_of(src_row * ROW_SUBLANES, ROW_SUBLANES)
    d0 = pl.multiple_of(dst_row * ROW_SUBLANES, ROW_SUBLANES)
    return pltpu.make_async_copy(src.at[pl.ds(s0, ROW_SUBLANES)], dst.at[pl.ds(d0, ROW_SUBLANES)], sem)


def _to_row_tiles(ref, x):
    m = x.shape[0]
    for c in range(ROW_SUBLANES):
        ref[pl.ds(c, m, stride=ROW_SUBLANES), :] = x[:, c * LANES:(c + 1) * LANES]


def _from_row_tiles(ref):
    m = ref.shape[0] // ROW_SUBLANES
    return jnp.concatenate([ref[pl.ds(c, m, stride=ROW_SUBLANES), :] for c in range(ROW_SUBLANES)], axis=1)


DISPATCH_SLOTS = 3


def _dispatch_kernel(pad_start_ref, pad_count_ref, used_ref, dest_ref, x_ref, xs_ref, buf, tiles, zero_row,
                     load_sem, row_sem, pad_sem, *, tm):
    i = pl.program_id(0)

    @pl.when(i == 0)
    def _():
        zero_row[...] = jnp.zeros(zero_row.shape, F32)
        for e in range(N_EXPERTS):
            def fill(j, carry, e=e):
                _tile_copy(zero_row, 0, xs_ref, pad_start_ref[e] + j, pad_sem).start()
                return carry
            lax.fori_loop(0, pad_count_ref[e], fill, 0)
        for e in range(N_EXPERTS):
            def drain(j, carry):
                _tile_copy(zero_row, 0, xs_ref, 0, pad_sem).wait()
                return carry
            lax.fori_loop(0, pad_count_ref[e], drain, 0)

        tile_rows = tm * ROW_SUBLANES
        spare = xs_ref.shape[0] // tile_rows - used_ref[0]
        tiles[1] = jnp.zeros(tiles.shape[1:], F32)

        def zero_tile(j):
            row0 = pl.multiple_of((used_ref[0] + j) * tile_rows, tile_rows)
            return pltpu.make_async_copy(tiles.at[1], xs_ref.at[pl.ds(row0, tile_rows)], pad_sem)

        def fill_tile(j, carry):
            zero_tile(j).start()
            return carry

        def drain_tile(j, carry):
            zero_tile(j).wait()
            return carry

        lax.fori_loop(0, spare, fill_tile, 0)
        lax.fori_loop(0, spare, drain_tile, 0)

    steps = pl.num_programs(0)
    slot = i % DISPATCH_SLOTS
    prev_slot = (i + DISPATCH_SLOTS - 1) % DISPATCH_SLOTS
    stage = i % 2

    def load(tile, into):
        return pltpu.make_async_copy(x_ref.at[pl.ds(tile * tm, tm), :], buf.at[into], load_sem.at[into])

    def wait_rows(of_stage):
        for _ in range(TOP_K):
            _rows_wait(tiles.at[of_stage], xs_ref, tm * ROW_SUBLANES, row_sem.at[of_stage])

    @pl.when(i == 0)
    def _():
        load(0, 0).start()

        @pl.when(steps > 1)
        def _():
            load(1, 1).start()

    load(i, slot).wait()
    _to_row_tiles(tiles.at[stage], buf[slot])

    def start(t, carry):
        for c in range(TOP_K):
            _tile_copy(tiles.at[stage], t, xs_ref, dest_ref[0, 0, TOP_K * t + c], row_sem.at[stage]).start(
                priority=c % 2)
        return carry

    lax.fori_loop(0, tm, start, 0, unroll=DMA_ISSUE_UNROLL)

    @pl.when(i > 0)
    def _():
        wait_rows(1 - stage)

    @pl.when(i + 2 < steps)
    def _():
        load(i + 2, prev_slot).start()

    @pl.when(i == steps - 1)
    def _():
        wait_rows(stage)


def _dispatch(xf, dest, pad_start, pad_count, used, n_slots, tm=EXPERT_TILE):
    n = xf.shape[0]
    dest3 = dest.reshape(n // tm, 1, TOP_K * tm)
    grid_spec = pltpu.PrefetchScalarGridSpec(
        num_scalar_prefetch=3,
        grid=(n // tm,),
        in_specs=[pl.BlockSpec((1, 1, TOP_K * tm), lambda i, ps, pc, nu: (i, 0, 0), memory_space=pltpu.SMEM),
                  pl.BlockSpec(memory_space=pl.ANY)],
        out_specs=pl.BlockSpec(memory_space=pl.ANY),
        scratch_shapes=[pltpu.VMEM((DISPATCH_SLOTS, tm, D_MODEL), F32),
                        pltpu.VMEM((2, tm * ROW_SUBLANES, LANES), F32),
                        pltpu.VMEM((ROW_SUBLANES, LANES), F32),
                        pltpu.SemaphoreType.DMA((DISPATCH_SLOTS,)),
                        pltpu.SemaphoreType.DMA((2,)),
                        pltpu.SemaphoreType.DMA],
    )
    return pl.pallas_call(
        functools.partial(_dispatch_kernel, tm=tm),
        grid_spec=grid_spec,
        out_shape=jax.ShapeDtypeStruct((n_slots * ROW_SUBLANES, LANES), F32),
        compiler_params=_cparams("arbitrary"),
        name="moe_dispatch",
    )(pad_start, pad_count, used, dest3, xf)


def _experts_kernel(te_ref, used_ref, xs_ref, w1_ref, w3_ref, w2_ref, ys_ref, prod, *, tf):
    del te_ref
    i = pl.program_id(0)

    @pl.when(i < used_ref[0])
    def _():
        x = _from_row_tiles(xs_ref).astype(BF16)
        _to_row_tiles(ys_ref, _swiglu_tile(x, w1_ref, w3_ref, w2_ref, prod, tf))

    @pl.when(i >= used_ref[0])
    def _():
        ys_ref[...] = jnp.zeros(ys_ref.shape, ys_ref.dtype)


def _experts(xs, tile_expert, used, w1, w3, w2, n_tiles, tf=256):
    tile = EXPERT_TILE
    grid_spec = pltpu.PrefetchScalarGridSpec(
        num_scalar_prefetch=2,
        grid=(n_tiles,),
        in_specs=[pl.BlockSpec((tile * ROW_SUBLANES, LANES), lambda i, te, nu: (jnp.minimum(i, nu[0] - 1), 0)),
                  pl.BlockSpec((None, D_MODEL, D_FF), lambda i, te, nu: (te[i], 0, 0)),
                  pl.BlockSpec((None, D_MODEL, D_FF), lambda i, te, nu: (te[i], 0, 0)),
                  pl.BlockSpec((None, D_FF, D_MODEL), lambda i, te, nu: (te[i], 0, 0))],
        out_specs=pl.BlockSpec((tile * ROW_SUBLANES, LANES), lambda i, te, nu: (i, 0)),
        scratch_shapes=[pltpu.VMEM((tile, D_FF), BF16)],
    )
    return pl.pallas_call(
        functools.partial(_experts_kernel, tf=tf),
        grid_spec=grid_spec,
        out_shape=jax.ShapeDtypeStruct((n_tiles * tile * ROW_SUBLANES, LANES), F32),
        compiler_params=_cparams("arbitrary"),
        name="moe_experts",
    )(tile_expert, used, xs, w1, w3, w2)


def _combine_kernel(dest_ref, next_dest_ref, xf_ref, info_ref, ys_ref, lng_ref, lnb_ref, of_ref, got, sem, *, alpha):
    tm = xf_ref.shape[0]
    i = pl.program_id(0)
    slot = i % 2

    def gather(dests, into):
        def start(t, carry):
            for c in range(TOP_K):
                _tile_copy(ys_ref, dests[0, 0, TOP_K * t + c], got.at[into, c], t, sem.at[into]).start(
                    priority=c % 2)
            return carry
        lax.fori_loop(0, tm, start, 0, unroll=DMA_ISSUE_UNROLL)

    @pl.when(i == 0)
    def _():
        gather(dest_ref, 0)

    @pl.when(i + 1 < pl.num_programs(0))
    def _():
        gather(next_dest_ref, 1 - slot)

    for c in range(TOP_K):
        _rows_wait(ys_ref, got.at[slot, c], tm * ROW_SUBLANES, sem.at[slot])
    info = info_ref[...]
    mix = (info[:, 2:3] * _from_row_tiles(got.at[slot, 0]) + info[:, 3:4] * _from_row_tiles(got.at[slot, 1]))
    of_ref[...] = _layer_norm(alpha * xf_ref[...] + mix, lng_ref[...], lnb_ref[...])


def _combine(xf, info, ys, dest, ln_g, ln_b, alpha, tm=256):
    n = xf.shape[0]
    steps = n // tm
    dest3 = dest.reshape(steps, 1, TOP_K * tm)
    return pl.pallas_call(
        functools.partial(_combine_kernel, alpha=alpha),
        grid=(steps,),
        in_specs=[pl.BlockSpec((1, 1, TOP_K * tm), lambda i: (i, 0, 0), memory_space=pltpu.SMEM),
                  pl.BlockSpec((1, 1, TOP_K * tm), lambda i: (jnp.minimum(i + 1, steps - 1), 0, 0),
                               memory_space=pltpu.SMEM),
                  pl.BlockSpec((tm, D_MODEL), lambda i: (i, 0)),
                  pl.BlockSpec((tm, LANES), lambda i: (i, 0)),
                  pl.BlockSpec(memory_space=pl.ANY),
                  pl.BlockSpec((1, D_MODEL), lambda i: (0, 0)),
                  pl.BlockSpec((1, D_MODEL), lambda i: (0, 0))],
        out_specs=pl.BlockSpec((tm, D_MODEL), lambda i: (i, 0)),
        out_shape=jax.ShapeDtypeStruct((n, D_MODEL), F32),
        scratch_shapes=[pltpu.VMEM((2, TOP_K, tm * ROW_SUBLANES, LANES), F32), pltpu.SemaphoreType.DMA((2,))],
        compiler_params=_cparams("arbitrary"),
        name="moe_combine",
    )(dest3, dest3, xf, info, ys, ln_g, ln_b)


def _moe(xf, router_w, router_b, w1, w3, w2, ln_g, ln_b, alpha):
    n = xf.shape[0]
    info = _router(xf, router_w, router_b)
    plan, n_tiles = _routing_plan(info, n)
    xs = _dispatch(xf, plan["dest"], plan["pad_start"], plan["pad_count"], plan["used"], n_tiles * EXPERT_TILE)
    ys = _experts(xs, plan["tile_expert"], plan["used"], w1, w3, w2, n_tiles)
    return _combine(xf, info, ys, plan["dest"], ln_g, ln_b, alpha)


def _pad_heads(w):
    lead = w.shape[:-1]
    w = w.reshape(*lead, ML_HEADS, ML_HEAD_DIM)
    w = jnp.pad(w, [(0, 0)] * len(lead) + [(0, 0), (0, ML_HEAD_PAD - ML_HEAD_DIM)])
    return w.reshape(*lead, ML_PAD_WIDTH)


def _group_qkv(w, group):
    gw = ATT_GROUP_WIDTH
    return jnp.concatenate([w[..., part * ATT_WIDTH + group * gw:part * ATT_WIDTH + (group + 1) * gw]
                            for part in range(3)], axis=-1)


def _layer_params(layer, w_in_bf16, b_in, conv_w, conv_b, sg_ln_g, sg_ln_b, sg_w, sg_b,
                  w_br_a, w_br_b, w_br_c, w_out, ln_g, ln_b):
    w, b = w_in_bf16, b_in[layer]
    o_att, o_sg, o_qk, o_v, o_o, o_i, o_g = 0, 2304, 3840, 5376, 6144, 6912, 6920
    ones_col = jnp.zeros((ML_HEADS, ML_HEAD_PAD), F32).at[:, ML_HEAD_DIM].set(1.0).reshape(ML_PAD_WIDTH)
    w_qk = jnp.concatenate([_pad_heads(w[:, o_qk:o_qk + ML_WIDTH]), _pad_heads(w[:, o_qk + ML_WIDTH:o_v])], axis=1)
    b_qk = jnp.concatenate([_pad_heads(b[o_qk:o_qk + ML_WIDTH]), _pad_heads(b[o_qk + ML_WIDTH:o_v])])
    qk_scale = jnp.concatenate([jnp.ones((ML_PAD_WIDTH,), F32),
                                jnp.full((ML_PAD_WIDTH,), ML_HEAD_DIM ** -0.5, F32)])
    cw, cb = conv_w[layer], conv_b[layer]
    w_att, b_att = w[:, o_att:o_sg], b[o_att:o_sg]
    return dict(
        w_att=[_group_qkv(w_att, g) for g in range(len(ATT_GROUPS))],
        b_att=[_group_qkv(b_att, g)[None, :] for g in range(len(ATT_GROUPS))],
        w_sg=w[:, o_sg:o_qk], b_sg=b[None, o_sg:o_qk],
        w_ml=jnp.concatenate([w_qk, _pad_heads(w[:, o_v:o_o])], axis=1),
        b_ml=jnp.concatenate([b_qk, _pad_heads(b[o_v:o_o]) + ones_col])[None, :],
        qk_scale=qk_scale[None, :],
        w_mlo=_pad_heads(w[:, o_o:o_i]), b_mlo=_pad_heads(b[o_o:o_i])[None, :],
        w_if=jnp.pad(w[:, o_i:o_g], ((0, 0), (0, LANES - 2 * ML_HEADS))),
        b_if=jnp.pad(b[o_i:o_g], (0, LANES - 2 * ML_HEADS))[None, :],
        w_g=w[:, o_g:], b_g=b[None, o_g:],
        conv_w=jnp.concatenate([_pad_heads(cw[:, :ML_WIDTH]), _pad_heads(cw[:, ML_WIDTH:])], axis=1),
        conv_b=jnp.concatenate([_pad_heads(cb[:ML_WIDTH]), _pad_heads(cb[ML_WIDTH:])])[None, :],
        sg_ln_g=sg_ln_g[layer][None, :], sg_ln_b=sg_ln_b[layer][None, :],
        sg_w=sg_w[layer], sg_bias_t=sg_b[layer].T,
        w_br_a=w_br_a[layer].astype(BF16), w_br_b=w_br_b[layer].astype(BF16),
        w_br_c=_pad_heads(w_br_c[layer].T).T.astype(BF16),
        w_out=w_out[layer].astype(BF16),
        ln_g0=ln_g[layer, 0][None, :], ln_b0=ln_b[layer, 0][None, :],
        ln_g1=ln_g[layer, 1][None, :], ln_b1=ln_b[layer, 1][None, :],
    )


def kernel(x, positions, w_in, b_in, conv_w, conv_b, sg_ln_g, sg_ln_b, sg_w, sg_b, w_br_a, w_br_b, w_br_c, w_out,
           ln_g, ln_b, ffn_w1, ffn_w3, ffn_w2, router_w, router_b, moe_w1, moe_w3, moe_w2):
    batch, seq, _ = x.shape
    n = batch * seq
    depth = w_in.shape[0]
    alpha = (2.0 * depth) ** 0.25
    cos, sin = _rope_tables(positions)
    w_in_cast = None
    xf = x.reshape(n, D_MODEL)
    xb = xf.astype(BF16)
    for layer in range(depth):
        w_layer = w_in_cast if w_in_cast is not None else w_in[layer].astype(BF16)
        w_in_cast = None
        p = _layer_params(layer, w_layer, b_in, conv_w, conv_b, sg_ln_g, sg_ln_b, sg_w, sg_b,
                          w_br_a, w_br_b, w_br_c, w_out, ln_g, ln_b)
        undilated = [g for g, (_, dil) in enumerate(ATT_GROUPS) if dil == 1]
        j = layer // 2
        if layer % 2 == 0:
            riders = [ffn_w1[j], ffn_w3[j], ffn_w2[j]]
        else:
            riders = [moe_w2[j].reshape(N_EXPERTS * D_FF, D_MODEL)]
        sg, ml, og, gates_if, gbr, *rest = _projections(xb, [
            (p["w_sg"], p["b_sg"], BF16, _gelu_tanh),
            (p["w_ml"], p["b_ml"], BF16, None),
            (p["w_mlo"], p["b_mlo"], BF16, _sigmoid),
            (p["w_if"], p["b_if"], F32, None),
            (p["w_g"], p["b_g"], BF16, _sigmoid),
        ] + [(p["w_att"][g], p["b_att"][g], BF16, None) for g in undilated],
            riders=riders)
        qkv_plain, cast = rest[:len(undilated)], rest[len(undilated):]
        att_outs = []
        for g, (_, dil) in enumerate(ATT_GROUPS):
            if dil == 1:
                qkv = qkv_plain[undilated.index(g)].reshape(batch, 1, seq, ATT_PARTS_WIDTH)
            else:
                qkv = _linear_dilated(xb, p["w_att"][g], p["b_att"][g], dil, batch, seq)
            att_outs.append(_attention_group(qkv, cos, sin, g, batch, seq))
        hc = _mlstm(ml, og, gates_if, p["conv_w"], p["conv_b"], p["qk_scale"], batch, seq)
        xf, xb = _mixer_out(xf, att_outs, sg, hc, gbr, p, alpha, seq)
        if layer % 2 == 0:
            next_moe = (layer + 1) // 2 if layer + 1 < depth else None
            riders = [] if next_moe is None else [w_in[layer + 1],
                                                  moe_w1[next_moe].reshape(N_EXPERTS * D_MODEL, D_FF),
                                                  moe_w3[next_moe].reshape(N_EXPERTS * D_MODEL, D_FF)]
            xf, xb, *riding = _ffn(xf, xb, *cast, p["ln_g1"], p["ln_b1"], alpha, riders=riders)
            if riding:
                w_in_cast, moe_cast = riding[0], riding[1:]
        else:
            if layer == 0:
                moe_cast = [moe_w1[j].astype(BF16), moe_w3[j].astype(BF16)]
            w1, w3 = (m.reshape(N_EXPERTS, D_MODEL, D_FF) for m in moe_cast)
            w2 = cast[0].reshape(N_EXPERTS, D_FF, D_MODEL)
            xf = _moe(xf, router_w[j], router_b[j], w1, w3, w2, p["ln_g1"], p["ln_b1"], alpha)
            xb = xf.astype(BF16)
    return xf.reshape(batch, seq, D_MODEL)
```

```python
import functools

import numpy as np
import jax
import jax.numpy as jnp
from jax import lax
from jax.experimental import pallas as pl
from jax.experimental.pallas import tpu as pltpu

F32 = jnp.float32
BF16 = jnp.bfloat16

D_MODEL = 1024
ATT_GROUPS = ((128, 1), (512, 4), (2048, 16))
ATT_HEADS_PER_GROUP = 4
ATT_HEAD_DIM = 64
ATT_WIDTH = 768
ATT_GROUP_WIDTH = ATT_HEADS_PER_GROUP * ATT_HEAD_DIM
ATT_PARTS_WIDTH = 3 * ATT_GROUP_WIDTH
ROPE_THETA = 10000.0
SG_CHUNK = 128
SG_GROUPS = 6
SG_WIDTH = 768
ML_HEADS = 4
ML_HEAD_DIM = 192
ML_HEAD_PAD = 256
ML_WIDTH = 768
ML_PAD_WIDTH = ML_HEADS * ML_HEAD_PAD
ML_CHUNK = 128
ML_CONV = 4
N_BRANCH = 3
D_FF = 2816
N_EXPERTS = 8
TOP_K = 2
LN_EPS = 1e-5
BLOCK = 128
ATT_MIN_ROWS_PER_STEP = 1024
LANES = 128
VMEM_LIMIT = 56 * 1024 * 1024
EXPERT_TILE = 512
DMA_ISSUE_UNROLL = 8
MIXER_SUB_ROWS = 256


def _cparams(*sem):
    return pltpu.CompilerParams(dimension_semantics=sem, vmem_limit_bytes=VMEM_LIMIT)


def _layer_norm(x, g, b):
    mu = jnp.mean(x, axis=-1, keepdims=True)
    xc = x - mu
    var = jnp.mean(xc * xc, axis=-1, keepdims=True)
    return xc * lax.rsqrt(var + LN_EPS) * g + b


def _gelu_tanh(x):
    return 0.5 * x * (1.0 + jnp.tanh(np.sqrt(2.0 / np.pi) * (x + 0.044715 * (x * x * x))))


def _sigmoid(x):
    return 0.5 * (1.0 + jnp.tanh(0.5 * x))


def _dot(a, b):
    return jnp.dot(a, b, preferred_element_type=F32)


def _dot_nt(a, b):
    return lax.dot_general(a, b, (((1,), (1,)), ((), ())), preferred_element_type=F32)


def _dot_tn(a, b):
    return lax.dot_general(a, b, (((0,), (0,)), ((), ())), preferred_element_type=F32)


def _split3(x):
    hi = x.astype(BF16)
    r1 = x - hi.astype(F32)
    mid = r1.astype(BF16)
    lo = (r1 - mid.astype(F32)).astype(BF16)
    return hi, mid, lo


def _rider_specs(arrays, steps):
    specs, shapes = [], []
    for a in arrays:
        span = 1
        while (a.shape[0] * span) % steps or (a.shape[0] * span // steps) % 16:
            span *= 2
        rows = a.shape[0] * span // steps
        specs.append(pl.BlockSpec((rows, a.shape[1]), lambda i, span=span: (i // span, 0)))
        shapes.append(jax.ShapeDtypeStruct(a.shape, BF16))
    return specs, shapes


def _cast_riders(in_refs, out_refs):
    for i_ref, o_ref in zip(in_refs, out_refs):
        o_ref[...] = i_ref[...].astype(o_ref.dtype)


def _projections_kernel(x_ref, *refs, acts, riders):
    n = len(acts)
    ws, bs, rin = refs[:n], refs[n:2 * n], refs[2 * n:2 * n + riders]
    outs, rout = refs[2 * n + riders:3 * n + riders], refs[3 * n + riders:]
    x = x_ref[...]
    for w_ref, b_ref, o_ref, act in zip(ws, bs, outs, acts):
        res = _dot(x, w_ref[...]) + b_ref[...]
        o_ref[...] = (res if act is None else act(res)).astype(o_ref.dtype)
    _cast_riders(rin, rout)


def _projections(x, heads, riders=(), tm=512):
    m, k = x.shape

    def resident(a):
        return pl.BlockSpec(a.shape, lambda i: (0, 0), pipeline_mode=pl.Buffered(1))

    ws = [h[0] for h in heads]
    bs = [h[1] for h in heads]
    rider_specs, rider_shapes = _rider_specs(riders, m // tm)
    return pl.pallas_call(
        functools.partial(_projections_kernel, acts=tuple(h[3] for h in heads), riders=len(riders)),
        grid=(m // tm,),
        in_specs=([pl.BlockSpec((tm, k), lambda i: (i, 0))] + [resident(w) for w in ws] + [resident(b) for b in bs]
                  + rider_specs),
        out_specs=[pl.BlockSpec((tm, w.shape[1]), lambda i: (i, 0)) for w in ws] + rider_specs,
        out_shape=[jax.ShapeDtypeStruct((m, h[0].shape[1]), h[2]) for h in heads] + rider_shapes,
        compiler_params=_cparams("arbitrary"),
        name="projections",
    )(x, *ws, *bs, *riders)


def _linear_dilated_kernel(x_ref, w_ref, b_ref, o_ref, acc, *, dil):
    res = _dot(x_ref[...], w_ref[...]) + b_ref[...]
    slabs, rows, _ = acc.shape
    for c in range(slabs):
        acc[c] = res[:, c * LANES:(c + 1) * LANES]
    sub = rows // dil
    for r in range(dil):
        for c in range(slabs):
            o_ref[r, :, c * LANES:(c + 1) * LANES] = acc[c, pl.ds(r, sub, stride=dil), :].astype(o_ref.dtype)


def _linear_dilated(x, w, b, dil, batch, seq):
    k = x.shape[1]
    n = w.shape[1]
    tn = ATT_GROUP_WIDTH
    sub = seq // dil
    return pl.pallas_call(
        functools.partial(_linear_dilated_kernel, dil=dil),
        grid=(batch, n // tn),
        in_specs=[pl.BlockSpec((seq, k), lambda i, j: (i, 0)),
                  pl.BlockSpec((k, tn), lambda i, j: (0, j)),
                  pl.BlockSpec((1, tn), lambda i, j: (0, j))],
        out_specs=pl.BlockSpec((None, dil, sub, tn), lambda i, j: (i, 0, 0, j)),
        out_shape=jax.ShapeDtypeStruct((batch, dil, sub, n), BF16),
        scratch_shapes=[pltpu.VMEM((tn // LANES, seq, LANES), F32)],
        compiler_params=_cparams("parallel", "parallel"),
        name=f"linear_dil{dil}",
    )(x, w, b)


def _rope_table_kernel(pos_ref, freq_ref, sign_ref, cos_ref, sin_ref):
    ang = pos_ref[...] * freq_ref[...]
    cos_ref[...] = jnp.cos(ang)
    sin_ref[...] = jnp.sin(ang) * sign_ref[...]


def _rope_tables(positions):
    n = positions.size
    half = ATT_HEAD_DIM // 2
    freqs = ROPE_THETA ** (-jnp.arange(half, dtype=F32) * (2.0 / ATT_HEAD_DIM))
    lane = np.arange(LANES)
    freq_row = freqs[lane % half][None, :]
    sign_row = jnp.asarray(np.where(lane % ATT_HEAD_DIM < half, -1.0, 1.0), F32)[None, :]
    pos = positions.astype(F32).reshape(n, 1)
    tm = 2048
    return pl.pallas_call(
        _rope_table_kernel,
        grid=(n // tm,),
        in_specs=[pl.BlockSpec((tm, 1), lambda i: (i, 0)),
                  pl.BlockSpec((1, LANES), lambda i: (0, 0)),
                  pl.BlockSpec((1, LANES), lambda i: (0, 0))],
        out_specs=[pl.BlockSpec((tm, LANES), lambda i: (i, 0))] * 2,
        out_shape=[jax.ShapeDtypeStruct((n, LANES), F32)] * 2,
        compiler_params=_cparams("parallel"),
        name="rope_tables",
    )(pos, freq_row, sign_row)


def _attn_kernel(q_ref, k_ref, v_ref, cos_ref, sin_ref, o_ref, lse_ref, qs, ks, vs, *, sub, dil):
    per_step = q_ref.shape[0]
    for rr in range(per_step):
        _attn_residue(q_ref.at[rr], k_ref.at[rr], v_ref.at[rr], cos_ref, sin_ref, o_ref.at[rr], lse_ref.at[rr],
                      qs.at[rr], ks.at[rr], vs.at[rr], pl.program_id(1) * per_step + rr, sub=sub, dil=dil)


def _attn_residue(q_ref, k_ref, v_ref, cos_ref, sin_ref, o_ref, lse_ref, qs, ks, vs, res, *, sub, dil):
    nb = sub // BLOCK
    lane = lax.broadcasted_iota(jnp.int32, (BLOCK, LANES), 1)
    first_half = (lane % ATT_HEAD_DIM) < (ATT_HEAD_DIM // 2)
    low_head = lane < ATT_HEAD_DIM

    def rope(x, c, s):
        partner = jnp.where(first_half, pltpu.roll(x, LANES - 32, 1), pltpu.roll(x, 32, 1))
        return x * c + partner * s

    ks[0:BLOCK, :] = jnp.zeros((BLOCK, ATT_GROUP_WIDTH), BF16)
    vs[0:BLOCK, :] = jnp.zeros((BLOCK, ATT_GROUP_WIDTH), BF16)

    def prep(n, carry):
        r0 = pl.multiple_of(n * BLOCK, BLOCK)
        if dil == 1:
            c = cos_ref[pl.ds(r0, BLOCK), :]
            s = sin_ref[pl.ds(r0, BLOCK), :]
        else:
            c = cos_ref[pl.ds(r0 * dil + res, BLOCK, stride=dil), :]
            s = sin_ref[pl.ds(r0 * dil + res, BLOCK, stride=dil), :]
        for pair in range(2):
            cols = slice(pair * LANES, (pair + 1) * LANES)
            q = rope(q_ref[pl.ds(r0, BLOCK), cols].astype(F32), c, s) * (ATT_HEAD_DIM ** -0.5)
            qs[pl.ds(r0, BLOCK), (2 * pair) * LANES:(2 * pair + 1) * LANES] = jnp.where(low_head, q, 0.0).astype(BF16)
            qs[pl.ds(r0, BLOCK), (2 * pair + 1) * LANES:(2 * pair + 2) * LANES] = jnp.where(low_head, 0.0, q).astype(BF16)
            k = rope(k_ref[pl.ds(r0, BLOCK), cols].astype(F32), c, s)
            ks[pl.ds(r0 + BLOCK, BLOCK), cols] = k.astype(BF16)
        vs[pl.ds(r0 + BLOCK, BLOCK), :] = v_ref[pl.ds(r0, BLOCK), :]
        return carry

    if nb <= 2:
        for n in range(nb):
            prep(n, 0)
    else:
        lax.fori_loop(0, nb, prep, 0)

    qi = lax.broadcasted_iota(jnp.int32, (BLOCK, 2 * BLOCK), 0)
    ki = lax.broadcasted_iota(jnp.int32, (BLOCK, 2 * BLOCK), 1)
    dist = qi + BLOCK - ki
    band = (dist >= 0) & (dist <= BLOCK)
    neg_inf = jnp.float32(-jnp.inf)
    bias_rest = jnp.where(band, 0.0, neg_inf)
    bias_first = jnp.where(band & (ki >= BLOCK), 0.0, neg_inf)

    def block(n, bias):
        r0 = pl.multiple_of(n * BLOCK, BLOCK)
        heads = range(ATT_HEADS_PER_GROUP)
        pair_cols = [slice(pair * LANES, (pair + 1) * LANES) for pair in range(2)]
        kw = [ks[pl.ds(r0, 2 * BLOCK), c] for c in pair_cols]
        vw = [vs[pl.ds(r0, 2 * BLOCK), c] for c in pair_cols]
        s = [_dot_nt(qs[pl.ds(r0, BLOCK), hd * LANES:(hd + 1) * LANES], kw[hd // 2]) + bias for hd in heads]
        m = [jnp.max(s[hd], axis=-1, keepdims=True) for hd in heads]
        p = [jnp.exp(s[hd] - m[hd]) for hd in heads]
        den = [jnp.sum(p[hd], axis=-1, keepdims=True) for hd in heads]
        o = [_dot(p[hd].astype(BF16), vw[hd // 2]) * (1.0 / den[hd]) for hd in heads]
        lse = [m[hd] + jnp.log(den[hd]) for hd in heads]
        for pair in range(2):
            lo, hi = 2 * pair, 2 * pair + 1
            o_ref[pl.ds(r0, BLOCK), pair_cols[pair]] = jnp.where(low_head, o[lo], o[hi]).astype(o_ref.dtype)
            lse_ref[pl.ds(r0, BLOCK), pair_cols[pair]] = jnp.where(low_head, lse[lo], lse[hi])

    block(0, bias_first)
    if nb <= 2:
        for n in range(1, nb):
            block(n, bias_rest)
        return

    def body(n, carry):
        block(n, bias_rest)
        return carry

    lax.fori_loop(1, nb, body, 0, unroll=4)


def _attention_group(qkv, cos, sin, group, batch, seq):
    window, dil = ATT_GROUPS[group]
    assert window // dil == BLOCK
    sub = seq // dil
    assert sub % BLOCK == 0

    per_step = min(dil, max(1, ATT_MIN_ROWS_PER_STEP // sub))

    def part(p):
        return pl.BlockSpec((None, per_step, sub, ATT_GROUP_WIDTH), lambda b, r: (b, r, 0, p))

    tab = pl.BlockSpec((seq, LANES), lambda b, r: (b, 0))
    out = pl.BlockSpec((None, per_step, sub, ATT_GROUP_WIDTH), lambda b, r: (b, r, 0, 0))
    return pl.pallas_call(
        functools.partial(_attn_kernel, sub=sub, dil=dil),
        grid=(batch, dil // per_step),
        in_specs=[part(0), part(1), part(2), tab, tab],
        out_specs=[out, out],
        out_shape=[jax.ShapeDtypeStruct((batch, dil, sub, ATT_GROUP_WIDTH), BF16),
                   jax.ShapeDtypeStruct((batch, dil, sub, ATT_GROUP_WIDTH), F32)],
        scratch_shapes=[pltpu.VMEM((per_step, sub, ATT_HEADS_PER_GROUP * LANES), BF16),
                        pltpu.VMEM((per_step, sub + BLOCK, ATT_GROUP_WIDTH), BF16),
                        pltpu.VMEM((per_step, sub + BLOCK, ATT_GROUP_WIDTH), BF16)],
        compiler_params=_cparams("parallel", "parallel"),
        name=f"attention_g{group}",
    )(qkv, qkv, qkv, cos, sin)


ML_SEQS_PER_STEP = 2


def _mlstm_kernel(qk_ref, v_ref, og_ref, g_ref, cw_ref, cb_ref, sc_ref, out_ref, frame, ct, m_s):
    taps = ML_CONV

    @pl.when(pl.program_id(1) == 0)
    def _():
        frame[:, 0:ML_CHUNK, :] = jnp.zeros((frame.shape[0], ML_CHUNK, 2 * ML_PAD_WIDTH), BF16)
        ct[...] = jnp.zeros(ct.shape, F32)
        m_s[...] = jnp.zeros(m_s.shape, F32)

    out_row = lax.broadcasted_iota(jnp.int32, (ML_CHUNK, 2 * ML_CHUNK), 0)
    src_row = lax.broadcasted_iota(jnp.int32, (ML_CHUNK, 2 * ML_CHUNK), 1)
    picks = [jnp.where(src_row == out_row + (ML_CHUNK - (taps - 1 - j)), 1.0, 0.0).astype(BF16)
             for j in range(taps - 1)]
    row = lax.broadcasted_iota(jnp.int32, (ML_CHUNK, ML_CHUNK), 0)
    colm = lax.broadcasted_iota(jnp.int32, (ML_CHUNK, ML_CHUNK), 1)
    causal = row >= colm
    tri = jnp.where(causal, 1.0, 0.0).astype(BF16)
    neg_inf = jnp.float32(-jnp.inf)

    seqs = range(qk_ref.shape[0])
    pairs = [(s, h) for s in seqs for h in range(ML_HEADS)]

    def head_cols(h):
        return slice(h * ML_HEAD_PAD, (h + 1) * ML_HEAD_PAD)

    act, gates, bcum, bcum_t, gates_t = {}, {}, {}, {}, {}
    for s in seqs:
        cur = qk_ref[s]
        frame[s, ML_CHUNK:2 * ML_CHUNK, :] = cur
        both = frame[s]
        acc = cb_ref[...] + cur.astype(F32) * cw_ref[taps - 1:taps, :]
        for j in range(taps - 1):
            acc = acc + _dot(picks[j], both) * cw_ref[j:j + 1, :]
        frame[s, 0:ML_CHUNK, :] = cur
        act[s] = (acc * _sigmoid(acc) * sc_ref[...]).astype(BF16)
        gates[s] = g_ref[s]
        log_f = -(jnp.maximum(-gates[s], 0.0) + jnp.log1p(jnp.exp(-jnp.abs(gates[s]))))
        hi, mid, lo = _split3(log_f)
        bcum[s] = _dot(tri, hi) + _dot(tri, mid) + _dot(tri, lo)
        bcum_t[s] = bcum[s].T
        gates_t[s] = gates[s].T

    qh, kh, vh, c_old, m_prev, b_c, i_c, log_d, m_inter, m = ({} for _ in range(10))
    for p in pairs:
        s, h = p
        qh[p] = act[s][:, head_cols(h)]
        kh[p] = act[s][:, ML_PAD_WIDTH + h * ML_HEAD_PAD:ML_PAD_WIDTH + (h + 1) * ML_HEAD_PAD]
        vh[p] = v_ref[s, :, head_cols(h)]
        c_old[p] = ct[s, h]
        m_prev[p] = m_s[s, h, 0:1, 0:1]
        b_c[p] = bcum[s][:, ML_HEADS + h:ML_HEADS + h + 1]
        b_r = bcum_t[s][ML_HEADS + h:ML_HEADS + h + 1, :]
        i_c[p] = gates[s][:, h:h + 1]
        i_r = gates_t[s][h:h + 1, :]
        log_d[p] = jnp.where(causal, b_c[p] - b_r + i_r, neg_inf)
        m_inter[p] = b_c[p] + m_prev[p]
        m[p] = jnp.maximum(m_inter[p], jnp.max(log_d[p], axis=-1, keepdims=True))

    scores = {p: _dot_nt(qh[p], kh[p]) for p in pairs}
    carried = {p: _dot(qh[p], c_old[p].astype(BF16)) for p in pairs}
    sc = {p: (scores[p] * jnp.exp(log_d[p] - m[p])).astype(BF16) for p in pairs}
    num = {p: _dot(sc[p], vh[p]) + jnp.exp(m_inter[p] - m[p]) * carried[p] for p in pairs}
    for p in pairs:
        s, h = p
        den = num[p][:, ML_HEAD_DIM:ML_HEAD_DIM + 1]
        hid = num[p] / jnp.maximum(jnp.abs(den), jnp.exp(-m[p]))
        out_ref[s, :, head_cols(h)] = (og_ref[s, :, head_cols(h)].astype(F32) * hid).astype(out_ref.dtype)
    for p in pairs:
        s, h = p
        b_last = b_c[p][ML_CHUNK - 1:ML_CHUNK, :]
        m_new = m[p][ML_CHUNK - 1:ML_CHUNK, :]
        w = jnp.exp(b_last - b_c[p] + i_c[p] - m_new)
        decay = jnp.exp(b_last + m_prev[p] - m_new)
        wv = (w * vh[p].astype(F32)).astype(BF16)
        ct[s, h] = decay * c_old[p] + _dot_tn(kh[p], wv)
        m_s[s, h] = jnp.broadcast_to(m_new, m_s.shape[2:])


def _mlstm(ml, og, gates, conv_w, conv_b, qk_scale, batch, seq):
    nc = seq // ML_CHUNK
    group = ML_SEQS_PER_STEP if batch % ML_SEQS_PER_STEP == 0 else 1

    def whole(a):
        return pl.BlockSpec(a.shape, lambda b, c: (0, 0))

    def chunk(width, col):
        return pl.BlockSpec((group, ML_CHUNK, width), lambda b, c: (b, c, col))

    out = pl.pallas_call(
        _mlstm_kernel,
        grid=(batch // group, nc),
        in_specs=[chunk(2 * ML_PAD_WIDTH, 0), chunk(ML_PAD_WIDTH, 2), chunk(ML_PAD_WIDTH, 0), chunk(LANES, 0),
                  whole(conv_w), whole(conv_b), whole(qk_scale)],
        out_specs=chunk(ML_PAD_WIDTH, 0),
        out_shape=jax.ShapeDtypeStruct((batch, seq, ML_PAD_WIDTH), BF16),
        scratch_shapes=[pltpu.VMEM((group, 2 * ML_CHUNK, 2 * ML_PAD_WIDTH), BF16),
                        pltpu.VMEM((group, ML_HEADS, ML_HEAD_PAD, ML_HEAD_PAD), F32),
                        pltpu.VMEM((group, ML_HEADS, 8, LANES), F32)],
        compiler_params=_cparams("parallel", "arbitrary"),
        name="mlstm",
    )(ml.reshape(batch, seq, -1), ml.reshape(batch, seq, -1), og.reshape(batch, seq, -1),
      gates.reshape(batch, seq, -1), conv_w, conv_b, qk_scale)
    return out.reshape(batch * seq, ML_PAD_WIDTH)


def _mixer_out_kernel(x_ref, o0_ref, l0_ref, o1_ref, l1_ref, o2_ref, l2_ref, sg_ref, hc_ref, gp_ref,
                      sgg_ref, sgb_ref, sgw_ref, sgbias_ref, wa_ref, wb_ref, wc_ref, wo_ref, lng_ref, lnb_ref,
                      xf_ref, xb_ref, unperm, *, alpha):
    tm = x_ref.shape[0]
    slabs = ATT_GROUP_WIDTH // LANES
    att_refs = (l0_ref, l1_ref, l2_ref, o0_ref, o1_ref, o2_ref)

    for slot, ref in enumerate(att_refs):
        dil = ref.shape[0]
        if dil > 1:
            for r in range(dil):
                for c in range(slabs):
                    unperm[slabs * slot + c, pl.ds(r, tm // dil, stride=dil), :] = (
                        ref[r, :, c * LANES:(c + 1) * LANES].astype(F32))

    def token_order(slot, rows):
        ref = att_refs[slot]
        if ref.shape[0] == 1:
            return ref[0, rows, :].astype(F32)
        return jnp.concatenate([unperm[slabs * slot + c, rows, :] for c in range(slabs)], axis=1)

    row = lax.broadcasted_iota(jnp.int32, (SG_CHUNK, SG_CHUNK), 0)
    colm = lax.broadcasted_iota(jnp.int32, (SG_CHUNK, SG_CHUNK), 1)
    causal = row >= colm
    group_dim = SG_WIDTH // SG_GROUPS
    sg_w = [jnp.where(causal, sgw_ref[g], 0.0).astype(BF16) for g in range(SG_GROUPS)]

    for h in range(tm // MIXER_SUB_ROWS):
        rows = slice(h * MIXER_SUB_ROWS, (h + 1) * MIXER_SUB_ROWS)

        l0, l1, l2 = token_order(0, rows), token_order(1, rows), token_order(2, rows)
        lmax = jnp.maximum(jnp.maximum(l0, l1), l2)
        e0, e1, e2 = jnp.exp(l0 - lmax), jnp.exp(l1 - lmax), jnp.exp(l2 - lmax)
        inv = 1.0 / (e0 + e1 + e2)
        y_a = ((e0 * inv) * token_order(3, rows) + (e1 * inv) * token_order(4, rows)
               + (e2 * inv) * token_order(5, rows))

        u = sg_ref[rows, 0:SG_WIDTH].astype(F32)
        v = sg_ref[rows, SG_WIDTH:2 * SG_WIDTH].astype(F32)
        v = _layer_norm(v, sgg_ref[...], sgb_ref[...]).astype(BF16)
        chunks = []
        for c in range(MIXER_SUB_ROWS // SG_CHUNK):
            crows = slice(c * SG_CHUNK, (c + 1) * SG_CHUNK)
            parts = [_dot(sg_w[g], v[crows, g * group_dim:(g + 1) * group_dim]) + sgbias_ref[:, g:g + 1]
                     for g in range(SG_GROUPS)]
            chunks.append(jnp.concatenate(parts, axis=1))
        y_b = u * jnp.concatenate(chunks, axis=0)

        def gate(i):
            return gp_ref[rows, i * D_MODEL:(i + 1) * D_MODEL].astype(F32)

        z = (gate(0) * _dot(y_a.astype(BF16), wa_ref[...])
             + gate(1) * _dot(y_b.astype(BF16), wb_ref[...])
             + gate(2) * _dot(hc_ref[rows, :], wc_ref[...]))
        mix = _dot(z.astype(BF16), wo_ref[...])
        out = _layer_norm(alpha * x_ref[rows, :] + mix, lng_ref[...], lnb_ref[...])
        xf_ref[rows, :] = out
        xb_ref[rows, :] = out.astype(BF16)


def _mixer_out(x, att_outs, sg, hc, gpre, p, alpha, seq, tm=512):
    n = x.shape[0]
    tiles_per_seq = seq // tm

    def rows(width):
        return pl.BlockSpec((tm, width), lambda i: (i, 0))

    def whole(a):
        return pl.BlockSpec(a.shape, lambda i: (0,) * a.ndim)

    def residue_major(dil):
        return pl.BlockSpec((None, dil, tm // dil, ATT_GROUP_WIDTH),
                            lambda i: (i // tiles_per_seq, 0, i % tiles_per_seq, 0))

    att_specs, att_args = [], []
    for (o, lse), (_, dil) in zip(att_outs, ATT_GROUPS):
        att_specs += [residue_major(dil)] * 2
        att_args += [o, lse]
    weights = [p["sg_ln_g"], p["sg_ln_b"], p["sg_w"], p["sg_bias_t"], p["w_br_a"], p["w_br_b"], p["w_br_c"],
               p["w_out"], p["ln_g0"], p["ln_b0"]]
    return pl.pallas_call(
        functools.partial(_mixer_out_kernel, alpha=alpha),
        grid=(n // tm,),
        in_specs=[rows(D_MODEL)] + att_specs
                 + [rows(2 * SG_WIDTH), rows(ML_PAD_WIDTH), rows(N_BRANCH * D_MODEL)]
                 + [whole(w) for w in weights],
        out_specs=[rows(D_MODEL), rows(D_MODEL)],
        out_shape=[jax.ShapeDtypeStruct((n, D_MODEL), F32), jax.ShapeDtypeStruct((n, D_MODEL), BF16)],
        scratch_shapes=[pltpu.VMEM((2 * len(ATT_GROUPS) * ATT_GROUP_WIDTH // LANES, tm, LANES), F32)],
        compiler_params=_cparams("parallel"),
        name="mixer_out",
    )(x, *att_args, sg, hc, gpre, *weights)


def _swiglu_tile(x, w1_ref, w3_ref, w2_ref, prod, tf):
    for c in range(D_FF // tf):
        cols = slice(c * tf, (c + 1) * tf)
        h1 = _dot(x, w1_ref[:, cols])
        h3 = _dot(x, w3_ref[:, cols])
        prod[:, cols] = (h1 * _sigmoid(h1) * h3).astype(BF16)
    return _dot(prod[...], w2_ref[...])


def _ffn_kernel(xf_ref, xb_ref, w1_ref, w3_ref, w2_ref, lng_ref, lnb_ref, *refs, alpha, tf, riders):
    rin, (of_ref, ob_ref), rout, prod = refs[:riders], refs[riders:riders + 2], refs[riders + 2:-1], refs[-1]
    f = _swiglu_tile(xb_ref[...], w1_ref, w3_ref, w2_ref, prod, tf)
    out = _layer_norm(alpha * xf_ref[...] + f, lng_ref[...], lnb_ref[...])
    of_ref[...] = out
    ob_ref[...] = out.astype(BF16)
    _cast_riders(rin, rout)


def _ffn(xf, xb, w1, w3, w2, ln_g, ln_b, alpha, riders=(), tm=512, tf=256):
    n = xf.shape[0]

    def resident(shape):
        return pl.BlockSpec(shape, lambda i: (0, 0), pipeline_mode=pl.Buffered(1))

    rider_specs, rider_shapes = _rider_specs(riders, n // tm)
    return pl.pallas_call(
        functools.partial(_ffn_kernel, alpha=alpha, tf=tf, riders=len(riders)),
        grid=(n // tm,),
        in_specs=[pl.BlockSpec((tm, D_MODEL), lambda i: (i, 0)),
                  pl.BlockSpec((tm, D_MODEL), lambda i: (i, 0)),
                  resident((D_MODEL, D_FF)), resident((D_MODEL, D_FF)), resident((D_FF, D_MODEL)),
                  pl.BlockSpec((1, D_MODEL), lambda i: (0, 0)),
                  pl.BlockSpec((1, D_MODEL), lambda i: (0, 0))] + rider_specs,
        out_specs=[pl.BlockSpec((tm, D_MODEL), lambda i: (i, 0))] * 2 + rider_specs,
        out_shape=[jax.ShapeDtypeStruct((n, D_MODEL), F32), jax.ShapeDtypeStruct((n, D_MODEL), BF16)]
                  + rider_shapes,
        scratch_shapes=[pltpu.VMEM((tm, D_FF), BF16)],
        compiler_params=_cparams("arbitrary"),
        name="ffn",
    )(xf, xb, w1, w3, w2, ln_g, ln_b, *riders)


def _router_kernel(x_ref, w_ref, b_ref, info_ref):
    xh, xm, _ = _split3(x_ref[...])
    wh, wm, _ = _split3(w_ref[...])
    logits = (_dot(xh, wh) + (_dot(xh, wm) + _dot(xm, wh))) + b_ref[...]
    lane = lax.broadcasted_iota(jnp.int32, logits.shape, 1)
    neg_inf = jnp.float32(-jnp.inf)
    logits = jnp.where(lane < N_EXPERTS, logits, neg_inf)
    v1 = jnp.max(logits, axis=-1, keepdims=True)
    i1 = jnp.min(jnp.where(logits == v1, lane, LANES), axis=-1, keepdims=True)
    rest = jnp.where(lane == i1, neg_inf, logits)
    v2 = jnp.max(rest, axis=-1, keepdims=True)
    i2 = jnp.min(jnp.where(rest == v2, lane, LANES), axis=-1, keepdims=True)
    e = jnp.exp(v2 - v1)
    inv = 1.0 / (1.0 + e)
    info_ref[...] = jnp.where(lane == 0, i1.astype(F32),
                              jnp.where(lane == 1, i2.astype(F32),
                                        jnp.where(lane == 2, inv, jnp.where(lane == 3, e * inv, 0.0))))


def _router(xf, router_w, router_b, tm=1024):
    n = xf.shape[0]
    w = jnp.pad(router_w, ((0, 0), (0, LANES - N_EXPERTS)))
    b = jnp.pad(router_b, (0, LANES - N_EXPERTS))[None, :]
    return pl.pallas_call(
        _router_kernel,
        grid=(n // tm,),
        in_specs=[pl.BlockSpec((tm, D_MODEL), lambda i: (i, 0)),
                  pl.BlockSpec((D_MODEL, LANES), lambda i: (0, 0)),
                  pl.BlockSpec((1, LANES), lambda i: (0, 0))],
        out_specs=pl.BlockSpec((tm, LANES), lambda i: (i, 0)),
        out_shape=jax.ShapeDtypeStruct((n, LANES), F32),
        compiler_params=_cparams("parallel"),
        name="router",
    )(xf, w, b)


def _routing_plan(info, n):
    tile = EXPERT_TILE
    n_tiles = TOP_K * n // tile + N_EXPERTS
    expert = info[:, 0:TOP_K].astype(jnp.int32).reshape(-1)
    onehot = (expert[:, None] == jnp.arange(N_EXPERTS, dtype=jnp.int32)[None, :]).astype(jnp.int32)
    running = jnp.cumsum(onehot, axis=0)
    rank = jnp.sum(onehot * running, axis=1) - 1
    counts = running[-1]
    padded = ((counts + tile - 1) // tile) * tile
    ends = jnp.cumsum(padded)
    starts = ends - padded
    dest = jnp.sum(onehot * starts[None, :], axis=1) + rank
    used = ends[-1] // tile
    tile_start = jnp.arange(n_tiles, dtype=jnp.int32) * tile
    tile_expert = jnp.sum((tile_start[:, None] >= ends[None, :]).astype(jnp.int32), axis=1)
    last_expert = jnp.sum((tile_start[used - 1] >= ends).astype(jnp.int32))
    tile_expert = jnp.minimum(tile_expert, last_expert)
    plan = dict(dest=dest, tile_expert=tile_expert, used=used.reshape(1),
                pad_start=starts + counts, pad_count=padded - counts)
    return {k: v.astype(jnp.int32) for k, v in plan.items()}, n_tiles


def _rows_wait(src, dst, rows, sem):
    pltpu.make_async_copy(src.at[pl.ds(0, rows)], dst.at[pl.ds(0, rows)], sem).wait()


ROW_SUBLANES = D_MODEL // LANES


def _tile_copy(src, src_row, dst, dst_row, sem):
    s0 = pl.multiple_of(src_row * ROW_SUBLANES, ROW_SUBLANES)
    d0 = pl.multiple_of(dst_row * ROW_SUBLANES, ROW_SUBLANES)
    return pltpu.make_async_copy(src.at[pl.ds(s0, ROW_SUBLANES)], dst.at[pl.ds(d0, ROW_SUBLANES)], sem)


def _to_row_tiles(ref, x):
    m = x.shape[0]
    for c in range(ROW_SUBLANES):
        ref[pl.ds(c, m, stride=ROW_SUBLANES), :] = x[:, c * LANES:(c + 1) * LANES]


def _from_row_tiles(ref):
    m = ref.shape[0] // ROW_SUBLANES
    return jnp.concatenate([ref[pl.ds(c, m, stride=ROW_SUBLANES), :] for c in range(ROW_SUBLANES)], axis=1)


DISPATCH_SLOTS = 3


def _dispatch_kernel(pad_start_ref, pad_count_ref, used_ref, dest_ref, x_ref, xs_ref, buf, tiles, zero_row,
                     load_sem, row_sem, pad_sem, *, tm):
    i = pl.program_id(0)

    @pl.when(i == 0)
    def _():
        zero_row[...] = jnp.zeros(zero_row.shape, F32)
        for e in range(N_EXPERTS):
            def fill(j, carry, e=e):
                _tile_copy(zero_row, 0, xs_ref, pad_start_ref[e] + j, pad_sem).start()
                return carry
            lax.fori_loop(0, pad_count_ref[e], fill, 0)
        for e in range(N_EXPERTS):
            def drain(j, carry):
                _tile_copy(zero_row, 0, xs_ref, 0, pad_sem).wait()
                return carry
            lax.fori_loop(0, pad_count_ref[e], drain, 0)

        tile_rows = tm * ROW_SUBLANES
        spare = xs_ref.shape[0] // tile_rows - used_ref[0]
        tiles[1] = jnp.zeros(tiles.shape[1:], F32)

        def zero_tile(j):
            row0 = pl.multiple_of((used_ref[0] + j) * tile_rows, tile_rows)
            return pltpu.make_async_copy(tiles.at[1], xs_ref.at[pl.ds(row0, tile_rows)], pad_sem)

        def fill_tile(j, carry):
            zero_tile(j).start()
            return carry

        def drain_tile(j, carry):
            zero_tile(j).wait()
            return carry

        lax.fori_loop(0, spare, fill_tile, 0)
        lax.fori_loop(0, spare, drain_tile, 0)

    steps = pl.num_programs(0)
    slot = i % DISPATCH_SLOTS
    prev_slot = (i + DISPATCH_SLOTS - 1) % DISPATCH_SLOTS
    stage = i % 2

    def load(tile, into):
        return pltpu.make_async_copy(x_ref.at[pl.ds(tile * tm, tm), :], buf.at[into], load_sem.at[into])

    def wait_rows(of_stage):
        for _ in range(TOP_K):
            _rows_wait(tiles.at[of_stage], xs_ref, tm * ROW_SUBLANES, row_sem.at[of_stage])

    @pl.when(i == 0)
    def _():
        load(0, 0).start()

        @pl.when(steps > 1)
        def _():
            load(1, 1).start()

    load(i, slot).wait()
    _to_row_tiles(tiles.at[stage], buf[slot])

    def start(t, carry):
        for c in range(TOP_K):
            _tile_copy(tiles.at[stage], t, xs_ref, dest_ref[0, 0, TOP_K * t + c], row_sem.at[stage]).start(
                priority=c % 2)
        return carry

    lax.fori_loop(0, tm, start, 0, unroll=DMA_ISSUE_UNROLL)

    @pl.when(i > 0)
    def _():
        wait_rows(1 - stage)

    @pl.when(i + 2 < steps)
    def _():
        load(i + 2, prev_slot).start()

    @pl.when(i == steps - 1)
    def _():
        wait_rows(stage)


def _dispatch(xf, dest, pad_start, pad_count, used, n_slots, tm=EXPERT_TILE):
    n = xf.shape[0]
    dest3 = dest.reshape(n // tm, 1, TOP_K * tm)
    grid_spec = pltpu.PrefetchScalarGridSpec(
        num_scalar_prefetch=3,
        grid=(n // tm,),
        in_specs=[pl.BlockSpec((1, 1, TOP_K * tm), lambda i, ps, pc, nu: (i, 0, 0), memory_space=pltpu.SMEM),
                  pl.BlockSpec(memory_space=pl.ANY)],
        out_specs=pl.BlockSpec(memory_space=pl.ANY),
        scratch_shapes=[pltpu.VMEM((DISPATCH_SLOTS, tm, D_MODEL), F32),
                        pltpu.VMEM((2, tm * ROW_SUBLANES, LANES), F32),
                        pltpu.VMEM((ROW_SUBLANES, LANES), F32),
                        pltpu.SemaphoreType.DMA((DISPATCH_SLOTS,)),
                        pltpu.SemaphoreType.DMA((2,)),
                        pltpu.SemaphoreType.DMA],
    )
    return pl.pallas_call(
        functools.partial(_dispatch_kernel, tm=tm),
        grid_spec=grid_spec,
        out_shape=jax.ShapeDtypeStruct((n_slots * ROW_SUBLANES, LANES), F32),
        compiler_params=_cparams("arbitrary"),
        name="moe_dispatch",
    )(pad_start, pad_count, used, dest3, xf)


def _experts_kernel(te_ref, used_ref, xs_ref, w1_ref, w3_ref, w2_ref, ys_ref, prod, *, tf):
    del te_ref
    i = pl.program_id(0)

    @pl.when(i < used_ref[0])
    def _():
        x = _from_row_tiles(xs_ref).astype(BF16)
        _to_row_tiles(ys_ref, _swiglu_tile(x, w1_ref, w3_ref, w2_ref, prod, tf))

    @pl.when(i >= used_ref[0])
    def _():
        ys_ref[...] = jnp.zeros(ys_ref.shape, ys_ref.dtype)


def _experts(xs, tile_expert, used, w1, w3, w2, n_tiles, tf=256):
    tile = EXPERT_TILE
    grid_spec = pltpu.PrefetchScalarGridSpec(
        num_scalar_prefetch=2,
        grid=(n_tiles,),
        in_specs=[pl.BlockSpec((tile * ROW_SUBLANES, LANES), lambda i, te, nu: (jnp.minimum(i, nu[0] - 1), 0)),
                  pl.BlockSpec((None, D_MODEL, D_FF), lambda i, te, nu: (te[i], 0, 0)),
                  pl.BlockSpec((None, D_MODEL, D_FF), lambda i, te, nu: (te[i], 0, 0)),
                  pl.BlockSpec((None, D_FF, D_MODEL), lambda i, te, nu: (te[i], 0, 0))],
        out_specs=pl.BlockSpec((tile * ROW_SUBLANES, LANES), lambda i, te, nu: (i, 0)),
        scratch_shapes=[pltpu.VMEM((tile, D_FF), BF16)],
    )
    return pl.pallas_call(
        functools.partial(_experts_kernel, tf=tf),
        grid_spec=grid_spec,
        out_shape=jax.ShapeDtypeStruct((n_tiles * tile * ROW_SUBLANES, LANES), F32),
        compiler_params=_cparams("arbitrary"),
        name="moe_experts",
    )(tile_expert, used, xs, w1, w3, w2)


def _combine_kernel(dest_ref, next_dest_ref, xf_ref, info_ref, ys_ref, lng_ref, lnb_ref, of_ref, got, sem, *, alpha):
    tm = xf_ref.shape[0]
    i = pl.program_id(0)
    slot = i % 2

    def gather(dests, into):
        def start(t, carry):
            for c in range(TOP_K):
                _tile_copy(ys_ref, dests[0, 0, TOP_K * t + c], got.at[into, c], t, sem.at[into]).start(
                    priority=c % 2)
            return carry
        lax.fori_loop(0, tm, start, 0, unroll=DMA_ISSUE_UNROLL)

    @pl.when(i == 0)
    def _():
        gather(dest_ref, 0)

    @pl.when(i + 1 < pl.num_programs(0))
    def _():
        gather(next_dest_ref, 1 - slot)

    for c in range(TOP_K):
        _rows_wait(ys_ref, got.at[slot, c], tm * ROW_SUBLANES, sem.at[slot])
    info = info_ref[...]
    mix = (info[:, 2:3] * _from_row_tiles(got.at[slot, 0]) + info[:, 3:4] * _from_row_tiles(got.at[slot, 1]))
    of_ref[...] = _layer_norm(alpha * xf_ref[...] + mix, lng_ref[...], lnb_ref[...])


def _combine(xf, info, ys, dest, ln_g, ln_b, alpha, tm=256):
    n = xf.shape[0]
    steps = n // tm
    dest3 = dest.reshape(steps, 1, TOP_K * tm)
    return pl.pallas_call(
        functools.partial(_combine_kernel, alpha=alpha),
        grid=(steps,),
        in_specs=[pl.BlockSpec((1, 1, TOP_K * tm), lambda i: (i, 0, 0), memory_space=pltpu.SMEM),
                  pl.BlockSpec((1, 1, TOP_K * tm), lambda i: (jnp.minimum(i + 1, steps - 1), 0, 0),
                               memory_space=pltpu.SMEM),
                  pl.BlockSpec((tm, D_MODEL), lambda i: (i, 0)),
                  pl.BlockSpec((tm, LANES), lambda i: (i, 0)),
                  pl.BlockSpec(memory_space=pl.ANY),
                  pl.BlockSpec((1, D_MODEL), lambda i: (0, 0)),
                  pl.BlockSpec((1, D_MODEL), lambda i: (0, 0))],
        out_specs=pl.BlockSpec((tm, D_MODEL), lambda i: (i, 0)),
        out_shape=jax.ShapeDtypeStruct((n, D_MODEL), F32),
        scratch_shapes=[pltpu.VMEM((2, TOP_K, tm * ROW_SUBLANES, LANES), F32), pltpu.SemaphoreType.DMA((2,))],
        compiler_params=_cparams("arbitrary"),
        name="moe_combine",
    )(dest3, dest3, xf, info, ys, ln_g, ln_b)


def _moe(xf, router_w, router_b, w1, w3, w2, ln_g, ln_b, alpha):
    n = xf.shape[0]
    info = _router(xf, router_w, router_b)
    plan, n_tiles = _routing_plan(info, n)
    xs = _dispatch(xf, plan["dest"], plan["pad_start"], plan["pad_count"], plan["used"], n_tiles * EXPERT_TILE)
    ys = _experts(xs, plan["tile_expert"], plan["used"], w1, w3, w2, n_tiles)
    return _combine(xf, info, ys, plan["dest"], ln_g, ln_b, alpha)


def _pad_heads(w):
    lead = w.shape[:-1]
    w = w.reshape(*lead, ML_HEADS, ML_HEAD_DIM)
    w = jnp.pad(w, [(0, 0)] * len(lead) + [(0, 0), (0, ML_HEAD_PAD - ML_HEAD_DIM)])
    return w.reshape(*lead, ML_PAD_WIDTH)


def _group_qkv(w, group):
    gw = ATT_GROUP_WIDTH
    return jnp.concatenate([w[..., part * ATT_WIDTH + group * gw:part * ATT_WIDTH + (group + 1) * gw]
                            for part in range(3)], axis=-1)


def _layer_params(layer, w_in, b_in, conv_w, conv_b, sg_ln_g, sg_ln_b, sg_w, sg_b,
                  w_br_a, w_br_b, w_br_c, w_out, ln_g, ln_b):
    w, b = w_in[layer].astype(BF16), b_in[layer]
    o_att, o_sg, o_qk, o_v, o_o, o_i, o_g = 0, 2304, 3840, 5376, 6144, 6912, 6920
    ones_col = jnp.zeros((ML_HEADS, ML_HEAD_PAD), F32).at[:, ML_HEAD_DIM].set(1.0).reshape(ML_PAD_WIDTH)
    w_qk = jnp.concatenate([_pad_heads(w[:, o_qk:o_qk + ML_WIDTH]), _pad_heads(w[:, o_qk + ML_WIDTH:o_v])], axis=1)
    b_qk = jnp.concatenate([_pad_heads(b[o_qk:o_qk + ML_WIDTH]), _pad_heads(b[o_qk + ML_WIDTH:o_v])])
    qk_scale = jnp.concatenate([jnp.ones((ML_PAD_WIDTH,), F32),
                                jnp.full((ML_PAD_WIDTH,), ML_HEAD_DIM ** -0.5, F32)])
    cw, cb = conv_w[layer], conv_b[layer]
    w_att, b_att = w[:, o_att:o_sg], b[o_att:o_sg]
    return dict(
        w_att=[_group_qkv(w_att, g) for g in range(len(ATT_GROUPS))],
        b_att=[_group_qkv(b_att, g)[None, :] for g in range(len(ATT_GROUPS))],
        w_sg=w[:, o_sg:o_qk], b_sg=b[None, o_sg:o_qk],
        w_ml=jnp.concatenate([w_qk, _pad_heads(w[:, o_v:o_o])], axis=1),
        b_ml=jnp.concatenate([b_qk, _pad_heads(b[o_v:o_o]) + ones_col])[None, :],
        qk_scale=qk_scale[None, :],
        w_mlo=_pad_heads(w[:, o_o:o_i]), b_mlo=_pad_heads(b[o_o:o_i])[None, :],
        w_if=jnp.pad(w[:, o_i:o_g], ((0, 0), (0, LANES - 2 * ML_HEADS))),
        b_if=jnp.pad(b[o_i:o_g], (0, LANES - 2 * ML_HEADS))[None, :],
        w_g=w[:, o_g:], b_g=b[None, o_g:],
        conv_w=jnp.concatenate([_pad_heads(cw[:, :ML_WIDTH]), _pad_heads(cw[:, ML_WIDTH:])], axis=1),
        conv_b=jnp.concatenate([_pad_heads(cb[:ML_WIDTH]), _pad_heads(cb[ML_WIDTH:])])[None, :],
        sg_ln_g=sg_ln_g[layer][None, :], sg_ln_b=sg_ln_b[layer][None, :],
        sg_w=sg_w[layer], sg_bias_t=sg_b[layer].T,
        w_br_a=w_br_a[layer].astype(BF16), w_br_b=w_br_b[layer].astype(BF16),
        w_br_c=_pad_heads(w_br_c[layer].T).T.astype(BF16),
        w_out=w_out[layer].astype(BF16),
        ln_g0=ln_g[layer, 0][None, :], ln_b0=ln_b[layer, 0][None, :],
        ln_g1=ln_g[layer, 1][None, :], ln_b1=ln_b[layer, 1][None, :],
    )


def kernel(x, positions, w_in, b_in, conv_w, conv_b, sg_ln_g, sg_ln_b, sg_w, sg_b, w_br_a, w_br_b, w_br_c, w_out,
           ln_g, ln_b, ffn_w1, ffn_w3, ffn_w2, router_w, router_b, moe_w1, moe_w3, moe_w2):
    batch, seq, _ = x.shape
    n = batch * seq
    depth = w_in.shape[0]
    alpha = (2.0 * depth) ** 0.25
    cos, sin = _rope_tables(positions)
    xf = x.reshape(n, D_MODEL)
    xb = xf.astype(BF16)
    for layer in range(depth):
        p = _layer_params(layer, w_in, b_in, conv_w, conv_b, sg_ln_g, sg_ln_b, sg_w, sg_b,
                          w_br_a, w_br_b, w_br_c, w_out, ln_g, ln_b)
        undilated = [g for g, (_, dil) in enumerate(ATT_GROUPS) if dil == 1]
        j = layer // 2
        if layer % 2 == 0:
            riders = [ffn_w1[j], ffn_w3[j], ffn_w2[j]]
        else:
            riders = [moe_w2[j].reshape(N_EXPERTS * D_FF, D_MODEL)]
        sg, ml, og, gates_if, gbr, *rest = _projections(xb, [
            (p["w_sg"], p["b_sg"], BF16, _gelu_tanh),
            (p["w_ml"], p["b_ml"], BF16, None),
            (p["w_mlo"], p["b_mlo"], BF16, _sigmoid),
            (p["w_if"], p["b_if"], F32, None),
            (p["w_g"], p["b_g"], BF16, _sigmoid),
        ] + [(p["w_att"][g], p["b_att"][g], BF16, None) for g in undilated],
            riders=riders)
        qkv_plain, cast = rest[:len(undilated)], rest[len(undilated):]
        att_outs = []
        for g, (_, dil) in enumerate(ATT_GROUPS):
            if dil == 1:
                qkv = qkv_plain[undilated.index(g)].reshape(batch, 1, seq, ATT_PARTS_WIDTH)
            else:
                qkv = _linear_dilated(xb, p["w_att"][g], p["b_att"][g], dil, batch, seq)
            att_outs.append(_attention_group(qkv, cos, sin, g, batch, seq))
        hc = _mlstm(ml, og, gates_if, p["conv_w"], p["conv_b"], p["qk_scale"], batch, seq)
        xf, xb = _mixer_out(xf, att_outs, sg, hc, gbr, p, alpha, seq)
        if layer % 2 == 0:
            next_moe = (layer + 1) // 2 if layer + 1 < depth else None
            riders = [] if next_moe is None else [moe_w1[next_moe].reshape(N_EXPERTS * D_MODEL, D_FF),
                                                  moe_w3[next_moe].reshape(N_EXPERTS * D_MODEL, D_FF)]
            xf, xb, *moe_cast = _ffn(xf, xb, *cast, p["ln_g1"], p["ln_b1"], alpha, riders=riders)
        else:
            if layer == 0:
                moe_cast = [moe_w1[j].astype(BF16), moe_w3[j].astype(BF16)]
            w1, w3 = (m.reshape(N_EXPERTS, D_MODEL, D_FF) for m in moe_cast)
            w2 = cast[0].reshape(N_EXPERTS, D_FF, D_MODEL)
            xf = _moe(xf, router_w[j], router_b[j], w1, w3, w2, p["ln_g1"], p["ln_b1"], alpha)
            xb = xf.astype(BF16)
    return xf.reshape(batch, seq, D_MODEL)
```

```python
import functools

import numpy as np
import jax
import jax.numpy as jnp
from jax import lax
from jax.experimental import pallas as pl
from jax.experimental.pallas import tpu as pltpu

F32 = jnp.float32
BF16 = jnp.bfloat16

D_MODEL = 1024
ATT_GROUPS = ((128, 1), (512, 4), (2048, 16))
ATT_HEADS_PER_GROUP = 4
ATT_HEAD_DIM = 64
ATT_WIDTH = 768
ATT_GROUP_WIDTH = ATT_HEADS_PER_GROUP * ATT_HEAD_DIM
ATT_PARTS_WIDTH = 3 * ATT_GROUP_WIDTH
ROPE_THETA = 10000.0
SG_CHUNK = 128
SG_GROUPS = 6
SG_WIDTH = 768
ML_HEADS = 4
ML_HEAD_DIM = 192
ML_HEAD_PAD = 256
ML_WIDTH = 768
ML_PAD_WIDTH = ML_HEADS * ML_HEAD_PAD
ML_CHUNK = 128
ML_CONV = 4
N_BRANCH = 3
D_FF = 2816
N_EXPERTS = 8
TOP_K = 2
LN_EPS = 1e-5
BLOCK = 128
ATT_MIN_ROWS_PER_STEP = 1024
LANES = 128
VMEM_LIMIT = 56 * 1024 * 1024
EXPERT_TILE = 512
DMA_ISSUE_UNROLL = 8
MIXER_SUB_ROWS = 256


def _cparams(*sem):
    return pltpu.CompilerParams(dimension_semantics=sem, vmem_limit_bytes=VMEM_LIMIT)


def _layer_norm(x, g, b):
    mu = jnp.mean(x, axis=-1, keepdims=True)
    xc = x - mu
    var = jnp.mean(xc * xc, axis=-1, keepdims=True)
    return xc * lax.rsqrt(var + LN_EPS) * g + b


def _gelu_tanh(x):
    return 0.5 * x * (1.0 + jnp.tanh(np.sqrt(2.0 / np.pi) * (x + 0.044715 * (x * x * x))))


def _sigmoid(x):
    return 0.5 * (1.0 + jnp.tanh(0.5 * x))


def _dot(a, b):
    return jnp.dot(a, b, preferred_element_type=F32)


def _dot_nt(a, b):
    return lax.dot_general(a, b, (((1,), (1,)), ((), ())), preferred_element_type=F32)


def _dot_tn(a, b):
    return lax.dot_general(a, b, (((0,), (0,)), ((), ())), preferred_element_type=F32)


def _split3(x):
    hi = x.astype(BF16)
    r1 = x - hi.astype(F32)
    mid = r1.astype(BF16)
    lo = (r1 - mid.astype(F32)).astype(BF16)
    return hi, mid, lo


def _rider_specs(arrays, steps):
    specs, shapes = [], []
    for a in arrays:
        span = 1
        while (a.shape[0] * span) % steps or (a.shape[0] * span // steps) % 16:
            span *= 2
        rows = a.shape[0] * span // steps
        specs.append(pl.BlockSpec((rows, a.shape[1]), lambda i, span=span: (i // span, 0)))
        shapes.append(jax.ShapeDtypeStruct(a.shape, BF16))
    return specs, shapes


def _cast_riders(in_refs, out_refs):
    for i_ref, o_ref in zip(in_refs, out_refs):
        o_ref[...] = i_ref[...].astype(o_ref.dtype)


def _projections_kernel(x_ref, *refs, acts, riders):
    n = len(acts)
    ws, bs, rin = refs[:n], refs[n:2 * n], refs[2 * n:2 * n + riders]
    outs, rout = refs[2 * n + riders:3 * n + riders], refs[3 * n + riders:]
    x = x_ref[...]
    for w_ref, b_ref, o_ref, act in zip(ws, bs, outs, acts):
        res = _dot(x, w_ref[...]) + b_ref[...]
        o_ref[...] = (res if act is None else act(res)).astype(o_ref.dtype)
    _cast_riders(rin, rout)


def _projections(x, heads, riders=(), tm=512):
    m, k = x.shape

    def resident(a):
        return pl.BlockSpec(a.shape, lambda i: (0, 0), pipeline_mode=pl.Buffered(1))

    ws = [h[0] for h in heads]
    bs = [h[1] for h in heads]
    rider_specs, rider_shapes = _rider_specs(riders, m // tm)
    return pl.pallas_call(
        functools.partial(_projections_kernel, acts=tuple(h[3] for h in heads), riders=len(riders)),
        grid=(m // tm,),
        in_specs=([pl.BlockSpec((tm, k), lambda i: (i, 0))] + [resident(w) for w in ws] + [resident(b) for b in bs]
                  + rider_specs),
        out_specs=[pl.BlockSpec((tm, w.shape[1]), lambda i: (i, 0)) for w in ws] + rider_specs,
        out_shape=[jax.ShapeDtypeStruct((m, h[0].shape[1]), h[2]) for h in heads] + rider_shapes,
        compiler_params=_cparams("arbitrary"),
        name="projections",
    )(x, *ws, *bs, *riders)


def _linear_dilated_kernel(x_ref, w_ref, b_ref, o_ref, acc, *, dil):
    res = _dot(x_ref[...], w_ref[...]) + b_ref[...]
    slabs, rows, _ = acc.shape
    for c in range(slabs):
        acc[c] = res[:, c * LANES:(c + 1) * LANES]
    sub = rows // dil
    for r in range(dil):
        for c in range(slabs):
            o_ref[r, :, c * LANES:(c + 1) * LANES] = acc[c, pl.ds(r, sub, stride=dil), :].astype(o_ref.dtype)


def _linear_dilated(x, w, b, dil, batch, seq, tm=2048):
    k = x.shape[1]
    n = w.shape[1]
    tm = min(tm, seq)
    tiles_per_seq = seq // tm

    def resident(a):
        return pl.BlockSpec(a.shape, lambda i: (0, 0), pipeline_mode=pl.Buffered(1))

    return pl.pallas_call(
        functools.partial(_linear_dilated_kernel, dil=dil),
        grid=(batch * tiles_per_seq,),
        in_specs=[pl.BlockSpec((tm, k), lambda i: (i, 0)), resident(w), resident(b)],
        out_specs=pl.BlockSpec((None, dil, tm // dil, n),
                               lambda i: (i // tiles_per_seq, 0, i % tiles_per_seq, 0)),
        out_shape=jax.ShapeDtypeStruct((batch, dil, seq // dil, n), BF16),
        scratch_shapes=[pltpu.VMEM((n // LANES, tm, LANES), F32)],
        compiler_params=_cparams("parallel"),
        name=f"linear_dil{dil}",
    )(x, w, b)


def _rope_table_kernel(pos_ref, freq_ref, sign_ref, cos_ref, sin_ref):
    ang = pos_ref[...] * freq_ref[...]
    cos_ref[...] = jnp.cos(ang)
    sin_ref[...] = jnp.sin(ang) * sign_ref[...]


def _rope_tables(positions):
    n = positions.size
    half = ATT_HEAD_DIM // 2
    freqs = ROPE_THETA ** (-jnp.arange(half, dtype=F32) * (2.0 / ATT_HEAD_DIM))
    lane = np.arange(LANES)
    freq_row = freqs[lane % half][None, :]
    sign_row = jnp.asarray(np.where(lane % ATT_HEAD_DIM < half, -1.0, 1.0), F32)[None, :]
    pos = positions.astype(F32).reshape(n, 1)
    tm = 2048
    return pl.pallas_call(
        _rope_table_kernel,
        grid=(n // tm,),
        in_specs=[pl.BlockSpec((tm, 1), lambda i: (i, 0)),
                  pl.BlockSpec((1, LANES), lambda i: (0, 0)),
                  pl.BlockSpec((1, LANES), lambda i: (0, 0))],
        out_specs=[pl.BlockSpec((tm, LANES), lambda i: (i, 0))] * 2,
        out_shape=[jax.ShapeDtypeStruct((n, LANES), F32)] * 2,
        compiler_params=_cparams("parallel"),
        name="rope_tables",
    )(pos, freq_row, sign_row)


def _attn_kernel(q_ref, k_ref, v_ref, cos_ref, sin_ref, o_ref, lse_ref, qs, ks, vs, *, sub, dil):
    per_step = q_ref.shape[0]
    for rr in range(per_step):
        _attn_residue(q_ref.at[rr], k_ref.at[rr], v_ref.at[rr], cos_ref, sin_ref, o_ref.at[rr], lse_ref.at[rr],
                      qs.at[rr], ks.at[rr], vs.at[rr], pl.program_id(1) * per_step + rr, sub=sub, dil=dil)


def _attn_residue(q_ref, k_ref, v_ref, cos_ref, sin_ref, o_ref, lse_ref, qs, ks, vs, res, *, sub, dil):
    nb = sub // BLOCK
    lane = lax.broadcasted_iota(jnp.int32, (BLOCK, LANES), 1)
    first_half = (lane % ATT_HEAD_DIM) < (ATT_HEAD_DIM // 2)
    low_head = lane < ATT_HEAD_DIM

    def rope(x, c, s):
        partner = jnp.where(first_half, pltpu.roll(x, LANES - 32, 1), pltpu.roll(x, 32, 1))
        return x * c + partner * s

    ks[0:BLOCK, :] = jnp.zeros((BLOCK, ATT_GROUP_WIDTH), BF16)
    vs[0:BLOCK, :] = jnp.zeros((BLOCK, ATT_GROUP_WIDTH), BF16)

    def prep(n, carry):
        r0 = pl.multiple_of(n * BLOCK, BLOCK)
        if dil == 1:
            c = cos_ref[pl.ds(r0, BLOCK), :]
            s = sin_ref[pl.ds(r0, BLOCK), :]
        else:
            c = cos_ref[pl.ds(r0 * dil + res, BLOCK, stride=dil), :]
            s = sin_ref[pl.ds(r0 * dil + res, BLOCK, stride=dil), :]
        for pair in range(2):
            cols = slice(pair * LANES, (pair + 1) * LANES)
            q = rope(q_ref[pl.ds(r0, BLOCK), cols].astype(F32), c, s) * (ATT_HEAD_DIM ** -0.5)
            qs[pl.ds(r0, BLOCK), (2 * pair) * LANES:(2 * pair + 1) * LANES] = jnp.where(low_head, q, 0.0).astype(BF16)
            qs[pl.ds(r0, BLOCK), (2 * pair + 1) * LANES:(2 * pair + 2) * LANES] = jnp.where(low_head, 0.0, q).astype(BF16)
            k = rope(k_ref[pl.ds(r0, BLOCK), cols].astype(F32), c, s)
            ks[pl.ds(r0 + BLOCK, BLOCK), cols] = k.astype(BF16)
        vs[pl.ds(r0 + BLOCK, BLOCK), :] = v_ref[pl.ds(r0, BLOCK), :]
        return carry

    if nb <= 2:
        for n in range(nb):
            prep(n, 0)
    else:
        lax.fori_loop(0, nb, prep, 0)

    qi = lax.broadcasted_iota(jnp.int32, (BLOCK, 2 * BLOCK), 0)
    ki = lax.broadcasted_iota(jnp.int32, (BLOCK, 2 * BLOCK), 1)
    dist = qi + BLOCK - ki
    band = (dist >= 0) & (dist <= BLOCK)
    neg_inf = jnp.float32(-jnp.inf)
    bias_rest = jnp.where(band, 0.0, neg_inf)
    bias_first = jnp.where(band & (ki >= BLOCK), 0.0, neg_inf)

    def block(n, bias):
        r0 = pl.multiple_of(n * BLOCK, BLOCK)
        heads = range(ATT_HEADS_PER_GROUP)
        pair_cols = [slice(pair * LANES, (pair + 1) * LANES) for pair in range(2)]
        kw = [ks[pl.ds(r0, 2 * BLOCK), c] for c in pair_cols]
        vw = [vs[pl.ds(r0, 2 * BLOCK), c] for c in pair_cols]
        s = [_dot_nt(qs[pl.ds(r0, BLOCK), hd * LANES:(hd + 1) * LANES], kw[hd // 2]) + bias for hd in heads]
        m = [jnp.max(s[hd], axis=-1, keepdims=True) for hd in heads]
        p = [jnp.exp(s[hd] - m[hd]) for hd in heads]
        den = [jnp.sum(p[hd], axis=-1, keepdims=True) for hd in heads]
        o = [_dot(p[hd].astype(BF16), vw[hd // 2]) * (1.0 / den[hd]) for hd in heads]
        lse = [m[hd] + jnp.log(den[hd]) for hd in heads]
        for pair in range(2):
            lo, hi = 2 * pair, 2 * pair + 1
            o_ref[pl.ds(r0, BLOCK), pair_cols[pair]] = jnp.where(low_head, o[lo], o[hi]).astype(o_ref.dtype)
            lse_ref[pl.ds(r0, BLOCK), pair_cols[pair]] = jnp.where(low_head, lse[lo], lse[hi])

    block(0, bias_first)
    if nb <= 2:
        for n in range(1, nb):
            block(n, bias_rest)
        return

    def body(n, carry):
        block(n, bias_rest)
        return carry

    lax.fori_loop(1, nb, body, 0, unroll=4)


def _attention_group(qkv, cos, sin, group, batch, seq):
    window, dil = ATT_GROUPS[group]
    assert window // dil == BLOCK
    sub = seq // dil
    assert sub % BLOCK == 0

    per_step = min(dil, max(1, ATT_MIN_ROWS_PER_STEP // sub))

    def part(p):
        return pl.BlockSpec((None, per_step, sub, ATT_GROUP_WIDTH), lambda b, r: (b, r, 0, p))

    tab = pl.BlockSpec((seq, LANES), lambda b, r: (b, 0))
    out = pl.BlockSpec((None, per_step, sub, ATT_GROUP_WIDTH), lambda b, r: (b, r, 0, 0))
    return pl.pallas_call(
        functools.partial(_attn_kernel, sub=sub, dil=dil),
        grid=(batch, dil // per_step),
        in_specs=[part(0), part(1), part(2), tab, tab],
        out_specs=[out, out],
        out_shape=[jax.ShapeDtypeStruct((batch, dil, sub, ATT_GROUP_WIDTH), BF16),
                   jax.ShapeDtypeStruct((batch, dil, sub, ATT_GROUP_WIDTH), F32)],
        scratch_shapes=[pltpu.VMEM((per_step, sub, ATT_HEADS_PER_GROUP * LANES), BF16),
                        pltpu.VMEM((per_step, sub + BLOCK, ATT_GROUP_WIDTH), BF16),
                        pltpu.VMEM((per_step, sub + BLOCK, ATT_GROUP_WIDTH), BF16)],
        compiler_params=_cparams("parallel", "parallel"),
        name=f"attention_g{group}",
    )(qkv, qkv, qkv, cos, sin)


ML_SEQS_PER_STEP = 2


def _mlstm_kernel(qk_ref, v_ref, og_ref, g_ref, cw_ref, cb_ref, sc_ref, out_ref, frame, ct, m_s):
    taps = ML_CONV

    @pl.when(pl.program_id(1) == 0)
    def _():
        frame[:, 0:ML_CHUNK, :] = jnp.zeros((frame.shape[0], ML_CHUNK, 2 * ML_PAD_WIDTH), BF16)
        ct[...] = jnp.zeros(ct.shape, F32)
        m_s[...] = jnp.zeros(m_s.shape, F32)

    out_row = lax.broadcasted_iota(jnp.int32, (ML_CHUNK, 2 * ML_CHUNK), 0)
    src_row = lax.broadcasted_iota(jnp.int32, (ML_CHUNK, 2 * ML_CHUNK), 1)
    picks = [jnp.where(src_row == out_row + (ML_CHUNK - (taps - 1 - j)), 1.0, 0.0).astype(BF16)
             for j in range(taps - 1)]
    row = lax.broadcasted_iota(jnp.int32, (ML_CHUNK, ML_CHUNK), 0)
    colm = lax.broadcasted_iota(jnp.int32, (ML_CHUNK, ML_CHUNK), 1)
    causal = row >= colm
    tri = jnp.where(causal, 1.0, 0.0).astype(BF16)
    neg_inf = jnp.float32(-jnp.inf)

    seqs = range(qk_ref.shape[0])
    pairs = [(s, h) for s in seqs for h in range(ML_HEADS)]

    def head_cols(h):
        return slice(h * ML_HEAD_PAD, (h + 1) * ML_HEAD_PAD)

    act, gates, bcum, bcum_t, gates_t = {}, {}, {}, {}, {}
    for s in seqs:
        cur = qk_ref[s]
        frame[s, ML_CHUNK:2 * ML_CHUNK, :] = cur
        both = frame[s]
        acc = cb_ref[...] + cur.astype(F32) * cw_ref[taps - 1:taps, :]
        for j in range(taps - 1):
            acc = acc + _dot(picks[j], both) * cw_ref[j:j + 1, :]
        frame[s, 0:ML_CHUNK, :] = cur
        act[s] = (acc * _sigmoid(acc) * sc_ref[...]).astype(BF16)
        gates[s] = g_ref[s]
        log_f = -(jnp.maximum(-gates[s], 0.0) + jnp.log1p(jnp.exp(-jnp.abs(gates[s]))))
        hi, mid, lo = _split3(log_f)
        bcum[s] = _dot(tri, hi) + _dot(tri, mid) + _dot(tri, lo)
        bcum_t[s] = bcum[s].T
        gates_t[s] = gates[s].T

    qh, kh, vh, c_old, m_prev, b_c, i_c, log_d, m_inter, m = ({} for _ in range(10))
    for p in pairs:
        s, h = p
        qh[p] = act[s][:, head_cols(h)]
        kh[p] = act[s][:, ML_PAD_WIDTH + h * ML_HEAD_PAD:ML_PAD_WIDTH + (h + 1) * ML_HEAD_PAD]
        vh[p] = v_ref[s, :, head_cols(h)]
        c_old[p] = ct[s, h]
        m_prev[p] = m_s[s, h, 0:1, 0:1]
        b_c[p] = bcum[s][:, ML_HEADS + h:ML_HEADS + h + 1]
        b_r = bcum_t[s][ML_HEADS + h:ML_HEADS + h + 1, :]
        i_c[p] = gates[s][:, h:h + 1]
        i_r = gates_t[s][h:h + 1, :]
        log_d[p] = jnp.where(causal, b_c[p] - b_r + i_r, neg_inf)
        m_inter[p] = b_c[p] + m_prev[p]
        m[p] = jnp.maximum(m_inter[p], jnp.max(log_d[p], axis=-1, keepdims=True))

    scores = {p: _dot_nt(qh[p], kh[p]) for p in pairs}
    carried = {p: _dot(qh[p], c_old[p].astype(BF16)) for p in pairs}
    sc = {p: (scores[p] * jnp.exp(log_d[p] - m[p])).astype(BF16) for p in pairs}
    num = {p: _dot(sc[p], vh[p]) + jnp.exp(m_inter[p] - m[p]) * carried[p] for p in pairs}
    for p in pairs:
        s, h = p
        den = num[p][:, ML_HEAD_DIM:ML_HEAD_DIM + 1]
        hid = num[p] / jnp.maximum(jnp.abs(den), jnp.exp(-m[p]))
        out_ref[s, :, head_cols(h)] = (og_ref[s, :, head_cols(h)].astype(F32) * hid).astype(out_ref.dtype)
    for p in pairs:
        s, h = p
        b_last = b_c[p][ML_CHUNK - 1:ML_CHUNK, :]
        m_new = m[p][ML_CHUNK - 1:ML_CHUNK, :]
        w = jnp.exp(b_last - b_c[p] + i_c[p] - m_new)
        decay = jnp.exp(b_last + m_prev[p] - m_new)
        wv = (w * vh[p].astype(F32)).astype(BF16)
        ct[s, h] = decay * c_old[p] + _dot_tn(kh[p], wv)
        m_s[s, h] = jnp.broadcast_to(m_new, m_s.shape[2:])


def _mlstm(ml, og, gates, conv_w, conv_b, qk_scale, batch, seq):
    nc = seq // ML_CHUNK
    group = ML_SEQS_PER_STEP if batch % ML_SEQS_PER_STEP == 0 else 1

    def whole(a):
        return pl.BlockSpec(a.shape, lambda b, c: (0, 0))

    def chunk(width, col):
        return pl.BlockSpec((group, ML_CHUNK, width), lambda b, c: (b, c, col))

    out = pl.pallas_call(
        _mlstm_kernel,
        grid=(batch // group, nc),
        in_specs=[chunk(2 * ML_PAD_WIDTH, 0), chunk(ML_PAD_WIDTH, 2), chunk(ML_PAD_WIDTH, 0), chunk(LANES, 0),
                  whole(conv_w), whole(conv_b), whole(qk_scale)],
        out_specs=chunk(ML_PAD_WIDTH, 0),
        out_shape=jax.ShapeDtypeStruct((batch, seq, ML_PAD_WIDTH), BF16),
        scratch_shapes=[pltpu.VMEM((group, 2 * ML_CHUNK, 2 * ML_PAD_WIDTH), BF16),
                        pltpu.VMEM((group, ML_HEADS, ML_HEAD_PAD, ML_HEAD_PAD), F32),
                        pltpu.VMEM((group, ML_HEADS, 8, LANES), F32)],
        compiler_params=_cparams("parallel", "arbitrary"),
        name="mlstm",
    )(ml.reshape(batch, seq, -1), ml.reshape(batch, seq, -1), og.reshape(batch, seq, -1),
      gates.reshape(batch, seq, -1), conv_w, conv_b, qk_scale)
    return out.reshape(batch * seq, ML_PAD_WIDTH)


def _mixer_out_kernel(x_ref, o0_ref, l0_ref, o1_ref, l1_ref, o2_ref, l2_ref, sg_ref, hc_ref, gp_ref,
                      sgg_ref, sgb_ref, sgw_ref, sgbias_ref, wa_ref, wb_ref, wc_ref, wo_ref, lng_ref, lnb_ref,
                      xf_ref, xb_ref, unperm, *, alpha):
    tm = x_ref.shape[0]
    slabs = ATT_GROUP_WIDTH // LANES
    att_refs = (l0_ref, l1_ref, l2_ref, o0_ref, o1_ref, o2_ref)

    for slot, ref in enumerate(att_refs):
        dil = ref.shape[0]
        if dil > 1:
            for r in range(dil):
                for c in range(slabs):
                    unperm[slabs * slot + c, pl.ds(r, tm // dil, stride=dil), :] = (
                        ref[r, :, c * LANES:(c + 1) * LANES].astype(F32))

    def token_order(slot, rows):
        ref = att_refs[slot]
        if ref.shape[0] == 1:
            return ref[0, rows, :].astype(F32)
        return jnp.concatenate([unperm[slabs * slot + c, rows, :] for c in range(slabs)], axis=1)

    row = lax.broadcasted_iota(jnp.int32, (SG_CHUNK, SG_CHUNK), 0)
    colm = lax.broadcasted_iota(jnp.int32, (SG_CHUNK, SG_CHUNK), 1)
    causal = row >= colm
    group_dim = SG_WIDTH // SG_GROUPS
    sg_w = [jnp.where(causal, sgw_ref[g], 0.0).astype(BF16) for g in range(SG_GROUPS)]

    for h in range(tm // MIXER_SUB_ROWS):
        rows = slice(h * MIXER_SUB_ROWS, (h + 1) * MIXER_SUB_ROWS)

        l0, l1, l2 = token_order(0, rows), token_order(1, rows), token_order(2, rows)
        lmax = jnp.maximum(jnp.maximum(l0, l1), l2)
        e0, e1, e2 = jnp.exp(l0 - lmax), jnp.exp(l1 - lmax), jnp.exp(l2 - lmax)
        inv = 1.0 / (e0 + e1 + e2)
        y_a = ((e0 * inv) * token_order(3, rows) + (e1 * inv) * token_order(4, rows)
               + (e2 * inv) * token_order(5, rows))

        u = sg_ref[rows, 0:SG_WIDTH].astype(F32)
        v = sg_ref[rows, SG_WIDTH:2 * SG_WIDTH].astype(F32)
        v = _layer_norm(v, sgg_ref[...], sgb_ref[...]).astype(BF16)
        chunks = []
        for c in range(MIXER_SUB_ROWS // SG_CHUNK):
            crows = slice(c * SG_CHUNK, (c + 1) * SG_CHUNK)
            parts = [_dot(sg_w[g], v[crows, g * group_dim:(g + 1) * group_dim]) + sgbias_ref[:, g:g + 1]
                     for g in range(SG_GROUPS)]
            chunks.append(jnp.concatenate(parts, axis=1))
        y_b = u * jnp.concatenate(chunks, axis=0)

        def gate(i):
            return gp_ref[rows, i * D_MODEL:(i + 1) * D_MODEL].astype(F32)

        z = (gate(0) * _dot(y_a.astype(BF16), wa_ref[...])
             + gate(1) * _dot(y_b.astype(BF16), wb_ref[...])
             + gate(2) * _dot(hc_ref[rows, :], wc_ref[...]))
        mix = _dot(z.astype(BF16), wo_ref[...])
        out = _layer_norm(alpha * x_ref[rows, :] + mix, lng_ref[...], lnb_ref[...])
        xf_ref[rows, :] = out
        xb_ref[rows, :] = out.astype(BF16)


def _mixer_out(x, att_outs, sg, hc, gpre, p, alpha, seq, tm=512):
    n = x.shape[0]
    tiles_per_seq = seq // tm

    def rows(width):
        return pl.BlockSpec((tm, width), lambda i: (i, 0))

    def whole(a):
        return pl.BlockSpec(a.shape, lambda i: (0,) * a.ndim)

    def residue_major(dil):
        return pl.BlockSpec((None, dil, tm // dil, ATT_GROUP_WIDTH),
                            lambda i: (i // tiles_per_seq, 0, i % tiles_per_seq, 0))

    att_specs, att_args = [], []
    for (o, lse), (_, dil) in zip(att_outs, ATT_GROUPS):
        att_specs += [residue_major(dil)] * 2
        att_args += [o, lse]
    weights = [p["sg_ln_g"], p["sg_ln_b"], p["sg_w"], p["sg_bias_t"], p["w_br_a"], p["w_br_b"], p["w_br_c"],
               p["w_out"], p["ln_g0"], p["ln_b0"]]
    return pl.pallas_call(
        functools.partial(_mixer_out_kernel, alpha=alpha),
        grid=(n // tm,),
        in_specs=[rows(D_MODEL)] + att_specs
                 + [rows(2 * SG_WIDTH), rows(ML_PAD_WIDTH), rows(N_BRANCH * D_MODEL)]
                 + [whole(w) for w in weights],
        out_specs=[rows(D_MODEL), rows(D_MODEL)],
        out_shape=[jax.ShapeDtypeStruct((n, D_MODEL), F32), jax.ShapeDtypeStruct((n, D_MODEL), BF16)],
        scratch_shapes=[pltpu.VMEM((2 * len(ATT_GROUPS) * ATT_GROUP_WIDTH // LANES, tm, LANES), F32)],
        compiler_params=_cparams("parallel"),
        name="mixer_out",
    )(x, *att_args, sg, hc, gpre, *weights)


def _swiglu_tile(x, w1_ref, w3_ref, w2_ref, prod, tf):
    for c in range(D_FF // tf):
        cols = slice(c * tf, (c + 1) * tf)
        h1 = _dot(x, w1_ref[:, cols])
        h3 = _dot(x, w3_ref[:, cols])
        prod[:, cols] = (h1 * _sigmoid(h1) * h3).astype(BF16)
    return _dot(prod[...], w2_ref[...])


def _ffn_kernel(xf_ref, xb_ref, w1_ref, w3_ref, w2_ref, lng_ref, lnb_ref, *refs, alpha, tf, riders):
    rin, (of_ref, ob_ref), rout, prod = refs[:riders], refs[riders:riders + 2], refs[riders + 2:-1], refs[-1]
    f = _swiglu_tile(xb_ref[...], w1_ref, w3_ref, w2_ref, prod, tf)
    out = _layer_norm(alpha * xf_ref[...] + f, lng_ref[...], lnb_ref[...])
    of_ref[...] = out
    ob_ref[...] = out.astype(BF16)
    _cast_riders(rin, rout)


def _ffn(xf, xb, w1, w3, w2, ln_g, ln_b, alpha, riders=(), tm=512, tf=256):
    n = xf.shape[0]

    def resident(shape):
        return pl.BlockSpec(shape, lambda i: (0, 0), pipeline_mode=pl.Buffered(1))

    rider_specs, rider_shapes = _rider_specs(riders, n // tm)
    return pl.pallas_call(
        functools.partial(_ffn_kernel, alpha=alpha, tf=tf, riders=len(riders)),
        grid=(n // tm,),
        in_specs=[pl.BlockSpec((tm, D_MODEL), lambda i: (i, 0)),
                  pl.BlockSpec((tm, D_MODEL), lambda i: (i, 0)),
                  resident((D_MODEL, D_FF)), resident((D_MODEL, D_FF)), resident((D_FF, D_MODEL)),
                  pl.BlockSpec((1, D_MODEL), lambda i: (0, 0)),
                  pl.BlockSpec((1, D_MODEL), lambda i: (0, 0))] + rider_specs,
        out_specs=[pl.BlockSpec((tm, D_MODEL), lambda i: (i, 0))] * 2 + rider_specs,
        out_shape=[jax.ShapeDtypeStruct((n, D_MODEL), F32), jax.ShapeDtypeStruct((n, D_MODEL), BF16)]
                  + rider_shapes,
        scratch_shapes=[pltpu.VMEM((tm, D_FF), BF16)],
        compiler_params=_cparams("arbitrary"),
        name="ffn",
    )(xf, xb, w1, w3, w2, ln_g, ln_b, *riders)


def _router_kernel(x_ref, w_ref, b_ref, info_ref):
    xh, xm, _ = _split3(x_ref[...])
    wh, wm, _ = _split3(w_ref[...])
    logits = (_dot(xh, wh) + (_dot(xh, wm) + _dot(xm, wh))) + b_ref[...]
    lane = lax.broadcasted_iota(jnp.int32, logits.shape, 1)
    neg_inf = jnp.float32(-jnp.inf)
    logits = jnp.where(lane < N_EXPERTS, logits, neg_inf)
    v1 = jnp.max(logits, axis=-1, keepdims=True)
    i1 = jnp.min(jnp.where(logits == v1, lane, LANES), axis=-1, keepdims=True)
    rest = jnp.where(lane == i1, neg_inf, logits)
    v2 = jnp.max(rest, axis=-1, keepdims=True)
    i2 = jnp.min(jnp.where(rest == v2, lane, LANES), axis=-1, keepdims=True)
    e = jnp.exp(v2 - v1)
    inv = 1.0 / (1.0 + e)
    info_ref[...] = jnp.where(lane == 0, i1.astype(F32),
                              jnp.where(lane == 1, i2.astype(F32),
                                        jnp.where(lane == 2, inv, jnp.where(lane == 3, e * inv, 0.0))))


def _router(xf, router_w, router_b, tm=1024):
    n = xf.shape[0]
    w = jnp.pad(router_w, ((0, 0), (0, LANES - N_EXPERTS)))
    b = jnp.pad(router_b, (0, LANES - N_EXPERTS))[None, :]
    return pl.pallas_call(
        _router_kernel,
        grid=(n // tm,),
        in_specs=[pl.BlockSpec((tm, D_MODEL), lambda i: (i, 0)),
                  pl.BlockSpec((D_MODEL, LANES), lambda i: (0, 0)),
                  pl.BlockSpec((1, LANES), lambda i: (0, 0))],
        out_specs=pl.BlockSpec((tm, LANES), lambda i: (i, 0)),
        out_shape=jax.ShapeDtypeStruct((n, LANES), F32),
        compiler_params=_cparams("parallel"),
        name="router",
    )(xf, w, b)


def _routing_plan(info, n):
    tile = EXPERT_TILE
    n_tiles = TOP_K * n // tile + N_EXPERTS
    expert = info[:, 0:TOP_K].astype(jnp.int32).reshape(-1)
    onehot = (expert[:, None] == jnp.arange(N_EXPERTS, dtype=jnp.int32)[None, :]).astype(jnp.int32)
    running = jnp.cumsum(onehot, axis=0)
    rank = jnp.sum(onehot * running, axis=1) - 1
    counts = running[-1]
    padded = ((counts + tile - 1) // tile) * tile
    ends = jnp.cumsum(padded)
    starts = ends - padded
    dest = jnp.sum(onehot * starts[None, :], axis=1) + rank
    used = ends[-1] // tile
    tile_start = jnp.arange(n_tiles, dtype=jnp.int32) * tile
    tile_expert = jnp.sum((tile_start[:, None] >= ends[None, :]).astype(jnp.int32), axis=1)
    last_expert = jnp.sum((tile_start[used - 1] >= ends).astype(jnp.int32))
    tile_expert = jnp.minimum(tile_expert, last_expert)
    plan = dict(dest=dest, tile_expert=tile_expert, used=used.reshape(1),
                pad_start=starts + counts, pad_count=padded - counts)
    return {k: v.astype(jnp.int32) for k, v in plan.items()}, n_tiles


def _rows_wait(src, dst, rows, sem):
    pltpu.make_async_copy(src.at[pl.ds(0, rows)], dst.at[pl.ds(0, rows)], sem).wait()


ROW_SUBLANES = D_MODEL // LANES


def _tile_copy(src, src_row, dst, dst_row, sem):
    s0 = pl.multiple_of(src_row * ROW_SUBLANES, ROW_SUBLANES)
    d0 = pl.multiple_of(dst_row * ROW_SUBLANES, ROW_SUBLANES)
    return pltpu.make_async_copy(src.at[pl.ds(s0, ROW_SUBLANES)], dst.at[pl.ds(d0, ROW_SUBLANES)], sem)


def _to_row_tiles(ref, x):
    m = x.shape[0]
    for c in range(ROW_SUBLANES):
        ref[pl.ds(c, m, stride=ROW_SUBLANES), :] = x[:, c * LANES:(c + 1) * LANES]


def _from_row_tiles(ref):
    m = ref.shape[0] // ROW_SUBLANES
    return jnp.concatenate([ref[pl.ds(c, m, stride=ROW_SUBLANES), :] for c in range(ROW_SUBLANES)], axis=1)


DISPATCH_SLOTS = 3


def _dispatch_kernel(pad_start_ref, pad_count_ref, used_ref, dest_ref, x_ref, xs_ref, buf, tiles, zero_row,
                     load_sem, row_sem, pad_sem, *, tm):
    i = pl.program_id(0)

    @pl.when(i == 0)
    def _():
        zero_row[...] = jnp.zeros(zero_row.shape, F32)
        for e in range(N_EXPERTS):
            def fill(j, carry, e=e):
                _tile_copy(zero_row, 0, xs_ref, pad_start_ref[e] + j, pad_sem).start()
                return carry
            lax.fori_loop(0, pad_count_ref[e], fill, 0)
        for e in range(N_EXPERTS):
            def drain(j, carry):
                _tile_copy(zero_row, 0, xs_ref, 0, pad_sem).wait()
                return carry
            lax.fori_loop(0, pad_count_ref[e], drain, 0)

        tile_rows = tm * ROW_SUBLANES
        spare = xs_ref.shape[0] // tile_rows - used_ref[0]
        tiles[1] = jnp.zeros(tiles.shape[1:], F32)

        def zero_tile(j):
            row0 = pl.multiple_of((used_ref[0] + j) * tile_rows, tile_rows)
            return pltpu.make_async_copy(tiles.at[1], xs_ref.at[pl.ds(row0, tile_rows)], pad_sem)

        def fill_tile(j, carry):
            zero_tile(j).start()
            return carry

        def drain_tile(j, carry):
            zero_tile(j).wait()
            return carry

        lax.fori_loop(0, spare, fill_tile, 0)
        lax.fori_loop(0, spare, drain_tile, 0)

    steps = pl.num_programs(0)
    slot = i % DISPATCH_SLOTS
    prev_slot = (i + DISPATCH_SLOTS - 1) % DISPATCH_SLOTS
    stage = i % 2

    def load(tile, into):
        return pltpu.make_async_copy(x_ref.at[pl.ds(tile * tm, tm), :], buf.at[into], load_sem.at[into])

    def wait_rows(of_stage):
        for _ in range(TOP_K):
            _rows_wait(tiles.at[of_stage], xs_ref, tm * ROW_SUBLANES, row_sem.at[of_stage])

    @pl.when(i == 0)
    def _():
        load(0, 0).start()

        @pl.when(steps > 1)
        def _():
            load(1, 1).start()

    load(i, slot).wait()
    _to_row_tiles(tiles.at[stage], buf[slot])

    def start(t, carry):
        for c in range(TOP_K):
            _tile_copy(tiles.at[stage], t, xs_ref, dest_ref[0, 0, TOP_K * t + c], row_sem.at[stage]).start(
                priority=c % 2)
        return carry

    lax.fori_loop(0, tm, start, 0, unroll=DMA_ISSUE_UNROLL)

    @pl.when(i > 0)
    def _():
        wait_rows(1 - stage)

    @pl.when(i + 2 < steps)
    def _():
        load(i + 2, prev_slot).start()

    @pl.when(i == steps - 1)
    def _():
        wait_rows(stage)


def _dispatch(xf, dest, pad_start, pad_count, used, n_slots, tm=EXPERT_TILE):
    n = xf.shape[0]
    dest3 = dest.reshape(n // tm, 1, TOP_K * tm)
    grid_spec = pltpu.PrefetchScalarGridSpec(
        num_scalar_prefetch=3,
        grid=(n // tm,),
        in_specs=[pl.BlockSpec((1, 1, TOP_K * tm), lambda i, ps, pc, nu: (i, 0, 0), memory_space=pltpu.SMEM),
                  pl.BlockSpec(memory_space=pl.ANY)],
        out_specs=pl.BlockSpec(memory_space=pl.ANY),
        scratch_shapes=[pltpu.VMEM((DISPATCH_SLOTS, tm, D_MODEL), F32),
                        pltpu.VMEM((2, tm * ROW_SUBLANES, LANES), F32),
                        pltpu.VMEM((ROW_SUBLANES, LANES), F32),
                        pltpu.SemaphoreType.DMA((DISPATCH_SLOTS,)),
                        pltpu.SemaphoreType.DMA((2,)),
                        pltpu.SemaphoreType.DMA],
    )
    return pl.pallas_call(
        functools.partial(_dispatch_kernel, tm=tm),
        grid_spec=grid_spec,
        out_shape=jax.ShapeDtypeStruct((n_slots * ROW_SUBLANES, LANES), F32),
        compiler_params=_cparams("arbitrary"),
        name="moe_dispatch",
    )(pad_start, pad_count, used, dest3, xf)


def _experts_kernel(te_ref, used_ref, xs_ref, w1_ref, w3_ref, w2_ref, ys_ref, prod, *, tf):
    del te_ref
    i = pl.program_id(0)

    @pl.when(i < used_ref[0])
    def _():
        x = _from_row_tiles(xs_ref).astype(BF16)
        _to_row_tiles(ys_ref, _swiglu_tile(x, w1_ref, w3_ref, w2_ref, prod, tf))

    @pl.when(i >= used_ref[0])
    def _():
        ys_ref[...] = jnp.zeros(ys_ref.shape, ys_ref.dtype)


def _experts(xs, tile_expert, used, w1, w3, w2, n_tiles, tf=256):
    tile = EXPERT_TILE
    grid_spec = pltpu.PrefetchScalarGridSpec(
        num_scalar_prefetch=2,
        grid=(n_tiles,),
        in_specs=[pl.BlockSpec((tile * ROW_SUBLANES, LANES), lambda i, te, nu: (jnp.minimum(i, nu[0] - 1), 0)),
                  pl.BlockSpec((None, D_MODEL, D_FF), lambda i, te, nu: (te[i], 0, 0)),
                  pl.BlockSpec((None, D_MODEL, D_FF), lambda i, te, nu: (te[i], 0, 0)),
                  pl.BlockSpec((None, D_FF, D_MODEL), lambda i, te, nu: (te[i], 0, 0))],
        out_specs=pl.BlockSpec((tile * ROW_SUBLANES, LANES), lambda i, te, nu: (i, 0)),
        scratch_shapes=[pltpu.VMEM((tile, D_FF), BF16)],
    )
    return pl.pallas_call(
        functools.partial(_experts_kernel, tf=tf),
        grid_spec=grid_spec,
        out_shape=jax.ShapeDtypeStruct((n_tiles * tile * ROW_SUBLANES, LANES), F32),
        compiler_params=_cparams("arbitrary"),
        name="moe_experts",
    )(tile_expert, used, xs, w1, w3, w2)


def _combine_kernel(dest_ref, next_dest_ref, xf_ref, info_ref, ys_ref, lng_ref, lnb_ref, of_ref, got, sem, *, alpha):
    tm = xf_ref.shape[0]
    i = pl.program_id(0)
    slot = i % 2

    def gather(dests, into):
        def start(t, carry):
            for c in range(TOP_K):
                _tile_copy(ys_ref, dests[0, 0, TOP_K * t + c], got.at[into, c], t, sem.at[into]).start(
                    priority=c % 2)
            return carry
        lax.fori_loop(0, tm, start, 0, unroll=DMA_ISSUE_UNROLL)

    @pl.when(i == 0)
    def _():
        gather(dest_ref, 0)

    @pl.when(i + 1 < pl.num_programs(0))
    def _():
        gather(next_dest_ref, 1 - slot)

    for c in range(TOP_K):
        _rows_wait(ys_ref, got.at[slot, c], tm * ROW_SUBLANES, sem.at[slot])
    info = info_ref[...]
    mix = (info[:, 2:3] * _from_row_tiles(got.at[slot, 0]) + info[:, 3:4] * _from_row_tiles(got.at[slot, 1]))
    of_ref[...] = _layer_norm(alpha * xf_ref[...] + mix, lng_ref[...], lnb_ref[...])


def _combine(xf, info, ys, dest, ln_g, ln_b, alpha, tm=256):
    n = xf.shape[0]
    steps = n // tm
    dest3 = dest.reshape(steps, 1, TOP_K * tm)
    return pl.pallas_call(
        functools.partial(_combine_kernel, alpha=alpha),
        grid=(steps,),
        in_specs=[pl.BlockSpec((1, 1, TOP_K * tm), lambda i: (i, 0, 0), memory_space=pltpu.SMEM),
                  pl.BlockSpec((1, 1, TOP_K * tm), lambda i: (jnp.minimum(i + 1, steps - 1), 0, 0),
                               memory_space=pltpu.SMEM),
                  pl.BlockSpec((tm, D_MODEL), lambda i: (i, 0)),
                  pl.BlockSpec((tm, LANES), lambda i: (i, 0)),
                  pl.BlockSpec(memory_space=pl.ANY),
                  pl.BlockSpec((1, D_MODEL), lambda i: (0, 0)),
                  pl.BlockSpec((1, D_MODEL), lambda i: (0, 0))],
        out_specs=pl.BlockSpec((tm, D_MODEL), lambda i: (i, 0)),
        out_shape=jax.ShapeDtypeStruct((n, D_MODEL), F32),
        scratch_shapes=[pltpu.VMEM((2, TOP_K, tm * ROW_SUBLANES, LANES), F32), pltpu.SemaphoreType.DMA((2,))],
        compiler_params=_cparams("arbitrary"),
        name="moe_combine",
    )(dest3, dest3, xf, info, ys, ln_g, ln_b)


def _moe(xf, router_w, router_b, w1, w3, w2, ln_g, ln_b, alpha):
    n = xf.shape[0]
    info = _router(xf, router_w, router_b)
    plan, n_tiles = _routing_plan(info, n)
    xs = _dispatch(xf, plan["dest"], plan["pad_start"], plan["pad_count"], plan["used"], n_tiles * EXPERT_TILE)
    ys = _experts(xs, plan["tile_expert"], plan["used"], w1, w3, w2, n_tiles)
    return _combine(xf, info, ys, plan["dest"], ln_g, ln_b, alpha)


def _pad_heads(w):
    lead = w.shape[:-1]
    w = w.reshape(*lead, ML_HEADS, ML_HEAD_DIM)
    w = jnp.pad(w, [(0, 0)] * len(lead) + [(0, 0), (0, ML_HEAD_PAD - ML_HEAD_DIM)])
    return w.reshape(*lead, ML_PAD_WIDTH)


def _group_qkv(w, group):
    gw = ATT_GROUP_WIDTH
    return jnp.concatenate([w[..., part * ATT_WIDTH + group * gw:part * ATT_WIDTH + (group + 1) * gw]
                            for part in range(3)], axis=-1)


def _layer_params(layer, w_in, b_in, conv_w, conv_b, sg_ln_g, sg_ln_b, sg_w, sg_b,
                  w_br_a, w_br_b, w_br_c, w_out, ln_g, ln_b):
    w, b = w_in[layer].astype(BF16), b_in[layer]
    o_att, o_sg, o_qk, o_v, o_o, o_i, o_g = 0, 2304, 3840, 5376, 6144, 6912, 6920
    ones_col = jnp.zeros((ML_HEADS, ML_HEAD_PAD), F32).at[:, ML_HEAD_DIM].set(1.0).reshape(ML_PAD_WIDTH)
    w_qk = jnp.concatenate([_pad_heads(w[:, o_qk:o_qk + ML_WIDTH]), _pad_heads(w[:, o_qk + ML_WIDTH:o_v])], axis=1)
    b_qk = jnp.concatenate([_pad_heads(b[o_qk:o_qk + ML_WIDTH]), _pad_heads(b[o_qk + ML_WIDTH:o_v])])
    qk_scale = jnp.concatenate([jnp.ones((ML_PAD_WIDTH,), F32),
                                jnp.full((ML_PAD_WIDTH,), ML_HEAD_DIM ** -0.5, F32)])
    cw, cb = conv_w[layer], conv_b[layer]
    w_att, b_att = w[:, o_att:o_sg], b[o_att:o_sg]
    return dict(
        w_att=[_group_qkv(w_att, g) for g in range(len(ATT_GROUPS))],
        b_att=[_group_qkv(b_att, g)[None, :] for g in range(len(ATT_GROUPS))],
        w_sg=w[:, o_sg:o_qk], b_sg=b[None, o_sg:o_qk],
        w_ml=jnp.concatenate([w_qk, _pad_heads(w[:, o_v:o_o])], axis=1),
        b_ml=jnp.concatenate([b_qk, _pad_heads(b[o_v:o_o]) + ones_col])[None, :],
        qk_scale=qk_scale[None, :],
        w_mlo=_pad_heads(w[:, o_o:o_i]), b_mlo=_pad_heads(b[o_o:o_i])[None, :],
        w_if=jnp.pad(w[:, o_i:o_g], ((0, 0), (0, LANES - 2 * ML_HEADS))),
        b_if=jnp.pad(b[o_i:o_g], (0, LANES - 2 * ML_HEADS))[None, :],
        w_g=w[:, o_g:], b_g=b[None, o_g:],
        conv_w=jnp.concatenate([_pad_heads(cw[:, :ML_WIDTH]), _pad_heads(cw[:, ML_WIDTH:])], axis=1),
        conv_b=jnp.concatenate([_pad_heads(cb[:ML_WIDTH]), _pad_heads(cb[ML_WIDTH:])])[None, :],
        sg_ln_g=sg_ln_g[layer][None, :], sg_ln_b=sg_ln_b[layer][None, :],
        sg_w=sg_w[layer], sg_bias_t=sg_b[layer].T,
        w_br_a=w_br_a[layer].astype(BF16), w_br_b=w_br_b[layer].astype(BF16),
        w_br_c=_pad_heads(w_br_c[layer].T).T.astype(BF16),
        w_out=w_out[layer].astype(BF16),
        ln_g0=ln_g[layer, 0][None, :], ln_b0=ln_b[layer, 0][None, :],
        ln_g1=ln_g[layer, 1][None, :], ln_b1=ln_b[layer, 1][None, :],
    )


def kernel(x, positions, w_in, b_in, conv_w, conv_b, sg_ln_g, sg_ln_b, sg_w, sg_b, w_br_a, w_br_b, w_br_c, w_out,
           ln_g, ln_b, ffn_w1, ffn_w3, ffn_w2, router_w, router_b, moe_w1, moe_w3, moe_w2):
    batch, seq, _ = x.shape
    n = batch * seq
    depth = w_in.shape[0]
    alpha = (2.0 * depth) ** 0.25
    cos, sin = _rope_tables(positions)
    xf = x.reshape(n, D_MODEL)
    xb = xf.astype(BF16)
    for layer in range(depth):
        p = _layer_params(layer, w_in, b_in, conv_w, conv_b, sg_ln_g, sg_ln_b, sg_w, sg_b,
                          w_br_a, w_br_b, w_br_c, w_out, ln_g, ln_b)
        undilated = [g for g, (_, dil) in enumerate(ATT_GROUPS) if dil == 1]
        j = layer // 2
        if layer % 2 == 0:
            riders = [ffn_w1[j], ffn_w3[j], ffn_w2[j]]
        else:
            riders = [moe_w2[j].reshape(N_EXPERTS * D_FF, D_MODEL)]
        sg, ml, og, gates_if, gbr, *rest = _projections(xb, [
            (p["w_sg"], p["b_sg"], BF16, _gelu_tanh),
            (p["w_ml"], p["b_ml"], BF16, None),
            (p["w_mlo"], p["b_mlo"], BF16, _sigmoid),
            (p["w_if"], p["b_if"], F32, None),
            (p["w_g"], p["b_g"], BF16, _sigmoid),
        ] + [(p["w_att"][g], p["b_att"][g], BF16, None) for g in undilated],
            riders=riders)
        qkv_plain, cast = rest[:len(undilated)], rest[len(undilated):]
        att_outs = []
        for g, (_, dil) in enumerate(ATT_GROUPS):
            if dil == 1:
                qkv = qkv_plain[undilated.index(g)].reshape(batch, 1, seq, ATT_PARTS_WIDTH)
            else:
                qkv = _linear_dilated(xb, p["w_att"][g], p["b_att"][g], dil, batch, seq)
            att_outs.append(_attention_group(qkv, cos, sin, g, batch, seq))
        hc = _mlstm(ml, og, gates_if, p["conv_w"], p["conv_b"], p["qk_scale"], batch, seq)
        xf, xb = _mixer_out(xf, att_outs, sg, hc, gbr, p, alpha, seq)
        if layer % 2 == 0:
            next_moe = (layer + 1) // 2 if layer + 1 < depth else None
            riders = [] if next_moe is None else [moe_w1[next_moe].reshape(N_EXPERTS * D_MODEL, D_FF),
                                                  moe_w3[next_moe].reshape(N_EXPERTS * D_MODEL, D_FF)]
            xf, xb, *moe_cast = _ffn(xf, xb, *cast, p["ln_g1"], p["ln_b1"], alpha, riders=riders)
        else:
            if layer == 0:
                moe_cast = [moe_w1[j].astype(BF16), moe_w3[j].astype(BF16)]
            w1, w3 = (m.reshape(N_EXPERTS, D_MODEL, D_FF) for m in moe_cast)
            w2 = cast[0].reshape(N_EXPERTS, D_FF, D_MODEL)
            xf = _moe(xf, router_w[j], router_b[j], w1, w3, w2, p["ln_g1"], p["ln_b1"], alpha)
            xb = xf.astype(BF16)
    return xf.reshape(batch, seq, D_MODEL)
```

```python
import functools

import numpy as np
import jax
import jax.numpy as jnp
from jax import lax
from jax.experimental import pallas as pl
from jax.experimental.pallas import tpu as pltpu

F32 = jnp.float32
BF16 = jnp.bfloat16

D_MODEL = 1024
ATT_GROUPS = ((128, 1), (512, 4), (2048, 16))
ATT_HEADS_PER_GROUP = 4
ATT_HEAD_DIM = 64
ATT_WIDTH = 768
ATT_GROUP_WIDTH = ATT_HEADS_PER_GROUP * ATT_HEAD_DIM
ATT_PARTS_WIDTH = 3 * ATT_GROUP_WIDTH
ROPE_THETA = 10000.0
SG_CHUNK = 128
SG_GROUPS = 6
SG_WIDTH = 768
ML_HEADS = 4
ML_HEAD_DIM = 192
ML_HEAD_PAD = 256
ML_WIDTH = 768
ML_PAD_WIDTH = ML_HEADS * ML_HEAD_PAD
ML_CHUNK = 128
ML_CONV = 4
N_BRANCH = 3
D_FF = 2816
N_EXPERTS = 8
TOP_K = 2
LN_EPS = 1e-5
BLOCK = 128
ATT_MIN_ROWS_PER_STEP = 1024
LANES = 128
VMEM_LIMIT = 56 * 1024 * 1024
EXPERT_TILE = 512
DMA_ISSUE_UNROLL = 8
MIXER_SUB_ROWS = 256


def _cparams(*sem):
    return pltpu.CompilerParams(dimension_semantics=sem, vmem_limit_bytes=VMEM_LIMIT)


def _layer_norm(x, g, b):
    mu = jnp.mean(x, axis=-1, keepdims=True)
    xc = x - mu
    var = jnp.mean(xc * xc, axis=-1, keepdims=True)
    return xc * lax.rsqrt(var + LN_EPS) * g + b


def _gelu_tanh(x):
    return 0.5 * x * (1.0 + jnp.tanh(np.sqrt(2.0 / np.pi) * (x + 0.044715 * (x * x * x))))


def _sigmoid(x):
    return 0.5 * (1.0 + jnp.tanh(0.5 * x))


def _dot(a, b):
    return jnp.dot(a, b, preferred_element_type=F32)


def _dot_nt(a, b):
    return lax.dot_general(a, b, (((1,), (1,)), ((), ())), preferred_element_type=F32)


def _dot_tn(a, b):
    return lax.dot_general(a, b, (((0,), (0,)), ((), ())), preferred_element_type=F32)


def _split3(x):
    hi = x.astype(BF16)
    r1 = x - hi.astype(F32)
    mid = r1.astype(BF16)
    lo = (r1 - mid.astype(F32)).astype(BF16)
    return hi, mid, lo


def _rider_specs(arrays, steps):
    specs, shapes = [], []
    for a in arrays:
        span = 1
        while (a.shape[0] * span) % steps or (a.shape[0] * span // steps) % 16:
            span *= 2
        rows = a.shape[0] * span // steps
        specs.append(pl.BlockSpec((rows, a.shape[1]), lambda i, span=span: (i // span, 0)))
        shapes.append(jax.ShapeDtypeStruct(a.shape, BF16))
    return specs, shapes


def _cast_riders(in_refs, out_refs):
    for i_ref, o_ref in zip(in_refs, out_refs):
        o_ref[...] = i_ref[...].astype(o_ref.dtype)


def _projections_kernel(x_ref, *refs, acts, riders):
    n = len(acts)
    ws, bs, rin = refs[:n], refs[n:2 * n], refs[2 * n:2 * n + riders]
    outs, rout = refs[2 * n + riders:3 * n + riders], refs[3 * n + riders:]
    x = x_ref[...]
    for w_ref, b_ref, o_ref, act in zip(ws, bs, outs, acts):
        res = _dot(x, w_ref[...]) + b_ref[...]
        o_ref[...] = (res if act is None else act(res)).astype(o_ref.dtype)
    _cast_riders(rin, rout)


def _projections(x, heads, riders=(), tm=512):
    m, k = x.shape

    def resident(a):
        return pl.BlockSpec(a.shape, lambda i: (0, 0), pipeline_mode=pl.Buffered(1))

    ws = [h[0] for h in heads]
    bs = [h[1] for h in heads]
    rider_specs, rider_shapes = _rider_specs(riders, m // tm)
    return pl.pallas_call(
        functools.partial(_projections_kernel, acts=tuple(h[3] for h in heads), riders=len(riders)),
        grid=(m // tm,),
        in_specs=([pl.BlockSpec((tm, k), lambda i: (i, 0))] + [resident(w) for w in ws] + [resident(b) for b in bs]
                  + rider_specs),
        out_specs=[pl.BlockSpec((tm, w.shape[1]), lambda i: (i, 0)) for w in ws] + rider_specs,
        out_shape=[jax.ShapeDtypeStruct((m, h[0].shape[1]), h[2]) for h in heads] + rider_shapes,
        compiler_params=_cparams("arbitrary"),
        name="projections",
    )(x, *ws, *bs, *riders)


def _linear_dilated_kernel(x_ref, w_ref, b_ref, o_ref, acc, *, dil):
    res = _dot(x_ref[...], w_ref[...]) + b_ref[...]
    slabs, rows, _ = acc.shape
    for c in range(slabs):
        acc[c] = res[:, c * LANES:(c + 1) * LANES]
    sub = rows // dil
    for r in range(dil):
        for c in range(slabs):
            o_ref[r, :, c * LANES:(c + 1) * LANES] = acc[c, pl.ds(r, sub, stride=dil), :].astype(o_ref.dtype)


def _linear_dilated(x, w, b, dil, batch, seq, tm=2048):
    k = x.shape[1]
    n = w.shape[1]
    tm = min(tm, seq)
    tiles_per_seq = seq // tm

    def resident(a):
        return pl.BlockSpec(a.shape, lambda i: (0, 0), pipeline_mode=pl.Buffered(1))

    return pl.pallas_call(
        functools.partial(_linear_dilated_kernel, dil=dil),
        grid=(batch * tiles_per_seq,),
        in_specs=[pl.BlockSpec((tm, k), lambda i: (i, 0)), resident(w), resident(b)],
        out_specs=pl.BlockSpec((None, dil, tm // dil, n),
                               lambda i: (i // tiles_per_seq, 0, i % tiles_per_seq, 0)),
        out_shape=jax.ShapeDtypeStruct((batch, dil, seq // dil, n), BF16),
        scratch_shapes=[pltpu.VMEM((n // LANES, tm, LANES), F32)],
        compiler_params=_cparams("parallel"),
        name=f"linear_dil{dil}",
    )(x, w, b)


def _rope_table_kernel(pos_ref, freq_ref, sign_ref, cos_ref, sin_ref):
    ang = pos_ref[...] * freq_ref[...]
    cos_ref[...] = jnp.cos(ang)
    sin_ref[...] = jnp.sin(ang) * sign_ref[...]


def _rope_tables(positions):
    n = positions.size
    half = ATT_HEAD_DIM // 2
    freqs = ROPE_THETA ** (-jnp.arange(half, dtype=F32) * (2.0 / ATT_HEAD_DIM))
    lane = np.arange(LANES)
    freq_row = freqs[lane % half][None, :]
    sign_row = jnp.asarray(np.where(lane % ATT_HEAD_DIM < half, -1.0, 1.0), F32)[None, :]
    pos = positions.astype(F32).reshape(n, 1)
    tm = 2048
    return pl.pallas_call(
        _rope_table_kernel,
        grid=(n // tm,),
        in_specs=[pl.BlockSpec((tm, 1), lambda i: (i, 0)),
                  pl.BlockSpec((1, LANES), lambda i: (0, 0)),
                  pl.BlockSpec((1, LANES), lambda i: (0, 0))],
        out_specs=[pl.BlockSpec((tm, LANES), lambda i: (i, 0))] * 2,
        out_shape=[jax.ShapeDtypeStruct((n, LANES), F32)] * 2,
        compiler_params=_cparams("parallel"),
        name="rope_tables",
    )(pos, freq_row, sign_row)


def _attn_kernel(q_ref, k_ref, v_ref, cos_ref, sin_ref, o_ref, lse_ref, qs, ks, vs, *, sub, dil):
    per_step = q_ref.shape[0]
    for rr in range(per_step):
        _attn_residue(q_ref.at[rr], k_ref.at[rr], v_ref.at[rr], cos_ref, sin_ref, o_ref.at[rr], lse_ref.at[rr],
                      qs.at[rr], ks.at[rr], vs.at[rr], pl.program_id(1) * per_step + rr, sub=sub, dil=dil)


def _attn_residue(q_ref, k_ref, v_ref, cos_ref, sin_ref, o_ref, lse_ref, qs, ks, vs, res, *, sub, dil):
    nb = sub // BLOCK
    lane = lax.broadcasted_iota(jnp.int32, (BLOCK, LANES), 1)
    first_half = (lane % ATT_HEAD_DIM) < (ATT_HEAD_DIM // 2)
    low_head = lane < ATT_HEAD_DIM

    def rope(x, c, s):
        partner = jnp.where(first_half, pltpu.roll(x, LANES - 32, 1), pltpu.roll(x, 32, 1))
        return x * c + partner * s

    ks[0:BLOCK, :] = jnp.zeros((BLOCK, ATT_GROUP_WIDTH), BF16)
    vs[0:BLOCK, :] = jnp.zeros((BLOCK, ATT_GROUP_WIDTH), BF16)

    def prep(n, carry):
        r0 = pl.multiple_of(n * BLOCK, BLOCK)
        if dil == 1:
            c = cos_ref[pl.ds(r0, BLOCK), :]
            s = sin_ref[pl.ds(r0, BLOCK), :]
        else:
            c = cos_ref[pl.ds(r0 * dil + res, BLOCK, stride=dil), :]
            s = sin_ref[pl.ds(r0 * dil + res, BLOCK, stride=dil), :]
        for pair in range(2):
            cols = slice(pair * LANES, (pair + 1) * LANES)
            q = rope(q_ref[pl.ds(r0, BLOCK), cols].astype(F32), c, s) * (ATT_HEAD_DIM ** -0.5)
            qs[pl.ds(r0, BLOCK), (2 * pair) * LANES:(2 * pair + 1) * LANES] = jnp.where(low_head, q, 0.0).astype(BF16)
            qs[pl.ds(r0, BLOCK), (2 * pair + 1) * LANES:(2 * pair + 2) * LANES] = jnp.where(low_head, 0.0, q).astype(BF16)
            k = rope(k_ref[pl.ds(r0, BLOCK), cols].astype(F32), c, s)
            ks[pl.ds(r0 + BLOCK, BLOCK), cols] = k.astype(BF16)
        vs[pl.ds(r0 + BLOCK, BLOCK), :] = v_ref[pl.ds(r0, BLOCK), :]
        return carry

    if nb <= 2:
        for n in range(nb):
            prep(n, 0)
    else:
        lax.fori_loop(0, nb, prep, 0)

    qi = lax.broadcasted_iota(jnp.int32, (BLOCK, 2 * BLOCK), 0)
    ki = lax.broadcasted_iota(jnp.int32, (BLOCK, 2 * BLOCK), 1)
    dist = qi + BLOCK - ki
    band = (dist >= 0) & (dist <= BLOCK)
    neg_inf = jnp.float32(-jnp.inf)
    bias_rest = jnp.where(band, 0.0, neg_inf)
    bias_first = jnp.where(band & (ki >= BLOCK), 0.0, neg_inf)

    def block(n, bias):
        r0 = pl.multiple_of(n * BLOCK, BLOCK)
        heads = range(ATT_HEADS_PER_GROUP)
        pair_cols = [slice(pair * LANES, (pair + 1) * LANES) for pair in range(2)]
        kw = [ks[pl.ds(r0, 2 * BLOCK), c] for c in pair_cols]
        vw = [vs[pl.ds(r0, 2 * BLOCK), c] for c in pair_cols]
        s = [_dot_nt(qs[pl.ds(r0, BLOCK), hd * LANES:(hd + 1) * LANES], kw[hd // 2]) + bias for hd in heads]
        m = [jnp.max(s[hd], axis=-1, keepdims=True) for hd in heads]
        p = [jnp.exp(s[hd] - m[hd]) for hd in heads]
        den = [jnp.sum(p[hd], axis=-1, keepdims=True) for hd in heads]
        o = [_dot(p[hd].astype(BF16), vw[hd // 2]) * (1.0 / den[hd]) for hd in heads]
        lse = [m[hd] + jnp.log(den[hd]) for hd in heads]
        for pair in range(2):
            lo, hi = 2 * pair, 2 * pair + 1
            o_ref[pl.ds(r0, BLOCK), pair_cols[pair]] = jnp.where(low_head, o[lo], o[hi]).astype(o_ref.dtype)
            lse_ref[pl.ds(r0, BLOCK), pair_cols[pair]] = jnp.where(low_head, lse[lo], lse[hi])

    block(0, bias_first)
    if nb <= 2:
        for n in range(1, nb):
            block(n, bias_rest)
        return

    def body(n, carry):
        block(n, bias_rest)
        return carry

    lax.fori_loop(1, nb, body, 0, unroll=4)


def _attention_group(qkv, cos, sin, group, batch, seq):
    window, dil = ATT_GROUPS[group]
    assert window // dil == BLOCK
    sub = seq // dil
    assert sub % BLOCK == 0

    per_step = min(dil, max(1, ATT_MIN_ROWS_PER_STEP // sub))

    def part(p):
        return pl.BlockSpec((None, per_step, sub, ATT_GROUP_WIDTH), lambda b, r: (b, r, 0, p))

    tab = pl.BlockSpec((seq, LANES), lambda b, r: (b, 0))
    out = pl.BlockSpec((None, per_step, sub, ATT_GROUP_WIDTH), lambda b, r: (b, r, 0, 0))
    return pl.pallas_call(
        functools.partial(_attn_kernel, sub=sub, dil=dil),
        grid=(batch, dil // per_step),
        in_specs=[part(0), part(1), part(2), tab, tab],
        out_specs=[out, out],
        out_shape=[jax.ShapeDtypeStruct((batch, dil, sub, ATT_GROUP_WIDTH), BF16),
                   jax.ShapeDtypeStruct((batch, dil, sub, ATT_GROUP_WIDTH), F32)],
        scratch_shapes=[pltpu.VMEM((per_step, sub, ATT_HEADS_PER_GROUP * LANES), BF16),
                        pltpu.VMEM((per_step, sub + BLOCK, ATT_GROUP_WIDTH), BF16),
                        pltpu.VMEM((per_step, sub + BLOCK, ATT_GROUP_WIDTH), BF16)],
        compiler_params=_cparams("parallel", "parallel"),
        name=f"attention_g{group}",
    )(qkv, qkv, qkv, cos, sin)


ML_SEQS_PER_STEP = 2


def _mlstm_kernel(qk_ref, v_ref, og_ref, g_ref, cw_ref, cb_ref, sc_ref, out_ref, frame, ct, m_s):
    taps = ML_CONV

    @pl.when(pl.program_id(1) == 0)
    def _():
        frame[:, 0:ML_CHUNK, :] = jnp.zeros((frame.shape[0], ML_CHUNK, 2 * ML_PAD_WIDTH), BF16)
        ct[...] = jnp.zeros(ct.shape, F32)
        m_s[...] = jnp.zeros(m_s.shape, F32)

    out_row = lax.broadcasted_iota(jnp.int32, (ML_CHUNK, 2 * ML_CHUNK), 0)
    src_row = lax.broadcasted_iota(jnp.int32, (ML_CHUNK, 2 * ML_CHUNK), 1)
    picks = [jnp.where(src_row == out_row + (ML_CHUNK - (taps - 1 - j)), 1.0, 0.0).astype(BF16)
             for j in range(taps - 1)]
    row = lax.broadcasted_iota(jnp.int32, (ML_CHUNK, ML_CHUNK), 0)
    colm = lax.broadcasted_iota(jnp.int32, (ML_CHUNK, ML_CHUNK), 1)
    causal = row >= colm
    tri = jnp.where(causal, 1.0, 0.0).astype(BF16)
    neg_inf = jnp.float32(-jnp.inf)

    seqs = range(qk_ref.shape[0])
    pairs = [(s, h) for s in seqs for h in range(ML_HEADS)]

    def head_cols(h):
        return slice(h * ML_HEAD_PAD, (h + 1) * ML_HEAD_PAD)

    act, gates, bcum, bcum_t, gates_t = {}, {}, {}, {}, {}
    for s in seqs:
        cur = qk_ref[s]
        frame[s, ML_CHUNK:2 * ML_CHUNK, :] = cur
        both = frame[s]
        acc = cb_ref[...] + cur.astype(F32) * cw_ref[taps - 1:taps, :]
        for j in range(taps - 1):
            acc = acc + _dot(picks[j], both) * cw_ref[j:j + 1, :]
        frame[s, 0:ML_CHUNK, :] = cur
        act[s] = (acc * _sigmoid(acc) * sc_ref[...]).astype(BF16)
        gates[s] = g_ref[s]
        log_f = -(jnp.maximum(-gates[s], 0.0) + jnp.log1p(jnp.exp(-jnp.abs(gates[s]))))
        hi, mid, lo = _split3(log_f)
        bcum[s] = _dot(tri, hi) + _dot(tri, mid) + _dot(tri, lo)
        bcum_t[s] = bcum[s].T
        gates_t[s] = gates[s].T

    qh, kh, vh, c_old, m_prev, b_c, i_c, log_d, m_inter, m = ({} for _ in range(10))
    for p in pairs:
        s, h = p
        qh[p] = act[s][:, head_cols(h)]
        kh[p] = act[s][:, ML_PAD_WIDTH + h * ML_HEAD_PAD:ML_PAD_WIDTH + (h + 1) * ML_HEAD_PAD]
        vh[p] = v_ref[s, :, head_cols(h)]
        c_old[p] = ct[s, h]
        m_prev[p] = m_s[s, h, 0:1, 0:1]
        b_c[p] = bcum[s][:, ML_HEADS + h:ML_HEADS + h + 1]
        b_r = bcum_t[s][ML_HEADS + h:ML_HEADS + h + 1, :]
        i_c[p] = gates[s][:, h:h + 1]
        i_r = gates_t[s][h:h + 1, :]
        log_d[p] = jnp.where(causal, b_c[p] - b_r + i_r, neg_inf)
        m_inter[p] = b_c[p] + m_prev[p]
        m[p] = jnp.maximum(m_inter[p], jnp.max(log_d[p], axis=-1, keepdims=True))

    scores = {p: _dot_nt(qh[p], kh[p]) for p in pairs}
    carried = {p: _dot(qh[p], c_old[p].astype(BF16)) for p in pairs}
    sc = {p: (scores[p] * jnp.exp(log_d[p] - m[p])).astype(BF16) for p in pairs}
    num = {p: _dot(sc[p], vh[p]) + jnp.exp(m_inter[p] - m[p]) * carried[p] for p in pairs}
    for p in pairs:
        s, h = p
        den = num[p][:, ML_HEAD_DIM:ML_HEAD_DIM + 1]
        hid = num[p] / jnp.maximum(jnp.abs(den), jnp.exp(-m[p]))
        out_ref[s, :, head_cols(h)] = (og_ref[s, :, head_cols(h)].astype(F32) * hid).astype(out_ref.dtype)
    for p in pairs:
        s, h = p
        b_last = b_c[p][ML_CHUNK - 1:ML_CHUNK, :]
        m_new = m[p][ML_CHUNK - 1:ML_CHUNK, :]
        w = jnp.exp(b_last - b_c[p] + i_c[p] - m_new)
        decay = jnp.exp(b_last + m_prev[p] - m_new)
        wv = (w * vh[p].astype(F32)).astype(BF16)
        ct[s, h] = decay * c_old[p] + _dot_tn(kh[p], wv)
        m_s[s, h] = jnp.broadcast_to(m_new, m_s.shape[2:])


def _mlstm(ml, og, gates, conv_w, conv_b, qk_scale, batch, seq):
    nc = seq // ML_CHUNK
    group = ML_SEQS_PER_STEP if batch % ML_SEQS_PER_STEP == 0 else 1

    def whole(a):
        return pl.BlockSpec(a.shape, lambda b, c: (0, 0))

    def chunk(width, col):
        return pl.BlockSpec((group, ML_CHUNK, width), lambda b, c: (b, c, col))

    out = pl.pallas_call(
        _mlstm_kernel,
        grid=(batch // group, nc),
        in_specs=[chunk(2 * ML_PAD_WIDTH, 0), chunk(ML_PAD_WIDTH, 2), chunk(ML_PAD_WIDTH, 0), chunk(LANES, 0),
                  whole(conv_w), whole(conv_b), whole(qk_scale)],
        out_specs=chunk(ML_PAD_WIDTH, 0),
        out_shape=jax.ShapeDtypeStruct((batch, seq, ML_PAD_WIDTH), BF16),
        scratch_shapes=[pltpu.VMEM((group, 2 * ML_CHUNK, 2 * ML_PAD_WIDTH), BF16),
                        pltpu.VMEM((group, ML_HEADS, ML_HEAD_PAD, ML_HEAD_PAD), F32),
                        pltpu.VMEM((group, ML_HEADS, 8, LANES), F32)],
        compiler_params=_cparams("parallel", "arbitrary"),
        name="mlstm",
    )(ml.reshape(batch, seq, -1), ml.reshape(batch, seq, -1), og.reshape(batch, seq, -1),
      gates.reshape(batch, seq, -1), conv_w, conv_b, qk_scale)
    return out.reshape(batch * seq, ML_PAD_WIDTH)


def _mixer_out_kernel(x_ref, o0_ref, l0_ref, o1_ref, l1_ref, o2_ref, l2_ref, sg_ref, hc_ref, gp_ref,
                      sgg_ref, sgb_ref, sgw_ref, sgbias_ref, wa_ref, wb_ref, wc_ref, wo_ref, lng_ref, lnb_ref,
                      xf_ref, xb_ref, unperm, *, alpha):
    tm = x_ref.shape[0]
    slabs = ATT_GROUP_WIDTH // LANES
    att_refs = (l0_ref, l1_ref, l2_ref, o0_ref, o1_ref, o2_ref)

    for slot, ref in enumerate(att_refs):
        dil = ref.shape[0]
        if dil > 1:
            for r in range(dil):
                for c in range(slabs):
                    unperm[slabs * slot + c, pl.ds(r, tm // dil, stride=dil), :] = (
                        ref[r, :, c * LANES:(c + 1) * LANES].astype(F32))

    def token_order(slot, rows):
        ref = att_refs[slot]
        if ref.shape[0] == 1:
            return ref[0, rows, :].astype(F32)
        return jnp.concatenate([unperm[slabs * slot + c, rows, :] for c in range(slabs)], axis=1)

    row = lax.broadcasted_iota(jnp.int32, (SG_CHUNK, SG_CHUNK), 0)
    colm = lax.broadcasted_iota(jnp.int32, (SG_CHUNK, SG_CHUNK), 1)
    causal = row >= colm
    group_dim = SG_WIDTH // SG_GROUPS
    sg_w = [jnp.where(causal, sgw_ref[g], 0.0).astype(BF16) for g in range(SG_GROUPS)]

    subs = [slice(h * MIXER_SUB_ROWS, (h + 1) * MIXER_SUB_ROWS) for h in range(tm // MIXER_SUB_ROWS)]

    def gate(rows, i):
        return gp_ref[rows, i * D_MODEL:(i + 1) * D_MODEL].astype(F32)

    proj_c = [_dot(hc_ref[rows, :], wc_ref[...]) for rows in subs]

    y_a = []
    for rows in subs:
        l0, l1, l2 = token_order(0, rows), token_order(1, rows), token_order(2, rows)
        lmax = jnp.maximum(jnp.maximum(l0, l1), l2)
        e0, e1, e2 = jnp.exp(l0 - lmax), jnp.exp(l1 - lmax), jnp.exp(l2 - lmax)
        inv = 1.0 / (e0 + e1 + e2)
        y_a.append(((e0 * inv) * token_order(3, rows) + (e1 * inv) * token_order(4, rows)
                    + (e2 * inv) * token_order(5, rows)).astype(BF16))
    proj_a = [_dot(y, wa_ref[...]) for y in y_a]

    v = [_layer_norm(sg_ref[rows, SG_WIDTH:2 * SG_WIDTH].astype(F32), sgg_ref[...], sgb_ref[...]).astype(BF16)
         for rows in subs]
    y_b = []
    for rows, vs in zip(subs, v):
        chunks = []
        for c in range(MIXER_SUB_ROWS // SG_CHUNK):
            crows = slice(c * SG_CHUNK, (c + 1) * SG_CHUNK)
            parts = [_dot(sg_w[g], vs[crows, g * group_dim:(g + 1) * group_dim]) + sgbias_ref[:, g:g + 1]
                     for g in range(SG_GROUPS)]
            chunks.append(jnp.concatenate(parts, axis=1))
        y_b.append((sg_ref[rows, 0:SG_WIDTH].astype(F32) * jnp.concatenate(chunks, axis=0)).astype(BF16))
    proj_b = [_dot(y, wb_ref[...]) for y in y_b]

    z = [(gate(rows, 0) * pa + gate(rows, 1) * pb + gate(rows, 2) * pc).astype(BF16)
         for rows, pa, pb, pc in zip(subs, proj_a, proj_b, proj_c)]
    mix = [_dot(zs, wo_ref[...]) for zs in z]
    for rows, m in zip(subs, mix):
        out = _layer_norm(alpha * x_ref[rows, :] + m, lng_ref[...], lnb_ref[...])
        xf_ref[rows, :] = out
        xb_ref[rows, :] = out.astype(BF16)


def _mixer_out(x, att_outs, sg, hc, gpre, p, alpha, seq, tm=512):
    n = x.shape[0]
    tiles_per_seq = seq // tm

    def rows(width):
        return pl.BlockSpec((tm, width), lambda i: (i, 0))

    def whole(a):
        return pl.BlockSpec(a.shape, lambda i: (0,) * a.ndim)

    def residue_major(dil):
        return pl.BlockSpec((None, dil, tm // dil, ATT_GROUP_WIDTH),
                            lambda i: (i // tiles_per_seq, 0, i % tiles_per_seq, 0))

    att_specs, att_args = [], []
    for (o, lse), (_, dil) in zip(att_outs, ATT_GROUPS):
        att_specs += [residue_major(dil)] * 2
        att_args += [o, lse]
    weights = [p["sg_ln_g"], p["sg_ln_b"], p["sg_w"], p["sg_bias_t"], p["w_br_a"], p["w_br_b"], p["w_br_c"],
               p["w_out"], p["ln_g0"], p["ln_b0"]]
    return pl.pallas_call(
        functools.partial(_mixer_out_kernel, alpha=alpha),
        grid=(n // tm,),
        in_specs=[rows(D_MODEL)] + att_specs
                 + [rows(2 * SG_WIDTH), rows(ML_PAD_WIDTH), rows(N_BRANCH * D_MODEL)]
                 + [whole(w) for w in weights],
        out_specs=[rows(D_MODEL), rows(D_MODEL)],
        out_shape=[jax.ShapeDtypeStruct((n, D_MODEL), F32), jax.ShapeDtypeStruct((n, D_MODEL), BF16)],
        scratch_shapes=[pltpu.VMEM((2 * len(ATT_GROUPS) * ATT_GROUP_WIDTH // LANES, tm, LANES), F32)],
        compiler_params=_cparams("parallel"),
        name="mixer_out",
    )(x, *att_args, sg, hc, gpre, *weights)


def _swiglu_tile(x, w1_ref, w3_ref, w2_ref, prod, tf):
    for c in range(D_FF // tf):
        cols = slice(c * tf, (c + 1) * tf)
        h1 = _dot(x, w1_ref[:, cols])
        h3 = _dot(x, w3_ref[:, cols])
        prod[:, cols] = (h1 * _sigmoid(h1) * h3).astype(BF16)
    return _dot(prod[...], w2_ref[...])


def _ffn_kernel(xf_ref, xb_ref, w1_ref, w3_ref, w2_ref, lng_ref, lnb_ref, *refs, alpha, tf, riders):
    rin, (of_ref, ob_ref), rout, prod = refs[:riders], refs[riders:riders + 2], refs[riders + 2:-1], refs[-1]
    f = _swiglu_tile(xb_ref[...], w1_ref, w3_ref, w2_ref, prod, tf)
    out = _layer_norm(alpha * xf_ref[...] + f, lng_ref[...], lnb_ref[...])
    of_ref[...] = out
    ob_ref[...] = out.astype(BF16)
    _cast_riders(rin, rout)


def _ffn(xf, xb, w1, w3, w2, ln_g, ln_b, alpha, riders=(), tm=512, tf=256):
    n = xf.shape[0]

    def resident(shape):
        return pl.BlockSpec(shape, lambda i: (0, 0), pipeline_mode=pl.Buffered(1))

    rider_specs, rider_shapes = _rider_specs(riders, n // tm)
    return pl.pallas_call(
        functools.partial(_ffn_kernel, alpha=alpha, tf=tf, riders=len(riders)),
        grid=(n // tm,),
        in_specs=[pl.BlockSpec((tm, D_MODEL), lambda i: (i, 0)),
                  pl.BlockSpec((tm, D_MODEL), lambda i: (i, 0)),
                  resident((D_MODEL, D_FF)), resident((D_MODEL, D_FF)), resident((D_FF, D_MODEL)),
                  pl.BlockSpec((1, D_MODEL), lambda i: (0, 0)),
                  pl.BlockSpec((1, D_MODEL), lambda i: (0, 0))] + rider_specs,
        out_specs=[pl.BlockSpec((tm, D_MODEL), lambda i: (i, 0))] * 2 + rider_specs,
        out_shape=[jax.ShapeDtypeStruct((n, D_MODEL), F32), jax.ShapeDtypeStruct((n, D_MODEL), BF16)]
                  + rider_shapes,
        scratch_shapes=[pltpu.VMEM((tm, D_FF), BF16)],
        compiler_params=_cparams("arbitrary"),
        name="ffn",
    )(xf, xb, w1, w3, w2, ln_g, ln_b, *riders)


def _router_kernel(x_ref, w_ref, b_ref, info_ref):
    xh, xm, _ = _split3(x_ref[...])
    wh, wm, _ = _split3(w_ref[...])
    logits = (_dot(xh, wh) + (_dot(xh, wm) + _dot(xm, wh))) + b_ref[...]
    lane = lax.broadcasted_iota(jnp.int32, logits.shape, 1)
    neg_inf = jnp.float32(-jnp.inf)
    logits = jnp.where(lane < N_EXPERTS, logits, neg_inf)
    v1 = jnp.max(logits, axis=-1, keepdims=True)
    i1 = jnp.min(jnp.where(logits == v1, lane, LANES), axis=-1, keepdims=True)
    rest = jnp.where(lane == i1, neg_inf, logits)
    v2 = jnp.max(rest, axis=-1, keepdims=True)
    i2 = jnp.min(jnp.where(rest == v2, lane, LANES), axis=-1, keepdims=True)
    e = jnp.exp(v2 - v1)
    inv = 1.0 / (1.0 + e)
    info_ref[...] = jnp.where(lane == 0, i1.astype(F32),
                              jnp.where(lane == 1, i2.astype(F32),
                                        jnp.where(lane == 2, inv, jnp.where(lane == 3, e * inv, 0.0))))


def _router(xf, router_w, router_b, tm=1024):
    n = xf.shape[0]
    w = jnp.pad(router_w, ((0, 0), (0, LANES - N_EXPERTS)))
    b = jnp.pad(router_b, (0, LANES - N_EXPERTS))[None, :]
    return pl.pallas_call(
        _router_kernel,
        grid=(n // tm,),
        in_specs=[pl.BlockSpec((tm, D_MODEL), lambda i: (i, 0)),
                  pl.BlockSpec((D_MODEL, LANES), lambda i: (0, 0)),
                  pl.BlockSpec((1, LANES), lambda i: (0, 0))],
        out_specs=pl.BlockSpec((tm, LANES), lambda i: (i, 0)),
        out_shape=jax.ShapeDtypeStruct((n, LANES), F32),
        compiler_params=_cparams("parallel"),
        name="router",
    )(xf, w, b)


def _routing_plan(info, n):
    tile = EXPERT_TILE
    n_tiles = TOP_K * n // tile + N_EXPERTS
    expert = info[:, 0:TOP_K].astype(jnp.int32).reshape(-1)
    onehot = (expert[:, None] == jnp.arange(N_EXPERTS, dtype=jnp.int32)[None, :]).astype(jnp.int32)
    running = jnp.cumsum(onehot, axis=0)
    rank = jnp.sum(onehot * running, axis=1) - 1
    counts = running[-1]
    padded = ((counts + tile - 1) // tile) * tile
    ends = jnp.cumsum(padded)
    starts = ends - padded
    dest = jnp.sum(onehot * starts[None, :], axis=1) + rank
    used = ends[-1] // tile
    tile_start = jnp.arange(n_tiles, dtype=jnp.int32) * tile
    tile_expert = jnp.sum((tile_start[:, None] >= ends[None, :]).astype(jnp.int32), axis=1)
    last_expert = jnp.sum((tile_start[used - 1] >= ends).astype(jnp.int32))
    tile_expert = jnp.minimum(tile_expert, last_expert)
    plan = dict(dest=dest, tile_expert=tile_expert, used=used.reshape(1),
                pad_start=starts + counts, pad_count=padded - counts)
    return {k: v.astype(jnp.int32) for k, v in plan.items()}, n_tiles


def _rows_wait(src, dst, rows, sem):
    pltpu.make_async_copy(src.at[pl.ds(0, rows)], dst.at[pl.ds(0, rows)], sem).wait()


ROW_SUBLANES = D_MODEL // LANES


def _tile_copy(src, src_row, dst, dst_row, sem):
    s0 = pl.multiple_of(src_row * ROW_SUBLANES, ROW_SUBLANES)
    d0 = pl.multiple_of(dst_row * ROW_SUBLANES, ROW_SUBLANES)
    return pltpu.make_async_copy(src.at[pl.ds(s0, ROW_SUBLANES)], dst.at[pl.ds(d0, ROW_SUBLANES)], sem)


def _to_row_tiles(ref, x):
    m = x.shape[0]
    for c in range(ROW_SUBLANES):
        ref[pl.ds(c, m, stride=ROW_SUBLANES), :] = x[:, c * LANES:(c + 1) * LANES]


def _from_row_tiles(ref):
    m = ref.shape[0] // ROW_SUBLANES
    return jnp.concatenate([ref[pl.ds(c, m, stride=ROW_SUBLANES), :] for c in range(ROW_SUBLANES)], axis=1)


DISPATCH_SLOTS = 3


def _dispatch_kernel(pad_start_ref, pad_count_ref, used_ref, dest_ref, x_ref, xs_ref, buf, tiles, zero_row,
                     load_sem, row_sem, pad_sem, *, tm):
    i = pl.program_id(0)

    @pl.when(i == 0)
    def _():
        zero_row[...] = jnp.zeros(zero_row.shape, F32)
        for e in range(N_EXPERTS):
            def fill(j, carry, e=e):
                _tile_copy(zero_row, 0, xs_ref, pad_start_ref[e] + j, pad_sem).start()
                return carry
            lax.fori_loop(0, pad_count_ref[e], fill, 0)
        for e in range(N_EXPERTS):
            def drain(j, carry):
                _tile_copy(zero_row, 0, xs_ref, 0, pad_sem).wait()
                return carry
            lax.fori_loop(0, pad_count_ref[e], drain, 0)

        tile_rows = tm * ROW_SUBLANES
        spare = xs_ref.shape[0] // tile_rows - used_ref[0]
        tiles[1] = jnp.zeros(tiles.shape[1:], F32)

        def zero_tile(j):
            row0 = pl.multiple_of((used_ref[0] + j) * tile_rows, tile_rows)
            return pltpu.make_async_copy(tiles.at[1], xs_ref.at[pl.ds(row0, tile_rows)], pad_sem)

        def fill_tile(j, carry):
            zero_tile(j).start()
            return carry

        def drain_tile(j, carry):
            zero_tile(j).wait()
            return carry

        lax.fori_loop(0, spare, fill_tile, 0)
        lax.fori_loop(0, spare, drain_tile, 0)

    steps = pl.num_programs(0)
    slot = i % DISPATCH_SLOTS
    prev_slot = (i + DISPATCH_SLOTS - 1) % DISPATCH_SLOTS
    stage = i % 2

    def load(tile, into):
        return pltpu.make_async_copy(x_ref.at[pl.ds(tile * tm, tm), :], buf.at[into], load_sem.at[into])

    def wait_rows(of_stage):
        for _ in range(TOP_K):
            _rows_wait(tiles.at[of_stage], xs_ref, tm * ROW_SUBLANES, row_sem.at[of_stage])

    @pl.when(i == 0)
    def _():
        load(0, 0).start()

        @pl.when(steps > 1)
        def _():
            load(1, 1).start()

    load(i, slot).wait()
    _to_row_tiles(tiles.at[stage], buf[slot])

    def start(t, carry):
        for c in range(TOP_K):
            _tile_copy(tiles.at[stage], t, xs_ref, dest_ref[0, 0, TOP_K * t + c], row_sem.at[stage]).start(
                priority=c % 2)
        return carry

    lax.fori_loop(0, tm, start, 0, unroll=DMA_ISSUE_UNROLL)

    @pl.when(i > 0)
    def _():
        wait_rows(1 - stage)

    @pl.when(i + 2 < steps)
    def _():
        load(i + 2, prev_slot).start()

    @pl.when(i == steps - 1)
    def _():
        wait_rows(stage)


def _dispatch(xf, dest, pad_start, pad_count, used, n_slots, tm=EXPERT_TILE):
    n = xf.shape[0]
    dest3 = dest.reshape(n // tm, 1, TOP_K * tm)
    grid_spec = pltpu.PrefetchScalarGridSpec(
        num_scalar_prefetch=3,
        grid=(n // tm,),
        in_specs=[pl.BlockSpec((1, 1, TOP_K * tm), lambda i, ps, pc, nu: (i, 0, 0), memory_space=pltpu.SMEM),
                  pl.BlockSpec(memory_space=pl.ANY)],
        out_specs=pl.BlockSpec(memory_space=pl.ANY),
        scratch_shapes=[pltpu.VMEM((DISPATCH_SLOTS, tm, D_MODEL), F32),
                        pltpu.VMEM((2, tm * ROW_SUBLANES, LANES), F32),
                        pltpu.VMEM((ROW_SUBLANES, LANES), F32),
                        pltpu.SemaphoreType.DMA((DISPATCH_SLOTS,)),
                        pltpu.SemaphoreType.DMA((2,)),
                        pltpu.SemaphoreType.DMA],
    )
    return pl.pallas_call(
        functools.partial(_dispatch_kernel, tm=tm),
        grid_spec=grid_spec,
        out_shape=jax.ShapeDtypeStruct((n_slots * ROW_SUBLANES, LANES), F32),
        compiler_params=_cparams("arbitrary"),
        name="moe_dispatch",
    )(pad_start, pad_count, used, dest3, xf)


def _experts_kernel(te_ref, used_ref, xs_ref, w1_ref, w3_ref, w2_ref, ys_ref, prod, *, tf):
    del te_ref
    i = pl.program_id(0)

    @pl.when(i < used_ref[0])
    def _():
        x = _from_row_tiles(xs_ref).astype(BF16)
        _to_row_tiles(ys_ref, _swiglu_tile(x, w1_ref, w3_ref, w2_ref, prod, tf))

    @pl.when(i >= used_ref[0])
    def _():
        ys_ref[...] = jnp.zeros(ys_ref.shape, ys_ref.dtype)


def _experts(xs, tile_expert, used, w1, w3, w2, n_tiles, tf=256):
    tile = EXPERT_TILE
    grid_spec = pltpu.PrefetchScalarGridSpec(
        num_scalar_prefetch=2,
        grid=(n_tiles,),
        in_specs=[pl.BlockSpec((tile * ROW_SUBLANES, LANES), lambda i, te, nu: (jnp.minimum(i, nu[0] - 1), 0)),
                  pl.BlockSpec((None, D_MODEL, D_FF), lambda i, te, nu: (te[i], 0, 0)),
                  pl.BlockSpec((None, D_MODEL, D_FF), lambda i, te, nu: (te[i], 0, 0)),
                  pl.BlockSpec((None, D_FF, D_MODEL), lambda i, te, nu: (te[i], 0, 0))],
        out_specs=pl.BlockSpec((tile * ROW_SUBLANES, LANES), lambda i, te, nu: (i, 0)),
        scratch_shapes=[pltpu.VMEM((tile, D_FF), BF16)],
    )
    return pl.pallas_call(
        functools.partial(_experts_kernel, tf=tf),
        grid_spec=grid_spec,
        out_shape=jax.ShapeDtypeStruct((n_tiles * tile * ROW_SUBLANES, LANES), F32),
        compiler_params=_cparams("arbitrary"),
        name="moe_experts",
    )(tile_expert, used, xs, w1, w3, w2)


def _combine_kernel(dest_ref, next_dest_ref, xf_ref, info_ref, ys_ref, lng_ref, lnb_ref, of_ref, got, sem, *, alpha):
    tm = xf_ref.shape[0]
    i = pl.program_id(0)
    slot = i % 2

    def gather(dests, into):
        def start(t, carry):
            for c in range(TOP_K):
                _tile_copy(ys_ref, dests[0, 0, TOP_K * t + c], got.at[into, c], t, sem.at[into]).start(
                    priority=c % 2)
            return carry
        lax.fori_loop(0, tm, start, 0, unroll=DMA_ISSUE_UNROLL)

    @pl.when(i == 0)
    def _():
        gather(dest_ref, 0)

    @pl.when(i + 1 < pl.num_programs(0))
    def _():
        gather(next_dest_ref, 1 - slot)

    for c in range(TOP_K):
        _rows_wait(ys_ref, got.at[slot, c], tm * ROW_SUBLANES, sem.at[slot])
    info = info_ref[...]
    mix = (info[:, 2:3] * _from_row_tiles(got.at[slot, 0]) + info[:, 3:4] * _from_row_tiles(got.at[slot, 1]))
    of_ref[...] = _layer_norm(alpha * xf_ref[...] + mix, lng_ref[...], lnb_ref[...])


def _combine(xf, info, ys, dest, ln_g, ln_b, alpha, tm=512):
    n = xf.shape[0]
    steps = n // tm
    dest3 = dest.reshape(steps, 1, TOP_K * tm)
    return pl.pallas_call(
        functools.partial(_combine_kernel, alpha=alpha),
        grid=(steps,),
        in_specs=[pl.BlockSpec((1, 1, TOP_K * tm), lambda i: (i, 0, 0), memory_space=pltpu.SMEM),
                  pl.BlockSpec((1, 1, TOP_K * tm), lambda i: (jnp.minimum(i + 1, steps - 1), 0, 0),
                               memory_space=pltpu.SMEM),
                  pl.BlockSpec((tm, D_MODEL), lambda i: (i, 0)),
                  pl.BlockSpec((tm, LANES), lambda i: (i, 0)),
                  pl.BlockSpec(memory_space=pl.ANY),
                  pl.BlockSpec((1, D_MODEL), lambda i: (0, 0)),
                  pl.BlockSpec((1, D_MODEL), lambda i: (0, 0))],
        out_specs=pl.BlockSpec((tm, D_MODEL), lambda i: (i, 0)),
        out_shape=jax.ShapeDtypeStruct((n, D_MODEL), F32),
        scratch_shapes=[pltpu.VMEM((2, TOP_K, tm * ROW_SUBLANES, LANES), F32), pltpu.SemaphoreType.DMA((2,))],
        compiler_params=_cparams("arbitrary"),
        name="moe_combine",
    )(dest3, dest3, xf, info, ys, ln_g, ln_b)


def _moe(xf, router_w, router_b, w1, w3, w2, ln_g, ln_b, alpha):
    n = xf.shape[0]
    info = _router(xf, router_w, router_b)
    plan, n_tiles = _routing_plan(info, n)
    xs = _dispatch(xf, plan["dest"], plan["pad_start"], plan["pad_count"], plan["used"], n_tiles * EXPERT_TILE)
    ys = _experts(xs, plan["tile_expert"], plan["used"], w1, w3, w2, n_tiles)
    return _combine(xf, info, ys, plan["dest"], ln_g, ln_b, alpha)


def _pad_heads(w):
    lead = w.shape[:-1]
    w = w.reshape(*lead, ML_HEADS, ML_HEAD_DIM)
    w = jnp.pad(w, [(0, 0)] * len(lead) + [(0, 0), (0, ML_HEAD_PAD - ML_HEAD_DIM)])
    return w.reshape(*lead, ML_PAD_WIDTH)


def _group_qkv(w, group):
    gw = ATT_GROUP_WIDTH
    return jnp.concatenate([w[..., part * ATT_WIDTH + group * gw:part * ATT_WIDTH + (group + 1) * gw]
                            for part in range(3)], axis=-1)


def _layer_params(layer, w_in, b_in, conv_w, conv_b, sg_ln_g, sg_ln_b, sg_w, sg_b,
                  w_br_a, w_br_b, w_br_c, w_out, ln_g, ln_b):
    w, b = w_in[layer].astype(BF16), b_in[layer]
    o_att, o_sg, o_qk, o_v, o_o, o_i, o_g = 0, 2304, 3840, 5376, 6144, 6912, 6920
    ones_col = jnp.zeros((ML_HEADS, ML_HEAD_PAD), F32).at[:, ML_HEAD_DIM].set(1.0).reshape(ML_PAD_WIDTH)
    w_qk = jnp.concatenate([_pad_heads(w[:, o_qk:o_qk + ML_WIDTH]), _pad_heads(w[:, o_qk + ML_WIDTH:o_v])], axis=1)
    b_qk = jnp.concatenate([_pad_heads(b[o_qk:o_qk + ML_WIDTH]), _pad_heads(b[o_qk + ML_WIDTH:o_v])])
    qk_scale = jnp.concatenate([jnp.ones((ML_PAD_WIDTH,), F32),
                                jnp.full((ML_PAD_WIDTH,), ML_HEAD_DIM ** -0.5, F32)])
    cw, cb = conv_w[layer], conv_b[layer]
    w_att, b_att = w[:, o_att:o_sg], b[o_att:o_sg]
    return dict(
        w_att=[_group_qkv(w_att, g) for g in range(len(ATT_GROUPS))],
        b_att=[_group_qkv(b_att, g)[None, :] for g in range(len(ATT_GROUPS))],
        w_sg=w[:, o_sg:o_qk], b_sg=b[None, o_sg:o_qk],
        w_ml=jnp.concatenate([w_qk, _pad_heads(w[:, o_v:o_o])], axis=1),
        b_ml=jnp.concatenate([b_qk, _pad_heads(b[o_v:o_o]) + ones_col])[None, :],
        qk_scale=qk_scale[None, :],
        w_mlo=_pad_heads(w[:, o_o:o_i]), b_mlo=_pad_heads(b[o_o:o_i])[None, :],
        w_if=jnp.pad(w[:, o_i:o_g], ((0, 0), (0, LANES - 2 * ML_HEADS))),
        b_if=jnp.pad(b[o_i:o_g], (0, LANES - 2 * ML_HEADS))[None, :],
        w_g=w[:, o_g:], b_g=b[None, o_g:],
        conv_w=jnp.concatenate([_pad_heads(cw[:, :ML_WIDTH]), _pad_heads(cw[:, ML_WIDTH:])], axis=1),
        conv_b=jnp.concatenate([_pad_heads(cb[:ML_WIDTH]), _pad_heads(cb[ML_WIDTH:])])[None, :],
        sg_ln_g=sg_ln_g[layer][None, :], sg_ln_b=sg_ln_b[layer][None, :],
        sg_w=sg_w[layer], sg_bias_t=sg_b[layer].T,
        w_br_a=w_br_a[layer].astype(BF16), w_br_b=w_br_b[layer].astype(BF16),
        w_br_c=_pad_heads(w_br_c[layer].T).T.astype(BF16),
        w_out=w_out[layer].astype(BF16),
        ln_g0=ln_g[layer, 0][None, :], ln_b0=ln_b[layer, 0][None, :],
        ln_g1=ln_g[layer, 1][None, :], ln_b1=ln_b[layer, 1][None, :],
    )


def kernel(x, positions, w_in, b_in, conv_w, conv_b, sg_ln_g, sg_ln_b, sg_w, sg_b, w_br_a, w_br_b, w_br_c, w_out,
           ln_g, ln_b, ffn_w1, ffn_w3, ffn_w2, router_w, router_b, moe_w1, moe_w3, moe_w2):
    batch, seq, _ = x.shape
    n = batch * seq
    depth = w_in.shape[0]
    alpha = (2.0 * depth) ** 0.25
    cos, sin = _rope_tables(positions)
    xf = x.reshape(n, D_MODEL)
    xb = xf.astype(BF16)
    for layer in range(depth):
        p = _layer_params(layer, w_in, b_in, conv_w, conv_b, sg_ln_g, sg_ln_b, sg_w, sg_b,
                          w_br_a, w_br_b, w_br_c, w_out, ln_g, ln_b)
        undilated = [g for g, (_, dil) in enumerate(ATT_GROUPS) if dil == 1]
        j = layer // 2
        if layer % 2 == 0:
            riders = [ffn_w1[j], ffn_w3[j], ffn_w2[j]]
        else:
            riders = [moe_w2[j].reshape(N_EXPERTS * D_FF, D_MODEL)]
        sg, ml, og, gates_if, gbr, *rest = _projections(xb, [
            (p["w_sg"], p["b_sg"], BF16, _gelu_tanh),
            (p["w_ml"], p["b_ml"], BF16, None),
            (p["w_mlo"], p["b_mlo"], BF16, _sigmoid),
            (p["w_if"], p["b_if"], F32, None),
            (p["w_g"], p["b_g"], BF16, _sigmoid),
        ] + [(p["w_att"][g], p["b_att"][g], BF16, None) for g in undilated],
            riders=riders)
        qkv_plain, cast = rest[:len(undilated)], rest[len(undilated):]
        att_outs = []
        for g, (_, dil) in enumerate(ATT_GROUPS):
            if dil == 1:
                qkv = qkv_plain[undilated.index(g)].reshape(batch, 1, seq, ATT_PARTS_WIDTH)
            else:
                qkv = _linear_dilated(xb, p["w_att"][g], p["b_att"][g], dil, batch, seq)
            att_outs.append(_attention_group(qkv, cos, sin, g, batch, seq))
        hc = _mlstm(ml, og, gates_if, p["conv_w"], p["conv_b"], p["qk_scale"], batch, seq)
        xf, xb = _mixer_out(xf, att_outs, sg, hc, gbr, p, alpha, seq)
        if layer % 2 == 0:
            next_moe = (layer + 1) // 2 if layer + 1 < depth else None
            riders = [] if next_moe is None else [moe_w1[next_moe].reshape(N_EXPERTS * D_MODEL, D_FF),
                                                  moe_w3[next_moe].reshape(N_EXPERTS * D_MODEL, D_FF)]
            xf, xb, *moe_cast = _ffn(xf, xb, *cast, p["ln_g1"], p["ln_b1"], alpha, riders=riders)
        else:
            if layer == 0:
                moe_cast = [moe_w1[j].astype(BF16), moe_w3[j].astype(BF16)]
            w1, w3 = (m.reshape(N_EXPERTS, D_MODEL, D_FF) for m in moe_cast)
            w2 = cast[0].reshape(N_EXPERTS, D_FF, D_MODEL)
            xf = _moe(xf, router_w[j], router_b[j], w1, w3, w2, p["ln_g1"], p["ln_b1"], alpha)
            xb = xf.astype(BF16)
    return xf.reshape(batch, seq, D_MODEL)
```

```python
import functools

import numpy as np
import jax
import jax.numpy as jnp
from jax import lax
from jax.experimental import pallas as pl
from jax.experimental.pallas import tpu as pltpu

F32 = jnp.float32
BF16 = jnp.bfloat16

D_MODEL = 1024
ATT_GROUPS = ((128, 1), (512, 4), (2048, 16))
ATT_HEADS_PER_GROUP = 4
ATT_HEAD_DIM = 64
ATT_WIDTH = 768
ATT_GROUP_WIDTH = ATT_HEADS_PER_GROUP * ATT_HEAD_DIM
ATT_PARTS_WIDTH = 3 * ATT_GROUP_WIDTH
ROPE_THETA = 10000.0
SG_CHUNK = 128
SG_GROUPS = 6
SG_WIDTH = 768
ML_HEADS = 4
ML_HEAD_DIM = 192
ML_HEAD_PAD = 256
ML_WIDTH = 768
ML_PAD_WIDTH = ML_HEADS * ML_HEAD_PAD
ML_CHUNK = 128
ML_CONV = 4
N_BRANCH = 3
D_FF = 2816
N_EXPERTS = 8
TOP_K = 2
LN_EPS = 1e-5
BLOCK = 128
ATT_MIN_ROWS_PER_STEP = 1024
LANES = 128
VMEM_LIMIT = 56 * 1024 * 1024
EXPERT_TILE = 512
DMA_ISSUE_UNROLL = 8
MIXER_SUB_ROWS = 256


def _cparams(*sem):
    return pltpu.CompilerParams(dimension_semantics=sem, vmem_limit_bytes=VMEM_LIMIT)


def _layer_norm(x, g, b):
    mu = jnp.mean(x, axis=-1, keepdims=True)
    xc = x - mu
    var = jnp.mean(xc * xc, axis=-1, keepdims=True)
    return xc * lax.rsqrt(var + LN_EPS) * g + b


def _gelu_tanh(x):
    return 0.5 * x * (1.0 + jnp.tanh(np.sqrt(2.0 / np.pi) * (x + 0.044715 * (x * x * x))))


def _sigmoid(x):
    return 0.5 * (1.0 + jnp.tanh(0.5 * x))


def _dot(a, b):
    return jnp.dot(a, b, preferred_element_type=F32)


def _dot_nt(a, b):
    return lax.dot_general(a, b, (((1,), (1,)), ((), ())), preferred_element_type=F32)


def _dot_tn(a, b):
    return lax.dot_general(a, b, (((0,), (0,)), ((), ())), preferred_element_type=F32)


def _split3(x):
    hi = x.astype(BF16)
    r1 = x - hi.astype(F32)
    mid = r1.astype(BF16)
    lo = (r1 - mid.astype(F32)).astype(BF16)
    return hi, mid, lo


def _rider_specs(arrays, steps):
    specs, shapes = [], []
    for a in arrays:
        span = 1
        while (a.shape[0] * span) % steps or (a.shape[0] * span // steps) % 16:
            span *= 2
        rows = a.shape[0] * span // steps
        specs.append(pl.BlockSpec((rows, a.shape[1]), lambda i, span=span: (i // span, 0)))
        shapes.append(jax.ShapeDtypeStruct(a.shape, BF16))
    return specs, shapes


def _cast_riders(in_refs, out_refs):
    for i_ref, o_ref in zip(in_refs, out_refs):
        o_ref[...] = i_ref[...].astype(o_ref.dtype)


def _projections_kernel(x_ref, *refs, acts, riders):
    n = len(acts)
    ws, bs, rin = refs[:n], refs[n:2 * n], refs[2 * n:2 * n + riders]
    outs, rout = refs[2 * n + riders:3 * n + riders], refs[3 * n + riders:]
    x = x_ref[...]
    for w_ref, b_ref, o_ref, act in zip(ws, bs, outs, acts):
        res = _dot(x, w_ref[...]) + b_ref[...]
        o_ref[...] = (res if act is None else act(res)).astype(o_ref.dtype)
    _cast_riders(rin, rout)


def _projections(x, heads, riders=(), tm=512):
    m, k = x.shape

    def resident(a):
        return pl.BlockSpec(a.shape, lambda i: (0, 0), pipeline_mode=pl.Buffered(1))

    ws = [h[0] for h in heads]
    bs = [h[1] for h in heads]
    rider_specs, rider_shapes = _rider_specs(riders, m // tm)
    return pl.pallas_call(
        functools.partial(_projections_kernel, acts=tuple(h[3] for h in heads), riders=len(riders)),
        grid=(m // tm,),
        in_specs=([pl.BlockSpec((tm, k), lambda i: (i, 0))] + [resident(w) for w in ws] + [resident(b) for b in bs]
                  + rider_specs),
        out_specs=[pl.BlockSpec((tm, w.shape[1]), lambda i: (i, 0)) for w in ws] + rider_specs,
        out_shape=[jax.ShapeDtypeStruct((m, h[0].shape[1]), h[2]) for h in heads] + rider_shapes,
        compiler_params=_cparams("arbitrary"),
        name="projections",
    )(x, *ws, *bs, *riders)


def _linear_dilated_kernel(x_ref, w_ref, b_ref, o_ref, acc, *, dil):
    res = _dot(x_ref[...], w_ref[...]) + b_ref[...]
    slabs, rows, _ = acc.shape
    for c in range(slabs):
        acc[c] = res[:, c * LANES:(c + 1) * LANES]
    sub = rows // dil
    for r in range(dil):
        for c in range(slabs):
            o_ref[r, :, c * LANES:(c + 1) * LANES] = acc[c, pl.ds(r, sub, stride=dil), :].astype(o_ref.dtype)


def _linear_dilated(x, w, b, dil, batch, seq, tm=2048):
    k = x.shape[1]
    n = w.shape[1]
    tm = min(tm, seq)
    tiles_per_seq = seq // tm

    def resident(a):
        return pl.BlockSpec(a.shape, lambda i: (0, 0), pipeline_mode=pl.Buffered(1))

    return pl.pallas_call(
        functools.partial(_linear_dilated_kernel, dil=dil),
        grid=(batch * tiles_per_seq,),
        in_specs=[pl.BlockSpec((tm, k), lambda i: (i, 0)), resident(w), resident(b)],
        out_specs=pl.BlockSpec((None, dil, tm // dil, n),
                               lambda i: (i // tiles_per_seq, 0, i % tiles_per_seq, 0)),
        out_shape=jax.ShapeDtypeStruct((batch, dil, seq // dil, n), BF16),
        scratch_shapes=[pltpu.VMEM((n // LANES, tm, LANES), F32)],
        compiler_params=_cparams("parallel"),
        name=f"linear_dil{dil}",
    )(x, w, b)


def _rope_table_kernel(pos_ref, freq_ref, sign_ref, *refs):
    riders = (len(refs) - 2) // 2
    rin, (cos_ref, sin_ref), rout = refs[:riders], refs[riders:riders + 2], refs[riders + 2:]
    ang = pos_ref[...] * freq_ref[...]
    cos_ref[...] = jnp.cos(ang)
    sin_ref[...] = jnp.sin(ang) * sign_ref[...]
    _cast_riders(rin, rout)


def _rope_tables(positions, riders=()):
    n = positions.size
    half = ATT_HEAD_DIM // 2
    freqs = ROPE_THETA ** (-jnp.arange(half, dtype=F32) * (2.0 / ATT_HEAD_DIM))
    lane = np.arange(LANES)
    freq_row = freqs[lane % half][None, :]
    sign_row = jnp.asarray(np.where(lane % ATT_HEAD_DIM < half, -1.0, 1.0), F32)[None, :]
    pos = positions.astype(F32).reshape(n, 1)
    tm = 1024
    rider_specs, rider_shapes = _rider_specs(riders, n // tm)
    return pl.pallas_call(
        _rope_table_kernel,
        grid=(n // tm,),
        in_specs=[pl.BlockSpec((tm, 1), lambda i: (i, 0)),
                  pl.BlockSpec((1, LANES), lambda i: (0, 0)),
                  pl.BlockSpec((1, LANES), lambda i: (0, 0))] + rider_specs,
        out_specs=[pl.BlockSpec((tm, LANES), lambda i: (i, 0))] * 2 + rider_specs,
        out_shape=[jax.ShapeDtypeStruct((n, LANES), F32)] * 2 + rider_shapes,
        compiler_params=_cparams("arbitrary"),
        name="rope_tables",
    )(pos, freq_row, sign_row, *riders)


def _attn_kernel(q_ref, k_ref, v_ref, cos_ref, sin_ref, o_ref, lse_ref, qs, ks, vs, *, sub, dil):
    per_step = q_ref.shape[0]
    for rr in range(per_step):
        _attn_residue(q_ref.at[rr], k_ref.at[rr], v_ref.at[rr], cos_ref, sin_ref, o_ref.at[rr], lse_ref.at[rr],
                      qs.at[rr], ks.at[rr], vs.at[rr], pl.program_id(1) * per_step + rr, sub=sub, dil=dil)


def _attn_residue(q_ref, k_ref, v_ref, cos_ref, sin_ref, o_ref, lse_ref, qs, ks, vs, res, *, sub, dil):
    nb = sub // BLOCK
    lane = lax.broadcasted_iota(jnp.int32, (BLOCK, LANES), 1)
    first_half = (lane % ATT_HEAD_DIM) < (ATT_HEAD_DIM // 2)
    low_head = lane < ATT_HEAD_DIM

    def rope(x, c, s):
        partner = jnp.where(first_half, pltpu.roll(x, LANES - 32, 1), pltpu.roll(x, 32, 1))
        return x * c + partner * s

    ks[0:BLOCK, :] = jnp.zeros((BLOCK, ATT_GROUP_WIDTH), BF16)
    vs[0:BLOCK, :] = jnp.zeros((BLOCK, ATT_GROUP_WIDTH), BF16)

    def prep(n, carry):
        r0 = pl.multiple_of(n * BLOCK, BLOCK)
        if dil == 1:
            c = cos_ref[pl.ds(r0, BLOCK), :]
            s = sin_ref[pl.ds(r0, BLOCK), :]
        else:
            c = cos_ref[pl.ds(r0 * dil + res, BLOCK, stride=dil), :]
            s = sin_ref[pl.ds(r0 * dil + res, BLOCK, stride=dil), :]
        for pair in range(2):
            cols = slice(pair * LANES, (pair + 1) * LANES)
            q = rope(q_ref[pl.ds(r0, BLOCK), cols].astype(F32), c, s) * (ATT_HEAD_DIM ** -0.5)
            qs[pl.ds(r0, BLOCK), (2 * pair) * LANES:(2 * pair + 1) * LANES] = jnp.where(low_head, q, 0.0).astype(BF16)
            qs[pl.ds(r0, BLOCK), (2 * pair + 1) * LANES:(2 * pair + 2) * LANES] = jnp.where(low_head, 0.0, q).astype(BF16)
            k = rope(k_ref[pl.ds(r0, BLOCK), cols].astype(F32), c, s)
            ks[pl.ds(r0 + BLOCK, BLOCK), cols] = k.astype(BF16)
        vs[pl.ds(r0 + BLOCK, BLOCK), :] = v_ref[pl.ds(r0, BLOCK), :]
        return carry

    if nb <= 2:
        for n in range(nb):
            prep(n, 0)
    else:
        lax.fori_loop(0, nb, prep, 0)

    qi = lax.broadcasted_iota(jnp.int32, (BLOCK, 2 * BLOCK), 0)
    ki = lax.broadcasted_iota(jnp.int32, (BLOCK, 2 * BLOCK), 1)
    dist = qi + BLOCK - ki
    band = (dist >= 0) & (dist <= BLOCK)
    neg_inf = jnp.float32(-jnp.inf)
    bias_rest = jnp.where(band, 0.0, neg_inf)
    bias_first = jnp.where(band & (ki >= BLOCK), 0.0, neg_inf)

    def block(n, bias):
        r0 = pl.multiple_of(n * BLOCK, BLOCK)
        heads = range(ATT_HEADS_PER_GROUP)
        pair_cols = [slice(pair * LANES, (pair + 1) * LANES) for pair in range(2)]
        kw = [ks[pl.ds(r0, 2 * BLOCK), c] for c in pair_cols]
        vw = [vs[pl.ds(r0, 2 * BLOCK), c] for c in pair_cols]
        s = [_dot_nt(qs[pl.ds(r0, BLOCK), hd * LANES:(hd + 1) * LANES], kw[hd // 2]) + bias for hd in heads]
        m = [jnp.max(s[hd], axis=-1, keepdims=True) for hd in heads]
        p = [jnp.exp(s[hd] - m[hd]) for hd in heads]
        den = [jnp.sum(p[hd], axis=-1, keepdims=True) for hd in heads]
        o = [_dot(p[hd].astype(BF16), vw[hd // 2]) * (1.0 / den[hd]) for hd in heads]
        lse = [m[hd] + jnp.log(den[hd]) for hd in heads]
        for pair in range(2):
            lo, hi = 2 * pair, 2 * pair + 1
            o_ref[pl.ds(r0, BLOCK), pair_cols[pair]] = jnp.where(low_head, o[lo], o[hi]).astype(o_ref.dtype)
            lse_ref[pl.ds(r0, BLOCK), pair_cols[pair]] = jnp.where(low_head, lse[lo], lse[hi])

    block(0, bias_first)
    if nb <= 2:
        for n in range(1, nb):
            block(n, bias_rest)
        return

    def body(n, carry):
        block(n, bias_rest)
        return carry

    lax.fori_loop(1, nb, body, 0, unroll=4)


def _attention_group(qkv, cos, sin, group, batch, seq):
    window, dil = ATT_GROUPS[group]
    assert window // dil == BLOCK
    sub = seq // dil
    assert sub % BLOCK == 0

    per_step = min(dil, max(1, ATT_MIN_ROWS_PER_STEP // sub))

    def part(p):
        return pl.BlockSpec((None, per_step, sub, ATT_GROUP_WIDTH), lambda b, r: (b, r, 0, p))

    tab = pl.BlockSpec((seq, LANES), lambda b, r: (b, 0))
    out = pl.BlockSpec((None, per_step, sub, ATT_GROUP_WIDTH), lambda b, r: (b, r, 0, 0))
    return pl.pallas_call(
        functools.partial(_attn_kernel, sub=sub, dil=dil),
        grid=(batch, dil // per_step),
        in_specs=[part(0), part(1), part(2), tab, tab],
        out_specs=[out, out],
        out_shape=[jax.ShapeDtypeStruct((batch, dil, sub, ATT_GROUP_WIDTH), BF16),
                   jax.ShapeDtypeStruct((batch, dil, sub, ATT_GROUP_WIDTH), F32)],
        scratch_shapes=[pltpu.VMEM((per_step, sub, ATT_HEADS_PER_GROUP * LANES), BF16),
                        pltpu.VMEM((per_step, sub + BLOCK, ATT_GROUP_WIDTH), BF16),
                        pltpu.VMEM((per_step, sub + BLOCK, ATT_GROUP_WIDTH), BF16)],
        compiler_params=_cparams("parallel", "parallel"),
        name=f"attention_g{group}",
    )(qkv, qkv, qkv, cos, sin)


ML_SEQS_PER_STEP = 2


def _mlstm_kernel(qk_ref, v_ref, og_ref, g_ref, cw_ref, cb_ref, sc_ref, out_ref, frame, ct, m_s):
    taps = ML_CONV

    @pl.when(pl.program_id(1) == 0)
    def _():
        frame[:, 0:ML_CHUNK, :] = jnp.zeros((frame.shape[0], ML_CHUNK, 2 * ML_PAD_WIDTH), BF16)
        ct[...] = jnp.zeros(ct.shape, F32)
        m_s[...] = jnp.zeros(m_s.shape, F32)

    out_row = lax.broadcasted_iota(jnp.int32, (ML_CHUNK, 2 * ML_CHUNK), 0)
    src_row = lax.broadcasted_iota(jnp.int32, (ML_CHUNK, 2 * ML_CHUNK), 1)
    picks = [jnp.where(src_row == out_row + (ML_CHUNK - (taps - 1 - j)), 1.0, 0.0).astype(BF16)
             for j in range(taps - 1)]
    row = lax.broadcasted_iota(jnp.int32, (ML_CHUNK, ML_CHUNK), 0)
    colm = lax.broadcasted_iota(jnp.int32, (ML_CHUNK, ML_CHUNK), 1)
    causal = row >= colm
    tri = jnp.where(causal, 1.0, 0.0).astype(BF16)
    neg_inf = jnp.float32(-jnp.inf)

    seqs = range(qk_ref.shape[0])
    pairs = [(s, h) for s in seqs for h in range(ML_HEADS)]

    def head_cols(h):
        return slice(h * ML_HEAD_PAD, (h + 1) * ML_HEAD_PAD)

    act, gates, bcum, bcum_t, gates_t = {}, {}, {}, {}, {}
    for s in seqs:
        cur = qk_ref[s]
        frame[s, ML_CHUNK:2 * ML_CHUNK, :] = cur
        both = frame[s]
        acc = cb_ref[...] + cur.astype(F32) * cw_ref[taps - 1:taps, :]
        for j in range(taps - 1):
            acc = acc + _dot(picks[j], both) * cw_ref[j:j + 1, :]
        frame[s, 0:ML_CHUNK, :] = cur
        act[s] = (acc * _sigmoid(acc) * sc_ref[...]).astype(BF16)
        gates[s] = g_ref[s]
        log_f = -(jnp.maximum(-gates[s], 0.0) + jnp.log1p(jnp.exp(-jnp.abs(gates[s]))))
        hi, mid, lo = _split3(log_f)
        bcum[s] = _dot(tri, hi) + _dot(tri, mid) + _dot(tri, lo)
        bcum_t[s] = bcum[s].T
        gates_t[s] = gates[s].T

    qh, kh, vh, c_old, m_prev, b_c, i_c, log_d, m_inter, m = ({} for _ in range(10))
    for p in pairs:
        s, h = p
        qh[p] = act[s][:, head_cols(h)]
        kh[p] = act[s][:, ML_PAD_WIDTH + h * ML_HEAD_PAD:ML_PAD_WIDTH + (h + 1) * ML_HEAD_PAD]
        vh[p] = v_ref[s, :, head_cols(h)]
        c_old[p] = ct[s, h]
        m_prev[p] = m_s[s, h, 0:1, 0:1]
        b_c[p] = bcum[s][:, ML_HEADS + h:ML_HEADS + h + 1]
        b_r = bcum_t[s][ML_HEADS + h:ML_HEADS + h + 1, :]
        i_c[p] = gates[s][:, h:h + 1]
        i_r = gates_t[s][h:h + 1, :]
        log_d[p] = jnp.where(causal, b_c[p] - b_r + i_r, neg_inf)
        m_inter[p] = b_c[p] + m_prev[p]
        m[p] = jnp.maximum(m_inter[p], jnp.max(log_d[p], axis=-1, keepdims=True))

    scores = {p: _dot_nt(qh[p], kh[p]) for p in pairs}
    carried = {p: _dot(qh[p], c_old[p].astype(BF16)) for p in pairs}
    sc = {p: (scores[p] * jnp.exp(log_d[p] - m[p])).astype(BF16) for p in pairs}
    num = {p: _dot(sc[p], vh[p]) + jnp.exp(m_inter[p] - m[p]) * carried[p] for p in pairs}
    for p in pairs:
        s, h = p
        den = num[p][:, ML_HEAD_DIM:ML_HEAD_DIM + 1]
        hid = num[p] / jnp.maximum(jnp.abs(den), jnp.exp(-m[p]))
        out_ref[s, :, head_cols(h)] = (og_ref[s, :, head_cols(h)].astype(F32) * hid).astype(out_ref.dtype)
    for p in pairs:
        s, h = p
        b_last = b_c[p][ML_CHUNK - 1:ML_CHUNK, :]
        m_new = m[p][ML_CHUNK - 1:ML_CHUNK, :]
        w = jnp.exp(b_last - b_c[p] + i_c[p] - m_new)
        decay = jnp.exp(b_last + m_prev[p] - m_new)
        wv = (w * vh[p].astype(F32)).astype(BF16)
        ct[s, h] = decay * c_old[p] + _dot_tn(kh[p], wv)
        m_s[s, h] = jnp.broadcast_to(m_new, m_s.shape[2:])


def _mlstm(ml, og, gates, conv_w, conv_b, qk_scale, batch, seq):
    nc = seq // ML_CHUNK
    group = ML_SEQS_PER_STEP if batch % ML_SEQS_PER_STEP == 0 else 1

    def whole(a):
        return pl.BlockSpec(a.shape, lambda b, c: (0, 0))

    def chunk(width, col):
        return pl.BlockSpec((group, ML_CHUNK, width), lambda b, c: (b, c, col))

    out = pl.pallas_call(
        _mlstm_kernel,
        grid=(batch // group, nc),
        in_specs=[chunk(2 * ML_PAD_WIDTH, 0), chunk(ML_PAD_WIDTH, 2), chunk(ML_PAD_WIDTH, 0), chunk(LANES, 0),
                  whole(conv_w), whole(conv_b), whole(qk_scale)],
        out_specs=chunk(ML_PAD_WIDTH, 0),
        out_shape=jax.ShapeDtypeStruct((batch, seq, ML_PAD_WIDTH), BF16),
        scratch_shapes=[pltpu.VMEM((group, 2 * ML_CHUNK, 2 * ML_PAD_WIDTH), BF16),
                        pltpu.VMEM((group, ML_HEADS, ML_HEAD_PAD, ML_HEAD_PAD), F32),
                        pltpu.VMEM((group, ML_HEADS, 8, LANES), F32)],
        compiler_params=_cparams("parallel", "arbitrary"),
        name="mlstm",
    )(ml.reshape(batch, seq, -1), ml.reshape(batch, seq, -1), og.reshape(batch, seq, -1),
      gates.reshape(batch, seq, -1), conv_w, conv_b, qk_scale)
    return out.reshape(batch * seq, ML_PAD_WIDTH)


def _mixer_out_kernel(x_ref, o0_ref, l0_ref, o1_ref, l1_ref, o2_ref, l2_ref, sg_ref, hc_ref, gp_ref,
                      sgg_ref, sgb_ref, sgw_ref, sgbias_ref, wa_ref, wb_ref, wc_ref, wo_ref, lng_ref, lnb_ref,
                      xf_ref, xb_ref, unperm, *, alpha):
    tm = x_ref.shape[0]
    slabs = ATT_GROUP_WIDTH // LANES
    att_refs = (l0_ref, l1_ref, l2_ref, o0_ref, o1_ref, o2_ref)

    for slot, ref in enumerate(att_refs):
        dil = ref.shape[0]
        if dil > 1:
            for r in range(dil):
                for c in range(slabs):
                    unperm[slabs * slot + c, pl.ds(r, tm // dil, stride=dil), :] = (
                        ref[r, :, c * LANES:(c + 1) * LANES].astype(F32))

    def token_order(slot, rows):
        ref = att_refs[slot]
        if ref.shape[0] == 1:
            return ref[0, rows, :].astype(F32)
        return jnp.concatenate([unperm[slabs * slot + c, rows, :] for c in range(slabs)], axis=1)

    row = lax.broadcasted_iota(jnp.int32, (SG_CHUNK, SG_CHUNK), 0)
    colm = lax.broadcasted_iota(jnp.int32, (SG_CHUNK, SG_CHUNK), 1)
    causal = row >= colm
    group_dim = SG_WIDTH // SG_GROUPS
    sg_w = [jnp.where(causal, sgw_ref[g], 0.0).astype(BF16) for g in range(SG_GROUPS)]

    subs = [slice(h * MIXER_SUB_ROWS, (h + 1) * MIXER_SUB_ROWS) for h in range(tm // MIXER_SUB_ROWS)]

    def gate(rows, i):
        return gp_ref[rows, i * D_MODEL:(i + 1) * D_MODEL].astype(F32)

    proj_c = [_dot(hc_ref[rows, :], wc_ref[...]) for rows in subs]

    y_a = []
    for rows in subs:
        l0, l1, l2 = token_order(0, rows), token_order(1, rows), token_order(2, rows)
        lmax = jnp.maximum(jnp.maximum(l0, l1), l2)
        e0, e1, e2 = jnp.exp(l0 - lmax), jnp.exp(l1 - lmax), jnp.exp(l2 - lmax)
        inv = 1.0 / (e0 + e1 + e2)
        y_a.append(((e0 * inv) * token_order(3, rows) + (e1 * inv) * token_order(4, rows)
                    + (e2 * inv) * token_order(5, rows)).astype(BF16))
    proj_a = [_dot(y, wa_ref[...]) for y in y_a]

    v = [_layer_norm(sg_ref[rows, SG_WIDTH:2 * SG_WIDTH].astype(F32), sgg_ref[...], sgb_ref[...]).astype(BF16)
         for rows in subs]
    y_b = []
    for rows, vs in zip(subs, v):
        chunks = []
        for c in range(MIXER_SUB_ROWS // SG_CHUNK):
            crows = slice(c * SG_CHUNK, (c + 1) * SG_CHUNK)
            parts = [_dot(sg_w[g], vs[crows, g * group_dim:(g + 1) * group_dim]) + sgbias_ref[:, g:g + 1]
                     for g in range(SG_GROUPS)]
            chunks.append(jnp.concatenate(parts, axis=1))
        y_b.append((sg_ref[rows, 0:SG_WIDTH].astype(F32) * jnp.concatenate(chunks, axis=0)).astype(BF16))
    proj_b = [_dot(y, wb_ref[...]) for y in y_b]

    z = [(gate(rows, 0) * pa + gate(rows, 1) * pb + gate(rows, 2) * pc).astype(BF16)
         for rows, pa, pb, pc in zip(subs, proj_a, proj_b, proj_c)]
    mix = [_dot(zs, wo_ref[...]) for zs in z]
    for rows, m in zip(subs, mix):
        out = _layer_norm(alpha * x_ref[rows, :] + m, lng_ref[...], lnb_ref[...])
        xf_ref[rows, :] = out
        xb_ref[rows, :] = out.astype(BF16)


def _mixer_out(x, att_outs, sg, hc, gpre, p, alpha, seq, tm=512):
    n = x.shape[0]
    tiles_per_seq = seq // tm

    def rows(width):
        return pl.BlockSpec((tm, width), lambda i: (i, 0))

    def whole(a):
        return pl.BlockSpec(a.shape, lambda i: (0,) * a.ndim)

    def residue_major(dil):
        return pl.BlockSpec((None, dil, tm // dil, ATT_GROUP_WIDTH),
                            lambda i: (i // tiles_per_seq, 0, i % tiles_per_seq, 0))

    att_specs, att_args = [], []
    for (o, lse), (_, dil) in zip(att_outs, ATT_GROUPS):
        att_specs += [residue_major(dil)] * 2
        att_args += [o, lse]
    weights = [p["sg_ln_g"], p["sg_ln_b"], p["sg_w"], p["sg_bias_t"], p["w_br_a"], p["w_br_b"], p["w_br_c"],
               p["w_out"], p["ln_g0"], p["ln_b0"]]
    return pl.pallas_call(
        functools.partial(_mixer_out_kernel, alpha=alpha),
        grid=(n // tm,),
        in_specs=[rows(D_MODEL)] + att_specs
                 + [rows(2 * SG_WIDTH), rows(ML_PAD_WIDTH), rows(N_BRANCH * D_MODEL)]
                 + [whole(w) for w in weights],
        out_specs=[rows(D_MODEL), rows(D_MODEL)],
        out_shape=[jax.ShapeDtypeStruct((n, D_MODEL), F32), jax.ShapeDtypeStruct((n, D_MODEL), BF16)],
        scratch_shapes=[pltpu.VMEM((2 * len(ATT_GROUPS) * ATT_GROUP_WIDTH // LANES, tm, LANES), F32)],
        compiler_params=_cparams("parallel"),
        name="mixer_out",
    )(x, *att_args, sg, hc, gpre, *weights)


def _swiglu_tile(x, w1_ref, w3_ref, w2_ref, prod, tf):
    for c in range(D_FF // tf):
        cols = slice(c * tf, (c + 1) * tf)
        h1 = _dot(x, w1_ref[:, cols])
        h3 = _dot(x, w3_ref[:, cols])
        prod[:, cols] = (h1 * _sigmoid(h1) * h3).astype(BF16)
    return _dot(prod[...], w2_ref[...])


def _ffn_kernel(xf_ref, xb_ref, w1_ref, w3_ref, w2_ref, lng_ref, lnb_ref, *refs, alpha, tf, riders):
    rin, (of_ref, ob_ref), rout, prod = refs[:riders], refs[riders:riders + 2], refs[riders + 2:-1], refs[-1]
    f = _swiglu_tile(xb_ref[...], w1_ref, w3_ref, w2_ref, prod, tf)
    out = _layer_norm(alpha * xf_ref[...] + f, lng_ref[...], lnb_ref[...])
    of_ref[...] = out
    ob_ref[...] = out.astype(BF16)
    _cast_riders(rin, rout)


def _ffn(xf, xb, w1, w3, w2, ln_g, ln_b, alpha, riders=(), tm=512, tf=256):
    n = xf.shape[0]

    def resident(shape):
        return pl.BlockSpec(shape, lambda i: (0, 0), pipeline_mode=pl.Buffered(1))

    rider_specs, rider_shapes = _rider_specs(riders, n // tm)
    return pl.pallas_call(
        functools.partial(_ffn_kernel, alpha=alpha, tf=tf, riders=len(riders)),
        grid=(n // tm,),
        in_specs=[pl.BlockSpec((tm, D_MODEL), lambda i: (i, 0)),
                  pl.BlockSpec((tm, D_MODEL), lambda i: (i, 0)),
                  resident((D_MODEL, D_FF)), resident((D_MODEL, D_FF)), resident((D_FF, D_MODEL)),
                  pl.BlockSpec((1, D_MODEL), lambda i: (0, 0)),
                  pl.BlockSpec((1, D_MODEL), lambda i: (0, 0))] + rider_specs,
        out_specs=[pl.BlockSpec((tm, D_MODEL), lambda i: (i, 0))] * 2 + rider_specs,
        out_shape=[jax.ShapeDtypeStruct((n, D_MODEL), F32), jax.ShapeDtypeStruct((n, D_MODEL), BF16)]
                  + rider_shapes,
        scratch_shapes=[pltpu.VMEM((tm, D_FF), BF16)],
        compiler_params=_cparams("arbitrary"),
        name="ffn",
    )(xf, xb, w1, w3, w2, ln_g, ln_b, *riders)


def _router_kernel(x_ref, w_ref, b_ref, info_ref):
    xh, xm, _ = _split3(x_ref[...])
    wh, wm, _ = _split3(w_ref[...])
    logits = (_dot(xh, wh) + (_dot(xh, wm) + _dot(xm, wh))) + b_ref[...]
    lane = lax.broadcasted_iota(jnp.int32, logits.shape, 1)
    neg_inf = jnp.float32(-jnp.inf)
    logits = jnp.where(lane < N_EXPERTS, logits, neg_inf)
    v1 = jnp.max(logits, axis=-1, keepdims=True)
    i1 = jnp.min(jnp.where(logits == v1, lane, LANES), axis=-1, keepdims=True)
    rest = jnp.where(lane == i1, neg_inf, logits)
    v2 = jnp.max(rest, axis=-1, keepdims=True)
    i2 = jnp.min(jnp.where(rest == v2, lane, LANES), axis=-1, keepdims=True)
    e = jnp.exp(v2 - v1)
    inv = 1.0 / (1.0 + e)
    info_ref[...] = jnp.where(lane == 0, i1.astype(F32),
                              jnp.where(lane == 1, i2.astype(F32),
                                        jnp.where(lane == 2, inv, jnp.where(lane == 3, e * inv, 0.0))))


def _router(xf, router_w, router_b, tm=1024):
    n = xf.shape[0]
    w = jnp.pad(router_w, ((0, 0), (0, LANES - N_EXPERTS)))
    b = jnp.pad(router_b, (0, LANES - N_EXPERTS))[None, :]
    return pl.pallas_call(
        _router_kernel,
        grid=(n // tm,),
        in_specs=[pl.BlockSpec((tm, D_MODEL), lambda i: (i, 0)),
                  pl.BlockSpec((D_MODEL, LANES), lambda i: (0, 0)),
                  pl.BlockSpec((1, LANES), lambda i: (0, 0))],
        out_specs=pl.BlockSpec((tm, LANES), lambda i: (i, 0)),
        out_shape=jax.ShapeDtypeStruct((n, LANES), F32),
        compiler_params=_cparams("parallel"),
        name="router",
    )(xf, w, b)


def _routing_plan(info, n):
    tile = EXPERT_TILE
    n_tiles = TOP_K * n // tile + N_EXPERTS
    expert = info[:, 0:TOP_K].astype(jnp.int32).reshape(-1)
    onehot = (expert[:, None] == jnp.arange(N_EXPERTS, dtype=jnp.int32)[None, :]).astype(jnp.int32)
    running = jnp.cumsum(onehot, axis=0)
    rank = jnp.sum(onehot * running, axis=1) - 1
    counts = running[-1]
    padded = ((counts + tile - 1) // tile) * tile
    ends = jnp.cumsum(padded)
    starts = ends - padded
    dest = jnp.sum(onehot * starts[None, :], axis=1) + rank
    used = ends[-1] // tile
    tile_start = jnp.arange(n_tiles, dtype=jnp.int32) * tile
    tile_expert = jnp.sum((tile_start[:, None] >= ends[None, :]).astype(jnp.int32), axis=1)
    last_expert = jnp.sum((tile_start[used - 1] >= ends).astype(jnp.int32))
    tile_expert = jnp.minimum(tile_expert, last_expert)
    plan = dict(dest=dest, tile_expert=tile_expert, used=used.reshape(1),
                pad_start=starts + counts, pad_count=padded - counts)
    return {k: v.astype(jnp.int32) for k, v in plan.items()}, n_tiles


def _rows_wait(src, dst, rows, sem):
    pltpu.make_async_copy(src.at[pl.ds(0, rows)], dst.at[pl.ds(0, rows)], sem).wait()


ROW_SUBLANES = D_MODEL // LANES


def _tile_copy(src, src_row, dst, dst_row, sem):
    s0 = pl.multiple_of(src_row * ROW_SUBLANES, ROW_SUBLANES)
    d0 = pl.multiple_of(dst_row * ROW_SUBLANES, ROW_SUBLANES)
    return pltpu.make_async_copy(src.at[pl.ds(s0, ROW_SUBLANES)], dst.at[pl.ds(d0, ROW_SUBLANES)], sem)


def _to_row_tiles(ref, x):
    m = x.shape[0]
    for c in range(ROW_SUBLANES):
        ref[pl.ds(c, m, stride=ROW_SUBLANES), :] = x[:, c * LANES:(c + 1) * LANES]


def _from_row_tiles(ref):
    m = ref.shape[0] // ROW_SUBLANES
    return jnp.concatenate([ref[pl.ds(c, m, stride=ROW_SUBLANES), :] for c in range(ROW_SUBLANES)], axis=1)


DISPATCH_SLOTS = 3


def _dispatch_kernel(pad_start_ref, pad_count_ref, used_ref, dest_ref, x_ref, xs_ref, buf, tiles, zero_row,
                     load_sem, row_sem, pad_sem, *, tm):
    i = pl.program_id(0)

    @pl.when(i == 0)
    def _():
        zero_row[...] = jnp.zeros(zero_row.shape, F32)
        for e in range(N_EXPERTS):
            def fill(j, carry, e=e):
                _tile_copy(zero_row, 0, xs_ref, pad_start_ref[e] + j, pad_sem).start()
                return carry
            lax.fori_loop(0, pad_count_ref[e], fill, 0)
        for e in range(N_EXPERTS):
            def drain(j, carry):
                _tile_copy(zero_row, 0, xs_ref, 0, pad_sem).wait()
                return carry
            lax.fori_loop(0, pad_count_ref[e], drain, 0)

        tile_rows = tm * ROW_SUBLANES
        spare = xs_ref.shape[0] // tile_rows - used_ref[0]
        tiles[1] = jnp.zeros(tiles.shape[1:], F32)

        def zero_tile(j):
            row0 = pl.multiple_of((used_ref[0] + j) * tile_rows, tile_rows)
            return pltpu.make_async_copy(tiles.at[1], xs_ref.at[pl.ds(row0, tile_rows)], pad_sem)

        def fill_tile(j, carry):
            zero_tile(j).start()
            return carry

        def drain_tile(j, carry):
            zero_tile(j).wait()
            return carry

        lax.fori_loop(0, spare, fill_tile, 0)
        lax.fori_loop(0, spare, drain_tile, 0)

    steps = pl.num_programs(0)
    slot = i % DISPATCH_SLOTS
    prev_slot = (i + DISPATCH_SLOTS - 1) % DISPATCH_SLOTS
    stage = i % 2

    def load(tile, into):
        return pltpu.make_async_copy(x_ref.at[pl.ds(tile * tm, tm), :], buf.at[into], load_sem.at[into])

    def wait_rows(of_stage):
        for _ in range(TOP_K):
            _rows_wait(tiles.at[of_stage], xs_ref, tm * ROW_SUBLANES, row_sem.at[of_stage])

    @pl.when(i == 0)
    def _():
        load(0, 0).start()

        @pl.when(steps > 1)
        def _():
            load(1, 1).start()

    load(i, slot).wait()
    _to_row_tiles(tiles.at[stage], buf[slot])

    def start(t, carry):
        for c in range(TOP_K):
            _tile_copy(tiles.at[stage], t, xs_ref, dest_ref[0, 0, TOP_K * t + c], row_sem.at[stage]).start(
                priority=c % 2)
        return carry

    lax.fori_loop(0, tm, start, 0, unroll=DMA_ISSUE_UNROLL)

    @pl.when(i > 0)
    def _():
        wait_rows(1 - stage)

    @pl.when(i + 2 < steps)
    def _():
        load(i + 2, prev_slot).start()

    @pl.when(i == steps - 1)
    def _():
        wait_rows(stage)


def _dispatch(xf, dest, pad_start, pad_count, used, n_slots, tm=EXPERT_TILE):
    n = xf.shape[0]
    dest3 = dest.reshape(n // tm, 1, TOP_K * tm)
    grid_spec = pltpu.PrefetchScalarGridSpec(
        num_scalar_prefetch=3,
        grid=(n // tm,),
        in_specs=[pl.BlockSpec((1, 1, TOP_K * tm), lambda i, ps, pc, nu: (i, 0, 0), memory_space=pltpu.SMEM),
                  pl.BlockSpec(memory_space=pl.ANY)],
        out_specs=pl.BlockSpec(memory_space=pl.ANY),
        scratch_shapes=[pltpu.VMEM((DISPATCH_SLOTS, tm, D_MODEL), F32),
                        pltpu.VMEM((2, tm * ROW_SUBLANES, LANES), F32),
                        pltpu.VMEM((ROW_SUBLANES, LANES), F32),
                        pltpu.SemaphoreType.DMA((DISPATCH_SLOTS,)),
                        pltpu.SemaphoreType.DMA((2,)),
                        pltpu.SemaphoreType.DMA],
    )
    return pl.pallas_call(
        functools.partial(_dispatch_kernel, tm=tm),
        grid_spec=grid_spec,
        out_shape=jax.ShapeDtypeStruct((n_slots * ROW_SUBLANES, LANES), F32),
        compiler_params=_cparams("arbitrary"),
        name="moe_dispatch",
    )(pad_start, pad_count, used, dest3, xf)


def _experts_kernel(te_ref, used_ref, xs_ref, w1_ref, w3_ref, w2_ref, ys_ref, prod, *, tf):
    del te_ref
    i = pl.program_id(0)

    @pl.when(i < used_ref[0])
    def _():
        x = _from_row_tiles(xs_ref).astype(BF16)
        _to_row_tiles(ys_ref, _swiglu_tile(x, w1_ref, w3_ref, w2_ref, prod, tf))

    @pl.when(i >= used_ref[0])
    def _():
        ys_ref[...] = jnp.zeros(ys_ref.shape, ys_ref.dtype)


def _experts(xs, tile_expert, used, w1, w3, w2, n_tiles, tf=256):
    tile = EXPERT_TILE
    grid_spec = pltpu.PrefetchScalarGridSpec(
        num_scalar_prefetch=2,
        grid=(n_tiles,),
        in_specs=[pl.BlockSpec((tile * ROW_SUBLANES, LANES), lambda i, te, nu: (jnp.minimum(i, nu[0] - 1), 0)),
                  pl.BlockSpec((None, D_MODEL, D_FF), lambda i, te, nu: (te[i], 0, 0)),
                  pl.BlockSpec((None, D_MODEL, D_FF), lambda i, te, nu: (te[i], 0, 0)),
                  pl.BlockSpec((None, D_FF, D_MODEL), lambda i, te, nu: (te[i], 0, 0))],
        out_specs=pl.BlockSpec((tile * ROW_SUBLANES, LANES), lambda i, te, nu: (i, 0)),
        scratch_shapes=[pltpu.VMEM((tile, D_FF), BF16)],
    )
    return pl.pallas_call(
        functools.partial(_experts_kernel, tf=tf),
        grid_spec=grid_spec,
        out_shape=jax.ShapeDtypeStruct((n_tiles * tile * ROW_SUBLANES, LANES), F32),
        compiler_params=_cparams("arbitrary"),
        name="moe_experts",
    )(tile_expert, used, xs, w1, w3, w2)


def _combine_kernel(dest_ref, next_dest_ref, xf_ref, info_ref, ys_ref, lng_ref, lnb_ref, of_ref, got, sem, *, alpha):
    tm = xf_ref.shape[0]
    i = pl.program_id(0)
    slot = i % 2

    def gather(dests, into):
        def start(t, carry):
            for c in range(TOP_K):
                _tile_copy(ys_ref, dests[0, 0, TOP_K * t + c], got.at[into, c], t, sem.at[into]).start(
                    priority=c % 2)
            return carry
        lax.fori_loop(0, tm, start, 0, unroll=DMA_ISSUE_UNROLL)

    @pl.when(i == 0)
    def _():
        gather(dest_ref, 0)

    @pl.when(i + 1 < pl.num_programs(0))
    def _():
        gather(next_dest_ref, 1 - slot)

    for c in range(TOP_K):
        _rows_wait(ys_ref, got.at[slot, c], tm * ROW_SUBLANES, sem.at[slot])
    info = info_ref[...]
    mix = (info[:, 2:3] * _from_row_tiles(got.at[slot, 0]) + info[:, 3:4] * _from_row_tiles(got.at[slot, 1]))
    of_ref[...] = _layer_norm(alpha * xf_ref[...] + mix, lng_ref[...], lnb_ref[...])


def _combine(xf, info, ys, dest, ln_g, ln_b, alpha, tm=512):
    n = xf.shape[0]
    steps = n // tm
    dest3 = dest.reshape(steps, 1, TOP_K * tm)
    return pl.pallas_call(
        functools.partial(_combine_kernel, alpha=alpha),
        grid=(steps,),
        in_specs=[pl.BlockSpec((1, 1, TOP_K * tm), lambda i: (i, 0, 0), memory_space=pltpu.SMEM),
                  pl.BlockSpec((1, 1, TOP_K * tm), lambda i: (jnp.minimum(i + 1, steps - 1), 0, 0),
                               memory_space=pltpu.SMEM),
                  pl.BlockSpec((tm, D_MODEL), lambda i: (i, 0)),
                  pl.BlockSpec((tm, LANES), lambda i: (i, 0)),
                  pl.BlockSpec(memory_space=pl.ANY),
                  pl.BlockSpec((1, D_MODEL), lambda i: (0, 0)),
                  pl.BlockSpec((1, D_MODEL), lambda i: (0, 0))],
        out_specs=pl.BlockSpec((tm, D_MODEL), lambda i: (i, 0)),
        out_shape=jax.ShapeDtypeStruct((n, D_MODEL), F32),
        scratch_shapes=[pltpu.VMEM((2, TOP_K, tm * ROW_SUBLANES, LANES), F32), pltpu.SemaphoreType.DMA((2,))],
        compiler_params=_cparams("arbitrary"),
        name="moe_combine",
    )(dest3, dest3, xf, info, ys, ln_g, ln_b)


def _moe(xf, router_w, router_b, w1, w3, w2, ln_g, ln_b, alpha):
    n = xf.shape[0]
    info = _router(xf, router_w, router_b)
    plan, n_tiles = _routing_plan(info, n)
    xs = _dispatch(xf, plan["dest"], plan["pad_start"], plan["pad_count"], plan["used"], n_tiles * EXPERT_TILE)
    ys = _experts(xs, plan["tile_expert"], plan["used"], w1, w3, w2, n_tiles)
    return _combine(xf, info, ys, plan["dest"], ln_g, ln_b, alpha)


def _pad_heads(w):
    lead = w.shape[:-1]
    w = w.reshape(*lead, ML_HEADS, ML_HEAD_DIM)
    w = jnp.pad(w, [(0, 0)] * len(lead) + [(0, 0), (0, ML_HEAD_PAD - ML_HEAD_DIM)])
    return w.reshape(*lead, ML_PAD_WIDTH)


def _group_qkv(w, group):
    gw = ATT_GROUP_WIDTH
    return jnp.concatenate([w[..., part * ATT_WIDTH + group * gw:part * ATT_WIDTH + (group + 1) * gw]
                            for part in range(3)], axis=-1)


def _layer_params(layer, w_in, b_in, conv_w, conv_b, sg_ln_g, sg_ln_b, sg_w, sg_b,
                  w_br_a, w_br_b, w_br_c, w_out, ln_g, ln_b):
    w, b = w_in[layer], b_in[layer]
    o_att, o_sg, o_qk, o_v, o_o, o_i, o_g = 0, 2304, 3840, 5376, 6144, 6912, 6920
    ones_col = jnp.zeros((ML_HEADS, ML_HEAD_PAD), F32).at[:, ML_HEAD_DIM].set(1.0).reshape(ML_PAD_WIDTH)
    w_qk = jnp.concatenate([_pad_heads(w[:, o_qk:o_qk + ML_WIDTH]), _pad_heads(w[:, o_qk + ML_WIDTH:o_v])], axis=1)
    b_qk = jnp.concatenate([_pad_heads(b[o_qk:o_qk + ML_WIDTH]), _pad_heads(b[o_qk + ML_WIDTH:o_v])])
    qk_scale = jnp.concatenate([jnp.ones((ML_PAD_WIDTH,), F32),
                                jnp.full((ML_PAD_WIDTH,), ML_HEAD_DIM ** -0.5, F32)])
    cw, cb = conv_w[layer], conv_b[layer]
    w_att, b_att = w[:, o_att:o_sg], b[o_att:o_sg]
    return dict(
        w_att=[_group_qkv(w_att, g) for g in range(len(ATT_GROUPS))],
        b_att=[_group_qkv(b_att, g)[None, :] for g in range(len(ATT_GROUPS))],
        w_sg=w[:, o_sg:o_qk], b_sg=b[None, o_sg:o_qk],
        w_ml=jnp.concatenate([w_qk, _pad_heads(w[:, o_v:o_o])], axis=1),
        b_ml=jnp.concatenate([b_qk, _pad_heads(b[o_v:o_o]) + ones_col])[None, :],
        qk_scale=qk_scale[None, :],
        w_mlo=_pad_heads(w[:, o_o:o_i]), b_mlo=_pad_heads(b[o_o:o_i])[None, :],
        w_if=jnp.pad(w[:, o_i:o_g], ((0, 0), (0, LANES - 2 * ML_HEADS))),
        b_if=jnp.pad(b[o_i:o_g], (0, LANES - 2 * ML_HEADS))[None, :],
        w_g=w[:, o_g:], b_g=b[None, o_g:],
        conv_w=jnp.concatenate([_pad_heads(cw[:, :ML_WIDTH]), _pad_heads(cw[:, ML_WIDTH:])], axis=1),
        conv_b=jnp.concatenate([_pad_heads(cb[:ML_WIDTH]), _pad_heads(cb[ML_WIDTH:])])[None, :],
        sg_ln_g=sg_ln_g[layer][None, :], sg_ln_b=sg_ln_b[layer][None, :],
        sg_w=sg_w[layer], sg_bias_t=sg_b[layer].T,
        w_br_a=w_br_a[layer].astype(BF16), w_br_b=w_br_b[layer].astype(BF16),
        w_br_c=_pad_heads(w_br_c[layer].T).T.astype(BF16),
        w_out=w_out[layer].astype(BF16),
        ln_g0=ln_g[layer, 0][None, :], ln_b0=ln_b[layer, 0][None, :],
        ln_g1=ln_g[layer, 1][None, :], ln_b1=ln_b[layer, 1][None, :],
    )


def kernel(x, positions, w_in, b_in, conv_w, conv_b, sg_ln_g, sg_ln_b, sg_w, sg_b, w_br_a, w_br_b, w_br_c, w_out,
           ln_g, ln_b, ffn_w1, ffn_w3, ffn_w2, router_w, router_b, moe_w1, moe_w3, moe_w2):
    batch, seq, _ = x.shape
    n = batch * seq
    depth = w_in.shape[0]
    alpha = (2.0 * depth) ** 0.25
    xf = x.reshape(n, D_MODEL)
    cos, sin, xb, w_in_bf16 = _rope_tables(positions, riders=[xf, w_in.reshape(depth * D_MODEL, -1)])
    w_in_bf16 = w_in_bf16.reshape(w_in.shape)
    for layer in range(depth):
        p = _layer_params(layer, w_in_bf16, b_in, conv_w, conv_b, sg_ln_g, sg_ln_b, sg_w, sg_b,
                          w_br_a, w_br_b, w_br_c, w_out, ln_g, ln_b)
        undilated = [g for g, (_, dil) in enumerate(ATT_GROUPS) if dil == 1]
        j = layer // 2
        if layer % 2 == 0:
            riders = [ffn_w1[j], ffn_w3[j], ffn_w2[j]]
        else:
            riders = [moe_w2[j].reshape(N_EXPERTS * D_FF, D_MODEL)]
        sg, ml, og, gates_if, gbr, *rest = _projections(xb, [
            (p["w_sg"], p["b_sg"], BF16, _gelu_tanh),
            (p["w_ml"], p["b_ml"], BF16, None),
            (p["w_mlo"], p["b_mlo"], BF16, _sigmoid),
            (p["w_if"], p["b_if"], F32, None),
            (p["w_g"], p["b_g"], BF16, _sigmoid),
        ] + [(p["w_att"][g], p["b_att"][g], BF16, None) for g in undilated],
            riders=riders)
        qkv_plain, cast = rest[:len(undilated)], rest[len(undilated):]
        att_outs = []
        for g, (_, dil) in enumerate(ATT_GROUPS):
            if dil == 1:
                qkv = qkv_plain[undilated.index(g)].reshape(batch, 1, seq, ATT_PARTS_WIDTH)
            else:
                qkv = _linear_dilated(xb, p["w_att"][g], p["b_att"][g], dil, batch, seq)
            att_outs.append(_attention_group(qkv, cos, sin, g, batch, seq))
        hc = _mlstm(ml, og, gates_if, p["conv_w"], p["conv_b"], p["qk_scale"], batch, seq)
        xf, xb = _mixer_out(xf, att_outs, sg, hc, gbr, p, alpha, seq)
        if layer % 2 == 0:
            next_moe = (layer + 1) // 2 if layer + 1 < depth else None
            riders = [] if next_moe is None else [moe_w1[next_moe].reshape(N_EXPERTS * D_MODEL, D_FF),
                                                  moe_w3[next_moe].reshape(N_EXPERTS * D_MODEL, D_FF)]
            xf, xb, *moe_cast = _ffn(xf, xb, *cast, p["ln_g1"], p["ln_b1"], alpha, riders=riders)
        else:
            if layer == 0:
                moe_cast = [moe_w1[j].astype(BF16), moe_w3[j].astype(BF16)]
            w1, w3 = (m.reshape(N_EXPERTS, D_MODEL, D_FF) for m in moe_cast)
            w2 = cast[0].reshape(N_EXPERTS, D_FF, D_MODEL)
            xf = _moe(xf, router_w[j], router_b[j], w1, w3, w2, p["ln_g1"], p["ln_b1"], alpha)
            xb = xf.astype(BF16)
    return xf.reshape(batch, seq, D_MODEL)
```

```python
import functools

import numpy as np
import jax
import jax.numpy as jnp
from jax import lax
from jax.experimental import pallas as pl
from jax.experimental.pallas import tpu as pltpu

F32 = jnp.float32
BF16 = jnp.bfloat16

D_MODEL = 1024
ATT_GROUPS = ((128, 1), (512, 4), (2048, 16))
ATT_HEADS_PER_GROUP = 4
ATT_HEAD_DIM = 64
ATT_WIDTH = 768
ATT_GROUP_WIDTH = ATT_HEADS_PER_GROUP * ATT_HEAD_DIM
ATT_PARTS_WIDTH = 3 * ATT_GROUP_WIDTH
ROPE_THETA = 10000.0
SG_CHUNK = 128
SG_GROUPS = 6
SG_WIDTH = 768
ML_HEADS = 4
ML_HEAD_DIM = 192
ML_HEAD_PAD = 256
ML_WIDTH = 768
ML_PAD_WIDTH = ML_HEADS * ML_HEAD_PAD
ML_CHUNK = 128
ML_CONV = 4
N_BRANCH = 3
D_FF = 2816
N_EXPERTS = 8
TOP_K = 2
LN_EPS = 1e-5
BLOCK = 128
ATT_MIN_ROWS_PER_STEP = 1024
LANES = 128
VMEM_LIMIT = 56 * 1024 * 1024
EXPERT_TILE = 512
DMA_ISSUE_UNROLL = 8
MIXER_SUB_ROWS = 256


def _cparams(*sem):
    return pltpu.CompilerParams(dimension_semantics=sem, vmem_limit_bytes=VMEM_LIMIT)


def _layer_norm(x, g, b):
    mu = jnp.mean(x, axis=-1, keepdims=True)
    xc = x - mu
    var = jnp.mean(xc * xc, axis=-1, keepdims=True)
    return xc * lax.rsqrt(var + LN_EPS) * g + b


def _gelu_tanh(x):
    return 0.5 * x * (1.0 + jnp.tanh(np.sqrt(2.0 / np.pi) * (x + 0.044715 * (x * x * x))))


def _sigmoid(x):
    return 0.5 * (1.0 + jnp.tanh(0.5 * x))


def _dot(a, b):
    return jnp.dot(a, b, preferred_element_type=F32)


def _dot_nt(a, b):
    return lax.dot_general(a, b, (((1,), (1,)), ((), ())), preferred_element_type=F32)


def _dot_tn(a, b):
    return lax.dot_general(a, b, (((0,), (0,)), ((), ())), preferred_element_type=F32)


def _split3(x):
    hi = x.astype(BF16)
    r1 = x - hi.astype(F32)
    mid = r1.astype(BF16)
    lo = (r1 - mid.astype(F32)).astype(BF16)
    return hi, mid, lo


def _rider_specs(arrays, steps):
    specs, shapes = [], []
    for a in arrays:
        span = 1
        while (a.shape[0] * span) % steps or (a.shape[0] * span // steps) % 16:
            span *= 2
        rows = a.shape[0] * span // steps
        specs.append(pl.BlockSpec((rows, a.shape[1]), lambda i, span=span: (i // span, 0)))
        shapes.append(jax.ShapeDtypeStruct(a.shape, BF16))
    return specs, shapes


def _cast_riders(in_refs, out_refs):
    for i_ref, o_ref in zip(in_refs, out_refs):
        o_ref[...] = i_ref[...].astype(o_ref.dtype)


def _projections_kernel(x_ref, *refs, acts, riders):
    n = len(acts)
    ws, bs, rin = refs[:n], refs[n:2 * n], refs[2 * n:2 * n + riders]
    outs, rout = refs[2 * n + riders:3 * n + riders], refs[3 * n + riders:]
    x = x_ref[...].astype(BF16)
    for w_ref, b_ref, o_ref, act in zip(ws, bs, outs, acts):
        res = _dot(x, w_ref[...]) + b_ref[...]
        o_ref[...] = (res if act is None else act(res)).astype(o_ref.dtype)
    _cast_riders(rin, rout)


def _projections(x, heads, riders=(), tm=512):
    m, k = x.shape

    def resident(a):
        return pl.BlockSpec(a.shape, lambda i: (0, 0), pipeline_mode=pl.Buffered(1))

    ws = [h[0] for h in heads]
    bs = [h[1] for h in heads]
    rider_specs, rider_shapes = _rider_specs(riders, m // tm)
    return pl.pallas_call(
        functools.partial(_projections_kernel, acts=tuple(h[3] for h in heads), riders=len(riders)),
        grid=(m // tm,),
        in_specs=([pl.BlockSpec((tm, k), lambda i: (i, 0))] + [resident(w) for w in ws] + [resident(b) for b in bs]
                  + rider_specs),
        out_specs=[pl.BlockSpec((tm, w.shape[1]), lambda i: (i, 0)) for w in ws] + rider_specs,
        out_shape=[jax.ShapeDtypeStruct((m, h[0].shape[1]), h[2]) for h in heads] + rider_shapes,
        compiler_params=_cparams("arbitrary"),
        name="projections",
    )(x, *ws, *bs, *riders)


def _linear_dilated_kernel(x_ref, w_ref, b_ref, o_ref, acc, *, dil):
    res = _dot(x_ref[...], w_ref[...]) + b_ref[...]
    slabs, rows, _ = acc.shape
    for c in range(slabs):
        acc[c] = res[:, c * LANES:(c + 1) * LANES]
    sub = rows // dil
    for r in range(dil):
        for c in range(slabs):
            o_ref[r, :, c * LANES:(c + 1) * LANES] = acc[c, pl.ds(r, sub, stride=dil), :].astype(o_ref.dtype)


def _linear_dilated(x, w, b, dil, batch, seq, tm=2048):
    k = x.shape[1]
    n = w.shape[1]
    tm = min(tm, seq)
    tiles_per_seq = seq // tm

    def resident(a):
        return pl.BlockSpec(a.shape, lambda i: (0, 0), pipeline_mode=pl.Buffered(1))

    return pl.pallas_call(
        functools.partial(_linear_dilated_kernel, dil=dil),
        grid=(batch * tiles_per_seq,),
        in_specs=[pl.BlockSpec((tm, k), lambda i: (i, 0)), resident(w), resident(b)],
        out_specs=pl.BlockSpec((None, dil, tm // dil, n),
                               lambda i: (i // tiles_per_seq, 0, i % tiles_per_seq, 0)),
        out_shape=jax.ShapeDtypeStruct((batch, dil, seq // dil, n), BF16),
        scratch_shapes=[pltpu.VMEM((n // LANES, tm, LANES), F32)],
        compiler_params=_cparams("parallel"),
        name=f"linear_dil{dil}",
    )(x, w, b)


def _rope_table_kernel(pos_ref, freq_ref, sign_ref, cos_ref, sin_ref):
    ang = pos_ref[...] * freq_ref[...]
    cos_ref[...] = jnp.cos(ang)
    sin_ref[...] = jnp.sin(ang) * sign_ref[...]


def _rope_tables(positions):
    n = positions.size
    half = ATT_HEAD_DIM // 2
    freqs = ROPE_THETA ** (-jnp.arange(half, dtype=F32) * (2.0 / ATT_HEAD_DIM))
    lane = np.arange(LANES)
    freq_row = freqs[lane % half][None, :]
    sign_row = jnp.asarray(np.where(lane % ATT_HEAD_DIM < half, -1.0, 1.0), F32)[None, :]
    pos = positions.astype(F32).reshape(n, 1)
    tm = 2048
    return pl.pallas_call(
        _rope_table_kernel,
        grid=(n // tm,),
        in_specs=[pl.BlockSpec((tm, 1), lambda i: (i, 0)),
                  pl.BlockSpec((1, LANES), lambda i: (0, 0)),
                  pl.BlockSpec((1, LANES), lambda i: (0, 0))],
        out_specs=[pl.BlockSpec((tm, LANES), lambda i: (i, 0))] * 2,
        out_shape=[jax.ShapeDtypeStruct((n, LANES), F32)] * 2,
        compiler_params=_cparams("parallel"),
        name="rope_tables",
    )(pos, freq_row, sign_row)


def _attn_kernel(q_ref, k_ref, v_ref, cos_ref, sin_ref, o_ref, lse_ref, qs, ks, vs, *, sub, dil):
    per_step = q_ref.shape[0]
    for rr in range(per_step):
        _attn_residue(q_ref.at[rr], k_ref.at[rr], v_ref.at[rr], cos_ref, sin_ref, o_ref.at[rr], lse_ref.at[rr],
                      qs.at[rr], ks.at[rr], vs.at[rr], pl.program_id(1) * per_step + rr, sub=sub, dil=dil)


def _attn_residue(q_ref, k_ref, v_ref, cos_ref, sin_ref, o_ref, lse_ref, qs, ks, vs, res, *, sub, dil):
    nb = sub // BLOCK
    lane = lax.broadcasted_iota(jnp.int32, (BLOCK, LANES), 1)
    first_half = (lane % ATT_HEAD_DIM) < (ATT_HEAD_DIM // 2)
    low_head = lane < ATT_HEAD_DIM

    def rope(x, c, s):
        partner = jnp.where(first_half, pltpu.roll(x, LANES - 32, 1), pltpu.roll(x, 32, 1))
        return x * c + partner * s

    ks[0:BLOCK, :] = jnp.zeros((BLOCK, ATT_GROUP_WIDTH), BF16)
    vs[0:BLOCK, :] = jnp.zeros((BLOCK, ATT_GROUP_WIDTH), BF16)

    def prep(n, carry):
        r0 = pl.multiple_of(n * BLOCK, BLOCK)
        if dil == 1:
            c = cos_ref[pl.ds(r0, BLOCK), :]
            s = sin_ref[pl.ds(r0, BLOCK), :]
        else:
            c = cos_ref[pl.ds(r0 * dil + res, BLOCK, stride=dil), :]
            s = sin_ref[pl.ds(r0 * dil + res, BLOCK, stride=dil), :]
        for pair in range(2):
            cols = slice(pair * LANES, (pair + 1) * LANES)
            q = rope(q_ref[pl.ds(r0, BLOCK), cols].astype(F32), c, s) * (ATT_HEAD_DIM ** -0.5)
            qs[pl.ds(r0, BLOCK), (2 * pair) * LANES:(2 * pair + 1) * LANES] = jnp.where(low_head, q, 0.0).astype(BF16)
            qs[pl.ds(r0, BLOCK), (2 * pair + 1) * LANES:(2 * pair + 2) * LANES] = jnp.where(low_head, 0.0, q).astype(BF16)
            k = rope(k_ref[pl.ds(r0, BLOCK), cols].astype(F32), c, s)
            ks[pl.ds(r0 + BLOCK, BLOCK), cols] = k.astype(BF16)
        vs[pl.ds(r0 + BLOCK, BLOCK), :] = v_ref[pl.ds(r0, BLOCK), :]
        return carry

    if nb <= 2:
        for n in range(nb):
            prep(n, 0)
    else:
        lax.fori_loop(0, nb, prep, 0)

    qi = lax.broadcasted_iota(jnp.int32, (BLOCK, 2 * BLOCK), 0)
    ki = lax.broadcasted_iota(jnp.int32, (BLOCK, 2 * BLOCK), 1)
    dist = qi + BLOCK - ki
    band = (dist >= 0) & (dist <= BLOCK)
    neg_inf = jnp.float32(-jnp.inf)
    bias_rest = jnp.where(band, 0.0, neg_inf)
    bias_first = jnp.where(band & (ki >= BLOCK), 0.0, neg_inf)

    def block(n, bias):
        r0 = pl.multiple_of(n * BLOCK, BLOCK)
        heads = range(ATT_HEADS_PER_GROUP)
        pair_cols = [slice(pair * LANES, (pair + 1) * LANES) for pair in range(2)]
        kw = [ks[pl.ds(r0, 2 * BLOCK), c] for c in pair_cols]
        vw = [vs[pl.ds(r0, 2 * BLOCK), c] for c in pair_cols]
        s = [_dot_nt(qs[pl.ds(r0, BLOCK), hd * LANES:(hd + 1) * LANES], kw[hd // 2]) + bias for hd in heads]
        m = [jnp.max(s[hd], axis=-1, keepdims=True) for hd in heads]
        p = [jnp.exp(s[hd] - m[hd]) for hd in heads]
        den = [jnp.sum(p[hd], axis=-1, keepdims=True) for hd in heads]
        o = [_dot(p[hd].astype(BF16), vw[hd // 2]) * (1.0 / den[hd]) for hd in heads]
        lse = [m[hd] + jnp.log(den[hd]) for hd in heads]
        for pair in range(2):
            lo, hi = 2 * pair, 2 * pair + 1
            o_ref[pl.ds(r0, BLOCK), pair_cols[pair]] = jnp.where(low_head, o[lo], o[hi]).astype(o_ref.dtype)
            lse_ref[pl.ds(r0, BLOCK), pair_cols[pair]] = jnp.where(low_head, lse[lo], lse[hi])

    block(0, bias_first)
    if nb <= 2:
        for n in range(1, nb):
            block(n, bias_rest)
        return

    def body(n, carry):
        block(n, bias_rest)
        return carry

    lax.fori_loop(1, nb, body, 0, unroll=4)


def _attention_group(qkv, cos, sin, group, batch, seq):
    window, dil = ATT_GROUPS[group]
    assert window // dil == BLOCK
    sub = seq // dil
    assert sub % BLOCK == 0

    per_step = min(dil, max(1, ATT_MIN_ROWS_PER_STEP // sub))

    def part(p):
        return pl.BlockSpec((None, per_step, sub, ATT_GROUP_WIDTH), lambda b, r: (b, r, 0, p))

    tab = pl.BlockSpec((seq, LANES), lambda b, r: (b, 0))
    out = pl.BlockSpec((None, per_step, sub, ATT_GROUP_WIDTH), lambda b, r: (b, r, 0, 0))
    return pl.pallas_call(
        functools.partial(_attn_kernel, sub=sub, dil=dil),
        grid=(batch, dil // per_step),
        in_specs=[part(0), part(1), part(2), tab, tab],
        out_specs=[out, out],
        out_shape=[jax.ShapeDtypeStruct((batch, dil, sub, ATT_GROUP_WIDTH), BF16),
                   jax.ShapeDtypeStruct((batch, dil, sub, ATT_GROUP_WIDTH), F32)],
        scratch_shapes=[pltpu.VMEM((per_step, sub, ATT_HEADS_PER_GROUP * LANES), BF16),
                        pltpu.VMEM((per_step, sub + BLOCK, ATT_GROUP_WIDTH), BF16),
                        pltpu.VMEM((per_step, sub + BLOCK, ATT_GROUP_WIDTH), BF16)],
        compiler_params=_cparams("parallel", "parallel"),
        name=f"attention_g{group}",
    )(qkv, qkv, qkv, cos, sin)


ML_SEQS_PER_STEP = 2


def _mlstm_kernel(qk_ref, v_ref, og_ref, g_ref, cw_ref, cb_ref, sc_ref, out_ref, frame, ct, m_s):
    taps = ML_CONV

    @pl.when(pl.program_id(1) == 0)
    def _():
        frame[:, 0:ML_CHUNK, :] = jnp.zeros((frame.shape[0], ML_CHUNK, 2 * ML_PAD_WIDTH), BF16)
        ct[...] = jnp.zeros(ct.shape, F32)
        m_s[...] = jnp.zeros(m_s.shape, F32)

    out_row = lax.broadcasted_iota(jnp.int32, (ML_CHUNK, 2 * ML_CHUNK), 0)
    src_row = lax.broadcasted_iota(jnp.int32, (ML_CHUNK, 2 * ML_CHUNK), 1)
    picks = [jnp.where(src_row == out_row + (ML_CHUNK - (taps - 1 - j)), 1.0, 0.0).astype(BF16)
             for j in range(taps - 1)]
    row = lax.broadcasted_iota(jnp.int32, (ML_CHUNK, ML_CHUNK), 0)
    colm = lax.broadcasted_iota(jnp.int32, (ML_CHUNK, ML_CHUNK), 1)
    causal = row >= colm
    tri = jnp.where(causal, 1.0, 0.0).astype(BF16)
    neg_inf = jnp.float32(-jnp.inf)

    seqs = range(qk_ref.shape[0])
    pairs = [(s, h) for s in seqs for h in range(ML_HEADS)]

    def head_cols(h):
        return slice(h * ML_HEAD_PAD, (h + 1) * ML_HEAD_PAD)

    act, gates, bcum, bcum_t, gates_t = {}, {}, {}, {}, {}
    for s in seqs:
        cur = qk_ref[s]
        frame[s, ML_CHUNK:2 * ML_CHUNK, :] = cur
        both = frame[s]
        acc = cb_ref[...] + cur.astype(F32) * cw_ref[taps - 1:taps, :]
        for j in range(taps - 1):
            acc = acc + _dot(picks[j], both) * cw_ref[j:j + 1, :]
        frame[s, 0:ML_CHUNK, :] = cur
        act[s] = (acc * _sigmoid(acc) * sc_ref[...]).astype(BF16)
        gates[s] = g_ref[s]
        log_f = -(jnp.maximum(-gates[s], 0.0) + jnp.log1p(jnp.exp(-jnp.abs(gates[s]))))
        hi, mid, lo = _split3(log_f)
        bcum[s] = _dot(tri, hi) + _dot(tri, mid) + _dot(tri, lo)
        bcum_t[s] = bcum[s].T
        gates_t[s] = gates[s].T

    qh, kh, vh, c_old, m_prev, b_c, i_c, log_d, m_inter, m = ({} for _ in range(10))
    for p in pairs:
        s, h = p
        qh[p] = act[s][:, head_cols(h)]
        kh[p] = act[s][:, ML_PAD_WIDTH + h * ML_HEAD_PAD:ML_PAD_WIDTH + (h + 1) * ML_HEAD_PAD]
        vh[p] = v_ref[s, :, head_cols(h)]
        c_old[p] = ct[s, h]
        m_prev[p] = m_s[s, h, 0:1, 0:1]
        b_c[p] = bcum[s][:, ML_HEADS + h:ML_HEADS + h + 1]
        b_r = bcum_t[s][ML_HEADS + h:ML_HEADS + h + 1, :]
        i_c[p] = gates[s][:, h:h + 1]
        i_r = gates_t[s][h:h + 1, :]
        log_d[p] = jnp.where(causal, b_c[p] - b_r + i_r, neg_inf)
        m_inter[p] = b_c[p] + m_prev[p]
        m[p] = jnp.maximum(m_inter[p], jnp.max(log_d[p], axis=-1, keepdims=True))

    scores = {p: _dot_nt(qh[p], kh[p]) for p in pairs}
    carried = {p: _dot(qh[p], c_old[p].astype(BF16)) for p in pairs}
    sc = {p: (scores[p] * jnp.exp(log_d[p] - m[p])).astype(BF16) for p in pairs}
    num = {p: _dot(sc[p], vh[p]) + jnp.exp(m_inter[p] - m[p]) * carried[p] for p in pairs}
    for p in pairs:
        s, h = p
        den = num[p][:, ML_HEAD_DIM:ML_HEAD_DIM + 1]
        hid = num[p] / jnp.maximum(jnp.abs(den), jnp.exp(-m[p]))
        out_ref[s, :, head_cols(h)] = (og_ref[s, :, head_cols(h)].astype(F32) * hid).astype(out_ref.dtype)
    for p in pairs:
        s, h = p
        b_last = b_c[p][ML_CHUNK - 1:ML_CHUNK, :]
        m_new = m[p][ML_CHUNK - 1:ML_CHUNK, :]
        w = jnp.exp(b_last - b_c[p] + i_c[p] - m_new)
        decay = jnp.exp(b_last + m_prev[p] - m_new)
        wv = (w * vh[p].astype(F32)).astype(BF16)
        ct[s, h] = decay * c_old[p] + _dot_tn(kh[p], wv)
        m_s[s, h] = jnp.broadcast_to(m_new, m_s.shape[2:])


def _mlstm(ml, og, gates, conv_w, conv_b, qk_scale, batch, seq):
    nc = seq // ML_CHUNK
    group = ML_SEQS_PER_STEP if batch % ML_SEQS_PER_STEP == 0 else 1

    def whole(a):
        return pl.BlockSpec(a.shape, lambda b, c: (0, 0))

    def chunk(width, col):
        return pl.BlockSpec((group, ML_CHUNK, width), lambda b, c: (b, c, col))

    out = pl.pallas_call(
        _mlstm_kernel,
        grid=(batch // group, nc),
        in_specs=[chunk(2 * ML_PAD_WIDTH, 0), chunk(ML_PAD_WIDTH, 2), chunk(ML_PAD_WIDTH, 0), chunk(LANES, 0),
                  whole(conv_w), whole(conv_b), whole(qk_scale)],
        out_specs=chunk(ML_PAD_WIDTH, 0),
        out_shape=jax.ShapeDtypeStruct((batch, seq, ML_PAD_WIDTH), BF16),
        scratch_shapes=[pltpu.VMEM((group, 2 * ML_CHUNK, 2 * ML_PAD_WIDTH), BF16),
                        pltpu.VMEM((group, ML_HEADS, ML_HEAD_PAD, ML_HEAD_PAD), F32),
                        pltpu.VMEM((group, ML_HEADS, 8, LANES), F32)],
        compiler_params=_cparams("parallel", "arbitrary"),
        name="mlstm",
    )(ml.reshape(batch, seq, -1), ml.reshape(batch, seq, -1), og.reshape(batch, seq, -1),
      gates.reshape(batch, seq, -1), conv_w, conv_b, qk_scale)
    return out.reshape(batch * seq, ML_PAD_WIDTH)


def _mixer_out_kernel(x_ref, o0_ref, l0_ref, o1_ref, l1_ref, o2_ref, l2_ref, sg_ref, hc_ref, gp_ref,
                      sgg_ref, sgb_ref, sgw_ref, sgbias_ref, wa_ref, wb_ref, wc_ref, wo_ref, lng_ref, lnb_ref,
                      xf_ref, xb_ref, unperm, *, alpha):
    tm = x_ref.shape[0]
    slabs = ATT_GROUP_WIDTH // LANES
    att_refs = (l0_ref, l1_ref, l2_ref, o0_ref, o1_ref, o2_ref)

    for slot, ref in enumerate(att_refs):
        dil = ref.shape[0]
        if dil > 1:
            for r in range(dil):
                for c in range(slabs):
                    unperm[slabs * slot + c, pl.ds(r, tm // dil, stride=dil), :] = (
                        ref[r, :, c * LANES:(c + 1) * LANES].astype(F32))

    def token_order(slot, rows):
        ref = att_refs[slot]
        if ref.shape[0] == 1:
            return ref[0, rows, :].astype(F32)
        return jnp.concatenate([unperm[slabs * slot + c, rows, :] for c in range(slabs)], axis=1)

    row = lax.broadcasted_iota(jnp.int32, (SG_CHUNK, SG_CHUNK), 0)
    colm = lax.broadcasted_iota(jnp.int32, (SG_CHUNK, SG_CHUNK), 1)
    causal = row >= colm
    group_dim = SG_WIDTH // SG_GROUPS
    sg_w = [jnp.where(causal, sgw_ref[g], 0.0).astype(BF16) for g in range(SG_GROUPS)]

    subs = [slice(h * MIXER_SUB_ROWS, (h + 1) * MIXER_SUB_ROWS) for h in range(tm // MIXER_SUB_ROWS)]

    def gate(rows, i):
        return gp_ref[rows, i * D_MODEL:(i + 1) * D_MODEL].astype(F32)

    proj_c = [_dot(hc_ref[rows, :], wc_ref[...]) for rows in subs]

    y_a = []
    for rows in subs:
        l0, l1, l2 = token_order(0, rows), token_order(1, rows), token_order(2, rows)
        lmax = jnp.maximum(jnp.maximum(l0, l1), l2)
        e0, e1, e2 = jnp.exp(l0 - lmax), jnp.exp(l1 - lmax), jnp.exp(l2 - lmax)
        inv = 1.0 / (e0 + e1 + e2)
        y_a.append(((e0 * inv) * token_order(3, rows) + (e1 * inv) * token_order(4, rows)
                    + (e2 * inv) * token_order(5, rows)).astype(BF16))
    proj_a = [_dot(y, wa_ref[...]) for y in y_a]

    v = [_layer_norm(sg_ref[rows, SG_WIDTH:2 * SG_WIDTH].astype(F32), sgg_ref[...], sgb_ref[...]).astype(BF16)
         for rows in subs]
    y_b = []
    for rows, vs in zip(subs, v):
        chunks = []
        for c in range(MIXER_SUB_ROWS // SG_CHUNK):
            crows = slice(c * SG_CHUNK, (c + 1) * SG_CHUNK)
            parts = [_dot(sg_w[g], vs[crows, g * group_dim:(g + 1) * group_dim]) + sgbias_ref[:, g:g + 1]
                     for g in range(SG_GROUPS)]
            chunks.append(jnp.concatenate(parts, axis=1))
        y_b.append((sg_ref[rows, 0:SG_WIDTH].astype(F32) * jnp.concatenate(chunks, axis=0)).astype(BF16))
    proj_b = [_dot(y, wb_ref[...]) for y in y_b]

    z = [(gate(rows, 0) * pa + gate(rows, 1) * pb + gate(rows, 2) * pc).astype(BF16)
         for rows, pa, pb, pc in zip(subs, proj_a, proj_b, proj_c)]
    mix = [_dot(zs, wo_ref[...]) for zs in z]
    for rows, m in zip(subs, mix):
        out = _layer_norm(alpha * x_ref[rows, :] + m, lng_ref[...], lnb_ref[...])
        xf_ref[rows, :] = out
        xb_ref[rows, :] = out.astype(BF16)


def _mixer_out(x, att_outs, sg, hc, gpre, p, alpha, seq, tm=512):
    n = x.shape[0]
    tiles_per_seq = seq // tm

    def rows(width):
        return pl.BlockSpec((tm, width), lambda i: (i, 0))

    def whole(a):
        return pl.BlockSpec(a.shape, lambda i: (0,) * a.ndim)

    def residue_major(dil):
        return pl.BlockSpec((None, dil, tm // dil, ATT_GROUP_WIDTH),
                            lambda i: (i // tiles_per_seq, 0, i % tiles_per_seq, 0))

    att_specs, att_args = [], []
    for (o, lse), (_, dil) in zip(att_outs, ATT_GROUPS):
        att_specs += [residue_major(dil)] * 2
        att_args += [o, lse]
    weights = [p["sg_ln_g"], p["sg_ln_b"], p["sg_w"], p["sg_bias_t"], p["w_br_a"], p["w_br_b"], p["w_br_c"],
               p["w_out"], p["ln_g0"], p["ln_b0"]]
    return pl.pallas_call(
        functools.partial(_mixer_out_kernel, alpha=alpha),
        grid=(n // tm,),
        in_specs=[rows(D_MODEL)] + att_specs
                 + [rows(2 * SG_WIDTH), rows(ML_PAD_WIDTH), rows(N_BRANCH * D_MODEL)]
                 + [whole(w) for w in weights],
        out_specs=[rows(D_MODEL), rows(D_MODEL)],
        out_shape=[jax.ShapeDtypeStruct((n, D_MODEL), F32), jax.ShapeDtypeStruct((n, D_MODEL), BF16)],
        scratch_shapes=[pltpu.VMEM((2 * len(ATT_GROUPS) * ATT_GROUP_WIDTH // LANES, tm, LANES), F32)],
        compiler_params=_cparams("parallel"),
        name="mixer_out",
    )(x, *att_args, sg, hc, gpre, *weights)


def _swiglu_tile(x, w1_ref, w3_ref, w2_ref, prod, tf):
    for c in range(D_FF // tf):
        cols = slice(c * tf, (c + 1) * tf)
        h1 = _dot(x, w1_ref[:, cols])
        h3 = _dot(x, w3_ref[:, cols])
        prod[:, cols] = (h1 * _sigmoid(h1) * h3).astype(BF16)
    return _dot(prod[...], w2_ref[...])


def _ffn_kernel(xf_ref, xb_ref, w1_ref, w3_ref, w2_ref, lng_ref, lnb_ref, *refs, alpha, tf, riders):
    rin, (of_ref, ob_ref), rout, prod = refs[:riders], refs[riders:riders + 2], refs[riders + 2:-1], refs[-1]
    f = _swiglu_tile(xb_ref[...], w1_ref, w3_ref, w2_ref, prod, tf)
    out = _layer_norm(alpha * xf_ref[...] + f, lng_ref[...], lnb_ref[...])
    of_ref[...] = out
    ob_ref[...] = out.astype(BF16)
    _cast_riders(rin, rout)


def _ffn(xf, xb, w1, w3, w2, ln_g, ln_b, alpha, riders=(), tm=512, tf=256):
    n = xf.shape[0]

    def resident(shape):
        return pl.BlockSpec(shape, lambda i: (0, 0), pipeline_mode=pl.Buffered(1))

    rider_specs, rider_shapes = _rider_specs(riders, n // tm)
    return pl.pallas_call(
        functools.partial(_ffn_kernel, alpha=alpha, tf=tf, riders=len(riders)),
        grid=(n // tm,),
        in_specs=[pl.BlockSpec((tm, D_MODEL), lambda i: (i, 0)),
                  pl.BlockSpec((tm, D_MODEL), lambda i: (i, 0)),
                  resident((D_MODEL, D_FF)), resident((D_MODEL, D_FF)), resident((D_FF, D_MODEL)),
                  pl.BlockSpec((1, D_MODEL), lambda i: (0, 0)),
                  pl.BlockSpec((1, D_MODEL), lambda i: (0, 0))] + rider_specs,
        out_specs=[pl.BlockSpec((tm, D_MODEL), lambda i: (i, 0))] * 2 + rider_specs,
        out_shape=[jax.ShapeDtypeStruct((n, D_MODEL), F32), jax.ShapeDtypeStruct((n, D_MODEL), BF16)]
                  + rider_shapes,
        scratch_shapes=[pltpu.VMEM((tm, D_FF), BF16)],
        compiler_params=_cparams("arbitrary"),
        name="ffn",
    )(xf, xb, w1, w3, w2, ln_g, ln_b, *riders)


def _router_kernel(x_ref, w_ref, b_ref, info_ref):
    xh, xm, _ = _split3(x_ref[...])
    wh, wm, _ = _split3(w_ref[...])
    logits = (_dot(xh, wh) + (_dot(xh, wm) + _dot(xm, wh))) + b_ref[...]
    lane = lax.broadcasted_iota(jnp.int32, logits.shape, 1)
    neg_inf = jnp.float32(-jnp.inf)
    logits = jnp.where(lane < N_EXPERTS, logits, neg_inf)
    v1 = jnp.max(logits, axis=-1, keepdims=True)
    i1 = jnp.min(jnp.where(logits == v1, lane, LANES), axis=-1, keepdims=True)
    rest = jnp.where(lane == i1, neg_inf, logits)
    v2 = jnp.max(rest, axis=-1, keepdims=True)
    i2 = jnp.min(jnp.where(rest == v2, lane, LANES), axis=-1, keepdims=True)
    e = jnp.exp(v2 - v1)
    inv = 1.0 / (1.0 + e)
    info_ref[...] = jnp.where(lane == 0, i1.astype(F32),
                              jnp.where(lane == 1, i2.astype(F32),
                                        jnp.where(lane == 2, inv, jnp.where(lane == 3, e * inv, 0.0))))


def _router(xf, router_w, router_b, tm=1024):
    n = xf.shape[0]
    w = jnp.pad(router_w, ((0, 0), (0, LANES - N_EXPERTS)))
    b = jnp.pad(router_b, (0, LANES - N_EXPERTS))[None, :]
    return pl.pallas_call(
        _router_kernel,
        grid=(n // tm,),
        in_specs=[pl.BlockSpec((tm, D_MODEL), lambda i: (i, 0)),
                  pl.BlockSpec((D_MODEL, LANES), lambda i: (0, 0)),
                  pl.BlockSpec((1, LANES), lambda i: (0, 0))],
        out_specs=pl.BlockSpec((tm, LANES), lambda i: (i, 0)),
        out_shape=jax.ShapeDtypeStruct((n, LANES), F32),
        compiler_params=_cparams("parallel"),
        name="router",
    )(xf, w, b)


def _routing_plan(info, n):
    tile = EXPERT_TILE
    n_tiles = TOP_K * n // tile + N_EXPERTS
    expert = info[:, 0:TOP_K].astype(jnp.int32).reshape(-1)
    onehot = (expert[:, None] == jnp.arange(N_EXPERTS, dtype=jnp.int32)[None, :]).astype(jnp.int32)
    running = jnp.cumsum(onehot, axis=0)
    rank = jnp.sum(onehot * running, axis=1) - 1
    counts = running[-1]
    padded = ((counts + tile - 1) // tile) * tile
    ends = jnp.cumsum(padded)
    starts = ends - padded
    dest = jnp.sum(onehot * starts[None, :], axis=1) + rank
    used = ends[-1] // tile
    tile_start = jnp.arange(n_tiles, dtype=jnp.int32) * tile
    tile_expert = jnp.sum((tile_start[:, None] >= ends[None, :]).astype(jnp.int32), axis=1)
    last_expert = jnp.sum((tile_start[used - 1] >= ends).astype(jnp.int32))
    tile_expert = jnp.minimum(tile_expert, last_expert)
    plan = dict(dest=dest, tile_expert=tile_expert, used=used.reshape(1),
                pad_start=starts + counts, pad_count=padded - counts)
    return {k: v.astype(jnp.int32) for k, v in plan.items()}, n_tiles


def _rows_wait(src, dst, rows, sem):
    pltpu.make_async_copy(src.at[pl.ds(0, rows)], dst.at[pl.ds(0, rows)], sem).wait()


ROW_SUBLANES = D_MODEL // LANES


def _tile_copy(src, src_row, dst, dst_row, sem):
    s0 = pl.multiple_of(src_row * ROW_SUBLANES, ROW_SUBLANES)
    d0 = pl.multiple_of(dst_row * ROW_SUBLANES, ROW_SUBLANES)
    return pltpu.make_async_copy(src.at[pl.ds(s0, ROW_SUBLANES)], dst.at[pl.ds(d0, ROW_SUBLANES)], sem)


def _to_row_tiles(ref, x):
    m = x.shape[0]
    for c in range(ROW_SUBLANES):
        ref[pl.ds(c, m, stride=ROW_SUBLANES), :] = x[:, c * LANES:(c + 1) * LANES]


def _from_row_tiles(ref):
    m = ref.shape[0] // ROW_SUBLANES
    return jnp.concatenate([ref[pl.ds(c, m, stride=ROW_SUBLANES), :] for c in range(ROW_SUBLANES)], axis=1)


DISPATCH_SLOTS = 3


def _dispatch_kernel(pad_start_ref, pad_count_ref, used_ref, dest_ref, x_ref, xs_ref, buf, tiles, zero_row,
                     load_sem, row_sem, pad_sem, *, tm):
    i = pl.program_id(0)

    @pl.when(i == 0)
    def _():
        zero_row[...] = jnp.zeros(zero_row.shape, F32)
        for e in range(N_EXPERTS):
            def fill(j, carry, e=e):
                _tile_copy(zero_row, 0, xs_ref, pad_start_ref[e] + j, pad_sem).start()
                return carry
            lax.fori_loop(0, pad_count_ref[e], fill, 0)
        for e in range(N_EXPERTS):
            def drain(j, carry):
                _tile_copy(zero_row, 0, xs_ref, 0, pad_sem).wait()
                return carry
            lax.fori_loop(0, pad_count_ref[e], drain, 0)

        tile_rows = tm * ROW_SUBLANES
        spare = xs_ref.shape[0] // tile_rows - used_ref[0]
        tiles[1] = jnp.zeros(tiles.shape[1:], F32)

        def zero_tile(j):
            row0 = pl.multiple_of((used_ref[0] + j) * tile_rows, tile_rows)
            return pltpu.make_async_copy(tiles.at[1], xs_ref.at[pl.ds(row0, tile_rows)], pad_sem)

        def fill_tile(j, carry):
            zero_tile(j).start()
            return carry

        def drain_tile(j, carry):
            zero_tile(j).wait()
            return carry

        lax.fori_loop(0, spare, fill_tile, 0)
        lax.fori_loop(0, spare, drain_tile, 0)

    steps = pl.num_programs(0)
    slot = i % DISPATCH_SLOTS
    prev_slot = (i + DISPATCH_SLOTS - 1) % DISPATCH_SLOTS
    stage = i % 2

    def load(tile, into):
        return pltpu.make_async_copy(x_ref.at[pl.ds(tile * tm, tm), :], buf.at[into], load_sem.at[into])

    def wait_rows(of_stage):
        for _ in range(TOP_K):
            _rows_wait(tiles.at[of_stage], xs_ref, tm * ROW_SUBLANES, row_sem.at[of_stage])

    @pl.when(i == 0)
    def _():
        load(0, 0).start()

        @pl.when(steps > 1)
        def _():
            load(1, 1).start()

    load(i, slot).wait()
    _to_row_tiles(tiles.at[stage], buf[slot])

    def start(t, carry):
        for c in range(TOP_K):
            _tile_copy(tiles.at[stage], t, xs_ref, dest_ref[0, 0, TOP_K * t + c], row_sem.at[stage]).start(
                priority=c % 2)
        return carry

    lax.fori_loop(0, tm, start, 0, unroll=DMA_ISSUE_UNROLL)

    @pl.when(i > 0)
    def _():
        wait_rows(1 - stage)

    @pl.when(i + 2 < steps)
    def _():
        load(i + 2, prev_slot).start()

    @pl.when(i == steps - 1)
    def _():
        wait_rows(stage)


def _dispatch(xf, dest, pad_start, pad_count, used, n_slots, tm=EXPERT_TILE):
    n = xf.shape[0]
    dest3 = dest.reshape(n // tm, 1, TOP_K * tm)
    grid_spec = pltpu.PrefetchScalarGridSpec(
        num_scalar_prefetch=3,
        grid=(n // tm,),
        in_specs=[pl.BlockSpec((1, 1, TOP_K * tm), lambda i, ps, pc, nu: (i, 0, 0), memory_space=pltpu.SMEM),
                  pl.BlockSpec(memory_space=pl.ANY)],
        out_specs=pl.BlockSpec(memory_space=pl.ANY),
        scratch_shapes=[pltpu.VMEM((DISPATCH_SLOTS, tm, D_MODEL), F32),
                        pltpu.VMEM((2, tm * ROW_SUBLANES, LANES), F32),
                        pltpu.VMEM((ROW_SUBLANES, LANES), F32),
                        pltpu.SemaphoreType.DMA((DISPATCH_SLOTS,)),
                        pltpu.SemaphoreType.DMA((2,)),
                        pltpu.SemaphoreType.DMA],
    )
    return pl.pallas_call(
        functools.partial(_dispatch_kernel, tm=tm),
        grid_spec=grid_spec,
        out_shape=jax.ShapeDtypeStruct((n_slots * ROW_SUBLANES, LANES), F32),
        compiler_params=_cparams("arbitrary"),
        name="moe_dispatch",
    )(pad_start, pad_count, used, dest3, xf)


def _experts_kernel(te_ref, used_ref, xs_ref, w1_ref, w3_ref, w2_ref, ys_ref, prod, *, tf):
    del te_ref
    i = pl.program_id(0)

    @pl.when(i < used_ref[0])
    def _():
        x = _from_row_tiles(xs_ref).astype(BF16)
        _to_row_tiles(ys_ref, _swiglu_tile(x, w1_ref, w3_ref, w2_ref, prod, tf))

    @pl.when(i >= used_ref[0])
    def _():
        ys_ref[...] = jnp.zeros(ys_ref.shape, ys_ref.dtype)


def _experts(xs, tile_expert, used, w1, w3, w2, n_tiles, tf=256):
    tile = EXPERT_TILE
    grid_spec = pltpu.PrefetchScalarGridSpec(
        num_scalar_prefetch=2,
        grid=(n_tiles,),
        in_specs=[pl.BlockSpec((tile * ROW_SUBLANES, LANES), lambda i, te, nu: (jnp.minimum(i, nu[0] - 1), 0)),
                  pl.BlockSpec((None, D_MODEL, D_FF), lambda i, te, nu: (te[i], 0, 0)),
                  pl.BlockSpec((None, D_MODEL, D_FF), lambda i, te, nu: (te[i], 0, 0)),
                  pl.BlockSpec((None, D_FF, D_MODEL), lambda i, te, nu: (te[i], 0, 0))],
        out_specs=pl.BlockSpec((tile * ROW_SUBLANES, LANES), lambda i, te, nu: (i, 0)),
        scratch_shapes=[pltpu.VMEM((tile, D_FF), BF16)],
    )
    return pl.pallas_call(
        functools.partial(_experts_kernel, tf=tf),
        grid_spec=grid_spec,
        out_shape=jax.ShapeDtypeStruct((n_tiles * tile * ROW_SUBLANES, LANES), F32),
        compiler_params=_cparams("arbitrary"),
        name="moe_experts",
    )(tile_expert, used, xs, w1, w3, w2)


def _combine_kernel(dest_ref, next_dest_ref, xf_ref, info_ref, ys_ref, lng_ref, lnb_ref, of_ref, got, sem, *, alpha):
    tm = xf_ref.shape[0]
    i = pl.program_id(0)
    slot = i % 2

    def gather(dests, into):
        def start(t, carry):
            for c in range(TOP_K):
                _tile_copy(ys_ref, dests[0, 0, TOP_K * t + c], got.at[into, c], t, sem.at[into]).start(
                    priority=c % 2)
            return carry
        lax.fori_loop(0, tm, start, 0, unroll=DMA_ISSUE_UNROLL)

    @pl.when(i == 0)
    def _():
        gather(dest_ref, 0)

    @pl.when(i + 1 < pl.num_programs(0))
    def _():
        gather(next_dest_ref, 1 - slot)

    for c in range(TOP_K):
        _rows_wait(ys_ref, got.at[slot, c], tm * ROW_SUBLANES, sem.at[slot])
    info = info_ref[...]
    mix = (info[:, 2:3] * _from_row_tiles(got.at[slot, 0]) + info[:, 3:4] * _from_row_tiles(got.at[slot, 1]))
    of_ref[...] = _layer_norm(alpha * xf_ref[...] + mix, lng_ref[...], lnb_ref[...])


def _combine(xf, info, ys, dest, ln_g, ln_b, alpha, tm=512):
    n = xf.shape[0]
    steps = n // tm
    dest3 = dest.reshape(steps, 1, TOP_K * tm)
    return pl.pallas_call(
        functools.partial(_combine_kernel, alpha=alpha),
        grid=(steps,),
        in_specs=[pl.BlockSpec((1, 1, TOP_K * tm), lambda i: (i, 0, 0), memory_space=pltpu.SMEM),
                  pl.BlockSpec((1, 1, TOP_K * tm), lambda i: (jnp.minimum(i + 1, steps - 1), 0, 0),
                               memory_space=pltpu.SMEM),
                  pl.BlockSpec((tm, D_MODEL), lambda i: (i, 0)),
                  pl.BlockSpec((tm, LANES), lambda i: (i, 0)),
                  pl.BlockSpec(memory_space=pl.ANY),
                  pl.BlockSpec((1, D_MODEL), lambda i: (0, 0)),
                  pl.BlockSpec((1, D_MODEL), lambda i: (0, 0))],
        out_specs=pl.BlockSpec((tm, D_MODEL), lambda i: (i, 0)),
        out_shape=jax.ShapeDtypeStruct((n, D_MODEL), F32),
        scratch_shapes=[pltpu.VMEM((2, TOP_K, tm * ROW_SUBLANES, LANES), F32), pltpu.SemaphoreType.DMA((2,))],
        compiler_params=_cparams("arbitrary"),
        name="moe_combine",
    )(dest3, dest3, xf, info, ys, ln_g, ln_b)


def _moe(xf, router_w, router_b, w1, w3, w2, ln_g, ln_b, alpha):
    n = xf.shape[0]
    info = _router(xf, router_w, router_b)
    plan, n_tiles = _routing_plan(info, n)
    xs = _dispatch(xf, plan["dest"], plan["pad_start"], plan["pad_count"], plan["used"], n_tiles * EXPERT_TILE)
    ys = _experts(xs, plan["tile_expert"], plan["used"], w1, w3, w2, n_tiles)
    return _combine(xf, info, ys, plan["dest"], ln_g, ln_b, alpha)


def _pad_heads(w):
    lead = w.shape[:-1]
    w = w.reshape(*lead, ML_HEADS, ML_HEAD_DIM)
    w = jnp.pad(w, [(0, 0)] * len(lead) + [(0, 0), (0, ML_HEAD_PAD - ML_HEAD_DIM)])
    return w.reshape(*lead, ML_PAD_WIDTH)


def _group_qkv(w, group):
    gw = ATT_GROUP_WIDTH
    return jnp.concatenate([w[..., part * ATT_WIDTH + group * gw:part * ATT_WIDTH + (group + 1) * gw]
                            for part in range(3)], axis=-1)


def _layer_params(layer, w_in, b_in, conv_w, conv_b, sg_ln_g, sg_ln_b, sg_w, sg_b,
                  w_br_a, w_br_b, w_br_c, w_out, ln_g, ln_b):
    w, b = w_in[layer].astype(BF16), b_in[layer]
    o_att, o_sg, o_qk, o_v, o_o, o_i, o_g = 0, 2304, 3840, 5376, 6144, 6912, 6920
    ones_col = jnp.zeros((ML_HEADS, ML_HEAD_PAD), F32).at[:, ML_HEAD_DIM].set(1.0).reshape(ML_PAD_WIDTH)
    w_qk = jnp.concatenate([_pad_heads(w[:, o_qk:o_qk + ML_WIDTH]), _pad_heads(w[:, o_qk + ML_WIDTH:o_v])], axis=1)
    b_qk = jnp.concatenate([_pad_heads(b[o_qk:o_qk + ML_WIDTH]), _pad_heads(b[o_qk + ML_WIDTH:o_v])])
    qk_scale = jnp.concatenate([jnp.ones((ML_PAD_WIDTH,), F32),
                                jnp.full((ML_PAD_WIDTH,), ML_HEAD_DIM ** -0.5, F32)])
    cw, cb = conv_w[layer], conv_b[layer]
    w_att, b_att = w[:, o_att:o_sg], b[o_att:o_sg]
    return dict(
        w_att=[_group_qkv(w_att, g) for g in range(len(ATT_GROUPS))],
        b_att=[_group_qkv(b_att, g)[None, :] for g in range(len(ATT_GROUPS))],
        w_sg=w[:, o_sg:o_qk], b_sg=b[None, o_sg:o_qk],
        w_ml=jnp.concatenate([w_qk, _pad_heads(w[:, o_v:o_o])], axis=1),
        b_ml=jnp.concatenate([b_qk, _pad_heads(b[o_v:o_o]) + ones_col])[None, :],
        qk_scale=qk_scale[None, :],
        w_mlo=_pad_heads(w[:, o_o:o_i]), b_mlo=_pad_heads(b[o_o:o_i])[None, :],
        w_if=jnp.pad(w[:, o_i:o_g], ((0, 0), (0, LANES - 2 * ML_HEADS))),
        b_if=jnp.pad(b[o_i:o_g], (0, LANES - 2 * ML_HEADS))[None, :],
        w_g=w[:, o_g:], b_g=b[None, o_g:],
        conv_w=jnp.concatenate([_pad_heads(cw[:, :ML_WIDTH]), _pad_heads(cw[:, ML_WIDTH:])], axis=1),
        conv_b=jnp.concatenate([_pad_heads(cb[:ML_WIDTH]), _pad_heads(cb[ML_WIDTH:])])[None, :],
        sg_ln_g=sg_ln_g[layer][None, :], sg_ln_b=sg_ln_b[layer][None, :],
        sg_w=sg_w[layer], sg_bias_t=sg_b[layer].T,
        w_br_a=w_br_a[layer].astype(BF16), w_br_b=w_br_b[layer].astype(BF16),
        w_br_c=_pad_heads(w_br_c[layer].T).T.astype(BF16),
        w_out=w_out[layer].astype(BF16),
        ln_g0=ln_g[layer, 0][None, :], ln_b0=ln_b[layer, 0][None, :],
        ln_g1=ln_g[layer, 1][None, :], ln_b1=ln_b[layer, 1][None, :],
    )


def kernel(x, positions, w_in, b_in, conv_w, conv_b, sg_ln_g, sg_ln_b, sg_w, sg_b, w_br_a, w_br_b, w_br_c, w_out,
           ln_g, ln_b, ffn_w1, ffn_w3, ffn_w2, router_w, router_b, moe_w1, moe_w3, moe_w2):
    batch, seq, _ = x.shape
    n = batch * seq
    depth = w_in.shape[0]
    alpha = (2.0 * depth) ** 0.25
    cos, sin = _rope_tables(positions)
    xf = x.reshape(n, D_MODEL)
    xb = None
    for layer in range(depth):
        p = _layer_params(layer, w_in, b_in, conv_w, conv_b, sg_ln_g, sg_ln_b, sg_w, sg_b,
                          w_br_a, w_br_b, w_br_c, w_out, ln_g, ln_b)
        undilated = [g for g, (_, dil) in enumerate(ATT_GROUPS) if dil == 1]
        j = layer // 2
        if layer % 2 == 0:
            riders = [ffn_w1[j], ffn_w3[j], ffn_w2[j]]
        else:
            riders = [moe_w2[j].reshape(N_EXPERTS * D_FF, D_MODEL)]
        if xb is None:
            riders = riders + [xf]
        sg, ml, og, gates_if, gbr, *rest = _projections(xf if xb is None else xb, [
            (p["w_sg"], p["b_sg"], BF16, _gelu_tanh),
            (p["w_ml"], p["b_ml"], BF16, None),
            (p["w_mlo"], p["b_mlo"], BF16, _sigmoid),
            (p["w_if"], p["b_if"], F32, None),
            (p["w_g"], p["b_g"], BF16, _sigmoid),
        ] + [(p["w_att"][g], p["b_att"][g], BF16, None) for g in undilated],
            riders=riders)
        qkv_plain, cast = rest[:len(undilated)], rest[len(undilated):]
        if xb is None:
            *cast, xb = cast
        att_outs = []
        for g, (_, dil) in enumerate(ATT_GROUPS):
            if dil == 1:
                qkv = qkv_plain[undilated.index(g)].reshape(batch, 1, seq, ATT_PARTS_WIDTH)
            else:
                qkv = _linear_dilated(xb, p["w_att"][g], p["b_att"][g], dil, batch, seq)
            att_outs.append(_attention_group(qkv, cos, sin, g, batch, seq))
        hc = _mlstm(ml, og, gates_if, p["conv_w"], p["conv_b"], p["qk_scale"], batch, seq)
        xf, xb = _mixer_out(xf, att_outs, sg, hc, gbr, p, alpha, seq)
        if layer % 2 == 0:
            next_moe = (layer + 1) // 2 if layer + 1 < depth else None
            riders = [] if next_moe is None else [moe_w1[next_moe].reshape(N_EXPERTS * D_MODEL, D_FF),
                                                  moe_w3[next_moe].reshape(N_EXPERTS * D_MODEL, D_FF)]
            xf, xb, *moe_cast = _ffn(xf, xb, *cast, p["ln_g1"], p["ln_b1"], alpha, riders=riders)
        else:
            if layer == 0:
                moe_cast = [moe_w1[j].astype(BF16), moe_w3[j].astype(BF16)]
            w1, w3 = (m.reshape(N_EXPERTS, D_MODEL, D_FF) for m in moe_cast)
            w2 = cast[0].reshape(N_EXPERTS, D_FF, D_MODEL)
            xf = _moe(xf, router_w[j], router_b[j], w1, w3, w2, p["ln_g1"], p["ln_b1"], alpha)
            xb = xf.astype(BF16)
    return xf.reshape(batch, seq, D_MODEL)
```

```python
import functools

import numpy as np
import jax
import jax.numpy as jnp
from jax import lax
from jax.experimental import pallas as pl
from jax.experimental.pallas import tpu as pltpu

F32 = jnp.float32
BF16 = jnp.bfloat16

D_MODEL = 1024
ATT_GROUPS = ((128, 1), (512, 4), (2048, 16))
ATT_HEADS_PER_GROUP = 4
ATT_HEAD_DIM = 64
ATT_WIDTH = 768
ATT_GROUP_WIDTH = ATT_HEADS_PER_GROUP * ATT_HEAD_DIM
ATT_PARTS_WIDTH = 3 * ATT_GROUP_WIDTH
ROPE_THETA = 10000.0
SG_CHUNK = 128
SG_GROUPS = 6
SG_WIDTH = 768
ML_HEADS = 4
ML_HEAD_DIM = 192
ML_HEAD_PAD = 256
ML_WIDTH = 768
ML_PAD_WIDTH = ML_HEADS * ML_HEAD_PAD
ML_CHUNK = 128
ML_CONV = 4
N_BRANCH = 3
D_FF = 2816
N_EXPERTS = 8
TOP_K = 2
LN_EPS = 1e-5
BLOCK = 128
ATT_MIN_ROWS_PER_STEP = 1024
LANES = 128
VMEM_LIMIT = 56 * 1024 * 1024
EXPERT_TILE = 512
DMA_ISSUE_UNROLL = 8
MIXER_SUB_ROWS = 256


def _cparams(*sem):
    return pltpu.CompilerParams(dimension_semantics=sem, vmem_limit_bytes=VMEM_LIMIT)


def _layer_norm(x, g, b):
    mu = jnp.mean(x, axis=-1, keepdims=True)
    xc = x - mu
    var = jnp.mean(xc * xc, axis=-1, keepdims=True)
    return xc * lax.rsqrt(var + LN_EPS) * g + b


def _gelu_tanh(x):
    return 0.5 * x * (1.0 + jnp.tanh(np.sqrt(2.0 / np.pi) * (x + 0.044715 * (x * x * x))))


def _sigmoid(x):
    return 0.5 * (1.0 + jnp.tanh(0.5 * x))


def _dot(a, b):
    return jnp.dot(a, b, preferred_element_type=F32)


def _dot_nt(a, b):
    return lax.dot_general(a, b, (((1,), (1,)), ((), ())), preferred_element_type=F32)


def _dot_tn(a, b):
    return lax.dot_general(a, b, (((0,), (0,)), ((), ())), preferred_element_type=F32)


def _split3(x):
    hi = x.astype(BF16)
    r1 = x - hi.astype(F32)
    mid = r1.astype(BF16)
    lo = (r1 - mid.astype(F32)).astype(BF16)
    return hi, mid, lo


def _rider_specs(arrays, steps):
    specs, shapes = [], []
    for a in arrays:
        span = 1
        while (a.shape[0] * span) % steps or (a.shape[0] * span // steps) % 16:
            span *= 2
        rows = a.shape[0] * span // steps
        specs.append(pl.BlockSpec((rows, a.shape[1]), lambda i, span=span: (i // span, 0)))
        shapes.append(jax.ShapeDtypeStruct(a.shape, BF16))
    return specs, shapes


def _cast_riders(in_refs, out_refs):
    for i_ref, o_ref in zip(in_refs, out_refs):
        o_ref[...] = i_ref[...].astype(o_ref.dtype)


def _projections_kernel(x_ref, *refs, acts, riders):
    n = len(acts)
    ws, bs, rin = refs[:n], refs[n:2 * n], refs[2 * n:2 * n + riders]
    outs, rout = refs[2 * n + riders:3 * n + riders], refs[3 * n + riders:]
    x = x_ref[...].astype(BF16)
    for w_ref, b_ref, o_ref, act in zip(ws, bs, outs, acts):
        res = _dot(x, w_ref[...]) + b_ref[...]
        o_ref[...] = (res if act is None else act(res)).astype(o_ref.dtype)
    _cast_riders(rin, rout)


def _projections(x, heads, riders=(), tm=512):
    m, k = x.shape

    def resident(a):
        return pl.BlockSpec(a.shape, lambda i: (0, 0), pipeline_mode=pl.Buffered(1))

    ws = [h[0] for h in heads]
    bs = [h[1] for h in heads]
    rider_specs, rider_shapes = _rider_specs(riders, m // tm)
    return pl.pallas_call(
        functools.partial(_projections_kernel, acts=tuple(h[3] for h in heads), riders=len(riders)),
        grid=(m // tm,),
        in_specs=([pl.BlockSpec((tm, k), lambda i: (i, 0))] + [resident(w) for w in ws] + [resident(b) for b in bs]
                  + rider_specs),
        out_specs=[pl.BlockSpec((tm, w.shape[1]), lambda i: (i, 0)) for w in ws] + rider_specs,
        out_shape=[jax.ShapeDtypeStruct((m, h[0].shape[1]), h[2]) for h in heads] + rider_shapes,
        compiler_params=_cparams("arbitrary"),
        name="projections",
    )(x, *ws, *bs, *riders)


MAX_ROW_STRIDE = 4


def _linear_dilated_kernel(x_ref, w_ref, b_ref, o_ref, acc, acc2, *, dil):
    res = _dot(x_ref[...], w_ref[...]) + b_ref[...]
    slabs, rows, _ = acc.shape
    for c in range(slabs):
        acc[c] = res[:, c * LANES:(c + 1) * LANES]
    sub = rows // dil
    if dil <= MAX_ROW_STRIDE:
        for r in range(dil):
            for c in range(slabs):
                o_ref[r, :, c * LANES:(c + 1) * LANES] = acc[c, pl.ds(r, sub, stride=dil), :].astype(o_ref.dtype)
        return
    s1 = MAX_ROW_STRIDE
    s2 = dil // s1
    part = rows // s1
    for r1 in range(s1):
        for c in range(slabs):
            acc2[c, r1 * part:(r1 + 1) * part, :] = acc[c, pl.ds(r1, part, stride=s1), :]
    for r1 in range(s1):
        for r2 in range(s2):
            for c in range(slabs):
                o_ref[r1 + s1 * r2, :, c * LANES:(c + 1) * LANES] = (
                    acc2[c, pl.ds(r1 * part + r2, sub, stride=s2), :].astype(o_ref.dtype))


def _linear_dilated(x, w, b, dil, batch, seq, tm=2048):
    k = x.shape[1]
    n = w.shape[1]
    tm = min(tm, seq)
    tiles_per_seq = seq // tm

    def resident(a):
        return pl.BlockSpec(a.shape, lambda i: (0, 0), pipeline_mode=pl.Buffered(1))

    return pl.pallas_call(
        functools.partial(_linear_dilated_kernel, dil=dil),
        grid=(batch * tiles_per_seq,),
        in_specs=[pl.BlockSpec((tm, k), lambda i: (i, 0)), resident(w), resident(b)],
        out_specs=pl.BlockSpec((None, dil, tm // dil, n),
                               lambda i: (i // tiles_per_seq, 0, i % tiles_per_seq, 0)),
        out_shape=jax.ShapeDtypeStruct((batch, dil, seq // dil, n), BF16),
        scratch_shapes=[pltpu.VMEM((n // LANES, tm, LANES), F32),
                        pltpu.VMEM((n // LANES, tm if dil > MAX_ROW_STRIDE else 8, LANES), F32)],
        compiler_params=_cparams("parallel"),
        name=f"linear_dil{dil}",
    )(x, w, b)


def _rope_table_kernel(pos_ref, freq_ref, sign_ref, cos_ref, sin_ref):
    ang = pos_ref[...] * freq_ref[...]
    cos_ref[...] = jnp.cos(ang)
    sin_ref[...] = jnp.sin(ang) * sign_ref[...]


def _rope_tables(positions):
    n = positions.size
    half = ATT_HEAD_DIM // 2
    freqs = ROPE_THETA ** (-jnp.arange(half, dtype=F32) * (2.0 / ATT_HEAD_DIM))
    lane = np.arange(LANES)
    freq_row = freqs[lane % half][None, :]
    sign_row = jnp.asarray(np.where(lane % ATT_HEAD_DIM < half, -1.0, 1.0), F32)[None, :]
    pos = positions.astype(F32).reshape(n, 1)
    tm = 2048
    return pl.pallas_call(
        _rope_table_kernel,
        grid=(n // tm,),
        in_specs=[pl.BlockSpec((tm, 1), lambda i: (i, 0)),
                  pl.BlockSpec((1, LANES), lambda i: (0, 0)),
                  pl.BlockSpec((1, LANES), lambda i: (0, 0))],
        out_specs=[pl.BlockSpec((tm, LANES), lambda i: (i, 0))] * 2,
        out_shape=[jax.ShapeDtypeStruct((n, LANES), F32)] * 2,
        compiler_params=_cparams("parallel"),
        name="rope_tables",
    )(pos, freq_row, sign_row)


def _attn_kernel(q_ref, k_ref, v_ref, cos_ref, sin_ref, o_ref, lse_ref, qs, ks, vs, *, sub, dil):
    per_step = q_ref.shape[0]
    for rr in range(per_step):
        _attn_residue(q_ref.at[rr], k_ref.at[rr], v_ref.at[rr], cos_ref, sin_ref, o_ref.at[rr], lse_ref.at[rr],
                      qs.at[rr], ks.at[rr], vs.at[rr], pl.program_id(1) * per_step + rr, sub=sub, dil=dil)


def _attn_residue(q_ref, k_ref, v_ref, cos_ref, sin_ref, o_ref, lse_ref, qs, ks, vs, res, *, sub, dil):
    nb = sub // BLOCK
    lane = lax.broadcasted_iota(jnp.int32, (BLOCK, LANES), 1)
    first_half = (lane % ATT_HEAD_DIM) < (ATT_HEAD_DIM // 2)
    low_head = lane < ATT_HEAD_DIM

    def rope(x, c, s):
        partner = jnp.where(first_half, pltpu.roll(x, LANES - 32, 1), pltpu.roll(x, 32, 1))
        return x * c + partner * s

    ks[0:BLOCK, :] = jnp.zeros((BLOCK, ATT_GROUP_WIDTH), BF16)
    vs[0:BLOCK, :] = jnp.zeros((BLOCK, ATT_GROUP_WIDTH), BF16)

    def prep(n, carry):
        r0 = pl.multiple_of(n * BLOCK, BLOCK)
        if dil == 1:
            c = cos_ref[pl.ds(r0, BLOCK), :]
            s = sin_ref[pl.ds(r0, BLOCK), :]
        else:
            c = cos_ref[pl.ds(r0 * dil + res, BLOCK, stride=dil), :]
            s = sin_ref[pl.ds(r0 * dil + res, BLOCK, stride=dil), :]
        for pair in range(2):
            cols = slice(pair * LANES, (pair + 1) * LANES)
            q = rope(q_ref[pl.ds(r0, BLOCK), cols].astype(F32), c, s) * (ATT_HEAD_DIM ** -0.5)
            qs[pl.ds(r0, BLOCK), (2 * pair) * LANES:(2 * pair + 1) * LANES] = jnp.where(low_head, q, 0.0).astype(BF16)
            qs[pl.ds(r0, BLOCK), (2 * pair + 1) * LANES:(2 * pair + 2) * LANES] = jnp.where(low_head, 0.0, q).astype(BF16)
            k = rope(k_ref[pl.ds(r0, BLOCK), cols].astype(F32), c, s)
            ks[pl.ds(r0 + BLOCK, BLOCK), cols] = k.astype(BF16)
        vs[pl.ds(r0 + BLOCK, BLOCK), :] = v_ref[pl.ds(r0, BLOCK), :]
        return carry

    if nb <= 2:
        for n in range(nb):
            prep(n, 0)
    else:
        lax.fori_loop(0, nb, prep, 0)

    qi = lax.broadcasted_iota(jnp.int32, (BLOCK, 2 * BLOCK), 0)
    ki = lax.broadcasted_iota(jnp.int32, (BLOCK, 2 * BLOCK), 1)
    dist = qi + BLOCK - ki
    band = (dist >= 0) & (dist <= BLOCK)
    neg_inf = jnp.float32(-jnp.inf)
    bias_rest = jnp.where(band, 0.0, neg_inf)
    bias_first = jnp.where(band & (ki >= BLOCK), 0.0, neg_inf)

    def block(n, bias):
        r0 = pl.multiple_of(n * BLOCK, BLOCK)
        heads = range(ATT_HEADS_PER_GROUP)
        pair_cols = [slice(pair * LANES, (pair + 1) * LANES) for pair in range(2)]
        kw = [ks[pl.ds(r0, 2 * BLOCK), c] for c in pair_cols]
        vw = [vs[pl.ds(r0, 2 * BLOCK), c] for c in pair_cols]
        s = [_dot_nt(qs[pl.ds(r0, BLOCK), hd * LANES:(hd + 1) * LANES], kw[hd // 2]) + bias for hd in heads]
        m = [jnp.max(s[hd], axis=-1, keepdims=True) for hd in heads]
        p = [jnp.exp(s[hd] - m[hd]) for hd in heads]
        den = [jnp.sum(p[hd], axis=-1, keepdims=True) for hd in heads]
        o = [_dot(p[hd].astype(BF16), vw[hd // 2]) * (1.0 / den[hd]) for hd in heads]
        lse = [m[hd] + jnp.log(den[hd]) for hd in heads]
        for pair in range(2):
            lo, hi = 2 * pair, 2 * pair + 1
            o_ref[pl.ds(r0, BLOCK), pair_cols[pair]] = jnp.where(low_head, o[lo], o[hi]).astype(o_ref.dtype)
            lse_ref[pl.ds(r0, BLOCK), pair_cols[pair]] = jnp.where(low_head, lse[lo], lse[hi])

    block(0, bias_first)
    if nb <= 2:
        for n in range(1, nb):
            block(n, bias_rest)
        return

    def body(n, carry):
        block(n, bias_rest)
        return carry

    lax.fori_loop(1, nb, body, 0, unroll=4)


def _attention_group(qkv, cos, sin, group, batch, seq):
    window, dil = ATT_GROUPS[group]
    assert window // dil == BLOCK
    sub = seq // dil
    assert sub % BLOCK == 0

    per_step = min(dil, max(1, ATT_MIN_ROWS_PER_STEP // sub))

    def part(p):
        return pl.BlockSpec((None, per_step, sub, ATT_GROUP_WIDTH), lambda b, r: (b, r, 0, p))

    tab = pl.BlockSpec((seq, LANES), lambda b, r: (b, 0))
    out = pl.BlockSpec((None, per_step, sub, ATT_GROUP_WIDTH), lambda b, r: (b, r, 0, 0))
    return pl.pallas_call(
        functools.partial(_attn_kernel, sub=sub, dil=dil),
        grid=(batch, dil // per_step),
        in_specs=[part(0), part(1), part(2), tab, tab],
        out_specs=[out, out],
        out_shape=[jax.ShapeDtypeStruct((batch, dil, sub, ATT_GROUP_WIDTH), BF16),
                   jax.ShapeDtypeStruct((batch, dil, sub, ATT_GROUP_WIDTH), F32)],
        scratch_shapes=[pltpu.VMEM((per_step, sub, ATT_HEADS_PER_GROUP * LANES), BF16),
                        pltpu.VMEM((per_step, sub + BLOCK, ATT_GROUP_WIDTH), BF16),
                        pltpu.VMEM((per_step, sub + BLOCK, ATT_GROUP_WIDTH), BF16)],
        compiler_params=_cparams("parallel", "parallel"),
        name=f"attention_g{group}",
    )(qkv, qkv, qkv, cos, sin)


ML_SEQS_PER_STEP = 2


def _mlstm_kernel(qk_ref, v_ref, og_ref, g_ref, cw_ref, cb_ref, sc_ref, out_ref, frame, ct, m_s):
    taps = ML_CONV

    @pl.when(pl.program_id(1) == 0)
    def _():
        frame[:, 0:ML_CHUNK, :] = jnp.zeros((frame.shape[0], ML_CHUNK, 2 * ML_PAD_WIDTH), BF16)
        ct[...] = jnp.zeros(ct.shape, F32)
        m_s[...] = jnp.zeros(m_s.shape, F32)

    out_row = lax.broadcasted_iota(jnp.int32, (ML_CHUNK, 2 * ML_CHUNK), 0)
    src_row = lax.broadcasted_iota(jnp.int32, (ML_CHUNK, 2 * ML_CHUNK), 1)
    picks = [jnp.where(src_row == out_row + (ML_CHUNK - (taps - 1 - j)), 1.0, 0.0).astype(BF16)
             for j in range(taps - 1)]
    row = lax.broadcasted_iota(jnp.int32, (ML_CHUNK, ML_CHUNK), 0)
    colm = lax.broadcasted_iota(jnp.int32, (ML_CHUNK, ML_CHUNK), 1)
    causal = row >= colm
    tri = jnp.where(causal, 1.0, 0.0).astype(BF16)
    neg_inf = jnp.float32(-jnp.inf)

    seqs = range(qk_ref.shape[0])
    pairs = [(s, h) for s in seqs for h in range(ML_HEADS)]

    def head_cols(h):
        return slice(h * ML_HEAD_PAD, (h + 1) * ML_HEAD_PAD)

    act, gates, bcum, bcum_t, gates_t = {}, {}, {}, {}, {}
    for s in seqs:
        cur = qk_ref[s]
        frame[s, ML_CHUNK:2 * ML_CHUNK, :] = cur
        both = frame[s]
        acc = cb_ref[...] + cur.astype(F32) * cw_ref[taps - 1:taps, :]
        for j in range(taps - 1):
            acc = acc + _dot(picks[j], both) * cw_ref[j:j + 1, :]
        frame[s, 0:ML_CHUNK, :] = cur
        act[s] = (acc * _sigmoid(acc) * sc_ref[...]).astype(BF16)
        gates[s] = g_ref[s]
        log_f = -(jnp.maximum(-gates[s], 0.0) + jnp.log1p(jnp.exp(-jnp.abs(gates[s]))))
        hi, mid, lo = _split3(log_f)
        bcum[s] = _dot(tri, hi) + _dot(tri, mid) + _dot(tri, lo)
        bcum_t[s] = bcum[s].T
        gates_t[s] = gates[s].T

    qh, kh, vh, c_old, m_prev, b_c, i_c, log_d, m_inter, m = ({} for _ in range(10))
    for p in pairs:
        s, h = p
        qh[p] = act[s][:, head_cols(h)]
        kh[p] = act[s][:, ML_PAD_WIDTH + h * ML_HEAD_PAD:ML_PAD_WIDTH + (h + 1) * ML_HEAD_PAD]
        vh[p] = v_ref[s, :, head_cols(h)]
        c_old[p] = ct[s, h]
        m_prev[p] = m_s[s, h, 0:1, 0:1]
        b_c[p] = bcum[s][:, ML_HEADS + h:ML_HEADS + h + 1]
        b_r = bcum_t[s][ML_HEADS + h:ML_HEADS + h + 1, :]
        i_c[p] = gates[s][:, h:h + 1]
        i_r = gates_t[s][h:h + 1, :]
        log_d[p] = jnp.where(causal, b_c[p] - b_r + i_r, neg_inf)
        m_inter[p] = b_c[p] + m_prev[p]
        m[p] = jnp.maximum(m_inter[p], jnp.max(log_d[p], axis=-1, keepdims=True))

    scores = {p: _dot_nt(qh[p], kh[p]) for p in pairs}
    carried = {p: _dot(qh[p], c_old[p].astype(BF16)) for p in pairs}
    sc = {p: (scores[p] * jnp.exp(log_d[p] - m[p])).astype(BF16) for p in pairs}
    num = {p: _dot(sc[p], vh[p]) + jnp.exp(m_inter[p] - m[p]) * carried[p] for p in pairs}
    for p in pairs:
        s, h = p
        den = num[p][:, ML_HEAD_DIM:ML_HEAD_DIM + 1]
        hid = num[p] / jnp.maximum(jnp.abs(den), jnp.exp(-m[p]))
        out_ref[s, :, head_cols(h)] = (og_ref[s, :, head_cols(h)].astype(F32) * hid).astype(out_ref.dtype)
    for p in pairs:
        s, h = p
        b_last = b_c[p][ML_CHUNK - 1:ML_CHUNK, :]
        m_new = m[p][ML_CHUNK - 1:ML_CHUNK, :]
        w = jnp.exp(b_last - b_c[p] + i_c[p] - m_new)
        decay = jnp.exp(b_last + m_prev[p] - m_new)
        wv = (w * vh[p].astype(F32)).astype(BF16)
        ct[s, h] = decay * c_old[p] + _dot_tn(kh[p], wv)
        m_s[s, h] = jnp.broadcast_to(m_new, m_s.shape[2:])


def _mlstm(ml, og, gates, conv_w, conv_b, qk_scale, batch, seq):
    nc = seq // ML_CHUNK
    group = ML_SEQS_PER_STEP if batch % ML_SEQS_PER_STEP == 0 else 1

    def whole(a):
        return pl.BlockSpec(a.shape, lambda b, c: (0, 0))

    def chunk(width, col):
        return pl.BlockSpec((group, ML_CHUNK, width), lambda b, c: (b, c, col))

    out = pl.pallas_call(
        _mlstm_kernel,
        grid=(batch // group, nc),
        in_specs=[chunk(2 * ML_PAD_WIDTH, 0), chunk(ML_PAD_WIDTH, 2), chunk(ML_PAD_WIDTH, 0), chunk(LANES, 0),
                  whole(conv_w), whole(conv_b), whole(qk_scale)],
        out_specs=chunk(ML_PAD_WIDTH, 0),
        out_shape=jax.ShapeDtypeStruct((batch, seq, ML_PAD_WIDTH), BF16),
        scratch_shapes=[pltpu.VMEM((group, 2 * ML_CHUNK, 2 * ML_PAD_WIDTH), BF16),
                        pltpu.VMEM((group, ML_HEADS, ML_HEAD_PAD, ML_HEAD_PAD), F32),
                        pltpu.VMEM((group, ML_HEADS, 8, LANES), F32)],
        compiler_params=_cparams("parallel", "arbitrary"),
        name="mlstm",
    )(ml.reshape(batch, seq, -1), ml.reshape(batch, seq, -1), og.reshape(batch, seq, -1),
      gates.reshape(batch, seq, -1), conv_w, conv_b, qk_scale)
    return out.reshape(batch * seq, ML_PAD_WIDTH)


def _mixer_out_kernel(x_ref, o0_ref, l0_ref, o1_ref, l1_ref, o2_ref, l2_ref, sg_ref, hc_ref, gp_ref,
                      sgg_ref, sgb_ref, sgw_ref, sgbias_ref, wa_ref, wb_ref, wc_ref, wo_ref, lng_ref, lnb_ref,
                      xf_ref, xb_ref, unperm, *, alpha):
    tm = x_ref.shape[0]
    slabs = ATT_GROUP_WIDTH // LANES
    att_refs = (l0_ref, l1_ref, l2_ref, o0_ref, o1_ref, o2_ref)

    for slot, ref in enumerate(att_refs):
        dil = ref.shape[0]
        if dil > 1:
            for r in range(dil):
                for c in range(slabs):
                    unperm[slabs * slot + c, pl.ds(r, tm // dil, stride=dil), :] = (
                        ref[r, :, c * LANES:(c + 1) * LANES].astype(F32))

    def token_order(slot, rows):
        ref = att_refs[slot]
        if ref.shape[0] == 1:
            return ref[0, rows, :].astype(F32)
        return jnp.concatenate([unperm[slabs * slot + c, rows, :] for c in range(slabs)], axis=1)

    row = lax.broadcasted_iota(jnp.int32, (SG_CHUNK, SG_CHUNK), 0)
    colm = lax.broadcasted_iota(jnp.int32, (SG_CHUNK, SG_CHUNK), 1)
    causal = row >= colm
    group_dim = SG_WIDTH // SG_GROUPS
    sg_w = [jnp.where(causal, sgw_ref[g], 0.0).astype(BF16) for g in range(SG_GROUPS)]

    subs = [slice(h * MIXER_SUB_ROWS, (h + 1) * MIXER_SUB_ROWS) for h in range(tm // MIXER_SUB_ROWS)]

    def gate(rows, i):
        return gp_ref[rows, i * D_MODEL:(i + 1) * D_MODEL].astype(F32)

    proj_c = [_dot(hc_ref[rows, :], wc_ref[...]) for rows in subs]

    y_a = []
    for rows in subs:
        l0, l1, l2 = token_order(0, rows), token_order(1, rows), token_order(2, rows)
        lmax = jnp.maximum(jnp.maximum(l0, l1), l2)
        e0, e1, e2 = jnp.exp(l0 - lmax), jnp.exp(l1 - lmax), jnp.exp(l2 - lmax)
        inv = 1.0 / (e0 + e1 + e2)
        y_a.append(((e0 * inv) * token_order(3, rows) + (e1 * inv) * token_order(4, rows)
                    + (e2 * inv) * token_order(5, rows)).astype(BF16))
    proj_a = [_dot(y, wa_ref[...]) for y in y_a]

    v = [_layer_norm(sg_ref[rows, SG_WIDTH:2 * SG_WIDTH].astype(F32), sgg_ref[...], sgb_ref[...]).astype(BF16)
         for rows in subs]
    y_b = []
    for rows, vs in zip(subs, v):
        chunks = []
        for c in range(MIXER_SUB_ROWS // SG_CHUNK):
            crows = slice(c * SG_CHUNK, (c + 1) * SG_CHUNK)
            parts = [_dot(sg_w[g], vs[crows, g * group_dim:(g + 1) * group_dim]) + sgbias_ref[:, g:g + 1]
                     for g in range(SG_GROUPS)]
            chunks.append(jnp.concatenate(parts, axis=1))
        y_b.append((sg_ref[rows, 0:SG_WIDTH].astype(F32) * jnp.concatenate(chunks, axis=0)).astype(BF16))
    proj_b = [_dot(y, wb_ref[...]) for y in y_b]

    z = [(gate(rows, 0) * pa + gate(rows, 1) * pb + gate(rows, 2) * pc).astype(BF16)
         for rows, pa, pb, pc in zip(subs, proj_a, proj_b, proj_c)]
    mix = [_dot(zs, wo_ref[...]) for zs in z]
    for rows, m in zip(subs, mix):
        out = _layer_norm(alpha * x_ref[rows, :] + m, lng_ref[...], lnb_ref[...])
        xf_ref[rows, :] = out
        xb_ref[rows, :] = out.astype(BF16)


def _mixer_out(x, att_outs, sg, hc, gpre, p, alpha, seq, tm=512):
    n = x.shape[0]
    tiles_per_seq = seq // tm

    def rows(width):
        return pl.BlockSpec((tm, width), lambda i: (i, 0))

    def whole(a):
        return pl.BlockSpec(a.shape, lambda i: (0,) * a.ndim)

    def residue_major(dil):
        return pl.BlockSpec((None, dil, tm // dil, ATT_GROUP_WIDTH),
                            lambda i: (i // tiles_per_seq, 0, i % tiles_per_seq, 0))

    att_specs, att_args = [], []
    for (o, lse), (_, dil) in zip(att_outs, ATT_GROUPS):
        att_specs += [residue_major(dil)] * 2
        att_args += [o, lse]
    weights = [p["sg_ln_g"], p["sg_ln_b"], p["sg_w"], p["sg_bias_t"], p["w_br_a"], p["w_br_b"], p["w_br_c"],
               p["w_out"], p["ln_g0"], p["ln_b0"]]
    return pl.pallas_call(
        functools.partial(_mixer_out_kernel, alpha=alpha),
        grid=(n // tm,),
        in_specs=[rows(D_MODEL)] + att_specs
                 + [rows(2 * SG_WIDTH), rows(ML_PAD_WIDTH), rows(N_BRANCH * D_MODEL)]
                 + [whole(w) for w in weights],
        out_specs=[rows(D_MODEL), rows(D_MODEL)],
        out_shape=[jax.ShapeDtypeStruct((n, D_MODEL), F32), jax.ShapeDtypeStruct((n, D_MODEL), BF16)],
        scratch_shapes=[pltpu.VMEM((2 * len(ATT_GROUPS) * ATT_GROUP_WIDTH // LANES, tm, LANES), F32)],
        compiler_params=_cparams("parallel"),
        name="mixer_out",
    )(x, *att_args, sg, hc, gpre, *weights)


def _swiglu_tile(x, w1_ref, w3_ref, w2_ref, prod, tf):
    for c in range(D_FF // tf):
        cols = slice(c * tf, (c + 1) * tf)
        h1 = _dot(x, w1_ref[:, cols])
        h3 = _dot(x, w3_ref[:, cols])
        prod[:, cols] = (h1 * _sigmoid(h1) * h3).astype(BF16)
    return _dot(prod[...], w2_ref[...])


def _ffn_kernel(xf_ref, xb_ref, w1_ref, w3_ref, w2_ref, lng_ref, lnb_ref, *refs, alpha, tf, riders):
    rin, (of_ref, ob_ref), rout, prod = refs[:riders], refs[riders:riders + 2], refs[riders + 2:-1], refs[-1]
    f = _swiglu_tile(xb_ref[...], w1_ref, w3_ref, w2_ref, prod, tf)
    out = _layer_norm(alpha * xf_ref[...] + f, lng_ref[...], lnb_ref[...])
    of_ref[...] = out
    ob_ref[...] = out.astype(BF16)
    _cast_riders(rin, rout)


def _ffn(xf, xb, w1, w3, w2, ln_g, ln_b, alpha, riders=(), tm=512, tf=256):
    n = xf.shape[0]

    def resident(shape):
        return pl.BlockSpec(shape, lambda i: (0, 0), pipeline_mode=pl.Buffered(1))

    rider_specs, rider_shapes = _rider_specs(riders, n // tm)
    return pl.pallas_call(
        functools.partial(_ffn_kernel, alpha=alpha, tf=tf, riders=len(riders)),
        grid=(n // tm,),
        in_specs=[pl.BlockSpec((tm, D_MODEL), lambda i: (i, 0)),
                  pl.BlockSpec((tm, D_MODEL), lambda i: (i, 0)),
                  resident((D_MODEL, D_FF)), resident((D_MODEL, D_FF)), resident((D_FF, D_MODEL)),
                  pl.BlockSpec((1, D_MODEL), lambda i: (0, 0)),
                  pl.BlockSpec((1, D_MODEL), lambda i: (0, 0))] + rider_specs,
        out_specs=[pl.BlockSpec((tm, D_MODEL), lambda i: (i, 0))] * 2 + rider_specs,
        out_shape=[jax.ShapeDtypeStruct((n, D_MODEL), F32), jax.ShapeDtypeStruct((n, D_MODEL), BF16)]
                  + rider_shapes,
        scratch_shapes=[pltpu.VMEM((tm, D_FF), BF16)],
        compiler_params=_cparams("arbitrary"),
        name="ffn",
    )(xf, xb, w1, w3, w2, ln_g, ln_b, *riders)


def _router_kernel(x_ref, w_ref, b_ref, info_ref):
    xh, xm, _ = _split3(x_ref[...])
    wh, wm, _ = _split3(w_ref[...])
    logits = (_dot(xh, wh) + (_dot(xh, wm) + _dot(xm, wh))) + b_ref[...]
    lane = lax.broadcasted_iota(jnp.int32, logits.shape, 1)
    neg_inf = jnp.float32(-jnp.inf)
    logits = jnp.where(lane < N_EXPERTS, logits, neg_inf)
    v1 = jnp.max(logits, axis=-1, keepdims=True)
    i1 = jnp.min(jnp.where(logits == v1, lane, LANES), axis=-1, keepdims=True)
    rest = jnp.where(lane == i1, neg_inf, logits)
    v2 = jnp.max(rest, axis=-1, keepdims=True)
    i2 = jnp.min(jnp.where(rest == v2, lane, LANES), axis=-1, keepdims=True)
    e = jnp.exp(v2 - v1)
    inv = 1.0 / (1.0 + e)
    info_ref[...] = jnp.where(lane == 0, i1.astype(F32),
                              jnp.where(lane == 1, i2.astype(F32),
                                        jnp.where(lane == 2, inv, jnp.where(lane == 3, e * inv, 0.0))))


def _router(xf, router_w, router_b, tm=1024):
    n = xf.shape[0]
    w = jnp.pad(router_w, ((0, 0), (0, LANES - N_EXPERTS)))
    b = jnp.pad(router_b, (0, LANES - N_EXPERTS))[None, :]
    return pl.pallas_call(
        _router_kernel,
        grid=(n // tm,),
        in_specs=[pl.BlockSpec((tm, D_MODEL), lambda i: (i, 0)),
                  pl.BlockSpec((D_MODEL, LANES), lambda i: (0, 0)),
                  pl.BlockSpec((1, LANES), lambda i: (0, 0))],
        out_specs=pl.BlockSpec((tm, LANES), lambda i: (i, 0)),
        out_shape=jax.ShapeDtypeStruct((n, LANES), F32),
        compiler_params=_cparams("parallel"),
        name="router",
    )(xf, w, b)


def _routing_plan(info, n):
    tile = EXPERT_TILE
    n_tiles = TOP_K * n // tile + N_EXPERTS
    expert = info[:, 0:TOP_K].astype(jnp.int32).reshape(-1)
    onehot = (expert[:, None] == jnp.arange(N_EXPERTS, dtype=jnp.int32)[None, :]).astype(jnp.int32)
    running = jnp.cumsum(onehot, axis=0)
    rank = jnp.sum(onehot * running, axis=1) - 1
    counts = running[-1]
    padded = ((counts + tile - 1) // tile) * tile
    ends = jnp.cumsum(padded)
    starts = ends - padded
    dest = jnp.sum(onehot * starts[None, :], axis=1) + rank
    used = ends[-1] // tile
    tile_start = jnp.arange(n_tiles, dtype=jnp.int32) * tile
    tile_expert = jnp.sum((tile_start[:, None] >= ends[None, :]).astype(jnp.int32), axis=1)
    last_expert = jnp.sum((tile_start[used - 1] >= ends).astype(jnp.int32))
    tile_expert = jnp.minimum(tile_expert, last_expert)
    plan = dict(dest=dest, tile_expert=tile_expert, used=used.reshape(1),
                pad_start=starts + counts, pad_count=padded - counts)
    return {k: v.astype(jnp.int32) for k, v in plan.items()}, n_tiles


def _rows_wait(src, dst, rows, sem):
    pltpu.make_async_copy(src.at[pl.ds(0, rows)], dst.at[pl.ds(0, rows)], sem).wait()


ROW_SUBLANES = D_MODEL // LANES


def _tile_copy(src, src_row, dst, dst_row, sem):
    s0 = pl.multiple_of(src_row * ROW_SUBLANES, ROW_SUBLANES)
    d0 = pl.multiple_of(dst_row * ROW_SUBLANES, ROW_SUBLANES)
    return pltpu.make_async_copy(src.at[pl.ds(s0, ROW_SUBLANES)], dst.at[pl.ds(d0, ROW_SUBLANES)], sem)


def _to_row_tiles(ref, x):
    m = x.shape[0]
    for c in range(ROW_SUBLANES):
        ref[pl.ds(c, m, stride=ROW_SUBLANES), :] = x[:, c * LANES:(c + 1) * LANES]


def _from_row_tiles(ref):
    m = ref.shape[0] // ROW_SUBLANES
    return jnp.concatenate([ref[pl.ds(c, m, stride=ROW_SUBLANES), :] for c in range(ROW_SUBLANES)], axis=1)


DISPATCH_SLOTS = 3


def _dispatch_kernel(pad_start_ref, pad_count_ref, used_ref, dest_ref, x_ref, xs_ref, buf, tiles, zero_row,
                     load_sem, row_sem, pad_sem, *, tm):
    i = pl.program_id(0)

    @pl.when(i == 0)
    def _():
        zero_row[...] = jnp.zeros(zero_row.shape, F32)
        for e in range(N_EXPERTS):
            def fill(j, carry, e=e):
                _tile_copy(zero_row, 0, xs_ref, pad_start_ref[e] + j, pad_sem).start()
                return carry
            lax.fori_loop(0, pad_count_ref[e], fill, 0)
        for e in range(N_EXPERTS):
            def drain(j, carry):
                _tile_copy(zero_row, 0, xs_ref, 0, pad_sem).wait()
                return carry
            lax.fori_loop(0, pad_count_ref[e], drain, 0)

        tile_rows = tm * ROW_SUBLANES
        spare = xs_ref.shape[0] // tile_rows - used_ref[0]
        tiles[1] = jnp.zeros(tiles.shape[1:], F32)

        def zero_tile(j):
            row0 = pl.multiple_of((used_ref[0] + j) * tile_rows, tile_rows)
            return pltpu.make_async_copy(tiles.at[1], xs_ref.at[pl.ds(row0, tile_rows)], pad_sem)

        def fill_tile(j, carry):
            zero_tile(j).start()
            return carry

        def drain_tile(j, carry):
            zero_tile(j).wait()
            return carry

        lax.fori_loop(0, spare, fill_tile, 0)
        lax.fori_loop(0, spare, drain_tile, 0)

    steps = pl.num_programs(0)
    slot = i % DISPATCH_SLOTS
    prev_slot = (i + DISPATCH_SLOTS - 1) % DISPATCH_SLOTS
    stage = i % 2

    def load(tile, into):
        return pltpu.make_async_copy(x_ref.at[pl.ds(tile * tm, tm), :], buf.at[into], load_sem.at[into])

    def wait_rows(of_stage):
        for _ in range(TOP_K):
            _rows_wait(tiles.at[of_stage], xs_ref, tm * ROW_SUBLANES, row_sem.at[of_stage])

    @pl.when(i == 0)
    def _():
        load(0, 0).start()

        @pl.when(steps > 1)
        def _():
            load(1, 1).start()

    load(i, slot).wait()
    _to_row_tiles(tiles.at[stage], buf[slot])

    def start(t, carry):
        for c in range(TOP_K):
            _tile_copy(tiles.at[stage], t, xs_ref, dest_ref[0, 0, TOP_K * t + c], row_sem.at[stage]).start(
                priority=c % 2)
        return carry

    lax.fori_loop(0, tm, start, 0, unroll=DMA_ISSUE_UNROLL)

    @pl.when(i > 0)
    def _():
        wait_rows(1 - stage)

    @pl.when(i + 2 < steps)
    def _():
        load(i + 2, prev_slot).start()

    @pl.when(i == steps - 1)
    def _():
        wait_rows(stage)


def _dispatch(xf, dest, pad_start, pad_count, used, n_slots, tm=EXPERT_TILE):
    n = xf.shape[0]
    dest3 = dest.reshape(n // tm, 1, TOP_K * tm)
    grid_spec = pltpu.PrefetchScalarGridSpec(
        num_scalar_prefetch=3,
        grid=(n // tm,),
        in_specs=[pl.BlockSpec((1, 1, TOP_K * tm), lambda i, ps, pc, nu: (i, 0, 0), memory_space=pltpu.SMEM),
                  pl.BlockSpec(memory_space=pl.ANY)],
        out_specs=pl.BlockSpec(memory_space=pl.ANY),
        scratch_shapes=[pltpu.VMEM((DISPATCH_SLOTS, tm, D_MODEL), F32),
                        pltpu.VMEM((2, tm * ROW_SUBLANES, LANES), F32),
                        pltpu.VMEM((ROW_SUBLANES, LANES), F32),
                        pltpu.SemaphoreType.DMA((DISPATCH_SLOTS,)),
                        pltpu.SemaphoreType.DMA((2,)),
                        pltpu.SemaphoreType.DMA],
    )
    return pl.pallas_call(
        functools.partial(_dispatch_kernel, tm=tm),
        grid_spec=grid_spec,
        out_shape=jax.ShapeDtypeStruct((n_slots * ROW_SUBLANES, LANES), F32),
        compiler_params=_cparams("arbitrary"),
        name="moe_dispatch",
    )(pad_start, pad_count, used, dest3, xf)


def _experts_kernel(te_ref, used_ref, xs_ref, w1_ref, w3_ref, w2_ref, ys_ref, prod, *, tf):
    del te_ref
    i = pl.program_id(0)

    @pl.when(i < used_ref[0])
    def _():
        x = _from_row_tiles(xs_ref).astype(BF16)
        _to_row_tiles(ys_ref, _swiglu_tile(x, w1_ref, w3_ref, w2_ref, prod, tf))

    @pl.when(i >= used_ref[0])
    def _():
        ys_ref[...] = jnp.zeros(ys_ref.shape, ys_ref.dtype)


def _experts(xs, tile_expert, used, w1, w3, w2, n_tiles, tf=256):
    tile = EXPERT_TILE
    grid_spec = pltpu.PrefetchScalarGridSpec(
        num_scalar_prefetch=2,
        grid=(n_tiles,),
        in_specs=[pl.BlockSpec((tile * ROW_SUBLANES, LANES), lambda i, te, nu: (jnp.minimum(i, nu[0] - 1), 0)),
                  pl.BlockSpec((None, D_MODEL, D_FF), lambda i, te, nu: (te[i], 0, 0)),
                  pl.BlockSpec((None, D_MODEL, D_FF), lambda i, te, nu: (te[i], 0, 0)),
                  pl.BlockSpec((None, D_FF, D_MODEL), lambda i, te, nu: (te[i], 0, 0))],
        out_specs=pl.BlockSpec((tile * ROW_SUBLANES, LANES), lambda i, te, nu: (i, 0)),
        scratch_shapes=[pltpu.VMEM((tile, D_FF), BF16)],
    )
    return pl.pallas_call(
        functools.partial(_experts_kernel, tf=tf),
        grid_spec=grid_spec,
        out_shape=jax.ShapeDtypeStruct((n_tiles * tile * ROW_SUBLANES, LANES), F32),
        compiler_params=_cparams("arbitrary"),
        name="moe_experts",
    )(tile_expert, used, xs, w1, w3, w2)


def _combine_kernel(dest_ref, next_dest_ref, xf_ref, info_ref, ys_ref, lng_ref, lnb_ref, of_ref, got, sem, *, alpha):
    tm = xf_ref.shape[0]
    i = pl.program_id(0)
    slot = i % 2

    def gather(dests, into):
        def start(t, carry):
            for c in range(TOP_K):
                _tile_copy(ys_ref, dests[0, 0, TOP_K * t + c], got.at[into, c], t, sem.at[into]).start(
                    priority=c % 2)
            return carry
        lax.fori_loop(0, tm, start, 0, unroll=DMA_ISSUE_UNROLL)

    @pl.when(i == 0)
    def _():
        gather(dest_ref, 0)

    @pl.when(i + 1 < pl.num_programs(0))
    def _():
        gather(next_dest_ref, 1 - slot)

    for c in range(TOP_K):
        _rows_wait(ys_ref, got.at[slot, c], tm * ROW_SUBLANES, sem.at[slot])
    info = info_ref[...]
    mix = (info[:, 2:3] * _from_row_tiles(got.at[slot, 0]) + info[:, 3:4] * _from_row_tiles(got.at[slot, 1]))
    of_ref[...] = _layer_norm(alpha * xf_ref[...] + mix, lng_ref[...], lnb_ref[...])


def _combine(xf, info, ys, dest, ln_g, ln_b, alpha, tm=256):
    n = xf.shape[0]
    steps = n // tm
    dest3 = dest.reshape(steps, 1, TOP_K * tm)
    return pl.pallas_call(
        functools.partial(_combine_kernel, alpha=alpha),
        grid=(steps,),
        in_specs=[pl.BlockSpec((1, 1, TOP_K * tm), lambda i: (i, 0, 0), memory_space=pltpu.SMEM),
                  pl.BlockSpec((1, 1, TOP_K * tm), lambda i: (jnp.minimum(i + 1, steps - 1), 0, 0),
                               memory_space=pltpu.SMEM),
                  pl.BlockSpec((tm, D_MODEL), lambda i: (i, 0)),
                  pl.BlockSpec((tm, LANES), lambda i: (i, 0)),
                  pl.BlockSpec(memory_space=pl.ANY),
                  pl.BlockSpec((1, D_MODEL), lambda i: (0, 0)),
                  pl.BlockSpec((1, D_MODEL), lambda i: (0, 0))],
        out_specs=pl.BlockSpec((tm, D_MODEL), lambda i: (i, 0)),
        out_shape=jax.ShapeDtypeStruct((n, D_MODEL), F32),
        scratch_shapes=[pltpu.VMEM((2, TOP_K, tm * ROW_SUBLANES, LANES), F32), pltpu.SemaphoreType.DMA((2,))],
        compiler_params=_cparams("arbitrary"),
        name="moe_combine",
    )(dest3, dest3, xf, info, ys, ln_g, ln_b)


def _moe(xf, router_w, router_b, w1, w3, w2, ln_g, ln_b, alpha):
    n = xf.shape[0]
    info = _router(xf, router_w, router_b)
    plan, n_tiles = _routing_plan(info, n)
    xs = _dispatch(xf, plan["dest"], plan["pad_start"], plan["pad_count"], plan["used"], n_tiles * EXPERT_TILE)
    ys = _experts(xs, plan["tile_expert"], plan["used"], w1, w3, w2, n_tiles)
    return _combine(xf, info, ys, plan["dest"], ln_g, ln_b, alpha)


def _pad_heads(w):
    lead = w.shape[:-1]
    w = w.reshape(*lead, ML_HEADS, ML_HEAD_DIM)
    w = jnp.pad(w, [(0, 0)] * len(lead) + [(0, 0), (0, ML_HEAD_PAD - ML_HEAD_DIM)])
    return w.reshape(*lead, ML_PAD_WIDTH)


def _group_qkv(w, group):
    gw = ATT_GROUP_WIDTH
    return jnp.concatenate([w[..., part * ATT_WIDTH + group * gw:part * ATT_WIDTH + (group + 1) * gw]
                            for part in range(3)], axis=-1)


def _layer_params(layer, w_in, b_in, conv_w, conv_b, sg_ln_g, sg_ln_b, sg_w, sg_b,
                  w_br_a, w_br_b, w_br_c, w_out, ln_g, ln_b):
    w, b = w_in[layer].astype(BF16), b_in[layer]
    o_att, o_sg, o_qk, o_v, o_o, o_i, o_g = 0, 2304, 3840, 5376, 6144, 6912, 6920
    ones_col = jnp.zeros((ML_HEADS, ML_HEAD_PAD), F32).at[:, ML_HEAD_DIM].set(1.0).reshape(ML_PAD_WIDTH)
    w_qk = jnp.concatenate([_pad_heads(w[:, o_qk:o_qk + ML_WIDTH]), _pad_heads(w[:, o_qk + ML_WIDTH:o_v])], axis=1)
    b_qk = jnp.concatenate([_pad_heads(b[o_qk:o_qk + ML_WIDTH]), _pad_heads(b[o_qk + ML_WIDTH:o_v])])
    qk_scale = jnp.concatenate([jnp.ones((ML_PAD_WIDTH,), F32),
                                jnp.full((ML_PAD_WIDTH,), ML_HEAD_DIM ** -0.5, F32)])
    cw, cb = conv_w[layer], conv_b[layer]
    w_att, b_att = w[:, o_att:o_sg], b[o_att:o_sg]
    return dict(
        w_att=[_group_qkv(w_att, g) for g in range(len(ATT_GROUPS))],
        b_att=[_group_qkv(b_att, g)[None, :] for g in range(len(ATT_GROUPS))],
        w_sg=w[:, o_sg:o_qk], b_sg=b[None, o_sg:o_qk],
        w_ml=jnp.concatenate([w_qk, _pad_heads(w[:, o_v:o_o])], axis=1),
        b_ml=jnp.concatenate([b_qk, _pad_heads(b[o_v:o_o]) + ones_col])[None, :],
        qk_scale=qk_scale[None, :],
        w_mlo=_pad_heads(w[:, o_o:o_i]), b_mlo=_pad_heads(b[o_o:o_i])[None, :],
        w_if=jnp.pad(w[:, o_i:o_g], ((0, 0), (0, LANES - 2 * ML_HEADS))),
        b_if=jnp.pad(b[o_i:o_g], (0, LANES - 2 * ML_HEADS))[None, :],
        w_g=w[:, o_g:], b_g=b[None, o_g:],
        conv_w=jnp.concatenate([_pad_heads(cw[:, :ML_WIDTH]), _pad_heads(cw[:, ML_WIDTH:])], axis=1),
        conv_b=jnp.concatenate([_pad_heads(cb[:ML_WIDTH]), _pad_heads(cb[ML_WIDTH:])])[None, :],
        sg_ln_g=sg_ln_g[layer][None, :], sg_ln_b=sg_ln_b[layer][None, :],
        sg_w=sg_w[layer], sg_bias_t=sg_b[layer].T,
        w_br_a=w_br_a[layer].astype(BF16), w_br_b=w_br_b[layer].astype(BF16),
        w_br_c=_pad_heads(w_br_c[layer].T).T.astype(BF16),
        w_out=w_out[layer].astype(BF16),
        ln_g0=ln_g[layer, 0][None, :], ln_b0=ln_b[layer, 0][None, :],
        ln_g1=ln_g[layer, 1][None, :], ln_b1=ln_b[layer, 1][None, :],
    )


def kernel(x, positions, w_in, b_in, conv_w, conv_b, sg_ln_g, sg_ln_b, sg_w, sg_b, w_br_a, w_br_b, w_br_c, w_out,
           ln_g, ln_b, ffn_w1, ffn_w3, ffn_w2, router_w, router_b, moe_w1, moe_w3, moe_w2):
    batch, seq, _ = x.shape
    n = batch * seq
    depth = w_in.shape[0]
    alpha = (2.0 * depth) ** 0.25
    cos, sin = _rope_tables(positions)
    xf = x.reshape(n, D_MODEL)
    xb = None
    for layer in range(depth):
        p = _layer_params(layer, w_in, b_in, conv_w, conv_b, sg_ln_g, sg_ln_b, sg_w, sg_b,
                          w_br_a, w_br_b, w_br_c, w_out, ln_g, ln_b)
        undilated = [g for g, (_, dil) in enumerate(ATT_GROUPS) if dil == 1]
        j = layer // 2
        if layer % 2 == 0:
            riders = [ffn_w1[j], ffn_w3[j], ffn_w2[j]]
        else:
            riders = [moe_w2[j].reshape(N_EXPERTS * D_FF, D_MODEL)]
        if xb is None:
            riders = riders + [xf]
        sg, ml, og, gates_if, gbr, *rest = _projections(xf if xb is None else xb, [
            (p["w_sg"], p["b_sg"], BF16, _gelu_tanh),
            (p["w_ml"], p["b_ml"], BF16, None),
            (p["w_mlo"], p["b_mlo"], BF16, _sigmoid),
            (p["w_if"], p["b_if"], F32, None),
            (p["w_g"], p["b_g"], BF16, _sigmoid),
        ] + [(p["w_att"][g], p["b_att"][g], BF16, None) for g in undilated],
            riders=riders)
        qkv_plain, cast = rest[:len(undilated)], rest[len(undilated):]
        if xb is None:
            *cast, xb = cast
        att_outs = []
        for g, (_, dil) in enumerate(ATT_GROUPS):
            if dil == 1:
                qkv = qkv_plain[undilated.index(g)].reshape(batch, 1, seq, ATT_PARTS_WIDTH)
            else:
                qkv = _linear_dilated(xb, p["w_att"][g], p["b_att"][g], dil, batch, seq)
            att_outs.append(_attention_group(qkv, cos, sin, g, batch, seq))
        hc = _mlstm(ml, og, gates_if, p["conv_w"], p["conv_b"], p["qk_scale"], batch, seq)
        xf, xb = _mixer_out(xf, att_outs, sg, hc, gbr, p, alpha, seq)
        if layer % 2 == 0:
            next_moe = (layer + 1) // 2 if layer + 1 < depth else None
            riders = [] if next_moe is None else [moe_w1[next_moe].reshape(N_EXPERTS * D_MODEL, D_FF),
                                                  moe_w3[next_moe].reshape(N_EXPERTS * D_MODEL, D_FF)]
            xf, xb, *moe_cast = _ffn(xf, xb, *cast, p["ln_g1"], p["ln_b1"], alpha, riders=riders)
        else:
            if layer == 0:
                moe_cast = [moe_w1[j].astype(BF16), moe_w3[j].astype(BF16)]
            w1, w3 = (m.reshape(N_EXPERTS, D_MODEL, D_FF) for m in moe_cast)
            w2 = cast[0].reshape(N_EXPERTS, D_FF, D_MODEL)
            xf = _moe(xf, router_w[j], router_b[j], w1, w3, w2, p["ln_g1"], p["ln_b1"], alpha)
            xb = xf.astype(BF16)
    return xf.reshape(batch, seq, D_MODEL)
```

```python
import functools

import numpy as np
import jax
import jax.numpy as jnp
from jax import lax
from jax.experimental import pallas as pl
from jax.experimental.pallas import tpu as pltpu

F32 = jnp.float32
BF16 = jnp.bfloat16

D_MODEL = 1024
ATT_GROUPS = ((128, 1), (512, 4), (2048, 16))
ATT_HEADS_PER_GROUP = 4
ATT_HEAD_DIM = 64
ATT_WIDTH = 768
ATT_GROUP_WIDTH = ATT_HEADS_PER_GROUP * ATT_HEAD_DIM
ATT_PARTS_WIDTH = 3 * ATT_GROUP_WIDTH
ROPE_THETA = 10000.0
SG_CHUNK = 128
SG_GROUPS = 6
SG_WIDTH = 768
ML_HEADS = 4
ML_HEAD_DIM = 192
ML_HEAD_PAD = 256
ML_WIDTH = 768
ML_PAD_WIDTH = ML_HEADS * ML_HEAD_PAD
ML_CHUNK = 128
ML_CONV = 4
N_BRANCH = 3
D_FF = 2816
N_EXPERTS = 8
TOP_K = 2
LN_EPS = 1e-5
BLOCK = 128
ATT_MIN_ROWS_PER_STEP = 1024
LANES = 128
VMEM_LIMIT = 56 * 1024 * 1024
EXPERT_TILE = 512
DMA_ISSUE_UNROLL = 8
MIXER_SUB_ROWS = 256


def _cparams(*sem):
    return pltpu.CompilerParams(dimension_semantics=sem, vmem_limit_bytes=VMEM_LIMIT)


def _layer_norm(x, g, b):
    mu = jnp.mean(x, axis=-1, keepdims=True)
    xc = x - mu
    var = jnp.mean(xc * xc, axis=-1, keepdims=True)
    return xc * lax.rsqrt(var + LN_EPS) * g + b


def _gelu_tanh(x):
    return 0.5 * x * (1.0 + jnp.tanh(np.sqrt(2.0 / np.pi) * (x + 0.044715 * (x * x * x))))


def _sigmoid(x):
    return 0.5 * (1.0 + jnp.tanh(0.5 * x))


def _dot(a, b):
    return jnp.dot(a, b, preferred_element_type=F32)


def _dot_nt(a, b):
    return lax.dot_general(a, b, (((1,), (1,)), ((), ())), preferred_element_type=F32)


def _dot_tn(a, b):
    return lax.dot_general(a, b, (((0,), (0,)), ((), ())), preferred_element_type=F32)


def _split3(x):
    hi = x.astype(BF16)
    r1 = x - hi.astype(F32)
    mid = r1.astype(BF16)
    lo = (r1 - mid.astype(F32)).astype(BF16)
    return hi, mid, lo


def _rider_specs(arrays, steps):
    specs, shapes = [], []
    for a in arrays:
        span = 1
        while (a.shape[0] * span) % steps or (a.shape[0] * span // steps) % 16:
            span *= 2
        rows = a.shape[0] * span // steps
        specs.append(pl.BlockSpec((rows, a.shape[1]), lambda i, span=span: (i // span, 0)))
        shapes.append(jax.ShapeDtypeStruct(a.shape, BF16))
    return specs, shapes


def _cast_riders(in_refs, out_refs):
    for i_ref, o_ref in zip(in_refs, out_refs):
        o_ref[...] = i_ref[...].astype(o_ref.dtype)


def _projections_kernel(x_ref, *refs, acts, riders):
    n = len(acts)
    ws, bs, rin = refs[:n], refs[n:2 * n], refs[2 * n:2 * n + riders]
    outs, rout = refs[2 * n + riders:3 * n + riders], refs[3 * n + riders:]
    x = x_ref[...].astype(BF16)
    for w_ref, b_ref, o_ref, act in zip(ws, bs, outs, acts):
        res = _dot(x, w_ref[...]) + b_ref[...]
        o_ref[...] = (res if act is None else act(res)).astype(o_ref.dtype)
    _cast_riders(rin, rout)


def _projections(x, heads, riders=(), tm=512):
    m, k = x.shape

    def resident(a):
        return pl.BlockSpec(a.shape, lambda i: (0, 0), pipeline_mode=pl.Buffered(1))

    ws = [h[0] for h in heads]
    bs = [h[1] for h in heads]
    rider_specs, rider_shapes = _rider_specs(riders, m // tm)
    return pl.pallas_call(
        functools.partial(_projections_kernel, acts=tuple(h[3] for h in heads), riders=len(riders)),
        grid=(m // tm,),
        in_specs=([pl.BlockSpec((tm, k), lambda i: (i, 0))] + [resident(w) for w in ws] + [resident(b) for b in bs]
                  + rider_specs),
        out_specs=[pl.BlockSpec((tm, w.shape[1]), lambda i: (i, 0)) for w in ws] + rider_specs,
        out_shape=[jax.ShapeDtypeStruct((m, h[0].shape[1]), h[2]) for h in heads] + rider_shapes,
        compiler_params=_cparams("arbitrary"),
        name="projections",
    )(x, *ws, *bs, *riders)


MAX_ROW_STRIDE = 4


def _linear_dilated_kernel(x_ref, w_ref, b_ref, o_ref, acc, acc2, *, dil):
    res = _dot(x_ref[...], w_ref[...]) + b_ref[...]
    slabs, rows, _ = acc.shape
    for c in range(slabs):
        acc[c] = res[:, c * LANES:(c + 1) * LANES]
    sub = rows // dil
    if dil <= MAX_ROW_STRIDE:
        for r in range(dil):
            for c in range(slabs):
                o_ref[r, :, c * LANES:(c + 1) * LANES] = acc[c, pl.ds(r, sub, stride=dil), :].astype(o_ref.dtype)
        return
    s1 = MAX_ROW_STRIDE
    s2 = dil // s1
    part = rows // s1
    for r1 in range(s1):
        for c in range(slabs):
            acc2[c, r1 * part:(r1 + 1) * part, :] = acc[c, pl.ds(r1, part, stride=s1), :]
    for r1 in range(s1):
        for r2 in range(s2):
            for c in range(slabs):
                o_ref[r1 + s1 * r2, :, c * LANES:(c + 1) * LANES] = (
                    acc2[c, pl.ds(r1 * part + r2, sub, stride=s2), :].astype(o_ref.dtype))


def _linear_dilated(x, w, b, dil, batch, seq, tm=2048):
    k = x.shape[1]
    n = w.shape[1]
    tm = min(tm, seq)
    tiles_per_seq = seq // tm

    def resident(a):
        return pl.BlockSpec(a.shape, lambda i: (0, 0), pipeline_mode=pl.Buffered(1))

    return pl.pallas_call(
        functools.partial(_linear_dilated_kernel, dil=dil),
        grid=(batch * tiles_per_seq,),
        in_specs=[pl.BlockSpec((tm, k), lambda i: (i, 0)), resident(w), resident(b)],
        out_specs=pl.BlockSpec((None, dil, tm // dil, n),
                               lambda i: (i // tiles_per_seq, 0, i % tiles_per_seq, 0)),
        out_shape=jax.ShapeDtypeStruct((batch, dil, seq // dil, n), BF16),
        scratch_shapes=[pltpu.VMEM((n // LANES, tm, LANES), F32),
                        pltpu.VMEM((n // LANES, tm if dil > MAX_ROW_STRIDE else 8, LANES), F32)],
        compiler_params=_cparams("parallel"),
        name=f"linear_dil{dil}",
    )(x, w, b)


def _rope_table_kernel(pos_ref, freq_ref, sign_ref, cos_ref, sin_ref):
    ang = pos_ref[...] * freq_ref[...]
    cos_ref[...] = jnp.cos(ang)
    sin_ref[...] = jnp.sin(ang) * sign_ref[...]


def _rope_tables(positions):
    n = positions.size
    half = ATT_HEAD_DIM // 2
    freqs = ROPE_THETA ** (-jnp.arange(half, dtype=F32) * (2.0 / ATT_HEAD_DIM))
    lane = np.arange(LANES)
    freq_row = freqs[lane % half][None, :]
    sign_row = jnp.asarray(np.where(lane % ATT_HEAD_DIM < half, -1.0, 1.0), F32)[None, :]
    pos = positions.astype(F32).reshape(n, 1)
    tm = 2048
    return pl.pallas_call(
        _rope_table_kernel,
        grid=(n // tm,),
        in_specs=[pl.BlockSpec((tm, 1), lambda i: (i, 0)),
                  pl.BlockSpec((1, LANES), lambda i: (0, 0)),
                  pl.BlockSpec((1, LANES), lambda i: (0, 0))],
        out_specs=[pl.BlockSpec((tm, LANES), lambda i: (i, 0))] * 2,
        out_shape=[jax.ShapeDtypeStruct((n, LANES), F32)] * 2,
        compiler_params=_cparams("parallel"),
        name="rope_tables",
    )(pos, freq_row, sign_row)


def _attn_kernel(q_ref, k_ref, v_ref, cos_ref, sin_ref, o_ref, lse_ref, qs, ks, vs, *, sub, dil):
    per_step = q_ref.shape[0]
    for rr in range(per_step):
        _attn_residue(q_ref.at[rr], k_ref.at[rr], v_ref.at[rr], cos_ref, sin_ref, o_ref.at[rr], lse_ref.at[rr],
                      qs.at[rr], ks.at[rr], vs.at[rr], pl.program_id(1) * per_step + rr, sub=sub, dil=dil)


def _attn_residue(q_ref, k_ref, v_ref, cos_ref, sin_ref, o_ref, lse_ref, qs, ks, vs, res, *, sub, dil):
    nb = sub // BLOCK
    lane = lax.broadcasted_iota(jnp.int32, (BLOCK, LANES), 1)
    first_half = (lane % ATT_HEAD_DIM) < (ATT_HEAD_DIM // 2)
    low_head = lane < ATT_HEAD_DIM

    def rope(x, c, s):
        partner = jnp.where(first_half, pltpu.roll(x, LANES - 32, 1), pltpu.roll(x, 32, 1))
        return x * c + partner * s

    ks[0:BLOCK, :] = jnp.zeros((BLOCK, ATT_GROUP_WIDTH), BF16)
    vs[0:BLOCK, :] = jnp.zeros((BLOCK, ATT_GROUP_WIDTH), BF16)

    def prep(n, carry):
        r0 = pl.multiple_of(n * BLOCK, BLOCK)
        if dil == 1:
            c = cos_ref[pl.ds(r0, BLOCK), :]
            s = sin_ref[pl.ds(r0, BLOCK), :]
        else:
            c = cos_ref[pl.ds(r0 * dil + res, BLOCK, stride=dil), :]
            s = sin_ref[pl.ds(r0 * dil + res, BLOCK, stride=dil), :]
        for pair in range(2):
            cols = slice(pair * LANES, (pair + 1) * LANES)
            q = rope(q_ref[pl.ds(r0, BLOCK), cols].astype(F32), c, s) * (ATT_HEAD_DIM ** -0.5)
            qs[pl.ds(r0, BLOCK), (2 * pair) * LANES:(2 * pair + 1) * LANES] = jnp.where(low_head, q, 0.0).astype(BF16)
            qs[pl.ds(r0, BLOCK), (2 * pair + 1) * LANES:(2 * pair + 2) * LANES] = jnp.where(low_head, 0.0, q).astype(BF16)
            k = rope(k_ref[pl.ds(r0, BLOCK), cols].astype(F32), c, s)
            ks[pl.ds(r0 + BLOCK, BLOCK), cols] = k.astype(BF16)
        vs[pl.ds(r0 + BLOCK, BLOCK), :] = v_ref[pl.ds(r0, BLOCK), :]
        return carry

    if nb <= 2:
        for n in range(nb):
            prep(n, 0)
    else:
        lax.fori_loop(0, nb, prep, 0)

    qi = lax.broadcasted_iota(jnp.int32, (BLOCK, 2 * BLOCK), 0)
    ki = lax.broadcasted_iota(jnp.int32, (BLOCK, 2 * BLOCK), 1)
    dist = qi + BLOCK - ki
    band = (dist >= 0) & (dist <= BLOCK)
    neg_inf = jnp.float32(-jnp.inf)
    bias_rest = jnp.where(band, 0.0, neg_inf)
    bias_first = jnp.where(band & (ki >= BLOCK), 0.0, neg_inf)

    def block(n, bias):
        r0 = pl.multiple_of(n * BLOCK, BLOCK)
        heads = range(ATT_HEADS_PER_GROUP)
        pair_cols = [slice(pair * LANES, (pair + 1) * LANES) for pair in range(2)]
        kw = [ks[pl.ds(r0, 2 * BLOCK), c] for c in pair_cols]
        vw = [vs[pl.ds(r0, 2 * BLOCK), c] for c in pair_cols]
        s = [_dot_nt(qs[pl.ds(r0, BLOCK), hd * LANES:(hd + 1) * LANES], kw[hd // 2]) + bias for hd in heads]
        m = [jnp.max(s[hd], axis=-1, keepdims=True) for hd in heads]
        p = [jnp.exp(s[hd] - m[hd]) for hd in heads]
        den = [jnp.sum(p[hd], axis=-1, keepdims=True) for hd in heads]
        o = [_dot(p[hd].astype(BF16), vw[hd // 2]) * (1.0 / den[hd]) for hd in heads]
        lse = [m[hd] + jnp.log(den[hd]) for hd in heads]
        for pair in range(2):
            lo, hi = 2 * pair, 2 * pair + 1
            o_ref[pl.ds(r0, BLOCK), pair_cols[pair]] = jnp.where(low_head, o[lo], o[hi]).astype(o_ref.dtype)
            lse_ref[pl.ds(r0, BLOCK), pair_cols[pair]] = jnp.where(low_head, lse[lo], lse[hi])

    block(0, bias_first)
    if nb <= 2:
        for n in range(1, nb):
            block(n, bias_rest)
        return

    def body(n, carry):
        block(n, bias_rest)
        return carry

    lax.fori_loop(1, nb, body, 0, unroll=4)


def _attention_group(qkv, cos, sin, group, batch, seq):
    window, dil = ATT_GROUPS[group]
    assert window // dil == BLOCK
    sub = seq // dil
    assert sub % BLOCK == 0

    per_step = min(dil, max(1, ATT_MIN_ROWS_PER_STEP // sub))

    def part(p):
        return pl.BlockSpec((None, per_step, sub, ATT_GROUP_WIDTH), lambda b, r: (b, r, 0, p))

    tab = pl.BlockSpec((seq, LANES), lambda b, r: (b, 0))
    out = pl.BlockSpec((None, per_step, sub, ATT_GROUP_WIDTH), lambda b, r: (b, r, 0, 0))
    return pl.pallas_call(
        functools.partial(_attn_kernel, sub=sub, dil=dil),
        grid=(batch, dil // per_step),
        in_specs=[part(0), part(1), part(2), tab, tab],
        out_specs=[out, out],
        out_shape=[jax.ShapeDtypeStruct((batch, dil, sub, ATT_GROUP_WIDTH), BF16),
                   jax.ShapeDtypeStruct((batch, dil, sub, ATT_GROUP_WIDTH), F32)],
        scratch_shapes=[pltpu.VMEM((per_step, sub, ATT_HEADS_PER_GROUP * LANES), BF16),
                        pltpu.VMEM((per_step, sub + BLOCK, ATT_GROUP_WIDTH), BF16),
                        pltpu.VMEM((per_step, sub + BLOCK, ATT_GROUP_WIDTH), BF16)],
        compiler_params=_cparams("parallel", "parallel"),
        name=f"attention_g{group}",
    )(qkv, qkv, qkv, cos, sin)


ML_SEQS_PER_STEP = 2


def _mlstm_kernel(qk_ref, v_ref, og_ref, g_ref, cw_ref, cb_ref, sc_ref, out_ref, frame, ct, m_s):
    taps = ML_CONV

    @pl.when(pl.program_id(1) == 0)
    def _():
        frame[:, 0:ML_CHUNK, :] = jnp.zeros((frame.shape[0], ML_CHUNK, 2 * ML_PAD_WIDTH), BF16)
        ct[...] = jnp.zeros(ct.shape, F32)
        m_s[...] = jnp.zeros(m_s.shape, F32)

    out_row = lax.broadcasted_iota(jnp.int32, (ML_CHUNK, 2 * ML_CHUNK), 0)
    src_row = lax.broadcasted_iota(jnp.int32, (ML_CHUNK, 2 * ML_CHUNK), 1)
    picks = [jnp.where(src_row == out_row + (ML_CHUNK - (taps - 1 - j)), 1.0, 0.0).astype(BF16)
             for j in range(taps - 1)]
    row = lax.broadcasted_iota(jnp.int32, (ML_CHUNK, ML_CHUNK), 0)
    colm = lax.broadcasted_iota(jnp.int32, (ML_CHUNK, ML_CHUNK), 1)
    causal = row >= colm
    tri = jnp.where(causal, 1.0, 0.0).astype(BF16)
    neg_inf = jnp.float32(-jnp.inf)

    seqs = range(qk_ref.shape[0])
    pairs = [(s, h) for s in seqs for h in range(ML_HEADS)]

    def head_cols(h):
        return slice(h * ML_HEAD_PAD, (h + 1) * ML_HEAD_PAD)

    act, gates, bcum, bcum_t, gates_t = {}, {}, {}, {}, {}
    for s in seqs:
        cur = qk_ref[s]
        frame[s, ML_CHUNK:2 * ML_CHUNK, :] = cur
        both = frame[s]
        acc = cb_ref[...] + cur.astype(F32) * cw_ref[taps - 1:taps, :]
        for j in range(taps - 1):
            acc = acc + _dot(picks[j], both) * cw_ref[j:j + 1, :]
        frame[s, 0:ML_CHUNK, :] = cur
        act[s] = (acc * _sigmoid(acc) * sc_ref[...]).astype(BF16)
        gates[s] = g_ref[s]
        log_f = -(jnp.maximum(-gates[s], 0.0) + jnp.log1p(jnp.exp(-jnp.abs(gates[s]))))
        hi, mid, lo = _split3(log_f)
        bcum[s] = _dot(tri, hi) + _dot(tri, mid) + _dot(tri, lo)
        bcum_t[s] = bcum[s].T
        gates_t[s] = gates[s].T

    qh, kh, vh, c_old, m_prev, b_c, i_c, log_d, m_inter, m = ({} for _ in range(10))
    for p in pairs:
        s, h = p
        qh[p] = act[s][:, head_cols(h)]
        kh[p] = act[s][:, ML_PAD_WIDTH + h * ML_HEAD_PAD:ML_PAD_WIDTH + (h + 1) * ML_HEAD_PAD]
        vh[p] = v_ref[s, :, head_cols(h)]
        c_old[p] = ct[s, h]
        m_prev[p] = m_s[s, h, 0:1, 0:1]
        b_c[p] = bcum[s][:, ML_HEADS + h:ML_HEADS + h + 1]
        b_r = bcum_t[s][ML_HEADS + h:ML_HEADS + h + 1, :]
        i_c[p] = gates[s][:, h:h + 1]
        i_r = gates_t[s][h:h + 1, :]
        log_d[p] = jnp.where(causal, b_c[p] - b_r + i_r, neg_inf)
        m_inter[p] = b_c[p] + m_prev[p]
        m[p] = jnp.maximum(m_inter[p], jnp.max(log_d[p], axis=-1, keepdims=True))

    scores = {p: _dot_nt(qh[p], kh[p]) for p in pairs}
    carried = {p: _dot(qh[p], c_old[p].astype(BF16)) for p in pairs}
    sc = {p: (scores[p] * jnp.exp(log_d[p] - m[p])).astype(BF16) for p in pairs}
    num = {p: _dot(sc[p], vh[p]) + jnp.exp(m_inter[p] - m[p]) * carried[p] for p in pairs}
    for p in pairs:
        s, h = p
        den = num[p][:, ML_HEAD_DIM:ML_HEAD_DIM + 1]
        hid = num[p] / jnp.maximum(jnp.abs(den), jnp.exp(-m[p]))
        out_ref[s, :, head_cols(h)] = (og_ref[s, :, head_cols(h)].astype(F32) * hid).astype(out_ref.dtype)
    for p in pairs:
        s, h = p
        b_last = b_c[p][ML_CHUNK - 1:ML_CHUNK, :]
        m_new = m[p][ML_CHUNK - 1:ML_CHUNK, :]
        w = jnp.exp(b_last - b_c[p] + i_c[p] - m_new)
        decay = jnp.exp(b_last + m_prev[p] - m_new)
        wv = (w * vh[p].astype(F32)).astype(BF16)
        ct[s, h] = decay * c_old[p] + _dot_tn(kh[p], wv)
        m_s[s, h] = jnp.broadcast_to(m_new, m_s.shape[2:])


def _mlstm(ml, og, gates, conv_w, conv_b, qk_scale, batch, seq):
    nc = seq // ML_CHUNK
    group = ML_SEQS_PER_STEP if batch % ML_SEQS_PER_STEP == 0 else 1

    def whole(a):
        return pl.BlockSpec(a.shape, lambda b, c: (0, 0))

    def chunk(width, col):
        return pl.BlockSpec((group, ML_CHUNK, width), lambda b, c: (b, c, col))

    out = pl.pallas_call(
        _mlstm_kernel,
        grid=(batch // group, nc),
        in_specs=[chunk(2 * ML_PAD_WIDTH, 0), chunk(ML_PAD_WIDTH, 2), chunk(ML_PAD_WIDTH, 0), chunk(LANES, 0),
                  whole(conv_w), whole(conv_b), whole(qk_scale)],
        out_specs=chunk(ML_PAD_WIDTH, 0),
        out_shape=jax.ShapeDtypeStruct((batch, seq, ML_PAD_WIDTH), BF16),
        scratch_shapes=[pltpu.VMEM((group, 2 * ML_CHUNK, 2 * ML_PAD_WIDTH), BF16),
                        pltpu.VMEM((group, ML_HEADS, ML_HEAD_PAD, ML_HEAD_PAD), F32),
                        pltpu.VMEM((group, ML_HEADS, 8, LANES), F32)],
        compiler_params=_cparams("parallel", "arbitrary"),
        name="mlstm",
    )(ml.reshape(batch, seq, -1), ml.reshape(batch, seq, -1), og.reshape(batch, seq, -1),
      gates.reshape(batch, seq, -1), conv_w, conv_b, qk_scale)
    return out.reshape(batch * seq, ML_PAD_WIDTH)


def _mixer_out_kernel(x_ref, o0_ref, l0_ref, o1_ref, l1_ref, o2_ref, l2_ref, sg_ref, hc_ref, xin_ref,
                      sgg_ref, sgb_ref, sgw_ref, sgbias_ref, wa_ref, wb_ref, wc_ref, wo_ref, lng_ref, lnb_ref,
                      wg_ref, bg_ref, xf_ref, xb_ref, unperm, *, alpha):
    tm = x_ref.shape[0]
    slabs = ATT_GROUP_WIDTH // LANES
    att_refs = (l0_ref, l1_ref, l2_ref, o0_ref, o1_ref, o2_ref)

    for slot, ref in enumerate(att_refs):
        dil = ref.shape[0]
        if dil > 1:
            for r in range(dil):
                for c in range(slabs):
                    unperm[slabs * slot + c, pl.ds(r, tm // dil, stride=dil), :] = (
                        ref[r, :, c * LANES:(c + 1) * LANES].astype(F32))

    def token_order(slot, rows):
        ref = att_refs[slot]
        if ref.shape[0] == 1:
            return ref[0, rows, :].astype(F32)
        return jnp.concatenate([unperm[slabs * slot + c, rows, :] for c in range(slabs)], axis=1)

    row = lax.broadcasted_iota(jnp.int32, (SG_CHUNK, SG_CHUNK), 0)
    colm = lax.broadcasted_iota(jnp.int32, (SG_CHUNK, SG_CHUNK), 1)
    causal = row >= colm
    group_dim = SG_WIDTH // SG_GROUPS
    sg_w = [jnp.where(causal, sgw_ref[g], 0.0).astype(BF16) for g in range(SG_GROUPS)]

    subs = [slice(h * MIXER_SUB_ROWS, (h + 1) * MIXER_SUB_ROWS) for h in range(tm // MIXER_SUB_ROWS)]

    def gate(rows, i):
        cols = slice(i * D_MODEL, (i + 1) * D_MODEL)
        return _sigmoid(_dot(xin_ref[rows, :], wg_ref[:, cols]) + bg_ref[:, cols])

    proj_c = [_dot(hc_ref[rows, :], wc_ref[...]) for rows in subs]

    y_a = []
    for rows in subs:
        l0, l1, l2 = token_order(0, rows), token_order(1, rows), token_order(2, rows)
        lmax = jnp.maximum(jnp.maximum(l0, l1), l2)
        e0, e1, e2 = jnp.exp(l0 - lmax), jnp.exp(l1 - lmax), jnp.exp(l2 - lmax)
        inv = 1.0 / (e0 + e1 + e2)
        y_a.append(((e0 * inv) * token_order(3, rows) + (e1 * inv) * token_order(4, rows)
                    + (e2 * inv) * token_order(5, rows)).astype(BF16))
    proj_a = [_dot(y, wa_ref[...]) for y in y_a]

    v = [_layer_norm(sg_ref[rows, SG_WIDTH:2 * SG_WIDTH].astype(F32), sgg_ref[...], sgb_ref[...]).astype(BF16)
         for rows in subs]
    y_b = []
    for rows, vs in zip(subs, v):
        chunks = []
        for c in range(MIXER_SUB_ROWS // SG_CHUNK):
            crows = slice(c * SG_CHUNK, (c + 1) * SG_CHUNK)
            parts = [_dot(sg_w[g], vs[crows, g * group_dim:(g + 1) * group_dim]) + sgbias_ref[:, g:g + 1]
                     for g in range(SG_GROUPS)]
            chunks.append(jnp.concatenate(parts, axis=1))
        y_b.append((sg_ref[rows, 0:SG_WIDTH].astype(F32) * jnp.concatenate(chunks, axis=0)).astype(BF16))
    proj_b = [_dot(y, wb_ref[...]) for y in y_b]

    z = [(gate(rows, 0) * pa + gate(rows, 1) * pb + gate(rows, 2) * pc).astype(BF16)
         for rows, pa, pb, pc in zip(subs, proj_a, proj_b, proj_c)]
    mix = [_dot(zs, wo_ref[...]) for zs in z]
    for rows, m in zip(subs, mix):
        out = _layer_norm(alpha * x_ref[rows, :] + m, lng_ref[...], lnb_ref[...])
        xf_ref[rows, :] = out
        xb_ref[rows, :] = out.astype(BF16)


def _mixer_out(x, att_outs, sg, hc, x_in, p, alpha, seq, tm=512):
    n = x.shape[0]
    tiles_per_seq = seq // tm

    def rows(width):
        return pl.BlockSpec((tm, width), lambda i: (i, 0))

    def whole(a):
        return pl.BlockSpec(a.shape, lambda i: (0,) * a.ndim)

    def residue_major(dil):
        return pl.BlockSpec((None, dil, tm // dil, ATT_GROUP_WIDTH),
                            lambda i: (i // tiles_per_seq, 0, i % tiles_per_seq, 0))

    att_specs, att_args = [], []
    for (o, lse), (_, dil) in zip(att_outs, ATT_GROUPS):
        att_specs += [residue_major(dil)] * 2
        att_args += [o, lse]
    weights = [p["sg_ln_g"], p["sg_ln_b"], p["sg_w"], p["sg_bias_t"], p["w_br_a"], p["w_br_b"], p["w_br_c"],
               p["w_out"], p["ln_g0"], p["ln_b0"], p["w_g"], p["b_g"]]
    return pl.pallas_call(
        functools.partial(_mixer_out_kernel, alpha=alpha),
        grid=(n // tm,),
        in_specs=[rows(D_MODEL)] + att_specs
                 + [rows(2 * SG_WIDTH), rows(ML_PAD_WIDTH), rows(D_MODEL)]
                 + [whole(w) for w in weights],
        out_specs=[rows(D_MODEL), rows(D_MODEL)],
        out_shape=[jax.ShapeDtypeStruct((n, D_MODEL), F32), jax.ShapeDtypeStruct((n, D_MODEL), BF16)],
        scratch_shapes=[pltpu.VMEM((2 * len(ATT_GROUPS) * ATT_GROUP_WIDTH // LANES, tm, LANES), F32)],
        compiler_params=_cparams("parallel"),
        name="mixer_out",
    )(x, *att_args, sg, hc, x_in, *weights)


def _swiglu_tile(x, w1_ref, w3_ref, w2_ref, prod, tf):
    for c in range(D_FF // tf):
        cols = slice(c * tf, (c + 1) * tf)
        h1 = _dot(x, w1_ref[:, cols])
        h3 = _dot(x, w3_ref[:, cols])
        prod[:, cols] = (h1 * _sigmoid(h1) * h3).astype(BF16)
    return _dot(prod[...], w2_ref[...])


def _ffn_kernel(xf_ref, xb_ref, w1_ref, w3_ref, w2_ref, lng_ref, lnb_ref, *refs, alpha, tf, riders):
    rin, (of_ref, ob_ref), rout, prod = refs[:riders], refs[riders:riders + 2], refs[riders + 2:-1], refs[-1]
    f = _swiglu_tile(xb_ref[...], w1_ref, w3_ref, w2_ref, prod, tf)
    out = _layer_norm(alpha * xf_ref[...] + f, lng_ref[...], lnb_ref[...])
    of_ref[...] = out
    ob_ref[...] = out.astype(BF16)
    _cast_riders(rin, rout)


def _ffn(xf, xb, w1, w3, w2, ln_g, ln_b, alpha, riders=(), tm=512, tf=256):
    n = xf.shape[0]

    def resident(shape):
        return pl.BlockSpec(shape, lambda i: (0, 0), pipeline_mode=pl.Buffered(1))

    rider_specs, rider_shapes = _rider_specs(riders, n // tm)
    return pl.pallas_call(
        functools.partial(_ffn_kernel, alpha=alpha, tf=tf, riders=len(riders)),
        grid=(n // tm,),
        in_specs=[pl.BlockSpec((tm, D_MODEL), lambda i: (i, 0)),
                  pl.BlockSpec((tm, D_MODEL), lambda i: (i, 0)),
                  resident((D_MODEL, D_FF)), resident((D_MODEL, D_FF)), resident((D_FF, D_MODEL)),
                  pl.BlockSpec((1, D_MODEL), lambda i: (0, 0)),
                  pl.BlockSpec((1, D_MODEL), lambda i: (0, 0))] + rider_specs,
        out_specs=[pl.BlockSpec((tm, D_MODEL), lambda i: (i, 0))] * 2 + rider_specs,
        out_shape=[jax.ShapeDtypeStruct((n, D_MODEL), F32), jax.ShapeDtypeStruct((n, D_MODEL), BF16)]
                  + rider_shapes,
        scratch_shapes=[pltpu.VMEM((tm, D_FF), BF16)],
        compiler_params=_cparams("arbitrary"),
        name="ffn",
    )(xf, xb, w1, w3, w2, ln_g, ln_b, *riders)


def _router_kernel(x_ref, w_ref, b_ref, info_ref):
    xh, xm, _ = _split3(x_ref[...])
    wh, wm, _ = _split3(w_ref[...])
    logits = (_dot(xh, wh) + (_dot(xh, wm) + _dot(xm, wh))) + b_ref[...]
    lane = lax.broadcasted_iota(jnp.int32, logits.shape, 1)
    neg_inf = jnp.float32(-jnp.inf)
    logits = jnp.where(lane < N_EXPERTS, logits, neg_inf)
    v1 = jnp.max(logits, axis=-1, keepdims=True)
    i1 = jnp.min(jnp.where(logits == v1, lane, LANES), axis=-1, keepdims=True)
    rest = jnp.where(lane == i1, neg_inf, logits)
    v2 = jnp.max(rest, axis=-1, keepdims=True)
    i2 = jnp.min(jnp.where(rest == v2, lane, LANES), axis=-1, keepdims=True)
    e = jnp.exp(v2 - v1)
    inv = 1.0 / (1.0 + e)
    info_ref[...] = jnp.where(lane == 0, i1.astype(F32),
                              jnp.where(lane == 1, i2.astype(F32),
                                        jnp.where(lane == 2, inv, jnp.where(lane == 3, e * inv, 0.0))))


def _router(xf, router_w, router_b, tm=1024):
    n = xf.shape[0]
    w = jnp.pad(router_w, ((0, 0), (0, LANES - N_EXPERTS)))
    b = jnp.pad(router_b, (0, LANES - N_EXPERTS))[None, :]
    return pl.pallas_call(
        _router_kernel,
        grid=(n // tm,),
        in_specs=[pl.BlockSpec((tm, D_MODEL), lambda i: (i, 0)),
                  pl.BlockSpec((D_MODEL, LANES), lambda i: (0, 0)),
                  pl.BlockSpec((1, LANES), lambda i: (0, 0))],
        out_specs=pl.BlockSpec((tm, LANES), lambda i: (i, 0)),
        out_shape=jax.ShapeDtypeStruct((n, LANES), F32),
        compiler_params=_cparams("parallel"),
        name="router",
    )(xf, w, b)


def _routing_plan(info, n):
    tile = EXPERT_TILE
    n_tiles = TOP_K * n // tile + N_EXPERTS
    expert = info[:, 0:TOP_K].astype(jnp.int32).reshape(-1)
    onehot = (expert[:, None] == jnp.arange(N_EXPERTS, dtype=jnp.int32)[None, :]).astype(jnp.int32)
    running = jnp.cumsum(onehot, axis=0)
    rank = jnp.sum(onehot * running, axis=1) - 1
    counts = running[-1]
    padded = ((counts + tile - 1) // tile) * tile
    ends = jnp.cumsum(padded)
    starts = ends - padded
    dest = jnp.sum(onehot * starts[None, :], axis=1) + rank
    used = ends[-1] // tile
    tile_start = jnp.arange(n_tiles, dtype=jnp.int32) * tile
    tile_expert = jnp.sum((tile_start[:, None] >= ends[None, :]).astype(jnp.int32), axis=1)
    last_expert = jnp.sum((tile_start[used - 1] >= ends).astype(jnp.int32))
    tile_expert = jnp.minimum(tile_expert, last_expert)
    plan = dict(dest=dest, tile_expert=tile_expert, used=used.reshape(1),
                pad_start=starts + counts, pad_count=padded - counts)
    return {k: v.astype(jnp.int32) for k, v in plan.items()}, n_tiles


def _rows_wait(src, dst, rows, sem):
    pltpu.make_async_copy(src.at[pl.ds(0, rows)], dst.at[pl.ds(0, rows)], sem).wait()


ROW_SUBLANES = D_MODEL // LANES


def _tile_copy(src, src_row, dst, dst_row, sem):
    s0 = pl.multiple_of(src_row * ROW_SUBLANES, ROW_SUBLANES)
    d0 = pl.multiple_of(dst_row * ROW_SUBLANES, ROW_SUBLANES)
    return pltpu.make_async_copy(src.at[pl.ds(s0, ROW_SUBLANES)], dst.at[pl.ds(d0, ROW_SUBLANES)], sem)


def _to_row_tiles(ref, x):
    m = x.shape[0]
    for c in range(ROW_SUBLANES):
        ref[pl.ds(c, m, stride=ROW_SUBLANES), :] = x[:, c * LANES:(c + 1) * LANES]


def _from_row_tiles(ref):
    m = ref.shape[0] // ROW_SUBLANES
    return jnp.concatenate([ref[pl.ds(c, m, stride=ROW_SUBLANES), :] for c in range(ROW_SUBLANES)], axis=1)


DISPATCH_SLOTS = 3


def _dispatch_kernel(pad_start_ref, pad_count_ref, used_ref, dest_ref, x_ref, xs_ref, buf, tiles, zero_row,
                     load_sem, row_sem, pad_sem, *, tm):
    i = pl.program_id(0)

    @pl.when(i == 0)
    def _():
        zero_row[...] = jnp.zeros(zero_row.shape, F32)
        for e in range(N_EXPERTS):
            def fill(j, carry, e=e):
                _tile_copy(zero_row, 0, xs_ref, pad_start_ref[e] + j, pad_sem).start()
                return carry
            lax.fori_loop(0, pad_count_ref[e], fill, 0)
        for e in range(N_EXPERTS):
            def drain(j, carry):
                _tile_copy(zero_row, 0, xs_ref, 0, pad_sem).wait()
                return carry
            lax.fori_loop(0, pad_count_ref[e], drain, 0)

        tile_rows = tm * ROW_SUBLANES
        spare = xs_ref.shape[0] // tile_rows - used_ref[0]
        tiles[1] = jnp.zeros(tiles.shape[1:], F32)

        def zero_tile(j):
            row0 = pl.multiple_of((used_ref[0] + j) * tile_rows, tile_rows)
            return pltpu.make_async_copy(tiles.at[1], xs_ref.at[pl.ds(row0, tile_rows)], pad_sem)

        def fill_tile(j, carry):
            zero_tile(j).start()
            return carry

        def drain_tile(j, carry):
            zero_tile(j).wait()
            return carry

        lax.fori_loop(0, spare, fill_tile, 0)
        lax.fori_loop(0, spare, drain_tile, 0)

    steps = pl.num_programs(0)
    slot = i % DISPATCH_SLOTS
    prev_slot = (i + DISPATCH_SLOTS - 1) % DISPATCH_SLOTS
    stage = i % 2

    def load(tile, into):
        return pltpu.make_async_copy(x_ref.at[pl.ds(tile * tm, tm), :], buf.at[into], load_sem.at[into])

    def wait_rows(of_stage):
        for _ in range(TOP_K):
            _rows_wait(tiles.at[of_stage], xs_ref, tm * ROW_SUBLANES, row_sem.at[of_stage])

    @pl.when(i == 0)
    def _():
        load(0, 0).start()

        @pl.when(steps > 1)
        def _():
            load(1, 1).start()

    load(i, slot).wait()
    _to_row_tiles(tiles.at[stage], buf[slot])

    def start(t, carry):
        for c in range(TOP_K):
            _tile_copy(tiles.at[stage], t, xs_ref, dest_ref[0, 0, TOP_K * t + c], row_sem.at[stage]).start(
                priority=c % 2)
        return carry

    lax.fori_loop(0, tm, start, 0, unroll=DMA_ISSUE_UNROLL)

    @pl.when(i > 0)
    def _():
        wait_rows(1 - stage)

    @pl.when(i + 2 < steps)
    def _():
        load(i + 2, prev_slot).start()

    @pl.when(i == steps - 1)
    def _():
        wait_rows(stage)


def _dispatch(xf, dest, pad_start, pad_count, used, n_slots, tm=EXPERT_TILE):
    n = xf.shape[0]
    dest3 = dest.reshape(n // tm, 1, TOP_K * tm)
    grid_spec = pltpu.PrefetchScalarGridSpec(
        num_scalar_prefetch=3,
        grid=(n // tm,),
        in_specs=[pl.BlockSpec((1, 1, TOP_K * tm), lambda i, ps, pc, nu: (i, 0, 0), memory_space=pltpu.SMEM),
                  pl.BlockSpec(memory_space=pl.ANY)],
        out_specs=pl.BlockSpec(memory_space=pl.ANY),
        scratch_shapes=[pltpu.VMEM((DISPATCH_SLOTS, tm, D_MODEL), F32),
                        pltpu.VMEM((2, tm * ROW_SUBLANES, LANES), F32),
                        pltpu.VMEM((ROW_SUBLANES, LANES), F32),
                        pltpu.SemaphoreType.DMA((DISPATCH_SLOTS,)),
                        pltpu.SemaphoreType.DMA((2,)),
                        pltpu.SemaphoreType.DMA],
    )
    return pl.pallas_call(
        functools.partial(_dispatch_kernel, tm=tm),
        grid_spec=grid_spec,
        out_shape=jax.ShapeDtypeStruct((n_slots * ROW_SUBLANES, LANES), F32),
        compiler_params=_cparams("arbitrary"),
        name="moe_dispatch",
    )(pad_start, pad_count, used, dest3, xf)


def _experts_kernel(te_ref, used_ref, xs_ref, w1_ref, w3_ref, w2_ref, ys_ref, prod, *, tf):
    del te_ref
    i = pl.program_id(0)

    @pl.when(i < used_ref[0])
    def _():
        x = _from_row_tiles(xs_ref).astype(BF16)
        _to_row_tiles(ys_ref, _swiglu_tile(x, w1_ref, w3_ref, w2_ref, prod, tf))

    @pl.when(i >= used_ref[0])
    def _():
        ys_ref[...] = jnp.zeros(ys_ref.shape, ys_ref.dtype)


def _experts(xs, tile_expert, used, w1, w3, w2, n_tiles, tf=256):
    tile = EXPERT_TILE
    grid_spec = pltpu.PrefetchScalarGridSpec(
        num_scalar_prefetch=2,
        grid=(n_tiles,),
        in_specs=[pl.BlockSpec((tile * ROW_SUBLANES, LANES), lambda i, te, nu: (jnp.minimum(i, nu[0] - 1), 0)),
                  pl.BlockSpec((None, D_MODEL, D_FF), lambda i, te, nu: (te[i], 0, 0)),
                  pl.BlockSpec((None, D_MODEL, D_FF), lambda i, te, nu: (te[i], 0, 0)),
                  pl.BlockSpec((None, D_FF, D_MODEL), lambda i, te, nu: (te[i], 0, 0))],
        out_specs=pl.BlockSpec((tile * ROW_SUBLANES, LANES), lambda i, te, nu: (i, 0)),
        scratch_shapes=[pltpu.VMEM((tile, D_FF), BF16)],
    )
    return pl.pallas_call(
        functools.partial(_experts_kernel, tf=tf),
        grid_spec=grid_spec,
        out_shape=jax.ShapeDtypeStruct((n_tiles * tile * ROW_SUBLANES, LANES), F32),
        compiler_params=_cparams("arbitrary"),
        name="moe_experts",
    )(tile_expert, used, xs, w1, w3, w2)


def _combine_kernel(dest_ref, next_dest_ref, xf_ref, info_ref, ys_ref, lng_ref, lnb_ref, of_ref, got, sem, *, alpha):
    tm = xf_ref.shape[0]
    i = pl.program_id(0)
    slot = i % 2

    def gather(dests, into):
        def start(t, carry):
            for c in range(TOP_K):
                _tile_copy(ys_ref, dests[0, 0, TOP_K * t + c], got.at[into, c], t, sem.at[into]).start(
                    priority=c % 2)
            return carry
        lax.fori_loop(0, tm, start, 0, unroll=DMA_ISSUE_UNROLL)

    @pl.when(i == 0)
    def _():
        gather(dest_ref, 0)

    @pl.when(i + 1 < pl.num_programs(0))
    def _():
        gather(next_dest_ref, 1 - slot)

    for c in range(TOP_K):
        _rows_wait(ys_ref, got.at[slot, c], tm * ROW_SUBLANES, sem.at[slot])
    info = info_ref[...]
    mix = (info[:, 2:3] * _from_row_tiles(got.at[slot, 0]) + info[:, 3:4] * _from_row_tiles(got.at[slot, 1]))
    of_ref[...] = _layer_norm(alpha * xf_ref[...] + mix, lng_ref[...], lnb_ref[...])


def _combine(xf, info, ys, dest, ln_g, ln_b, alpha, tm=256):
    n = xf.shape[0]
    steps = n // tm
    dest3 = dest.reshape(steps, 1, TOP_K * tm)
    return pl.pallas_call(
        functools.partial(_combine_kernel, alpha=alpha),
        grid=(steps,),
        in_specs=[pl.BlockSpec((1, 1, TOP_K * tm), lambda i: (i, 0, 0), memory_space=pltpu.SMEM),
                  pl.BlockSpec((1, 1, TOP_K * tm), lambda i: (jnp.minimum(i + 1, steps - 1), 0, 0),
                               memory_space=pltpu.SMEM),
                  pl.BlockSpec((tm, D_MODEL), lambda i: (i, 0)),
                  pl.BlockSpec((tm, LANES), lambda i: (i, 0)),
                  pl.BlockSpec(memory_space=pl.ANY),
                  pl.BlockSpec((1, D_MODEL), lambda i: (0, 0)),
                  pl.BlockSpec((1, D_MODEL), lambda i: (0, 0))],
        out_specs=pl.BlockSpec((tm, D_MODEL), lambda i: (i, 0)),
        out_shape=jax.ShapeDtypeStruct((n, D_MODEL), F32),
        scratch_shapes=[pltpu.VMEM((2, TOP_K, tm * ROW_SUBLANES, LANES), F32), pltpu.SemaphoreType.DMA((2,))],
        compiler_params=_cparams("arbitrary"),
        name="moe_combine",
    )(dest3, dest3, xf, info, ys, ln_g, ln_b)


def _moe(xf, router_w, router_b, w1, w3, w2, ln_g, ln_b, alpha):
    n = xf.shape[0]
    info = _router(xf, router_w, router_b)
    plan, n_tiles = _routing_plan(info, n)
    xs = _dispatch(xf, plan["dest"], plan["pad_start"], plan["pad_count"], plan["used"], n_tiles * EXPERT_TILE)
    ys = _experts(xs, plan["tile_expert"], plan["used"], w1, w3, w2, n_tiles)
    return _combine(xf, info, ys, plan["dest"], ln_g, ln_b, alpha)


def _pad_heads(w):
    lead = w.shape[:-1]
    w = w.reshape(*lead, ML_HEADS, ML_HEAD_DIM)
    w = jnp.pad(w, [(0, 0)] * len(lead) + [(0, 0), (0, ML_HEAD_PAD - ML_HEAD_DIM)])
    return w.reshape(*lead, ML_PAD_WIDTH)


def _group_qkv(w, group):
    gw = ATT_GROUP_WIDTH
    return jnp.concatenate([w[..., part * ATT_WIDTH + group * gw:part * ATT_WIDTH + (group + 1) * gw]
                            for part in range(3)], axis=-1)


def _layer_params(layer, w_in, b_in, conv_w, conv_b, sg_ln_g, sg_ln_b, sg_w, sg_b,
                  w_br_a, w_br_b, w_br_c, w_out, ln_g, ln_b):
    w, b = w_in[layer].astype(BF16), b_in[layer]
    o_att, o_sg, o_qk, o_v, o_o, o_i, o_g = 0, 2304, 3840, 5376, 6144, 6912, 6920
    ones_col = jnp.zeros((ML_HEADS, ML_HEAD_PAD), F32).at[:, ML_HEAD_DIM].set(1.0).reshape(ML_PAD_WIDTH)
    w_qk = jnp.concatenate([_pad_heads(w[:, o_qk:o_qk + ML_WIDTH]), _pad_heads(w[:, o_qk + ML_WIDTH:o_v])], axis=1)
    b_qk = jnp.concatenate([_pad_heads(b[o_qk:o_qk + ML_WIDTH]), _pad_heads(b[o_qk + ML_WIDTH:o_v])])
    qk_scale = jnp.concatenate([jnp.ones((ML_PAD_WIDTH,), F32),
                                jnp.full((ML_PAD_WIDTH,), ML_HEAD_DIM ** -0.5, F32)])
    cw, cb = conv_w[layer], conv_b[layer]
    w_att, b_att = w[:, o_att:o_sg], b[o_att:o_sg]
    return dict(
        w_att=[_group_qkv(w_att, g) for g in range(len(ATT_GROUPS))],
        b_att=[_group_qkv(b_att, g)[None, :] for g in range(len(ATT_GROUPS))],
        w_sg=w[:, o_sg:o_qk], b_sg=b[None, o_sg:o_qk],
        w_ml=jnp.concatenate([w_qk, _pad_heads(w[:, o_v:o_o])], axis=1),
        b_ml=jnp.concatenate([b_qk, _pad_heads(b[o_v:o_o]) + ones_col])[None, :],
        qk_scale=qk_scale[None, :],
        w_mlo=_pad_heads(w[:, o_o:o_i]), b_mlo=_pad_heads(b[o_o:o_i])[None, :],
        w_if=jnp.pad(w[:, o_i:o_g], ((0, 0), (0, LANES - 2 * ML_HEADS))),
        b_if=jnp.pad(b[o_i:o_g], (0, LANES - 2 * ML_HEADS))[None, :],
        w_g=w[:, o_g:], b_g=b[None, o_g:],
        conv_w=jnp.concatenate([_pad_heads(cw[:, :ML_WIDTH]), _pad_heads(cw[:, ML_WIDTH:])], axis=1),
        conv_b=jnp.concatenate([_pad_heads(cb[:ML_WIDTH]), _pad_heads(cb[ML_WIDTH:])])[None, :],
        sg_ln_g=sg_ln_g[layer][None, :], sg_ln_b=sg_ln_b[layer][None, :],
        sg_w=sg_w[layer], sg_bias_t=sg_b[layer].T,
        w_br_a=w_br_a[layer].astype(BF16), w_br_b=w_br_b[layer].astype(BF16),
        w_br_c=_pad_heads(w_br_c[layer].T).T.astype(BF16),
        w_out=w_out[layer].astype(BF16),
        ln_g0=ln_g[layer, 0][None, :], ln_b0=ln_b[layer, 0][None, :],
        ln_g1=ln_g[layer, 1][None, :], ln_b1=ln_b[layer, 1][None, :],
    )


def kernel(x, positions, w_in, b_in, conv_w, conv_b, sg_ln_g, sg_ln_b, sg_w, sg_b, w_br_a, w_br_b, w_br_c, w_out,
           ln_g, ln_b, ffn_w1, ffn_w3, ffn_w2, router_w, router_b, moe_w1, moe_w3, moe_w2):
    batch, seq, _ = x.shape
    n = batch * seq
    depth = w_in.shape[0]
    alpha = (2.0 * depth) ** 0.25
    cos, sin = _rope_tables(positions)
    xf = x.reshape(n, D_MODEL)
    xb = None
    for layer in range(depth):
        p = _layer_params(layer, w_in, b_in, conv_w, conv_b, sg_ln_g, sg_ln_b, sg_w, sg_b,
                          w_br_a, w_br_b, w_br_c, w_out, ln_g, ln_b)
        undilated = [g for g, (_, dil) in enumerate(ATT_GROUPS) if dil == 1]
        j = layer // 2
        if layer % 2 == 0:
            riders = [ffn_w1[j], ffn_w3[j], ffn_w2[j]]
        else:
            riders = [moe_w2[j].reshape(N_EXPERTS * D_FF, D_MODEL)]
        if xb is None:
            riders = riders + [xf]
        sg, ml, og, gates_if, *rest = _projections(xf if xb is None else xb, [
            (p["w_sg"], p["b_sg"], BF16, _gelu_tanh),
            (p["w_ml"], p["b_ml"], BF16, None),
            (p["w_mlo"], p["b_mlo"], BF16, _sigmoid),
            (p["w_if"], p["b_if"], F32, None),
        ] + [(p["w_att"][g], p["b_att"][g], BF16, None) for g in undilated],
            riders=riders)
        qkv_plain, cast = rest[:len(undilated)], rest[len(undilated):]
        if xb is None:
            *cast, xb = cast
        att_outs = []
        for g, (_, dil) in enumerate(ATT_GROUPS):
            if dil == 1:
                qkv = qkv_plain[undilated.index(g)].reshape(batch, 1, seq, ATT_PARTS_WIDTH)
            else:
                qkv = _linear_dilated(xb, p["w_att"][g], p["b_att"][g], dil, batch, seq)
            att_outs.append(_attention_group(qkv, cos, sin, g, batch, seq))
        hc = _mlstm(ml, og, gates_if, p["conv_w"], p["conv_b"], p["qk_scale"], batch, seq)
        xf, xb = _mixer_out(xf, att_outs, sg, hc, xb, p, alpha, seq)
        if layer % 2 == 0:
            next_moe = (layer + 1) // 2 if layer + 1 < depth else None
            riders = [] if next_moe is None else [moe_w1[next_moe].reshape(N_EXPERTS * D_MODEL, D_FF),
                                                  moe_w3[next_moe].reshape(N_EXPERTS * D_MODEL, D_FF)]
            xf, xb, *moe_cast = _ffn(xf, xb, *cast, p["ln_g1"], p["ln_b1"], alpha, riders=riders)
        else:
            if layer == 0:
                moe_cast = [moe_w1[j].astype(BF16), moe_w3[j].astype(BF16)]
            w1, w3 = (m.reshape(N_EXPERTS, D_MODEL, D_FF) for m in moe_cast)
            w2 = cast[0].reshape(N_EXPERTS, D_FF, D_MODEL)
            xf = _moe(xf, router_w[j], router_b[j], w1, w3, w2, p["ln_g1"], p["ln_b1"], alpha)
            xb = xf.astype(BF16)
    return xf.reshape(batch, seq, D_MODEL)
```

```python
import functools

import numpy as np
import jax
import jax.numpy as jnp
from jax import lax
from jax.experimental import pallas as pl
from jax.experimental.pallas import tpu as pltpu

F32 = jnp.float32
BF16 = jnp.bfloat16

D_MODEL = 1024
ATT_GROUPS = ((128, 1), (512, 4), (2048, 16))
ATT_HEADS_PER_GROUP = 4
ATT_HEAD_DIM = 64
ATT_WIDTH = 768
ATT_GROUP_WIDTH = ATT_HEADS_PER_GROUP * ATT_HEAD_DIM
ATT_PARTS_WIDTH = 3 * ATT_GROUP_WIDTH
ROPE_THETA = 10000.0
SG_CHUNK = 128
SG_GROUPS = 6
SG_WIDTH = 768
ML_HEADS = 4
ML_HEAD_DIM = 192
ML_HEAD_PAD = 256
ML_WIDTH = 768
ML_PAD_WIDTH = ML_HEADS * ML_HEAD_PAD
ML_CHUNK = 128
ML_CONV = 4
N_BRANCH = 3
D_FF = 2816
N_EXPERTS = 8
TOP_K = 2
LN_EPS = 1e-5
BLOCK = 128
ATT_MIN_ROWS_PER_STEP = 1024
LANES = 128
VMEM_LIMIT = 56 * 1024 * 1024
EXPERT_TILE = 512
DMA_ISSUE_UNROLL = 8
MIXER_SUB_ROWS = 256


def _cparams(*sem):
    return pltpu.CompilerParams(dimension_semantics=sem, vmem_limit_bytes=VMEM_LIMIT)


def _layer_norm(x, g, b):
    mu = jnp.mean(x, axis=-1, keepdims=True)
    xc = x - mu
    var = jnp.mean(xc * xc, axis=-1, keepdims=True)
    return xc * lax.rsqrt(var + LN_EPS) * g + b


def _gelu_tanh(x):
    return 0.5 * x * (1.0 + jnp.tanh(np.sqrt(2.0 / np.pi) * (x + 0.044715 * (x * x * x))))


def _sigmoid(x):
    return 0.5 * (1.0 + jnp.tanh(0.5 * x))


def _dot(a, b):
    return jnp.dot(a, b, preferred_element_type=F32)


def _dot_nt(a, b):
    return lax.dot_general(a, b, (((1,), (1,)), ((), ())), preferred_element_type=F32)


def _dot_tn(a, b):
    return lax.dot_general(a, b, (((0,), (0,)), ((), ())), preferred_element_type=F32)


def _split3(x):
    hi = x.astype(BF16)
    r1 = x - hi.astype(F32)
    mid = r1.astype(BF16)
    lo = (r1 - mid.astype(F32)).astype(BF16)
    return hi, mid, lo


def _rider_specs(arrays, steps):
    specs, shapes = [], []
    for a in arrays:
        span = 1
        while (a.shape[0] * span) % steps or (a.shape[0] * span // steps) % 16:
            span *= 2
        rows = a.shape[0] * span // steps
        specs.append(pl.BlockSpec((rows, a.shape[1]), lambda i, span=span: (i // span, 0)))
        shapes.append(jax.ShapeDtypeStruct(a.shape, BF16))
    return specs, shapes


def _cast_riders(in_refs, out_refs):
    for i_ref, o_ref in zip(in_refs, out_refs):
        o_ref[...] = i_ref[...].astype(o_ref.dtype)


def _projections_kernel(x_ref, *refs, acts, riders):
    n = len(acts)
    ws, bs, rin = refs[:n], refs[n:2 * n], refs[2 * n:2 * n + riders]
    outs, rout = refs[2 * n + riders:3 * n + riders], refs[3 * n + riders:]
    x = x_ref[...].astype(BF16)
    for w_ref, b_ref, o_ref, act in zip(ws, bs, outs, acts):
        res = _dot(x, w_ref[...]) + b_ref[...]
        o_ref[...] = (res if act is None else act(res)).astype(o_ref.dtype)
    _cast_riders(rin, rout)


def _projections(x, heads, riders=(), tm=512):
    m, k = x.shape

    def resident(a):
        return pl.BlockSpec(a.shape, lambda i: (0, 0), pipeline_mode=pl.Buffered(1))

    ws = [h[0] for h in heads]
    bs = [h[1] for h in heads]
    rider_specs, rider_shapes = _rider_specs(riders, m // tm)
    return pl.pallas_call(
        functools.partial(_projections_kernel, acts=tuple(h[3] for h in heads), riders=len(riders)),
        grid=(m // tm,),
        in_specs=([pl.BlockSpec((tm, k), lambda i: (i, 0))] + [resident(w) for w in ws] + [resident(b) for b in bs]
                  + rider_specs),
        out_specs=[pl.BlockSpec((tm, w.shape[1]), lambda i: (i, 0)) for w in ws] + rider_specs,
        out_shape=[jax.ShapeDtypeStruct((m, h[0].shape[1]), h[2]) for h in heads] + rider_shapes,
        compiler_params=_cparams("arbitrary"),
        name="projections",
    )(x, *ws, *bs, *riders)


MAX_ROW_STRIDE = 4


def _linear_dilated_kernel(x_ref, w_ref, b_ref, o_ref, acc, acc2, *, dil):
    res = _dot(x_ref[...], w_ref[...]) + b_ref[...]
    slabs, rows, _ = acc.shape
    for c in range(slabs):
        acc[c] = res[:, c * LANES:(c + 1) * LANES]
    sub = rows // dil
    if dil <= MAX_ROW_STRIDE:
        for r in range(dil):
            for c in range(slabs):
                o_ref[r, :, c * LANES:(c + 1) * LANES] = acc[c, pl.ds(r, sub, stride=dil), :].astype(o_ref.dtype)
        return
    s1 = MAX_ROW_STRIDE
    s2 = dil // s1
    part = rows // s1
    for r1 in range(s1):
        for c in range(slabs):
            acc2[c, r1 * part:(r1 + 1) * part, :] = acc[c, pl.ds(r1, part, stride=s1), :]
    for r1 in range(s1):
        for r2 in range(s2):
            for c in range(slabs):
                o_ref[r1 + s1 * r2, :, c * LANES:(c + 1) * LANES] = (
                    acc2[c, pl.ds(r1 * part + r2, sub, stride=s2), :].astype(o_ref.dtype))


def _linear_dilated(x, w, b, dil, batch, seq, tm=2048):
    k = x.shape[1]
    n = w.shape[1]
    tm = min(tm, seq)
    tiles_per_seq = seq // tm

    def resident(a):
        return pl.BlockSpec(a.shape, lambda i: (0, 0), pipeline_mode=pl.Buffered(1))

    return pl.pallas_call(
        functools.partial(_linear_dilated_kernel, dil=dil),
        grid=(batch * tiles_per_seq,),
        in_specs=[pl.BlockSpec((tm, k), lambda i: (i, 0)), resident(w), resident(b)],
        out_specs=pl.BlockSpec((None, dil, tm // dil, n),
                               lambda i: (i // tiles_per_seq, 0, i % tiles_per_seq, 0)),
        out_shape=jax.ShapeDtypeStruct((batch, dil, seq // dil, n), BF16),
        scratch_shapes=[pltpu.VMEM((n // LANES, tm, LANES), F32),
                        pltpu.VMEM((n // LANES, tm if dil > MAX_ROW_STRIDE else 8, LANES), F32)],
        compiler_params=_cparams("parallel"),
        name=f"linear_dil{dil}",
    )(x, w, b)


def _rope_table_kernel(pos_ref, freq_ref, sign_ref, cos_ref, sin_ref):
    ang = pos_ref[...] * freq_ref[...]
    cos_ref[...] = jnp.cos(ang)
    sin_ref[...] = jnp.sin(ang) * sign_ref[...]


def _rope_tables(positions):
    n = positions.size
    half = ATT_HEAD_DIM // 2
    freqs = ROPE_THETA ** (-jnp.arange(half, dtype=F32) * (2.0 / ATT_HEAD_DIM))
    lane = np.arange(LANES)
    freq_row = freqs[lane % half][None, :]
    sign_row = jnp.asarray(np.where(lane % ATT_HEAD_DIM < half, -1.0, 1.0), F32)[None, :]
    pos = positions.astype(F32).reshape(n, 1)
    tm = 2048
    return pl.pallas_call(
        _rope_table_kernel,
        grid=(n // tm,),
        in_specs=[pl.BlockSpec((tm, 1), lambda i: (i, 0)),
                  pl.BlockSpec((1, LANES), lambda i: (0, 0)),
                  pl.BlockSpec((1, LANES), lambda i: (0, 0))],
        out_specs=[pl.BlockSpec((tm, LANES), lambda i: (i, 0))] * 2,
        out_shape=[jax.ShapeDtypeStruct((n, LANES), F32)] * 2,
        compiler_params=_cparams("parallel"),
        name="rope_tables",
    )(pos, freq_row, sign_row)


def _attn_kernel(q_ref, k_ref, v_ref, cos_ref, sin_ref, o_ref, lse_ref, qs, ks, vs, *, sub, dil):
    per_step = q_ref.shape[0]
    for rr in range(per_step):
        _attn_residue(q_ref.at[rr], k_ref.at[rr], v_ref.at[rr], cos_ref, sin_ref, o_ref.at[rr], lse_ref.at[rr],
                      qs.at[rr], ks.at[rr], vs.at[rr], pl.program_id(1) * per_step + rr, sub=sub, dil=dil)


def _attn_residue(q_ref, k_ref, v_ref, cos_ref, sin_ref, o_ref, lse_ref, qs, ks, vs, res, *, sub, dil):
    nb = sub // BLOCK
    lane = lax.broadcasted_iota(jnp.int32, (BLOCK, LANES), 1)
    first_half = (lane % ATT_HEAD_DIM) < (ATT_HEAD_DIM // 2)
    low_head = lane < ATT_HEAD_DIM

    def rope(x, c, s):
        partner = jnp.where(first_half, pltpu.roll(x, LANES - 32, 1), pltpu.roll(x, 32, 1))
        return x * c + partner * s

    ks[0:BLOCK, :] = jnp.zeros((BLOCK, ATT_GROUP_WIDTH), BF16)
    vs[0:BLOCK, :] = jnp.zeros((BLOCK, ATT_GROUP_WIDTH), BF16)

    def prep(n, carry):
        r0 = pl.multiple_of(n * BLOCK, BLOCK)
        if dil == 1:
            c = cos_ref[pl.ds(r0, BLOCK), :]
            s = sin_ref[pl.ds(r0, BLOCK), :]
        else:
            c = cos_ref[pl.ds(r0 * dil + res, BLOCK, stride=dil), :]
            s = sin_ref[pl.ds(r0 * dil + res, BLOCK, stride=dil), :]
        for pair in range(2):
            cols = slice(pair * LANES, (pair + 1) * LANES)
            q = rope(q_ref[pl.ds(r0, BLOCK), cols].astype(F32), c, s) * (ATT_HEAD_DIM ** -0.5)
            qs[pl.ds(r0, BLOCK), (2 * pair) * LANES:(2 * pair + 1) * LANES] = jnp.where(low_head, q, 0.0).astype(BF16)
            qs[pl.ds(r0, BLOCK), (2 * pair + 1) * LANES:(2 * pair + 2) * LANES] = jnp.where(low_head, 0.0, q).astype(BF16)
            k = rope(k_ref[pl.ds(r0, BLOCK), cols].astype(F32), c, s)
            ks[pl.ds(r0 + BLOCK, BLOCK), cols] = k.astype(BF16)
        vs[pl.ds(r0 + BLOCK, BLOCK), :] = v_ref[pl.ds(r0, BLOCK), :]
        return carry

    if nb <= 2:
        for n in range(nb):
            prep(n, 0)
    else:
        lax.fori_loop(0, nb, prep, 0)

    qi = lax.broadcasted_iota(jnp.int32, (BLOCK, 2 * BLOCK), 0)
    ki = lax.broadcasted_iota(jnp.int32, (BLOCK, 2 * BLOCK), 1)
    dist = qi + BLOCK - ki
    band = (dist >= 0) & (dist <= BLOCK)
    neg_inf = jnp.float32(-jnp.inf)
    bias_rest = jnp.where(band, 0.0, neg_inf)
    bias_first = jnp.where(band & (ki >= BLOCK), 0.0, neg_inf)

    def block(n, bias):
        r0 = pl.multiple_of(n * BLOCK, BLOCK)
        heads = range(ATT_HEADS_PER_GROUP)
        pair_cols = [slice(pair * LANES, (pair + 1) * LANES) for pair in range(2)]
        kw = [ks[pl.ds(r0, 2 * BLOCK), c] for c in pair_cols]
        vw = [vs[pl.ds(r0, 2 * BLOCK), c] for c in pair_cols]
        s = [_dot_nt(qs[pl.ds(r0, BLOCK), hd * LANES:(hd + 1) * LANES], kw[hd // 2]) + bias for hd in heads]
        m = [jnp.max(s[hd], axis=-1, keepdims=True) for hd in heads]
        p = [jnp.exp(s[hd] - m[hd]) for hd in heads]
        den = [jnp.sum(p[hd], axis=-1, keepdims=True) for hd in heads]
        o = [_dot(p[hd].astype(BF16), vw[hd // 2]) * (1.0 / den[hd]) for hd in heads]
        lse = [m[hd] + jnp.log(den[hd]) for hd in heads]
        for pair in range(2):
            lo, hi = 2 * pair, 2 * pair + 1
            o_ref[pl.ds(r0, BLOCK), pair_cols[pair]] = jnp.where(low_head, o[lo], o[hi]).astype(o_ref.dtype)
            lse_ref[pl.ds(r0, BLOCK), pair_cols[pair]] = jnp.where(low_head, lse[lo], lse[hi])

    block(0, bias_first)
    if nb <= 2:
        for n in range(1, nb):
            block(n, bias_rest)
        return

    def body(n, carry):
        block(n, bias_rest)
        return carry

    lax.fori_loop(1, nb, body, 0, unroll=4)


def _attention_group(qkv, cos, sin, group, batch, seq):
    window, dil = ATT_GROUPS[group]
    assert window // dil == BLOCK
    sub = seq // dil
    assert sub % BLOCK == 0

    per_step = min(dil, max(1, ATT_MIN_ROWS_PER_STEP // sub))

    def part(p):
        return pl.BlockSpec((None, per_step, sub, ATT_GROUP_WIDTH), lambda b, r: (b, r, 0, p))

    tab = pl.BlockSpec((seq, LANES), lambda b, r: (b, 0))
    out = pl.BlockSpec((None, per_step, sub, ATT_GROUP_WIDTH), lambda b, r: (b, r, 0, 0))
    return pl.pallas_call(
        functools.partial(_attn_kernel, sub=sub, dil=dil),
        grid=(batch, dil // per_step),
        in_specs=[part(0), part(1), part(2), tab, tab],
        out_specs=[out, out],
        out_shape=[jax.ShapeDtypeStruct((batch, dil, sub, ATT_GROUP_WIDTH), BF16),
                   jax.ShapeDtypeStruct((batch, dil, sub, ATT_GROUP_WIDTH), F32)],
        scratch_shapes=[pltpu.VMEM((per_step, sub, ATT_HEADS_PER_GROUP * LANES), BF16),
                        pltpu.VMEM((per_step, sub + BLOCK, ATT_GROUP_WIDTH), BF16),
                        pltpu.VMEM((per_step, sub + BLOCK, ATT_GROUP_WIDTH), BF16)],
        compiler_params=_cparams("parallel", "parallel"),
        name=f"attention_g{group}",
    )(qkv, qkv, qkv, cos, sin)


ML_SEQS_PER_STEP = 2


def _mlstm_kernel(qk_ref, v_ref, og_ref, g_ref, cw_ref, cb_ref, sc_ref, out_ref, frame, ct, m_s):
    taps = ML_CONV

    @pl.when(pl.program_id(1) == 0)
    def _():
        frame[:, 0:ML_CHUNK, :] = jnp.zeros((frame.shape[0], ML_CHUNK, 2 * ML_PAD_WIDTH), BF16)
        ct[...] = jnp.zeros(ct.shape, F32)
        m_s[...] = jnp.zeros(m_s.shape, F32)

    out_row = lax.broadcasted_iota(jnp.int32, (ML_CHUNK, 2 * ML_CHUNK), 0)
    src_row = lax.broadcasted_iota(jnp.int32, (ML_CHUNK, 2 * ML_CHUNK), 1)
    picks = [jnp.where(src_row == out_row + (ML_CHUNK - (taps - 1 - j)), 1.0, 0.0).astype(BF16)
             for j in range(taps - 1)]
    row = lax.broadcasted_iota(jnp.int32, (ML_CHUNK, ML_CHUNK), 0)
    colm = lax.broadcasted_iota(jnp.int32, (ML_CHUNK, ML_CHUNK), 1)
    causal = row >= colm
    tri = jnp.where(causal, 1.0, 0.0).astype(BF16)
    neg_inf = jnp.float32(-jnp.inf)

    seqs = range(qk_ref.shape[0])
    pairs = [(s, h) for s in seqs for h in range(ML_HEADS)]

    def head_cols(h):
        return slice(h * ML_HEAD_PAD, (h + 1) * ML_HEAD_PAD)

    act, gates, bcum, bcum_t, gates_t = {}, {}, {}, {}, {}
    for s in seqs:
        cur = qk_ref[s]
        frame[s, ML_CHUNK:2 * ML_CHUNK, :] = cur
        both = frame[s]
        acc = cb_ref[...] + cur.astype(F32) * cw_ref[taps - 1:taps, :]
        for j in range(taps - 1):
            acc = acc + _dot(picks[j], both) * cw_ref[j:j + 1, :]
        frame[s, 0:ML_CHUNK, :] = cur
        act[s] = (acc * _sigmoid(acc) * sc_ref[...]).astype(BF16)
        gates[s] = g_ref[s]
        log_f = -(jnp.maximum(-gates[s], 0.0) + jnp.log1p(jnp.exp(-jnp.abs(gates[s]))))
        hi, mid, lo = _split3(log_f)
        bcum[s] = _dot(tri, hi) + _dot(tri, mid) + _dot(tri, lo)
        bcum_t[s] = bcum[s].T
        gates_t[s] = gates[s].T

    qh, kh, vh, c_old, m_prev, b_c, i_c, log_d, m_inter, m = ({} for _ in range(10))
    for p in pairs:
        s, h = p
        qh[p] = act[s][:, head_cols(h)]
        kh[p] = act[s][:, ML_PAD_WIDTH + h * ML_HEAD_PAD:ML_PAD_WIDTH + (h + 1) * ML_HEAD_PAD]
        vh[p] = v_ref[s, :, head_cols(h)]
        c_old[p] = ct[s, h]
        m_prev[p] = m_s[s, h, 0:1, 0:1]
        b_c[p] = bcum[s][:, ML_HEADS + h:ML_HEADS + h + 1]
        b_r = bcum_t[s][ML_HEADS + h:ML_HEADS + h + 1, :]
        i_c[p] = gates[s][:, h:h + 1]
        i_r = gates_t[s][h:h + 1, :]
        log_d[p] = jnp.where(causal, b_c[p] - b_r + i_r, neg_inf)
        m_inter[p] = b_c[p] + m_prev[p]
        m[p] = jnp.maximum(m_inter[p], jnp.max(log_d[p], axis=-1, keepdims=True))

    scores = {p: _dot_nt(qh[p], kh[p]) for p in pairs}
    carried = {p: _dot(qh[p], c_old[p].astype(BF16)) for p in pairs}
    sc = {p: (scores[p] * jnp.exp(log_d[p] - m[p])).astype(BF16) for p in pairs}
    num = {p: _dot(sc[p], vh[p]) + jnp.exp(m_inter[p] - m[p]) * carried[p] for p in pairs}
    for p in pairs:
        s, h = p
        den = num[p][:, ML_HEAD_DIM:ML_HEAD_DIM + 1]
        hid = num[p] / jnp.maximum(jnp.abs(den), jnp.exp(-m[p]))
        out_ref[s, :, head_cols(h)] = (og_ref[s, :, head_cols(h)].astype(F32) * hid).astype(out_ref.dtype)
    for p in pairs:
        s, h = p
        b_last = b_c[p][ML_CHUNK - 1:ML_CHUNK, :]
        m_new = m[p][ML_CHUNK - 1:ML_CHUNK, :]
        w = jnp.exp(b_last - b_c[p] + i_c[p] - m_new)
        decay = jnp.exp(b_last + m_prev[p] - m_new)
        wv = (w * vh[p].astype(F32)).astype(BF16)
        ct[s, h] = decay * c_old[p] + _dot_tn(kh[p], wv)
        m_s[s, h] = jnp.broadcast_to(m_new, m_s.shape[2:])


def _mlstm(ml, og, gates, conv_w, conv_b, qk_scale, batch, seq):
    nc = seq // ML_CHUNK
    group = ML_SEQS_PER_STEP if batch % ML_SEQS_PER_STEP == 0 else 1

    def whole(a):
        return pl.BlockSpec(a.shape, lambda b, c: (0, 0))

    def chunk(width, col):
        return pl.BlockSpec((group, ML_CHUNK, width), lambda b, c: (b, c, col))

    out = pl.pallas_call(
        _mlstm_kernel,
        grid=(batch // group, nc),
        in_specs=[chunk(2 * ML_PAD_WIDTH, 0), chunk(ML_PAD_WIDTH, 2), chunk(ML_PAD_WIDTH, 0), chunk(LANES, 0),
                  whole(conv_w), whole(conv_b), whole(qk_scale)],
        out_specs=chunk(ML_PAD_WIDTH, 0),
        out_shape=jax.ShapeDtypeStruct((batch, seq, ML_PAD_WIDTH), BF16),
        scratch_shapes=[pltpu.VMEM((group, 2 * ML_CHUNK, 2 * ML_PAD_WIDTH), BF16),
                        pltpu.VMEM((group, ML_HEADS, ML_HEAD_PAD, ML_HEAD_PAD), F32),
                        pltpu.VMEM((group, ML_HEADS, 8, LANES), F32)],
        compiler_params=_cparams("parallel", "arbitrary"),
        name="mlstm",
    )(ml.reshape(batch, seq, -1), ml.reshape(batch, seq, -1), og.reshape(batch, seq, -1),
      gates.reshape(batch, seq, -1), conv_w, conv_b, qk_scale)
    return out.reshape(batch * seq, ML_PAD_WIDTH)


def _mixer_out_kernel(x_ref, o0_ref, l0_ref, o1_ref, l1_ref, o2_ref, l2_ref, hc_ref, xin_ref,
                      sgg_ref, sgb_ref, sgw_ref, sgbias_ref, wa_ref, wb_ref, wc_ref, wo_ref, lng_ref, lnb_ref,
                      wg_ref, bg_ref, wsg_ref, bsg_ref, xf_ref, xb_ref, unperm, *, alpha):
    tm = x_ref.shape[0]
    slabs = ATT_GROUP_WIDTH // LANES
    att_refs = (l0_ref, l1_ref, l2_ref, o0_ref, o1_ref, o2_ref)

    for slot, ref in enumerate(att_refs):
        dil = ref.shape[0]
        if dil > 1:
            for r in range(dil):
                for c in range(slabs):
                    unperm[slabs * slot + c, pl.ds(r, tm // dil, stride=dil), :] = (
                        ref[r, :, c * LANES:(c + 1) * LANES].astype(F32))

    def token_order(slot, rows):
        ref = att_refs[slot]
        if ref.shape[0] == 1:
            return ref[0, rows, :].astype(F32)
        return jnp.concatenate([unperm[slabs * slot + c, rows, :] for c in range(slabs)], axis=1)

    row = lax.broadcasted_iota(jnp.int32, (SG_CHUNK, SG_CHUNK), 0)
    colm = lax.broadcasted_iota(jnp.int32, (SG_CHUNK, SG_CHUNK), 1)
    causal = row >= colm
    group_dim = SG_WIDTH // SG_GROUPS
    sg_w = [jnp.where(causal, sgw_ref[g], 0.0).astype(BF16) for g in range(SG_GROUPS)]

    subs = [slice(h * MIXER_SUB_ROWS, (h + 1) * MIXER_SUB_ROWS) for h in range(tm // MIXER_SUB_ROWS)]

    def gate(rows, i):
        cols = slice(i * D_MODEL, (i + 1) * D_MODEL)
        return _sigmoid(_dot(xin_ref[rows, :], wg_ref[:, cols]) + bg_ref[:, cols])

    proj_c = [_dot(hc_ref[rows, :], wc_ref[...]) for rows in subs]

    y_a = []
    for rows in subs:
        l0, l1, l2 = token_order(0, rows), token_order(1, rows), token_order(2, rows)
        lmax = jnp.maximum(jnp.maximum(l0, l1), l2)
        e0, e1, e2 = jnp.exp(l0 - lmax), jnp.exp(l1 - lmax), jnp.exp(l2 - lmax)
        inv = 1.0 / (e0 + e1 + e2)
        y_a.append(((e0 * inv) * token_order(3, rows) + (e1 * inv) * token_order(4, rows)
                    + (e2 * inv) * token_order(5, rows)).astype(BF16))
    proj_a = [_dot(y, wa_ref[...]) for y in y_a]

    def sg_half(rows, i):
        cols = slice(i * SG_WIDTH, (i + 1) * SG_WIDTH)
        return _gelu_tanh(_dot(xin_ref[rows, :], wsg_ref[:, cols]) + bsg_ref[:, cols])

    v = [_layer_norm(sg_half(rows, 1), sgg_ref[...], sgb_ref[...]).astype(BF16) for rows in subs]
    y_b = []
    for rows, vs in zip(subs, v):
        chunks = []
        for c in range(MIXER_SUB_ROWS // SG_CHUNK):
            crows = slice(c * SG_CHUNK, (c + 1) * SG_CHUNK)
            parts = [_dot(sg_w[g], vs[crows, g * group_dim:(g + 1) * group_dim]) + sgbias_ref[:, g:g + 1]
                     for g in range(SG_GROUPS)]
            chunks.append(jnp.concatenate(parts, axis=1))
        y_b.append((sg_half(rows, 0) * jnp.concatenate(chunks, axis=0)).astype(BF16))
    proj_b = [_dot(y, wb_ref[...]) for y in y_b]

    z = [(gate(rows, 0) * pa + gate(rows, 1) * pb + gate(rows, 2) * pc).astype(BF16)
         for rows, pa, pb, pc in zip(subs, proj_a, proj_b, proj_c)]
    mix = [_dot(zs, wo_ref[...]) for zs in z]
    for rows, m in zip(subs, mix):
        out = _layer_norm(alpha * x_ref[rows, :] + m, lng_ref[...], lnb_ref[...])
        xf_ref[rows, :] = out
        xb_ref[rows, :] = out.astype(BF16)


def _mixer_out(x, att_outs, hc, x_in, p, alpha, seq, tm=512):
    n = x.shape[0]
    tiles_per_seq = seq // tm

    def rows(width):
        return pl.BlockSpec((tm, width), lambda i: (i, 0))

    def whole(a):
        return pl.BlockSpec(a.shape, lambda i: (0,) * a.ndim)

    def residue_major(dil):
        return pl.BlockSpec((None, dil, tm // dil, ATT_GROUP_WIDTH),
                            lambda i: (i // tiles_per_seq, 0, i % tiles_per_seq, 0))

    att_specs, att_args = [], []
    for (o, lse), (_, dil) in zip(att_outs, ATT_GROUPS):
        att_specs += [residue_major(dil)] * 2
        att_args += [o, lse]
    weights = [p["sg_ln_g"], p["sg_ln_b"], p["sg_w"], p["sg_bias_t"], p["w_br_a"], p["w_br_b"], p["w_br_c"],
               p["w_out"], p["ln_g0"], p["ln_b0"], p["w_g"], p["b_g"], p["w_sg"], p["b_sg"]]
    return pl.pallas_call(
        functools.partial(_mixer_out_kernel, alpha=alpha),
        grid=(n // tm,),
        in_specs=[rows(D_MODEL)] + att_specs
                 + [rows(ML_PAD_WIDTH), rows(D_MODEL)]
                 + [whole(w) for w in weights],
        out_specs=[rows(D_MODEL), rows(D_MODEL)],
        out_shape=[jax.ShapeDtypeStruct((n, D_MODEL), F32), jax.ShapeDtypeStruct((n, D_MODEL), BF16)],
        scratch_shapes=[pltpu.VMEM((2 * len(ATT_GROUPS) * ATT_GROUP_WIDTH // LANES, tm, LANES), F32)],
        compiler_params=_cparams("parallel"),
        name="mixer_out",
    )(x, *att_args, hc, x_in, *weights)


def _swiglu_tile(x, w1_ref, w3_ref, w2_ref, prod, tf):
    for c in range(D_FF // tf):
        cols = slice(c * tf, (c + 1) * tf)
        h1 = _dot(x, w1_ref[:, cols])
        h3 = _dot(x, w3_ref[:, cols])
        prod[:, cols] = (h1 * _sigmoid(h1) * h3).astype(BF16)
    return _dot(prod[...], w2_ref[...])


def _ffn_kernel(xf_ref, xb_ref, w1_ref, w3_ref, w2_ref, lng_ref, lnb_ref, *refs, alpha, tf, riders):
    rin, (of_ref, ob_ref), rout, prod = refs[:riders], refs[riders:riders + 2], refs[riders + 2:-1], refs[-1]
    f = _swiglu_tile(xb_ref[...], w1_ref, w3_ref, w2_ref, prod, tf)
    out = _layer_norm(alpha * xf_ref[...] + f, lng_ref[...], lnb_ref[...])
    of_ref[...] = out
    ob_ref[...] = out.astype(BF16)
    _cast_riders(rin, rout)


def _ffn(xf, xb, w1, w3, w2, ln_g, ln_b, alpha, riders=(), tm=512, tf=256):
    n = xf.shape[0]

    def resident(shape):
        return pl.BlockSpec(shape, lambda i: (0, 0), pipeline_mode=pl.Buffered(1))

    rider_specs, rider_shapes = _rider_specs(riders, n // tm)
    return pl.pallas_call(
        functools.partial(_ffn_kernel, alpha=alpha, tf=tf, riders=len(riders)),
        grid=(n // tm,),
        in_specs=[pl.BlockSpec((tm, D_MODEL), lambda i: (i, 0)),
                  pl.BlockSpec((tm, D_MODEL), lambda i: (i, 0)),
                  resident((D_MODEL, D_FF)), resident((D_MODEL, D_FF)), resident((D_FF, D_MODEL)),
                  pl.BlockSpec((1, D_MODEL), lambda i: (0, 0)),
                  pl.BlockSpec((1, D_MODEL), lambda i: (0, 0))] + rider_specs,
        out_specs=[pl.BlockSpec((tm, D_MODEL), lambda i: (i, 0))] * 2 + rider_specs,
        out_shape=[jax.ShapeDtypeStruct((n, D_MODEL), F32), jax.ShapeDtypeStruct((n, D_MODEL), BF16)]
                  + rider_shapes,
        scratch_shapes=[pltpu.VMEM((tm, D_FF), BF16)],
        compiler_params=_cparams("arbitrary"),
        name="ffn",
    )(xf, xb, w1, w3, w2, ln_g, ln_b, *riders)


def _router_kernel(x_ref, w_ref, b_ref, info_ref):
    xh, xm, _ = _split3(x_ref[...])
    wh, wm, _ = _split3(w_ref[...])
    logits = (_dot(xh, wh) + (_dot(xh, wm) + _dot(xm, wh))) + b_ref[...]
    lane = lax.broadcasted_iota(jnp.int32, logits.shape, 1)
    neg_inf = jnp.float32(-jnp.inf)
    logits = jnp.where(lane < N_EXPERTS, logits, neg_inf)
    v1 = jnp.max(logits, axis=-1, keepdims=True)
    i1 = jnp.min(jnp.where(logits == v1, lane, LANES), axis=-1, keepdims=True)
    rest = jnp.where(lane == i1, neg_inf, logits)
    v2 = jnp.max(rest, axis=-1, keepdims=True)
    i2 = jnp.min(jnp.where(rest == v2, lane, LANES), axis=-1, keepdims=True)
    e = jnp.exp(v2 - v1)
    inv = 1.0 / (1.0 + e)
    info_ref[...] = jnp.where(lane == 0, i1.astype(F32),
                              jnp.where(lane == 1, i2.astype(F32),
                                        jnp.where(lane == 2, inv, jnp.where(lane == 3, e * inv, 0.0))))


def _router(xf, router_w, router_b, tm=1024):
    n = xf.shape[0]
    w = jnp.pad(router_w, ((0, 0), (0, LANES - N_EXPERTS)))
    b = jnp.pad(router_b, (0, LANES - N_EXPERTS))[None, :]
    return pl.pallas_call(
        _router_kernel,
        grid=(n // tm,),
        in_specs=[pl.BlockSpec((tm, D_MODEL), lambda i: (i, 0)),
                  pl.BlockSpec((D_MODEL, LANES), lambda i: (0, 0)),
                  pl.BlockSpec((1, LANES), lambda i: (0, 0))],
        out_specs=pl.BlockSpec((tm, LANES), lambda i: (i, 0)),
        out_shape=jax.ShapeDtypeStruct((n, LANES), F32),
        compiler_params=_cparams("parallel"),
        name="router",
    )(xf, w, b)


def _routing_plan(info, n):
    tile = EXPERT_TILE
    n_tiles = TOP_K * n // tile + N_EXPERTS
    expert = info[:, 0:TOP_K].astype(jnp.int32).reshape(-1)
    onehot = (expert[:, None] == jnp.arange(N_EXPERTS, dtype=jnp.int32)[None, :]).astype(jnp.int32)
    running = jnp.cumsum(onehot, axis=0)
    rank = jnp.sum(onehot * running, axis=1) - 1
    counts = running[-1]
    padded = ((counts + tile - 1) // tile) * tile
    ends = jnp.cumsum(padded)
    starts = ends - padded
    dest = jnp.sum(onehot * starts[None, :], axis=1) + rank
    used = ends[-1] // tile
    tile_start = jnp.arange(n_tiles, dtype=jnp.int32) * tile
    tile_expert = jnp.sum((tile_start[:, None] >= ends[None, :]).astype(jnp.int32), axis=1)
    last_expert = jnp.sum((tile_start[used - 1] >= ends).astype(jnp.int32))
    tile_expert = jnp.minimum(tile_expert, last_expert)
    plan = dict(dest=dest, tile_expert=tile_expert, used=used.reshape(1),
                pad_start=starts + counts, pad_count=padded - counts)
    return {k: v.astype(jnp.int32) for k, v in plan.items()}, n_tiles


def _rows_wait(src, dst, rows, sem):
    pltpu.make_async_copy(src.at[pl.ds(0, rows)], dst.at[pl.ds(0, rows)], sem).wait()


ROW_SUBLANES = D_MODEL // LANES


def _tile_copy(src, src_row, dst, dst_row, sem):
    s0 = pl.multiple_of(src_row * ROW_SUBLANES, ROW_SUBLANES)
    d0 = pl.multiple_of(dst_row * ROW_SUBLANES, ROW_SUBLANES)
    return pltpu.make_async_copy(src.at[pl.ds(s0, ROW_SUBLANES)], dst.at[pl.ds(d0, ROW_SUBLANES)], sem)


def _to_row_tiles(ref, x):
    m = x.shape[0]
    for c in range(ROW_SUBLANES):
        ref[pl.ds(c, m, stride=ROW_SUBLANES), :] = x[:, c * LANES:(c + 1) * LANES]


def _from_row_tiles(ref):
    m = ref.shape[0] // ROW_SUBLANES
    return jnp.concatenate([ref[pl.ds(c, m, stride=ROW_SUBLANES), :] for c in range(ROW_SUBLANES)], axis=1)


DISPATCH_SLOTS = 3


def _dispatch_kernel(pad_start_ref, pad_count_ref, used_ref, dest_ref, x_ref, xs_ref, buf, tiles, zero_row,
                     load_sem, row_sem, pad_sem, *, tm):
    i = pl.program_id(0)

    @pl.when(i == 0)
    def _():
        zero_row[...] = jnp.zeros(zero_row.shape, F32)
        for e in range(N_EXPERTS):
            def fill(j, carry, e=e):
                _tile_copy(zero_row, 0, xs_ref, pad_start_ref[e] + j, pad_sem).start()
                return carry
            lax.fori_loop(0, pad_count_ref[e], fill, 0)
        for e in range(N_EXPERTS):
            def drain(j, carry):
                _tile_copy(zero_row, 0, xs_ref, 0, pad_sem).wait()
                return carry
            lax.fori_loop(0, pad_count_ref[e], drain, 0)

        tile_rows = tm * ROW_SUBLANES
        spare = xs_ref.shape[0] // tile_rows - used_ref[0]
        tiles[1] = jnp.zeros(tiles.shape[1:], F32)

        def zero_tile(j):
            row0 = pl.multiple_of((used_ref[0] + j) * tile_rows, tile_rows)
            return pltpu.make_async_copy(tiles.at[1], xs_ref.at[pl.ds(row0, tile_rows)], pad_sem)

        def fill_tile(j, carry):
            zero_tile(j).start()
            return carry

        def drain_tile(j, carry):
            zero_tile(j).wait()
            return carry

        lax.fori_loop(0, spare, fill_tile, 0)
        lax.fori_loop(0, spare, drain_tile, 0)

    steps = pl.num_programs(0)
    slot = i % DISPATCH_SLOTS
    prev_slot = (i + DISPATCH_SLOTS - 1) % DISPATCH_SLOTS
    stage = i % 2

    def load(tile, into):
        return pltpu.make_async_copy(x_ref.at[pl.ds(tile * tm, tm), :], buf.at[into], load_sem.at[into])

    def wait_rows(of_stage):
        for _ in range(TOP_K):
            _rows_wait(tiles.at[of_stage], xs_ref, tm * ROW_SUBLANES, row_sem.at[of_stage])

    @pl.when(i == 0)
    def _():
        load(0, 0).start()

        @pl.when(steps > 1)
        def _():
            load(1, 1).start()

    load(i, slot).wait()
    _to_row_tiles(tiles.at[stage], buf[slot])

    def start(t, carry):
        for c in range(TOP_K):
            _tile_copy(tiles.at[stage], t, xs_ref, dest_ref[0, 0, TOP_K * t + c], row_sem.at[stage]).start(
                priority=c % 2)
        return carry

    lax.fori_loop(0, tm, start, 0, unroll=DMA_ISSUE_UNROLL)

    @pl.when(i > 0)
    def _():
        wait_rows(1 - stage)

    @pl.when(i + 2 < steps)
    def _():
        load(i + 2, prev_slot).start()

    @pl.when(i == steps - 1)
    def _():
        wait_rows(stage)


def _dispatch(xf, dest, pad_start, pad_count, used, n_slots, tm=EXPERT_TILE):
    n = xf.shape[0]
    dest3 = dest.reshape(n // tm, 1, TOP_K * tm)
    grid_spec = pltpu.PrefetchScalarGridSpec(
        num_scalar_prefetch=3,
        grid=(n // tm,),
        in_specs=[pl.BlockSpec((1, 1, TOP_K * tm), lambda i, ps, pc, nu: (i, 0, 0), memory_space=pltpu.SMEM),
                  pl.BlockSpec(memory_space=pl.ANY)],
        out_specs=pl.BlockSpec(memory_space=pl.ANY),
        scratch_shapes=[pltpu.VMEM((DISPATCH_SLOTS, tm, D_MODEL), F32),
                        pltpu.VMEM((2, tm * ROW_SUBLANES, LANES), F32),
                        pltpu.VMEM((ROW_SUBLANES, LANES), F32),
                        pltpu.SemaphoreType.DMA((DISPATCH_SLOTS,)),
                        pltpu.SemaphoreType.DMA((2,)),
                        pltpu.SemaphoreType.DMA],
    )
    return pl.pallas_call(
        functools.partial(_dispatch_kernel, tm=tm),
        grid_spec=grid_spec,
        out_shape=jax.ShapeDtypeStruct((n_slots * ROW_SUBLANES, LANES), F32),
        compiler_params=_cparams("arbitrary"),
        name="moe_dispatch",
    )(pad_start, pad_count, used, dest3, xf)


def _experts_kernel(te_ref, used_ref, xs_ref, w1_ref, w3_ref, w2_ref, ys_ref, prod, *, tf):
    del te_ref
    i = pl.program_id(0)

    @pl.when(i < used_ref[0])
    def _():
        x = _from_row_tiles(xs_ref).astype(BF16)
        _to_row_tiles(ys_ref, _swiglu_tile(x, w1_ref, w3_ref, w2_ref, prod, tf))

    @pl.when(i >= used_ref[0])
    def _():
        ys_ref[...] = jnp.zeros(ys_ref.shape, ys_ref.dtype)


def _experts(xs, tile_expert, used, w1, w3, w2, n_tiles, tf=256):
    tile = EXPERT_TILE
    grid_spec = pltpu.PrefetchScalarGridSpec(
        num_scalar_prefetch=2,
        grid=(n_tiles,),
        in_specs=[pl.BlockSpec((tile * ROW_SUBLANES, LANES), lambda i, te, nu: (jnp.minimum(i, nu[0] - 1), 0)),
                  pl.BlockSpec((None, D_MODEL, D_FF), lambda i, te, nu: (te[i], 0, 0)),
                  pl.BlockSpec((None, D_MODEL, D_FF), lambda i, te, nu: (te[i], 0, 0)),
                  pl.BlockSpec((None, D_FF, D_MODEL), lambda i, te, nu: (te[i], 0, 0))],
        out_specs=pl.BlockSpec((tile * ROW_SUBLANES, LANES), lambda i, te, nu: (i, 0)),
        scratch_shapes=[pltpu.VMEM((tile, D_FF), BF16)],
    )
    return pl.pallas_call(
        functools.partial(_experts_kernel, tf=tf),
        grid_spec=grid_spec,
        out_shape=jax.ShapeDtypeStruct((n_tiles * tile * ROW_SUBLANES, LANES), F32),
        compiler_params=_cparams("arbitrary"),
        name="moe_experts",
    )(tile_expert, used, xs, w1, w3, w2)


def _combine_kernel(dest_ref, next_dest_ref, xf_ref, info_ref, ys_ref, lng_ref, lnb_ref, of_ref, got, sem, *, alpha):
    tm = xf_ref.shape[0]
    i = pl.program_id(0)
    slot = i % 2

    def gather(dests, into):
        def start(t, carry):
            for c in range(TOP_K):
                _tile_copy(ys_ref, dests[0, 0, TOP_K * t + c], got.at[into, c], t, sem.at[into]).start(
                    priority=c % 2)
            return carry
        lax.fori_loop(0, tm, start, 0, unroll=DMA_ISSUE_UNROLL)

    @pl.when(i == 0)
    def _():
        gather(dest_ref, 0)

    @pl.when(i + 1 < pl.num_programs(0))
    def _():
        gather(next_dest_ref, 1 - slot)

    for c in range(TOP_K):
        _rows_wait(ys_ref, got.at[slot, c], tm * ROW_SUBLANES, sem.at[slot])
    info = info_ref[...]
    mix = (info[:, 2:3] * _from_row_tiles(got.at[slot, 0]) + info[:, 3:4] * _from_row_tiles(got.at[slot, 1]))
    of_ref[...] = _layer_norm(alpha * xf_ref[...] + mix, lng_ref[...], lnb_ref[...])


def _combine(xf, info, ys, dest, ln_g, ln_b, alpha, tm=256):
    n = xf.shape[0]
    steps = n // tm
    dest3 = dest.reshape(steps, 1, TOP_K * tm)
    return pl.pallas_call(
        functools.partial(_combine_kernel, alpha=alpha),
        grid=(steps,),
        in_specs=[pl.BlockSpec((1, 1, TOP_K * tm), lambda i: (i, 0, 0), memory_space=pltpu.SMEM),
                  pl.BlockSpec((1, 1, TOP_K * tm), lambda i: (jnp.minimum(i + 1, steps - 1), 0, 0),
                               memory_space=pltpu.SMEM),
                  pl.BlockSpec((tm, D_MODEL), lambda i: (i, 0)),
                  pl.BlockSpec((tm, LANES), lambda i: (i, 0)),
                  pl.BlockSpec(memory_space=pl.ANY),
                  pl.BlockSpec((1, D_MODEL), lambda i: (0, 0)),
                  pl.BlockSpec((1, D_MODEL), lambda i: (0, 0))],
        out_specs=pl.BlockSpec((tm, D_MODEL), lambda i: (i, 0)),
        out_shape=jax.ShapeDtypeStruct((n, D_MODEL), F32),
        scratch_shapes=[pltpu.VMEM((2, TOP_K, tm * ROW_SUBLANES, LANES), F32), pltpu.SemaphoreType.DMA((2,))],
        compiler_params=_cparams("arbitrary"),
        name="moe_combine",
    )(dest3, dest3, xf, info, ys, ln_g, ln_b)


def _moe(xf, router_w, router_b, w1, w3, w2, ln_g, ln_b, alpha):
    n = xf.shape[0]
    info = _router(xf, router_w, router_b)
    plan, n_tiles = _routing_plan(info, n)
    xs = _dispatch(xf, plan["dest"], plan["pad_start"], plan["pad_count"], plan["used"], n_tiles * EXPERT_TILE)
    ys = _experts(xs, plan["tile_expert"], plan["used"], w1, w3, w2, n_tiles)
    return _combine(xf, info, ys, plan["dest"], ln_g, ln_b, alpha)


def _pad_heads(w):
    lead = w.shape[:-1]
    w = w.reshape(*lead, ML_HEADS, ML_HEAD_DIM)
    w = jnp.pad(w, [(0, 0)] * len(lead) + [(0, 0), (0, ML_HEAD_PAD - ML_HEAD_DIM)])
    return w.reshape(*lead, ML_PAD_WIDTH)


def _group_qkv(w, group):
    gw = ATT_GROUP_WIDTH
    return jnp.concatenate([w[..., part * ATT_WIDTH + group * gw:part * ATT_WIDTH + (group + 1) * gw]
                            for part in range(3)], axis=-1)


def _layer_params(layer, w_in, b_in, conv_w, conv_b, sg_ln_g, sg_ln_b, sg_w, sg_b,
                  w_br_a, w_br_b, w_br_c, w_out, ln_g, ln_b):
    w, b = w_in[layer].astype(BF16), b_in[layer]
    o_att, o_sg, o_qk, o_v, o_o, o_i, o_g = 0, 2304, 3840, 5376, 6144, 6912, 6920
    ones_col = jnp.zeros((ML_HEADS, ML_HEAD_PAD), F32).at[:, ML_HEAD_DIM].set(1.0).reshape(ML_PAD_WIDTH)
    w_qk = jnp.concatenate([_pad_heads(w[:, o_qk:o_qk + ML_WIDTH]), _pad_heads(w[:, o_qk + ML_WIDTH:o_v])], axis=1)
    b_qk = jnp.concatenate([_pad_heads(b[o_qk:o_qk + ML_WIDTH]), _pad_heads(b[o_qk + ML_WIDTH:o_v])])
    qk_scale = jnp.concatenate([jnp.ones((ML_PAD_WIDTH,), F32),
                                jnp.full((ML_PAD_WIDTH,), ML_HEAD_DIM ** -0.5, F32)])
    cw, cb = conv_w[layer], conv_b[layer]
    w_att, b_att = w[:, o_att:o_sg], b[o_att:o_sg]
    return dict(
        w_att=[_group_qkv(w_att, g) for g in range(len(ATT_GROUPS))],
        b_att=[_group_qkv(b_att, g)[None, :] for g in range(len(ATT_GROUPS))],
        w_sg=w[:, o_sg:o_qk], b_sg=b[None, o_sg:o_qk],
        w_ml=jnp.concatenate([w_qk, _pad_heads(w[:, o_v:o_o])], axis=1),
        b_ml=jnp.concatenate([b_qk, _pad_heads(b[o_v:o_o]) + ones_col])[None, :],
        qk_scale=qk_scale[None, :],
        w_mlo=_pad_heads(w[:, o_o:o_i]), b_mlo=_pad_heads(b[o_o:o_i])[None, :],
        w_if=jnp.pad(w[:, o_i:o_g], ((0, 0), (0, LANES - 2 * ML_HEADS))),
        b_if=jnp.pad(b[o_i:o_g], (0, LANES - 2 * ML_HEADS))[None, :],
        w_g=w[:, o_g:], b_g=b[None, o_g:],
        conv_w=jnp.concatenate([_pad_heads(cw[:, :ML_WIDTH]), _pad_heads(cw[:, ML_WIDTH:])], axis=1),
        conv_b=jnp.concatenate([_pad_heads(cb[:ML_WIDTH]), _pad_heads(cb[ML_WIDTH:])])[None, :],
        sg_ln_g=sg_ln_g[layer][None, :], sg_ln_b=sg_ln_b[layer][None, :],
        sg_w=sg_w[layer], sg_bias_t=sg_b[layer].T,
        w_br_a=w_br_a[layer].astype(BF16), w_br_b=w_br_b[layer].astype(BF16),
        w_br_c=_pad_heads(w_br_c[layer].T).T.astype(BF16),
        w_out=w_out[layer].astype(BF16),
        ln_g0=ln_g[layer, 0][None, :], ln_b0=ln_b[layer, 0][None, :],
        ln_g1=ln_g[layer, 1][None, :], ln_b1=ln_b[layer, 1][None, :],
    )


def kernel(x, positions, w_in, b_in, conv_w, conv_b, sg_ln_g, sg_ln_b, sg_w, sg_b, w_br_a, w_br_b, w_br_c, w_out,
           ln_g, ln_b, ffn_w1, ffn_w3, ffn_w2, router_w, router_b, moe_w1, moe_w3, moe_w2):
    batch, seq, _ = x.shape
    n = batch * seq
    depth = w_in.shape[0]
    alpha = (2.0 * depth) ** 0.25
    cos, sin = _rope_tables(positions)
    xf = x.reshape(n, D_MODEL)
    xb = None
    for layer in range(depth):
        p = _layer_params(layer, w_in, b_in, conv_w, conv_b, sg_ln_g, sg_ln_b, sg_w, sg_b,
                          w_br_a, w_br_b, w_br_c, w_out, ln_g, ln_b)
        undilated = [g for g, (_, dil) in enumerate(ATT_GROUPS) if dil == 1]
        j = layer // 2
        if layer % 2 == 0:
            riders = [ffn_w1[j], ffn_w3[j], ffn_w2[j]]
        else:
            riders = [moe_w2[j].reshape(N_EXPERTS * D_FF, D_MODEL)]
        if xb is None:
            riders = riders + [xf]
        ml, og, gates_if, *rest = _projections(xf if xb is None else xb, [
            (p["w_ml"], p["b_ml"], BF16, None),
            (p["w_mlo"], p["b_mlo"], BF16, _sigmoid),
            (p["w_if"], p["b_if"], F32, None),
        ] + [(p["w_att"][g], p["b_att"][g], BF16, None) for g in undilated],
            riders=riders)
        qkv_plain, cast = rest[:len(undilated)], rest[len(undilated):]
        if xb is None:
            *cast, xb = cast
        att_outs = []
        for g, (_, dil) in enumerate(ATT_GROUPS):
            if dil == 1:
                qkv = qkv_plain[undilated.index(g)].reshape(batch, 1, seq, ATT_PARTS_WIDTH)
            else:
                qkv = _linear_dilated(xb, p["w_att"][g], p["b_att"][g], dil, batch, seq)
            att_outs.append(_attention_group(qkv, cos, sin, g, batch, seq))
        hc = _mlstm(ml, og, gates_if, p["conv_w"], p["conv_b"], p["qk_scale"], batch, seq)
        xf, xb = _mixer_out(xf, att_outs, hc, xb, p, alpha, seq)
        if layer % 2 == 0:
            next_moe = (layer + 1) // 2 if layer + 1 < depth else None
            riders = [] if next_moe is None else [moe_w1[next_moe].reshape(N_EXPERTS * D_MODEL, D_FF),
                                                  moe_w3[next_moe].reshape(N_EXPERTS * D_MODEL, D_FF)]
            xf, xb, *moe_cast = _ffn(xf, xb, *cast, p["ln_g1"], p["ln_b1"], alpha, riders=riders)
        else:
            if layer == 0:
                moe_cast = [moe_w1[j].astype(BF16), moe_w3[j].astype(BF16)]
            w1, w3 = (m.reshape(N_EXPERTS, D_MODEL, D_FF) for m in moe_cast)
            w2 = cast[0].reshape(N_EXPERTS, D_FF, D_MODEL)
            xf = _moe(xf, router_w[j], router_b[j], w1, w3, w2, p["ln_g1"], p["ln_b1"], alpha)
            xb = xf.astype(BF16)
    return xf.reshape(batch, seq, D_MODEL)
```
